```python
import math
import jax, jax.numpy as jnp
from jax import lax
import numpy as np

D_MODEL = 1024
BATCH = 8
SEQ = 4096
DEPTH = 4

MIX_WIDTH = D_MODEL
SGU_HEADS = 4
SGU_HEAD_DIM = 128
SGU_WIDTH = SGU_HEADS * SGU_HEAD_DIM
SGU_CHUNK = 128
SGU_W_STD = 0.05
GDN_HEADS = 4
GDN_DK = 128
GDN_DV = 128
GDN_QK_WIDTH = GDN_HEADS * GDN_DK
GDN_V_WIDTH = GDN_HEADS * GDN_DV
GDN_CHUNK = 64
CONV_WIDTH = 4
CONV_CHANNELS = 2 * GDN_QK_WIDTH + GDN_V_WIDTH
D_FF = 4 * D_MODEL
N_MOD = 6
RMS_EPS = 1e-6
LN_EPS = 1e-5
IN_SIZES = (SGU_WIDTH, SGU_WIDTH, GDN_QK_WIDTH, GDN_QK_WIDTH, GDN_V_WIDTH, GDN_V_WIDTH, GDN_HEADS, GDN_HEADS)
IN_WIDTH = 2 * SGU_WIDTH + 2 * GDN_QK_WIDTH + 2 * GDN_V_WIDTH + 2 * GDN_HEADS

kernel_name = 'hybrid_sgu_gdn_adaln_trunk'


def rmsnorm(x, g):
    xf = x.astype(jnp.float32)
    y = xf * lax.rsqrt(jnp.mean(xf * xf, axis=-1, keepdims=True) + RMS_EPS)
    return (y * g.astype(jnp.float32)).astype(x.dtype)


def layernorm(x, g, b):
    xf = x.astype(jnp.float32)
    mu = jnp.mean(xf, axis=-1, keepdims=True)
    xc = xf - mu
    y = xc * lax.rsqrt(jnp.mean(xc * xc, axis=-1, keepdims=True) + LN_EPS)
    return (y * g.astype(jnp.float32) + b.astype(jnp.float32)).astype(x.dtype)


def l2norm(x):
    xf = x.astype(jnp.float32)
    return xf * lax.rsqrt(jnp.sum(xf * xf, axis=-1, keepdims=True) + RMS_EPS)


def split_cols(p, sizes):
    out, start = [], 0
    for s in sizes:
        out.append(p[..., start:start + s])
        start += s
    return out


def causal_depthwise_conv(x, w):
    k = w.shape[0]
    return lax.conv_general_dilated(x, w[:, None, :], window_strides=(1,), padding=[(k - 1, 0)],
                                    dimension_numbers=('NWC', 'WIO', 'NWC'),
                                    feature_group_count=x.shape[-1])


def sgu_mixer(u, v, ln_g, ln_b, w_s, b_s):
    bsz, t, _ = u.shape
    n = t // SGU_CHUNK
    v = layernorm(v, ln_g, ln_b)
    vc = v.reshape(bsz, n, SGU_CHUNK, SGU_HEADS, SGU_HEAD_DIM)
    causal = jnp.tril(jnp.ones((SGU_CHUNK, SGU_CHUNK), dtype=bool))
    w = jnp.where(causal[None], w_s, 0.0)
    mixed = jnp.einsum('hts,bnshd->bnthd', w, vc) + b_s.T[None, None, :, :, None]
    return u * mixed.reshape(bsz, t, SGU_WIDTH)


def gated_delta_rule(q, k, v, g, beta):
    bsz, t, h, dk = q.shape
    dv = v.shape[-1]
    c = GDN_CHUNK
    n = t // c
    q = q.reshape(bsz, n, c, h, dk).transpose(0, 3, 1, 2, 4)
    k = k.reshape(bsz, n, c, h, dk).transpose(0, 3, 1, 2, 4)
    v = v.reshape(bsz, n, c, h, dv).transpose(0, 3, 1, 2, 4)
    g = g.reshape(bsz, n, c, h).transpose(0, 3, 1, 2)
    beta = beta.reshape(bsz, n, c, h).transpose(0, 3, 1, 2)
    g_cum = jnp.cumsum(g, axis=-1)
    idx = jnp.arange(c)
    incl = idx[:, None] >= idx[None, :]
    strict = idx[:, None] > idx[None, :]
    diff = g_cum[..., :, None] - g_cum[..., None, :]
    decay = jnp.where(incl, jnp.exp(jnp.where(incl, diff, 0.0)), 0.0)
    kk = jnp.einsum('bhntd,bhnsd->bhnts', k, k)
    m = jnp.where(strict, beta[..., :, None] * kk * decay, 0.0)
    a_mat = jnp.eye(c, dtype=jnp.float32) + m
    gamma = jnp.exp(g_cum)
    rhs = jnp.concatenate([beta[..., None] * v, (beta * gamma)[..., None] * k], axis=-1)
    sol = lax.linalg.triangular_solve(a_mat, rhs, left_side=True, lower=True, unit_diagonal=True)
    u_new = sol[..., :dv]
    w_k = sol[..., dv:]
    qk = jnp.einsum('bhntd,bhnsd->bhnts', q, k) * decay
    q_dec = q * gamma[..., None]
    k_dec = k * jnp.exp(g_cum[..., -1:] - g_cum)[..., None]
    gamma_last = gamma[..., -1]

    def step(s, inp):
        q_d, k_d, u_c, wk_c, a_c, gl = inp
        w = u_c - jnp.einsum('bhck,bhkv->bhcv', wk_c, s)
        o = jnp.einsum('bhck,bhkv->bhcv', q_d, s) + jnp.einsum('bhts,bhsv->bhtv', a_c, w)
        s = gl[..., None, None] * s + jnp.einsum('bhck,bhcv->bhkv', k_d, w)
        return s, o

    xs = tuple(jnp.moveaxis(a, 2, 0) for a in (q_dec, k_dec, u_new, w_k, qk, gamma_last))
    s0 = jnp.zeros((bsz, h, dk, dv), jnp.float32)
    _, o = lax.scan(step, s0, xs)
    return o.transpose(1, 0, 3, 2, 4).reshape(bsz, t, h, dv)


def gdn_mixer(q, k, v, z, b_raw, a_raw, conv_w, a_log, dt_bias, norm_g):
    bsz, t, _ = q.shape
    dtype = q.dtype
    qkv = jax.nn.silu(causal_depthwise_conv(jnp.concatenate([q, k, v], axis=-1), conv_w))
    q, k, v = split_cols(qkv, (GDN_QK_WIDTH, GDN_QK_WIDTH, GDN_V_WIDTH))
    q = l2norm(q.reshape(bsz, t, GDN_HEADS, GDN_DK)) * (GDN_DK ** -0.5)
    k = l2norm(k.reshape(bsz, t, GDN_HEADS, GDN_DK))
    v = v.reshape(bsz, t, GDN_HEADS, GDN_DV).astype(jnp.float32)
    beta = jax.nn.sigmoid(b_raw.astype(jnp.float32))
    g = -jnp.exp(a_log.astype(jnp.float32)) * jax.nn.softplus(a_raw.astype(jnp.float32) + dt_bias.astype(jnp.float32))
    o = gated_delta_rule(q, k, v, g, beta)
    o = rmsnorm(o, norm_g) * jax.nn.silu(z.reshape(bsz, t, GDN_HEADS, GDN_DV).astype(jnp.float32))
    return o.reshape(bsz, t, GDN_V_WIDTH).astype(dtype)


def _fwd_setup_inputs(seed: int = 0) -> dict:
    key = jax.random.key(seed)
    ks = jax.random.split(key, 20)
    d = D_MODEL

    def nrm(k, shape, std):
        return jax.random.normal(k, shape, jnp.float32) * std

    dt = jnp.exp(jax.random.uniform(ks[12], (DEPTH, GDN_HEADS), jnp.float32,
                                    minval=math.log(1e-3), maxval=math.log(1e-1)))
    return {
        'x': nrm(ks[0], (BATCH, SEQ, d), 1.0),
        'c': nrm(ks[1], (BATCH, d), 1.0),
        'w_ada': nrm(ks[2], (DEPTH, d, N_MOD * d), 0.5 * d ** -0.5),
        'b_ada': nrm(ks[3], (DEPTH, N_MOD * d), 0.01),
        'norm1_g': 1.0 + nrm(ks[4], (DEPTH, d), 0.02),
        'w_in': nrm(ks[5], (DEPTH, d, IN_WIDTH), d ** -0.5),
        'sgu_ln_g': 1.0 + nrm(ks[6], (DEPTH, SGU_WIDTH), 0.02),
        'sgu_ln_b': nrm(ks[7], (DEPTH, SGU_WIDTH), 0.02),
        'sgu_w': nrm(ks[8], (DEPTH, SGU_HEADS, SGU_CHUNK, SGU_CHUNK), SGU_W_STD),
        'sgu_b': 1.0 + nrm(ks[9], (DEPTH, SGU_HEADS, SGU_CHUNK), 0.1),
        'conv_w': nrm(ks[10], (DEPTH, CONV_WIDTH, CONV_CHANNELS), CONV_WIDTH ** -0.5),
        'a_log': jnp.log(jax.random.uniform(ks[11], (DEPTH, GDN_HEADS), jnp.float32, minval=1.0, maxval=16.0)),
        'dt_bias': dt + jnp.log(-jnp.expm1(-dt)),
        'gdn_norm_g': 1.0 + nrm(ks[13], (DEPTH, GDN_DV), 0.02),
        'w_out': nrm(ks[14], (DEPTH, MIX_WIDTH, d), MIX_WIDTH ** -0.5),
        'norm2_g': 1.0 + nrm(ks[15], (DEPTH, d), 0.02),
        'w_ff1': nrm(ks[16], (DEPTH, d, D_FF), d ** -0.5),
        'w_ff2': nrm(ks[17], (DEPTH, D_FF, d), D_FF ** -0.5),
        'final_g': 1.0 + nrm(ks[18], (d,), 0.02),
    }


def _fwd_reference(x, c, w_ada, b_ada, norm1_g, w_in, sgu_ln_g, sgu_ln_b, sgu_w, sgu_b, conv_w,
              a_log, dt_bias, gdn_norm_g, w_out, norm2_g, w_ff1, w_ff2, final_g):
    c_act = jax.nn.silu(c)
    for l in range(DEPTH):
        mod = (c_act @ w_ada[l] + b_ada[l])[:, None, :]
        sh1, sc1, g1, sh2, sc2, g2 = jnp.split(mod, N_MOD, axis=-1)
        h = rmsnorm(x, norm1_g[l]) * (1.0 + sc1) + sh1
        p = h @ w_in[l]
        u, vs, q, k, v, z, b_raw, a_raw = split_cols(p, IN_SIZES)
        y_sgu = sgu_mixer(jax.nn.gelu(u), jax.nn.gelu(vs), sgu_ln_g[l], sgu_ln_b[l], sgu_w[l], sgu_b[l])
        y_gdn = gdn_mixer(q, k, v, z, b_raw, a_raw, conv_w[l], a_log[l], dt_bias[l], gdn_norm_g[l])
        mix = jnp.concatenate([y_sgu, y_gdn], axis=-1)
        x = x + g1 * (mix @ w_out[l])
        h = rmsnorm(x, norm2_g[l]) * (1.0 + sc2) + sh2
        x = x + g2 * (jnp.square(jax.nn.relu(h @ w_ff1[l])) @ w_ff2[l])
    return rmsnorm(x, final_g)


import jax as _jax
import jax.numpy as _jnp

TWIN_FORMAT = 'train_step'
FWD_PARAMS = ['x', 'c', 'w_ada', 'b_ada', 'norm1_g', 'w_in', 'sgu_ln_g', 'sgu_ln_b', 'sgu_w', 'sgu_b', 'conv_w', 'a_log', 'dt_bias', 'gdn_norm_g', 'w_out', 'norm2_g', 'w_ff1', 'w_ff2', 'final_g']
TWIN_WEIGHTS = ['w_ada', 'b_ada', 'norm1_g', 'w_in', 'sgu_ln_g', 'sgu_ln_b', 'sgu_w', 'sgu_b', 'conv_w', 'a_log', 'dt_bias', 'gdn_norm_g', 'w_out', 'norm2_g', 'w_ff1', 'w_ff2', 'final_g']
TWIN_DIFF_INPUT = 'x'
TWIN_INPUTS = ['x', 'c', 'w_ada', 'b_ada', 'norm1_g', 'w_in', 'sgu_ln_g', 'sgu_ln_b', 'sgu_w', 'sgu_b', 'conv_w', 'a_log', 'dt_bias', 'gdn_norm_g', 'w_out', 'norm2_g', 'w_ff1', 'w_ff2', 'final_g', 'loss_target', 'm_w_ada', 'm_b_ada', 'm_norm1_g', 'm_w_in', 'm_sgu_ln_g', 'm_sgu_ln_b', 'm_sgu_w', 'm_sgu_b', 'm_conv_w', 'm_a_log', 'm_dt_bias', 'm_gdn_norm_g', 'm_w_out', 'm_norm2_g', 'm_w_ff1', 'm_w_ff2', 'm_final_g', 'v_w_ada', 'v_b_ada', 'v_norm1_g', 'v_w_in', 'v_sgu_ln_g', 'v_sgu_ln_b', 'v_sgu_w', 'v_sgu_b', 'v_conv_w', 'v_a_log', 'v_dt_bias', 'v_gdn_norm_g', 'v_w_out', 'v_norm2_g', 'v_w_ff1', 'v_w_ff2', 'v_final_g']
TWIN_OUTPUTS = ['loss', 'grad_x', 'grad_w_ada', 'grad_b_ada', 'grad_norm1_g', 'grad_w_in', 'grad_sgu_ln_g', 'grad_sgu_ln_b', 'grad_sgu_w', 'grad_sgu_b', 'grad_conv_w', 'grad_a_log', 'grad_dt_bias', 'grad_gdn_norm_g', 'grad_w_out', 'grad_norm2_g', 'grad_w_ff1', 'grad_w_ff2', 'grad_final_g', 'delta_w_ada', 'delta_b_ada', 'delta_norm1_g', 'delta_w_in', 'delta_sgu_ln_g', 'delta_sgu_ln_b', 'delta_sgu_w', 'delta_sgu_b', 'delta_conv_w', 'delta_a_log', 'delta_dt_bias', 'delta_gdn_norm_g', 'delta_w_out', 'delta_norm2_g', 'delta_w_ff1', 'delta_w_ff2', 'delta_final_g', 'new_m_w_ada', 'new_m_b_ada', 'new_m_norm1_g', 'new_m_w_in', 'new_m_sgu_ln_g', 'new_m_sgu_ln_b', 'new_m_sgu_w', 'new_m_sgu_b', 'new_m_conv_w', 'new_m_a_log', 'new_m_dt_bias', 'new_m_gdn_norm_g', 'new_m_w_out', 'new_m_norm2_g', 'new_m_w_ff1', 'new_m_w_ff2', 'new_m_final_g', 'new_v_w_ada', 'new_v_b_ada', 'new_v_norm1_g', 'new_v_w_in', 'new_v_sgu_ln_g', 'new_v_sgu_ln_b', 'new_v_sgu_w', 'new_v_sgu_b', 'new_v_conv_w', 'new_v_a_log', 'new_v_dt_bias', 'new_v_gdn_norm_g', 'new_v_w_out', 'new_v_norm2_g', 'new_v_w_ff1', 'new_v_w_ff2', 'new_v_final_g']
TWIN_LEAF_KINDS = {'loss': 'loss', 'grad_x': 'grad_x', 'grad_w_ada': 'grad_w', 'grad_b_ada': 'grad_w', 'grad_norm1_g': 'grad_w', 'grad_w_in': 'grad_w', 'grad_sgu_ln_g': 'grad_w', 'grad_sgu_ln_b': 'grad_w', 'grad_sgu_w': 'grad_w', 'grad_sgu_b': 'grad_w', 'grad_conv_w': 'grad_w', 'grad_a_log': 'grad_w', 'grad_dt_bias': 'grad_w', 'grad_gdn_norm_g': 'grad_w', 'grad_w_out': 'grad_w', 'grad_norm2_g': 'grad_w', 'grad_w_ff1': 'grad_w', 'grad_w_ff2': 'grad_w', 'grad_final_g': 'grad_w', 'delta_w_ada': 'delta_w', 'delta_b_ada': 'delta_w', 'delta_norm1_g': 'delta_w', 'delta_w_in': 'delta_w', 'delta_sgu_ln_g': 'delta_w', 'delta_sgu_ln_b': 'delta_w', 'delta_sgu_w': 'delta_w', 'delta_sgu_b': 'delta_w', 'delta_conv_w': 'delta_w', 'delta_a_log': 'delta_w', 'delta_dt_bias': 'delta_w', 'delta_gdn_norm_g': 'delta_w', 'delta_w_out': 'delta_w', 'delta_norm2_g': 'delta_w', 'delta_w_ff1': 'delta_w', 'delta_w_ff2': 'delta_w', 'delta_final_g': 'delta_w', 'new_m_w_ada': 'new_m', 'new_m_b_ada': 'new_m', 'new_m_norm1_g': 'new_m', 'new_m_w_in': 'new_m', 'new_m_sgu_ln_g': 'new_m', 'new_m_sgu_ln_b': 'new_m', 'new_m_sgu_w': 'new_m', 'new_m_sgu_b': 'new_m', 'new_m_conv_w': 'new_m', 'new_m_a_log': 'new_m', 'new_m_dt_bias': 'new_m', 'new_m_gdn_norm_g': 'new_m', 'new_m_w_out': 'new_m', 'new_m_norm2_g': 'new_m', 'new_m_w_ff1': 'new_m', 'new_m_w_ff2': 'new_m', 'new_m_final_g': 'new_m', 'new_v_w_ada': 'new_v', 'new_v_b_ada': 'new_v', 'new_v_norm1_g': 'new_v', 'new_v_w_in': 'new_v', 'new_v_sgu_ln_g': 'new_v', 'new_v_sgu_ln_b': 'new_v', 'new_v_sgu_w': 'new_v', 'new_v_sgu_b': 'new_v', 'new_v_conv_w': 'new_v', 'new_v_a_log': 'new_v', 'new_v_dt_bias': 'new_v', 'new_v_gdn_norm_g': 'new_v', 'new_v_w_out': 'new_v', 'new_v_norm2_g': 'new_v', 'new_v_w_ff1': 'new_v', 'new_v_w_ff2': 'new_v', 'new_v_final_g': 'new_v'}


def _forward(args):
    return _fwd_reference(*[args[k] for k in FWD_PARAMS])


def _output_shape():
    out = _jax.eval_shape(lambda: _forward(_fwd_setup_inputs(0)))
    return out.shape, out.dtype

N_MICROBATCH = 1
ADAM_LR = 0.001
ADAM_B1 = 0.9
ADAM_B2 = 0.999
ADAM_EPS = 1e-08
ADAM_WD = 0.01
ADAM_STEP = 10
PER_EXAMPLE_BATCH_AXIS = {'x': 0, 'c': 0, 'loss_target': 0}
SHARED_INPUTS = []
_WEIGHT_DTYPES = {'w_ada': _jnp.float32, 'b_ada': _jnp.float32, 'norm1_g': _jnp.float32, 'w_in': _jnp.float32, 'sgu_ln_g': _jnp.float32, 'sgu_ln_b': _jnp.float32, 'sgu_w': _jnp.float32, 'sgu_b': _jnp.float32, 'conv_w': _jnp.float32, 'a_log': _jnp.float32, 'dt_bias': _jnp.float32, 'gdn_norm_g': _jnp.float32, 'w_out': _jnp.float32, 'norm2_g': _jnp.float32, 'w_ff1': _jnp.float32, 'w_ff2': _jnp.float32, 'final_g': _jnp.float32}
MOMENT_SCALE = {'w_ada': 7.692459e-02, 'b_ada': 1.304411e-01, 'norm1_g': 4.747679e-02, 'w_in': 2.833671e-02, 'sgu_ln_g': 1.532561e-02, 'sgu_ln_b': 1.489887e-02, 'sgu_w': 2.544171e-02, 'sgu_b': 3.643526e-02, 'conv_w': 2.431819e-02, 'a_log': 1.446298e-01, 'dt_bias': 1.371721e-01, 'gdn_norm_g': 6.634155e-02, 'w_out': 3.658862e-02, 'norm2_g': 7.502832e-02, 'w_ff1': 3.875268e-02, 'w_ff2': 7.066949e-02, 'final_g': 3.254211e+01}


def _to_microbatches(a, axis):
    t = _jnp.moveaxis(a, axis, 0)
    t = t.reshape((N_MICROBATCH, t.shape[0] // N_MICROBATCH) + t.shape[1:])
    return _jnp.moveaxis(t, 1, axis + 1)


def setup_inputs(seed: int = 0) -> dict:
    inp = _fwd_setup_inputs(seed)
    key = _jax.random.fold_in(_jax.random.key(seed), 7919)
    shape, _ = _output_shape()
    out = dict(inp)
    out["loss_target"] = _jax.random.normal(_jax.random.fold_in(key, 0), shape, _jnp.float32)
    for i, name in enumerate(TWIN_WEIGHTS):
        w = inp[name].astype(_jnp.float32)
        if MOMENT_SCALE is None:
            s = _jnp.sqrt(_jnp.mean(_jnp.square(w)) + 1e-30)
        else:
            s = MOMENT_SCALE[name]
        km, kv = _jax.random.split(_jax.random.fold_in(key, i + 1))
        out[name] = w
        out["m_" + name] = s * _jax.random.normal(km, w.shape, _jnp.float32)
        out["v_" + name] = (s * s) * _jax.random.uniform(kv, w.shape, _jnp.float32, 0.5, 1.5)
    if N_MICROBATCH > 1:
        for name, axis in PER_EXAMPLE_BATCH_AXIS.items():
            out[name] = _to_microbatches(out[name], axis)
    return {'x': out['x'], 'c': out['c'], 'w_ada': out['w_ada'], 'b_ada': out['b_ada'], 'norm1_g': out['norm1_g'], 'w_in': out['w_in'], 'sgu_ln_g': out['sgu_ln_g'], 'sgu_ln_b': out['sgu_ln_b'], 'sgu_w': out['sgu_w'], 'sgu_b': out['sgu_b'], 'conv_w': out['conv_w'], 'a_log': out['a_log'], 'dt_bias': out['dt_bias'], 'gdn_norm_g': out['gdn_norm_g'], 'w_out': out['w_out'], 'norm2_g': out['norm2_g'], 'w_ff1': out['w_ff1'], 'w_ff2': out['w_ff2'], 'final_g': out['final_g'], 'loss_target': out['loss_target'], 'm_w_ada': out['m_w_ada'], 'm_b_ada': out['m_b_ada'], 'm_norm1_g': out['m_norm1_g'], 'm_w_in': out['m_w_in'], 'm_sgu_ln_g': out['m_sgu_ln_g'], 'm_sgu_ln_b': out['m_sgu_ln_b'], 'm_sgu_w': out['m_sgu_w'], 'm_sgu_b': out['m_sgu_b'], 'm_conv_w': out['m_conv_w'], 'm_a_log': out['m_a_log'], 'm_dt_bias': out['m_dt_bias'], 'm_gdn_norm_g': out['m_gdn_norm_g'], 'm_w_out': out['m_w_out'], 'm_norm2_g': out['m_norm2_g'], 'm_w_ff1': out['m_w_ff1'], 'm_w_ff2': out['m_w_ff2'], 'm_final_g': out['m_final_g'], 'v_w_ada': out['v_w_ada'], 'v_b_ada': out['v_b_ada'], 'v_norm1_g': out['v_norm1_g'], 'v_w_in': out['v_w_in'], 'v_sgu_ln_g': out['v_sgu_ln_g'], 'v_sgu_ln_b': out['v_sgu_ln_b'], 'v_sgu_w': out['v_sgu_w'], 'v_sgu_b': out['v_sgu_b'], 'v_conv_w': out['v_conv_w'], 'v_a_log': out['v_a_log'], 'v_dt_bias': out['v_dt_bias'], 'v_gdn_norm_g': out['v_gdn_norm_g'], 'v_w_out': out['v_w_out'], 'v_norm2_g': out['v_norm2_g'], 'v_w_ff1': out['v_w_ff1'], 'v_w_ff2': out['v_w_ff2'], 'v_final_g': out['v_final_g']}


def _loss(weights, diff, rest, loss_target):
    with _jax.named_scope("forward"):
        args = {**rest, TWIN_DIFF_INPUT: diff, **{k: w.astype(_WEIGHT_DTYPES[k]) for k, w in weights.items()}}
        y = _forward(args)
    with _jax.named_scope("loss_head"):
        err = _jnp.square(y.astype(_jnp.float32) - loss_target)
        return 0.5 * _jnp.sum(_jnp.mean(err, axis=-1)) if err.ndim else 0.5 * err


def _adamw(w, g, m, v):
    m = ADAM_B1 * m + (1.0 - ADAM_B1) * g
    v = ADAM_B2 * v + (1.0 - ADAM_B2) * _jnp.square(g)
    m_hat = m / (1.0 - ADAM_B1 ** ADAM_STEP)
    v_hat = v / (1.0 - ADAM_B2 ** ADAM_STEP)
    delta = -ADAM_LR * (m_hat / (_jnp.sqrt(v_hat) + ADAM_EPS) + ADAM_WD * w)
    return delta, m, v


def reference(x, c, w_ada, b_ada, norm1_g, w_in, sgu_ln_g, sgu_ln_b, sgu_w, sgu_b, conv_w, a_log, dt_bias, gdn_norm_g, w_out, norm2_g, w_ff1, w_ff2, final_g, loss_target, m_w_ada, m_b_ada, m_norm1_g, m_w_in, m_sgu_ln_g, m_sgu_ln_b, m_sgu_w, m_sgu_b, m_conv_w, m_a_log, m_dt_bias, m_gdn_norm_g, m_w_out, m_norm2_g, m_w_ff1, m_w_ff2, m_final_g, v_w_ada, v_b_ada, v_norm1_g, v_w_in, v_sgu_ln_g, v_sgu_ln_b, v_sgu_w, v_sgu_b, v_conv_w, v_a_log, v_dt_bias, v_gdn_norm_g, v_w_out, v_norm2_g, v_w_ff1, v_w_ff2, v_final_g):
    given = dict(x=x, c=c, w_ada=w_ada, b_ada=b_ada, norm1_g=norm1_g, w_in=w_in, sgu_ln_g=sgu_ln_g, sgu_ln_b=sgu_ln_b, sgu_w=sgu_w, sgu_b=sgu_b, conv_w=conv_w, a_log=a_log, dt_bias=dt_bias, gdn_norm_g=gdn_norm_g, w_out=w_out, norm2_g=norm2_g, w_ff1=w_ff1, w_ff2=w_ff2, final_g=final_g, loss_target=loss_target, m_w_ada=m_w_ada, m_b_ada=m_b_ada, m_norm1_g=m_norm1_g, m_w_in=m_w_in, m_sgu_ln_g=m_sgu_ln_g, m_sgu_ln_b=m_sgu_ln_b, m_sgu_w=m_sgu_w, m_sgu_b=m_sgu_b, m_conv_w=m_conv_w, m_a_log=m_a_log, m_dt_bias=m_dt_bias, m_gdn_norm_g=m_gdn_norm_g, m_w_out=m_w_out, m_norm2_g=m_norm2_g, m_w_ff1=m_w_ff1, m_w_ff2=m_w_ff2, m_final_g=m_final_g, v_w_ada=v_w_ada, v_b_ada=v_b_ada, v_norm1_g=v_norm1_g, v_w_in=v_w_in, v_sgu_ln_g=v_sgu_ln_g, v_sgu_ln_b=v_sgu_ln_b, v_sgu_w=v_sgu_w, v_sgu_b=v_sgu_b, v_conv_w=v_conv_w, v_a_log=v_a_log, v_dt_bias=v_dt_bias, v_gdn_norm_g=v_gdn_norm_g, v_w_out=v_w_out, v_norm2_g=v_norm2_g, v_w_ff1=v_w_ff1, v_w_ff2=v_w_ff2, v_final_g=v_final_g)
    weights = {n: given[n] for n in TWIN_WEIGHTS}
    shared = {n: given[n] for n in SHARED_INPUTS}
    per_example = {n: given[n] for n in ['x', 'c']}
    grad_fn = _jax.value_and_grad(_loss, argnums=(0, 1))

    def one_microbatch(ex, loss_target):
        ex = dict(ex)
        diff = ex.pop(TWIN_DIFF_INPUT)
        return grad_fn(weights, diff, {**shared, **ex}, loss_target)

    if N_MICROBATCH == 1:
        loss, (grad_w, grad_x) = one_microbatch(per_example, given["loss_target"])
    else:
        def body(carry, xs):
            loss_sum, grad_sum = carry
            l_k, (gw_k, gx_k) = one_microbatch(xs[0], xs[1])
            with _jax.named_scope("update"):
                return (loss_sum + l_k, _jax.tree.map(_jnp.add, grad_sum, gw_k)), gx_k

        init = (_jnp.zeros((), _jnp.float32), _jax.tree.map(_jnp.zeros_like, weights))
        (loss, grad_w), grad_x = _jax.lax.scan(body, init, (per_example, given["loss_target"]))
    with _jax.named_scope("update"):
        delta_w, new_m, new_v = {}, {}, {}
        for n in TWIN_WEIGHTS:
            delta_w[n], new_m[n], new_v[n] = _adamw(weights[n], grad_w[n], given["m_" + n], given["v_" + n])
    return (loss, grad_x, *[grad_w[n] for n in TWIN_WEIGHTS], *[delta_w[n] for n in TWIN_WEIGHTS],
            *[new_m[n] for n in TWIN_WEIGHTS], *[new_v[n] for n in TWIN_WEIGHTS])
```

```python
import functools
import math

import jax
import jax.numpy as jnp
from jax import lax
from jax.experimental import pallas as pl
from jax.experimental.pallas import tpu as pltpu

F32 = jnp.float32
BF16 = jnp.bfloat16

DEPTH = 4
D = 1024
HEADS = 4
HD = 128
BLK = 128
IN_W = 3080
NW = 3200
GATE0 = 3072
DFF = 4096
N_CHIPS = 4
RMS_EPS = 1e-6
LN_EPS = 1e-5
QK_SCALE = HD ** -0.5
LR, B1, B2, ADAM_EPS, WD, STEP = 0.001, 0.9, 0.999, 1e-08, 0.01, 10
VMEM_LIMIT = 56 * 1024 * 1024
MESH = pl.DeviceIdType.MESH
HOPS = ((1, 0), (0, 1), (1, 1))
HI = lax.Precision.HIGHEST


def _dot(a, b):
    return jnp.dot(a.astype(BF16), b.astype(BF16), preferred_element_type=F32)


def _dot_nt(a, b):
    return lax.dot_general(a.astype(BF16), b.astype(BF16), (((1,), (1,)), ((), ())), preferred_element_type=F32)


def _dot_tn(a, b):
    return lax.dot_general(a.astype(BF16), b.astype(BF16), (((0,), (0,)), ((), ())), preferred_element_type=F32)


def _dotf(a, b):
    return jnp.dot(a, b, precision=HI, preferred_element_type=F32)


def _split(a):
    hi = a.astype(BF16)
    return hi, (a - hi.astype(F32)).astype(BF16)


def _dg3(a, b, dims):
    ah, al = _split(a)
    bh, bl = _split(b)
    f = lambda x, y: lax.dot_general(x, y, (dims, ((), ())), preferred_element_type=F32)
    return f(ah, bh) + (f(ah, bl) + f(al, bh))


def _d3(a, b):
    return _dg3(a, b, ((1,), (0,)))


def _d3_nt(a, b):
    return _dg3(a, b, ((1,), (1,)))


def _d3_tn(a, b):
    return _dg3(a, b, ((0,), (0,)))


def _dotf_tn(a, b):
    return lax.dot_general(a, b, (((0,), (0,)), ((), ())), precision=HI, preferred_element_type=F32)


def _sigmoid(x):
    return 1.0 / (1.0 + jnp.exp(-x))


def _softplus(x):
    return jnp.maximum(x, 0.0) + jnp.log(1.0 + jnp.exp(-jnp.abs(x)))


_G0 = math.sqrt(2.0 / math.pi)
_G1 = 0.044715


def _gelu(x):
    t = jnp.tanh(_G0 * (x + _G1 * x * x * x))
    return 0.5 * x * (1.0 + t)


def _gelu_grad(x):
    t = jnp.tanh(_G0 * (x + _G1 * x * x * x))
    return 0.5 * (1.0 + t) + 0.5 * x * (1.0 - t * t) * (_G0 * (1.0 + 3.0 * _G1 * x * x))


def _silu(x):
    return x * _sigmoid(x)


def _silu_grad(x):
    s = _sigmoid(x)
    return s * (1.0 + x * (1.0 - s))


def _rms_stats(x):
    rstd = lax.rsqrt(jnp.mean(x * x, axis=-1, keepdims=True) + RMS_EPS)
    return x * rstd, rstd


def _norm_mod(x, ng, sc, sh):
    xh, _ = _rms_stats(x)
    return xh * (ng * (1.0 + sc)) + sh


def _norm_mod_bwd(dh, x, ng, sc):
    xh, rstd = _rms_stats(x)
    dsh = jnp.sum(dh, axis=0, keepdims=True)
    dsc = jnp.sum(dh * xh, axis=0, keepdims=True) * ng
    dng = jnp.sum(dh * xh, axis=0, keepdims=True) * (1.0 + sc)
    dxh = dh * (ng * (1.0 + sc))
    dx = rstd * (dxh - xh * jnp.mean(dxh * xh, axis=-1, keepdims=True))
    return dx, dsh, dsc, dng


def _iota2(shape, axis):
    return lax.broadcasted_iota(jnp.int32, shape, axis)


def _col(tile, idx):
    return jnp.sum(jnp.where(_iota2(tile.shape, 1) == idx, tile, 0.0), axis=1, keepdims=True)


def _row(tile, idx):
    return jnp.sum(jnp.where(_iota2(tile.shape, 0) == idx, tile, 0.0), axis=0, keepdims=True)


def _put_col(col, idx, width=HD):
    shape = (col.shape[0], width)
    return jnp.where(_iota2(shape, 1) == idx, jnp.broadcast_to(col, shape), 0.0)


def _tri_inverse(m):
    rows, cols = _iota2(m.shape, 0), _iota2(m.shape, 1)
    eye = jnp.where(rows == cols, 1.0, 0.0).astype(F32)
    n = jnp.where((rows >> 3) == (cols >> 3), -m, 0.0)
    p = eye + n
    n2 = _dotf(n, n)
    p = p + _dotf(n2, p)
    n4 = _dotf(n2, n2)
    p = p + _dotf(n4, p)
    for shift in (3, 4, 5, 6):
        same_pair = (rows >> (shift + 1)) == (cols >> (shift + 1))
        below = jnp.logical_and(((rows >> shift) & 1) == 1, ((cols >> shift) & 1) == 0)
        off = jnp.where(jnp.logical_and(same_pair, below), m, 0.0)
        p = p - _dotf(p, _dotf(off, p))
    return p


def _cparams(sem=None):
    return pltpu.CompilerParams(dimension_semantics=sem, vmem_limit_bytes=VMEM_LIMIT)


def _my_place():
    return lax.axis_index("x"), lax.axis_index("y"), lax.axis_index("c")


def _hop(xi, yi, hop):
    dx, dy = hop
    return (1 - xi if dx else xi), (1 - yi if dy else yi)


def _pair_exchange(x, name):
    def body(x_ref, o_ref, ssem, rsem):
        xi, yi, ci = _my_place()
        cp = pltpu.make_async_remote_copy(x_ref, o_ref, ssem, rsem, device_id=(xi, yi, 1 - ci), device_id_type=MESH)
        cp.start()
        cp.wait()

    return pl.pallas_call(
        body, name=name, out_shape=jax.ShapeDtypeStruct(x.shape, x.dtype),
        in_specs=[pl.BlockSpec(memory_space=pltpu.VMEM)], out_specs=pl.BlockSpec(memory_space=pltpu.VMEM),
        scratch_shapes=[pltpu.SemaphoreType.DMA, pltpu.SemaphoreType.DMA],
        compiler_params=pltpu.CompilerParams(vmem_limit_bytes=VMEM_LIMIT),
    )(x)


def _chip_allgather(x, name):
    def body(x_ref, o_ref, ssems, rsems, lsem):
        xi, yi, ci = _my_place()
        me = 2 * xi + yi
        loc = pltpu.make_async_copy(x_ref, o_ref.at[me], lsem)
        loc.start()
        sends = []
        for k, hop in enumerate(HOPS):
            tx, ty = _hop(xi, yi, hop)
            cp = pltpu.make_async_remote_copy(x_ref, o_ref.at[me], ssems.at[k], rsems.at[k],
                                              device_id=(tx, ty, ci), device_id_type=MESH)
            cp.start()
            sends.append(cp)
        for k, hop in enumerate(HOPS):
            tx, ty = _hop(xi, yi, hop)
            pltpu.make_async_remote_copy(x_ref, o_ref.at[2 * tx + ty], ssems.at[k], rsems.at[k],
                                         device_id=(tx, ty, ci), device_id_type=MESH).wait_recv()
        for cp in sends:
            cp.wait_send()
        loc.wait()

    return pl.pallas_call(
        body, name=name, out_shape=jax.ShapeDtypeStruct((N_CHIPS,) + x.shape, x.dtype),
        in_specs=[pl.BlockSpec(memory_space=pltpu.VMEM)], out_specs=pl.BlockSpec(memory_space=pltpu.VMEM),
        scratch_shapes=[pltpu.SemaphoreType.DMA((3,)), pltpu.SemaphoreType.DMA((3,)), pltpu.SemaphoreType.DMA],
        compiler_params=pltpu.CompilerParams(vmem_limit_bytes=VMEM_LIMIT),
    )(x)


def _hbm_specs(n):
    return [pl.BlockSpec(memory_space=pl.ANY)] * n


def _weights_allgather(shards):
    n = len(shards)

    def body(*refs):
        ins, outs = refs[:n], refs[n:2 * n]
        s_ici, r_ici, s_d2d, r_d2d, lsem = refs[2 * n:]
        xi, yi, ci = _my_place()
        me = 2 * xi + yi
        lay, sib_lay = pl.ds(2 * ci, 2), pl.ds(2 * (1 - ci), 2)
        pending = []
        for i in range(n):
            cp = pltpu.make_async_copy(ins[i], outs[i].at[me], lsem.at[i])
            cp.start()
            pending.append(cp.wait)
        for i in range(n):
            for k, hop in enumerate(HOPS):
                tx, ty = _hop(xi, yi, hop)
                cp = pltpu.make_async_remote_copy(ins[i].at[lay], outs[i].at[me, lay], s_ici.at[i, k], r_ici.at[i, k],
                                                  device_id=(tx, ty, ci), device_id_type=MESH)
                cp.start()
                pending.append(cp.wait_send)
        for k, hop in enumerate(HOPS):
            tx, ty = _hop(xi, yi, hop)
            src = 2 * tx + ty
            for i in range(n):
                pltpu.make_async_remote_copy(ins[i].at[lay], outs[i].at[src, lay], s_ici.at[i, k], r_ici.at[i, k],
                                             device_id=(tx, ty, ci), device_id_type=MESH).wait_recv()
                cp = pltpu.make_async_remote_copy(outs[i].at[src, lay], outs[i].at[src, lay], s_d2d.at[i, k],
                                                  r_d2d.at[i, k], device_id=(xi, yi, 1 - ci), device_id_type=MESH)
                cp.start()
                pending.append(cp.wait_send)
        for k, hop in enumerate(HOPS):
            tx, ty = _hop(xi, yi, hop)
            src = 2 * tx + ty
            for i in range(n):
                pltpu.make_async_remote_copy(outs[i].at[src, sib_lay], outs[i].at[src, sib_lay], s_d2d.at[i, k],
                                             r_d2d.at[i, k], device_id=(xi, yi, 1 - ci), device_id_type=MESH).wait_recv()
        for wait in pending:
            wait()

    return pl.pallas_call(
        body, name="weights_allgather",
        out_shape=[jax.ShapeDtypeStruct((N_CHIPS,) + s.shape, s.dtype) for s in shards],
        in_specs=_hbm_specs(n), out_specs=_hbm_specs(n),
        scratch_shapes=[pltpu.SemaphoreType.DMA((n, 3))] * 4 + [pltpu.SemaphoreType.DMA((n,))],
    )(*shards)


def _grads_pair_send(gs):
    n = len(gs)

    def body(*refs):
        ins, outs, ssem, rsem = refs[:n], refs[n:2 * n], refs[2 * n], refs[2 * n + 1]
        xi, yi, ci = _my_place()
        sib_lay = pl.ds(2 * (1 - ci), 2)
        cps = []
        for i in range(n):
            cp = pltpu.make_async_remote_copy(ins[i].at[sib_lay], outs[i], ssem.at[i], rsem.at[i],
                                              device_id=(xi, yi, 1 - ci), device_id_type=MESH)
            cp.start()
            cps.append(cp)
        for cp in cps:
            cp.wait()

    return pl.pallas_call(
        body, name="grads_pair_send",
        out_shape=[jax.ShapeDtypeStruct((2,) + g.shape[1:], g.dtype) for g in gs],
        in_specs=_hbm_specs(n), out_specs=_hbm_specs(n),
        scratch_shapes=[pltpu.SemaphoreType.DMA((n,)), pltpu.SemaphoreType.DMA((n,))],
    )(*gs)


def _grads_chip_exchange(ps):
    n = len(ps)

    def body(*refs):
        ins, outs = refs[:n], refs[n:2 * n]
        ssems, rsems, lsem = refs[2 * n:]
        xi, yi, ci = _my_place()
        me = 2 * xi + yi
        both = pl.ds(0, 2)
        pending = []
        for i in range(n):
            cp = pltpu.make_async_copy(ins[i].at[both, me], outs[i].at[both, me], lsem.at[i])
            cp.start()
            pending.append(cp.wait)
        for i in range(n):
            for k, hop in enumerate(HOPS):
                tx, ty = _hop(xi, yi, hop)
                cp = pltpu.make_async_remote_copy(ins[i].at[both, 2 * tx + ty], outs[i].at[both, me], ssems.at[i, k],
                                                  rsems.at[i, k], device_id=(tx, ty, ci), device_id_type=MESH)
                cp.start()
                pending.append(cp.wait_send)
        for i in range(n):
            for k, hop in enumerate(HOPS):
                tx, ty = _hop(xi, yi, hop)
                pltpu.make_async_remote_copy(ins[i].at[both, me], outs[i].at[both, 2 * tx + ty], ssems.at[i, k],
                                             rsems.at[i, k], device_id=(tx, ty, ci), device_id_type=MESH).wait_recv()
        for wait in pending:
            wait()

    return pl.pallas_call(
        body, name="grads_chip_exchange",
        out_shape=[jax.ShapeDtypeStruct(p.shape, p.dtype) for p in ps],
        in_specs=_hbm_specs(n), out_specs=_hbm_specs(n),
        scratch_shapes=[pltpu.SemaphoreType.DMA((n, 3)), pltpu.SemaphoreType.DMA((n, 3)), pltpu.SemaphoreType.DMA((n,))],
    )(*ps)


def _grads_pair_share(rs):
    n = len(rs)

    def body(*refs):
        ins, outs = refs[:n], refs[n:2 * n]
        ssem, rsem, lsem = refs[2 * n:]
        xi, yi, ci = _my_place()
        lay, sib_lay = pl.ds(2 * ci, 2), pl.ds(2 * (1 - ci), 2)
        pending = []
        for i in range(n):
            cp = pltpu.make_async_copy(ins[i], outs[i].at[lay], lsem.at[i])
            cp.start()
            pending.append(cp.wait)
            cp = pltpu.make_async_remote_copy(ins[i], outs[i].at[lay], ssem.at[i], rsem.at[i],
                                              device_id=(xi, yi, 1 - ci), device_id_type=MESH)
            cp.start()
            pending.append(cp.wait_send)
        for i in range(n):
            pltpu.make_async_remote_copy(ins[i], outs[i].at[sib_lay], ssem.at[i], rsem.at[i],
                                         device_id=(xi, yi, 1 - ci), device_id_type=MESH).wait_recv()
        for wait in pending:
            wait()

    return pl.pallas_call(
        body, name="grads_pair_share",
        out_shape=[jax.ShapeDtypeStruct((DEPTH,) + r.shape[1:], r.dtype) for r in rs],
        in_specs=_hbm_specs(n), out_specs=_hbm_specs(n),
        scratch_shapes=[pltpu.SemaphoreType.DMA((n,)), pltpu.SemaphoreType.DMA((n,)), pltpu.SemaphoreType.DMA((n,))],
    )(*rs)


def _row_tile(r):
    return min(r, 256)


def _pair_sum(g, ga, cvec, name):
    _, _, r, c = g.shape
    tr = _row_tile(r)

    def body(c_ref, g_ref, ga_ref, o_ref):
        o_ref[...] = (g_ref[...].astype(F32) + ga_ref[...].astype(F32)).astype(o_ref.dtype)

    return pl.pallas_call(
        body, name=name, out_shape=jax.ShapeDtypeStruct(ga.shape, ga.dtype),
        grid_spec=pltpu.PrefetchScalarGridSpec(
            num_scalar_prefetch=1, grid=(2, N_CHIPS, r // tr),
            in_specs=[pl.BlockSpec((None, None, tr, c), lambda i, j, k, cr: (2 * cr[0] + i, j, k, 0)),
                      pl.BlockSpec((None, None, tr, c), lambda i, j, k, cr: (i, j, k, 0))],
            out_specs=pl.BlockSpec((None, None, tr, c), lambda i, j, k, cr: (i, j, k, 0))),
        compiler_params=_cparams(("parallel", "parallel", "parallel")),
    )(cvec, g, ga)


def _chip_sum(pb, name):
    _, _, r, c = pb.shape
    tr = _row_tile(r)

    def body(p_ref, o_ref):
        acc = p_ref[0].astype(F32) + p_ref[1].astype(F32)
        acc = acc + p_ref[2].astype(F32)
        o_ref[...] = acc + p_ref[3].astype(F32)

    return pl.pallas_call(
        body, name=name, out_shape=jax.ShapeDtypeStruct((2, r, c), F32), grid=(2, r // tr),
        in_specs=[pl.BlockSpec((None, N_CHIPS, tr, c), lambda i, k: (i, 0, k, 0))],
        out_specs=pl.BlockSpec((None, tr, c), lambda i, k: (i, k, 0)),
        compiler_params=_cparams(("parallel", "parallel")),
    )(pb)


def _adam_math(w, g, m, v):
    m = B1 * m + (1.0 - B1) * g
    v = B2 * v + (1.0 - B2) * (g * g)
    m_hat = m / (1.0 - B1 ** STEP)
    v_hat = v / (1.0 - B2 ** STEP)
    delta = -LR * (m_hat / (jnp.sqrt(v_hat) + ADAM_EPS) + WD * w)
    return delta, m, v


def _adamw(w, g, m, v, name):
    n_l, r, c = w.shape
    tr = _row_tile(r)

    def body(w_ref, g_ref, m_ref, v_ref, d_ref, nm_ref, nv_ref):
        d_ref[...], nm_ref[...], nv_ref[...] = _adam_math(w_ref[...], g_ref[...], m_ref[...], v_ref[...])

    spec = pl.BlockSpec((None, tr, c), lambda i, k: (i, k, 0))
    return pl.pallas_call(
        body, name=name, out_shape=[jax.ShapeDtypeStruct(w.shape, F32)] * 3, grid=(n_l, r // tr),
        in_specs=[spec] * 4, out_specs=[spec] * 3, compiler_params=_cparams(("parallel", "parallel")),
    )(w, g, m, v)


def _ada_forward(c_all, w_ada, b_cols):
    cols = w_ada.shape[2]
    tn = 512

    def body(c_ref, w_ref, b_ref, o_ref):
        o_ref[...] = _dotf(_silu(c_ref[...]), w_ref[...]) + b_ref[...]

    return pl.pallas_call(
        body, name="ada_forward", out_shape=jax.ShapeDtypeStruct((DEPTH, 8, cols), F32), grid=(DEPTH, cols // tn),
        in_specs=[pl.BlockSpec((8, D), lambda l, j: (0, 0)),
                  pl.BlockSpec((None, D, tn), lambda l, j: (l, 0, j)),
                  pl.BlockSpec((None, 1, tn), lambda l, j: (l, 0, j))],
        out_specs=pl.BlockSpec((None, 8, tn), lambda l, j: (l, 0, j)),
        compiler_params=_cparams(("parallel", "parallel")),
    )(c_all, w_ada, b_cols.reshape(DEPTH, 1, cols))


def _ada_backward_adamw(c_all, dmod_cols, w, m, v):
    cols = w.shape[2]
    tn = 512

    def body(c_ref, d_ref, w_ref, m_ref, v_ref, g_ref, dl_ref, nm_ref, nv_ref):
        g = _dotf_tn(_silu(c_ref[...]), d_ref[...])
        g_ref[...] = g
        dl_ref[...], nm_ref[...], nv_ref[...] = _adam_math(w_ref[...], g, m_ref[...], v_ref[...])

    wspec = pl.BlockSpec((None, D, tn), lambda l, j: (l, 0, j))
    return pl.pallas_call(
        body, name="ada_backward_adamw", out_shape=[jax.ShapeDtypeStruct(w.shape, F32)] * 4, grid=(DEPTH, cols // tn),
        in_specs=[pl.BlockSpec((8, D), lambda l, j: (0, 0)), pl.BlockSpec((None, 8, tn), lambda l, j: (l, 0, j)),
                  wspec, wspec, wspec],
        out_specs=[wspec] * 4, compiler_params=_cparams(("parallel", "parallel")),
    )(c_all, dmod_cols, w, m, v)


def _tok_tile(t):
    return min(t, 256)


def _wspec4(r, c, l):
    return pl.BlockSpec((N_CHIPS, None, r, c), lambda i: (0, l, 0, 0))


def _fwd_in(x, modv, w_in, l):
    t = x.shape[0]
    tm = _tok_tile(t)

    def body(x_ref, mod_ref, w_ref, o_ref):
        h = _norm_mod(x_ref[...], mod_ref[6:7, :], mod_ref[1:2, :], mod_ref[0:1, :])
        o_ref[...] = jnp.dot(h.astype(BF16), w_ref[...], preferred_element_type=F32)

    return pl.pallas_call(
        body, name=f"fwd_in_{l}", out_shape=jax.ShapeDtypeStruct((t, NW), F32), grid=(t // tm,),
        in_specs=[pl.BlockSpec((tm, D), lambda i: (i, 0)), pl.BlockSpec((None, 8, D), lambda i: (l, 0, 0)),
                  pl.BlockSpec((None, D, NW), lambda i: (l, 0, 0))],
        out_specs=pl.BlockSpec((tm, NW), lambda i: (i, 0)), compiler_params=_cparams(("parallel",)),
    )(x, modv, w_in)


def _fwd_out(x, mix, modv, w_out, l):
    t = x.shape[0]
    tm = _tok_tile(t)

    def body(x_ref, mix_ref, mod_ref, w_ref, o_ref):
        w = w_ref[...].reshape(D, D)
        o_ref[...] = x_ref[...] + mod_ref[2:3, :] * jnp.dot(mix_ref[...], w, preferred_element_type=F32)

    return pl.pallas_call(
        body, name=f"fwd_out_{l}", out_shape=jax.ShapeDtypeStruct((t, D), F32), grid=(t // tm,),
        in_specs=[pl.BlockSpec((tm, D), lambda i: (i, 0)), pl.BlockSpec((tm, D), lambda i: (i, 0)),
                  pl.BlockSpec((None, 8, D), lambda i: (l, 0, 0)), _wspec4(D // N_CHIPS, D, l)],
        out_specs=pl.BlockSpec((tm, D), lambda i: (i, 0)), compiler_params=_cparams(("parallel",)),
    )(x, mix, modv, w_out)


def _fwd_ff1(x, modv, w_ff1, l):
    t = x.shape[0]
    tm = _tok_tile(t)

    def body(x_ref, mod_ref, w_ref, o_ref):
        h = _norm_mod(x_ref[...], mod_ref[7:8, :], mod_ref[4:5, :], mod_ref[3:4, :]).astype(BF16)
        for j in range(N_CHIPS):
            f = jnp.dot(h, w_ref[j], preferred_element_type=F32)
            o_ref[:, j * D:(j + 1) * D] = jnp.maximum(f, 0.0).astype(BF16)

    return pl.pallas_call(
        body, name=f"fwd_ff1_{l}", out_shape=jax.ShapeDtypeStruct((t, DFF), BF16), grid=(t // tm,),
        in_specs=[pl.BlockSpec((tm, D), lambda i: (i, 0)), pl.BlockSpec((None, 8, D), lambda i: (l, 0, 0)),
                  _wspec4(D, D, l)],
        out_specs=pl.BlockSpec((tm, DFF), lambda i: (i, 0)), compiler_params=_cparams(("parallel",)),
    )(x, modv, w_ff1)


def _fwd_ff2(x, r, modv, w_ff2, l):
    t = x.shape[0]
    tm = _tok_tile(t)

    def body(x_ref, r_ref, mod_ref, w_ref, o_ref):
        acc = jnp.zeros((tm, D), F32)
        for j in range(N_CHIPS):
            rj = r_ref[:, j * D:(j + 1) * D].astype(F32)
            acc = acc + jnp.dot((rj * rj).astype(BF16), w_ref[j], preferred_element_type=F32)
        o_ref[...] = x_ref[...] + mod_ref[5:6, :] * acc

    return pl.pallas_call(
        body, name=f"fwd_ff2_{l}", out_shape=jax.ShapeDtypeStruct((t, D), F32), grid=(t // tm,),
        in_specs=[pl.BlockSpec((tm, D), lambda i: (i, 0)), pl.BlockSpec((tm, DFF), lambda i: (i, 0)),
                  pl.BlockSpec((None, 8, D), lambda i: (l, 0, 0)), _wspec4(D, D, l)],
        out_specs=pl.BlockSpec((tm, D), lambda i: (i, 0)), compiler_params=_cparams(("parallel",)),
    )(x, r, modv, w_ff2)


def _loss_head(x, target, final_g):
    t = x.shape[0]
    tm = _tok_tile(t)

    def body(x_ref, t_ref, g_ref, dx_ref, st_ref):
        @pl.when(pl.program_id(0) == 0)
        def _():
            st_ref[...] = jnp.zeros_like(st_ref)

        xh, rstd = _rms_stats(x_ref[...])
        g = g_ref[...]
        err = xh * g - t_ref[...]
        loss = 0.5 * jnp.sum(jnp.mean(err * err, axis=-1, keepdims=True), axis=0, keepdims=True)
        dy = err * (1.0 / D)
        st_ref[0:1, :] += jnp.sum(dy * xh, axis=0, keepdims=True)
        st_ref[1:2, :] += jnp.broadcast_to(loss, (1, D))
        dxh = dy * g
        dx_ref[...] = rstd * (dxh - xh * jnp.mean(dxh * xh, axis=-1, keepdims=True))

    return pl.pallas_call(
        body, name="loss_head", out_shape=[jax.ShapeDtypeStruct((t, D), F32), jax.ShapeDtypeStruct((8, D), F32)],
        grid=(t // tm,),
        in_specs=[pl.BlockSpec((tm, D), lambda i: (i, 0)), pl.BlockSpec((tm, D), lambda i: (i, 0)),
                  pl.BlockSpec((1, D), lambda i: (0, 0))],
        out_specs=[pl.BlockSpec((tm, D), lambda i: (i, 0)), pl.BlockSpec((8, D), lambda i: (0, 0))],
        compiler_params=_cparams(("arbitrary",)),
    )(x, target, final_g.reshape(1, D))


def _bwd_ff2(dx2, r, modv, w_ff2, l):
    t = dx2.shape[0]
    tm = _tok_tile(t)

    def body(d_ref, r_ref, mod_ref, w_ref, o_ref):
        dyg = (d_ref[...] * mod_ref[5:6, :]).astype(BF16)
        for j in range(N_CHIPS):
            da = lax.dot_general(dyg, w_ref[j], (((1,), (1,)), ((), ())), preferred_element_type=F32)
            o_ref[:, j * D:(j + 1) * D] = (da * 2.0 * r_ref[:, j * D:(j + 1) * D].astype(F32)).astype(BF16)

    return pl.pallas_call(
        body, name=f"bwd_ff2_{l}", out_shape=jax.ShapeDtypeStruct((t, DFF), BF16), grid=(t // tm,),
        in_specs=[pl.BlockSpec((tm, D), lambda i: (i, 0)), pl.BlockSpec((tm, DFF), lambda i: (i, 0)),
                  pl.BlockSpec((None, 8, D), lambda i: (l, 0, 0)), _wspec4(D, D, l)],
        out_specs=pl.BlockSpec((tm, DFF), lambda i: (i, 0)), compiler_params=_cparams(("parallel",)),
    )(dx2, r, modv, w_ff2)


def _bwd_out(dx1, modv, w_out, l):
    t = dx1.shape[0]
    tm = _tok_tile(t)

    def body(d_ref, mod_ref, w_ref, o_ref):
        dyg = (d_ref[...] * mod_ref[2:3, :]).astype(BF16)
        w = w_ref[...].reshape(D, D)
        o_ref[...] = lax.dot_general(dyg, w, (((1,), (1,)), ((), ())), preferred_element_type=F32).astype(BF16)

    return pl.pallas_call(
        body, name=f"bwd_out_{l}", out_shape=jax.ShapeDtypeStruct((t, D), BF16), grid=(t // tm,),
        in_specs=[pl.BlockSpec((tm, D), lambda i: (i, 0)), pl.BlockSpec((None, 8, D), lambda i: (l, 0, 0)),
                  _wspec4(D // N_CHIPS, D, l)],
        out_specs=pl.BlockSpec((tm, D), lambda i: (i, 0)), compiler_params=_cparams(("parallel",)),
    )(dx1, modv, w_out)


def _bwd_norm(dy, w, x, dres, modv, l, which):
    t = x.shape[0]
    tm = _tok_tile(t)
    rows = (6, 1) if which == "in" else (7, 4)
    width = dy.shape[1]

    def body(dy_ref, w_ref, x_ref, dr_ref, mod_ref, dx_ref, st_ref):
        @pl.when(pl.program_id(0) == 0)
        def _():
            st_ref[...] = jnp.zeros_like(st_ref)

        if which == "in":
            dh = lax.dot_general(dy_ref[...], w_ref[...], (((1,), (1,)), ((), ())), preferred_element_type=F32)
        else:
            dh = jnp.zeros((tm, D), F32)
            for j in range(N_CHIPS):
                dh = dh + lax.dot_general(dy_ref[:, j * D:(j + 1) * D], w_ref[j], (((1,), (1,)), ((), ())),
                                          preferred_element_type=F32)
        ng, sc = mod_ref[rows[0]:rows[0] + 1, :], mod_ref[rows[1]:rows[1] + 1, :]
        dx, dsh, dsc, dng = _norm_mod_bwd(dh, x_ref[...], ng, sc)
        dx_ref[...] = dr_ref[...] + dx
        st_ref[0:1, :] += dsh
        st_ref[1:2, :] += dsc
        st_ref[2:3, :] += dng

    wspec = pl.BlockSpec((None, D, NW), lambda i: (l, 0, 0)) if which == "in" else _wspec4(D, D, l)
    return pl.pallas_call(
        body, name=f"bwd_norm_{which}_{l}",
        out_shape=[jax.ShapeDtypeStruct((t, D), F32), jax.ShapeDtypeStruct((8, D), F32)], grid=(t // tm,),
        in_specs=[pl.BlockSpec((tm, width), lambda i: (i, 0)), wspec, pl.BlockSpec((tm, D), lambda i: (i, 0)),
                  pl.BlockSpec((tm, D), lambda i: (i, 0)), pl.BlockSpec((None, 8, D), lambda i: (l, 0, 0))],
        out_specs=[pl.BlockSpec((tm, D), lambda i: (i, 0)), pl.BlockSpec((8, D), lambda i: (0, 0))],
        compiler_params=_cparams(("arbitrary",)),
    )(dy, w, x, dres, modv)


def _grad_weight(lhs, rhs, buf, modv, l, which, w_gate=None):
    t = lhs.shape[0]
    tm = min(t, 512)
    nt = t // tm
    gated = which in ("out", "ff2")
    if which == "in":
        nj, lw, rw, orows, ocols = 5, D, NW // 5, D, NW // 5
    elif which == "ff1":
        nj, lw, rw, orows, ocols = N_CHIPS, D, D, D, D
    elif which == "out":
        nj, lw, rw, orows, ocols = N_CHIPS, D // N_CHIPS, D, D // N_CHIPS, D
    else:
        nj, lw, rw, orows, ocols = N_CHIPS, D, D, D, D
    gate_row = 2 if which == "out" else 5

    def body(*refs):
        if gated:
            l_ref, r_ref, mod_ref, wg_ref, _, o_ref, dg_ref, acc = refs
        else:
            l_ref, r_ref, mod_ref, _, o_ref, acc = refs
        j, k = pl.program_id(0), pl.program_id(1)

        @pl.when(k == 0)
        def _():
            acc[...] = jnp.zeros_like(acc)

        if which == "in":
            lv = _norm_mod(l_ref[...], mod_ref[6:7, :], mod_ref[1:2, :], mod_ref[0:1, :])
        elif which == "ff1":
            lv = _norm_mod(l_ref[...], mod_ref[7:8, :], mod_ref[4:5, :], mod_ref[3:4, :])
        elif which == "ff2":
            lv = l_ref[...].astype(F32)
            lv = lv * lv
        else:
            lv = l_ref[...]
        acc[...] += _dot_tn(lv, r_ref[...])

        if gated:
            @pl.when(jnp.logical_and(j == 0, k == 0))
            def _():
                dg_ref[...] = jnp.zeros_like(dg_ref)

        @pl.when(k == nt - 1)
        def _():
            raw = acc[...]
            if gated:
                o_ref[...] = (raw * mod_ref[gate_row:gate_row + 1, :]).astype(o_ref.dtype)
                dg_ref[0:1, :] += jnp.sum(raw * wg_ref[...].astype(F32), axis=0, keepdims=True)
            else:
                o_ref[...] = raw.astype(o_ref.dtype)

    if which in ("in", "ff1"):
        lspec = pl.BlockSpec((tm, lw), lambda j, k: (k, 0))
        rspec = pl.BlockSpec((tm, rw), lambda j, k: (k, j))
    else:
        lspec = pl.BlockSpec((tm, lw), lambda j, k: (k, j))
        rspec = pl.BlockSpec((tm, rw), lambda j, k: (k, 0))
    mspec = pl.BlockSpec((None, 8, D), lambda j, k: (l, 0, 0))
    if which == "in":
        ospec = pl.BlockSpec((None, orows, ocols), lambda j, k: (l, 0, j))
    else:
        ospec = pl.BlockSpec((None, None, orows, ocols), lambda j, k: (l, j, 0, 0))
    in_specs = [lspec, rspec, mspec]
    args = [lhs, rhs, modv]
    out_specs = [ospec]
    out_shape = [jax.ShapeDtypeStruct(buf.shape, buf.dtype)]
    if gated:
        in_specs.append(pl.BlockSpec((None, None, orows, ocols), lambda j, k: (j, l, 0, 0)))
        args.append(w_gate)
        out_specs.append(pl.BlockSpec((8, D), lambda j, k: (0, 0)))
        out_shape.append(jax.ShapeDtypeStruct((8, D), F32))
    in_specs.append(pl.BlockSpec(memory_space=pl.ANY))
    args.append(buf)
    res = pl.pallas_call(
        body, name=f"grad_w_{which}_{l}", out_shape=out_shape, grid=(nj, nt), in_specs=in_specs, out_specs=out_specs,
        scratch_shapes=[pltpu.VMEM((orows, ocols), F32)], input_output_aliases={len(args) - 1: 0},
        compiler_params=_cparams(("arbitrary", "arbitrary")),
    )(*args)
    return (res[0], res[1]) if gated else (res[0], None)


def _tri_masks():
    rows, cols = _iota2((BLK, BLK), 0), _iota2((BLK, BLK), 1)
    return rows >= cols, rows > cols


def _sgu_forward(p_ref, lnp_ref, sguw_ref, sgub_ref):
    incl, _ = _tri_masks()
    ug = _gelu(p_ref[:, 0:512])
    vg = _gelu(p_ref[:, 512:1024])
    mu = jnp.mean(vg, axis=-1, keepdims=True)
    xc = vg - mu
    rstd = lax.rsqrt(jnp.mean(xc * xc, axis=-1, keepdims=True) + LN_EPS)
    vhat = xc * rstd
    vn = vhat * lnp_ref[0:1, :] + lnp_ref[1:2, :]
    bias = sgub_ref[...]
    ys, mixed, wms = [], [], []
    for h in range(HEADS):
        wm = jnp.where(incl, sguw_ref[h], 0.0)
        mx = _dot(wm, vn[:, h * HD:(h + 1) * HD]) + _col(bias, h)
        ys.append(ug[:, h * HD:(h + 1) * HD] * mx)
        mixed.append(mx)
        wms.append(wm)
    return ys, ug, vhat, rstd, vn, mixed, wms


def _conv_forward(xbuf, cw_ref):
    conv = cw_ref[0:1, :] * xbuf[5:5 + BLK, :]
    for j in range(1, 4):
        conv = conv + cw_ref[j:j + 1, :] * xbuf[5 + j:5 + j + BLK, :]
    return conv


def _gates(gt, gv_ref):
    incl, _ = _tri_masks()
    beta = _sigmoid(gt)
    neg_a = -jnp.exp(gv_ref[0:1, :])
    gl = neg_a * _softplus(gt + gv_ref[1:2, :])
    gc = _dotf(jnp.where(incl, 1.0, 0.0).astype(F32), gl)
    return beta, gl, gc, gc.T, neg_a


def _head_chunk(act, beta, gc, gct, h):
    incl, strict = _tri_masks()
    qh = act[:, h * HD:(h + 1) * HD]
    kh = act[:, 512 + h * HD:512 + (h + 1) * HD]
    vh = act[:, 1024 + h * HD:1024 + (h + 1) * HD]
    rq = lax.rsqrt(jnp.sum(qh * qh, axis=-1, keepdims=True) + RMS_EPS)
    rk = lax.rsqrt(jnp.sum(kh * kh, axis=-1, keepdims=True) + RMS_EPS)
    qhat, khat = qh * rq, kh * rk
    qn = qhat * QK_SCALE
    b = _col(beta, h)
    gcol = _col(gc, 4 + h)
    grow = _row(gct, 4 + h)
    dmat = jnp.where(incl, jnp.exp(jnp.where(incl, gcol - grow, 0.0)), 0.0)
    gam = jnp.exp(gcol)
    glast = _row(gcol, BLK - 1)
    e = jnp.exp(glast - gcol)
    kk = _d3_nt(khat, khat)
    return dict(qhat=qhat, khat=khat, qn=qn, vh=vh, rq=rq, rk=rk, b=b, dmat=dmat, gam=gam, glast=glast, e=e, kk=kk,
                strict=strict, incl=incl)


def _mixer_forward(p, lnp, sgu_w, sgu_bt, cw, gv, l):
    t = p.shape[0]
    nb = t // BLK

    def body(p_ref, lnp_ref, sguw_ref, sgub_ref, cw_ref, gv_ref,
             mix_ref, s_out, t_out, u_out, w_out, o_out, tail_out, s_scr, xbuf):
        @pl.when(pl.program_id(0) == 0)
        def _():
            s_scr[...] = jnp.zeros_like(s_scr)
            xbuf[0:8, :] = jnp.zeros((8, 1536), F32)

        ys = _sgu_forward(p_ref, lnp_ref, sguw_ref, sgub_ref)[0]
        for h in range(HEADS):
            mix_ref[:, h * HD:(h + 1) * HD] = ys[h].astype(BF16)

        tail_out[...] = xbuf[0:8, :]
        xbuf[8:8 + BLK, :] = p_ref[:, 1024:2560]
        act = _silu(_conv_forward(xbuf, cw_ref))
        xbuf[0:8, :] = xbuf[BLK:BLK + 8, :]
        beta, _, gc, gct, _ = _gates(p_ref[:, GATE0:NW], gv_ref)
        for h in range(HEADS):
            hc = _head_chunk(act, beta, gc, gct, h)
            m = jnp.where(hc["strict"], hc["b"] * hc["kk"] * hc["dmat"], 0.0)
            tm = _tri_inverse(m)
            u = _d3(tm, hc["b"] * hc["vh"])
            w = _d3(tm, (hc["b"] * hc["gam"]) * hc["khat"])
            qkm = _d3_nt(hc["qn"], hc["khat"]) * hc["dmat"]
            s = s_scr[h]
            wn = u - _d3(w, s)
            o = _d3(hc["qn"] * hc["gam"], s) + _d3(qkm, wn)
            s_out[h] = s
            s_scr[h] = jnp.exp(hc["glast"]) * s + _d3_tn(hc["khat"] * hc["e"], wn)
            t_out[h] = tm
            sl = slice(h * HD, (h + 1) * HD)
            u_out[:, sl] = u
            w_out[:, sl] = w
            o_out[:, sl] = o
            on = o * lax.rsqrt(jnp.mean(o * o, axis=-1, keepdims=True) + RMS_EPS) * gv_ref[2:3, :]
            mix_ref[:, 512 + h * HD:512 + (h + 1) * HD] = (on * _silu(p_ref[:, 2560 + h * HD:2560 + (h + 1) * HD])).astype(BF16)

    tok = lambda w: pl.BlockSpec((BLK, w), lambda i: (i, 0))
    st = pl.BlockSpec((None, HEADS, HD, HD), lambda i: (i, 0, 0, 0))
    return pl.pallas_call(
        body, name=f"mixer_fwd_{l}", grid=(nb,),
        out_shape=[jax.ShapeDtypeStruct((t, D), BF16), jax.ShapeDtypeStruct((nb, HEADS, HD, HD), F32),
                   jax.ShapeDtypeStruct((nb, HEADS, HD, HD), F32), jax.ShapeDtypeStruct((t, 512), F32),
                   jax.ShapeDtypeStruct((t, 512), F32), jax.ShapeDtypeStruct((t, 512), F32),
                   jax.ShapeDtypeStruct((nb, 8, 1536), F32)],
        in_specs=[tok(NW), pl.BlockSpec((None, 8, 512), lambda i: (l, 0, 0)),
                  pl.BlockSpec((None, HEADS, HD, HD), lambda i: (l, 0, 0, 0)),
                  pl.BlockSpec((None, HD, HD), lambda i: (l, 0, 0)), pl.BlockSpec((None, 8, 1536), lambda i: (l, 0, 0)),
                  pl.BlockSpec((None, 8, HD), lambda i: (l, 0, 0))],
        out_specs=[tok(D), st, st, tok(512), tok(512), tok(512), pl.BlockSpec((None, 8, 1536), lambda i: (i, 0, 0))],
        scratch_shapes=[pltpu.VMEM((HEADS, HD, HD), F32), pltpu.VMEM((BLK + 8, 1536), F32)],
        compiler_params=_cparams(("arbitrary",)),
    )(p, lnp, sgu_w, sgu_bt, cw, gv)


def _mixer_backward(p, dmix, saved, lnp, sgu_w, sgu_bt, cw, gv, l):
    t = p.shape[0]
    nb = t // BLK
    s_sv, t_sv, u_sv, w_sv, o_sv, tail_sv = saved

    def body(p_ref, dmix_ref, s_ref, t_ref, u_ref, w_ref, o_ref, tail_ref, lnp_ref, sguw_ref, sgub_ref, cw_ref, gv_ref,
             dp_ref, dlnp_ref, dsguw_ref, dsgub_ref, dcw_ref, dgv_ref, ds_scr, xbuf, dcbuf):
        @pl.when(pl.program_id(0) == 0)
        def _():
            ds_scr[...] = jnp.zeros_like(ds_scr)
            dcbuf[BLK:BLK + 8, :] = jnp.zeros((8, 1536), F32)
            dlnp_ref[...] = jnp.zeros_like(dlnp_ref)
            dsguw_ref[...] = jnp.zeros_like(dsguw_ref)
            dsgub_ref[...] = jnp.zeros_like(dsgub_ref)
            dcw_ref[...] = jnp.zeros_like(dcw_ref)
            dgv_ref[...] = jnp.zeros_like(dgv_ref)

        incl, strict = _tri_masks()
        _, ug, vhat, rstd, vn, mixed, wms = _sgu_forward(p_ref, lnp_ref, sguw_ref, sgub_ref)
        dvn_parts, dug_parts = [], []
        dbias = jnp.zeros((BLK, HD), F32)
        for h in range(HEADS):
            sl = slice(h * HD, (h + 1) * HD)
            dy = dmix_ref[:, sl].astype(F32)
            dmx = dy * ug[:, sl]
            dug_parts.append(dy * mixed[h])
            dsguw_ref[h] += jnp.where(incl, _dot_nt(dmx, vn[:, sl]), 0.0)
            dbias = dbias + _put_col(jnp.sum(dmx, axis=1, keepdims=True), h)
            dvn_parts.append(_dot_tn(wms[h], dmx))
        dsgub_ref[...] += dbias
        dvn = jnp.concatenate(dvn_parts, axis=1)
        dug = jnp.concatenate(dug_parts, axis=1)
        dlnp_ref[0:1, :] += jnp.sum(dvn * vhat, axis=0, keepdims=True)
        dlnp_ref[1:2, :] += jnp.sum(dvn, axis=0, keepdims=True)
        dvhat = dvn * lnp_ref[0:1, :]
        dvg = rstd * (dvhat - jnp.mean(dvhat, axis=-1, keepdims=True)
                      - vhat * jnp.mean(dvhat * vhat, axis=-1, keepdims=True))
        dp_ref[:, 0:512] = (dug * _gelu_grad(p_ref[:, 0:512])).astype(BF16)
        dp_ref[:, 512:1024] = (dvg * _gelu_grad(p_ref[:, 512:1024])).astype(BF16)

        xbuf[0:8, :] = tail_ref[...]
        xbuf[8:8 + BLK, :] = p_ref[:, 1024:2560]
        conv = _conv_forward(xbuf, cw_ref)
        act = _silu(conv)
        gt = p_ref[:, GATE0:NW]
        beta, gl, gc, gct, neg_a = _gates(gt, gv_ref)
        gng = gv_ref[2:3, :]
        dbeta_t = jnp.zeros((BLK, HD), F32)
        dgc_t = jnp.zeros((BLK, HD), F32)
        dgng = jnp.zeros((1, HD), F32)
        for h in range(HEADS):
            sl = slice(h * HD, (h + 1) * HD)
            hc = _head_chunk(act, beta, gc, gct, h)
            b, gam, e, dmat, kk = hc["b"], hc["gam"], hc["e"], hc["dmat"], hc["kk"]
            qn, khat, vh = hc["qn"], hc["khat"], hc["vh"]
            gamlast = jnp.exp(hc["glast"])
            s, tm, u, w, o = s_ref[h], t_ref[h], u_ref[:, sl], w_ref[:, sl], o_ref[:, sl]
            ds_next = ds_scr[h]
            z = p_ref[:, 2560 + h * HD:2560 + (h + 1) * HD]
            dy = dmix_ref[:, 512 + h * HD:512 + (h + 1) * HD].astype(F32)
            ro = lax.rsqrt(jnp.mean(o * o, axis=-1, keepdims=True) + RMS_EPS)
            ohat = o * ro
            dp_ref[:, 2560 + h * HD:2560 + (h + 1) * HD] = (dy * ohat * gng * _silu_grad(z)).astype(BF16)
            don = dy * _silu(z)
            dgng = dgng + jnp.sum(don * ohat, axis=0, keepdims=True)
            dohat = don * gng
            do = ro * (dohat - ohat * jnp.mean(dohat * ohat, axis=-1, keepdims=True))
            qk_raw = _d3_nt(qn, khat)
            qkm = qk_raw * dmat
            qd, kd = qn * gam, khat * e
            wn = u - _d3(w, s)
            dwn = _d3_tn(qkm, do) + _d3(kd, ds_next)
            dqd = _d3_nt(do, s)
            dqkm = jnp.where(incl, _d3_nt(do, wn), 0.0)
            ds_scr[h] = _d3_tn(qd, do) + gamlast * ds_next - _d3_tn(w, dwn)
            dgamlast = jnp.sum(jnp.sum(ds_next * s, axis=1, keepdims=True), axis=0, keepdims=True)
            dkd = _d3_nt(wn, ds_next)
            dw = -_d3_nt(dwn, s)
            db1 = _d3_tn(tm, dwn)
            db2 = _d3_tn(tm, dw)
            dm = jnp.where(strict, -(_d3_nt(db1, u) + _d3_nt(db2, w)), 0.0)
            dbeta = (jnp.sum(dm * kk * dmat, axis=1, keepdims=True) + jnp.sum(db1 * vh, axis=1, keepdims=True)
                     + gam * jnp.sum(db2 * khat, axis=1, keepdims=True))
            dkkm = dm * b * dmat
            ddm = dm * b * kk + dqkm * qk_raw
            dgam = b * jnp.sum(db2 * khat, axis=1, keepdims=True) + jnp.sum(dqd * qn, axis=1, keepdims=True)
            g_qk = dqkm * dmat
            dqn = _d3(g_qk, khat) + dqd * gam
            dkhat = ((b * gam) * db2 + _d3_tn(g_qk, qn) + _d3(dkkm, khat) + _d3_tn(dkkm, khat) + dkd * e)
            dvh = b * db1
            rkd = jnp.sum(dkd * kd, axis=1, keepdims=True)
            emat = ddm * dmat
            dgc = (dgam * gam - rkd + jnp.sum(emat, axis=1, keepdims=True)
                   - jnp.sum(emat.T, axis=1, keepdims=True))
            last = _iota2((BLK, 1), 0) == BLK - 1
            dgc = dgc + jnp.where(last, jnp.sum(rkd, axis=0, keepdims=True) + dgamlast * gamlast, 0.0)
            dgc_t = dgc_t + _put_col(dgc, 4 + h)
            dbeta_t = dbeta_t + _put_col(dbeta, h)
            dqhat = dqn * QK_SCALE
            dq = hc["rq"] * (dqhat - hc["qhat"] * jnp.sum(dqhat * hc["qhat"], axis=-1, keepdims=True))
            dk = hc["rk"] * (dkhat - khat * jnp.sum(dkhat * khat, axis=-1, keepdims=True))
            dcbuf[0:BLK, h * HD:(h + 1) * HD] = dq
            dcbuf[0:BLK, 512 + h * HD:512 + (h + 1) * HD] = dk
            dcbuf[0:BLK, 1024 + h * HD:1024 + (h + 1) * HD] = dvh
        dgv_ref[2:3, :] += dgng
        dgl = _dotf_tn(jnp.where(incl, 1.0, 0.0).astype(F32), dgc_t)
        sig_a = _sigmoid(gt + gv_ref[1:2, :])
        d_araw = dgl * neg_a * sig_a
        dgv_ref[0:1, :] += jnp.sum(dgl * gl, axis=0, keepdims=True)
        dgv_ref[1:2, :] += jnp.sum(d_araw, axis=0, keepdims=True)
        dp_ref[:, GATE0:NW] = (dbeta_t * beta * (1.0 - beta) + d_araw).astype(BF16)
        dcbuf[0:BLK, :] = dcbuf[0:BLK, :] * _silu_grad(conv)
        dqkv = cw_ref[0:1, :] * dcbuf[3:3 + BLK, :]
        dcw_ref[0:1, :] += jnp.sum(dcbuf[0:BLK, :] * xbuf[5:5 + BLK, :], axis=0, keepdims=True)
        for j in range(1, 4):
            dqkv = dqkv + cw_ref[j:j + 1, :] * dcbuf[3 - j:3 - j + BLK, :]
            dcw_ref[j:j + 1, :] += jnp.sum(dcbuf[0:BLK, :] * xbuf[5 + j:5 + j + BLK, :], axis=0, keepdims=True)
        dp_ref[:, 1024:2560] = dqkv.astype(BF16)
        dcbuf[BLK:BLK + 8, :] = dcbuf[0:8, :]

    rev = lambda w: pl.BlockSpec((BLK, w), lambda i: (nb - 1 - i, 0))
    st = pl.BlockSpec((None, HEADS, HD, HD), lambda i: (nb - 1 - i, 0, 0, 0))
    fix = lambda *shape: pl.BlockSpec((None,) + shape, lambda i: (l,) + (0,) * len(shape))
    acc = lambda *shape: pl.BlockSpec(shape, lambda i: (0,) * len(shape))
    return pl.pallas_call(
        body, name=f"mixer_bwd_{l}", grid=(nb,),
        out_shape=[jax.ShapeDtypeStruct((t, NW), BF16), jax.ShapeDtypeStruct((8, 512), F32),
                   jax.ShapeDtypeStruct((HEADS, HD, HD), F32), jax.ShapeDtypeStruct((HD, HD), F32),
                   jax.ShapeDtypeStruct((8, 1536), F32), jax.ShapeDtypeStruct((8, HD), F32)],
        in_specs=[rev(NW), rev(D), st, st, rev(512), rev(512), rev(512),
                  pl.BlockSpec((None, 8, 1536), lambda i: (nb - 1 - i, 0, 0)),
                  fix(8, 512), fix(HEADS, HD, HD), fix(HD, HD), fix(8, 1536), fix(8, HD)],
        out_specs=[rev(NW), acc(8, 512), acc(HEADS, HD, HD), acc(HD, HD), acc(8, 1536), acc(8, HD)],
        scratch_shapes=[pltpu.VMEM((HEADS, HD, HD), F32), pltpu.VMEM((BLK + 8, 1536), F32),
                        pltpu.VMEM((BLK + 8, 1536), F32)],
        compiler_params=_cparams(("arbitrary",)),
    )(p, dmix, s_sv, t_sv, u_sv, w_sv, o_sv, tail_sv, lnp, sgu_w, sgu_bt, cw, gv)


_SMALL = (("b_ada", 24), ("norm1_g", 4), ("norm2_g", 4), ("final_g", 1), ("sgu_ln_g", 2), ("sgu_ln_b", 2),
          ("sgu_w", 256), ("sgu_b", 2), ("conv_w", 24), ("a_log", 1), ("dt_bias", 1), ("gdn_norm_g", 1))
_SMALL_ROWS = sum(n for _, n in _SMALL)
_SMALL_PAD = 328
_DMOD_ROWS = 24


def _pack_rows(parts):
    rows = []
    for (name, n), a in zip(_SMALL, parts):
        flat = a.reshape(-1).astype(F32)
        rows.append(jnp.pad(flat, (0, n * D - flat.shape[0])).reshape(n, D))
    rows.append(jnp.zeros((_SMALL_PAD - _SMALL_ROWS, D), F32))
    return jnp.concatenate(rows, axis=0)


def _unpack_rows(buf, shapes):
    out, r0 = {}, 0
    for name, n in _SMALL:
        size = math.prod(shapes[name])
        out[name] = buf[r0:r0 + n].reshape(-1)[:size].reshape(shapes[name])
        r0 += n
    return out


def _pair_combine(own, sib):
    n = own.shape[0] - _DMOD_ROWS

    def body(a_ref, b_ref, o_ref):
        first = lax.axis_index("c") == 0
        a, b = a_ref[0:_DMOD_ROWS, :], b_ref[0:_DMOD_ROWS, :]
        o_ref[0:_DMOD_ROWS, :] = jnp.where(first, a, b)
        o_ref[_DMOD_ROWS:2 * _DMOD_ROWS, :] = jnp.where(first, b, a)
        o_ref[2 * _DMOD_ROWS:, :] = a_ref[_DMOD_ROWS:, :] + b_ref[_DMOD_ROWS:, :]

    return pl.pallas_call(
        body, name="small_pair_combine", out_shape=jax.ShapeDtypeStruct((2 * _DMOD_ROWS + n, D), F32),
        compiler_params=pltpu.CompilerParams(vmem_limit_bytes=VMEM_LIMIT),
    )(own, sib)


def _small_finalize(gathered, w, m, v):
    def body(g_ref, w_ref, m_ref, v_ref, go_ref, d_ref, nm_ref, nv_ref):
        sm = g_ref[0, 2 * _DMOD_ROWS:, :] + g_ref[1, 2 * _DMOD_ROWS:, :]
        sm = sm + g_ref[2, 2 * _DMOD_ROWS:, :]
        sm = sm + g_ref[3, 2 * _DMOD_ROWS:, :]
        bsum = jnp.zeros((_DMOD_ROWS, D), F32)
        for j in range(N_CHIPS):
            bsum = bsum + g_ref[j, 0:_DMOD_ROWS, :]
            bsum = bsum + g_ref[j, _DMOD_ROWS:2 * _DMOD_ROWS, :]
        go_ref[0:_DMOD_ROWS, :] = bsum
        go_ref[_DMOD_ROWS:, :] = sm[_DMOD_ROWS:, :]
        d_ref[...], nm_ref[...], nv_ref[...] = _adam_math(w_ref[...], go_ref[...], m_ref[...], v_ref[...])

    return pl.pallas_call(
        body, name="small_finalize", out_shape=[jax.ShapeDtypeStruct(w.shape, F32)] * 4,
        compiler_params=pltpu.CompilerParams(vmem_limit_bytes=VMEM_LIMIT),
    )(gathered, w, m, v)


def kernel(x, c, w_ada, b_ada, norm1_g, w_in, sgu_ln_g, sgu_ln_b, sgu_w, sgu_b, conv_w, a_log, dt_bias, gdn_norm_g, w_out, norm2_g, w_ff1, w_ff2, final_g, loss_target, m_w_ada, m_b_ada, m_norm1_g, m_w_in, m_sgu_ln_g, m_sgu_ln_b, m_sgu_w, m_sgu_b, m_conv_w, m_a_log, m_dt_bias, m_gdn_norm_g, m_w_out, m_norm2_g, m_w_ff1, m_w_ff2, m_final_g, v_w_ada, v_b_ada, v_norm1_g, v_w_in, v_sgu_ln_g, v_sgu_ln_b, v_sgu_w, v_sgu_b, v_conv_w, v_a_log, v_dt_bias, v_gdn_norm_g, v_w_out, v_norm2_g, v_w_ff1, v_w_ff2, v_final_g):
    xi, yi, ci = lax.axis_index("x"), lax.axis_index("y"), lax.axis_index("c")
    chip = 2 * xi + yi
    dev = 2 * chip + ci
    t = x.shape[1]
    x0 = x.reshape(t, D)
    target = loss_target.reshape(t, D)

    c_sib = _pair_exchange(c, "c_pair")
    c_pair = jnp.where(ci == 0, jnp.concatenate([c, c_sib], 0), jnp.concatenate([c_sib, c], 0))
    c_all = _chip_allgather(c_pair, "c_chips").reshape(8, D)
    ada_cols = w_ada.shape[2]
    b_cols = lax.dynamic_slice_in_dim(b_ada, chip * ada_cols, ada_cols, axis=1)
    mod_part = _ada_forward(c_all, w_ada, b_cols)
    conv_cols = conv_w.shape[2]
    packed = jnp.concatenate([mod_part.reshape(DEPTH * 8, ada_cols), conv_w.reshape(DEPTH, 4 * conv_cols)], axis=0)
    packed = _chip_allgather(packed, "mod_chips")
    mod_all = packed[:, :DEPTH * 8].reshape(N_CHIPS, DEPTH, 8, ada_cols)
    mod_mine = lax.dynamic_index_in_dim(mod_all, dev, axis=2, keepdims=False)
    mod = mod_mine.transpose(1, 0, 2).reshape(DEPTH, 6, D)
    modv = jnp.concatenate([mod, norm1_g[:, None, :], norm2_g[:, None, :]], axis=1)
    conv_full = packed[:, DEPTH * 8:].reshape(N_CHIPS, DEPTH, 4, conv_cols).transpose(1, 2, 0, 3).reshape(DEPTH, 4, 1536)

    g_in, g_out, g_ff1, g_ff2 = _weights_allgather(
        [w_in.astype(BF16), w_out.astype(BF16), w_ff1.astype(BF16), w_ff2.astype(BF16)])
    win = g_in.transpose(1, 2, 0, 3).reshape(DEPTH, D, IN_W)
    win = jnp.pad(win, ((0, 0), (0, 0), (0, NW - IN_W)))

    lnp = jnp.pad(jnp.stack([sgu_ln_g, sgu_ln_b], axis=1), ((0, 0), (0, 6), (0, 0)))
    sgu_bt = jnp.pad(sgu_b.transpose(0, 2, 1), ((0, 0), (0, 0), (0, HD - HEADS)))
    cw = jnp.pad(conv_full, ((0, 0), (0, 4), (0, 0)))
    lane_pad = lambda a: jnp.pad(a, ((0, 0), (4, HD - 8)))
    gv = jnp.pad(jnp.stack([lane_pad(a_log), lane_pad(dt_bias), gdn_norm_g], axis=1), ((0, 0), (0, 5), (0, 0)))

    acts = []
    xl = x0
    for l in range(DEPTH):
        p = _fwd_in(xl, modv, win, l)
        mix, *saved = _mixer_forward(p, lnp, sgu_w, sgu_bt, cw, gv, l)
        x1 = _fwd_out(xl, mix, modv, g_out, l)
        r = _fwd_ff1(x1, modv, g_ff1, l)
        x2 = _fwd_ff2(x1, r, modv, g_ff2, l)
        acts.append((xl, p, mix, saved, x1, r))
        xl = x2

    dx, head_stats = _loss_head(xl, target, final_g)
    loss = lax.psum(jnp.sum(head_stats[1, 0:1]), ("x", "y", "c"))
    d_final_g = head_stats[0]
    gb_in = jnp.zeros((DEPTH, D, NW), BF16)
    gb_out = jnp.zeros((DEPTH, N_CHIPS, D // N_CHIPS, D), BF16)
    gb_ff1 = jnp.zeros((DEPTH, N_CHIPS, D, D), BF16)
    gb_ff2 = jnp.zeros((DEPTH, N_CHIPS, D, D), BF16)
    dmod, small = [None] * DEPTH, [None] * DEPTH
    for l in reversed(range(DEPTH)):
        xl, p, mix, saved, x1, r = acts[l]
        df = _bwd_ff2(dx, r, modv, g_ff2, l)
        gb_ff2, dg2 = _grad_weight(r, dx, gb_ff2, modv, l, "ff2", g_ff2)
        gb_ff1, _ = _grad_weight(x1, df, gb_ff1, modv, l, "ff1")
        dx1, st2 = _bwd_norm(df, g_ff1, x1, dx, modv, l, "ff1")
        dmix = _bwd_out(dx1, modv, g_out, l)
        gb_out, dg1 = _grad_weight(mix, dx1, gb_out, modv, l, "out", g_out)
        dp, dlnp, dsguw, dsgub, dcw, dgv = _mixer_backward(p, dmix, saved, lnp, sgu_w, sgu_bt, cw, gv, l)
        gb_in, _ = _grad_weight(xl, dp, gb_in, modv, l, "in")
        dx, st1 = _bwd_norm(dp, win, xl, dx1, modv, l, "in")
        dmod[l] = jnp.stack([st1[0], st1[1], dg1[0], st2[0], st2[1], dg2[0]], axis=0)
        small[l] = dict(norm1_g=st1[2], norm2_g=st2[2], sgu_ln_g=dlnp[0], sgu_ln_b=dlnp[1], sgu_w=dsguw,
                        sgu_b=dsgub[:, :HEADS].T, conv_w=dcw[:4], a_log=dgv[0, 4:8], dt_bias=dgv[1, 4:8],
                        gdn_norm_g=dgv[2])
    grad_x = dx.reshape(1, t, D)

    stack = lambda k: jnp.stack([small[l][k] for l in range(DEPTH)], axis=0)
    small_grads = [jnp.zeros((DEPTH, 6 * D), F32), stack("norm1_g"), stack("norm2_g"), d_final_g, stack("sgu_ln_g"),
                   stack("sgu_ln_b"), stack("sgu_w"), stack("sgu_b"), stack("conv_w"), stack("a_log"),
                   stack("dt_bias"), stack("gdn_norm_g")]
    own = jnp.concatenate([jnp.stack(dmod, axis=0).reshape(_DMOD_ROWS, D), _pack_rows(small_grads)], axis=0)
    sib = _pair_exchange(own, "small_pair")
    gathered = _chip_allgather(_pair_combine(own, sib), "small_chips")
    small_shapes = dict(b_ada=b_ada.shape, norm1_g=norm1_g.shape, norm2_g=norm2_g.shape, final_g=final_g.shape,
                        sgu_ln_g=sgu_ln_g.shape, sgu_ln_b=sgu_ln_b.shape, sgu_w=sgu_w.shape, sgu_b=sgu_b.shape,
                        conv_w=(DEPTH, 4, 1536), a_log=a_log.shape, dt_bias=dt_bias.shape,
                        gdn_norm_g=gdn_norm_g.shape)

    def full_conv(a):
        return lax.dynamic_update_slice_in_dim(jnp.zeros((DEPTH, 4, 1536), F32), a, chip * conv_cols, axis=2)

    def pack_state(b_, n1, n2, fg, lg, lb, sw, sb, cv, al, db, gn):
        return _pack_rows([b_, n1, n2, fg, lg, lb, sw, sb, full_conv(cv), al, db, gn])

    w_small = pack_state(b_ada, norm1_g, norm2_g, final_g, sgu_ln_g, sgu_ln_b, sgu_w, sgu_b, conv_w, a_log, dt_bias,
                         gdn_norm_g)
    m_small = pack_state(m_b_ada, m_norm1_g, m_norm2_g, m_final_g, m_sgu_ln_g, m_sgu_ln_b, m_sgu_w, m_sgu_b, m_conv_w,
                         m_a_log, m_dt_bias, m_gdn_norm_g)
    v_small = pack_state(v_b_ada, v_norm1_g, v_norm2_g, v_final_g, v_sgu_ln_g, v_sgu_ln_b, v_sgu_w, v_sgu_b, v_conv_w,
                         v_a_log, v_dt_bias, v_gdn_norm_g)
    small_out = _small_finalize(gathered, w_small, m_small, v_small)
    sg, sd, sm, sv = [_unpack_rows(a, small_shapes) for a in small_out]
    for dct in (sg, sd, sm, sv):
        dct["conv_w"] = lax.dynamic_slice_in_dim(dct["conv_w"], chip * conv_cols, conv_cols, axis=2)

    dmod_all = gathered[:, :2 * _DMOD_ROWS].reshape(8, DEPTH, 6 * D)
    dmod_cols = lax.dynamic_slice_in_dim(dmod_all, chip * ada_cols, ada_cols, axis=2).transpose(1, 0, 2)
    g_ada, d_ada, nm_ada, nv_ada = _ada_backward_adamw(c_all, dmod_cols, w_ada, m_w_ada, v_w_ada)

    gb_in_c = gb_in[:, :, :IN_W].reshape(DEPTH, D, N_CHIPS, IN_W // N_CHIPS).transpose(0, 2, 1, 3)
    partials = [gb_in_c, gb_out, gb_ff1, gb_ff2]
    from_sib = _grads_pair_send(partials)
    cvec = jnp.reshape(ci, (1,)).astype(jnp.int32)
    names = ("in", "out", "ff1", "ff2")
    pair = [_pair_sum(g, ga, cvec, f"pair_sum_{n}") for g, ga, n in zip(partials, from_sib, names)]
    slots = _grads_chip_exchange(pair)
    mine = [_chip_sum(pb, f"chip_sum_{n}") for pb, n in zip(slots, names)]
    grads = _grads_pair_share(mine)
    big = {}
    for n, g, (w, m, v) in zip(names, grads, ((w_in, m_w_in, v_w_in), (w_out, m_w_out, v_w_out),
                                              (w_ff1, m_w_ff1, v_w_ff1), (w_ff2, m_w_ff2, v_w_ff2))):
        big[n] = (g,) + tuple(_adamw(w, g, m, v, f"adamw_{n}"))

    def outs(k):
        s = (sg, sd, sm, sv)[k]
        return [(g_ada, d_ada, nm_ada, nv_ada)[k], s["b_ada"], s["norm1_g"], big["in"][k], s["sgu_ln_g"],
                s["sgu_ln_b"], s["sgu_w"], s["sgu_b"], s["conv_w"], s["a_log"], s["dt_bias"], s["gdn_norm_g"],
                big["out"][k], s["norm2_g"], big["ff1"][k], big["ff2"][k], s["final_g"]]

    return (loss, grad_x, *outs(0), *outs(1), *outs(2), *outs(3))
```

```python
import functools
import math

import jax
import jax.numpy as jnp
from jax import lax
from jax.experimental import pallas as pl
from jax.experimental.pallas import tpu as pltpu

F32 = jnp.float32
BF16 = jnp.bfloat16

DEPTH = 4
D = 1024
HEADS = 4
HD = 128
BLK = 128
IN_W = 3080
NW = 3200
GATE0 = 3072
DFF = 4096
N_CHIPS = 4
RMS_EPS = 1e-6
LN_EPS = 1e-5
QK_SCALE = HD ** -0.5
LR, B1, B2, ADAM_EPS, WD, STEP = 0.001, 0.9, 0.999, 1e-08, 0.01, 10
VMEM_LIMIT = 56 * 1024 * 1024
MESH = pl.DeviceIdType.MESH
HOPS = ((1, 0), (0, 1), (1, 1))
HI = lax.Precision.HIGHEST


def _dot(a, b):
    return jnp.dot(a.astype(BF16), b.astype(BF16), preferred_element_type=F32)


def _dot_nt(a, b):
    return lax.dot_general(a.astype(BF16), b.astype(BF16), (((1,), (1,)), ((), ())), preferred_element_type=F32)


def _dot_tn(a, b):
    return lax.dot_general(a.astype(BF16), b.astype(BF16), (((0,), (0,)), ((), ())), preferred_element_type=F32)


def _dotf(a, b):
    return jnp.dot(a, b, precision=HI, preferred_element_type=F32)


def _split(a):
    hi = a.astype(BF16)
    return hi, (a - hi.astype(F32)).astype(BF16)


def _dg3(a, b, dims):
    ah, al = _split(a)
    bh, bl = _split(b)
    f = lambda x, y: lax.dot_general(x, y, (dims, ((), ())), preferred_element_type=F32)
    return f(ah, bh) + (f(ah, bl) + f(al, bh))


def _d3(a, b):
    return _dg3(a, b, ((1,), (0,)))


def _d3_nt(a, b):
    return _dg3(a, b, ((1,), (1,)))


def _d3_tn(a, b):
    return _dg3(a, b, ((0,), (0,)))


def _dotf_tn(a, b):
    return lax.dot_general(a, b, (((0,), (0,)), ((), ())), precision=HI, preferred_element_type=F32)


def _sigmoid(x):
    return 1.0 / (1.0 + jnp.exp(-x))


def _softplus(x):
    return jnp.maximum(x, 0.0) + jnp.log(1.0 + jnp.exp(-jnp.abs(x)))


_G0 = math.sqrt(2.0 / math.pi)
_G1 = 0.044715


def _gelu(x):
    t = jnp.tanh(_G0 * (x + _G1 * x * x * x))
    return 0.5 * x * (1.0 + t)


def _gelu_grad(x):
    t = jnp.tanh(_G0 * (x + _G1 * x * x * x))
    return 0.5 * (1.0 + t) + 0.5 * x * (1.0 - t * t) * (_G0 * (1.0 + 3.0 * _G1 * x * x))


def _silu(x):
    return x * _sigmoid(x)


def _silu_grad(x):
    s = _sigmoid(x)
    return s * (1.0 + x * (1.0 - s))


def _rms_stats(x):
    rstd = lax.rsqrt(jnp.mean(x * x, axis=-1, keepdims=True) + RMS_EPS)
    return x * rstd, rstd


def _norm_mod(x, ng, sc, sh):
    xh, _ = _rms_stats(x)
    return xh * (ng * (1.0 + sc)) + sh


def _norm_mod_bwd(dh, x, ng, sc):
    xh, rstd = _rms_stats(x)
    dsh = jnp.sum(dh, axis=0, keepdims=True)
    dsc = jnp.sum(dh * xh, axis=0, keepdims=True) * ng
    dng = jnp.sum(dh * xh, axis=0, keepdims=True) * (1.0 + sc)
    dxh = dh * (ng * (1.0 + sc))
    dx = rstd * (dxh - xh * jnp.mean(dxh * xh, axis=-1, keepdims=True))
    return dx, dsh, dsc, dng


def _iota2(shape, axis):
    return lax.broadcasted_iota(jnp.int32, shape, axis)


def _col(tile, idx):
    return jnp.sum(jnp.where(_iota2(tile.shape, 1) == idx, tile, 0.0), axis=1, keepdims=True)


def _row(tile, idx):
    return jnp.sum(jnp.where(_iota2(tile.shape, 0) == idx, tile, 0.0), axis=0, keepdims=True)


def _put_col(col, idx, width=HD):
    shape = (col.shape[0], width)
    return jnp.where(_iota2(shape, 1) == idx, jnp.broadcast_to(col, shape), 0.0)


def _tri_inverse(m):
    rows, cols = _iota2(m.shape, 0), _iota2(m.shape, 1)
    eye = jnp.where(rows == cols, 1.0, 0.0).astype(F32)
    n = jnp.where((rows >> 3) == (cols >> 3), -m, 0.0)
    p = eye + n
    n2 = _dotf(n, n)
    p = p + _dotf(n2, p)
    n4 = _dotf(n2, n2)
    p = p + _dotf(n4, p)
    for shift in (3, 4, 5, 6):
        same_pair = (rows >> (shift + 1)) == (cols >> (shift + 1))
        below = jnp.logical_and(((rows >> shift) & 1) == 1, ((cols >> shift) & 1) == 0)
        off = jnp.where(jnp.logical_and(same_pair, below), m, 0.0)
        p = p - _dotf(p, _dotf(off, p))
    return p


def _cparams(sem=None):
    return pltpu.CompilerParams(dimension_semantics=sem, vmem_limit_bytes=VMEM_LIMIT)


def _my_place():
    return lax.axis_index("x"), lax.axis_index("y"), lax.axis_index("c")


def _hop(xi, yi, hop):
    dx, dy = hop
    return (1 - xi if dx else xi), (1 - yi if dy else yi)


def _pair_exchange(x, name):
    def body(x_ref, o_ref, ssem, rsem):
        xi, yi, ci = _my_place()
        cp = pltpu.make_async_remote_copy(x_ref, o_ref, ssem, rsem, device_id=(xi, yi, 1 - ci), device_id_type=MESH)
        cp.start()
        cp.wait()

    return pl.pallas_call(
        body, name=name, out_shape=jax.ShapeDtypeStruct(x.shape, x.dtype),
        in_specs=[pl.BlockSpec(memory_space=pltpu.VMEM)], out_specs=pl.BlockSpec(memory_space=pltpu.VMEM),
        scratch_shapes=[pltpu.SemaphoreType.DMA, pltpu.SemaphoreType.DMA],
        compiler_params=pltpu.CompilerParams(vmem_limit_bytes=VMEM_LIMIT),
    )(x)


def _chip_allgather(x, name):
    def body(x_ref, o_ref, ssems, rsems, lsem):
        xi, yi, ci = _my_place()
        me = 2 * xi + yi
        loc = pltpu.make_async_copy(x_ref, o_ref.at[me], lsem)
        loc.start()
        sends = []
        for k, hop in enumerate(HOPS):
            tx, ty = _hop(xi, yi, hop)
            cp = pltpu.make_async_remote_copy(x_ref, o_ref.at[me], ssems.at[k], rsems.at[k],
                                              device_id=(tx, ty, ci), device_id_type=MESH)
            cp.start()
            sends.append(cp)
        for k, hop in enumerate(HOPS):
            tx, ty = _hop(xi, yi, hop)
            pltpu.make_async_remote_copy(x_ref, o_ref.at[2 * tx + ty], ssems.at[k], rsems.at[k],
                                         device_id=(tx, ty, ci), device_id_type=MESH).wait_recv()
        for cp in sends:
            cp.wait_send()
        loc.wait()

    return pl.pallas_call(
        body, name=name, out_shape=jax.ShapeDtypeStruct((N_CHIPS,) + x.shape, x.dtype),
        in_specs=[pl.BlockSpec(memory_space=pltpu.VMEM)], out_specs=pl.BlockSpec(memory_space=pltpu.VMEM),
        scratch_shapes=[pltpu.SemaphoreType.DMA((3,)), pltpu.SemaphoreType.DMA((3,)), pltpu.SemaphoreType.DMA],
        compiler_params=pltpu.CompilerParams(vmem_limit_bytes=VMEM_LIMIT),
    )(x)


def _hbm_specs(n):
    return [pl.BlockSpec(memory_space=pl.ANY)] * n


def _cast_into_slot(w, place):
    n_l, r, c = w.shape
    tr = _row_tile(r)

    def body(p_ref, w_ref, o_ref):
        o_ref[...] = w_ref[...].astype(BF16)

    return pl.pallas_call(
        body, name=f"cast_slot_{r}x{c}", out_shape=jax.ShapeDtypeStruct((N_CHIPS,) + w.shape, BF16),
        grid_spec=pltpu.PrefetchScalarGridSpec(
            num_scalar_prefetch=1, grid=(n_l, r // tr),
            in_specs=[pl.BlockSpec((None, tr, c), lambda l, k, pr: (l, k, 0))],
            out_specs=pl.BlockSpec((None, None, tr, c), lambda l, k, pr: (pr[0], l, k, 0))),
        compiler_params=_cparams(("parallel", "parallel")),
    )(place, w)


def _weights_allgather(bufs):
    n = len(bufs)

    def body(*refs):
        outs = refs[n:2 * n]
        s_ici, r_ici, s_d2d, r_d2d = refs[2 * n:]
        xi, yi, ci = _my_place()
        me = 2 * xi + yi
        lay, sib_lay = pl.ds(2 * ci, 2), pl.ds(2 * (1 - ci), 2)
        pending = []
        for i in range(n):
            for k, hop in enumerate(HOPS):
                tx, ty = _hop(xi, yi, hop)
                cp = pltpu.make_async_remote_copy(outs[i].at[me, lay], outs[i].at[me, lay], s_ici.at[i, k],
                                                  r_ici.at[i, k], device_id=(tx, ty, ci), device_id_type=MESH)
                cp.start()
                pending.append(cp.wait_send)
        for k, hop in enumerate(HOPS):
            tx, ty = _hop(xi, yi, hop)
            src = 2 * tx + ty
            for i in range(n):
                pltpu.make_async_remote_copy(outs[i].at[src, lay], outs[i].at[src, lay], s_ici.at[i, k], r_ici.at[i, k],
                                             device_id=(tx, ty, ci), device_id_type=MESH).wait_recv()
                cp = pltpu.make_async_remote_copy(outs[i].at[src, lay], outs[i].at[src, lay], s_d2d.at[i, k],
                                                  r_d2d.at[i, k], device_id=(xi, yi, 1 - ci), device_id_type=MESH)
                cp.start()
                pending.append(cp.wait_send)
        for k, hop in enumerate(HOPS):
            tx, ty = _hop(xi, yi, hop)
            src = 2 * tx + ty
            for i in range(n):
                pltpu.make_async_remote_copy(outs[i].at[src, sib_lay], outs[i].at[src, sib_lay], s_d2d.at[i, k],
                                             r_d2d.at[i, k], device_id=(xi, yi, 1 - ci), device_id_type=MESH).wait_recv()
        for wait in pending:
            wait()

    return pl.pallas_call(
        body, name="weights_allgather",
        out_shape=[jax.ShapeDtypeStruct(b.shape, b.dtype) for b in bufs],
        in_specs=_hbm_specs(n), out_specs=_hbm_specs(n), input_output_aliases={i: i for i in range(n)},
        scratch_shapes=[pltpu.SemaphoreType.DMA((n, 3))] * 4,
    )(*bufs)


def _grads_pair_send(gs):
    n = len(gs)

    def body(*refs):
        ins, outs, ssem, rsem = refs[:n], refs[n:2 * n], refs[2 * n], refs[2 * n + 1]
        xi, yi, ci = _my_place()
        sib_lay = pl.ds(2 * (1 - ci), 2)
        cps = []
        for i in range(n):
            cp = pltpu.make_async_remote_copy(ins[i].at[sib_lay], outs[i], ssem.at[i], rsem.at[i],
                                              device_id=(xi, yi, 1 - ci), device_id_type=MESH)
            cp.start()
            cps.append(cp)
        for cp in cps:
            cp.wait()

    return pl.pallas_call(
        body, name="grads_pair_send",
        out_shape=[jax.ShapeDtypeStruct((2,) + g.shape[1:], g.dtype) for g in gs],
        in_specs=_hbm_specs(n), out_specs=_hbm_specs(n),
        scratch_shapes=[pltpu.SemaphoreType.DMA((n,)), pltpu.SemaphoreType.DMA((n,))],
    )(*gs)


def _grads_chip_exchange(ps):
    n = len(ps)

    def body(*refs):
        ins, outs = refs[:n], refs[n:2 * n]
        ssems, rsems = refs[2 * n:]
        xi, yi, ci = _my_place()
        both = pl.ds(0, 2)
        sends = []
        for i in range(n):
            for k, hop in enumerate(HOPS):
                tx, ty = _hop(xi, yi, hop)
                cp = pltpu.make_async_remote_copy(ins[i].at[both, 2 * tx + ty], outs[i].at[k], ssems.at[i, k],
                                                  rsems.at[i, k], device_id=(tx, ty, ci), device_id_type=MESH)
                cp.start()
                sends.append(cp)
        for cp in sends:
            cp.wait()

    return pl.pallas_call(
        body, name="grads_chip_exchange",
        out_shape=[jax.ShapeDtypeStruct((3, 2) + p.shape[2:], p.dtype) for p in ps],
        in_specs=_hbm_specs(n), out_specs=_hbm_specs(n),
        scratch_shapes=[pltpu.SemaphoreType.DMA((n, 3)), pltpu.SemaphoreType.DMA((n, 3))],
    )(*ps)


def _grads_pair_share(gs):
    n = len(gs)

    def body(*refs):
        outs = refs[n:2 * n]
        ssem, rsem = refs[2 * n:]
        xi, yi, ci = _my_place()
        lay, sib_lay = pl.ds(2 * ci, 2), pl.ds(2 * (1 - ci), 2)
        sends = []
        for i in range(n):
            cp = pltpu.make_async_remote_copy(outs[i].at[lay], outs[i].at[lay], ssem.at[i], rsem.at[i],
                                              device_id=(xi, yi, 1 - ci), device_id_type=MESH)
            cp.start()
            sends.append(cp)
        for i in range(n):
            pltpu.make_async_remote_copy(outs[i].at[sib_lay], outs[i].at[sib_lay], ssem.at[i], rsem.at[i],
                                         device_id=(xi, yi, 1 - ci), device_id_type=MESH).wait_recv()
        for cp in sends:
            cp.wait_send()

    return pl.pallas_call(
        body, name="grads_pair_share",
        out_shape=[jax.ShapeDtypeStruct(g.shape, g.dtype) for g in gs],
        in_specs=_hbm_specs(n), out_specs=_hbm_specs(n), input_output_aliases={i: i for i in range(n)},
        scratch_shapes=[pltpu.SemaphoreType.DMA((n,)), pltpu.SemaphoreType.DMA((n,))],
    )(*gs)


def _row_tile(r):
    return min(r, 256)


def _pair_sum(g, ga, place, name):
    _, _, r, c = g.shape
    tr = _row_tile(r)

    def body(p_ref, g_ref, ga_ref, o_ref):
        o_ref[...] = (g_ref[...].astype(F32) + ga_ref[...].astype(F32)).astype(o_ref.dtype)

    return pl.pallas_call(
        body, name=name, out_shape=jax.ShapeDtypeStruct(ga.shape, ga.dtype),
        grid_spec=pltpu.PrefetchScalarGridSpec(
            num_scalar_prefetch=1, grid=(2, N_CHIPS, r // tr),
            in_specs=[pl.BlockSpec((None, None, tr, c), lambda i, j, k, pr: (2 * pr[1] + i, j, k, 0)),
                      pl.BlockSpec((None, None, tr, c), lambda i, j, k, pr: (i, j, k, 0))],
            out_specs=pl.BlockSpec((None, None, tr, c), lambda i, j, k, pr: (i, j, k, 0))),
        compiler_params=_cparams(("parallel", "parallel", "parallel")),
    )(place, g, ga)


def _chip_sum(pair, recv, place, name):
    _, _, r, c = pair.shape
    tr = _row_tile(r)

    def body(p_ref, own_ref, r_ref, o_ref):
        acc = own_ref[...].astype(F32) + r_ref[0].astype(F32)
        acc = acc + r_ref[1].astype(F32)
        o_ref[...] = acc + r_ref[2].astype(F32)

    return pl.pallas_call(
        body, name=name, out_shape=jax.ShapeDtypeStruct((DEPTH, r, c), F32),
        grid_spec=pltpu.PrefetchScalarGridSpec(
            num_scalar_prefetch=1, grid=(2, r // tr),
            in_specs=[pl.BlockSpec((None, None, tr, c), lambda i, k, pr: (i, pr[0], k, 0)),
                      pl.BlockSpec((3, None, tr, c), lambda i, k, pr: (0, i, k, 0))],
            out_specs=pl.BlockSpec((None, tr, c), lambda i, k, pr: (2 * pr[1] + i, k, 0))),
        compiler_params=_cparams(("parallel", "parallel")),
    )(place, pair, recv)


def _adam_math(w, g, m, v):
    m = B1 * m + (1.0 - B1) * g
    v = B2 * v + (1.0 - B2) * (g * g)
    m_hat = m / (1.0 - B1 ** STEP)
    v_hat = v / (1.0 - B2 ** STEP)
    delta = -LR * (m_hat / (jnp.sqrt(v_hat) + ADAM_EPS) + WD * w)
    return delta, m, v


def _adamw(w, g, m, v, name):
    n_l, r, c = w.shape
    tr = _row_tile(r)

    def body(w_ref, g_ref, m_ref, v_ref, d_ref, nm_ref, nv_ref):
        d_ref[...], nm_ref[...], nv_ref[...] = _adam_math(w_ref[...], g_ref[...], m_ref[...], v_ref[...])

    spec = pl.BlockSpec((None, tr, c), lambda i, k: (i, k, 0))
    return pl.pallas_call(
        body, name=name, out_shape=[jax.ShapeDtypeStruct(w.shape, F32)] * 3, grid=(n_l, r // tr),
        in_specs=[spec] * 4, out_specs=[spec] * 3, compiler_params=_cparams(("parallel", "parallel")),
    )(w, g, m, v)


def _ada_forward(c_all, w_ada, b_cols):
    cols = w_ada.shape[2]
    tn = 512

    def body(c_ref, w_ref, b_ref, o_ref):
        o_ref[...] = _dotf(_silu(c_ref[...]), w_ref[...]) + b_ref[...]

    return pl.pallas_call(
        body, name="ada_forward", out_shape=jax.ShapeDtypeStruct((DEPTH, 8, cols), F32), grid=(DEPTH, cols // tn),
        in_specs=[pl.BlockSpec((8, D), lambda l, j: (0, 0)),
                  pl.BlockSpec((None, D, tn), lambda l, j: (l, 0, j)),
                  pl.BlockSpec((None, 1, tn), lambda l, j: (l, 0, j))],
        out_specs=pl.BlockSpec((None, 8, tn), lambda l, j: (l, 0, j)),
        compiler_params=_cparams(("parallel", "parallel")),
    )(c_all, w_ada, b_cols.reshape(DEPTH, 1, cols))


def _ada_backward_adamw(c_all, dmod_cols, w, m, v):
    cols = w.shape[2]
    tn = 512

    def body(c_ref, d_ref, w_ref, m_ref, v_ref, g_ref, dl_ref, nm_ref, nv_ref):
        g = _dotf_tn(_silu(c_ref[...]), d_ref[...])
        g_ref[...] = g
        dl_ref[...], nm_ref[...], nv_ref[...] = _adam_math(w_ref[...], g, m_ref[...], v_ref[...])

    wspec = pl.BlockSpec((None, D, tn), lambda l, j: (l, 0, j))
    return pl.pallas_call(
        body, name="ada_backward_adamw", out_shape=[jax.ShapeDtypeStruct(w.shape, F32)] * 4, grid=(DEPTH, cols // tn),
        in_specs=[pl.BlockSpec((8, D), lambda l, j: (0, 0)), pl.BlockSpec((None, 8, tn), lambda l, j: (l, 0, j)),
                  wspec, wspec, wspec],
        out_specs=[wspec] * 4, compiler_params=_cparams(("parallel", "parallel")),
    )(c_all, dmod_cols, w, m, v)


def _tok_tile(t):
    return min(t, 256)


def _wspec4(r, c, l):
    return pl.BlockSpec((N_CHIPS, None, r, c), lambda i: (0, l, 0, 0))


def _fwd_in(x, modv, w_in, l):
    t = x.shape[0]
    tm = _tok_tile(t)

    def body(x_ref, mod_ref, w_ref, o_ref):
        h = _norm_mod(x_ref[...], mod_ref[6:7, :], mod_ref[1:2, :], mod_ref[0:1, :])
        o_ref[...] = jnp.dot(h.astype(BF16), w_ref[...], preferred_element_type=F32)

    return pl.pallas_call(
        body, name=f"fwd_in_{l}", out_shape=jax.ShapeDtypeStruct((t, NW), F32), grid=(t // tm,),
        in_specs=[pl.BlockSpec((tm, D), lambda i: (i, 0)), pl.BlockSpec((None, 8, D), lambda i: (l, 0, 0)),
                  pl.BlockSpec((None, D, NW), lambda i: (l, 0, 0))],
        out_specs=pl.BlockSpec((tm, NW), lambda i: (i, 0)), compiler_params=_cparams(("parallel",)),
    )(x, modv, w_in)


def _fwd_out(x, mix, modv, w_out, l):
    t = x.shape[0]
    tm = _tok_tile(t)

    def body(x_ref, mix_ref, mod_ref, w_ref, o_ref):
        w = w_ref[...].reshape(D, D)
        o_ref[...] = x_ref[...] + mod_ref[2:3, :] * jnp.dot(mix_ref[...], w, preferred_element_type=F32)

    return pl.pallas_call(
        body, name=f"fwd_out_{l}", out_shape=jax.ShapeDtypeStruct((t, D), F32), grid=(t // tm,),
        in_specs=[pl.BlockSpec((tm, D), lambda i: (i, 0)), pl.BlockSpec((tm, D), lambda i: (i, 0)),
                  pl.BlockSpec((None, 8, D), lambda i: (l, 0, 0)), _wspec4(D // N_CHIPS, D, l)],
        out_specs=pl.BlockSpec((tm, D), lambda i: (i, 0)), compiler_params=_cparams(("parallel",)),
    )(x, mix, modv, w_out)


def _fwd_ff1(x, modv, w_ff1, l):
    t = x.shape[0]
    tm = _tok_tile(t)

    def body(x_ref, mod_ref, w_ref, o_ref):
        h = _norm_mod(x_ref[...], mod_ref[7:8, :], mod_ref[4:5, :], mod_ref[3:4, :]).astype(BF16)
        for j in range(N_CHIPS):
            f = jnp.dot(h, w_ref[j], preferred_element_type=F32)
            o_ref[:, j * D:(j + 1) * D] = jnp.maximum(f, 0.0).astype(BF16)

    return pl.pallas_call(
        body, name=f"fwd_ff1_{l}", out_shape=jax.ShapeDtypeStruct((t, DFF), BF16), grid=(t // tm,),
        in_specs=[pl.BlockSpec((tm, D), lambda i: (i, 0)), pl.BlockSpec((None, 8, D), lambda i: (l, 0, 0)),
                  _wspec4(D, D, l)],
        out_specs=pl.BlockSpec((tm, DFF), lambda i: (i, 0)), compiler_params=_cparams(("parallel",)),
    )(x, modv, w_ff1)


def _fwd_ff2(x, r, modv, w_ff2, l):
    t = x.shape[0]
    tm = _tok_tile(t)

    def body(x_ref, r_ref, mod_ref, w_ref, o_ref):
        acc = jnp.zeros((tm, D), F32)
        for j in range(N_CHIPS):
            rj = r_ref[:, j * D:(j + 1) * D].astype(F32)
            acc = acc + jnp.dot((rj * rj).astype(BF16), w_ref[j], preferred_element_type=F32)
        o_ref[...] = x_ref[...] + mod_ref[5:6, :] * acc

    return pl.pallas_call(
        body, name=f"fwd_ff2_{l}", out_shape=jax.ShapeDtypeStruct((t, D), F32), grid=(t // tm,),
        in_specs=[pl.BlockSpec((tm, D), lambda i: (i, 0)), pl.BlockSpec((tm, DFF), lambda i: (i, 0)),
                  pl.BlockSpec((None, 8, D), lambda i: (l, 0, 0)), _wspec4(D, D, l)],
        out_specs=pl.BlockSpec((tm, D), lambda i: (i, 0)), compiler_params=_cparams(("parallel",)),
    )(x, r, modv, w_ff2)


def _loss_head(x, target, final_g):
    t = x.shape[0]
    tm = _tok_tile(t)

    def body(x_ref, t_ref, g_ref, dx_ref, st_ref):
        @pl.when(pl.program_id(0) == 0)
        def _():
            st_ref[...] = jnp.zeros_like(st_ref)

        xh, rstd = _rms_stats(x_ref[...])
        g = g_ref[...]
        err = xh * g - t_ref[...]
        loss = 0.5 * jnp.sum(jnp.mean(err * err, axis=-1, keepdims=True), axis=0, keepdims=True)
        dy = err * (1.0 / D)
        st_ref[0:1, :] += jnp.sum(dy * xh, axis=0, keepdims=True)
        st_ref[1:2, :] += jnp.broadcast_to(loss, (1, D))
        dxh = dy * g
        dx_ref[...] = rstd * (dxh - xh * jnp.mean(dxh * xh, axis=-1, keepdims=True))

    return pl.pallas_call(
        body, name="loss_head", out_shape=[jax.ShapeDtypeStruct((t, D), F32), jax.ShapeDtypeStruct((8, D), F32)],
        grid=(t // tm,),
        in_specs=[pl.BlockSpec((tm, D), lambda i: (i, 0)), pl.BlockSpec((tm, D), lambda i: (i, 0)),
                  pl.BlockSpec((1, D), lambda i: (0, 0))],
        out_specs=[pl.BlockSpec((tm, D), lambda i: (i, 0)), pl.BlockSpec((8, D), lambda i: (0, 0))],
        compiler_params=_cparams(("arbitrary",)),
    )(x, target, final_g.reshape(1, D))


def _bwd_ff2(dx2, r, modv, w_ff2, l):
    t = dx2.shape[0]
    tm = _tok_tile(t)

    def body(d_ref, r_ref, mod_ref, w_ref, o_ref):
        dyg = (d_ref[...] * mod_ref[5:6, :]).astype(BF16)
        for j in range(N_CHIPS):
            da = lax.dot_general(dyg, w_ref[j], (((1,), (1,)), ((), ())), preferred_element_type=F32)
            o_ref[:, j * D:(j + 1) * D] = (da * 2.0 * r_ref[:, j * D:(j + 1) * D].astype(F32)).astype(BF16)

    return pl.pallas_call(
        body, name=f"bwd_ff2_{l}", out_shape=jax.ShapeDtypeStruct((t, DFF), BF16), grid=(t // tm,),
        in_specs=[pl.BlockSpec((tm, D), lambda i: (i, 0)), pl.BlockSpec((tm, DFF), lambda i: (i, 0)),
                  pl.BlockSpec((None, 8, D), lambda i: (l, 0, 0)), _wspec4(D, D, l)],
        out_specs=pl.BlockSpec((tm, DFF), lambda i: (i, 0)), compiler_params=_cparams(("parallel",)),
    )(dx2, r, modv, w_ff2)


def _bwd_out(dx1, modv, w_out, l):
    t = dx1.shape[0]
    tm = _tok_tile(t)

    def body(d_ref, mod_ref, w_ref, o_ref):
        dyg = (d_ref[...] * mod_ref[2:3, :]).astype(BF16)
        w = w_ref[...].reshape(D, D)
        o_ref[...] = lax.dot_general(dyg, w, (((1,), (1,)), ((), ())), preferred_element_type=F32).astype(BF16)

    return pl.pallas_call(
        body, name=f"bwd_out_{l}", out_shape=jax.ShapeDtypeStruct((t, D), BF16), grid=(t // tm,),
        in_specs=[pl.BlockSpec((tm, D), lambda i: (i, 0)), pl.BlockSpec((None, 8, D), lambda i: (l, 0, 0)),
                  _wspec4(D // N_CHIPS, D, l)],
        out_specs=pl.BlockSpec((tm, D), lambda i: (i, 0)), compiler_params=_cparams(("parallel",)),
    )(dx1, modv, w_out)


def _bwd_norm(dy, w, x, dres, modv, l, which):
    t = x.shape[0]
    tm = _tok_tile(t)
    rows = (6, 1) if which == "in" else (7, 4)
    width = dy.shape[1]

    def body(dy_ref, w_ref, x_ref, dr_ref, mod_ref, dx_ref, st_ref):
        @pl.when(pl.program_id(0) == 0)
        def _():
            st_ref[...] = jnp.zeros_like(st_ref)

        if which == "in":
            dh = lax.dot_general(dy_ref[...], w_ref[...], (((1,), (1,)), ((), ())), preferred_element_type=F32)
        else:
            dh = jnp.zeros((tm, D), F32)
            for j in range(N_CHIPS):
                dh = dh + lax.dot_general(dy_ref[:, j * D:(j + 1) * D], w_ref[j], (((1,), (1,)), ((), ())),
                                          preferred_element_type=F32)
        ng, sc = mod_ref[rows[0]:rows[0] + 1, :], mod_ref[rows[1]:rows[1] + 1, :]
        dx, dsh, dsc, dng = _norm_mod_bwd(dh, x_ref[...], ng, sc)
        dx_ref[...] = dr_ref[...] + dx
        st_ref[0:1, :] += dsh
        st_ref[1:2, :] += dsc
        st_ref[2:3, :] += dng

    wspec = pl.BlockSpec((None, D, NW), lambda i: (l, 0, 0)) if which == "in" else _wspec4(D, D, l)
    return pl.pallas_call(
        body, name=f"bwd_norm_{which}_{l}",
        out_shape=[jax.ShapeDtypeStruct((t, D), F32), jax.ShapeDtypeStruct((8, D), F32)], grid=(t // tm,),
        in_specs=[pl.BlockSpec((tm, width), lambda i: (i, 0)), wspec, pl.BlockSpec((tm, D), lambda i: (i, 0)),
                  pl.BlockSpec((tm, D), lambda i: (i, 0)), pl.BlockSpec((None, 8, D), lambda i: (l, 0, 0))],
        out_specs=[pl.BlockSpec((tm, D), lambda i: (i, 0)), pl.BlockSpec((8, D), lambda i: (0, 0))],
        compiler_params=_cparams(("arbitrary",)),
    )(dy, w, x, dres, modv)


def _grad_weight(lhs, rhs, buf, modv, l, which, w_gate=None):
    t = lhs.shape[0]
    tm = min(t, 512)
    nt = t // tm
    gated = which in ("out", "ff2")
    if which == "in":
        nj, lw, rw, orows, ocols = 5, D, NW // 5, D, NW // 5
    elif which == "ff1":
        nj, lw, rw, orows, ocols = N_CHIPS, D, D, D, D
    elif which == "out":
        nj, lw, rw, orows, ocols = N_CHIPS, D // N_CHIPS, D, D // N_CHIPS, D
    else:
        nj, lw, rw, orows, ocols = N_CHIPS, D, D, D, D
    gate_row = 2 if which == "out" else 5

    def body(*refs):
        if gated:
            l_ref, r_ref, mod_ref, wg_ref, _, o_ref, dg_ref, acc = refs
        else:
            l_ref, r_ref, mod_ref, _, o_ref, acc = refs
        j, k = pl.program_id(0), pl.program_id(1)

        @pl.when(k == 0)
        def _():
            acc[...] = jnp.zeros_like(acc)

        if which == "in":
            lv = _norm_mod(l_ref[...], mod_ref[6:7, :], mod_ref[1:2, :], mod_ref[0:1, :])
        elif which == "ff1":
            lv = _norm_mod(l_ref[...], mod_ref[7:8, :], mod_ref[4:5, :], mod_ref[3:4, :])
        elif which == "ff2":
            lv = l_ref[...].astype(F32)
            lv = lv * lv
        else:
            lv = l_ref[...]
        acc[...] += _dot_tn(lv, r_ref[...])

        if gated:
            @pl.when(jnp.logical_and(j == 0, k == 0))
            def _():
                dg_ref[...] = jnp.zeros_like(dg_ref)

        @pl.when(k == nt - 1)
        def _():
            raw = acc[...]
            if gated:
                o_ref[...] = (raw * mod_ref[gate_row:gate_row + 1, :]).astype(o_ref.dtype)
                dg_ref[0:1, :] += jnp.sum(raw * wg_ref[...].astype(F32), axis=0, keepdims=True)
            else:
                o_ref[...] = raw.astype(o_ref.dtype)

    if which in ("in", "ff1"):
        lspec = pl.BlockSpec((tm, lw), lambda j, k: (k, 0))
        rspec = pl.BlockSpec((tm, rw), lambda j, k: (k, j))
    else:
        lspec = pl.BlockSpec((tm, lw), lambda j, k: (k, j))
        rspec = pl.BlockSpec((tm, rw), lambda j, k: (k, 0))
    mspec = pl.BlockSpec((None, 8, D), lambda j, k: (l, 0, 0))
    if which == "in":
        ospec = pl.BlockSpec((None, orows, ocols), lambda j, k: (l, 0, j))
    else:
        ospec = pl.BlockSpec((None, None, orows, ocols), lambda j, k: (l, j, 0, 0))
    in_specs = [lspec, rspec, mspec]
    args = [lhs, rhs, modv]
    out_specs = [ospec]
    out_shape = [jax.ShapeDtypeStruct(buf.shape, buf.dtype)]
    if gated:
        in_specs.append(pl.BlockSpec((None, None, orows, ocols), lambda j, k: (j, l, 0, 0)))
        args.append(w_gate)
        out_specs.append(pl.BlockSpec((8, D), lambda j, k: (0, 0)))
        out_shape.append(jax.ShapeDtypeStruct((8, D), F32))
    in_specs.append(pl.BlockSpec(memory_space=pl.ANY))
    args.append(buf)
    res = pl.pallas_call(
        body, name=f"grad_w_{which}_{l}", out_shape=out_shape, grid=(nj, nt), in_specs=in_specs, out_specs=out_specs,
        scratch_shapes=[pltpu.VMEM((orows, ocols), F32)], input_output_aliases={len(args) - 1: 0},
        compiler_params=_cparams(("arbitrary", "arbitrary")),
    )(*args)
    return (res[0], res[1]) if gated else (res[0], None)


def _tri_masks():
    rows, cols = _iota2((BLK, BLK), 0), _iota2((BLK, BLK), 1)
    return rows >= cols, rows > cols


def _sgu_forward(p_ref, lnp_ref, sguw_ref, sgub_ref):
    incl, _ = _tri_masks()
    ug = _gelu(p_ref[:, 0:512])
    vg = _gelu(p_ref[:, 512:1024])
    mu = jnp.mean(vg, axis=-1, keepdims=True)
    xc = vg - mu
    rstd = lax.rsqrt(jnp.mean(xc * xc, axis=-1, keepdims=True) + LN_EPS)
    vhat = xc * rstd
    vn = vhat * lnp_ref[0:1, :] + lnp_ref[1:2, :]
    bias = sgub_ref[...]
    ys, mixed, wms = [], [], []
    for h in range(HEADS):
        wm = jnp.where(incl, sguw_ref[h], 0.0)
        mx = _dot(wm, vn[:, h * HD:(h + 1) * HD]) + _col(bias, h)
        ys.append(ug[:, h * HD:(h + 1) * HD] * mx)
        mixed.append(mx)
        wms.append(wm)
    return ys, ug, vhat, rstd, vn, mixed, wms


def _conv_forward(xbuf, cw_ref):
    conv = cw_ref[0:1, :] * xbuf[5:5 + BLK, :]
    for j in range(1, 4):
        conv = conv + cw_ref[j:j + 1, :] * xbuf[5 + j:5 + j + BLK, :]
    return conv


def _gates(gt, gv_ref):
    incl, _ = _tri_masks()
    beta = _sigmoid(gt)
    neg_a = -jnp.exp(gv_ref[0:1, :])
    gl = neg_a * _softplus(gt + gv_ref[1:2, :])
    gc = _dotf(jnp.where(incl, 1.0, 0.0).astype(F32), gl)
    return beta, gl, gc, gc.T, neg_a


def _head_chunk(act, beta, gc, gct, h):
    incl, strict = _tri_masks()
    qh = act[:, h * HD:(h + 1) * HD]
    kh = act[:, 512 + h * HD:512 + (h + 1) * HD]
    vh = act[:, 1024 + h * HD:1024 + (h + 1) * HD]
    rq = lax.rsqrt(jnp.sum(qh * qh, axis=-1, keepdims=True) + RMS_EPS)
    rk = lax.rsqrt(jnp.sum(kh * kh, axis=-1, keepdims=True) + RMS_EPS)
    qhat, khat = qh * rq, kh * rk
    qn = qhat * QK_SCALE
    b = _col(beta, h)
    gcol = _col(gc, 4 + h)
    grow = _row(gct, 4 + h)
    dmat = jnp.where(incl, jnp.exp(jnp.where(incl, gcol - grow, 0.0)), 0.0)
    gam = jnp.exp(gcol)
    glast = _row(gcol, BLK - 1)
    e = jnp.exp(glast - gcol)
    kk = _d3_nt(khat, khat)
    return dict(qhat=qhat, khat=khat, qn=qn, vh=vh, rq=rq, rk=rk, b=b, dmat=dmat, gam=gam, glast=glast, e=e, kk=kk,
                strict=strict, incl=incl)


def _mixer_forward(p, lnp, sgu_w, sgu_bt, cw, gv, l):
    t = p.shape[0]
    nb = t // BLK

    def body(p_ref, lnp_ref, sguw_ref, sgub_ref, cw_ref, gv_ref,
             mix_ref, s_out, t_out, u_out, w_out, o_out, tail_out, s_scr, xbuf):
        @pl.when(pl.program_id(0) == 0)
        def _():
            s_scr[...] = jnp.zeros_like(s_scr)
            xbuf[0:8, :] = jnp.zeros((8, 1536), F32)

        ys = _sgu_forward(p_ref, lnp_ref, sguw_ref, sgub_ref)[0]
        for h in range(HEADS):
            mix_ref[:, h * HD:(h + 1) * HD] = ys[h].astype(BF16)

        tail_out[...] = xbuf[0:8, :]
        xbuf[8:8 + BLK, :] = p_ref[:, 1024:2560]
        act = _silu(_conv_forward(xbuf, cw_ref))
        xbuf[0:8, :] = xbuf[BLK:BLK + 8, :]
        beta, _, gc, gct, _ = _gates(p_ref[:, GATE0:NW], gv_ref)
        for h in range(HEADS):
            hc = _head_chunk(act, beta, gc, gct, h)
            m = jnp.where(hc["strict"], hc["b"] * hc["kk"] * hc["dmat"], 0.0)
            tm = _tri_inverse(m)
            u = _d3(tm, hc["b"] * hc["vh"])
            w = _d3(tm, (hc["b"] * hc["gam"]) * hc["khat"])
            qkm = _d3_nt(hc["qn"], hc["khat"]) * hc["dmat"]
            s = s_scr[h]
            wn = u - _d3(w, s)
            o = _d3(hc["qn"] * hc["gam"], s) + _d3(qkm, wn)
            s_out[h] = s
            s_scr[h] = jnp.exp(hc["glast"]) * s + _d3_tn(hc["khat"] * hc["e"], wn)
            t_out[h] = tm
            sl = slice(h * HD, (h + 1) * HD)
            u_out[:, sl] = u
            w_out[:, sl] = w
            o_out[:, sl] = o
            on = o * lax.rsqrt(jnp.mean(o * o, axis=-1, keepdims=True) + RMS_EPS) * gv_ref[2:3, :]
            mix_ref[:, 512 + h * HD:512 + (h + 1) * HD] = (on * _silu(p_ref[:, 2560 + h * HD:2560 + (h + 1) * HD])).astype(BF16)

    tok = lambda w: pl.BlockSpec((BLK, w), lambda i: (i, 0))
    st = pl.BlockSpec((None, HEADS, HD, HD), lambda i: (i, 0, 0, 0))
    return pl.pallas_call(
        body, name=f"mixer_fwd_{l}", grid=(nb,),
        out_shape=[jax.ShapeDtypeStruct((t, D), BF16), jax.ShapeDtypeStruct((nb, HEADS, HD, HD), F32),
                   jax.ShapeDtypeStruct((nb, HEADS, HD, HD), F32), jax.ShapeDtypeStruct((t, 512), F32),
                   jax.ShapeDtypeStruct((t, 512), F32), jax.ShapeDtypeStruct((t, 512), F32),
                   jax.ShapeDtypeStruct((nb, 8, 1536), F32)],
        in_specs=[tok(NW), pl.BlockSpec((None, 8, 512), lambda i: (l, 0, 0)),
                  pl.BlockSpec((None, HEADS, HD, HD), lambda i: (l, 0, 0, 0)),
                  pl.BlockSpec((None, HD, HD), lambda i: (l, 0, 0)), pl.BlockSpec((None, 8, 1536), lambda i: (l, 0, 0)),
                  pl.BlockSpec((None, 8, HD), lambda i: (l, 0, 0))],
        out_specs=[tok(D), st, st, tok(512), tok(512), tok(512), pl.BlockSpec((None, 8, 1536), lambda i: (i, 0, 0))],
        scratch_shapes=[pltpu.VMEM((HEADS, HD, HD), F32), pltpu.VMEM((BLK + 8, 1536), F32)],
        compiler_params=_cparams(("arbitrary",)),
    )(p, lnp, sgu_w, sgu_bt, cw, gv)


def _mixer_backward(p, dmix, saved, lnp, sgu_w, sgu_bt, cw, gv, l):
    t = p.shape[0]
    nb = t // BLK
    s_sv, t_sv, u_sv, w_sv, o_sv, tail_sv = saved

    def body(p_ref, dmix_ref, s_ref, t_ref, u_ref, w_ref, o_ref, tail_ref, lnp_ref, sguw_ref, sgub_ref, cw_ref, gv_ref,
             dp_ref, dlnp_ref, dsguw_ref, dsgub_ref, dcw_ref, dgv_ref, ds_scr, xbuf, dcbuf):
        @pl.when(pl.program_id(0) == 0)
        def _():
            ds_scr[...] = jnp.zeros_like(ds_scr)
            dcbuf[BLK:BLK + 8, :] = jnp.zeros((8, 1536), F32)
            dlnp_ref[...] = jnp.zeros_like(dlnp_ref)
            dsguw_ref[...] = jnp.zeros_like(dsguw_ref)
            dsgub_ref[...] = jnp.zeros_like(dsgub_ref)
            dcw_ref[...] = jnp.zeros_like(dcw_ref)
            dgv_ref[...] = jnp.zeros_like(dgv_ref)

        incl, strict = _tri_masks()
        _, ug, vhat, rstd, vn, mixed, wms = _sgu_forward(p_ref, lnp_ref, sguw_ref, sgub_ref)
        dvn_parts, dug_parts = [], []
        dbias = jnp.zeros((BLK, HD), F32)
        for h in range(HEADS):
            sl = slice(h * HD, (h + 1) * HD)
            dy = dmix_ref[:, sl].astype(F32)
            dmx = dy * ug[:, sl]
            dug_parts.append(dy * mixed[h])
            dsguw_ref[h] += jnp.where(incl, _dot_nt(dmx, vn[:, sl]), 0.0)
            dbias = dbias + _put_col(jnp.sum(dmx, axis=1, keepdims=True), h)
            dvn_parts.append(_dot_tn(wms[h], dmx))
        dsgub_ref[...] += dbias
        dvn = jnp.concatenate(dvn_parts, axis=1)
        dug = jnp.concatenate(dug_parts, axis=1)
        dlnp_ref[0:1, :] += jnp.sum(dvn * vhat, axis=0, keepdims=True)
        dlnp_ref[1:2, :] += jnp.sum(dvn, axis=0, keepdims=True)
        dvhat = dvn * lnp_ref[0:1, :]
        dvg = rstd * (dvhat - jnp.mean(dvhat, axis=-1, keepdims=True)
                      - vhat * jnp.mean(dvhat * vhat, axis=-1, keepdims=True))
        dp_ref[:, 0:512] = (dug * _gelu_grad(p_ref[:, 0:512])).astype(BF16)
        dp_ref[:, 512:1024] = (dvg * _gelu_grad(p_ref[:, 512:1024])).astype(BF16)

        xbuf[0:8, :] = tail_ref[...]
        xbuf[8:8 + BLK, :] = p_ref[:, 1024:2560]
        conv = _conv_forward(xbuf, cw_ref)
        act = _silu(conv)
        gt = p_ref[:, GATE0:NW]
        beta, gl, gc, gct, neg_a = _gates(gt, gv_ref)
        gng = gv_ref[2:3, :]
        dbeta_t = jnp.zeros((BLK, HD), F32)
        dgc_t = jnp.zeros((BLK, HD), F32)
        dgng = jnp.zeros((1, HD), F32)
        for h in range(HEADS):
            sl = slice(h * HD, (h + 1) * HD)
            hc = _head_chunk(act, beta, gc, gct, h)
            b, gam, e, dmat, kk = hc["b"], hc["gam"], hc["e"], hc["dmat"], hc["kk"]
            qn, khat, vh = hc["qn"], hc["khat"], hc["vh"]
            gamlast = jnp.exp(hc["glast"])
            s, tm, u, w, o = s_ref[h], t_ref[h], u_ref[:, sl], w_ref[:, sl], o_ref[:, sl]
            ds_next = ds_scr[h]
            z = p_ref[:, 2560 + h * HD:2560 + (h + 1) * HD]
            dy = dmix_ref[:, 512 + h * HD:512 + (h + 1) * HD].astype(F32)
            ro = lax.rsqrt(jnp.mean(o * o, axis=-1, keepdims=True) + RMS_EPS)
            ohat = o * ro
            dp_ref[:, 2560 + h * HD:2560 + (h + 1) * HD] = (dy * ohat * gng * _silu_grad(z)).astype(BF16)
            don = dy * _silu(z)
            dgng = dgng + jnp.sum(don * ohat, axis=0, keepdims=True)
            dohat = don * gng
            do = ro * (dohat - ohat * jnp.mean(dohat * ohat, axis=-1, keepdims=True))
            qk_raw = _d3_nt(qn, khat)
            qkm = qk_raw * dmat
            qd, kd = qn * gam, khat * e
            wn = u - _d3(w, s)
            dwn = _d3_tn(qkm, do) + _d3(kd, ds_next)
            dqd = _d3_nt(do, s)
            dqkm = jnp.where(incl, _d3_nt(do, wn), 0.0)
            ds_scr[h] = _d3_tn(qd, do) + gamlast * ds_next - _d3_tn(w, dwn)
            dgamlast = jnp.sum(jnp.sum(ds_next * s, axis=1, keepdims=True), axis=0, keepdims=True)
            dkd = _d3_nt(wn, ds_next)
            dw = -_d3_nt(dwn, s)
            db1 = _d3_tn(tm, dwn)
            db2 = _d3_tn(tm, dw)
            dm = jnp.where(strict, -(_d3_nt(db1, u) + _d3_nt(db2, w)), 0.0)
            dbeta = (jnp.sum(dm * kk * dmat, axis=1, keepdims=True) + jnp.sum(db1 * vh, axis=1, keepdims=True)
                     + gam * jnp.sum(db2 * khat, axis=1, keepdims=True))
            dkkm = dm * b * dmat
            ddm = dm * b * kk + dqkm * qk_raw
            dgam = b * jnp.sum(db2 * khat, axis=1, keepdims=True) + jnp.sum(dqd * qn, axis=1, keepdims=True)
            g_qk = dqkm * dmat
            dqn = _d3(g_qk, khat) + dqd * gam
            dkhat = ((b * gam) * db2 + _d3_tn(g_qk, qn) + _d3(dkkm, khat) + _d3_tn(dkkm, khat) + dkd * e)
            dvh = b * db1
            rkd = jnp.sum(dkd * kd, axis=1, keepdims=True)
            emat = ddm * dmat
            dgc = (dgam * gam - rkd + jnp.sum(emat, axis=1, keepdims=True)
                   - jnp.sum(emat.T, axis=1, keepdims=True))
            last = _iota2((BLK, 1), 0) == BLK - 1
            dgc = dgc + jnp.where(last, jnp.sum(rkd, axis=0, keepdims=True) + dgamlast * gamlast, 0.0)
            dgc_t = dgc_t + _put_col(dgc, 4 + h)
            dbeta_t = dbeta_t + _put_col(dbeta, h)
            dqhat = dqn * QK_SCALE
            dq = hc["rq"] * (dqhat - hc["qhat"] * jnp.sum(dqhat * hc["qhat"], axis=-1, keepdims=True))
            dk = hc["rk"] * (dkhat - khat * jnp.sum(dkhat * khat, axis=-1, keepdims=True))
            dcbuf[0:BLK, h * HD:(h + 1) * HD] = dq
            dcbuf[0:BLK, 512 + h * HD:512 + (h + 1) * HD] = dk
            dcbuf[0:BLK, 1024 + h * HD:1024 + (h + 1) * HD] = dvh
        dgv_ref[2:3, :] += dgng
        dgl = _dotf_tn(jnp.where(incl, 1.0, 0.0).astype(F32), dgc_t)
        sig_a = _sigmoid(gt + gv_ref[1:2, :])
        d_araw = dgl * neg_a * sig_a
        dgv_ref[0:1, :] += jnp.sum(dgl * gl, axis=0, keepdims=True)
        dgv_ref[1:2, :] += jnp.sum(d_araw, axis=0, keepdims=True)
        dp_ref[:, GATE0:NW] = (dbeta_t * beta * (1.0 - beta) + d_araw).astype(BF16)
        dcbuf[0:BLK, :] = dcbuf[0:BLK, :] * _silu_grad(conv)
        dqkv = cw_ref[0:1, :] * dcbuf[3:3 + BLK, :]
        dcw_ref[0:1, :] += jnp.sum(dcbuf[0:BLK, :] * xbuf[5:5 + BLK, :], axis=0, keepdims=True)
        for j in range(1, 4):
            dqkv = dqkv + cw_ref[j:j + 1, :] * dcbuf[3 - j:3 - j + BLK, :]
            dcw_ref[j:j + 1, :] += jnp.sum(dcbuf[0:BLK, :] * xbuf[5 + j:5 + j + BLK, :], axis=0, keepdims=True)
        dp_ref[:, 1024:2560] = dqkv.astype(BF16)
        dcbuf[BLK:BLK + 8, :] = dcbuf[0:8, :]

    rev = lambda w: pl.BlockSpec((BLK, w), lambda i: (nb - 1 - i, 0))
    st = pl.BlockSpec((None, HEADS, HD, HD), lambda i: (nb - 1 - i, 0, 0, 0))
    fix = lambda *shape: pl.BlockSpec((None,) + shape, lambda i: (l,) + (0,) * len(shape))
    acc = lambda *shape: pl.BlockSpec(shape, lambda i: (0,) * len(shape))
    return pl.pallas_call(
        body, name=f"mixer_bwd_{l}", grid=(nb,),
        out_shape=[jax.ShapeDtypeStruct((t, NW), BF16), jax.ShapeDtypeStruct((8, 512), F32),
                   jax.ShapeDtypeStruct((HEADS, HD, HD), F32), jax.ShapeDtypeStruct((HD, HD), F32),
                   jax.ShapeDtypeStruct((8, 1536), F32), jax.ShapeDtypeStruct((8, HD), F32)],
        in_specs=[rev(NW), rev(D), st, st, rev(512), rev(512), rev(512),
                  pl.BlockSpec((None, 8, 1536), lambda i: (nb - 1 - i, 0, 0)),
                  fix(8, 512), fix(HEADS, HD, HD), fix(HD, HD), fix(8, 1536), fix(8, HD)],
        out_specs=[rev(NW), acc(8, 512), acc(HEADS, HD, HD), acc(HD, HD), acc(8, 1536), acc(8, HD)],
        scratch_shapes=[pltpu.VMEM((HEADS, HD, HD), F32), pltpu.VMEM((BLK + 8, 1536), F32),
                        pltpu.VMEM((BLK + 8, 1536), F32)],
        compiler_params=_cparams(("arbitrary",)),
    )(p, dmix, s_sv, t_sv, u_sv, w_sv, o_sv, tail_sv, lnp, sgu_w, sgu_bt, cw, gv)


_SMALL = (("b_ada", 24), ("norm1_g", 4), ("norm2_g", 4), ("final_g", 1), ("sgu_ln_g", 2), ("sgu_ln_b", 2),
          ("sgu_w", 256), ("sgu_b", 2), ("conv_w", 24), ("a_log", 1), ("dt_bias", 1), ("gdn_norm_g", 1))
_SMALL_ROWS = sum(n for _, n in _SMALL)
_SMALL_PAD = 328
_DMOD_ROWS = 24


def _pack_rows(parts):
    rows = []
    for (name, n), a in zip(_SMALL, parts):
        flat = a.reshape(-1).astype(F32)
        rows.append(jnp.pad(flat, (0, n * D - flat.shape[0])).reshape(n, D))
    rows.append(jnp.zeros((_SMALL_PAD - _SMALL_ROWS, D), F32))
    return jnp.concatenate(rows, axis=0)


def _unpack_rows(buf, shapes):
    out, r0 = {}, 0
    for name, n in _SMALL:
        size = math.prod(shapes[name])
        out[name] = buf[r0:r0 + n].reshape(-1)[:size].reshape(shapes[name])
        r0 += n
    return out


def _pair_combine(own, sib):
    n = own.shape[0] - _DMOD_ROWS

    def body(a_ref, b_ref, o_ref):
        first = lax.axis_index("c") == 0
        a, b = a_ref[0:_DMOD_ROWS, :], b_ref[0:_DMOD_ROWS, :]
        o_ref[0:_DMOD_ROWS, :] = jnp.where(first, a, b)
        o_ref[_DMOD_ROWS:2 * _DMOD_ROWS, :] = jnp.where(first, b, a)
        o_ref[2 * _DMOD_ROWS:, :] = a_ref[_DMOD_ROWS:, :] + b_ref[_DMOD_ROWS:, :]

    return pl.pallas_call(
        body, name="small_pair_combine", out_shape=jax.ShapeDtypeStruct((2 * _DMOD_ROWS + n, D), F32),
        compiler_params=pltpu.CompilerParams(vmem_limit_bytes=VMEM_LIMIT),
    )(own, sib)


def _small_finalize(gathered, w, m, v):
    def body(g_ref, w_ref, m_ref, v_ref, go_ref, d_ref, nm_ref, nv_ref):
        sm = g_ref[0, 2 * _DMOD_ROWS:, :] + g_ref[1, 2 * _DMOD_ROWS:, :]
        sm = sm + g_ref[2, 2 * _DMOD_ROWS:, :]
        sm = sm + g_ref[3, 2 * _DMOD_ROWS:, :]
        bsum = jnp.zeros((_DMOD_ROWS, D), F32)
        for j in range(N_CHIPS):
            bsum = bsum + g_ref[j, 0:_DMOD_ROWS, :]
            bsum = bsum + g_ref[j, _DMOD_ROWS:2 * _DMOD_ROWS, :]
        go_ref[0:_DMOD_ROWS, :] = bsum
        go_ref[_DMOD_ROWS:, :] = sm[_DMOD_ROWS:, :]
        d_ref[...], nm_ref[...], nv_ref[...] = _adam_math(w_ref[...], go_ref[...], m_ref[...], v_ref[...])

    return pl.pallas_call(
        body, name="small_finalize", out_shape=[jax.ShapeDtypeStruct(w.shape, F32)] * 4,
        compiler_params=pltpu.CompilerParams(vmem_limit_bytes=VMEM_LIMIT),
    )(gathered, w, m, v)


def kernel(x, c, w_ada, b_ada, norm1_g, w_in, sgu_ln_g, sgu_ln_b, sgu_w, sgu_b, conv_w, a_log, dt_bias, gdn_norm_g, w_out, norm2_g, w_ff1, w_ff2, final_g, loss_target, m_w_ada, m_b_ada, m_norm1_g, m_w_in, m_sgu_ln_g, m_sgu_ln_b, m_sgu_w, m_sgu_b, m_conv_w, m_a_log, m_dt_bias, m_gdn_norm_g, m_w_out, m_norm2_g, m_w_ff1, m_w_ff2, m_final_g, v_w_ada, v_b_ada, v_norm1_g, v_w_in, v_sgu_ln_g, v_sgu_ln_b, v_sgu_w, v_sgu_b, v_conv_w, v_a_log, v_dt_bias, v_gdn_norm_g, v_w_out, v_norm2_g, v_w_ff1, v_w_ff2, v_final_g):
    xi, yi, ci = lax.axis_index("x"), lax.axis_index("y"), lax.axis_index("c")
    chip = 2 * xi + yi
    dev = 2 * chip + ci
    t = x.shape[1]
    x0 = x.reshape(t, D)
    target = loss_target.reshape(t, D)

    c_sib = _pair_exchange(c, "c_pair")
    c_pair = jnp.where(ci == 0, jnp.concatenate([c, c_sib], 0), jnp.concatenate([c_sib, c], 0))
    c_all = _chip_allgather(c_pair, "c_chips").reshape(8, D)
    ada_cols = w_ada.shape[2]
    b_cols = lax.dynamic_slice_in_dim(b_ada, chip * ada_cols, ada_cols, axis=1)
    mod_part = _ada_forward(c_all, w_ada, b_cols)
    conv_cols = conv_w.shape[2]
    packed = jnp.concatenate([mod_part.reshape(DEPTH * 8, ada_cols), conv_w.reshape(DEPTH, 4 * conv_cols)], axis=0)
    packed = _chip_allgather(packed, "mod_chips")
    mod_all = packed[:, :DEPTH * 8].reshape(N_CHIPS, DEPTH, 8, ada_cols)
    mod_mine = lax.dynamic_index_in_dim(mod_all, dev, axis=2, keepdims=False)
    mod = mod_mine.transpose(1, 0, 2).reshape(DEPTH, 6, D)
    modv = jnp.concatenate([mod, norm1_g[:, None, :], norm2_g[:, None, :]], axis=1)
    conv_full = packed[:, DEPTH * 8:].reshape(N_CHIPS, DEPTH, 4, conv_cols).transpose(1, 2, 0, 3).reshape(DEPTH, 4, 1536)

    place = jnp.stack([chip, ci]).astype(jnp.int32)
    g_in, g_out, g_ff1, g_ff2 = _weights_allgather([_cast_into_slot(w, place) for w in (w_in, w_out, w_ff1, w_ff2)])
    win = g_in.transpose(1, 2, 0, 3).reshape(DEPTH, D, IN_W)
    win = jnp.pad(win, ((0, 0), (0, 0), (0, NW - IN_W)))

    lnp = jnp.pad(jnp.stack([sgu_ln_g, sgu_ln_b], axis=1), ((0, 0), (0, 6), (0, 0)))
    sgu_bt = jnp.pad(sgu_b.transpose(0, 2, 1), ((0, 0), (0, 0), (0, HD - HEADS)))
    cw = jnp.pad(conv_full, ((0, 0), (0, 4), (0, 0)))
    lane_pad = lambda a: jnp.pad(a, ((0, 0), (4, HD - 8)))
    gv = jnp.pad(jnp.stack([lane_pad(a_log), lane_pad(dt_bias), gdn_norm_g], axis=1), ((0, 0), (0, 5), (0, 0)))

    acts = []
    xl = x0
    for l in range(DEPTH):
        p = _fwd_in(xl, modv, win, l)
        mix, *saved = _mixer_forward(p, lnp, sgu_w, sgu_bt, cw, gv, l)
        x1 = _fwd_out(xl, mix, modv, g_out, l)
        r = _fwd_ff1(x1, modv, g_ff1, l)
        x2 = _fwd_ff2(x1, r, modv, g_ff2, l)
        acts.append((xl, p, mix, saved, x1, r))
        xl = x2

    dx, head_stats = _loss_head(xl, target, final_g)
    loss = lax.psum(jnp.sum(head_stats[1, 0:1]), ("x", "y", "c"))
    d_final_g = head_stats[0]
    gb_in = jnp.zeros((DEPTH, D, NW), BF16)
    gb_out = jnp.zeros((DEPTH, N_CHIPS, D // N_CHIPS, D), BF16)
    gb_ff1 = jnp.zeros((DEPTH, N_CHIPS, D, D), BF16)
    gb_ff2 = jnp.zeros((DEPTH, N_CHIPS, D, D), BF16)
    dmod, small = [None] * DEPTH, [None] * DEPTH
    for l in reversed(range(DEPTH)):
        xl, p, mix, saved, x1, r = acts[l]
        df = _bwd_ff2(dx, r, modv, g_ff2, l)
        gb_ff2, dg2 = _grad_weight(r, dx, gb_ff2, modv, l, "ff2", g_ff2)
        gb_ff1, _ = _grad_weight(x1, df, gb_ff1, modv, l, "ff1")
        dx1, st2 = _bwd_norm(df, g_ff1, x1, dx, modv, l, "ff1")
        dmix = _bwd_out(dx1, modv, g_out, l)
        gb_out, dg1 = _grad_weight(mix, dx1, gb_out, modv, l, "out", g_out)
        dp, dlnp, dsguw, dsgub, dcw, dgv = _mixer_backward(p, dmix, saved, lnp, sgu_w, sgu_bt, cw, gv, l)
        gb_in, _ = _grad_weight(xl, dp, gb_in, modv, l, "in")
        dx, st1 = _bwd_norm(dp, win, xl, dx1, modv, l, "in")
        dmod[l] = jnp.stack([st1[0], st1[1], dg1[0], st2[0], st2[1], dg2[0]], axis=0)
        small[l] = dict(norm1_g=st1[2], norm2_g=st2[2], sgu_ln_g=dlnp[0], sgu_ln_b=dlnp[1], sgu_w=dsguw,
                        sgu_b=dsgub[:, :HEADS].T, conv_w=dcw[:4], a_log=dgv[0, 4:8], dt_bias=dgv[1, 4:8],
                        gdn_norm_g=dgv[2])
    grad_x = dx.reshape(1, t, D)

    stack = lambda k: jnp.stack([small[l][k] for l in range(DEPTH)], axis=0)
    small_grads = [jnp.zeros((DEPTH, 6 * D), F32), stack("norm1_g"), stack("norm2_g"), d_final_g, stack("sgu_ln_g"),
                   stack("sgu_ln_b"), stack("sgu_w"), stack("sgu_b"), stack("conv_w"), stack("a_log"),
                   stack("dt_bias"), stack("gdn_norm_g")]
    own = jnp.concatenate([jnp.stack(dmod, axis=0).reshape(_DMOD_ROWS, D), _pack_rows(small_grads)], axis=0)
    sib = _pair_exchange(own, "small_pair")
    gathered = _chip_allgather(_pair_combine(own, sib), "small_chips")
    small_shapes = dict(b_ada=b_ada.shape, norm1_g=norm1_g.shape, norm2_g=norm2_g.shape, final_g=final_g.shape,
                        sgu_ln_g=sgu_ln_g.shape, sgu_ln_b=sgu_ln_b.shape, sgu_w=sgu_w.shape, sgu_b=sgu_b.shape,
                        conv_w=(DEPTH, 4, 1536), a_log=a_log.shape, dt_bias=dt_bias.shape,
                        gdn_norm_g=gdn_norm_g.shape)

    def full_conv(a):
        return lax.dynamic_update_slice_in_dim(jnp.zeros((DEPTH, 4, 1536), F32), a, chip * conv_cols, axis=2)

    def pack_state(b_, n1, n2, fg, lg, lb, sw, sb, cv, al, db, gn):
        return _pack_rows([b_, n1, n2, fg, lg, lb, sw, sb, full_conv(cv), al, db, gn])

    w_small = pack_state(b_ada, norm1_g, norm2_g, final_g, sgu_ln_g, sgu_ln_b, sgu_w, sgu_b, conv_w, a_log, dt_bias,
                         gdn_norm_g)
    m_small = pack_state(m_b_ada, m_norm1_g, m_norm2_g, m_final_g, m_sgu_ln_g, m_sgu_ln_b, m_sgu_w, m_sgu_b, m_conv_w,
                         m_a_log, m_dt_bias, m_gdn_norm_g)
    v_small = pack_state(v_b_ada, v_norm1_g, v_norm2_g, v_final_g, v_sgu_ln_g, v_sgu_ln_b, v_sgu_w, v_sgu_b, v_conv_w,
                         v_a_log, v_dt_bias, v_gdn_norm_g)
    small_out = _small_finalize(gathered, w_small, m_small, v_small)
    sg, sd, sm, sv = [_unpack_rows(a, small_shapes) for a in small_out]
    for dct in (sg, sd, sm, sv):
        dct["conv_w"] = lax.dynamic_slice_in_dim(dct["conv_w"], chip * conv_cols, conv_cols, axis=2)

    dmod_all = gathered[:, :2 * _DMOD_ROWS].reshape(8, DEPTH, 6 * D)
    dmod_cols = lax.dynamic_slice_in_dim(dmod_all, chip * ada_cols, ada_cols, axis=2).transpose(1, 0, 2)
    g_ada, d_ada, nm_ada, nv_ada = _ada_backward_adamw(c_all, dmod_cols, w_ada, m_w_ada, v_w_ada)

    gb_in_c = gb_in[:, :, :IN_W].reshape(DEPTH, D, N_CHIPS, IN_W // N_CHIPS).transpose(0, 2, 1, 3)
    partials = [gb_in_c, gb_out, gb_ff1, gb_ff2]
    from_sib = _grads_pair_send(partials)
    names = ("in", "out", "ff1", "ff2")
    pair = [_pair_sum(g, ga, place, f"pair_sum_{n}") for g, ga, n in zip(partials, from_sib, names)]
    recv = _grads_chip_exchange(pair)
    mine = [_chip_sum(p, rc, place, f"chip_sum_{n}") for p, rc, n in zip(pair, recv, names)]
    grads = _grads_pair_share(mine)
    big = {}
    for n, g, (w, m, v) in zip(names, grads, ((w_in, m_w_in, v_w_in), (w_out, m_w_out, v_w_out),
                                              (w_ff1, m_w_ff1, v_w_ff1), (w_ff2, m_w_ff2, v_w_ff2))):
        big[n] = (g,) + tuple(_adamw(w, g, m, v, f"adamw_{n}"))

    def outs(k):
        s = (sg, sd, sm, sv)[k]
        return [(g_ada, d_ada, nm_ada, nv_ada)[k], s["b_ada"], s["norm1_g"], big["in"][k], s["sgu_ln_g"],
                s["sgu_ln_b"], s["sgu_w"], s["sgu_b"], s["conv_w"], s["a_log"], s["dt_bias"], s["gdn_norm_g"],
                big["out"][k], s["norm2_g"], big["ff1"][k], big["ff2"][k], s["final_g"]]

    return (loss, grad_x, *outs(0), *outs(1), *outs(2), *outs(3))
```

```python
import functools
import math

import jax
import jax.numpy as jnp
from jax import lax
from jax.experimental import pallas as pl
from jax.experimental.pallas import tpu as pltpu

F32 = jnp.float32
BF16 = jnp.bfloat16

DEPTH = 4
D = 1024
HEADS = 4
HD = 128
BLK = 128
IN_W = 3080
NW = 3200
GATE0 = 3072
DFF = 4096
N_CHIPS = 4
RMS_EPS = 1e-6
LN_EPS = 1e-5
QK_SCALE = HD ** -0.5
LR, B1, B2, ADAM_EPS, WD, STEP = 0.001, 0.9, 0.999, 1e-08, 0.01, 10
VMEM_LIMIT = 56 * 1024 * 1024
MESH = pl.DeviceIdType.MESH
HOPS = ((1, 0), (0, 1), (1, 1))
HI = lax.Precision.HIGHEST


def _dot(a, b):
    return jnp.dot(a.astype(BF16), b.astype(BF16), preferred_element_type=F32)


def _dot_nt(a, b):
    return lax.dot_general(a.astype(BF16), b.astype(BF16), (((1,), (1,)), ((), ())), preferred_element_type=F32)


def _dot_tn(a, b):
    return lax.dot_general(a.astype(BF16), b.astype(BF16), (((0,), (0,)), ((), ())), preferred_element_type=F32)


def _dotf(a, b):
    return jnp.dot(a, b, precision=HI, preferred_element_type=F32)


def _split(a):
    hi = a.astype(BF16)
    return hi, (a - hi.astype(F32)).astype(BF16)


def _dg3(a, b, dims, batch=((), ())):
    ah, al = _split(a)
    bh, bl = _split(b)
    f = lambda x, y: lax.dot_general(x, y, (dims, batch), preferred_element_type=F32)
    return f(ah, bh) + (f(ah, bl) + f(al, bh))


def _bmm3(a, b):
    return _dg3(a, b, ((2,), (1,)), ((0,), (0,)))


def _d3(a, b):
    return _dg3(a, b, ((1,), (0,)))


def _d3_nt(a, b):
    return _dg3(a, b, ((1,), (1,)))


def _d3_tn(a, b):
    return _dg3(a, b, ((0,), (0,)))


def _dotf_tn(a, b):
    return lax.dot_general(a, b, (((0,), (0,)), ((), ())), precision=HI, preferred_element_type=F32)


def _sigmoid(x):
    return 1.0 / (1.0 + jnp.exp(-x))


def _softplus(x):
    return jnp.maximum(x, 0.0) + jnp.log(1.0 + jnp.exp(-jnp.abs(x)))


_G0 = math.sqrt(2.0 / math.pi)
_G1 = 0.044715


def _gelu(x):
    t = jnp.tanh(_G0 * (x + _G1 * x * x * x))
    return 0.5 * x * (1.0 + t)


def _gelu_grad(x):
    t = jnp.tanh(_G0 * (x + _G1 * x * x * x))
    return 0.5 * (1.0 + t) + 0.5 * x * (1.0 - t * t) * (_G0 * (1.0 + 3.0 * _G1 * x * x))


def _silu(x):
    return x * _sigmoid(x)


def _silu_grad(x):
    s = _sigmoid(x)
    return s * (1.0 + x * (1.0 - s))


def _rms_stats(x):
    rstd = lax.rsqrt(jnp.mean(x * x, axis=-1, keepdims=True) + RMS_EPS)
    return x * rstd, rstd


def _norm_mod(x, ng, sc, sh):
    xh, _ = _rms_stats(x)
    return xh * (ng * (1.0 + sc)) + sh


def _norm_mod_bwd(dh, x, ng, sc):
    xh, rstd = _rms_stats(x)
    dsh = jnp.sum(dh, axis=0, keepdims=True)
    dsc = jnp.sum(dh * xh, axis=0, keepdims=True) * ng
    dng = jnp.sum(dh * xh, axis=0, keepdims=True) * (1.0 + sc)
    dxh = dh * (ng * (1.0 + sc))
    dx = rstd * (dxh - xh * jnp.mean(dxh * xh, axis=-1, keepdims=True))
    return dx, dsh, dsc, dng


def _iota2(shape, axis):
    return lax.broadcasted_iota(jnp.int32, shape, axis)


def _col(tile, idx):
    return jnp.sum(jnp.where(_iota2(tile.shape, 1) == idx, tile, 0.0), axis=1, keepdims=True)


def _row(tile, idx):
    return jnp.sum(jnp.where(_iota2(tile.shape, 0) == idx, tile, 0.0), axis=0, keepdims=True)


def _put_col(col, idx, width=HD):
    shape = (col.shape[0], width)
    return jnp.where(_iota2(shape, 1) == idx, jnp.broadcast_to(col, shape), 0.0)


def _tri_inverse(m):
    rows, cols = _iota2(m.shape, m.ndim - 2), _iota2(m.shape, m.ndim - 1)
    mm = _bmm3 if m.ndim == 3 else _d3
    eye = jnp.where(rows == cols, 1.0, 0.0).astype(F32)
    n = jnp.where((rows >> 3) == (cols >> 3), -m, 0.0)
    p = eye + n
    n2 = mm(n, n)
    p = p + mm(n2, p)
    n4 = mm(n2, n2)
    p = p + mm(n4, p)
    for shift in (3, 4, 5, 6):
        same_pair = (rows >> (shift + 1)) == (cols >> (shift + 1))
        below = jnp.logical_and(((rows >> shift) & 1) == 1, ((cols >> shift) & 1) == 0)
        off = jnp.where(jnp.logical_and(same_pair, below), m, 0.0)
        p = p - mm(p, mm(off, p))
    return p


def _cparams(sem=None):
    return pltpu.CompilerParams(dimension_semantics=sem, vmem_limit_bytes=VMEM_LIMIT)


def _my_place():
    return lax.axis_index("x"), lax.axis_index("y"), lax.axis_index("c")


def _hop(xi, yi, hop):
    dx, dy = hop
    return (1 - xi if dx else xi), (1 - yi if dy else yi)


def _pair_exchange(x, name):
    def body(x_ref, o_ref, ssem, rsem):
        xi, yi, ci = _my_place()
        cp = pltpu.make_async_remote_copy(x_ref, o_ref, ssem, rsem, device_id=(xi, yi, 1 - ci), device_id_type=MESH)
        cp.start()
        cp.wait()

    return pl.pallas_call(
        body, name=name, out_shape=jax.ShapeDtypeStruct(x.shape, x.dtype),
        in_specs=[pl.BlockSpec(memory_space=pltpu.VMEM)], out_specs=pl.BlockSpec(memory_space=pltpu.VMEM),
        scratch_shapes=[pltpu.SemaphoreType.DMA, pltpu.SemaphoreType.DMA],
        compiler_params=pltpu.CompilerParams(vmem_limit_bytes=VMEM_LIMIT),
    )(x)


def _chip_allgather(x, name):
    def body(x_ref, o_ref, ssems, rsems, lsem):
        xi, yi, ci = _my_place()
        me = 2 * xi + yi
        loc = pltpu.make_async_copy(x_ref, o_ref.at[me], lsem)
        loc.start()
        sends = []
        for k, hop in enumerate(HOPS):
            tx, ty = _hop(xi, yi, hop)
            cp = pltpu.make_async_remote_copy(x_ref, o_ref.at[me], ssems.at[k], rsems.at[k],
                                              device_id=(tx, ty, ci), device_id_type=MESH)
            cp.start()
            sends.append(cp)
        for k, hop in enumerate(HOPS):
            tx, ty = _hop(xi, yi, hop)
            pltpu.make_async_remote_copy(x_ref, o_ref.at[2 * tx + ty], ssems.at[k], rsems.at[k],
                                         device_id=(tx, ty, ci), device_id_type=MESH).wait_recv()
        for cp in sends:
            cp.wait_send()
        loc.wait()

    return pl.pallas_call(
        body, name=name, out_shape=jax.ShapeDtypeStruct((N_CHIPS,) + x.shape, x.dtype),
        in_specs=[pl.BlockSpec(memory_space=pltpu.VMEM)], out_specs=pl.BlockSpec(memory_space=pltpu.VMEM),
        scratch_shapes=[pltpu.SemaphoreType.DMA((3,)), pltpu.SemaphoreType.DMA((3,)), pltpu.SemaphoreType.DMA],
        compiler_params=pltpu.CompilerParams(vmem_limit_bytes=VMEM_LIMIT),
    )(x)


def _hbm_specs(n):
    return [pl.BlockSpec(memory_space=pl.ANY)] * n


def _cast_into_slot(w, place):
    n_l, r, c = w.shape
    tr = _row_tile(r)

    def body(p_ref, w_ref, o_ref):
        o_ref[...] = w_ref[...].astype(BF16)

    return pl.pallas_call(
        body, name=f"cast_slot_{r}x{c}", out_shape=jax.ShapeDtypeStruct((N_CHIPS,) + w.shape, BF16),
        grid_spec=pltpu.PrefetchScalarGridSpec(
            num_scalar_prefetch=1, grid=(n_l, r // tr),
            in_specs=[pl.BlockSpec((None, tr, c), lambda l, k, pr: (l, k, 0))],
            out_specs=pl.BlockSpec((None, None, tr, c), lambda l, k, pr: (pr[0], l, k, 0))),
        compiler_params=_cparams(("parallel", "parallel")),
    )(place, w)


def _weights_allgather(bufs):
    n = len(bufs)

    def body(*refs):
        outs = refs[n:2 * n]
        s_ici, r_ici, s_d2d, r_d2d = refs[2 * n:]
        xi, yi, ci = _my_place()
        me = 2 * xi + yi
        lay, sib_lay = pl.ds(2 * ci, 2), pl.ds(2 * (1 - ci), 2)
        pending = []
        for i in range(n):
            for k, hop in enumerate(HOPS):
                tx, ty = _hop(xi, yi, hop)
                cp = pltpu.make_async_remote_copy(outs[i].at[me, lay], outs[i].at[me, lay], s_ici.at[i, k],
                                                  r_ici.at[i, k], device_id=(tx, ty, ci), device_id_type=MESH)
                cp.start()
                pending.append(cp.wait_send)
        for k, hop in enumerate(HOPS):
            tx, ty = _hop(xi, yi, hop)
            src = 2 * tx + ty
            for i in range(n):
                pltpu.make_async_remote_copy(outs[i].at[src, lay], outs[i].at[src, lay], s_ici.at[i, k], r_ici.at[i, k],
                                             device_id=(tx, ty, ci), device_id_type=MESH).wait_recv()
                cp = pltpu.make_async_remote_copy(outs[i].at[src, lay], outs[i].at[src, lay], s_d2d.at[i, k],
                                                  r_d2d.at[i, k], device_id=(xi, yi, 1 - ci), device_id_type=MESH)
                cp.start()
                pending.append(cp.wait_send)
        for k, hop in enumerate(HOPS):
            tx, ty = _hop(xi, yi, hop)
            src = 2 * tx + ty
            for i in range(n):
                pltpu.make_async_remote_copy(outs[i].at[src, sib_lay], outs[i].at[src, sib_lay], s_d2d.at[i, k],
                                             r_d2d.at[i, k], device_id=(xi, yi, 1 - ci), device_id_type=MESH).wait_recv()
        for wait in pending:
            wait()

    return pl.pallas_call(
        body, name="weights_allgather",
        out_shape=[jax.ShapeDtypeStruct(b.shape, b.dtype) for b in bufs],
        in_specs=_hbm_specs(n), out_specs=_hbm_specs(n), input_output_aliases={i: i for i in range(n)},
        scratch_shapes=[pltpu.SemaphoreType.DMA((n, 3))] * 4,
    )(*bufs)


def _grads_pair_send(gs):
    n = len(gs)

    def body(*refs):
        ins, outs, ssem, rsem = refs[:n], refs[n:2 * n], refs[2 * n], refs[2 * n + 1]
        xi, yi, ci = _my_place()
        sib_lay = pl.ds(2 * (1 - ci), 2)
        cps = []
        for i in range(n):
            cp = pltpu.make_async_remote_copy(ins[i].at[sib_lay], outs[i], ssem.at[i], rsem.at[i],
                                              device_id=(xi, yi, 1 - ci), device_id_type=MESH)
            cp.start()
            cps.append(cp)
        for cp in cps:
            cp.wait()

    return pl.pallas_call(
        body, name="grads_pair_send",
        out_shape=[jax.ShapeDtypeStruct((2,) + g.shape[1:], g.dtype) for g in gs],
        in_specs=_hbm_specs(n), out_specs=_hbm_specs(n),
        scratch_shapes=[pltpu.SemaphoreType.DMA((n,)), pltpu.SemaphoreType.DMA((n,))],
    )(*gs)


def _grads_chip_exchange(ps):
    n = len(ps)

    def body(*refs):
        ins, outs = refs[:n], refs[n:2 * n]
        ssems, rsems = refs[2 * n:]
        xi, yi, ci = _my_place()
        both = pl.ds(0, 2)
        sends = []
        for i in range(n):
            for k, hop in enumerate(HOPS):
                tx, ty = _hop(xi, yi, hop)
                cp = pltpu.make_async_remote_copy(ins[i].at[both, 2 * tx + ty], outs[i].at[k], ssems.at[i, k],
                                                  rsems.at[i, k], device_id=(tx, ty, ci), device_id_type=MESH)
                cp.start()
                sends.append(cp)
        for cp in sends:
            cp.wait()

    return pl.pallas_call(
        body, name="grads_chip_exchange",
        out_shape=[jax.ShapeDtypeStruct((3, 2) + p.shape[2:], p.dtype) for p in ps],
        in_specs=_hbm_specs(n), out_specs=_hbm_specs(n),
        scratch_shapes=[pltpu.SemaphoreType.DMA((n, 3)), pltpu.SemaphoreType.DMA((n, 3))],
    )(*ps)


def _grads_pair_share(gs):
    n = len(gs)

    def body(*refs):
        outs = refs[n:2 * n]
        ssem, rsem = refs[2 * n:]
        xi, yi, ci = _my_place()
        lay, sib_lay = pl.ds(2 * ci, 2), pl.ds(2 * (1 - ci), 2)
        sends = []
        for i in range(n):
            cp = pltpu.make_async_remote_copy(outs[i].at[lay], outs[i].at[lay], ssem.at[i], rsem.at[i],
                                              device_id=(xi, yi, 1 - ci), device_id_type=MESH)
            cp.start()
            sends.append(cp)
        for i in range(n):
            pltpu.make_async_remote_copy(outs[i].at[sib_lay], outs[i].at[sib_lay], ssem.at[i], rsem.at[i],
                                         device_id=(xi, yi, 1 - ci), device_id_type=MESH).wait_recv()
        for cp in sends:
            cp.wait_send()

    return pl.pallas_call(
        body, name="grads_pair_share",
        out_shape=[jax.ShapeDtypeStruct(g.shape, g.dtype) for g in gs],
        in_specs=_hbm_specs(n), out_specs=_hbm_specs(n), input_output_aliases={i: i for i in range(n)},
        scratch_shapes=[pltpu.SemaphoreType.DMA((n,)), pltpu.SemaphoreType.DMA((n,))],
    )(*gs)


def _row_tile(r):
    return min(r, 256)


def _pair_sum(g, ga, place, name):
    _, _, r, c = g.shape
    tr = _row_tile(r)

    def body(p_ref, g_ref, ga_ref, o_ref):
        o_ref[...] = (g_ref[...].astype(F32) + ga_ref[...].astype(F32)).astype(o_ref.dtype)

    return pl.pallas_call(
        body, name=name, out_shape=jax.ShapeDtypeStruct(ga.shape, ga.dtype),
        grid_spec=pltpu.PrefetchScalarGridSpec(
            num_scalar_prefetch=1, grid=(2, N_CHIPS, r // tr),
            in_specs=[pl.BlockSpec((None, None, tr, c), lambda i, j, k, pr: (2 * pr[1] + i, j, k, 0)),
                      pl.BlockSpec((None, None, tr, c), lambda i, j, k, pr: (i, j, k, 0))],
            out_specs=pl.BlockSpec((None, None, tr, c), lambda i, j, k, pr: (i, j, k, 0))),
        compiler_params=_cparams(("parallel", "parallel", "parallel")),
    )(place, g, ga)


def _chip_sum(pair, recv, place, name):
    _, _, r, c = pair.shape
    tr = _row_tile(r)

    def body(p_ref, own_ref, r_ref, o_ref):
        acc = own_ref[...].astype(F32) + r_ref[0].astype(F32)
        acc = acc + r_ref[1].astype(F32)
        o_ref[...] = acc + r_ref[2].astype(F32)

    return pl.pallas_call(
        body, name=name, out_shape=jax.ShapeDtypeStruct((DEPTH, r, c), F32),
        grid_spec=pltpu.PrefetchScalarGridSpec(
            num_scalar_prefetch=1, grid=(2, r // tr),
            in_specs=[pl.BlockSpec((None, None, tr, c), lambda i, k, pr: (i, pr[0], k, 0)),
                      pl.BlockSpec((3, None, tr, c), lambda i, k, pr: (0, i, k, 0))],
            out_specs=pl.BlockSpec((None, tr, c), lambda i, k, pr: (2 * pr[1] + i, k, 0))),
        compiler_params=_cparams(("parallel", "parallel")),
    )(place, pair, recv)


def _adam_math(w, g, m, v):
    m = B1 * m + (1.0 - B1) * g
    v = B2 * v + (1.0 - B2) * (g * g)
    m_hat = m / (1.0 - B1 ** STEP)
    v_hat = v / (1.0 - B2 ** STEP)
    delta = -LR * (m_hat / (jnp.sqrt(v_hat) + ADAM_EPS) + WD * w)
    return delta, m, v


def _adamw(w, g, m, v, name):
    n_l, r, c = w.shape
    tr = _row_tile(r)

    def body(w_ref, g_ref, m_ref, v_ref, d_ref, nm_ref, nv_ref):
        d_ref[...], nm_ref[...], nv_ref[...] = _adam_math(w_ref[...], g_ref[...], m_ref[...], v_ref[...])

    spec = pl.BlockSpec((None, tr, c), lambda i, k: (i, k, 0))
    return pl.pallas_call(
        body, name=name, out_shape=[jax.ShapeDtypeStruct(w.shape, F32)] * 3, grid=(n_l, r // tr),
        in_specs=[spec] * 4, out_specs=[spec] * 3, compiler_params=_cparams(("parallel", "parallel")),
    )(w, g, m, v)


def _ada_forward(c_all, w_ada, b_cols):
    cols = w_ada.shape[2]
    tn = 512

    def body(c_ref, w_ref, b_ref, o_ref):
        o_ref[...] = _dotf(_silu(c_ref[...]), w_ref[...]) + b_ref[...]

    return pl.pallas_call(
        body, name="ada_forward", out_shape=jax.ShapeDtypeStruct((DEPTH, 8, cols), F32), grid=(DEPTH, cols // tn),
        in_specs=[pl.BlockSpec((8, D), lambda l, j: (0, 0)),
                  pl.BlockSpec((None, D, tn), lambda l, j: (l, 0, j)),
                  pl.BlockSpec((None, 1, tn), lambda l, j: (l, 0, j))],
        out_specs=pl.BlockSpec((None, 8, tn), lambda l, j: (l, 0, j)),
        compiler_params=_cparams(("parallel", "parallel")),
    )(c_all, w_ada, b_cols.reshape(DEPTH, 1, cols))


def _ada_backward_adamw(c_all, dmod_cols, w, m, v):
    cols = w.shape[2]
    tn = 512

    def body(c_ref, d_ref, w_ref, m_ref, v_ref, g_ref, dl_ref, nm_ref, nv_ref):
        g = _dotf_tn(_silu(c_ref[...]), d_ref[...])
        g_ref[...] = g
        dl_ref[...], nm_ref[...], nv_ref[...] = _adam_math(w_ref[...], g, m_ref[...], v_ref[...])

    wspec = pl.BlockSpec((None, D, tn), lambda l, j: (l, 0, j))
    return pl.pallas_call(
        body, name="ada_backward_adamw", out_shape=[jax.ShapeDtypeStruct(w.shape, F32)] * 4, grid=(DEPTH, cols // tn),
        in_specs=[pl.BlockSpec((8, D), lambda l, j: (0, 0)), pl.BlockSpec((None, 8, tn), lambda l, j: (l, 0, j)),
                  wspec, wspec, wspec],
        out_specs=[wspec] * 4, compiler_params=_cparams(("parallel", "parallel")),
    )(c_all, dmod_cols, w, m, v)


def _tok_tile(t):
    return min(t, 256)


def _wspec4(r, c, l):
    return pl.BlockSpec((N_CHIPS, None, r, c), lambda i: (0, l, 0, 0))


def _fwd_in(x, modv, w_in, l):
    t = x.shape[0]
    tm = _tok_tile(t)

    def body(x_ref, mod_ref, w_ref, o_ref):
        h = _norm_mod(x_ref[...], mod_ref[6:7, :], mod_ref[1:2, :], mod_ref[0:1, :])
        o_ref[...] = jnp.dot(h.astype(BF16), w_ref[...], preferred_element_type=F32)

    return pl.pallas_call(
        body, name=f"fwd_in_{l}", out_shape=jax.ShapeDtypeStruct((t, NW), F32), grid=(t // tm,),
        in_specs=[pl.BlockSpec((tm, D), lambda i: (i, 0)), pl.BlockSpec((None, 8, D), lambda i: (l, 0, 0)),
                  pl.BlockSpec((None, D, NW), lambda i: (l, 0, 0))],
        out_specs=pl.BlockSpec((tm, NW), lambda i: (i, 0)), compiler_params=_cparams(("parallel",)),
    )(x, modv, w_in)


def _fwd_out(x, mix, modv, w_out, l):
    t = x.shape[0]
    tm = _tok_tile(t)

    def body(x_ref, mix_ref, mod_ref, w_ref, o_ref):
        w = w_ref[...].reshape(D, D)
        o_ref[...] = x_ref[...] + mod_ref[2:3, :] * jnp.dot(mix_ref[...], w, preferred_element_type=F32)

    return pl.pallas_call(
        body, name=f"fwd_out_{l}", out_shape=jax.ShapeDtypeStruct((t, D), F32), grid=(t // tm,),
        in_specs=[pl.BlockSpec((tm, D), lambda i: (i, 0)), pl.BlockSpec((tm, D), lambda i: (i, 0)),
                  pl.BlockSpec((None, 8, D), lambda i: (l, 0, 0)), _wspec4(D // N_CHIPS, D, l)],
        out_specs=pl.BlockSpec((tm, D), lambda i: (i, 0)), compiler_params=_cparams(("parallel",)),
    )(x, mix, modv, w_out)


def _fwd_ff1(x, modv, w_ff1, l):
    t = x.shape[0]
    tm = _tok_tile(t)

    def body(x_ref, mod_ref, w_ref, o_ref):
        h = _norm_mod(x_ref[...], mod_ref[7:8, :], mod_ref[4:5, :], mod_ref[3:4, :]).astype(BF16)
        for j in range(N_CHIPS):
            f = jnp.dot(h, w_ref[j], preferred_element_type=F32)
            o_ref[:, j * D:(j + 1) * D] = jnp.maximum(f, 0.0).astype(BF16)

    return pl.pallas_call(
        body, name=f"fwd_ff1_{l}", out_shape=jax.ShapeDtypeStruct((t, DFF), BF16), grid=(t // tm,),
        in_specs=[pl.BlockSpec((tm, D), lambda i: (i, 0)), pl.BlockSpec((None, 8, D), lambda i: (l, 0, 0)),
                  _wspec4(D, D, l)],
        out_specs=pl.BlockSpec((tm, DFF), lambda i: (i, 0)), compiler_params=_cparams(("parallel",)),
    )(x, modv, w_ff1)


def _fwd_ff2(x, r, modv, w_ff2, l):
    t = x.shape[0]
    tm = _tok_tile(t)

    def body(x_ref, r_ref, mod_ref, w_ref, o_ref):
        acc = jnp.zeros((tm, D), F32)
        for j in range(N_CHIPS):
            rj = r_ref[:, j * D:(j + 1) * D].astype(F32)
            acc = acc + jnp.dot((rj * rj).astype(BF16), w_ref[j], preferred_element_type=F32)
        o_ref[...] = x_ref[...] + mod_ref[5:6, :] * acc

    return pl.pallas_call(
        body, name=f"fwd_ff2_{l}", out_shape=jax.ShapeDtypeStruct((t, D), F32), grid=(t // tm,),
        in_specs=[pl.BlockSpec((tm, D), lambda i: (i, 0)), pl.BlockSpec((tm, DFF), lambda i: (i, 0)),
                  pl.BlockSpec((None, 8, D), lambda i: (l, 0, 0)), _wspec4(D, D, l)],
        out_specs=pl.BlockSpec((tm, D), lambda i: (i, 0)), compiler_params=_cparams(("parallel",)),
    )(x, r, modv, w_ff2)


def _loss_head(x, target, final_g):
    t = x.shape[0]
    tm = _tok_tile(t)

    def body(x_ref, t_ref, g_ref, dx_ref, st_ref):
        @pl.when(pl.program_id(0) == 0)
        def _():
            st_ref[...] = jnp.zeros_like(st_ref)

        xh, rstd = _rms_stats(x_ref[...])
        g = g_ref[...]
        err = xh * g - t_ref[...]
        loss = 0.5 * jnp.sum(jnp.mean(err * err, axis=-1, keepdims=True), axis=0, keepdims=True)
        dy = err * (1.0 / D)
        st_ref[0:1, :] += jnp.sum(dy * xh, axis=0, keepdims=True)
        st_ref[1:2, :] += jnp.broadcast_to(loss, (1, D))
        dxh = dy * g
        dx_ref[...] = rstd * (dxh - xh * jnp.mean(dxh * xh, axis=-1, keepdims=True))

    return pl.pallas_call(
        body, name="loss_head", out_shape=[jax.ShapeDtypeStruct((t, D), F32), jax.ShapeDtypeStruct((8, D), F32)],
        grid=(t // tm,),
        in_specs=[pl.BlockSpec((tm, D), lambda i: (i, 0)), pl.BlockSpec((tm, D), lambda i: (i, 0)),
                  pl.BlockSpec((1, D), lambda i: (0, 0))],
        out_specs=[pl.BlockSpec((tm, D), lambda i: (i, 0)), pl.BlockSpec((8, D), lambda i: (0, 0))],
        compiler_params=_cparams(("arbitrary",)),
    )(x, target, final_g.reshape(1, D))


def _bwd_ff2(dx2, r, modv, w_ff2, l):
    t = dx2.shape[0]
    tm = _tok_tile(t)

    def body(d_ref, r_ref, mod_ref, w_ref, o_ref):
        dyg = (d_ref[...] * mod_ref[5:6, :]).astype(BF16)
        for j in range(N_CHIPS):
            da = lax.dot_general(dyg, w_ref[j], (((1,), (1,)), ((), ())), preferred_element_type=F32)
            o_ref[:, j * D:(j + 1) * D] = (da * 2.0 * r_ref[:, j * D:(j + 1) * D].astype(F32)).astype(BF16)

    return pl.pallas_call(
        body, name=f"bwd_ff2_{l}", out_shape=jax.ShapeDtypeStruct((t, DFF), BF16), grid=(t // tm,),
        in_specs=[pl.BlockSpec((tm, D), lambda i: (i, 0)), pl.BlockSpec((tm, DFF), lambda i: (i, 0)),
                  pl.BlockSpec((None, 8, D), lambda i: (l, 0, 0)), _wspec4(D, D, l)],
        out_specs=pl.BlockSpec((tm, DFF), lambda i: (i, 0)), compiler_params=_cparams(("parallel",)),
    )(dx2, r, modv, w_ff2)


def _bwd_out(dx1, modv, w_out, l):
    t = dx1.shape[0]
    tm = _tok_tile(t)

    def body(d_ref, mod_ref, w_ref, o_ref):
        dyg = (d_ref[...] * mod_ref[2:3, :]).astype(BF16)
        w = w_ref[...].reshape(D, D)
        o_ref[...] = lax.dot_general(dyg, w, (((1,), (1,)), ((), ())), preferred_element_type=F32).astype(BF16)

    return pl.pallas_call(
        body, name=f"bwd_out_{l}", out_shape=jax.ShapeDtypeStruct((t, D), BF16), grid=(t // tm,),
        in_specs=[pl.BlockSpec((tm, D), lambda i: (i, 0)), pl.BlockSpec((None, 8, D), lambda i: (l, 0, 0)),
                  _wspec4(D // N_CHIPS, D, l)],
        out_specs=pl.BlockSpec((tm, D), lambda i: (i, 0)), compiler_params=_cparams(("parallel",)),
    )(dx1, modv, w_out)


def _bwd_norm(dy, w, x, dres, modv, l, which):
    t = x.shape[0]
    tm = _tok_tile(t)
    rows = (6, 1) if which == "in" else (7, 4)
    width = dy.shape[1]

    def body(dy_ref, w_ref, x_ref, dr_ref, mod_ref, dx_ref, st_ref):
        @pl.when(pl.program_id(0) == 0)
        def _():
            st_ref[...] = jnp.zeros_like(st_ref)

        if which == "in":
            dh = lax.dot_general(dy_ref[...], w_ref[...], (((1,), (1,)), ((), ())), preferred_element_type=F32)
        else:
            dh = jnp.zeros((tm, D), F32)
            for j in range(N_CHIPS):
                dh = dh + lax.dot_general(dy_ref[:, j * D:(j + 1) * D], w_ref[j], (((1,), (1,)), ((), ())),
                                          preferred_element_type=F32)
        ng, sc = mod_ref[rows[0]:rows[0] + 1, :], mod_ref[rows[1]:rows[1] + 1, :]
        dx, dsh, dsc, dng = _norm_mod_bwd(dh, x_ref[...], ng, sc)
        dx_ref[...] = dr_ref[...] + dx
        st_ref[0:1, :] += dsh
        st_ref[1:2, :] += dsc
        st_ref[2:3, :] += dng

    wspec = pl.BlockSpec((None, D, NW), lambda i: (l, 0, 0)) if which == "in" else _wspec4(D, D, l)
    return pl.pallas_call(
        body, name=f"bwd_norm_{which}_{l}",
        out_shape=[jax.ShapeDtypeStruct((t, D), F32), jax.ShapeDtypeStruct((8, D), F32)], grid=(t // tm,),
        in_specs=[pl.BlockSpec((tm, width), lambda i: (i, 0)), wspec, pl.BlockSpec((tm, D), lambda i: (i, 0)),
                  pl.BlockSpec((tm, D), lambda i: (i, 0)), pl.BlockSpec((None, 8, D), lambda i: (l, 0, 0))],
        out_specs=[pl.BlockSpec((tm, D), lambda i: (i, 0)), pl.BlockSpec((8, D), lambda i: (0, 0))],
        compiler_params=_cparams(("arbitrary",)),
    )(dy, w, x, dres, modv)


def _grad_weight(lhs, rhs, buf, modv, l, which, w_gate=None):
    t = lhs.shape[0]
    tm = min(t, 512)
    nt = t // tm
    gated = which in ("out", "ff2")
    if which == "in":
        nj, lw, rw, orows, ocols = 5, D, NW // 5, D, NW // 5
    elif which == "ff1":
        nj, lw, rw, orows, ocols = N_CHIPS, D, D, D, D
    elif which == "out":
        nj, lw, rw, orows, ocols = N_CHIPS, D // N_CHIPS, D, D // N_CHIPS, D
    else:
        nj, lw, rw, orows, ocols = N_CHIPS, D, D, D, D
    gate_row = 2 if which == "out" else 5

    def body(*refs):
        if gated:
            l_ref, r_ref, mod_ref, wg_ref, _, o_ref, dg_ref, acc = refs
        else:
            l_ref, r_ref, mod_ref, _, o_ref, acc = refs
        j, k = pl.program_id(0), pl.program_id(1)

        @pl.when(k == 0)
        def _():
            acc[...] = jnp.zeros_like(acc)

        if which == "in":
            lv = _norm_mod(l_ref[...], mod_ref[6:7, :], mod_ref[1:2, :], mod_ref[0:1, :])
        elif which == "ff1":
            lv = _norm_mod(l_ref[...], mod_ref[7:8, :], mod_ref[4:5, :], mod_ref[3:4, :])
        elif which == "ff2":
            lv = l_ref[...].astype(F32)
            lv = lv * lv
        else:
            lv = l_ref[...]
        acc[...] += _dot_tn(lv, r_ref[...])

        if gated:
            @pl.when(jnp.logical_and(j == 0, k == 0))
            def _():
                dg_ref[...] = jnp.zeros_like(dg_ref)

        @pl.when(k == nt - 1)
        def _():
            raw = acc[...]
            if gated:
                o_ref[...] = (raw * mod_ref[gate_row:gate_row + 1, :]).astype(o_ref.dtype)
                dg_ref[0:1, :] += jnp.sum(raw * wg_ref[...].astype(F32), axis=0, keepdims=True)
            else:
                o_ref[...] = raw.astype(o_ref.dtype)

    if which in ("in", "ff1"):
        lspec = pl.BlockSpec((tm, lw), lambda j, k: (k, 0))
        rspec = pl.BlockSpec((tm, rw), lambda j, k: (k, j))
    else:
        lspec = pl.BlockSpec((tm, lw), lambda j, k: (k, j))
        rspec = pl.BlockSpec((tm, rw), lambda j, k: (k, 0))
    mspec = pl.BlockSpec((None, 8, D), lambda j, k: (l, 0, 0))
    if which == "in":
        ospec = pl.BlockSpec((None, orows, ocols), lambda j, k: (l, 0, j))
    else:
        ospec = pl.BlockSpec((None, None, orows, ocols), lambda j, k: (l, j, 0, 0))
    in_specs = [lspec, rspec, mspec]
    args = [lhs, rhs, modv]
    out_specs = [ospec]
    out_shape = [jax.ShapeDtypeStruct(buf.shape, buf.dtype)]
    if gated:
        in_specs.append(pl.BlockSpec((None, None, orows, ocols), lambda j, k: (j, l, 0, 0)))
        args.append(w_gate)
        out_specs.append(pl.BlockSpec((8, D), lambda j, k: (0, 0)))
        out_shape.append(jax.ShapeDtypeStruct((8, D), F32))
    in_specs.append(pl.BlockSpec(memory_space=pl.ANY))
    args.append(buf)
    res = pl.pallas_call(
        body, name=f"grad_w_{which}_{l}", out_shape=out_shape, grid=(nj, nt), in_specs=in_specs, out_specs=out_specs,
        scratch_shapes=[pltpu.VMEM((orows, ocols), F32)], input_output_aliases={len(args) - 1: 0},
        compiler_params=_cparams(("arbitrary", "arbitrary")),
    )(*args)
    return (res[0], res[1]) if gated else (res[0], None)


def _tri_masks():
    rows, cols = _iota2((BLK, BLK), 0), _iota2((BLK, BLK), 1)
    return rows >= cols, rows > cols


def _sgu_forward(p_ref, lnp_ref, sguw_ref, sgub_ref):
    incl, _ = _tri_masks()
    ug = _gelu(p_ref[:, 0:512])
    vg = _gelu(p_ref[:, 512:1024])
    mu = jnp.mean(vg, axis=-1, keepdims=True)
    xc = vg - mu
    rstd = lax.rsqrt(jnp.mean(xc * xc, axis=-1, keepdims=True) + LN_EPS)
    vhat = xc * rstd
    vn = vhat * lnp_ref[0:1, :] + lnp_ref[1:2, :]
    bias = sgub_ref[...]
    ys, mixed, wms = [], [], []
    for h in range(HEADS):
        wm = jnp.where(incl, sguw_ref[h], 0.0)
        mx = _dot(wm, vn[:, h * HD:(h + 1) * HD]) + _col(bias, h)
        ys.append(ug[:, h * HD:(h + 1) * HD] * mx)
        mixed.append(mx)
        wms.append(wm)
    return ys, ug, vhat, rstd, vn, mixed, wms


def _conv_forward(xbuf, cw_ref):
    conv = cw_ref[0:1, :] * xbuf[5:5 + BLK, :]
    for j in range(1, 4):
        conv = conv + cw_ref[j:j + 1, :] * xbuf[5 + j:5 + j + BLK, :]
    return conv


def _gates(gt, gv_ref):
    incl, _ = _tri_masks()
    beta = _sigmoid(gt)
    neg_a = -jnp.exp(gv_ref[0:1, :])
    gl = neg_a * _softplus(gt + gv_ref[1:2, :])
    gc = _dotf(jnp.where(incl, 1.0, 0.0).astype(F32), gl)
    return beta, gl, gc, gc.T, neg_a


def _head_chunk(act, beta, gc, gct, h):
    incl, strict = _tri_masks()
    qh = act[:, h * HD:(h + 1) * HD]
    kh = act[:, 512 + h * HD:512 + (h + 1) * HD]
    vh = act[:, 1024 + h * HD:1024 + (h + 1) * HD]
    rq = lax.rsqrt(jnp.sum(qh * qh, axis=-1, keepdims=True) + RMS_EPS)
    rk = lax.rsqrt(jnp.sum(kh * kh, axis=-1, keepdims=True) + RMS_EPS)
    qhat, khat = qh * rq, kh * rk
    qn = qhat * QK_SCALE
    b = _col(beta, h)
    gcol = _col(gc, 4 + h)
    grow = _row(gct, 4 + h)
    dmat = jnp.where(incl, jnp.exp(jnp.where(incl, gcol - grow, 0.0)), 0.0)
    gam = jnp.exp(gcol)
    glast = _row(gcol, BLK - 1)
    e = jnp.exp(glast - gcol)
    kk = _d3_nt(khat, khat)
    return dict(qhat=qhat, khat=khat, qn=qn, vh=vh, rq=rq, rk=rk, b=b, dmat=dmat, gam=gam, glast=glast, e=e, kk=kk,
                strict=strict, incl=incl)


def _mixer_forward(p, lnp, sgu_w, sgu_bt, cw, gv, l):
    t = p.shape[0]
    nb = t // BLK

    def body(p_ref, lnp_ref, sguw_ref, sgub_ref, cw_ref, gv_ref,
             mix_ref, s_out, t_out, u_out, w_out, o_out, tail_out, s_scr, xbuf):
        @pl.when(pl.program_id(0) == 0)
        def _():
            s_scr[...] = jnp.zeros_like(s_scr)
            xbuf[0:8, :] = jnp.zeros((8, 1536), F32)

        ys = _sgu_forward(p_ref, lnp_ref, sguw_ref, sgub_ref)[0]
        for h in range(HEADS):
            mix_ref[:, h * HD:(h + 1) * HD] = ys[h].astype(BF16)

        tail_out[...] = xbuf[0:8, :]
        xbuf[8:8 + BLK, :] = p_ref[:, 1024:2560]
        act = _silu(_conv_forward(xbuf, cw_ref))
        xbuf[0:8, :] = xbuf[BLK:BLK + 8, :]
        beta, _, gc, gct, _ = _gates(p_ref[:, GATE0:NW], gv_ref)
        chunks = [_head_chunk(act, beta, gc, gct, h) for h in range(HEADS)]
        for h, hc in enumerate(chunks):
            t_out[h] = jnp.where(hc["strict"], hc["b"] * hc["kk"] * hc["dmat"], 0.0)
        t_out[...] = _tri_inverse(t_out[...])
        for h, hc in enumerate(chunks):
            tm = t_out[h]
            u = _d3(tm, hc["b"] * hc["vh"])
            w = _d3(tm, (hc["b"] * hc["gam"]) * hc["khat"])
            qkm = _d3_nt(hc["qn"], hc["khat"]) * hc["dmat"]
            s = s_scr[h]
            wn = u - _d3(w, s)
            o = _d3(hc["qn"] * hc["gam"], s) + _d3(qkm, wn)
            s_out[h] = s
            s_scr[h] = jnp.exp(hc["glast"]) * s + _d3_tn(hc["khat"] * hc["e"], wn)
            sl = slice(h * HD, (h + 1) * HD)
            u_out[:, sl] = u
            w_out[:, sl] = w
            o_out[:, sl] = o
            on = o * lax.rsqrt(jnp.mean(o * o, axis=-1, keepdims=True) + RMS_EPS) * gv_ref[2:3, :]
            mix_ref[:, 512 + h * HD:512 + (h + 1) * HD] = (on * _silu(p_ref[:, 2560 + h * HD:2560 + (h + 1) * HD])).astype(BF16)

    tok = lambda w: pl.BlockSpec((BLK, w), lambda i: (i, 0))
    st = pl.BlockSpec((None, HEADS, HD, HD), lambda i: (i, 0, 0, 0))
    return pl.pallas_call(
        body, name=f"mixer_fwd_{l}", grid=(nb,),
        out_shape=[jax.ShapeDtypeStruct((t, D), BF16), jax.ShapeDtypeStruct((nb, HEADS, HD, HD), F32),
                   jax.ShapeDtypeStruct((nb, HEADS, HD, HD), F32), jax.ShapeDtypeStruct((t, 512), F32),
                   jax.ShapeDtypeStruct((t, 512), F32), jax.ShapeDtypeStruct((t, 512), F32),
                   jax.ShapeDtypeStruct((nb, 8, 1536), F32)],
        in_specs=[tok(NW), pl.BlockSpec((None, 8, 512), lambda i: (l, 0, 0)),
                  pl.BlockSpec((None, HEADS, HD, HD), lambda i: (l, 0, 0, 0)),
                  pl.BlockSpec((None, HD, HD), lambda i: (l, 0, 0)), pl.BlockSpec((None, 8, 1536), lambda i: (l, 0, 0)),
                  pl.BlockSpec((None, 8, HD), lambda i: (l, 0, 0))],
        out_specs=[tok(D), st, st, tok(512), tok(512), tok(512), pl.BlockSpec((None, 8, 1536), lambda i: (i, 0, 0))],
        scratch_shapes=[pltpu.VMEM((HEADS, HD, HD), F32), pltpu.VMEM((BLK + 8, 1536), F32)],
        compiler_params=_cparams(("arbitrary",)),
    )(p, lnp, sgu_w, sgu_bt, cw, gv)


def _mixer_backward(p, dmix, saved, lnp, sgu_w, sgu_bt, cw, gv, l):
    t = p.shape[0]
    nb = t // BLK
    s_sv, t_sv, u_sv, w_sv, o_sv, tail_sv = saved

    def body(p_ref, dmix_ref, s_ref, t_ref, u_ref, w_ref, o_ref, tail_ref, lnp_ref, sguw_ref, sgub_ref, cw_ref, gv_ref,
             dp_ref, dlnp_ref, dsguw_ref, dsgub_ref, dcw_ref, dgv_ref, ds_scr, xbuf, dcbuf):
        @pl.when(pl.program_id(0) == 0)
        def _():
            ds_scr[...] = jnp.zeros_like(ds_scr)
            dcbuf[BLK:BLK + 8, :] = jnp.zeros((8, 1536), F32)
            dlnp_ref[...] = jnp.zeros_like(dlnp_ref)
            dsguw_ref[...] = jnp.zeros_like(dsguw_ref)
            dsgub_ref[...] = jnp.zeros_like(dsgub_ref)
            dcw_ref[...] = jnp.zeros_like(dcw_ref)
            dgv_ref[...] = jnp.zeros_like(dgv_ref)

        incl, strict = _tri_masks()
        _, ug, vhat, rstd, vn, mixed, wms = _sgu_forward(p_ref, lnp_ref, sguw_ref, sgub_ref)
        dvn_parts, dug_parts = [], []
        dbias = jnp.zeros((BLK, HD), F32)
        for h in range(HEADS):
            sl = slice(h * HD, (h + 1) * HD)
            dy = dmix_ref[:, sl].astype(F32)
            dmx = dy * ug[:, sl]
            dug_parts.append(dy * mixed[h])
            dsguw_ref[h] += jnp.where(incl, _dot_nt(dmx, vn[:, sl]), 0.0)
            dbias = dbias + _put_col(jnp.sum(dmx, axis=1, keepdims=True), h)
            dvn_parts.append(_dot_tn(wms[h], dmx))
        dsgub_ref[...] += dbias
        dvn = jnp.concatenate(dvn_parts, axis=1)
        dug = jnp.concatenate(dug_parts, axis=1)
        dlnp_ref[0:1, :] += jnp.sum(dvn * vhat, axis=0, keepdims=True)
        dlnp_ref[1:2, :] += jnp.sum(dvn, axis=0, keepdims=True)
        dvhat = dvn * lnp_ref[0:1, :]
        dvg = rstd * (dvhat - jnp.mean(dvhat, axis=-1, keepdims=True)
                      - vhat * jnp.mean(dvhat * vhat, axis=-1, keepdims=True))
        dp_ref[:, 0:512] = (dug * _gelu_grad(p_ref[:, 0:512])).astype(BF16)
        dp_ref[:, 512:1024] = (dvg * _gelu_grad(p_ref[:, 512:1024])).astype(BF16)

        xbuf[0:8, :] = tail_ref[...]
        xbuf[8:8 + BLK, :] = p_ref[:, 1024:2560]
        conv = _conv_forward(xbuf, cw_ref)
        act = _silu(conv)
        gt = p_ref[:, GATE0:NW]
        beta, gl, gc, gct, neg_a = _gates(gt, gv_ref)
        gng = gv_ref[2:3, :]
        dbeta_t = jnp.zeros((BLK, HD), F32)
        dgc_t = jnp.zeros((BLK, HD), F32)
        dgng = jnp.zeros((1, HD), F32)
        for h in range(HEADS):
            sl = slice(h * HD, (h + 1) * HD)
            hc = _head_chunk(act, beta, gc, gct, h)
            b, gam, e, dmat, kk = hc["b"], hc["gam"], hc["e"], hc["dmat"], hc["kk"]
            qn, khat, vh = hc["qn"], hc["khat"], hc["vh"]
            gamlast = jnp.exp(hc["glast"])
            s, tm, u, w, o = s_ref[h], t_ref[h], u_ref[:, sl], w_ref[:, sl], o_ref[:, sl]
            ds_next = ds_scr[h]
            z = p_ref[:, 2560 + h * HD:2560 + (h + 1) * HD]
            dy = dmix_ref[:, 512 + h * HD:512 + (h + 1) * HD].astype(F32)
            ro = lax.rsqrt(jnp.mean(o * o, axis=-1, keepdims=True) + RMS_EPS)
            ohat = o * ro
            dp_ref[:, 2560 + h * HD:2560 + (h + 1) * HD] = (dy * ohat * gng * _silu_grad(z)).astype(BF16)
            don = dy * _silu(z)
            dgng = dgng + jnp.sum(don * ohat, axis=0, keepdims=True)
            dohat = don * gng
            do = ro * (dohat - ohat * jnp.mean(dohat * ohat, axis=-1, keepdims=True))
            qk_raw = _d3_nt(qn, khat)
            qkm = qk_raw * dmat
            qd, kd = qn * gam, khat * e
            wn = u - _d3(w, s)
            dwn = _d3_tn(qkm, do) + _d3(kd, ds_next)
            dqd = _d3_nt(do, s)
            dqkm = jnp.where(incl, _d3_nt(do, wn), 0.0)
            ds_scr[h] = _d3_tn(qd, do) + gamlast * ds_next - _d3_tn(w, dwn)
            dgamlast = jnp.sum(jnp.sum(ds_next * s, axis=1, keepdims=True), axis=0, keepdims=True)
            dkd = _d3_nt(wn, ds_next)
            dw = -_d3_nt(dwn, s)
            db1 = _d3_tn(tm, dwn)
            db2 = _d3_tn(tm, dw)
            dm = jnp.where(strict, -(_d3_nt(db1, u) + _d3_nt(db2, w)), 0.0)
            dbeta = (jnp.sum(dm * kk * dmat, axis=1, keepdims=True) + jnp.sum(db1 * vh, axis=1, keepdims=True)
                     + gam * jnp.sum(db2 * khat, axis=1, keepdims=True))
            dkkm = dm * b * dmat
            ddm = dm * b * kk + dqkm * qk_raw
            dgam = b * jnp.sum(db2 * khat, axis=1, keepdims=True) + jnp.sum(dqd * qn, axis=1, keepdims=True)
            g_qk = dqkm * dmat
            dqn = _d3(g_qk, khat) + dqd * gam
            dkhat = ((b * gam) * db2 + _d3_tn(g_qk, qn) + _d3(dkkm, khat) + _d3_tn(dkkm, khat) + dkd * e)
            dvh = b * db1
            rkd = jnp.sum(dkd * kd, axis=1, keepdims=True)
            emat = ddm * dmat
            dgc = (dgam * gam - rkd + jnp.sum(emat, axis=1, keepdims=True)
                   - jnp.sum(emat.T, axis=1, keepdims=True))
            last = _iota2((BLK, 1), 0) == BLK - 1
            dgc = dgc + jnp.where(last, jnp.sum(rkd, axis=0, keepdims=True) + dgamlast * gamlast, 0.0)
            dgc_t = dgc_t + _put_col(dgc, 4 + h)
            dbeta_t = dbeta_t + _put_col(dbeta, h)
            dqhat = dqn * QK_SCALE
            dq = hc["rq"] * (dqhat - hc["qhat"] * jnp.sum(dqhat * hc["qhat"], axis=-1, keepdims=True))
            dk = hc["rk"] * (dkhat - khat * jnp.sum(dkhat * khat, axis=-1, keepdims=True))
            dcbuf[0:BLK, h * HD:(h + 1) * HD] = dq
            dcbuf[0:BLK, 512 + h * HD:512 + (h + 1) * HD] = dk
            dcbuf[0:BLK, 1024 + h * HD:1024 + (h + 1) * HD] = dvh
        dgv_ref[2:3, :] += dgng
        dgl = _dotf_tn(jnp.where(incl, 1.0, 0.0).astype(F32), dgc_t)
        sig_a = _sigmoid(gt + gv_ref[1:2, :])
        d_araw = dgl * neg_a * sig_a
        dgv_ref[0:1, :] += jnp.sum(dgl * gl, axis=0, keepdims=True)
        dgv_ref[1:2, :] += jnp.sum(d_araw, axis=0, keepdims=True)
        dp_ref[:, GATE0:NW] = (dbeta_t * beta * (1.0 - beta) + d_araw).astype(BF16)
        dcbuf[0:BLK, :] = dcbuf[0:BLK, :] * _silu_grad(conv)
        dqkv = cw_ref[0:1, :] * dcbuf[3:3 + BLK, :]
        dcw_ref[0:1, :] += jnp.sum(dcbuf[0:BLK, :] * xbuf[5:5 + BLK, :], axis=0, keepdims=True)
        for j in range(1, 4):
            dqkv = dqkv + cw_ref[j:j + 1, :] * dcbuf[3 - j:3 - j + BLK, :]
            dcw_ref[j:j + 1, :] += jnp.sum(dcbuf[0:BLK, :] * xbuf[5 + j:5 + j + BLK, :], axis=0, keepdims=True)
        dp_ref[:, 1024:2560] = dqkv.astype(BF16)
        dcbuf[BLK:BLK + 8, :] = dcbuf[0:8, :]

    rev = lambda w: pl.BlockSpec((BLK, w), lambda i: (nb - 1 - i, 0))
    st = pl.BlockSpec((None, HEADS, HD, HD), lambda i: (nb - 1 - i, 0, 0, 0))
    fix = lambda *shape: pl.BlockSpec((None,) + shape, lambda i: (l,) + (0,) * len(shape))
    acc = lambda *shape: pl.BlockSpec(shape, lambda i: (0,) * len(shape))
    return pl.pallas_call(
        body, name=f"mixer_bwd_{l}", grid=(nb,),
        out_shape=[jax.ShapeDtypeStruct((t, NW), BF16), jax.ShapeDtypeStruct((8, 512), F32),
                   jax.ShapeDtypeStruct((HEADS, HD, HD), F32), jax.ShapeDtypeStruct((HD, HD), F32),
                   jax.ShapeDtypeStruct((8, 1536), F32), jax.ShapeDtypeStruct((8, HD), F32)],
        in_specs=[rev(NW), rev(D), st, st, rev(512), rev(512), rev(512),
                  pl.BlockSpec((None, 8, 1536), lambda i: (nb - 1 - i, 0, 0)),
                  fix(8, 512), fix(HEADS, HD, HD), fix(HD, HD), fix(8, 1536), fix(8, HD)],
        out_specs=[rev(NW), acc(8, 512), acc(HEADS, HD, HD), acc(HD, HD), acc(8, 1536), acc(8, HD)],
        scratch_shapes=[pltpu.VMEM((HEADS, HD, HD), F32), pltpu.VMEM((BLK + 8, 1536), F32),
                        pltpu.VMEM((BLK + 8, 1536), F32)],
        compiler_params=_cparams(("arbitrary",)),
    )(p, dmix, s_sv, t_sv, u_sv, w_sv, o_sv, tail_sv, lnp, sgu_w, sgu_bt, cw, gv)


_SMALL = (("b_ada", 24), ("norm1_g", 4), ("norm2_g", 4), ("final_g", 1), ("sgu_ln_g", 2), ("sgu_ln_b", 2),
          ("sgu_w", 256), ("sgu_b", 2), ("conv_w", 24), ("a_log", 1), ("dt_bias", 1), ("gdn_norm_g", 1))
_SMALL_ROWS = sum(n for _, n in _SMALL)
_SMALL_PAD = 328
_DMOD_ROWS = 24


def _pack_rows(parts):
    rows = []
    for (name, n), a in zip(_SMALL, parts):
        flat = a.reshape(-1).astype(F32)
        rows.append(jnp.pad(flat, (0, n * D - flat.shape[0])).reshape(n, D))
    rows.append(jnp.zeros((_SMALL_PAD - _SMALL_ROWS, D), F32))
    return jnp.concatenate(rows, axis=0)


def _unpack_rows(buf, shapes):
    out, r0 = {}, 0
    for name, n in _SMALL:
        size = math.prod(shapes[name])
        out[name] = buf[r0:r0 + n].reshape(-1)[:size].reshape(shapes[name])
        r0 += n
    return out


def _pair_combine(own, sib):
    n = own.shape[0] - _DMOD_ROWS

    def body(a_ref, b_ref, o_ref):
        first = lax.axis_index("c") == 0
        a, b = a_ref[0:_DMOD_ROWS, :], b_ref[0:_DMOD_ROWS, :]
        o_ref[0:_DMOD_ROWS, :] = jnp.where(first, a, b)
        o_ref[_DMOD_ROWS:2 * _DMOD_ROWS, :] = jnp.where(first, b, a)
        o_ref[2 * _DMOD_ROWS:, :] = a_ref[_DMOD_ROWS:, :] + b_ref[_DMOD_ROWS:, :]

    return pl.pallas_call(
        body, name="small_pair_combine", out_shape=jax.ShapeDtypeStruct((2 * _DMOD_ROWS + n, D), F32),
        compiler_params=pltpu.CompilerParams(vmem_limit_bytes=VMEM_LIMIT),
    )(own, sib)


def _small_finalize(gathered, w, m, v):
    def body(g_ref, w_ref, m_ref, v_ref, go_ref, d_ref, nm_ref, nv_ref):
        sm = g_ref[0, 2 * _DMOD_ROWS:, :] + g_ref[1, 2 * _DMOD_ROWS:, :]
        sm = sm + g_ref[2, 2 * _DMOD_ROWS:, :]
        sm = sm + g_ref[3, 2 * _DMOD_ROWS:, :]
        bsum = jnp.zeros((_DMOD_ROWS, D), F32)
        for j in range(N_CHIPS):
            bsum = bsum + g_ref[j, 0:_DMOD_ROWS, :]
            bsum = bsum + g_ref[j, _DMOD_ROWS:2 * _DMOD_ROWS, :]
        go_ref[0:_DMOD_ROWS, :] = bsum
        go_ref[_DMOD_ROWS:, :] = sm[_DMOD_ROWS:, :]
        d_ref[...], nm_ref[...], nv_ref[...] = _adam_math(w_ref[...], go_ref[...], m_ref[...], v_ref[...])

    return pl.pallas_call(
        body, name="small_finalize", out_shape=[jax.ShapeDtypeStruct(w.shape, F32)] * 4,
        compiler_params=pltpu.CompilerParams(vmem_limit_bytes=VMEM_LIMIT),
    )(gathered, w, m, v)


def kernel(x, c, w_ada, b_ada, norm1_g, w_in, sgu_ln_g, sgu_ln_b, sgu_w, sgu_b, conv_w, a_log, dt_bias, gdn_norm_g, w_out, norm2_g, w_ff1, w_ff2, final_g, loss_target, m_w_ada, m_b_ada, m_norm1_g, m_w_in, m_sgu_ln_g, m_sgu_ln_b, m_sgu_w, m_sgu_b, m_conv_w, m_a_log, m_dt_bias, m_gdn_norm_g, m_w_out, m_norm2_g, m_w_ff1, m_w_ff2, m_final_g, v_w_ada, v_b_ada, v_norm1_g, v_w_in, v_sgu_ln_g, v_sgu_ln_b, v_sgu_w, v_sgu_b, v_conv_w, v_a_log, v_dt_bias, v_gdn_norm_g, v_w_out, v_norm2_g, v_w_ff1, v_w_ff2, v_final_g):
    xi, yi, ci = lax.axis_index("x"), lax.axis_index("y"), lax.axis_index("c")
    chip = 2 * xi + yi
    dev = 2 * chip + ci
    t = x.shape[1]
    x0 = x.reshape(t, D)
    target = loss_target.reshape(t, D)

    c_sib = _pair_exchange(c, "c_pair")
    c_pair = jnp.where(ci == 0, jnp.concatenate([c, c_sib], 0), jnp.concatenate([c_sib, c], 0))
    c_all = _chip_allgather(c_pair, "c_chips").reshape(8, D)
    ada_cols = w_ada.shape[2]
    b_cols = lax.dynamic_slice_in_dim(b_ada, chip * ada_cols, ada_cols, axis=1)
    mod_part = _ada_forward(c_all, w_ada, b_cols)
    conv_cols = conv_w.shape[2]
    packed = jnp.concatenate([mod_part.reshape(DEPTH * 8, ada_cols), conv_w.reshape(DEPTH, 4 * conv_cols)], axis=0)
    packed = _chip_allgather(packed, "mod_chips")
    mod_all = packed[:, :DEPTH * 8].reshape(N_CHIPS, DEPTH, 8, ada_cols)
    mod_mine = lax.dynamic_index_in_dim(mod_all, dev, axis=2, keepdims=False)
    mod = mod_mine.transpose(1, 0, 2).reshape(DEPTH, 6, D)
    modv = jnp.concatenate([mod, norm1_g[:, None, :], norm2_g[:, None, :]], axis=1)
    conv_full = packed[:, DEPTH * 8:].reshape(N_CHIPS, DEPTH, 4, conv_cols).transpose(1, 2, 0, 3).reshape(DEPTH, 4, 1536)

    place = jnp.stack([chip, ci]).astype(jnp.int32)
    g_in, g_out, g_ff1, g_ff2 = _weights_allgather([_cast_into_slot(w, place) for w in (w_in, w_out, w_ff1, w_ff2)])
    win = g_in.transpose(1, 2, 0, 3).reshape(DEPTH, D, IN_W)
    win = jnp.pad(win, ((0, 0), (0, 0), (0, NW - IN_W)))

    lnp = jnp.pad(jnp.stack([sgu_ln_g, sgu_ln_b], axis=1), ((0, 0), (0, 6), (0, 0)))
    sgu_bt = jnp.pad(sgu_b.transpose(0, 2, 1), ((0, 0), (0, 0), (0, HD - HEADS)))
    cw = jnp.pad(conv_full, ((0, 0), (0, 4), (0, 0)))
    lane_pad = lambda a: jnp.pad(a, ((0, 0), (4, HD - 8)))
    gv = jnp.pad(jnp.stack([lane_pad(a_log), lane_pad(dt_bias), gdn_norm_g], axis=1), ((0, 0), (0, 5), (0, 0)))

    acts = []
    xl = x0
    for l in range(DEPTH):
        p = _fwd_in(xl, modv, win, l)
        mix, *saved = _mixer_forward(p, lnp, sgu_w, sgu_bt, cw, gv, l)
        x1 = _fwd_out(xl, mix, modv, g_out, l)
        r = _fwd_ff1(x1, modv, g_ff1, l)
        x2 = _fwd_ff2(x1, r, modv, g_ff2, l)
        acts.append((xl, p, mix, saved, x1, r))
        xl = x2

    dx, head_stats = _loss_head(xl, target, final_g)
    loss = lax.psum(jnp.sum(head_stats[1, 0:1]), ("x", "y", "c"))
    d_final_g = head_stats[0]
    gb_in = jnp.zeros((DEPTH, D, NW), BF16)
    gb_out = jnp.zeros((DEPTH, N_CHIPS, D // N_CHIPS, D), BF16)
    gb_ff1 = jnp.zeros((DEPTH, N_CHIPS, D, D), BF16)
    gb_ff2 = jnp.zeros((DEPTH, N_CHIPS, D, D), BF16)
    dmod, small = [None] * DEPTH, [None] * DEPTH
    for l in reversed(range(DEPTH)):
        xl, p, mix, saved, x1, r = acts[l]
        df = _bwd_ff2(dx, r, modv, g_ff2, l)
        gb_ff2, dg2 = _grad_weight(r, dx, gb_ff2, modv, l, "ff2", g_ff2)
        gb_ff1, _ = _grad_weight(x1, df, gb_ff1, modv, l, "ff1")
        dx1, st2 = _bwd_norm(df, g_ff1, x1, dx, modv, l, "ff1")
        dmix = _bwd_out(dx1, modv, g_out, l)
        gb_out, dg1 = _grad_weight(mix, dx1, gb_out, modv, l, "out", g_out)
        dp, dlnp, dsguw, dsgub, dcw, dgv = _mixer_backward(p, dmix, saved, lnp, sgu_w, sgu_bt, cw, gv, l)
        gb_in, _ = _grad_weight(xl, dp, gb_in, modv, l, "in")
        dx, st1 = _bwd_norm(dp, win, xl, dx1, modv, l, "in")
        dmod[l] = jnp.stack([st1[0], st1[1], dg1[0], st2[0], st2[1], dg2[0]], axis=0)
        small[l] = dict(norm1_g=st1[2], norm2_g=st2[2], sgu_ln_g=dlnp[0], sgu_ln_b=dlnp[1], sgu_w=dsguw,
                        sgu_b=dsgub[:, :HEADS].T, conv_w=dcw[:4], a_log=dgv[0, 4:8], dt_bias=dgv[1, 4:8],
                        gdn_norm_g=dgv[2])
    grad_x = dx.reshape(1, t, D)

    stack = lambda k: jnp.stack([small[l][k] for l in range(DEPTH)], axis=0)
    small_grads = [jnp.zeros((DEPTH, 6 * D), F32), stack("norm1_g"), stack("norm2_g"), d_final_g, stack("sgu_ln_g"),
                   stack("sgu_ln_b"), stack("sgu_w"), stack("sgu_b"), stack("conv_w"), stack("a_log"),
                   stack("dt_bias"), stack("gdn_norm_g")]
    own = jnp.concatenate([jnp.stack(dmod, axis=0).reshape(_DMOD_ROWS, D), _pack_rows(small_grads)], axis=0)
    sib = _pair_exchange(own, "small_pair")
    gathered = _chip_allgather(_pair_combine(own, sib), "small_chips")
    small_shapes = dict(b_ada=b_ada.shape, norm1_g=norm1_g.shape, norm2_g=norm2_g.shape, final_g=final_g.shape,
                        sgu_ln_g=sgu_ln_g.shape, sgu_ln_b=sgu_ln_b.shape, sgu_w=sgu_w.shape, sgu_b=sgu_b.shape,
                        conv_w=(DEPTH, 4, 1536), a_log=a_log.shape, dt_bias=dt_bias.shape,
                        gdn_norm_g=gdn_norm_g.shape)

    def full_conv(a):
        return lax.dynamic_update_slice_in_dim(jnp.zeros((DEPTH, 4, 1536), F32), a, chip * conv_cols, axis=2)

    def pack_state(b_, n1, n2, fg, lg, lb, sw, sb, cv, al, db, gn):
        return _pack_rows([b_, n1, n2, fg, lg, lb, sw, sb, full_conv(cv), al, db, gn])

    w_small = pack_state(b_ada, norm1_g, norm2_g, final_g, sgu_ln_g, sgu_ln_b, sgu_w, sgu_b, conv_w, a_log, dt_bias,
                         gdn_norm_g)
    m_small = pack_state(m_b_ada, m_norm1_g, m_norm2_g, m_final_g, m_sgu_ln_g, m_sgu_ln_b, m_sgu_w, m_sgu_b, m_conv_w,
                         m_a_log, m_dt_bias, m_gdn_norm_g)
    v_small = pack_state(v_b_ada, v_norm1_g, v_norm2_g, v_final_g, v_sgu_ln_g, v_sgu_ln_b, v_sgu_w, v_sgu_b, v_conv_w,
                         v_a_log, v_dt_bias, v_gdn_norm_g)
    small_out = _small_finalize(gathered, w_small, m_small, v_small)
    sg, sd, sm, sv = [_unpack_rows(a, small_shapes) for a in small_out]
    for dct in (sg, sd, sm, sv):
        dct["conv_w"] = lax.dynamic_slice_in_dim(dct["conv_w"], chip * conv_cols, conv_cols, axis=2)

    dmod_all = gathered[:, :2 * _DMOD_ROWS].reshape(8, DEPTH, 6 * D)
    dmod_cols = lax.dynamic_slice_in_dim(dmod_all, chip * ada_cols, ada_cols, axis=2).transpose(1, 0, 2)
    g_ada, d_ada, nm_ada, nv_ada = _ada_backward_adamw(c_all, dmod_cols, w_ada, m_w_ada, v_w_ada)

    gb_in_c = gb_in[:, :, :IN_W].reshape(DEPTH, D, N_CHIPS, IN_W // N_CHIPS).transpose(0, 2, 1, 3)
    partials = [gb_in_c, gb_out, gb_ff1, gb_ff2]
    from_sib = _grads_pair_send(partials)
    names = ("in", "out", "ff1", "ff2")
    pair = [_pair_sum(g, ga, place, f"pair_sum_{n}") for g, ga, n in zip(partials, from_sib, names)]
    recv = _grads_chip_exchange(pair)
    mine = [_chip_sum(p, rc, place, f"chip_sum_{n}") for p, rc, n in zip(pair, recv, names)]
    grads = _grads_pair_share(mine)
    big = {}
    for n, g, (w, m, v) in zip(names, grads, ((w_in, m_w_in, v_w_in), (w_out, m_w_out, v_w_out),
                                              (w_ff1, m_w_ff1, v_w_ff1), (w_ff2, m_w_ff2, v_w_ff2))):
        big[n] = (g,) + tuple(_adamw(w, g, m, v, f"adamw_{n}"))

    def outs(k):
        s = (sg, sd, sm, sv)[k]
        return [(g_ada, d_ada, nm_ada, nv_ada)[k], s["b_ada"], s["norm1_g"], big["in"][k], s["sgu_ln_g"],
                s["sgu_ln_b"], s["sgu_w"], s["sgu_b"], s["conv_w"], s["a_log"], s["dt_bias"], s["gdn_norm_g"],
                big["out"][k], s["norm2_g"], big["ff1"][k], big["ff2"][k], s["final_g"]]

    return (loss, grad_x, *outs(0), *outs(1), *outs(2), *outs(3))
```

```python
import functools
import math

import jax
import jax.numpy as jnp
from jax import lax
from jax.experimental import pallas as pl
from jax.experimental.pallas import tpu as pltpu

F32 = jnp.float32
BF16 = jnp.bfloat16

DEPTH = 4
D = 1024
HEADS = 4
HD = 128
BLK = 128
IN_W = 3080
NW = 3200
GATE0 = 3072
DFF = 4096
N_CHIPS = 4
RMS_EPS = 1e-6
LN_EPS = 1e-5
QK_SCALE = HD ** -0.5
LR, B1, B2, ADAM_EPS, WD, STEP = 0.001, 0.9, 0.999, 1e-08, 0.01, 10
VMEM_LIMIT = 56 * 1024 * 1024
MESH = pl.DeviceIdType.MESH
HOPS = ((1, 0), (0, 1), (1, 1))
HI = lax.Precision.HIGHEST


def _dot(a, b):
    return jnp.dot(a.astype(BF16), b.astype(BF16), preferred_element_type=F32)


def _dot_nt(a, b):
    return lax.dot_general(a.astype(BF16), b.astype(BF16), (((1,), (1,)), ((), ())), preferred_element_type=F32)


def _dot_tn(a, b):
    return lax.dot_general(a.astype(BF16), b.astype(BF16), (((0,), (0,)), ((), ())), preferred_element_type=F32)


def _dotf(a, b):
    return jnp.dot(a, b, precision=HI, preferred_element_type=F32)


def _split(a):
    hi = a.astype(BF16)
    return hi, (a - hi.astype(F32)).astype(BF16)


def _dg3(a, b, dims, batch=((), ())):
    ah, al = _split(a)
    bh, bl = _split(b)
    f = lambda x, y: lax.dot_general(x, y, (dims, batch), preferred_element_type=F32)
    return f(ah, bh) + (f(ah, bl) + f(al, bh))


def _bmm3(a, b):
    return _dg3(a, b, ((2,), (1,)), ((0,), (0,)))


def _d3(a, b):
    return _dg3(a, b, ((1,), (0,)))


def _d3_nt(a, b):
    return _dg3(a, b, ((1,), (1,)))


def _d3_tn(a, b):
    return _dg3(a, b, ((0,), (0,)))


def _dotf_tn(a, b):
    return lax.dot_general(a, b, (((0,), (0,)), ((), ())), precision=HI, preferred_element_type=F32)


def _sigmoid(x):
    return 1.0 / (1.0 + jnp.exp(-x))


def _softplus(x):
    return jnp.maximum(x, 0.0) + jnp.log(1.0 + jnp.exp(-jnp.abs(x)))


_G0 = math.sqrt(2.0 / math.pi)
_G1 = 0.044715


def _gelu(x):
    t = jnp.tanh(_G0 * (x + _G1 * x * x * x))
    return 0.5 * x * (1.0 + t)


def _gelu_grad(x):
    t = jnp.tanh(_G0 * (x + _G1 * x * x * x))
    return 0.5 * (1.0 + t) + 0.5 * x * (1.0 - t * t) * (_G0 * (1.0 + 3.0 * _G1 * x * x))


def _silu(x):
    return x * _sigmoid(x)


def _silu_grad(x):
    s = _sigmoid(x)
    return s * (1.0 + x * (1.0 - s))


def _rms_stats(x):
    rstd = lax.rsqrt(jnp.mean(x * x, axis=-1, keepdims=True) + RMS_EPS)
    return x * rstd, rstd


def _norm_mod(x, ng, sc, sh):
    xh, _ = _rms_stats(x)
    return xh * (ng * (1.0 + sc)) + sh


def _norm_mod_bwd(dh, x, ng, sc):
    xh, rstd = _rms_stats(x)
    dsh = jnp.sum(dh, axis=0, keepdims=True)
    dsc = jnp.sum(dh * xh, axis=0, keepdims=True) * ng
    dng = jnp.sum(dh * xh, axis=0, keepdims=True) * (1.0 + sc)
    dxh = dh * (ng * (1.0 + sc))
    dx = rstd * (dxh - xh * jnp.mean(dxh * xh, axis=-1, keepdims=True))
    return dx, dsh, dsc, dng


def _iota2(shape, axis):
    return lax.broadcasted_iota(jnp.int32, shape, axis)


def _col(tile, idx):
    return jnp.sum(jnp.where(_iota2(tile.shape, 1) == idx, tile, 0.0), axis=1, keepdims=True)


def _row(tile, idx):
    return jnp.sum(jnp.where(_iota2(tile.shape, 0) == idx, tile, 0.0), axis=0, keepdims=True)


def _put_col(col, idx, width=HD):
    shape = (col.shape[0], width)
    return jnp.where(_iota2(shape, 1) == idx, jnp.broadcast_to(col, shape), 0.0)


def _tri_inverse(m):
    rows, cols = _iota2(m.shape, m.ndim - 2), _iota2(m.shape, m.ndim - 1)
    mm = _bmm3 if m.ndim == 3 else _d3
    eye = jnp.where(rows == cols, 1.0, 0.0).astype(F32)
    n = jnp.where((rows >> 3) == (cols >> 3), -m, 0.0)
    p = eye + n
    n2 = mm(n, n)
    p = p + mm(n2, p)
    n4 = mm(n2, n2)
    p = p + mm(n4, p)
    for shift in (3, 4, 5, 6):
        same_pair = (rows >> (shift + 1)) == (cols >> (shift + 1))
        below = jnp.logical_and(((rows >> shift) & 1) == 1, ((cols >> shift) & 1) == 0)
        off = jnp.where(jnp.logical_and(same_pair, below), m, 0.0)
        p = p - mm(p, mm(off, p))
    return p


def _cparams(sem=None):
    return pltpu.CompilerParams(dimension_semantics=sem, vmem_limit_bytes=VMEM_LIMIT)


def _my_place():
    return lax.axis_index("x"), lax.axis_index("y"), lax.axis_index("c")


def _hop(xi, yi, hop):
    dx, dy = hop
    return (1 - xi if dx else xi), (1 - yi if dy else yi)


def _pair_exchange(x, name):
    def body(x_ref, o_ref, ssem, rsem):
        xi, yi, ci = _my_place()
        cp = pltpu.make_async_remote_copy(x_ref, o_ref, ssem, rsem, device_id=(xi, yi, 1 - ci), device_id_type=MESH)
        cp.start()
        cp.wait()

    return pl.pallas_call(
        body, name=name, out_shape=jax.ShapeDtypeStruct(x.shape, x.dtype),
        in_specs=[pl.BlockSpec(memory_space=pltpu.VMEM)], out_specs=pl.BlockSpec(memory_space=pltpu.VMEM),
        scratch_shapes=[pltpu.SemaphoreType.DMA, pltpu.SemaphoreType.DMA],
        compiler_params=pltpu.CompilerParams(vmem_limit_bytes=VMEM_LIMIT),
    )(x)


def _chip_allgather(x, name):
    def body(x_ref, o_ref, ssems, rsems, lsem):
        xi, yi, ci = _my_place()
        me = 2 * xi + yi
        loc = pltpu.make_async_copy(x_ref, o_ref.at[me], lsem)
        loc.start()
        sends = []
        for k, hop in enumerate(HOPS):
            tx, ty = _hop(xi, yi, hop)
            cp = pltpu.make_async_remote_copy(x_ref, o_ref.at[me], ssems.at[k], rsems.at[k],
                                              device_id=(tx, ty, ci), device_id_type=MESH)
            cp.start()
            sends.append(cp)
        for k, hop in enumerate(HOPS):
            tx, ty = _hop(xi, yi, hop)
            pltpu.make_async_remote_copy(x_ref, o_ref.at[2 * tx + ty], ssems.at[k], rsems.at[k],
                                         device_id=(tx, ty, ci), device_id_type=MESH).wait_recv()
        for cp in sends:
            cp.wait_send()
        loc.wait()

    return pl.pallas_call(
        body, name=name, out_shape=jax.ShapeDtypeStruct((N_CHIPS,) + x.shape, x.dtype),
        in_specs=[pl.BlockSpec(memory_space=pltpu.VMEM)], out_specs=pl.BlockSpec(memory_space=pltpu.VMEM),
        scratch_shapes=[pltpu.SemaphoreType.DMA((3,)), pltpu.SemaphoreType.DMA((3,)), pltpu.SemaphoreType.DMA],
        compiler_params=pltpu.CompilerParams(vmem_limit_bytes=VMEM_LIMIT),
    )(x)


def _hbm_specs(n):
    return [pl.BlockSpec(memory_space=pl.ANY)] * n


def _cast_into_slot(w, place):
    n_l, r, c = w.shape
    tr = _row_tile(r)

    def body(p_ref, w_ref, o_ref):
        o_ref[...] = w_ref[...].astype(BF16)

    return pl.pallas_call(
        body, name=f"cast_slot_{r}x{c}", out_shape=jax.ShapeDtypeStruct((N_CHIPS,) + w.shape, BF16),
        grid_spec=pltpu.PrefetchScalarGridSpec(
            num_scalar_prefetch=1, grid=(n_l, r // tr),
            in_specs=[pl.BlockSpec((None, tr, c), lambda l, k, pr: (l, k, 0))],
            out_specs=pl.BlockSpec((None, None, tr, c), lambda l, k, pr: (pr[0], l, k, 0))),
        compiler_params=_cparams(("parallel", "parallel")),
    )(place, w)


def _weights_allgather(bufs):
    n = len(bufs)

    def body(*refs):
        outs = refs[n:2 * n]
        s_ici, r_ici, s_d2d, r_d2d = refs[2 * n:]
        xi, yi, ci = _my_place()
        me = 2 * xi + yi
        lay, sib_lay = pl.ds(2 * ci, 2), pl.ds(2 * (1 - ci), 2)
        pending = []
        for i in range(n):
            for k, hop in enumerate(HOPS):
                tx, ty = _hop(xi, yi, hop)
                cp = pltpu.make_async_remote_copy(outs[i].at[me, lay], outs[i].at[me, lay], s_ici.at[i, k],
                                                  r_ici.at[i, k], device_id=(tx, ty, ci), device_id_type=MESH)
                cp.start()
                pending.append(cp.wait_send)
        for k, hop in enumerate(HOPS):
            tx, ty = _hop(xi, yi, hop)
            src = 2 * tx + ty
            for i in range(n):
                pltpu.make_async_remote_copy(outs[i].at[src, lay], outs[i].at[src, lay], s_ici.at[i, k], r_ici.at[i, k],
                                             device_id=(tx, ty, ci), device_id_type=MESH).wait_recv()
                cp = pltpu.make_async_remote_copy(outs[i].at[src, lay], outs[i].at[src, lay], s_d2d.at[i, k],
                                                  r_d2d.at[i, k], device_id=(xi, yi, 1 - ci), device_id_type=MESH)
                cp.start()
                pending.append(cp.wait_send)
        for k, hop in enumerate(HOPS):
            tx, ty = _hop(xi, yi, hop)
            src = 2 * tx + ty
            for i in range(n):
                pltpu.make_async_remote_copy(outs[i].at[src, sib_lay], outs[i].at[src, sib_lay], s_d2d.at[i, k],
                                             r_d2d.at[i, k], device_id=(xi, yi, 1 - ci), device_id_type=MESH).wait_recv()
        for wait in pending:
            wait()

    return pl.pallas_call(
        body, name="weights_allgather",
        out_shape=[jax.ShapeDtypeStruct(b.shape, b.dtype) for b in bufs],
        in_specs=_hbm_specs(n), out_specs=_hbm_specs(n), input_output_aliases={i: i for i in range(n)},
        scratch_shapes=[pltpu.SemaphoreType.DMA((n, 3))] * 4,
    )(*bufs)


def _grads_pair_send(gs):
    n = len(gs)

    def body(*refs):
        ins, outs, ssem, rsem = refs[:n], refs[n:2 * n], refs[2 * n], refs[2 * n + 1]
        xi, yi, ci = _my_place()
        sib_lay = pl.ds(2 * (1 - ci), 2)
        cps = []
        for i in range(n):
            cp = pltpu.make_async_remote_copy(ins[i].at[sib_lay], outs[i], ssem.at[i], rsem.at[i],
                                              device_id=(xi, yi, 1 - ci), device_id_type=MESH)
            cp.start()
            cps.append(cp)
        for cp in cps:
            cp.wait()

    return pl.pallas_call(
        body, name="grads_pair_send",
        out_shape=[jax.ShapeDtypeStruct((2,) + g.shape[1:], g.dtype) for g in gs],
        in_specs=_hbm_specs(n), out_specs=_hbm_specs(n),
        scratch_shapes=[pltpu.SemaphoreType.DMA((n,)), pltpu.SemaphoreType.DMA((n,))],
    )(*gs)


def _grads_chip_exchange(ps):
    n = len(ps)

    def body(*refs):
        ins, outs = refs[:n], refs[n:2 * n]
        ssems, rsems = refs[2 * n:]
        xi, yi, ci = _my_place()
        both = pl.ds(0, 2)
        sends = []
        for i in range(n):
            for k, hop in enumerate(HOPS):
                tx, ty = _hop(xi, yi, hop)
                cp = pltpu.make_async_remote_copy(ins[i].at[both, 2 * tx + ty], outs[i].at[k], ssems.at[i, k],
                                                  rsems.at[i, k], device_id=(tx, ty, ci), device_id_type=MESH)
                cp.start()
                sends.append(cp)
        for cp in sends:
            cp.wait()

    return pl.pallas_call(
        body, name="grads_chip_exchange",
        out_shape=[jax.ShapeDtypeStruct((3, 2) + p.shape[2:], p.dtype) for p in ps],
        in_specs=_hbm_specs(n), out_specs=_hbm_specs(n),
        scratch_shapes=[pltpu.SemaphoreType.DMA((n, 3)), pltpu.SemaphoreType.DMA((n, 3))],
    )(*ps)


def _grads_pair_share(gs):
    n = len(gs)

    def body(*refs):
        outs = refs[n:2 * n]
        ssem, rsem = refs[2 * n:]
        xi, yi, ci = _my_place()
        lay, sib_lay = pl.ds(2 * ci, 2), pl.ds(2 * (1 - ci), 2)
        sends = []
        for i in range(n):
            cp = pltpu.make_async_remote_copy(outs[i].at[lay], outs[i].at[lay], ssem.at[i], rsem.at[i],
                                              device_id=(xi, yi, 1 - ci), device_id_type=MESH)
            cp.start()
            sends.append(cp)
        for i in range(n):
            pltpu.make_async_remote_copy(outs[i].at[sib_lay], outs[i].at[sib_lay], ssem.at[i], rsem.at[i],
                                         device_id=(xi, yi, 1 - ci), device_id_type=MESH).wait_recv()
        for cp in sends:
            cp.wait_send()

    return pl.pallas_call(
        body, name="grads_pair_share",
        out_shape=[jax.ShapeDtypeStruct(g.shape, g.dtype) for g in gs],
        in_specs=_hbm_specs(n), out_specs=_hbm_specs(n), input_output_aliases={i: i for i in range(n)},
        scratch_shapes=[pltpu.SemaphoreType.DMA((n,)), pltpu.SemaphoreType.DMA((n,))],
    )(*gs)


def _row_tile(r):
    return min(r, 256)


def _pair_sum(g, ga, place, name):
    _, _, r, c = g.shape
    tr = _row_tile(r)

    def body(p_ref, g_ref, ga_ref, o_ref):
        o_ref[...] = (g_ref[...].astype(F32) + ga_ref[...].astype(F32)).astype(o_ref.dtype)

    return pl.pallas_call(
        body, name=name, out_shape=jax.ShapeDtypeStruct(ga.shape, ga.dtype),
        grid_spec=pltpu.PrefetchScalarGridSpec(
            num_scalar_prefetch=1, grid=(2, N_CHIPS, r // tr),
            in_specs=[pl.BlockSpec((None, None, tr, c), lambda i, j, k, pr: (2 * pr[1] + i, j, k, 0)),
                      pl.BlockSpec((None, None, tr, c), lambda i, j, k, pr: (i, j, k, 0))],
            out_specs=pl.BlockSpec((None, None, tr, c), lambda i, j, k, pr: (i, j, k, 0))),
        compiler_params=_cparams(("parallel", "parallel", "parallel")),
    )(place, g, ga)


def _chip_sum(pair, recv, place, name):
    _, _, r, c = pair.shape
    tr = _row_tile(r)

    def body(p_ref, own_ref, r_ref, o_ref):
        acc = own_ref[...].astype(F32) + r_ref[0].astype(F32)
        acc = acc + r_ref[1].astype(F32)
        o_ref[...] = acc + r_ref[2].astype(F32)

    return pl.pallas_call(
        body, name=name, out_shape=jax.ShapeDtypeStruct((DEPTH, r, c), F32),
        grid_spec=pltpu.PrefetchScalarGridSpec(
            num_scalar_prefetch=1, grid=(2, r // tr),
            in_specs=[pl.BlockSpec((None, None, tr, c), lambda i, k, pr: (i, pr[0], k, 0)),
                      pl.BlockSpec((3, None, tr, c), lambda i, k, pr: (0, i, k, 0))],
            out_specs=pl.BlockSpec((None, tr, c), lambda i, k, pr: (2 * pr[1] + i, k, 0))),
        compiler_params=_cparams(("parallel", "parallel")),
    )(place, pair, recv)


def _adam_math(w, g, m, v):
    m = B1 * m + (1.0 - B1) * g
    v = B2 * v + (1.0 - B2) * (g * g)
    m_hat = m / (1.0 - B1 ** STEP)
    v_hat = v / (1.0 - B2 ** STEP)
    delta = -LR * (m_hat / (jnp.sqrt(v_hat) + ADAM_EPS) + WD * w)
    return delta, m, v


def _adamw(w, g, m, v, name):
    n_l, r, c = w.shape
    tr = _row_tile(r)

    def body(w_ref, g_ref, m_ref, v_ref, d_ref, nm_ref, nv_ref):
        d_ref[...], nm_ref[...], nv_ref[...] = _adam_math(w_ref[...], g_ref[...], m_ref[...], v_ref[...])

    spec = pl.BlockSpec((None, tr, c), lambda i, k: (i, k, 0))
    return pl.pallas_call(
        body, name=name, out_shape=[jax.ShapeDtypeStruct(w.shape, F32)] * 3, grid=(n_l, r // tr),
        in_specs=[spec] * 4, out_specs=[spec] * 3, compiler_params=_cparams(("parallel", "parallel")),
    )(w, g, m, v)


def _ada_forward(c_all, w_ada, b_cols):
    cols = w_ada.shape[2]
    tn = 512

    def body(c_ref, w_ref, b_ref, o_ref):
        o_ref[...] = _dotf(_silu(c_ref[...]), w_ref[...]) + b_ref[...]

    return pl.pallas_call(
        body, name="ada_forward", out_shape=jax.ShapeDtypeStruct((DEPTH, 8, cols), F32), grid=(DEPTH, cols // tn),
        in_specs=[pl.BlockSpec((8, D), lambda l, j: (0, 0)),
                  pl.BlockSpec((None, D, tn), lambda l, j: (l, 0, j)),
                  pl.BlockSpec((None, 1, tn), lambda l, j: (l, 0, j))],
        out_specs=pl.BlockSpec((None, 8, tn), lambda l, j: (l, 0, j)),
        compiler_params=_cparams(("parallel", "parallel")),
    )(c_all, w_ada, b_cols.reshape(DEPTH, 1, cols))


def _ada_backward_adamw(c_all, dmod_cols, w, m, v):
    cols = w.shape[2]
    tn = 512

    def body(c_ref, d_ref, w_ref, m_ref, v_ref, g_ref, dl_ref, nm_ref, nv_ref):
        g = _dotf_tn(_silu(c_ref[...]), d_ref[...])
        g_ref[...] = g
        dl_ref[...], nm_ref[...], nv_ref[...] = _adam_math(w_ref[...], g, m_ref[...], v_ref[...])

    wspec = pl.BlockSpec((None, D, tn), lambda l, j: (l, 0, j))
    return pl.pallas_call(
        body, name="ada_backward_adamw", out_shape=[jax.ShapeDtypeStruct(w.shape, F32)] * 4, grid=(DEPTH, cols // tn),
        in_specs=[pl.BlockSpec((8, D), lambda l, j: (0, 0)), pl.BlockSpec((None, 8, tn), lambda l, j: (l, 0, j)),
                  wspec, wspec, wspec],
        out_specs=[wspec] * 4, compiler_params=_cparams(("parallel", "parallel")),
    )(c_all, dmod_cols, w, m, v)


def _tok_tile(t):
    return min(t, 256)


def _wspec4(r, c, l):
    return pl.BlockSpec((N_CHIPS, None, r, c), lambda i: (0, l, 0, 0))


def _fwd_in(x, modv, w_in, l):
    t = x.shape[0]
    tm = _tok_tile(t)

    def body(x_ref, mod_ref, w_ref, o_ref, h_ref):
        h = _norm_mod(x_ref[...], mod_ref[6:7, :], mod_ref[1:2, :], mod_ref[0:1, :]).astype(BF16)
        h_ref[...] = h
        o_ref[...] = jnp.dot(h, w_ref[...], preferred_element_type=F32)

    return pl.pallas_call(
        body, name=f"fwd_in_{l}", grid=(t // tm,),
        out_shape=[jax.ShapeDtypeStruct((t, NW), F32), jax.ShapeDtypeStruct((t, D), BF16)],
        in_specs=[pl.BlockSpec((tm, D), lambda i: (i, 0)), pl.BlockSpec((None, 8, D), lambda i: (l, 0, 0)),
                  pl.BlockSpec((None, D, NW), lambda i: (l, 0, 0))],
        out_specs=[pl.BlockSpec((tm, NW), lambda i: (i, 0)), pl.BlockSpec((tm, D), lambda i: (i, 0))],
        compiler_params=_cparams(("parallel",)),
    )(x, modv, w_in)


def _fwd_out(x, mix, modv, w_out, l):
    t = x.shape[0]
    tm = _tok_tile(t)

    def body(x_ref, mix_ref, mod_ref, w_ref, o_ref):
        w = w_ref[...].reshape(D, D)
        o_ref[...] = x_ref[...] + mod_ref[2:3, :] * jnp.dot(mix_ref[...], w, preferred_element_type=F32)

    return pl.pallas_call(
        body, name=f"fwd_out_{l}", out_shape=jax.ShapeDtypeStruct((t, D), F32), grid=(t // tm,),
        in_specs=[pl.BlockSpec((tm, D), lambda i: (i, 0)), pl.BlockSpec((tm, D), lambda i: (i, 0)),
                  pl.BlockSpec((None, 8, D), lambda i: (l, 0, 0)), _wspec4(D // N_CHIPS, D, l)],
        out_specs=pl.BlockSpec((tm, D), lambda i: (i, 0)), compiler_params=_cparams(("parallel",)),
    )(x, mix, modv, w_out)


def _fwd_ff1(x, modv, w_ff1, l):
    t = x.shape[0]
    tm = _tok_tile(t)

    def body(x_ref, mod_ref, w_ref, o_ref, h_ref):
        h = _norm_mod(x_ref[...], mod_ref[7:8, :], mod_ref[4:5, :], mod_ref[3:4, :]).astype(BF16)
        h_ref[...] = h
        for j in range(N_CHIPS):
            f = jnp.dot(h, w_ref[j], preferred_element_type=F32)
            o_ref[:, j * D:(j + 1) * D] = jnp.maximum(f, 0.0).astype(BF16)

    return pl.pallas_call(
        body, name=f"fwd_ff1_{l}", grid=(t // tm,),
        out_shape=[jax.ShapeDtypeStruct((t, DFF), BF16), jax.ShapeDtypeStruct((t, D), BF16)],
        in_specs=[pl.BlockSpec((tm, D), lambda i: (i, 0)), pl.BlockSpec((None, 8, D), lambda i: (l, 0, 0)),
                  _wspec4(D, D, l)],
        out_specs=[pl.BlockSpec((tm, DFF), lambda i: (i, 0)), pl.BlockSpec((tm, D), lambda i: (i, 0))],
        compiler_params=_cparams(("parallel",)),
    )(x, modv, w_ff1)


def _fwd_ff2(x, r, modv, w_ff2, l):
    t = x.shape[0]
    tm = _tok_tile(t)

    def body(x_ref, r_ref, mod_ref, w_ref, o_ref):
        acc = jnp.zeros((tm, D), F32)
        for j in range(N_CHIPS):
            rj = r_ref[:, j * D:(j + 1) * D].astype(F32)
            acc = acc + jnp.dot((rj * rj).astype(BF16), w_ref[j], preferred_element_type=F32)
        o_ref[...] = x_ref[...] + mod_ref[5:6, :] * acc

    return pl.pallas_call(
        body, name=f"fwd_ff2_{l}", out_shape=jax.ShapeDtypeStruct((t, D), F32), grid=(t // tm,),
        in_specs=[pl.BlockSpec((tm, D), lambda i: (i, 0)), pl.BlockSpec((tm, DFF), lambda i: (i, 0)),
                  pl.BlockSpec((None, 8, D), lambda i: (l, 0, 0)), _wspec4(D, D, l)],
        out_specs=pl.BlockSpec((tm, D), lambda i: (i, 0)), compiler_params=_cparams(("parallel",)),
    )(x, r, modv, w_ff2)


def _loss_head(x, target, final_g):
    t = x.shape[0]
    tm = _tok_tile(t)

    def body(x_ref, t_ref, g_ref, dx_ref, st_ref):
        @pl.when(pl.program_id(0) == 0)
        def _():
            st_ref[...] = jnp.zeros_like(st_ref)

        xh, rstd = _rms_stats(x_ref[...])
        g = g_ref[...]
        err = xh * g - t_ref[...]
        loss = 0.5 * jnp.sum(jnp.mean(err * err, axis=-1, keepdims=True), axis=0, keepdims=True)
        dy = err * (1.0 / D)
        st_ref[0:1, :] += jnp.sum(dy * xh, axis=0, keepdims=True)
        st_ref[1:2, :] += jnp.broadcast_to(loss, (1, D))
        dxh = dy * g
        dx_ref[...] = rstd * (dxh - xh * jnp.mean(dxh * xh, axis=-1, keepdims=True))

    return pl.pallas_call(
        body, name="loss_head", out_shape=[jax.ShapeDtypeStruct((t, D), F32), jax.ShapeDtypeStruct((8, D), F32)],
        grid=(t // tm,),
        in_specs=[pl.BlockSpec((tm, D), lambda i: (i, 0)), pl.BlockSpec((tm, D), lambda i: (i, 0)),
                  pl.BlockSpec((1, D), lambda i: (0, 0))],
        out_specs=[pl.BlockSpec((tm, D), lambda i: (i, 0)), pl.BlockSpec((8, D), lambda i: (0, 0))],
        compiler_params=_cparams(("arbitrary",)),
    )(x, target, final_g.reshape(1, D))


def _bwd_ff2(dx2, r, modv, w_ff2, l):
    t = dx2.shape[0]
    tm = _tok_tile(t)

    def body(d_ref, r_ref, mod_ref, w_ref, o_ref):
        dyg = (d_ref[...] * mod_ref[5:6, :]).astype(BF16)
        for j in range(N_CHIPS):
            da = lax.dot_general(dyg, w_ref[j], (((1,), (1,)), ((), ())), preferred_element_type=F32)
            o_ref[:, j * D:(j + 1) * D] = (da * 2.0 * r_ref[:, j * D:(j + 1) * D].astype(F32)).astype(BF16)

    return pl.pallas_call(
        body, name=f"bwd_ff2_{l}", out_shape=jax.ShapeDtypeStruct((t, DFF), BF16), grid=(t // tm,),
        in_specs=[pl.BlockSpec((tm, D), lambda i: (i, 0)), pl.BlockSpec((tm, DFF), lambda i: (i, 0)),
                  pl.BlockSpec((None, 8, D), lambda i: (l, 0, 0)), _wspec4(D, D, l)],
        out_specs=pl.BlockSpec((tm, DFF), lambda i: (i, 0)), compiler_params=_cparams(("parallel",)),
    )(dx2, r, modv, w_ff2)


def _bwd_out(dx1, modv, w_out, l):
    t = dx1.shape[0]
    tm = _tok_tile(t)

    def body(d_ref, mod_ref, w_ref, o_ref):
        dyg = (d_ref[...] * mod_ref[2:3, :]).astype(BF16)
        w = w_ref[...].reshape(D, D)
        o_ref[...] = lax.dot_general(dyg, w, (((1,), (1,)), ((), ())), preferred_element_type=F32).astype(BF16)

    return pl.pallas_call(
        body, name=f"bwd_out_{l}", out_shape=jax.ShapeDtypeStruct((t, D), BF16), grid=(t // tm,),
        in_specs=[pl.BlockSpec((tm, D), lambda i: (i, 0)), pl.BlockSpec((None, 8, D), lambda i: (l, 0, 0)),
                  _wspec4(D // N_CHIPS, D, l)],
        out_specs=pl.BlockSpec((tm, D), lambda i: (i, 0)), compiler_params=_cparams(("parallel",)),
    )(dx1, modv, w_out)


def _bwd_norm(dy, w, x, dres, modv, l, which):
    t = x.shape[0]
    tm = _tok_tile(t)
    rows = (6, 1) if which == "in" else (7, 4)
    width = dy.shape[1]

    def body(dy_ref, w_ref, x_ref, dr_ref, mod_ref, dx_ref, st_ref):
        @pl.when(pl.program_id(0) == 0)
        def _():
            st_ref[...] = jnp.zeros_like(st_ref)

        if which == "in":
            dh = lax.dot_general(dy_ref[...], w_ref[...], (((1,), (1,)), ((), ())), preferred_element_type=F32)
        else:
            dh = jnp.zeros((tm, D), F32)
            for j in range(N_CHIPS):
                dh = dh + lax.dot_general(dy_ref[:, j * D:(j + 1) * D], w_ref[j], (((1,), (1,)), ((), ())),
                                          preferred_element_type=F32)
        ng, sc = mod_ref[rows[0]:rows[0] + 1, :], mod_ref[rows[1]:rows[1] + 1, :]
        dx, dsh, dsc, dng = _norm_mod_bwd(dh, x_ref[...], ng, sc)
        dx_ref[...] = dr_ref[...] + dx
        st_ref[0:1, :] += dsh
        st_ref[1:2, :] += dsc
        st_ref[2:3, :] += dng

    wspec = pl.BlockSpec((None, D, NW), lambda i: (l, 0, 0)) if which == "in" else _wspec4(D, D, l)
    return pl.pallas_call(
        body, name=f"bwd_norm_{which}_{l}",
        out_shape=[jax.ShapeDtypeStruct((t, D), F32), jax.ShapeDtypeStruct((8, D), F32)], grid=(t // tm,),
        in_specs=[pl.BlockSpec((tm, width), lambda i: (i, 0)), wspec, pl.BlockSpec((tm, D), lambda i: (i, 0)),
                  pl.BlockSpec((tm, D), lambda i: (i, 0)), pl.BlockSpec((None, 8, D), lambda i: (l, 0, 0))],
        out_specs=[pl.BlockSpec((tm, D), lambda i: (i, 0)), pl.BlockSpec((8, D), lambda i: (0, 0))],
        compiler_params=_cparams(("arbitrary",)),
    )(dy, w, x, dres, modv)


def _grad_weight(lhs, rhs, buf, modv, l, which, w_gate=None):
    t = lhs.shape[0]
    tm = min(t, 2048)
    nt = t // tm
    gated = which in ("out", "ff2")
    if which == "in":
        nj, lw, rw, orows, ocols = 5, D, NW // 5, D, NW // 5
    elif which == "ff1":
        nj, lw, rw, orows, ocols = N_CHIPS, D, D, D, D
    elif which == "out":
        nj, lw, rw, orows, ocols = N_CHIPS, D // N_CHIPS, D, D // N_CHIPS, D
    else:
        nj, lw, rw, orows, ocols = N_CHIPS, D, D, D, D
    gate_row = 2 if which == "out" else 5

    def body(*refs):
        if gated:
            l_ref, r_ref, mod_ref, wg_ref, _, o_ref, dg_ref, acc = refs
        else:
            l_ref, r_ref, mod_ref, _, o_ref, acc = refs
        j, k = pl.program_id(0), pl.program_id(1)

        @pl.when(k == 0)
        def _():
            acc[...] = jnp.zeros_like(acc)

        if which == "ff2":
            lv = l_ref[...].astype(F32)
            lv = lv * lv
        else:
            lv = l_ref[...]
        acc[...] += _dot_tn(lv, r_ref[...])

        if gated:
            @pl.when(jnp.logical_and(j == 0, k == 0))
            def _():
                dg_ref[...] = jnp.zeros_like(dg_ref)

        @pl.when(k == nt - 1)
        def _():
            raw = acc[...]
            if gated:
                o_ref[...] = (raw * mod_ref[gate_row:gate_row + 1, :]).astype(o_ref.dtype)
                dg_ref[0:1, :] += jnp.sum(raw * wg_ref[...].astype(F32), axis=0, keepdims=True)
            else:
                o_ref[...] = raw.astype(o_ref.dtype)

    if which in ("in", "ff1"):
        lspec = pl.BlockSpec((tm, lw), lambda j, k: (k, 0))
        rspec = pl.BlockSpec((tm, rw), lambda j, k: (k, j))
    else:
        lspec = pl.BlockSpec((tm, lw), lambda j, k: (k, j))
        rspec = pl.BlockSpec((tm, rw), lambda j, k: (k, 0))
    mspec = pl.BlockSpec((None, 8, D), lambda j, k: (l, 0, 0))
    if which == "in":
        ospec = pl.BlockSpec((None, orows, ocols), lambda j, k: (l, 0, j))
    else:
        ospec = pl.BlockSpec((None, None, orows, ocols), lambda j, k: (l, j, 0, 0))
    in_specs = [lspec, rspec, mspec]
    args = [lhs, rhs, modv]
    out_specs = [ospec]
    out_shape = [jax.ShapeDtypeStruct(buf.shape, buf.dtype)]
    if gated:
        in_specs.append(pl.BlockSpec((None, None, orows, ocols), lambda j, k: (j, l, 0, 0)))
        args.append(w_gate)
        out_specs.append(pl.BlockSpec((8, D), lambda j, k: (0, 0)))
        out_shape.append(jax.ShapeDtypeStruct((8, D), F32))
    in_specs.append(pl.BlockSpec(memory_space=pl.ANY))
    args.append(buf)
    res = pl.pallas_call(
        body, name=f"grad_w_{which}_{l}", out_shape=out_shape, grid=(nj, nt), in_specs=in_specs, out_specs=out_specs,
        scratch_shapes=[pltpu.VMEM((orows, ocols), F32)], input_output_aliases={len(args) - 1: 0},
        compiler_params=_cparams(("arbitrary", "arbitrary")),
    )(*args)
    return (res[0], res[1]) if gated else (res[0], None)


def _tri_masks():
    rows, cols = _iota2((BLK, BLK), 0), _iota2((BLK, BLK), 1)
    return rows >= cols, rows > cols


def _sgu_forward(p_ref, lnp_ref, sguw_ref, sgub_ref):
    incl, _ = _tri_masks()
    ug = _gelu(p_ref[:, 0:512])
    vg = _gelu(p_ref[:, 512:1024])
    mu = jnp.mean(vg, axis=-1, keepdims=True)
    xc = vg - mu
    rstd = lax.rsqrt(jnp.mean(xc * xc, axis=-1, keepdims=True) + LN_EPS)
    vhat = xc * rstd
    vn = vhat * lnp_ref[0:1, :] + lnp_ref[1:2, :]
    bias = sgub_ref[...]
    ys, mixed, wms = [], [], []
    for h in range(HEADS):
        wm = jnp.where(incl, sguw_ref[h], 0.0)
        mx = _dot(wm, vn[:, h * HD:(h + 1) * HD]) + _col(bias, h)
        ys.append(ug[:, h * HD:(h + 1) * HD] * mx)
        mixed.append(mx)
        wms.append(wm)
    return ys, ug, vhat, rstd, vn, mixed, wms


def _conv_forward(xbuf, cw_ref):
    conv = cw_ref[0:1, :] * xbuf[5:5 + BLK, :]
    for j in range(1, 4):
        conv = conv + cw_ref[j:j + 1, :] * xbuf[5 + j:5 + j + BLK, :]
    return conv


def _gates(gt, gv_ref):
    incl, _ = _tri_masks()
    beta = _sigmoid(gt)
    neg_a = -jnp.exp(gv_ref[0:1, :])
    gl = neg_a * _softplus(gt + gv_ref[1:2, :])
    gc = _dotf(jnp.where(incl, 1.0, 0.0).astype(F32), gl)
    return beta, gl, gc, gc.T, neg_a


def _head_chunk(act, beta, gc, gct, h):
    incl, strict = _tri_masks()
    qh = act[:, h * HD:(h + 1) * HD]
    kh = act[:, 512 + h * HD:512 + (h + 1) * HD]
    vh = act[:, 1024 + h * HD:1024 + (h + 1) * HD]
    rq = lax.rsqrt(jnp.sum(qh * qh, axis=-1, keepdims=True) + RMS_EPS)
    rk = lax.rsqrt(jnp.sum(kh * kh, axis=-1, keepdims=True) + RMS_EPS)
    qhat, khat = qh * rq, kh * rk
    qn = qhat * QK_SCALE
    b = _col(beta, h)
    gcol = _col(gc, 4 + h)
    grow = _row(gct, 4 + h)
    dmat = jnp.where(incl, jnp.exp(jnp.where(incl, gcol - grow, 0.0)), 0.0)
    gam = jnp.exp(gcol)
    glast = _row(gcol, BLK - 1)
    e = jnp.exp(glast - gcol)
    kk = _d3_nt(khat, khat)
    return dict(qhat=qhat, khat=khat, qn=qn, vh=vh, rq=rq, rk=rk, b=b, dmat=dmat, gam=gam, glast=glast, e=e, kk=kk,
                strict=strict, incl=incl)


def _mixer_forward(p, lnp, sgu_w, sgu_bt, cw, gv, l):
    t = p.shape[0]
    nb = t // BLK

    def body(p_ref, lnp_ref, sguw_ref, sgub_ref, cw_ref, gv_ref,
             mix_ref, s_out, t_out, u_out, w_out, o_out, tail_out, s_scr, xbuf):
        @pl.when(pl.program_id(0) == 0)
        def _():
            s_scr[...] = jnp.zeros_like(s_scr)
            xbuf[0:8, :] = jnp.zeros((8, 1536), F32)

        ys = _sgu_forward(p_ref, lnp_ref, sguw_ref, sgub_ref)[0]
        for h in range(HEADS):
            mix_ref[:, h * HD:(h + 1) * HD] = ys[h].astype(BF16)

        tail_out[...] = xbuf[0:8, :]
        xbuf[8:8 + BLK, :] = p_ref[:, 1024:2560]
        act = _silu(_conv_forward(xbuf, cw_ref))
        xbuf[0:8, :] = xbuf[BLK:BLK + 8, :]
        beta, _, gc, gct, _ = _gates(p_ref[:, GATE0:NW], gv_ref)
        chunks = [_head_chunk(act, beta, gc, gct, h) for h in range(HEADS)]
        for h, hc in enumerate(chunks):
            t_out[h] = jnp.where(hc["strict"], hc["b"] * hc["kk"] * hc["dmat"], 0.0)
        t_out[...] = _tri_inverse(t_out[...])
        for h, hc in enumerate(chunks):
            tm = t_out[h]
            u = _d3(tm, hc["b"] * hc["vh"])
            w = _d3(tm, (hc["b"] * hc["gam"]) * hc["khat"])
            qkm = _dot_nt(hc["qn"], hc["khat"]) * hc["dmat"]
            s = s_scr[h]
            wn = u - _dot(w, s)
            o = _dot(hc["qn"] * hc["gam"], s) + _dot(qkm, wn)
            s_out[h] = s
            s_scr[h] = jnp.exp(hc["glast"]) * s + _dot_tn(hc["khat"] * hc["e"], wn)
            sl = slice(h * HD, (h + 1) * HD)
            u_out[:, sl] = u
            w_out[:, sl] = w
            o_out[:, sl] = o
            on = o * lax.rsqrt(jnp.mean(o * o, axis=-1, keepdims=True) + RMS_EPS) * gv_ref[2:3, :]
            mix_ref[:, 512 + h * HD:512 + (h + 1) * HD] = (on * _silu(p_ref[:, 2560 + h * HD:2560 + (h + 1) * HD])).astype(BF16)

    tok = lambda w: pl.BlockSpec((BLK, w), lambda i: (i, 0))
    st = pl.BlockSpec((None, HEADS, HD, HD), lambda i: (i, 0, 0, 0))
    return pl.pallas_call(
        body, name=f"mixer_fwd_{l}", grid=(nb,),
        out_shape=[jax.ShapeDtypeStruct((t, D), BF16), jax.ShapeDtypeStruct((nb, HEADS, HD, HD), F32),
                   jax.ShapeDtypeStruct((nb, HEADS, HD, HD), F32), jax.ShapeDtypeStruct((t, 512), F32),
                   jax.ShapeDtypeStruct((t, 512), F32), jax.ShapeDtypeStruct((t, 512), F32),
                   jax.ShapeDtypeStruct((nb, 8, 1536), F32)],
        in_specs=[tok(NW), pl.BlockSpec((None, 8, 512), lambda i: (l, 0, 0)),
                  pl.BlockSpec((None, HEADS, HD, HD), lambda i: (l, 0, 0, 0)),
                  pl.BlockSpec((None, HD, HD), lambda i: (l, 0, 0)), pl.BlockSpec((None, 8, 1536), lambda i: (l, 0, 0)),
                  pl.BlockSpec((None, 8, HD), lambda i: (l, 0, 0))],
        out_specs=[tok(D), st, st, tok(512), tok(512), tok(512), pl.BlockSpec((None, 8, 1536), lambda i: (i, 0, 0))],
        scratch_shapes=[pltpu.VMEM((HEADS, HD, HD), F32), pltpu.VMEM((BLK + 8, 1536), F32)],
        compiler_params=_cparams(("arbitrary",)),
    )(p, lnp, sgu_w, sgu_bt, cw, gv)


def _mixer_backward(p, dmix, saved, lnp, sgu_w, sgu_bt, cw, gv, l):
    t = p.shape[0]
    nb = t // BLK
    s_sv, t_sv, u_sv, w_sv, o_sv, tail_sv = saved

    def body(p_ref, dmix_ref, s_ref, t_ref, u_ref, w_ref, o_ref, tail_ref, lnp_ref, sguw_ref, sgub_ref, cw_ref, gv_ref,
             dp_ref, dlnp_ref, dsguw_ref, dsgub_ref, dcw_ref, dgv_ref, ds_scr, xbuf, dcbuf):
        @pl.when(pl.program_id(0) == 0)
        def _():
            ds_scr[...] = jnp.zeros_like(ds_scr)
            dcbuf[BLK:BLK + 8, :] = jnp.zeros((8, 1536), F32)
            dlnp_ref[...] = jnp.zeros_like(dlnp_ref)
            dsguw_ref[...] = jnp.zeros_like(dsguw_ref)
            dsgub_ref[...] = jnp.zeros_like(dsgub_ref)
            dcw_ref[...] = jnp.zeros_like(dcw_ref)
            dgv_ref[...] = jnp.zeros_like(dgv_ref)

        incl, strict = _tri_masks()
        _, ug, vhat, rstd, vn, mixed, wms = _sgu_forward(p_ref, lnp_ref, sguw_ref, sgub_ref)
        dvn_parts, dug_parts = [], []
        dbias = jnp.zeros((BLK, HD), F32)
        for h in range(HEADS):
            sl = slice(h * HD, (h + 1) * HD)
            dy = dmix_ref[:, sl].astype(F32)
            dmx = dy * ug[:, sl]
            dug_parts.append(dy * mixed[h])
            dsguw_ref[h] += jnp.where(incl, _dot_nt(dmx, vn[:, sl]), 0.0)
            dbias = dbias + _put_col(jnp.sum(dmx, axis=1, keepdims=True), h)
            dvn_parts.append(_dot_tn(wms[h], dmx))
        dsgub_ref[...] += dbias
        dvn = jnp.concatenate(dvn_parts, axis=1)
        dug = jnp.concatenate(dug_parts, axis=1)
        dlnp_ref[0:1, :] += jnp.sum(dvn * vhat, axis=0, keepdims=True)
        dlnp_ref[1:2, :] += jnp.sum(dvn, axis=0, keepdims=True)
        dvhat = dvn * lnp_ref[0:1, :]
        dvg = rstd * (dvhat - jnp.mean(dvhat, axis=-1, keepdims=True)
                      - vhat * jnp.mean(dvhat * vhat, axis=-1, keepdims=True))
        dp_ref[:, 0:512] = (dug * _gelu_grad(p_ref[:, 0:512])).astype(BF16)
        dp_ref[:, 512:1024] = (dvg * _gelu_grad(p_ref[:, 512:1024])).astype(BF16)

        xbuf[0:8, :] = tail_ref[...]
        xbuf[8:8 + BLK, :] = p_ref[:, 1024:2560]
        conv = _conv_forward(xbuf, cw_ref)
        act = _silu(conv)
        gt = p_ref[:, GATE0:NW]
        beta, gl, gc, gct, neg_a = _gates(gt, gv_ref)
        gng = gv_ref[2:3, :]
        dbeta_t = jnp.zeros((BLK, HD), F32)
        dgc_t = jnp.zeros((BLK, HD), F32)
        dgng = jnp.zeros((1, HD), F32)
        for h in range(HEADS):
            sl = slice(h * HD, (h + 1) * HD)
            hc = _head_chunk(act, beta, gc, gct, h)
            b, gam, e, dmat, kk = hc["b"], hc["gam"], hc["e"], hc["dmat"], hc["kk"]
            qn, khat, vh = hc["qn"], hc["khat"], hc["vh"]
            gamlast = jnp.exp(hc["glast"])
            s, tm, u, w, o = s_ref[h], t_ref[h], u_ref[:, sl], w_ref[:, sl], o_ref[:, sl]
            ds_next = ds_scr[h]
            z = p_ref[:, 2560 + h * HD:2560 + (h + 1) * HD]
            dy = dmix_ref[:, 512 + h * HD:512 + (h + 1) * HD].astype(F32)
            ro = lax.rsqrt(jnp.mean(o * o, axis=-1, keepdims=True) + RMS_EPS)
            ohat = o * ro
            dp_ref[:, 2560 + h * HD:2560 + (h + 1) * HD] = (dy * ohat * gng * _silu_grad(z)).astype(BF16)
            don = dy * _silu(z)
            dgng = dgng + jnp.sum(don * ohat, axis=0, keepdims=True)
            dohat = don * gng
            do = ro * (dohat - ohat * jnp.mean(dohat * ohat, axis=-1, keepdims=True))
            qk_raw = _dot_nt(qn, khat)
            qkm = qk_raw * dmat
            qd, kd = qn * gam, khat * e
            wn = u - _dot(w, s)
            dwn = _dot_tn(qkm, do) + _dot(kd, ds_next)
            dqd = _dot_nt(do, s)
            dqkm = jnp.where(incl, _dot_nt(do, wn), 0.0)
            ds_scr[h] = _dot_tn(qd, do) + gamlast * ds_next - _dot_tn(w, dwn)
            dgamlast = jnp.sum(jnp.sum(ds_next * s, axis=1, keepdims=True), axis=0, keepdims=True)
            dkd = _dot_nt(wn, ds_next)
            dw = -_dot_nt(dwn, s)
            db1 = _d3_tn(tm, dwn)
            db2 = _d3_tn(tm, dw)
            dm = jnp.where(strict, -(_dot_nt(db1, u) + _dot_nt(db2, w)), 0.0)
            dbeta = (jnp.sum(dm * kk * dmat, axis=1, keepdims=True) + jnp.sum(db1 * vh, axis=1, keepdims=True)
                     + gam * jnp.sum(db2 * khat, axis=1, keepdims=True))
            dkkm = dm * b * dmat
            ddm = dm * b * kk + dqkm * qk_raw
            dgam = b * jnp.sum(db2 * khat, axis=1, keepdims=True) + jnp.sum(dqd * qn, axis=1, keepdims=True)
            g_qk = dqkm * dmat
            dqn = _dot(g_qk, khat) + dqd * gam
            dkhat = ((b * gam) * db2 + _dot_tn(g_qk, qn) + _dot(dkkm, khat) + _dot_tn(dkkm, khat) + dkd * e)
            dvh = b * db1
            rkd = jnp.sum(dkd * kd, axis=1, keepdims=True)
            emat = ddm * dmat
            dgc = (dgam * gam - rkd + jnp.sum(emat, axis=1, keepdims=True)
                   - jnp.sum(emat.T, axis=1, keepdims=True))
            last = _iota2((BLK, 1), 0) == BLK - 1
            dgc = dgc + jnp.where(last, jnp.sum(rkd, axis=0, keepdims=True) + dgamlast * gamlast, 0.0)
            dgc_t = dgc_t + _put_col(dgc, 4 + h)
            dbeta_t = dbeta_t + _put_col(dbeta, h)
            dqhat = dqn * QK_SCALE
            dq = hc["rq"] * (dqhat - hc["qhat"] * jnp.sum(dqhat * hc["qhat"], axis=-1, keepdims=True))
            dk = hc["rk"] * (dkhat - khat * jnp.sum(dkhat * khat, axis=-1, keepdims=True))
            dcbuf[0:BLK, h * HD:(h + 1) * HD] = dq
            dcbuf[0:BLK, 512 + h * HD:512 + (h + 1) * HD] = dk
            dcbuf[0:BLK, 1024 + h * HD:1024 + (h + 1) * HD] = dvh
        dgv_ref[2:3, :] += dgng
        dgl = _dotf_tn(jnp.where(incl, 1.0, 0.0).astype(F32), dgc_t)
        sig_a = _sigmoid(gt + gv_ref[1:2, :])
        d_araw = dgl * neg_a * sig_a
        dgv_ref[0:1, :] += jnp.sum(dgl * gl, axis=0, keepdims=True)
        dgv_ref[1:2, :] += jnp.sum(d_araw, axis=0, keepdims=True)
        dp_ref[:, GATE0:NW] = (dbeta_t * beta * (1.0 - beta) + d_araw).astype(BF16)
        dcbuf[0:BLK, :] = dcbuf[0:BLK, :] * _silu_grad(conv)
        dqkv = cw_ref[0:1, :] * dcbuf[3:3 + BLK, :]
        dcw_ref[0:1, :] += jnp.sum(dcbuf[0:BLK, :] * xbuf[5:5 + BLK, :], axis=0, keepdims=True)
        for j in range(1, 4):
            dqkv = dqkv + cw_ref[j:j + 1, :] * dcbuf[3 - j:3 - j + BLK, :]
            dcw_ref[j:j + 1, :] += jnp.sum(dcbuf[0:BLK, :] * xbuf[5 + j:5 + j + BLK, :], axis=0, keepdims=True)
        dp_ref[:, 1024:2560] = dqkv.astype(BF16)
        dcbuf[BLK:BLK + 8, :] = dcbuf[0:8, :]

    rev = lambda w: pl.BlockSpec((BLK, w), lambda i: (nb - 1 - i, 0))
    st = pl.BlockSpec((None, HEADS, HD, HD), lambda i: (nb - 1 - i, 0, 0, 0))
    fix = lambda *shape: pl.BlockSpec((None,) + shape, lambda i: (l,) + (0,) * len(shape))
    acc = lambda *shape: pl.BlockSpec(shape, lambda i: (0,) * len(shape))
    return pl.pallas_call(
        body, name=f"mixer_bwd_{l}", grid=(nb,),
        out_shape=[jax.ShapeDtypeStruct((t, NW), BF16), jax.ShapeDtypeStruct((8, 512), F32),
                   jax.ShapeDtypeStruct((HEADS, HD, HD), F32), jax.ShapeDtypeStruct((HD, HD), F32),
                   jax.ShapeDtypeStruct((8, 1536), F32), jax.ShapeDtypeStruct((8, HD), F32)],
        in_specs=[rev(NW), rev(D), st, st, rev(512), rev(512), rev(512),
                  pl.BlockSpec((None, 8, 1536), lambda i: (nb - 1 - i, 0, 0)),
                  fix(8, 512), fix(HEADS, HD, HD), fix(HD, HD), fix(8, 1536), fix(8, HD)],
        out_specs=[rev(NW), acc(8, 512), acc(HEADS, HD, HD), acc(HD, HD), acc(8, 1536), acc(8, HD)],
        scratch_shapes=[pltpu.VMEM((HEADS, HD, HD), F32), pltpu.VMEM((BLK + 8, 1536), F32),
                        pltpu.VMEM((BLK + 8, 1536), F32)],
        compiler_params=_cparams(("arbitrary",)),
    )(p, dmix, s_sv, t_sv, u_sv, w_sv, o_sv, tail_sv, lnp, sgu_w, sgu_bt, cw, gv)


_SMALL = (("b_ada", 24), ("norm1_g", 4), ("norm2_g", 4), ("final_g", 1), ("sgu_ln_g", 2), ("sgu_ln_b", 2),
          ("sgu_w", 256), ("sgu_b", 2), ("conv_w", 24), ("a_log", 1), ("dt_bias", 1), ("gdn_norm_g", 1))
_SMALL_ROWS = sum(n for _, n in _SMALL)
_SMALL_PAD = 328
_DMOD_ROWS = 24


def _pack_rows(parts):
    rows = []
    for (name, n), a in zip(_SMALL, parts):
        flat = a.reshape(-1).astype(F32)
        rows.append(jnp.pad(flat, (0, n * D - flat.shape[0])).reshape(n, D))
    rows.append(jnp.zeros((_SMALL_PAD - _SMALL_ROWS, D), F32))
    return jnp.concatenate(rows, axis=0)


def _unpack_rows(buf, shapes):
    out, r0 = {}, 0
    for name, n in _SMALL:
        size = math.prod(shapes[name])
        out[name] = buf[r0:r0 + n].reshape(-1)[:size].reshape(shapes[name])
        r0 += n
    return out


def _pair_combine(own, sib):
    n = own.shape[0] - _DMOD_ROWS

    def body(a_ref, b_ref, o_ref):
        first = lax.axis_index("c") == 0
        a, b = a_ref[0:_DMOD_ROWS, :], b_ref[0:_DMOD_ROWS, :]
        o_ref[0:_DMOD_ROWS, :] = jnp.where(first, a, b)
        o_ref[_DMOD_ROWS:2 * _DMOD_ROWS, :] = jnp.where(first, b, a)
        o_ref[2 * _DMOD_ROWS:, :] = a_ref[_DMOD_ROWS:, :] + b_ref[_DMOD_ROWS:, :]

    return pl.pallas_call(
        body, name="small_pair_combine", out_shape=jax.ShapeDtypeStruct((2 * _DMOD_ROWS + n, D), F32),
        compiler_params=pltpu.CompilerParams(vmem_limit_bytes=VMEM_LIMIT),
    )(own, sib)


def _small_finalize(gathered, w, m, v):
    def body(g_ref, w_ref, m_ref, v_ref, go_ref, d_ref, nm_ref, nv_ref):
        sm = g_ref[0, 2 * _DMOD_ROWS:, :] + g_ref[1, 2 * _DMOD_ROWS:, :]
        sm = sm + g_ref[2, 2 * _DMOD_ROWS:, :]
        sm = sm + g_ref[3, 2 * _DMOD_ROWS:, :]
        bsum = jnp.zeros((_DMOD_ROWS, D), F32)
        for j in range(N_CHIPS):
            bsum = bsum + g_ref[j, 0:_DMOD_ROWS, :]
            bsum = bsum + g_ref[j, _DMOD_ROWS:2 * _DMOD_ROWS, :]
        go_ref[0:_DMOD_ROWS, :] = bsum
        go_ref[_DMOD_ROWS:, :] = sm[_DMOD_ROWS:, :]
        d_ref[...], nm_ref[...], nv_ref[...] = _adam_math(w_ref[...], go_ref[...], m_ref[...], v_ref[...])

    return pl.pallas_call(
        body, name="small_finalize", out_shape=[jax.ShapeDtypeStruct(w.shape, F32)] * 4,
        compiler_params=pltpu.CompilerParams(vmem_limit_bytes=VMEM_LIMIT),
    )(gathered, w, m, v)


def kernel(x, c, w_ada, b_ada, norm1_g, w_in, sgu_ln_g, sgu_ln_b, sgu_w, sgu_b, conv_w, a_log, dt_bias, gdn_norm_g, w_out, norm2_g, w_ff1, w_ff2, final_g, loss_target, m_w_ada, m_b_ada, m_norm1_g, m_w_in, m_sgu_ln_g, m_sgu_ln_b, m_sgu_w, m_sgu_b, m_conv_w, m_a_log, m_dt_bias, m_gdn_norm_g, m_w_out, m_norm2_g, m_w_ff1, m_w_ff2, m_final_g, v_w_ada, v_b_ada, v_norm1_g, v_w_in, v_sgu_ln_g, v_sgu_ln_b, v_sgu_w, v_sgu_b, v_conv_w, v_a_log, v_dt_bias, v_gdn_norm_g, v_w_out, v_norm2_g, v_w_ff1, v_w_ff2, v_final_g):
    xi, yi, ci = lax.axis_index("x"), lax.axis_index("y"), lax.axis_index("c")
    chip = 2 * xi + yi
    dev = 2 * chip + ci
    t = x.shape[1]
    x0 = x.reshape(t, D)
    target = loss_target.reshape(t, D)

    c_sib = _pair_exchange(c, "c_pair")
    c_pair = jnp.where(ci == 0, jnp.concatenate([c, c_sib], 0), jnp.concatenate([c_sib, c], 0))
    c_all = _chip_allgather(c_pair, "c_chips").reshape(8, D)
    ada_cols = w_ada.shape[2]
    b_cols = lax.dynamic_slice_in_dim(b_ada, chip * ada_cols, ada_cols, axis=1)
    mod_part = _ada_forward(c_all, w_ada, b_cols)
    conv_cols = conv_w.shape[2]
    packed = jnp.concatenate([mod_part.reshape(DEPTH * 8, ada_cols), conv_w.reshape(DEPTH, 4 * conv_cols)], axis=0)
    packed = _chip_allgather(packed, "mod_chips")
    mod_all = packed[:, :DEPTH * 8].reshape(N_CHIPS, DEPTH, 8, ada_cols)
    mod_mine = lax.dynamic_index_in_dim(mod_all, dev, axis=2, keepdims=False)
    mod = mod_mine.transpose(1, 0, 2).reshape(DEPTH, 6, D)
    modv = jnp.concatenate([mod, norm1_g[:, None, :], norm2_g[:, None, :]], axis=1)
    conv_full = packed[:, DEPTH * 8:].reshape(N_CHIPS, DEPTH, 4, conv_cols).transpose(1, 2, 0, 3).reshape(DEPTH, 4, 1536)

    place = jnp.stack([chip, ci]).astype(jnp.int32)
    g_in, g_out, g_ff1, g_ff2 = _weights_allgather([_cast_into_slot(w, place) for w in (w_in, w_out, w_ff1, w_ff2)])
    win = g_in.transpose(1, 2, 0, 3).reshape(DEPTH, D, IN_W)
    win = jnp.pad(win, ((0, 0), (0, 0), (0, NW - IN_W)))

    lnp = jnp.pad(jnp.stack([sgu_ln_g, sgu_ln_b], axis=1), ((0, 0), (0, 6), (0, 0)))
    sgu_bt = jnp.pad(sgu_b.transpose(0, 2, 1), ((0, 0), (0, 0), (0, HD - HEADS)))
    cw = jnp.pad(conv_full, ((0, 0), (0, 4), (0, 0)))
    lane_pad = lambda a: jnp.pad(a, ((0, 0), (4, HD - 8)))
    gv = jnp.pad(jnp.stack([lane_pad(a_log), lane_pad(dt_bias), gdn_norm_g], axis=1), ((0, 0), (0, 5), (0, 0)))

    acts = []
    xl = x0
    for l in range(DEPTH):
        p, h1 = _fwd_in(xl, modv, win, l)
        mix, *saved = _mixer_forward(p, lnp, sgu_w, sgu_bt, cw, gv, l)
        x1 = _fwd_out(xl, mix, modv, g_out, l)
        r, h2 = _fwd_ff1(x1, modv, g_ff1, l)
        x2 = _fwd_ff2(x1, r, modv, g_ff2, l)
        acts.append((xl, p, mix, saved, x1, r, h1, h2))
        xl = x2

    dx, head_stats = _loss_head(xl, target, final_g)
    loss = lax.psum(jnp.sum(head_stats[1, 0:1]), ("x", "y", "c"))
    d_final_g = head_stats[0]
    gb_in = jnp.zeros((DEPTH, D, NW), BF16)
    gb_out = jnp.zeros((DEPTH, N_CHIPS, D // N_CHIPS, D), BF16)
    gb_ff1 = jnp.zeros((DEPTH, N_CHIPS, D, D), BF16)
    gb_ff2 = jnp.zeros((DEPTH, N_CHIPS, D, D), BF16)
    dmod, small = [None] * DEPTH, [None] * DEPTH
    for l in reversed(range(DEPTH)):
        xl, p, mix, saved, x1, r, h1, h2 = acts[l]
        df = _bwd_ff2(dx, r, modv, g_ff2, l)
        gb_ff2, dg2 = _grad_weight(r, dx, gb_ff2, modv, l, "ff2", g_ff2)
        gb_ff1, _ = _grad_weight(h2, df, gb_ff1, modv, l, "ff1")
        dx1, st2 = _bwd_norm(df, g_ff1, x1, dx, modv, l, "ff1")
        dmix = _bwd_out(dx1, modv, g_out, l)
        gb_out, dg1 = _grad_weight(mix, dx1, gb_out, modv, l, "out", g_out)
        dp, dlnp, dsguw, dsgub, dcw, dgv = _mixer_backward(p, dmix, saved, lnp, sgu_w, sgu_bt, cw, gv, l)
        gb_in, _ = _grad_weight(h1, dp, gb_in, modv, l, "in")
        dx, st1 = _bwd_norm(dp, win, xl, dx1, modv, l, "in")
        dmod[l] = jnp.stack([st1[0], st1[1], dg1[0], st2[0], st2[1], dg2[0]], axis=0)
        small[l] = dict(norm1_g=st1[2], norm2_g=st2[2], sgu_ln_g=dlnp[0], sgu_ln_b=dlnp[1], sgu_w=dsguw,
                        sgu_b=dsgub[:, :HEADS].T, conv_w=dcw[:4], a_log=dgv[0, 4:8], dt_bias=dgv[1, 4:8],
                        gdn_norm_g=dgv[2])
    grad_x = dx.reshape(1, t, D)

    stack = lambda k: jnp.stack([small[l][k] for l in range(DEPTH)], axis=0)
    small_grads = [jnp.zeros((DEPTH, 6 * D), F32), stack("norm1_g"), stack("norm2_g"), d_final_g, stack("sgu_ln_g"),
                   stack("sgu_ln_b"), stack("sgu_w"), stack("sgu_b"), stack("conv_w"), stack("a_log"),
                   stack("dt_bias"), stack("gdn_norm_g")]
    own = jnp.concatenate([jnp.stack(dmod, axis=0).reshape(_DMOD_ROWS, D), _pack_rows(small_grads)], axis=0)
    sib = _pair_exchange(own, "small_pair")
    gathered = _chip_allgather(_pair_combine(own, sib), "small_chips")
    small_shapes = dict(b_ada=b_ada.shape, norm1_g=norm1_g.shape, norm2_g=norm2_g.shape, final_g=final_g.shape,
                        sgu_ln_g=sgu_ln_g.shape, sgu_ln_b=sgu_ln_b.shape, sgu_w=sgu_w.shape, sgu_b=sgu_b.shape,
                        conv_w=(DEPTH, 4, 1536), a_log=a_log.shape, dt_bias=dt_bias.shape,
                        gdn_norm_g=gdn_norm_g.shape)

    def full_conv(a):
        return lax.dynamic_update_slice_in_dim(jnp.zeros((DEPTH, 4, 1536), F32), a, chip * conv_cols, axis=2)

    def pack_state(b_, n1, n2, fg, lg, lb, sw, sb, cv, al, db, gn):
        return _pack_rows([b_, n1, n2, fg, lg, lb, sw, sb, full_conv(cv), al, db, gn])

    w_small = pack_state(b_ada, norm1_g, norm2_g, final_g, sgu_ln_g, sgu_ln_b, sgu_w, sgu_b, conv_w, a_log, dt_bias,
                         gdn_norm_g)
    m_small = pack_state(m_b_ada, m_norm1_g, m_norm2_g, m_final_g, m_sgu_ln_g, m_sgu_ln_b, m_sgu_w, m_sgu_b, m_conv_w,
                         m_a_log, m_dt_bias, m_gdn_norm_g)
    v_small = pack_state(v_b_ada, v_norm1_g, v_norm2_g, v_final_g, v_sgu_ln_g, v_sgu_ln_b, v_sgu_w, v_sgu_b, v_conv_w,
                         v_a_log, v_dt_bias, v_gdn_norm_g)
    small_out = _small_finalize(gathered, w_small, m_small, v_small)
    sg, sd, sm, sv = [_unpack_rows(a, small_shapes) for a in small_out]
    for dct in (sg, sd, sm, sv):
        dct["conv_w"] = lax.dynamic_slice_in_dim(dct["conv_w"], chip * conv_cols, conv_cols, axis=2)

    dmod_all = gathered[:, :2 * _DMOD_ROWS].reshape(8, DEPTH, 6 * D)
    dmod_cols = lax.dynamic_slice_in_dim(dmod_all, chip * ada_cols, ada_cols, axis=2).transpose(1, 0, 2)
    g_ada, d_ada, nm_ada, nv_ada = _ada_backward_adamw(c_all, dmod_cols, w_ada, m_w_ada, v_w_ada)

    gb_in_c = gb_in[:, :, :IN_W].reshape(DEPTH, D, N_CHIPS, IN_W // N_CHIPS).transpose(0, 2, 1, 3)
    partials = [gb_in_c, gb_out, gb_ff1, gb_ff2]
    from_sib = _grads_pair_send(partials)
    names = ("in", "out", "ff1", "ff2")
    pair = [_pair_sum(g, ga, place, f"pair_sum_{n}") for g, ga, n in zip(partials, from_sib, names)]
    recv = _grads_chip_exchange(pair)
    mine = [_chip_sum(p, rc, place, f"chip_sum_{n}") for p, rc, n in zip(pair, recv, names)]
    grads = _grads_pair_share(mine)
    big = {}
    for n, g, (w, m, v) in zip(names, grads, ((w_in, m_w_in, v_w_in), (w_out, m_w_out, v_w_out),
                                              (w_ff1, m_w_ff1, v_w_ff1), (w_ff2, m_w_ff2, v_w_ff2))):
        big[n] = (g,) + tuple(_adamw(w, g, m, v, f"adamw_{n}"))

    def outs(k):
        s = (sg, sd, sm, sv)[k]
        return [(g_ada, d_ada, nm_ada, nv_ada)[k], s["b_ada"], s["norm1_g"], big["in"][k], s["sgu_ln_g"],
                s["sgu_ln_b"], s["sgu_w"], s["sgu_b"], s["conv_w"], s["a_log"], s["dt_bias"], s["gdn_norm_g"],
                big["out"][k], s["norm2_g"], big["ff1"][k], big["ff2"][k], s["final_g"]]

    return (loss, grad_x, *outs(0), *outs(1), *outs(2), *outs(3))
```

```python
import functools
import math

import jax
import jax.numpy as jnp
from jax import lax
from jax.experimental import pallas as pl
from jax.experimental.pallas import tpu as pltpu

F32 = jnp.float32
BF16 = jnp.bfloat16

DEPTH = 4
D = 1024
HEADS = 4
HD = 128
BLK = 128
IN_W = 3080
NW = 3200
GATE0 = 3072
DFF = 4096
N_CHIPS = 4
RMS_EPS = 1e-6
LN_EPS = 1e-5
QK_SCALE = HD ** -0.5
LR, B1, B2, ADAM_EPS, WD, STEP = 0.001, 0.9, 0.999, 1e-08, 0.01, 10
VMEM_LIMIT = 56 * 1024 * 1024
MESH = pl.DeviceIdType.MESH
HOPS = ((1, 0), (0, 1), (1, 1))
HI = lax.Precision.HIGHEST


def _dot(a, b):
    return jnp.dot(a.astype(BF16), b.astype(BF16), preferred_element_type=F32)


def _dot_nt(a, b):
    return lax.dot_general(a.astype(BF16), b.astype(BF16), (((1,), (1,)), ((), ())), preferred_element_type=F32)


def _dot_tn(a, b):
    return lax.dot_general(a.astype(BF16), b.astype(BF16), (((0,), (0,)), ((), ())), preferred_element_type=F32)


def _dotf(a, b):
    return jnp.dot(a, b, precision=HI, preferred_element_type=F32)


def _split(a):
    hi = a.astype(BF16)
    return hi, (a - hi.astype(F32)).astype(BF16)


def _dg3(a, b, dims, batch=((), ())):
    ah, al = _split(a)
    bh, bl = _split(b)
    f = lambda x, y: lax.dot_general(x, y, (dims, batch), preferred_element_type=F32)
    return f(ah, bh) + (f(ah, bl) + f(al, bh))


def _bmm3(a, b):
    return _dg3(a, b, ((2,), (1,)), ((0,), (0,)))


def _d3(a, b):
    return _dg3(a, b, ((1,), (0,)))


def _d3_nt(a, b):
    return _dg3(a, b, ((1,), (1,)))


def _d3_tn(a, b):
    return _dg3(a, b, ((0,), (0,)))


def _dotf_tn(a, b):
    return lax.dot_general(a, b, (((0,), (0,)), ((), ())), precision=HI, preferred_element_type=F32)


def _sigmoid(x):
    return 1.0 / (1.0 + jnp.exp(-x))


def _softplus(x):
    return jnp.maximum(x, 0.0) + jnp.log(1.0 + jnp.exp(-jnp.abs(x)))


_G0 = math.sqrt(2.0 / math.pi)
_G1 = 0.044715


def _gelu(x):
    t = jnp.tanh(_G0 * (x + _G1 * x * x * x))
    return 0.5 * x * (1.0 + t)


def _gelu_grad(x):
    t = jnp.tanh(_G0 * (x + _G1 * x * x * x))
    return 0.5 * (1.0 + t) + 0.5 * x * (1.0 - t * t) * (_G0 * (1.0 + 3.0 * _G1 * x * x))


def _silu(x):
    return x * _sigmoid(x)


def _silu_grad(x):
    s = _sigmoid(x)
    return s * (1.0 + x * (1.0 - s))


def _rms_stats(x):
    rstd = lax.rsqrt(jnp.mean(x * x, axis=-1, keepdims=True) + RMS_EPS)
    return x * rstd, rstd


def _norm_mod(x, ng, sc, sh):
    xh, _ = _rms_stats(x)
    return xh * (ng * (1.0 + sc)) + sh


def _norm_mod_bwd(dh, x, ng, sc):
    xh, rstd = _rms_stats(x)
    dsh = jnp.sum(dh, axis=0, keepdims=True)
    dsc = jnp.sum(dh * xh, axis=0, keepdims=True) * ng
    dng = jnp.sum(dh * xh, axis=0, keepdims=True) * (1.0 + sc)
    dxh = dh * (ng * (1.0 + sc))
    dx = rstd * (dxh - xh * jnp.mean(dxh * xh, axis=-1, keepdims=True))
    return dx, dsh, dsc, dng


def _iota2(shape, axis):
    return lax.broadcasted_iota(jnp.int32, shape, axis)


def _col(tile, idx):
    return jnp.sum(jnp.where(_iota2(tile.shape, 1) == idx, tile, 0.0), axis=1, keepdims=True)


def _row(tile, idx):
    return jnp.sum(jnp.where(_iota2(tile.shape, 0) == idx, tile, 0.0), axis=0, keepdims=True)


def _put_col(col, idx, width=HD):
    shape = (col.shape[0], width)
    return jnp.where(_iota2(shape, 1) == idx, jnp.broadcast_to(col, shape), 0.0)


def _tri_inverse(m):
    rows, cols = _iota2(m.shape, m.ndim - 2), _iota2(m.shape, m.ndim - 1)
    mm = _bmm3 if m.ndim == 3 else _d3
    eye = jnp.where(rows == cols, 1.0, 0.0).astype(F32)
    n = jnp.where((rows >> 3) == (cols >> 3), -m, 0.0)
    p = eye + n
    n2 = mm(n, n)
    p = p + mm(n2, p)
    n4 = mm(n2, n2)
    p = p + mm(n4, p)
    for shift in (3, 4, 5, 6):
        same_pair = (rows >> (shift + 1)) == (cols >> (shift + 1))
        below = jnp.logical_and(((rows >> shift) & 1) == 1, ((cols >> shift) & 1) == 0)
        off = jnp.where(jnp.logical_and(same_pair, below), m, 0.0)
        p = p - mm(p, mm(off, p))
    return p


def _cparams(sem=None):
    return pltpu.CompilerParams(dimension_semantics=sem, vmem_limit_bytes=VMEM_LIMIT)


def _my_place():
    return lax.axis_index("x"), lax.axis_index("y"), lax.axis_index("c")


def _hop(xi, yi, hop):
    dx, dy = hop
    return (1 - xi if dx else xi), (1 - yi if dy else yi)


def _pair_exchange(x, name):
    def body(x_ref, o_ref, ssem, rsem):
        xi, yi, ci = _my_place()
        cp = pltpu.make_async_remote_copy(x_ref, o_ref, ssem, rsem, device_id=(xi, yi, 1 - ci), device_id_type=MESH)
        cp.start()
        cp.wait()

    return pl.pallas_call(
        body, name=name, out_shape=jax.ShapeDtypeStruct(x.shape, x.dtype),
        in_specs=[pl.BlockSpec(memory_space=pltpu.VMEM)], out_specs=pl.BlockSpec(memory_space=pltpu.VMEM),
        scratch_shapes=[pltpu.SemaphoreType.DMA, pltpu.SemaphoreType.DMA],
        compiler_params=pltpu.CompilerParams(vmem_limit_bytes=VMEM_LIMIT),
    )(x)


def _chip_allgather(x, name):
    def body(x_ref, o_ref, ssems, rsems, lsem):
        xi, yi, ci = _my_place()
        me = 2 * xi + yi
        loc = pltpu.make_async_copy(x_ref, o_ref.at[me], lsem)
        loc.start()
        sends = []
        for k, hop in enumerate(HOPS):
            tx, ty = _hop(xi, yi, hop)
            cp = pltpu.make_async_remote_copy(x_ref, o_ref.at[me], ssems.at[k], rsems.at[k],
                                              device_id=(tx, ty, ci), device_id_type=MESH)
            cp.start()
            sends.append(cp)
        for k, hop in enumerate(HOPS):
            tx, ty = _hop(xi, yi, hop)
            pltpu.make_async_remote_copy(x_ref, o_ref.at[2 * tx + ty], ssems.at[k], rsems.at[k],
                                         device_id=(tx, ty, ci), device_id_type=MESH).wait_recv()
        for cp in sends:
            cp.wait_send()
        loc.wait()

    return pl.pallas_call(
        body, name=name, out_shape=jax.ShapeDtypeStruct((N_CHIPS,) + x.shape, x.dtype),
        in_specs=[pl.BlockSpec(memory_space=pltpu.VMEM)], out_specs=pl.BlockSpec(memory_space=pltpu.VMEM),
        scratch_shapes=[pltpu.SemaphoreType.DMA((3,)), pltpu.SemaphoreType.DMA((3,)), pltpu.SemaphoreType.DMA],
        compiler_params=pltpu.CompilerParams(vmem_limit_bytes=VMEM_LIMIT),
    )(x)


def _hbm_specs(n):
    return [pl.BlockSpec(memory_space=pl.ANY)] * n


def _cast_into_slot(w, l, place):
    _, r, c = w.shape
    tr = _row_tile(r)

    def body(p_ref, w_ref, o_ref):
        o_ref[...] = w_ref[...].astype(BF16)

    return pl.pallas_call(
        body, name=f"cast_slot_{r}x{c}_{l}", out_shape=jax.ShapeDtypeStruct((N_CHIPS, r, c), BF16),
        grid_spec=pltpu.PrefetchScalarGridSpec(
            num_scalar_prefetch=1, grid=(r // tr,),
            in_specs=[pl.BlockSpec((None, tr, c), lambda k, pr: (l, k, 0))],
            out_specs=pl.BlockSpec((None, tr, c), lambda k, pr: (pr[0], k, 0))),
        compiler_params=_cparams(("parallel",)),
    )(place, w)


def _halves(ref, ci):
    half = ref.shape[-2] // 2
    return pl.ds(ci * half, half), pl.ds((1 - ci) * half, half)


def _gather_copies(bufs, sems):
    s_ici, r_ici, s_d2d, r_d2d = sems
    xi, yi, ci = _my_place()
    me = 2 * xi + yi
    ici_send, ici_recv, d2d_send, d2d_recv = [], [], [], []
    for i, buf in enumerate(bufs):
        mine, sibs = _halves(buf, ci)
        for k, hop in enumerate(HOPS):
            tx, ty = _hop(xi, yi, hop)
            src = 2 * tx + ty
            ici_send.append(pltpu.make_async_remote_copy(buf.at[me, mine], buf.at[me, mine], s_ici.at[i, k],
                                                         r_ici.at[i, k], device_id=(tx, ty, ci), device_id_type=MESH))
            ici_recv.append(pltpu.make_async_remote_copy(buf.at[src, mine], buf.at[src, mine], s_ici.at[i, k],
                                                         r_ici.at[i, k], device_id=(tx, ty, ci), device_id_type=MESH))
            d2d_send.append(pltpu.make_async_remote_copy(buf.at[src, mine], buf.at[src, mine], s_d2d.at[i, k],
                                                         r_d2d.at[i, k], device_id=(xi, yi, 1 - ci), device_id_type=MESH))
            d2d_recv.append(pltpu.make_async_remote_copy(buf.at[src, sibs], buf.at[src, sibs], s_d2d.at[i, k],
                                                         r_d2d.at[i, k], device_id=(xi, yi, 1 - ci), device_id_type=MESH))
    return ici_send, ici_recv, d2d_send, d2d_recv


def _gather_start(bufs, sems):
    for cp in _gather_copies(bufs, sems)[0]:
        cp.start()


def _gather_finish(bufs, sems):
    ici_send, ici_recv, d2d_send, d2d_recv = _gather_copies(bufs, sems)
    for arrived, forward in zip(ici_recv, d2d_send):
        arrived.wait_recv()
        forward.start()
    for cp in d2d_recv:
        cp.wait_recv()
    for cp in ici_send + d2d_send:
        cp.wait_send()


def _gather_sems(n):
    return [pltpu.SemaphoreType.DMA((n, 3))] * 4


def _weights_allgather(bufs, l):
    n = len(bufs)

    def body(*refs):
        outs, sems = refs[n:2 * n], refs[2 * n:]
        _gather_start(outs, sems)
        _gather_finish(outs, sems)

    return pl.pallas_call(
        body, name=f"weights_allgather_{l}",
        out_shape=[jax.ShapeDtypeStruct(b.shape, b.dtype) for b in bufs],
        in_specs=_hbm_specs(n), out_specs=_hbm_specs(n), input_output_aliases={i: i for i in range(n)},
        scratch_shapes=_gather_sems(n),
    )(*bufs)


def _grads_pair_send(gs, l):
    n = len(gs)

    def body(*refs):
        ins, outs, ssem, rsem = refs[:n], refs[n:2 * n], refs[2 * n], refs[2 * n + 1]
        xi, yi, ci = _my_place()
        every = pl.ds(0, N_CHIPS)
        cps = []
        for i in range(n):
            sibs = _halves(ins[i], ci)[1]
            cp = pltpu.make_async_remote_copy(ins[i].at[every, sibs], outs[i], ssem.at[i], rsem.at[i],
                                              device_id=(xi, yi, 1 - ci), device_id_type=MESH)
            cp.start()
            cps.append(cp)
        for cp in cps:
            cp.wait()

    return pl.pallas_call(
        body, name=f"grads_pair_send_{l}",
        out_shape=[jax.ShapeDtypeStruct((N_CHIPS, g.shape[1] // 2, g.shape[2]), g.dtype) for g in gs],
        in_specs=_hbm_specs(n), out_specs=_hbm_specs(n),
        scratch_shapes=[pltpu.SemaphoreType.DMA((n,)), pltpu.SemaphoreType.DMA((n,))],
    )(*gs)


def _exchange_copies(ps, recvs, sems):
    ssems, rsems = sems
    xi, yi, ci = _my_place()
    cps = []
    for i, (p, rc) in enumerate(zip(ps, recvs)):
        for k, hop in enumerate(HOPS):
            tx, ty = _hop(xi, yi, hop)
            cps.append(pltpu.make_async_remote_copy(p.at[2 * tx + ty], rc.at[k], ssems.at[i, k], rsems.at[i, k],
                                                    device_id=(tx, ty, ci), device_id_type=MESH))
    return cps


def _exchange_sems(n):
    return [pltpu.SemaphoreType.DMA((n, 3))] * 2


def _exchange_shapes(ps):
    return [jax.ShapeDtypeStruct((3,) + p.shape[1:], p.dtype) for p in ps]


def _grads_chip_exchange(ps, l):
    n = len(ps)

    def body(*refs):
        cps = _exchange_copies(refs[:n], refs[n:2 * n], refs[2 * n:])
        for cp in cps:
            cp.start()
        for cp in cps:
            cp.wait()

    return pl.pallas_call(
        body, name=f"grads_chip_exchange_{l}", out_shape=_exchange_shapes(ps),
        in_specs=_hbm_specs(n), out_specs=_hbm_specs(n), scratch_shapes=_exchange_sems(n),
    )(*ps)


def _grads_pair_share(gs):
    n = len(gs)

    def body(*refs):
        outs = refs[n:2 * n]
        ssem, rsem = refs[2 * n:]
        xi, yi, ci = _my_place()
        every = pl.ds(0, DEPTH)
        sends = []
        for i in range(n):
            mine = _halves(outs[i], ci)[0]
            cp = pltpu.make_async_remote_copy(outs[i].at[every, mine], outs[i].at[every, mine], ssem.at[i], rsem.at[i],
                                              device_id=(xi, yi, 1 - ci), device_id_type=MESH)
            cp.start()
            sends.append(cp)
        for i in range(n):
            sibs = _halves(outs[i], ci)[1]
            pltpu.make_async_remote_copy(outs[i].at[every, sibs], outs[i].at[every, sibs], ssem.at[i], rsem.at[i],
                                         device_id=(xi, yi, 1 - ci), device_id_type=MESH).wait_recv()
        for cp in sends:
            cp.wait_send()

    return pl.pallas_call(
        body, name="grads_pair_share",
        out_shape=[jax.ShapeDtypeStruct(g.shape, g.dtype) for g in gs],
        in_specs=_hbm_specs(n), out_specs=_hbm_specs(n), input_output_aliases={i: i for i in range(n)},
        scratch_shapes=[pltpu.SemaphoreType.DMA((n,)), pltpu.SemaphoreType.DMA((n,))],
    )(*gs)


def _row_tile(r):
    return min(r, 256)


def _pair_sum(g, ga, place, name):
    _, r, c = g.shape
    tr = _row_tile(r // 2)
    nk = r // 2 // tr

    def body(p_ref, g_ref, ga_ref, o_ref):
        o_ref[...] = (g_ref[...].astype(F32) + ga_ref[...].astype(F32)).astype(o_ref.dtype)

    return pl.pallas_call(
        body, name=name, out_shape=jax.ShapeDtypeStruct(ga.shape, ga.dtype),
        grid_spec=pltpu.PrefetchScalarGridSpec(
            num_scalar_prefetch=1, grid=(N_CHIPS, nk),
            in_specs=[pl.BlockSpec((None, tr, c), lambda j, k, pr: (j, pr[1] * nk + k, 0)),
                      pl.BlockSpec((None, tr, c), lambda j, k, pr: (j, k, 0))],
            out_specs=pl.BlockSpec((None, tr, c), lambda j, k, pr: (j, k, 0))),
        compiler_params=_cparams(("parallel", "parallel")),
    )(place, g, ga)


def _chip_sum(pair, recv, buf, l, place, name):
    _, rh, c = pair.shape
    tr = _row_tile(rh)
    nk = rh // tr

    def body(p_ref, own_ref, r_ref, *rest):
        o_ref = rest[-1]
        acc = own_ref[...].astype(F32) + r_ref[0].astype(F32)
        acc = acc + r_ref[1].astype(F32)
        o_ref[...] = acc + r_ref[2].astype(F32)

    in_specs = [pl.BlockSpec((None, tr, c), lambda k, pr: (pr[0], k, 0)),
                pl.BlockSpec((3, tr, c), lambda k, pr: (0, k, 0))]
    args = [pair, recv]
    aliases = {}
    if buf is not None:
        in_specs.append(pl.BlockSpec(memory_space=pl.ANY))
        args.append(buf)
        aliases = {3: 0}
    return pl.pallas_call(
        body, name=name, out_shape=jax.ShapeDtypeStruct((DEPTH, 2 * rh, c), F32),
        grid_spec=pltpu.PrefetchScalarGridSpec(
            num_scalar_prefetch=1, grid=(nk,), in_specs=in_specs,
            out_specs=pl.BlockSpec((None, tr, c), lambda k, pr: (l, pr[1] * nk + k, 0))),
        input_output_aliases=aliases, compiler_params=_cparams(("parallel",)),
    )(place, *args)


def _adam_math(w, g, m, v):
    m = B1 * m + (1.0 - B1) * g
    v = B2 * v + (1.0 - B2) * (g * g)
    m_hat = m / (1.0 - B1 ** STEP)
    v_hat = v / (1.0 - B2 ** STEP)
    delta = -LR * (m_hat / (jnp.sqrt(v_hat) + ADAM_EPS) + WD * w)
    return delta, m, v


def _adamw(w, g, m, v, name):
    n_l, r, c = w.shape
    tr = _row_tile(r)

    def body(w_ref, g_ref, m_ref, v_ref, d_ref, nm_ref, nv_ref):
        d_ref[...], nm_ref[...], nv_ref[...] = _adam_math(w_ref[...], g_ref[...], m_ref[...], v_ref[...])

    spec = pl.BlockSpec((None, tr, c), lambda i, k: (i, k, 0))
    return pl.pallas_call(
        body, name=name, out_shape=[jax.ShapeDtypeStruct(w.shape, F32)] * 3, grid=(n_l, r // tr),
        in_specs=[spec] * 4, out_specs=[spec] * 3, compiler_params=_cparams(("parallel", "parallel")),
    )(w, g, m, v)


def _ada_forward(c_all, w_ada, b_cols):
    cols = w_ada.shape[2]
    tn = 512

    def body(c_ref, w_ref, b_ref, o_ref):
        o_ref[...] = _dotf(_silu(c_ref[...]), w_ref[...]) + b_ref[...]

    return pl.pallas_call(
        body, name="ada_forward", out_shape=jax.ShapeDtypeStruct((DEPTH, 8, cols), F32), grid=(DEPTH, cols // tn),
        in_specs=[pl.BlockSpec((8, D), lambda l, j: (0, 0)),
                  pl.BlockSpec((None, D, tn), lambda l, j: (l, 0, j)),
                  pl.BlockSpec((None, 1, tn), lambda l, j: (l, 0, j))],
        out_specs=pl.BlockSpec((None, 8, tn), lambda l, j: (l, 0, j)),
        compiler_params=_cparams(("parallel", "parallel")),
    )(c_all, w_ada, b_cols.reshape(DEPTH, 1, cols))


def _ada_backward_adamw(c_all, dmod_cols, w, m, v):
    cols = w.shape[2]
    tn = 512

    def body(c_ref, d_ref, w_ref, m_ref, v_ref, g_ref, dl_ref, nm_ref, nv_ref):
        g = _dotf_tn(_silu(c_ref[...]), d_ref[...])
        g_ref[...] = g
        dl_ref[...], nm_ref[...], nv_ref[...] = _adam_math(w_ref[...], g, m_ref[...], v_ref[...])

    wspec = pl.BlockSpec((None, D, tn), lambda l, j: (l, 0, j))
    return pl.pallas_call(
        body, name="ada_backward_adamw", out_shape=[jax.ShapeDtypeStruct(w.shape, F32)] * 4, grid=(DEPTH, cols // tn),
        in_specs=[pl.BlockSpec((8, D), lambda l, j: (0, 0)), pl.BlockSpec((None, 8, tn), lambda l, j: (l, 0, j)),
                  wspec, wspec, wspec],
        out_specs=[wspec] * 4, compiler_params=_cparams(("parallel", "parallel")),
    )(c_all, dmod_cols, w, m, v)


def _tok_tile(t):
    return min(t, 256)


def _wspec4(r, c, l):
    return pl.BlockSpec((N_CHIPS, r, c), lambda i: (0, 0, 0))


def _fwd_in(x, modv, w_in, l):
    t = x.shape[0]
    tm = _tok_tile(t)

    def body(x_ref, mod_ref, w_ref, o_ref, h_ref):
        h = _norm_mod(x_ref[...], mod_ref[6:7, :], mod_ref[1:2, :], mod_ref[0:1, :]).astype(BF16)
        h_ref[...] = h
        o_ref[...] = jnp.dot(h, w_ref[...], preferred_element_type=F32)

    return pl.pallas_call(
        body, name=f"fwd_in_{l}", grid=(t // tm,),
        out_shape=[jax.ShapeDtypeStruct((t, NW), F32), jax.ShapeDtypeStruct((t, D), BF16)],
        in_specs=[pl.BlockSpec((tm, D), lambda i: (i, 0)), pl.BlockSpec((None, 8, D), lambda i: (l, 0, 0)),
                  pl.BlockSpec((D, NW), lambda i: (0, 0))],
        out_specs=[pl.BlockSpec((tm, NW), lambda i: (i, 0)), pl.BlockSpec((tm, D), lambda i: (i, 0))],
        compiler_params=_cparams(("parallel",)),
    )(x, modv, w_in)


def _fwd_out(x, mix, modv, w_out, l):
    t = x.shape[0]
    tm = _tok_tile(t)

    def body(x_ref, mix_ref, mod_ref, w_ref, o_ref):
        w = w_ref[...].reshape(D, D)
        o_ref[...] = x_ref[...] + mod_ref[2:3, :] * jnp.dot(mix_ref[...], w, preferred_element_type=F32)

    return pl.pallas_call(
        body, name=f"fwd_out_{l}", out_shape=jax.ShapeDtypeStruct((t, D), F32), grid=(t // tm,),
        in_specs=[pl.BlockSpec((tm, D), lambda i: (i, 0)), pl.BlockSpec((tm, D), lambda i: (i, 0)),
                  pl.BlockSpec((None, 8, D), lambda i: (l, 0, 0)), _wspec4(D // N_CHIPS, D, l)],
        out_specs=pl.BlockSpec((tm, D), lambda i: (i, 0)), compiler_params=_cparams(("parallel",)),
    )(x, mix, modv, w_out)


def _fwd_ff1(x, modv, w_ff1, l):
    t = x.shape[0]
    tm = _tok_tile(t)

    def body(x_ref, mod_ref, w_ref, o_ref, h_ref):
        h = _norm_mod(x_ref[...], mod_ref[7:8, :], mod_ref[4:5, :], mod_ref[3:4, :]).astype(BF16)
        h_ref[...] = h
        for j in range(N_CHIPS):
            f = jnp.dot(h, w_ref[j], preferred_element_type=F32)
            o_ref[:, j * D:(j + 1) * D] = jnp.maximum(f, 0.0).astype(BF16)

    return pl.pallas_call(
        body, name=f"fwd_ff1_{l}", grid=(t // tm,),
        out_shape=[jax.ShapeDtypeStruct((t, DFF), BF16), jax.ShapeDtypeStruct((t, D), BF16)],
        in_specs=[pl.BlockSpec((tm, D), lambda i: (i, 0)), pl.BlockSpec((None, 8, D), lambda i: (l, 0, 0)),
                  _wspec4(D, D, l)],
        out_specs=[pl.BlockSpec((tm, DFF), lambda i: (i, 0)), pl.BlockSpec((tm, D), lambda i: (i, 0))],
        compiler_params=_cparams(("parallel",)),
    )(x, modv, w_ff1)


def _fwd_ff2(x, r, modv, w_ff2, l):
    t = x.shape[0]
    tm = _tok_tile(t)

    def body(x_ref, r_ref, mod_ref, w_ref, o_ref):
        acc = jnp.zeros((tm, D), F32)
        for j in range(N_CHIPS):
            rj = r_ref[:, j * D:(j + 1) * D].astype(F32)
            acc = acc + jnp.dot((rj * rj).astype(BF16), w_ref[j], preferred_element_type=F32)
        o_ref[...] = x_ref[...] + mod_ref[5:6, :] * acc

    return pl.pallas_call(
        body, name=f"fwd_ff2_{l}", out_shape=jax.ShapeDtypeStruct((t, D), F32), grid=(t // tm,),
        in_specs=[pl.BlockSpec((tm, D), lambda i: (i, 0)), pl.BlockSpec((tm, DFF), lambda i: (i, 0)),
                  pl.BlockSpec((None, 8, D), lambda i: (l, 0, 0)), _wspec4(D, D, l)],
        out_specs=pl.BlockSpec((tm, D), lambda i: (i, 0)), compiler_params=_cparams(("parallel",)),
    )(x, r, modv, w_ff2)


def _loss_head(x, target, final_g):
    t = x.shape[0]
    tm = _tok_tile(t)

    def body(x_ref, t_ref, g_ref, dx_ref, st_ref):
        @pl.when(pl.program_id(0) == 0)
        def _():
            st_ref[...] = jnp.zeros_like(st_ref)

        xh, rstd = _rms_stats(x_ref[...])
        g = g_ref[...]
        err = xh * g - t_ref[...]
        loss = 0.5 * jnp.sum(jnp.mean(err * err, axis=-1, keepdims=True), axis=0, keepdims=True)
        dy = err * (1.0 / D)
        st_ref[0:1, :] += jnp.sum(dy * xh, axis=0, keepdims=True)
        st_ref[1:2, :] += jnp.broadcast_to(loss, (1, D))
        dxh = dy * g
        dx_ref[...] = rstd * (dxh - xh * jnp.mean(dxh * xh, axis=-1, keepdims=True))

    return pl.pallas_call(
        body, name="loss_head", out_shape=[jax.ShapeDtypeStruct((t, D), F32), jax.ShapeDtypeStruct((8, D), F32)],
        grid=(t // tm,),
        in_specs=[pl.BlockSpec((tm, D), lambda i: (i, 0)), pl.BlockSpec((tm, D), lambda i: (i, 0)),
                  pl.BlockSpec((1, D), lambda i: (0, 0))],
        out_specs=[pl.BlockSpec((tm, D), lambda i: (i, 0)), pl.BlockSpec((8, D), lambda i: (0, 0))],
        compiler_params=_cparams(("arbitrary",)),
    )(x, target, final_g.reshape(1, D))


def _bwd_ff2(dx2, r, modv, w_ff2, l):
    t = dx2.shape[0]
    tm = _tok_tile(t)

    def body(d_ref, r_ref, mod_ref, w_ref, o_ref):
        dyg = (d_ref[...] * mod_ref[5:6, :]).astype(BF16)
        for j in range(N_CHIPS):
            da = lax.dot_general(dyg, w_ref[j], (((1,), (1,)), ((), ())), preferred_element_type=F32)
            o_ref[:, j * D:(j + 1) * D] = (da * 2.0 * r_ref[:, j * D:(j + 1) * D].astype(F32)).astype(BF16)

    return pl.pallas_call(
        body, name=f"bwd_ff2_{l}", out_shape=jax.ShapeDtypeStruct((t, DFF), BF16), grid=(t // tm,),
        in_specs=[pl.BlockSpec((tm, D), lambda i: (i, 0)), pl.BlockSpec((tm, DFF), lambda i: (i, 0)),
                  pl.BlockSpec((None, 8, D), lambda i: (l, 0, 0)), _wspec4(D, D, l)],
        out_specs=pl.BlockSpec((tm, DFF), lambda i: (i, 0)), compiler_params=_cparams(("parallel",)),
    )(dx2, r, modv, w_ff2)


def _bwd_out(dx1, modv, w_out, l):
    t = dx1.shape[0]
    tm = _tok_tile(t)

    def body(d_ref, mod_ref, w_ref, o_ref):
        dyg = (d_ref[...] * mod_ref[2:3, :]).astype(BF16)
        w = w_ref[...].reshape(D, D)
        o_ref[...] = lax.dot_general(dyg, w, (((1,), (1,)), ((), ())), preferred_element_type=F32).astype(BF16)

    return pl.pallas_call(
        body, name=f"bwd_out_{l}", out_shape=jax.ShapeDtypeStruct((t, D), BF16), grid=(t // tm,),
        in_specs=[pl.BlockSpec((tm, D), lambda i: (i, 0)), pl.BlockSpec((None, 8, D), lambda i: (l, 0, 0)),
                  _wspec4(D // N_CHIPS, D, l)],
        out_specs=pl.BlockSpec((tm, D), lambda i: (i, 0)), compiler_params=_cparams(("parallel",)),
    )(dx1, modv, w_out)


def _bwd_norm(dy, w, x, dres, modv, l, which):
    t = x.shape[0]
    tm = _tok_tile(t)
    rows = (6, 1) if which == "in" else (7, 4)
    width = dy.shape[1]

    def body(dy_ref, w_ref, x_ref, dr_ref, mod_ref, dx_ref, st_ref):
        @pl.when(pl.program_id(0) == 0)
        def _():
            st_ref[...] = jnp.zeros_like(st_ref)

        if which == "in":
            dh = lax.dot_general(dy_ref[...], w_ref[...], (((1,), (1,)), ((), ())), preferred_element_type=F32)
        else:
            dh = jnp.zeros((tm, D), F32)
            for j in range(N_CHIPS):
                dh = dh + lax.dot_general(dy_ref[:, j * D:(j + 1) * D], w_ref[j], (((1,), (1,)), ((), ())),
                                          preferred_element_type=F32)
        ng, sc = mod_ref[rows[0]:rows[0] + 1, :], mod_ref[rows[1]:rows[1] + 1, :]
        dx, dsh, dsc, dng = _norm_mod_bwd(dh, x_ref[...], ng, sc)
        dx_ref[...] = dr_ref[...] + dx
        st_ref[0:1, :] += dsh
        st_ref[1:2, :] += dsc
        st_ref[2:3, :] += dng

    wspec = pl.BlockSpec((D, NW), lambda i: (0, 0)) if which == "in" else _wspec4(D, D, l)
    return pl.pallas_call(
        body, name=f"bwd_norm_{which}_{l}",
        out_shape=[jax.ShapeDtypeStruct((t, D), F32), jax.ShapeDtypeStruct((8, D), F32)], grid=(t // tm,),
        in_specs=[pl.BlockSpec((tm, width), lambda i: (i, 0)), wspec, pl.BlockSpec((tm, D), lambda i: (i, 0)),
                  pl.BlockSpec((tm, D), lambda i: (i, 0)), pl.BlockSpec((None, 8, D), lambda i: (l, 0, 0))],
        out_specs=[pl.BlockSpec((tm, D), lambda i: (i, 0)), pl.BlockSpec((8, D), lambda i: (0, 0))],
        compiler_params=_cparams(("arbitrary",)),
    )(dy, w, x, dres, modv)


def _grad_weight(lhs, rhs, modv, l, which, w_gate=None):
    t = lhs.shape[0]
    tm = min(t, 2048)
    nt = t // tm
    gated = which in ("out", "ff2")
    if which == "in":
        nj, lw, rw, orows, ocols = 5, D, NW // 5, D, NW // 5
    elif which == "ff1":
        nj, lw, rw, orows, ocols = N_CHIPS, D, D, D, D
    elif which == "out":
        nj, lw, rw, orows, ocols = N_CHIPS, D // N_CHIPS, D, D // N_CHIPS, D
    else:
        nj, lw, rw, orows, ocols = N_CHIPS, D, D, D, D
    gate_row = 2 if which == "out" else 5

    def body(*refs):
        if gated:
            l_ref, r_ref, mod_ref, wg_ref, o_ref, dg_ref, acc = refs
        else:
            l_ref, r_ref, mod_ref, o_ref, acc = refs
        j, k = pl.program_id(0), pl.program_id(1)

        @pl.when(k == 0)
        def _():
            acc[...] = jnp.zeros_like(acc)

        if which == "ff2":
            lv = l_ref[...].astype(F32)
            lv = lv * lv
        else:
            lv = l_ref[...]
        acc[...] += _dot_tn(lv, r_ref[...])

        if gated:
            @pl.when(jnp.logical_and(j == 0, k == 0))
            def _():
                dg_ref[...] = jnp.zeros_like(dg_ref)

        @pl.when(k == nt - 1)
        def _():
            raw = acc[...]
            if gated:
                o_ref[...] = (raw * mod_ref[gate_row:gate_row + 1, :]).astype(o_ref.dtype)
                dg_ref[0:1, :] += jnp.sum(raw * wg_ref[...].astype(F32), axis=0, keepdims=True)
            else:
                o_ref[...] = raw.astype(o_ref.dtype)

    if which in ("in", "ff1"):
        lspec = pl.BlockSpec((tm, lw), lambda j, k: (k, 0))
        rspec = pl.BlockSpec((tm, rw), lambda j, k: (k, j))
    else:
        lspec = pl.BlockSpec((tm, lw), lambda j, k: (k, j))
        rspec = pl.BlockSpec((tm, rw), lambda j, k: (k, 0))
    mspec = pl.BlockSpec((None, 8, D), lambda j, k: (l, 0, 0))
    if which == "in":
        ospec = pl.BlockSpec((orows, ocols), lambda j, k: (0, j))
        out_shape = [jax.ShapeDtypeStruct((D, NW), BF16)]
    else:
        ospec = pl.BlockSpec((None, orows, ocols), lambda j, k: (j, 0, 0))
        out_shape = [jax.ShapeDtypeStruct((N_CHIPS, orows, ocols), BF16)]
    in_specs = [lspec, rspec, mspec]
    args = [lhs, rhs, modv]
    out_specs = [ospec]
    if gated:
        in_specs.append(pl.BlockSpec((None, orows, ocols), lambda j, k: (j, 0, 0)))
        args.append(w_gate)
        out_specs.append(pl.BlockSpec((8, D), lambda j, k: (0, 0)))
        out_shape.append(jax.ShapeDtypeStruct((8, D), F32))
    res = pl.pallas_call(
        body, name=f"grad_w_{which}_{l}", out_shape=out_shape, grid=(nj, nt), in_specs=in_specs, out_specs=out_specs,
        scratch_shapes=[pltpu.VMEM((orows, ocols), F32)], compiler_params=_cparams(("arbitrary", "arbitrary")),
    )(*args)
    return (res[0], res[1]) if gated else (res[0], None)


def _tri_masks():
    rows, cols = _iota2((BLK, BLK), 0), _iota2((BLK, BLK), 1)
    return rows >= cols, rows > cols


def _sgu_forward(p_ref, lnp_ref, sguw_ref, sgub_ref):
    incl, _ = _tri_masks()
    ug = _gelu(p_ref[:, 0:512])
    vg = _gelu(p_ref[:, 512:1024])
    mu = jnp.mean(vg, axis=-1, keepdims=True)
    xc = vg - mu
    rstd = lax.rsqrt(jnp.mean(xc * xc, axis=-1, keepdims=True) + LN_EPS)
    vhat = xc * rstd
    vn = vhat * lnp_ref[0:1, :] + lnp_ref[1:2, :]
    bias = sgub_ref[...]
    ys, mixed, wms = [], [], []
    for h in range(HEADS):
        wm = jnp.where(incl, sguw_ref[h], 0.0)
        mx = _dot(wm, vn[:, h * HD:(h + 1) * HD]) + _col(bias, h)
        ys.append(ug[:, h * HD:(h + 1) * HD] * mx)
        mixed.append(mx)
        wms.append(wm)
    return ys, ug, vhat, rstd, vn, mixed, wms


def _conv_forward(xbuf, cw_ref):
    conv = cw_ref[0:1, :] * xbuf[5:5 + BLK, :]
    for j in range(1, 4):
        conv = conv + cw_ref[j:j + 1, :] * xbuf[5 + j:5 + j + BLK, :]
    return conv


def _gates(gt, gv_ref):
    incl, _ = _tri_masks()
    beta = _sigmoid(gt)
    neg_a = -jnp.exp(gv_ref[0:1, :])
    gl = neg_a * _softplus(gt + gv_ref[1:2, :])
    gc = _dotf(jnp.where(incl, 1.0, 0.0).astype(F32), gl)
    return beta, gl, gc, gc.T, neg_a


def _head_chunk(act, beta, gc, gct, h):
    incl, strict = _tri_masks()
    qh = act[:, h * HD:(h + 1) * HD]
    kh = act[:, 512 + h * HD:512 + (h + 1) * HD]
    vh = act[:, 1024 + h * HD:1024 + (h + 1) * HD]
    rq = lax.rsqrt(jnp.sum(qh * qh, axis=-1, keepdims=True) + RMS_EPS)
    rk = lax.rsqrt(jnp.sum(kh * kh, axis=-1, keepdims=True) + RMS_EPS)
    qhat, khat = qh * rq, kh * rk
    qn = qhat * QK_SCALE
    b = _col(beta, h)
    gcol = _col(gc, 4 + h)
    grow = _row(gct, 4 + h)
    dmat = jnp.where(incl, jnp.exp(jnp.where(incl, gcol - grow, 0.0)), 0.0)
    gam = jnp.exp(gcol)
    glast = _row(gcol, BLK - 1)
    e = jnp.exp(glast - gcol)
    kk = _d3_nt(khat, khat)
    return dict(qhat=qhat, khat=khat, qn=qn, vh=vh, rq=rq, rk=rk, b=b, dmat=dmat, gam=gam, glast=glast, e=e, kk=kk,
                strict=strict, incl=incl)


def _mixer_forward(p, lnp, sgu_w, sgu_bt, cw, gv, l, gather=()):
    t = p.shape[0]
    nb = t // BLK
    ng = len(gather)

    def body(*refs):
        p_ref, lnp_ref, sguw_ref, sgub_ref, cw_ref, gv_ref = refs[:6]
        mix_ref, s_out, t_out, u_out, w_out, o_out, tail_out = refs[6 + ng:13 + ng]
        gbufs = refs[13 + ng:13 + 2 * ng]
        s_scr, xbuf = refs[13 + 2 * ng:15 + 2 * ng]
        gsems = refs[15 + 2 * ng:]

        @pl.when(pl.program_id(0) == 0)
        def _():
            s_scr[...] = jnp.zeros_like(s_scr)
            xbuf[0:8, :] = jnp.zeros((8, 1536), F32)
            if ng:
                _gather_start(gbufs, gsems)

        ys = _sgu_forward(p_ref, lnp_ref, sguw_ref, sgub_ref)[0]
        for h in range(HEADS):
            mix_ref[:, h * HD:(h + 1) * HD] = ys[h].astype(BF16)

        tail_out[...] = xbuf[0:8, :]
        xbuf[8:8 + BLK, :] = p_ref[:, 1024:2560]
        act = _silu(_conv_forward(xbuf, cw_ref))
        xbuf[0:8, :] = xbuf[BLK:BLK + 8, :]
        beta, _, gc, gct, _ = _gates(p_ref[:, GATE0:NW], gv_ref)
        chunks = [_head_chunk(act, beta, gc, gct, h) for h in range(HEADS)]
        for h, hc in enumerate(chunks):
            t_out[h] = jnp.where(hc["strict"], hc["b"] * hc["kk"] * hc["dmat"], 0.0)
        t_out[...] = _tri_inverse(t_out[...])
        for h, hc in enumerate(chunks):
            tm = t_out[h]
            u = _d3(tm, hc["b"] * hc["vh"])
            w = _d3(tm, (hc["b"] * hc["gam"]) * hc["khat"])
            qkm = _dot_nt(hc["qn"], hc["khat"]) * hc["dmat"]
            s = s_scr[h]
            wn = u - _dot(w, s)
            o = _dot(hc["qn"] * hc["gam"], s) + _dot(qkm, wn)
            s_out[h] = s
            s_scr[h] = jnp.exp(hc["glast"]) * s + _dot_tn(hc["khat"] * hc["e"], wn)
            sl = slice(h * HD, (h + 1) * HD)
            u_out[:, sl] = u
            w_out[:, sl] = w
            o_out[:, sl] = o
            on = o * lax.rsqrt(jnp.mean(o * o, axis=-1, keepdims=True) + RMS_EPS) * gv_ref[2:3, :]
            mix_ref[:, 512 + h * HD:512 + (h + 1) * HD] = (on * _silu(p_ref[:, 2560 + h * HD:2560 + (h + 1) * HD])).astype(BF16)

        if ng:
            @pl.when(pl.program_id(0) == nb - 1)
            def _():
                _gather_finish(gbufs, gsems)

    tok = lambda w: pl.BlockSpec((BLK, w), lambda i: (i, 0))
    st = pl.BlockSpec((None, HEADS, HD, HD), lambda i: (i, 0, 0, 0))
    return pl.pallas_call(
        body, name=f"mixer_fwd_{l}", grid=(nb,),
        out_shape=[jax.ShapeDtypeStruct((t, D), BF16), jax.ShapeDtypeStruct((nb, HEADS, HD, HD), F32),
                   jax.ShapeDtypeStruct((nb, HEADS, HD, HD), F32), jax.ShapeDtypeStruct((t, 512), F32),
                   jax.ShapeDtypeStruct((t, 512), F32), jax.ShapeDtypeStruct((t, 512), F32),
                   jax.ShapeDtypeStruct((nb, 8, 1536), F32)]
        + [jax.ShapeDtypeStruct(b.shape, b.dtype) for b in gather],
        in_specs=[tok(NW), pl.BlockSpec((None, 8, 512), lambda i: (l, 0, 0)),
                  pl.BlockSpec((None, HEADS, HD, HD), lambda i: (l, 0, 0, 0)),
                  pl.BlockSpec((None, HD, HD), lambda i: (l, 0, 0)), pl.BlockSpec((None, 8, 1536), lambda i: (l, 0, 0)),
                  pl.BlockSpec((None, 8, HD), lambda i: (l, 0, 0))] + _hbm_specs(ng),
        out_specs=[tok(D), st, st, tok(512), tok(512), tok(512), pl.BlockSpec((None, 8, 1536), lambda i: (i, 0, 0))]
        + _hbm_specs(ng),
        input_output_aliases={6 + i: 7 + i for i in range(ng)},
        scratch_shapes=[pltpu.VMEM((HEADS, HD, HD), F32), pltpu.VMEM((BLK + 8, 1536), F32)]
        + (_gather_sems(ng) if ng else []),
        compiler_params=_cparams(("arbitrary",)),
    )(p, lnp, sgu_w, sgu_bt, cw, gv, *gather)


def _mixer_backward(p, dmix, saved, lnp, sgu_w, sgu_bt, cw, gv, l, exchange=()):
    t = p.shape[0]
    nb = t // BLK
    s_sv, t_sv, u_sv, w_sv, o_sv, tail_sv = saved
    ne = len(exchange)

    def body(*refs):
        (p_ref, dmix_ref, s_ref, t_ref, u_ref, w_ref, o_ref, tail_ref, lnp_ref, sguw_ref, sgub_ref, cw_ref,
         gv_ref) = refs[:13]
        pairs = refs[13:13 + ne]
        dp_ref, dlnp_ref, dsguw_ref, dsgub_ref, dcw_ref, dgv_ref = refs[13 + ne:19 + ne]
        recvs = refs[19 + ne:19 + 2 * ne]
        ds_scr, xbuf, dcbuf = refs[19 + 2 * ne:22 + 2 * ne]
        esems = refs[22 + 2 * ne:]

        @pl.when(pl.program_id(0) == 0)
        def _():
            if ne:
                for cp in _exchange_copies(pairs, recvs, esems):
                    cp.start()
            ds_scr[...] = jnp.zeros_like(ds_scr)
            dcbuf[BLK:BLK + 8, :] = jnp.zeros((8, 1536), F32)
            dlnp_ref[...] = jnp.zeros_like(dlnp_ref)
            dsguw_ref[...] = jnp.zeros_like(dsguw_ref)
            dsgub_ref[...] = jnp.zeros_like(dsgub_ref)
            dcw_ref[...] = jnp.zeros_like(dcw_ref)
            dgv_ref[...] = jnp.zeros_like(dgv_ref)

        incl, strict = _tri_masks()
        _, ug, vhat, rstd, vn, mixed, wms = _sgu_forward(p_ref, lnp_ref, sguw_ref, sgub_ref)
        dvn_parts, dug_parts = [], []
        dbias = jnp.zeros((BLK, HD), F32)
        for h in range(HEADS):
            sl = slice(h * HD, (h + 1) * HD)
            dy = dmix_ref[:, sl].astype(F32)
            dmx = dy * ug[:, sl]
            dug_parts.append(dy * mixed[h])
            dsguw_ref[h] += jnp.where(incl, _dot_nt(dmx, vn[:, sl]), 0.0)
            dbias = dbias + _put_col(jnp.sum(dmx, axis=1, keepdims=True), h)
            dvn_parts.append(_dot_tn(wms[h], dmx))
        dsgub_ref[...] += dbias
        dvn = jnp.concatenate(dvn_parts, axis=1)
        dug = jnp.concatenate(dug_parts, axis=1)
        dlnp_ref[0:1, :] += jnp.sum(dvn * vhat, axis=0, keepdims=True)
        dlnp_ref[1:2, :] += jnp.sum(dvn, axis=0, keepdims=True)
        dvhat = dvn * lnp_ref[0:1, :]
        dvg = rstd * (dvhat - jnp.mean(dvhat, axis=-1, keepdims=True)
                      - vhat * jnp.mean(dvhat * vhat, axis=-1, keepdims=True))
        dp_ref[:, 0:512] = (dug * _gelu_grad(p_ref[:, 0:512])).astype(BF16)
        dp_ref[:, 512:1024] = (dvg * _gelu_grad(p_ref[:, 512:1024])).astype(BF16)

        xbuf[0:8, :] = tail_ref[...]
        xbuf[8:8 + BLK, :] = p_ref[:, 1024:2560]
        conv = _conv_forward(xbuf, cw_ref)
        act = _silu(conv)
        gt = p_ref[:, GATE0:NW]
        beta, gl, gc, gct, neg_a = _gates(gt, gv_ref)
        gng = gv_ref[2:3, :]
        dbeta_t = jnp.zeros((BLK, HD), F32)
        dgc_t = jnp.zeros((BLK, HD), F32)
        dgng = jnp.zeros((1, HD), F32)
        for h in range(HEADS):
            sl = slice(h * HD, (h + 1) * HD)
            hc = _head_chunk(act, beta, gc, gct, h)
            b, gam, e, dmat, kk = hc["b"], hc["gam"], hc["e"], hc["dmat"], hc["kk"]
            qn, khat, vh = hc["qn"], hc["khat"], hc["vh"]
            gamlast = jnp.exp(hc["glast"])
            s, tm, u, w, o = s_ref[h], t_ref[h], u_ref[:, sl], w_ref[:, sl], o_ref[:, sl]
            ds_next = ds_scr[h]
            z = p_ref[:, 2560 + h * HD:2560 + (h + 1) * HD]
            dy = dmix_ref[:, 512 + h * HD:512 + (h + 1) * HD].astype(F32)
            ro = lax.rsqrt(jnp.mean(o * o, axis=-1, keepdims=True) + RMS_EPS)
            ohat = o * ro
            dp_ref[:, 2560 + h * HD:2560 + (h + 1) * HD] = (dy * ohat * gng * _silu_grad(z)).astype(BF16)
            don = dy * _silu(z)
            dgng = dgng + jnp.sum(don * ohat, axis=0, keepdims=True)
            dohat = don * gng
            do = ro * (dohat - ohat * jnp.mean(dohat * ohat, axis=-1, keepdims=True))
            qk_raw = _dot_nt(qn, khat)
            qkm = qk_raw * dmat
            qd, kd = qn * gam, khat * e
            wn = u - _dot(w, s)
            dwn = _dot_tn(qkm, do) + _dot(kd, ds_next)
            dqd = _dot_nt(do, s)
            dqkm = jnp.where(incl, _dot_nt(do, wn), 0.0)
            ds_scr[h] = _dot_tn(qd, do) + gamlast * ds_next - _dot_tn(w, dwn)
            dgamlast = jnp.sum(jnp.sum(ds_next * s, axis=1, keepdims=True), axis=0, keepdims=True)
            dkd = _dot_nt(wn, ds_next)
            dw = -_dot_nt(dwn, s)
            db1 = _d3_tn(tm, dwn)
            db2 = _d3_tn(tm, dw)
            dm = jnp.where(strict, -(_dot_nt(db1, u) + _dot_nt(db2, w)), 0.0)
            dbeta = (jnp.sum(dm * kk * dmat, axis=1, keepdims=True) + jnp.sum(db1 * vh, axis=1, keepdims=True)
                     + gam * jnp.sum(db2 * khat, axis=1, keepdims=True))
            dkkm = dm * b * dmat
            ddm = dm * b * kk + dqkm * qk_raw
            dgam = b * jnp.sum(db2 * khat, axis=1, keepdims=True) + jnp.sum(dqd * qn, axis=1, keepdims=True)
            g_qk = dqkm * dmat
            dqn = _dot(g_qk, khat) + dqd * gam
            dkhat = ((b * gam) * db2 + _dot_tn(g_qk, qn) + _dot(dkkm, khat) + _dot_tn(dkkm, khat) + dkd * e)
            dvh = b * db1
            rkd = jnp.sum(dkd * kd, axis=1, keepdims=True)
            emat = ddm * dmat
            dgc = (dgam * gam - rkd + jnp.sum(emat, axis=1, keepdims=True)
                   - jnp.sum(emat.T, axis=1, keepdims=True))
            last = _iota2((BLK, 1), 0) == BLK - 1
            dgc = dgc + jnp.where(last, jnp.sum(rkd, axis=0, keepdims=True) + dgamlast * gamlast, 0.0)
            dgc_t = dgc_t + _put_col(dgc, 4 + h)
            dbeta_t = dbeta_t + _put_col(dbeta, h)
            dqhat = dqn * QK_SCALE
            dq = hc["rq"] * (dqhat - hc["qhat"] * jnp.sum(dqhat * hc["qhat"], axis=-1, keepdims=True))
            dk = hc["rk"] * (dkhat - khat * jnp.sum(dkhat * khat, axis=-1, keepdims=True))
            dcbuf[0:BLK, h * HD:(h + 1) * HD] = dq
            dcbuf[0:BLK, 512 + h * HD:512 + (h + 1) * HD] = dk
            dcbuf[0:BLK, 1024 + h * HD:1024 + (h + 1) * HD] = dvh
        dgv_ref[2:3, :] += dgng
        dgl = _dotf_tn(jnp.where(incl, 1.0, 0.0).astype(F32), dgc_t)
        sig_a = _sigmoid(gt + gv_ref[1:2, :])
        d_araw = dgl * neg_a * sig_a
        dgv_ref[0:1, :] += jnp.sum(dgl * gl, axis=0, keepdims=True)
        dgv_ref[1:2, :] += jnp.sum(d_araw, axis=0, keepdims=True)
        dp_ref[:, GATE0:NW] = (dbeta_t * beta * (1.0 - beta) + d_araw).astype(BF16)
        dcbuf[0:BLK, :] = dcbuf[0:BLK, :] * _silu_grad(conv)
        dqkv = cw_ref[0:1, :] * dcbuf[3:3 + BLK, :]
        dcw_ref[0:1, :] += jnp.sum(dcbuf[0:BLK, :] * xbuf[5:5 + BLK, :], axis=0, keepdims=True)
        for j in range(1, 4):
            dqkv = dqkv + cw_ref[j:j + 1, :] * dcbuf[3 - j:3 - j + BLK, :]
            dcw_ref[j:j + 1, :] += jnp.sum(dcbuf[0:BLK, :] * xbuf[5 + j:5 + j + BLK, :], axis=0, keepdims=True)
        dp_ref[:, 1024:2560] = dqkv.astype(BF16)
        dcbuf[BLK:BLK + 8, :] = dcbuf[0:8, :]

        if ne:
            @pl.when(pl.program_id(0) == nb - 1)
            def _():
                for cp in _exchange_copies(pairs, recvs, esems):
                    cp.wait()

    rev = lambda w: pl.BlockSpec((BLK, w), lambda i: (nb - 1 - i, 0))
    st = pl.BlockSpec((None, HEADS, HD, HD), lambda i: (nb - 1 - i, 0, 0, 0))
    fix = lambda *shape: pl.BlockSpec((None,) + shape, lambda i: (l,) + (0,) * len(shape))
    acc = lambda *shape: pl.BlockSpec(shape, lambda i: (0,) * len(shape))
    return pl.pallas_call(
        body, name=f"mixer_bwd_{l}", grid=(nb,),
        out_shape=[jax.ShapeDtypeStruct((t, NW), BF16), jax.ShapeDtypeStruct((8, 512), F32),
                   jax.ShapeDtypeStruct((HEADS, HD, HD), F32), jax.ShapeDtypeStruct((HD, HD), F32),
                   jax.ShapeDtypeStruct((8, 1536), F32), jax.ShapeDtypeStruct((8, HD), F32)]
        + _exchange_shapes(exchange),
        in_specs=[rev(NW), rev(D), st, st, rev(512), rev(512), rev(512),
                  pl.BlockSpec((None, 8, 1536), lambda i: (nb - 1 - i, 0, 0)),
                  fix(8, 512), fix(HEADS, HD, HD), fix(HD, HD), fix(8, 1536), fix(8, HD)] + _hbm_specs(ne),
        out_specs=[rev(NW), acc(8, 512), acc(HEADS, HD, HD), acc(HD, HD), acc(8, 1536), acc(8, HD)]
        + _hbm_specs(ne),
        scratch_shapes=[pltpu.VMEM((HEADS, HD, HD), F32), pltpu.VMEM((BLK + 8, 1536), F32),
                        pltpu.VMEM((BLK + 8, 1536), F32)] + (_exchange_sems(ne) if ne else []),
        compiler_params=_cparams(("arbitrary",)),
    )(p, dmix, s_sv, t_sv, u_sv, w_sv, o_sv, tail_sv, lnp, sgu_w, sgu_bt, cw, gv, *exchange)


_SMALL = (("b_ada", 24), ("norm1_g", 4), ("norm2_g", 4), ("final_g", 1), ("sgu_ln_g", 2), ("sgu_ln_b", 2),
          ("sgu_w", 256), ("sgu_b", 2), ("conv_w", 24), ("a_log", 1), ("dt_bias", 1), ("gdn_norm_g", 1))
_SMALL_ROWS = sum(n for _, n in _SMALL)
_SMALL_PAD = 328
_DMOD_ROWS = 24


def _pack_rows(parts):
    rows = []
    for (name, n), a in zip(_SMALL, parts):
        flat = a.reshape(-1).astype(F32)
        rows.append(jnp.pad(flat, (0, n * D - flat.shape[0])).reshape(n, D))
    rows.append(jnp.zeros((_SMALL_PAD - _SMALL_ROWS, D), F32))
    return jnp.concatenate(rows, axis=0)


def _unpack_rows(buf, shapes):
    out, r0 = {}, 0
    for name, n in _SMALL:
        size = math.prod(shapes[name])
        out[name] = buf[r0:r0 + n].reshape(-1)[:size].reshape(shapes[name])
        r0 += n
    return out


def _pair_combine(own, sib):
    n = own.shape[0] - _DMOD_ROWS

    def body(a_ref, b_ref, o_ref):
        first = lax.axis_index("c") == 0
        a, b = a_ref[0:_DMOD_ROWS, :], b_ref[0:_DMOD_ROWS, :]
        o_ref[0:_DMOD_ROWS, :] = jnp.where(first, a, b)
        o_ref[_DMOD_ROWS:2 * _DMOD_ROWS, :] = jnp.where(first, b, a)
        o_ref[2 * _DMOD_ROWS:, :] = a_ref[_DMOD_ROWS:, :] + b_ref[_DMOD_ROWS:, :]

    return pl.pallas_call(
        body, name="small_pair_combine", out_shape=jax.ShapeDtypeStruct((2 * _DMOD_ROWS + n, D), F32),
        compiler_params=pltpu.CompilerParams(vmem_limit_bytes=VMEM_LIMIT),
    )(own, sib)


def _small_finalize(gathered, w, m, v):
    def body(g_ref, w_ref, m_ref, v_ref, go_ref, d_ref, nm_ref, nv_ref):
        sm = g_ref[0, 2 * _DMOD_ROWS:, :] + g_ref[1, 2 * _DMOD_ROWS:, :]
        sm = sm + g_ref[2, 2 * _DMOD_ROWS:, :]
        sm = sm + g_ref[3, 2 * _DMOD_ROWS:, :]
        bsum = jnp.zeros((_DMOD_ROWS, D), F32)
        for j in range(N_CHIPS):
            bsum = bsum + g_ref[j, 0:_DMOD_ROWS, :]
            bsum = bsum + g_ref[j, _DMOD_ROWS:2 * _DMOD_ROWS, :]
        go_ref[0:_DMOD_ROWS, :] = bsum
        go_ref[_DMOD_ROWS:, :] = sm[_DMOD_ROWS:, :]
        d_ref[...], nm_ref[...], nv_ref[...] = _adam_math(w_ref[...], go_ref[...], m_ref[...], v_ref[...])

    return pl.pallas_call(
        body, name="small_finalize", out_shape=[jax.ShapeDtypeStruct(w.shape, F32)] * 4,
        compiler_params=pltpu.CompilerParams(vmem_limit_bytes=VMEM_LIMIT),
    )(gathered, w, m, v)


def kernel(x, c, w_ada, b_ada, norm1_g, w_in, sgu_ln_g, sgu_ln_b, sgu_w, sgu_b, conv_w, a_log, dt_bias, gdn_norm_g, w_out, norm2_g, w_ff1, w_ff2, final_g, loss_target, m_w_ada, m_b_ada, m_norm1_g, m_w_in, m_sgu_ln_g, m_sgu_ln_b, m_sgu_w, m_sgu_b, m_conv_w, m_a_log, m_dt_bias, m_gdn_norm_g, m_w_out, m_norm2_g, m_w_ff1, m_w_ff2, m_final_g, v_w_ada, v_b_ada, v_norm1_g, v_w_in, v_sgu_ln_g, v_sgu_ln_b, v_sgu_w, v_sgu_b, v_conv_w, v_a_log, v_dt_bias, v_gdn_norm_g, v_w_out, v_norm2_g, v_w_ff1, v_w_ff2, v_final_g):
    xi, yi, ci = lax.axis_index("x"), lax.axis_index("y"), lax.axis_index("c")
    chip = 2 * xi + yi
    dev = 2 * chip + ci
    t = x.shape[1]
    x0 = x.reshape(t, D)
    target = loss_target.reshape(t, D)

    c_sib = _pair_exchange(c, "c_pair")
    c_pair = jnp.where(ci == 0, jnp.concatenate([c, c_sib], 0), jnp.concatenate([c_sib, c], 0))
    c_all = _chip_allgather(c_pair, "c_chips").reshape(8, D)
    ada_cols = w_ada.shape[2]
    b_cols = lax.dynamic_slice_in_dim(b_ada, chip * ada_cols, ada_cols, axis=1)
    mod_part = _ada_forward(c_all, w_ada, b_cols)
    conv_cols = conv_w.shape[2]
    packed = jnp.concatenate([mod_part.reshape(DEPTH * 8, ada_cols), conv_w.reshape(DEPTH, 4 * conv_cols)], axis=0)
    packed = _chip_allgather(packed, "mod_chips")
    mod_all = packed[:, :DEPTH * 8].reshape(N_CHIPS, DEPTH, 8, ada_cols)
    mod_mine = lax.dynamic_index_in_dim(mod_all, dev, axis=2, keepdims=False)
    mod = mod_mine.transpose(1, 0, 2).reshape(DEPTH, 6, D)
    modv = jnp.concatenate([mod, norm1_g[:, None, :], norm2_g[:, None, :]], axis=1)
    conv_full = packed[:, DEPTH * 8:].reshape(N_CHIPS, DEPTH, 4, conv_cols).transpose(1, 2, 0, 3).reshape(DEPTH, 4, 1536)

    place = jnp.stack([chip, ci]).astype(jnp.int32)
    wbufs = [[_cast_into_slot(w, l, place) for w in (w_in, w_out, w_ff1, w_ff2)] for l in range(DEPTH)]
    wbufs[0] = _weights_allgather(wbufs[0], 0)

    def full_w_in(g):
        return jnp.pad(g.transpose(1, 0, 2).reshape(D, IN_W), ((0, 0), (0, NW - IN_W)))

    lnp = jnp.pad(jnp.stack([sgu_ln_g, sgu_ln_b], axis=1), ((0, 0), (0, 6), (0, 0)))
    sgu_bt = jnp.pad(sgu_b.transpose(0, 2, 1), ((0, 0), (0, 0), (0, HD - HEADS)))
    cw = jnp.pad(conv_full, ((0, 0), (0, 4), (0, 0)))
    lane_pad = lambda a: jnp.pad(a, ((0, 0), (4, HD - 8)))
    gv = jnp.pad(jnp.stack([lane_pad(a_log), lane_pad(dt_bias), gdn_norm_g], axis=1), ((0, 0), (0, 5), (0, 0)))

    acts = []
    xl = x0
    for l in range(DEPTH):
        g_in, g_out, g_ff1, g_ff2 = wbufs[l]
        win = full_w_in(g_in)
        p, h1 = _fwd_in(xl, modv, win, l)
        nxt = wbufs[l + 1] if l + 1 < DEPTH else ()
        mix, *rest = _mixer_forward(p, lnp, sgu_w, sgu_bt, cw, gv, l, gather=nxt)
        saved = rest[:6]
        if nxt:
            wbufs[l + 1] = rest[6:]
        x1 = _fwd_out(xl, mix, modv, g_out, l)
        r, h2 = _fwd_ff1(x1, modv, g_ff1, l)
        x2 = _fwd_ff2(x1, r, modv, g_ff2, l)
        acts.append((xl, p, mix, saved, x1, r, h1, h2, win))
        xl = x2

    dx, head_stats = _loss_head(xl, target, final_g)
    loss = lax.psum(jnp.sum(head_stats[1, 0:1]), ("x", "y", "c"))
    d_final_g = head_stats[0]
    names = ("in", "out", "ff1", "ff2")
    grads_buf = [None] * len(names)
    pending = None

    def reduce_into_buffers(lay, pair, recv):
        for i, n in enumerate(names):
            grads_buf[i] = _chip_sum(pair[i], recv[i], grads_buf[i], lay, place, f"chip_sum_{n}_{lay}")

    dmod, small = [None] * DEPTH, [None] * DEPTH
    for l in reversed(range(DEPTH)):
        xl, p, mix, saved, x1, r, h1, h2, win = acts[l]
        g_in, g_out, g_ff1, g_ff2 = wbufs[l]
        df = _bwd_ff2(dx, r, modv, g_ff2, l)
        gw_ff2, dg2 = _grad_weight(r, dx, modv, l, "ff2", g_ff2)
        gw_ff1, _ = _grad_weight(h2, df, modv, l, "ff1")
        dx1, st2 = _bwd_norm(df, g_ff1, x1, dx, modv, l, "ff1")
        dmix = _bwd_out(dx1, modv, g_out, l)
        gw_out, dg1 = _grad_weight(mix, dx1, modv, l, "out", g_out)
        exch = pending[1] if pending else ()
        dp, dlnp, dsguw, dsgub, dcw, dgv, *recv = _mixer_backward(p, dmix, saved, lnp, sgu_w, sgu_bt, cw, gv, l,
                                                                  exchange=exch)
        if pending:
            reduce_into_buffers(pending[0], pending[1], recv)
        gw_in, _ = _grad_weight(h1, dp, modv, l, "in")
        dx, st1 = _bwd_norm(dp, win, xl, dx1, modv, l, "in")
        gw_in_c = gw_in[:, :IN_W].reshape(D, N_CHIPS, IN_W // N_CHIPS).transpose(1, 0, 2)
        partials = [gw_in_c, gw_out, gw_ff1, gw_ff2]
        from_sib = _grads_pair_send(partials, l)
        pending = (l, [_pair_sum(g, ga, place, f"pair_sum_{n}_{l}") for g, ga, n in zip(partials, from_sib, names)])
        dmod[l] = jnp.stack([st1[0], st1[1], dg1[0], st2[0], st2[1], dg2[0]], axis=0)
        small[l] = dict(norm1_g=st1[2], norm2_g=st2[2], sgu_ln_g=dlnp[0], sgu_ln_b=dlnp[1], sgu_w=dsguw,
                        sgu_b=dsgub[:, :HEADS].T, conv_w=dcw[:4], a_log=dgv[0, 4:8], dt_bias=dgv[1, 4:8],
                        gdn_norm_g=dgv[2])
    grad_x = dx.reshape(1, t, D)

    stack = lambda k: jnp.stack([small[l][k] for l in range(DEPTH)], axis=0)
    small_grads = [jnp.zeros((DEPTH, 6 * D), F32), stack("norm1_g"), stack("norm2_g"), d_final_g, stack("sgu_ln_g"),
                   stack("sgu_ln_b"), stack("sgu_w"), stack("sgu_b"), stack("conv_w"), stack("a_log"),
                   stack("dt_bias"), stack("gdn_norm_g")]
    own = jnp.concatenate([jnp.stack(dmod, axis=0).reshape(_DMOD_ROWS, D), _pack_rows(small_grads)], axis=0)
    sib = _pair_exchange(own, "small_pair")
    gathered = _chip_allgather(_pair_combine(own, sib), "small_chips")
    small_shapes = dict(b_ada=b_ada.shape, norm1_g=norm1_g.shape, norm2_g=norm2_g.shape, final_g=final_g.shape,
                        sgu_ln_g=sgu_ln_g.shape, sgu_ln_b=sgu_ln_b.shape, sgu_w=sgu_w.shape, sgu_b=sgu_b.shape,
                        conv_w=(DEPTH, 4, 1536), a_log=a_log.shape, dt_bias=dt_bias.shape,
                        gdn_norm_g=gdn_norm_g.shape)

    def full_conv(a):
        return lax.dynamic_update_slice_in_dim(jnp.zeros((DEPTH, 4, 1536), F32), a, chip * conv_cols, axis=2)

    def pack_state(b_, n1, n2, fg, lg, lb, sw, sb, cv, al, db, gn):
        return _pack_rows([b_, n1, n2, fg, lg, lb, sw, sb, full_conv(cv), al, db, gn])

    w_small = pack_state(b_ada, norm1_g, norm2_g, final_g, sgu_ln_g, sgu_ln_b, sgu_w, sgu_b, conv_w, a_log, dt_bias,
                         gdn_norm_g)
    m_small = pack_state(m_b_ada, m_norm1_g, m_norm2_g, m_final_g, m_sgu_ln_g, m_sgu_ln_b, m_sgu_w, m_sgu_b, m_conv_w,
                         m_a_log, m_dt_bias, m_gdn_norm_g)
    v_small = pack_state(v_b_ada, v_norm1_g, v_norm2_g, v_final_g, v_sgu_ln_g, v_sgu_ln_b, v_sgu_w, v_sgu_b, v_conv_w,
                         v_a_log, v_dt_bias, v_gdn_norm_g)
    small_out = _small_finalize(gathered, w_small, m_small, v_small)
    sg, sd, sm, sv = [_unpack_rows(a, small_shapes) for a in small_out]
    for dct in (sg, sd, sm, sv):
        dct["conv_w"] = lax.dynamic_slice_in_dim(dct["conv_w"], chip * conv_cols, conv_cols, axis=2)

    dmod_all = gathered[:, :2 * _DMOD_ROWS].reshape(8, DEPTH, 6 * D)
    dmod_cols = lax.dynamic_slice_in_dim(dmod_all, chip * ada_cols, ada_cols, axis=2).transpose(1, 0, 2)
    g_ada, d_ada, nm_ada, nv_ada = _ada_backward_adamw(c_all, dmod_cols, w_ada, m_w_ada, v_w_ada)

    reduce_into_buffers(0, pending[1], _grads_chip_exchange(pending[1], 0))
    grads = _grads_pair_share(grads_buf)
    big = {}
    for n, g, (w, m, v) in zip(names, grads, ((w_in, m_w_in, v_w_in), (w_out, m_w_out, v_w_out),
                                              (w_ff1, m_w_ff1, v_w_ff1), (w_ff2, m_w_ff2, v_w_ff2))):
        big[n] = (g,) + tuple(_adamw(w, g, m, v, f"adamw_{n}"))

    def outs(k):
        s = (sg, sd, sm, sv)[k]
        return [(g_ada, d_ada, nm_ada, nv_ada)[k], s["b_ada"], s["norm1_g"], big["in"][k], s["sgu_ln_g"],
                s["sgu_ln_b"], s["sgu_w"], s["sgu_b"], s["conv_w"], s["a_log"], s["dt_bias"], s["gdn_norm_g"],
                big["out"][k], s["norm2_g"], big["ff1"][k], big["ff2"][k], s["final_g"]]

    return (loss, grad_x, *outs(0), *outs(1), *outs(2), *outs(3))
```

```python
import functools
import math

import jax
import jax.numpy as jnp
from jax import lax
from jax.experimental import pallas as pl
from jax.experimental.pallas import tpu as pltpu

F32 = jnp.float32
BF16 = jnp.bfloat16

DEPTH = 4
D = 1024
HEADS = 4
HD = 128
BLK = 128
IN_W = 3080
NW = 3200
GATE0 = 3072
DFF = 4096
N_CHIPS = 4
RMS_EPS = 1e-6
LN_EPS = 1e-5
QK_SCALE = HD ** -0.5
LR, B1, B2, ADAM_EPS, WD, STEP = 0.001, 0.9, 0.999, 1e-08, 0.01, 10
VMEM_LIMIT = 56 * 1024 * 1024
MESH = pl.DeviceIdType.MESH
HOPS = ((1, 0), (0, 1), (1, 1))
HI = lax.Precision.HIGHEST


def _dot(a, b):
    return jnp.dot(a.astype(BF16), b.astype(BF16), preferred_element_type=F32)


def _dot_nt(a, b):
    return lax.dot_general(a.astype(BF16), b.astype(BF16), (((1,), (1,)), ((), ())), preferred_element_type=F32)


def _dot_tn(a, b):
    return lax.dot_general(a.astype(BF16), b.astype(BF16), (((0,), (0,)), ((), ())), preferred_element_type=F32)


def _dotf(a, b):
    return jnp.dot(a, b, precision=HI, preferred_element_type=F32)


def _split(a):
    hi = a.astype(BF16)
    return hi, (a - hi.astype(F32)).astype(BF16)


def _dg3(a, b, dims, batch=((), ())):
    ah, al = _split(a)
    bh, bl = _split(b)
    f = lambda x, y: lax.dot_general(x, y, (dims, batch), preferred_element_type=F32)
    return f(ah, bh) + (f(ah, bl) + f(al, bh))


def _bmm3(a, b):
    return _dg3(a, b, ((2,), (1,)), ((0,), (0,)))


def _d3(a, b):
    return _dg3(a, b, ((1,), (0,)))


def _d3_nt(a, b):
    return _dg3(a, b, ((1,), (1,)))


def _d3_tn(a, b):
    return _dg3(a, b, ((0,), (0,)))


def _dotf_tn(a, b):
    return lax.dot_general(a, b, (((0,), (0,)), ((), ())), precision=HI, preferred_element_type=F32)


def _sigmoid(x):
    return 1.0 / (1.0 + jnp.exp(-x))


def _softplus(x):
    return jnp.maximum(x, 0.0) + jnp.log(1.0 + jnp.exp(-jnp.abs(x)))


_G0 = math.sqrt(2.0 / math.pi)
_G1 = 0.044715


def _gelu(x):
    t = jnp.tanh(_G0 * (x + _G1 * x * x * x))
    return 0.5 * x * (1.0 + t)


def _gelu_grad(x):
    t = jnp.tanh(_G0 * (x + _G1 * x * x * x))
    return 0.5 * (1.0 + t) + 0.5 * x * (1.0 - t * t) * (_G0 * (1.0 + 3.0 * _G1 * x * x))


def _silu(x):
    return x * _sigmoid(x)


def _silu_grad(x):
    s = _sigmoid(x)
    return s * (1.0 + x * (1.0 - s))


def _rms_stats(x):
    rstd = lax.rsqrt(jnp.mean(x * x, axis=-1, keepdims=True) + RMS_EPS)
    return x * rstd, rstd


def _norm_mod(x, ng, sc, sh):
    xh, _ = _rms_stats(x)
    return xh * (ng * (1.0 + sc)) + sh


def _norm_mod_bwd(dh, x, ng, sc):
    xh, rstd = _rms_stats(x)
    dsh = jnp.sum(dh, axis=0, keepdims=True)
    dsc = jnp.sum(dh * xh, axis=0, keepdims=True) * ng
    dng = jnp.sum(dh * xh, axis=0, keepdims=True) * (1.0 + sc)
    dxh = dh * (ng * (1.0 + sc))
    dx = rstd * (dxh - xh * jnp.mean(dxh * xh, axis=-1, keepdims=True))
    return dx, dsh, dsc, dng


def _iota2(shape, axis):
    return lax.broadcasted_iota(jnp.int32, shape, axis)


def _col(tile, idx):
    return jnp.sum(jnp.where(_iota2(tile.shape, 1) == idx, tile, 0.0), axis=1, keepdims=True)


def _row(tile, idx):
    return jnp.sum(jnp.where(_iota2(tile.shape, 0) == idx, tile, 0.0), axis=0, keepdims=True)


def _put_col(col, idx, width=HD):
    shape = (col.shape[0], width)
    return jnp.where(_iota2(shape, 1) == idx, jnp.broadcast_to(col, shape), 0.0)


def _tri_inverse(m):
    rows, cols = _iota2(m.shape, m.ndim - 2), _iota2(m.shape, m.ndim - 1)
    mm = _bmm3 if m.ndim == 3 else _d3
    eye = jnp.where(rows == cols, 1.0, 0.0).astype(F32)
    n = jnp.where((rows >> 3) == (cols >> 3), -m, 0.0)
    p = eye + n
    n2 = mm(n, n)
    p = p + mm(n2, p)
    n4 = mm(n2, n2)
    p = p + mm(n4, p)
    for shift in (3, 4, 5, 6):
        same_pair = (rows >> (shift + 1)) == (cols >> (shift + 1))
        below = jnp.logical_and(((rows >> shift) & 1) == 1, ((cols >> shift) & 1) == 0)
        off = jnp.where(jnp.logical_and(same_pair, below), m, 0.0)
        p = p - mm(p, mm(off, p))
    return p


def _cparams(sem=None):
    return pltpu.CompilerParams(dimension_semantics=sem, vmem_limit_bytes=VMEM_LIMIT)


def _my_place():
    return lax.axis_index("x"), lax.axis_index("y"), lax.axis_index("c")


def _hop(xi, yi, hop):
    dx, dy = hop
    return (1 - xi if dx else xi), (1 - yi if dy else yi)


def _pair_exchange(x, name):
    def body(x_ref, o_ref, ssem, rsem):
        xi, yi, ci = _my_place()
        cp = pltpu.make_async_remote_copy(x_ref, o_ref, ssem, rsem, device_id=(xi, yi, 1 - ci), device_id_type=MESH)
        cp.start()
        cp.wait()

    return pl.pallas_call(
        body, name=name, out_shape=jax.ShapeDtypeStruct(x.shape, x.dtype),
        in_specs=[pl.BlockSpec(memory_space=pltpu.VMEM)], out_specs=pl.BlockSpec(memory_space=pltpu.VMEM),
        scratch_shapes=[pltpu.SemaphoreType.DMA, pltpu.SemaphoreType.DMA],
        compiler_params=pltpu.CompilerParams(vmem_limit_bytes=VMEM_LIMIT),
    )(x)


def _chip_allgather(x, name):
    def body(x_ref, o_ref, ssems, rsems, lsem):
        xi, yi, ci = _my_place()
        me = 2 * xi + yi
        loc = pltpu.make_async_copy(x_ref, o_ref.at[me], lsem)
        loc.start()
        sends = []
        for k, hop in enumerate(HOPS):
            tx, ty = _hop(xi, yi, hop)
            cp = pltpu.make_async_remote_copy(x_ref, o_ref.at[me], ssems.at[k], rsems.at[k],
                                              device_id=(tx, ty, ci), device_id_type=MESH)
            cp.start()
            sends.append(cp)
        for k, hop in enumerate(HOPS):
            tx, ty = _hop(xi, yi, hop)
            pltpu.make_async_remote_copy(x_ref, o_ref.at[2 * tx + ty], ssems.at[k], rsems.at[k],
                                         device_id=(tx, ty, ci), device_id_type=MESH).wait_recv()
        for cp in sends:
            cp.wait_send()
        loc.wait()

    return pl.pallas_call(
        body, name=name, out_shape=jax.ShapeDtypeStruct((N_CHIPS,) + x.shape, x.dtype),
        in_specs=[pl.BlockSpec(memory_space=pltpu.VMEM)], out_specs=pl.BlockSpec(memory_space=pltpu.VMEM),
        scratch_shapes=[pltpu.SemaphoreType.DMA((3,)), pltpu.SemaphoreType.DMA((3,)), pltpu.SemaphoreType.DMA],
        compiler_params=pltpu.CompilerParams(vmem_limit_bytes=VMEM_LIMIT),
    )(x)


def _hbm_specs(n):
    return [pl.BlockSpec(memory_space=pl.ANY)] * n


def _cast_into_slot(w, l, place):
    _, r, c = w.shape
    tr = _row_tile(r)

    def body(p_ref, w_ref, o_ref):
        o_ref[...] = w_ref[...].astype(BF16)

    return pl.pallas_call(
        body, name=f"cast_slot_{r}x{c}_{l}", out_shape=jax.ShapeDtypeStruct((N_CHIPS, r, c), BF16),
        grid_spec=pltpu.PrefetchScalarGridSpec(
            num_scalar_prefetch=1, grid=(r // tr,),
            in_specs=[pl.BlockSpec((None, tr, c), lambda k, pr: (l, k, 0))],
            out_specs=pl.BlockSpec((None, tr, c), lambda k, pr: (pr[0], k, 0))),
        compiler_params=_cparams(("parallel",)),
    )(place, w)


def _halves(ref, ci):
    half = ref.shape[-2] // 2
    return pl.ds(ci * half, half), pl.ds((1 - ci) * half, half)


def _gather_copies(bufs, sems):
    s_ici, r_ici, s_d2d, r_d2d = sems
    xi, yi, ci = _my_place()
    me = 2 * xi + yi
    ici_send, ici_recv, d2d_send, d2d_recv = [], [], [], []
    for i, buf in enumerate(bufs):
        mine, sibs = _halves(buf, ci)
        for k, hop in enumerate(HOPS):
            tx, ty = _hop(xi, yi, hop)
            src = 2 * tx + ty
            ici_send.append(pltpu.make_async_remote_copy(buf.at[me, mine], buf.at[me, mine], s_ici.at[i, k],
                                                         r_ici.at[i, k], device_id=(tx, ty, ci), device_id_type=MESH))
            ici_recv.append(pltpu.make_async_remote_copy(buf.at[src, mine], buf.at[src, mine], s_ici.at[i, k],
                                                         r_ici.at[i, k], device_id=(tx, ty, ci), device_id_type=MESH))
            d2d_send.append(pltpu.make_async_remote_copy(buf.at[src, mine], buf.at[src, mine], s_d2d.at[i, k],
                                                         r_d2d.at[i, k], device_id=(xi, yi, 1 - ci), device_id_type=MESH))
            d2d_recv.append(pltpu.make_async_remote_copy(buf.at[src, sibs], buf.at[src, sibs], s_d2d.at[i, k],
                                                         r_d2d.at[i, k], device_id=(xi, yi, 1 - ci), device_id_type=MESH))
    return ici_send, ici_recv, d2d_send, d2d_recv


def _gather_start(bufs, sems):
    for cp in _gather_copies(bufs, sems)[0]:
        cp.start()


def _gather_finish(bufs, sems):
    ici_send, ici_recv, d2d_send, d2d_recv = _gather_copies(bufs, sems)
    for arrived, forward in zip(ici_recv, d2d_send):
        arrived.wait_recv()
        forward.start()
    for cp in d2d_recv:
        cp.wait_recv()
    for cp in ici_send + d2d_send:
        cp.wait_send()


def _gather_sems(n):
    return [pltpu.SemaphoreType.DMA((n, 3))] * 4


def _weights_allgather(bufs, l):
    n = len(bufs)

    def body(*refs):
        outs, sems = refs[n:2 * n], refs[2 * n:]
        _gather_start(outs, sems)
        _gather_finish(outs, sems)

    return pl.pallas_call(
        body, name=f"weights_allgather_{l}",
        out_shape=[jax.ShapeDtypeStruct(b.shape, b.dtype) for b in bufs],
        in_specs=_hbm_specs(n), out_specs=_hbm_specs(n), input_output_aliases={i: i for i in range(n)},
        scratch_shapes=_gather_sems(n),
    )(*bufs)


def _grads_pair_send(gs, l):
    n = len(gs)

    def body(*refs):
        ins, outs, ssem, rsem = refs[:n], refs[n:2 * n], refs[2 * n], refs[2 * n + 1]
        xi, yi, ci = _my_place()
        every = pl.ds(0, N_CHIPS)
        cps = []
        for i in range(n):
            sibs = _halves(ins[i], ci)[1]
            cp = pltpu.make_async_remote_copy(ins[i].at[every, sibs], outs[i], ssem.at[i], rsem.at[i],
                                              device_id=(xi, yi, 1 - ci), device_id_type=MESH)
            cp.start()
            cps.append(cp)
        for cp in cps:
            cp.wait()

    return pl.pallas_call(
        body, name=f"grads_pair_send_{l}",
        out_shape=[jax.ShapeDtypeStruct((N_CHIPS, g.shape[1] // 2, g.shape[2]), g.dtype) for g in gs],
        in_specs=_hbm_specs(n), out_specs=_hbm_specs(n),
        scratch_shapes=[pltpu.SemaphoreType.DMA((n,)), pltpu.SemaphoreType.DMA((n,))],
    )(*gs)


def _exchange_copies(ps, recvs, sems):
    ssems, rsems = sems
    xi, yi, ci = _my_place()
    cps = []
    for i, (p, rc) in enumerate(zip(ps, recvs)):
        for k, hop in enumerate(HOPS):
            tx, ty = _hop(xi, yi, hop)
            cps.append(pltpu.make_async_remote_copy(p.at[2 * tx + ty], rc.at[k], ssems.at[i, k], rsems.at[i, k],
                                                    device_id=(tx, ty, ci), device_id_type=MESH))
    return cps


def _exchange_sems(n):
    return [pltpu.SemaphoreType.DMA((n, 3))] * 2


def _exchange_shapes(ps):
    return [jax.ShapeDtypeStruct((3,) + p.shape[1:], p.dtype) for p in ps]


def _grads_chip_exchange(ps, l):
    n = len(ps)

    def body(*refs):
        cps = _exchange_copies(refs[:n], refs[n:2 * n], refs[2 * n:])
        for cp in cps:
            cp.start()
        for cp in cps:
            cp.wait()

    return pl.pallas_call(
        body, name=f"grads_chip_exchange_{l}", out_shape=_exchange_shapes(ps),
        in_specs=_hbm_specs(n), out_specs=_hbm_specs(n), scratch_shapes=_exchange_sems(n),
    )(*ps)


def _grads_pair_share(gs):
    n = len(gs)

    def body(*refs):
        outs = refs[n:2 * n]
        ssem, rsem = refs[2 * n:]
        xi, yi, ci = _my_place()
        every = pl.ds(0, DEPTH)
        sends = []
        for i in range(n):
            mine = _halves(outs[i], ci)[0]
            cp = pltpu.make_async_remote_copy(outs[i].at[every, mine], outs[i].at[every, mine], ssem.at[i], rsem.at[i],
                                              device_id=(xi, yi, 1 - ci), device_id_type=MESH)
            cp.start()
            sends.append(cp)
        for i in range(n):
            sibs = _halves(outs[i], ci)[1]
            pltpu.make_async_remote_copy(outs[i].at[every, sibs], outs[i].at[every, sibs], ssem.at[i], rsem.at[i],
                                         device_id=(xi, yi, 1 - ci), device_id_type=MESH).wait_recv()
        for cp in sends:
            cp.wait_send()

    return pl.pallas_call(
        body, name="grads_pair_share",
        out_shape=[jax.ShapeDtypeStruct(g.shape, g.dtype) for g in gs],
        in_specs=_hbm_specs(n), out_specs=_hbm_specs(n), input_output_aliases={i: i for i in range(n)},
        scratch_shapes=[pltpu.SemaphoreType.DMA((n,)), pltpu.SemaphoreType.DMA((n,))],
    )(*gs)


def _row_tile(r):
    return min(r, 256)


def _pair_sum(g, ga, place, name):
    _, r, c = g.shape
    tr = _row_tile(r // 2)
    nk = r // 2 // tr

    def body(p_ref, g_ref, ga_ref, o_ref):
        o_ref[...] = (g_ref[...].astype(F32) + ga_ref[...].astype(F32)).astype(o_ref.dtype)

    return pl.pallas_call(
        body, name=name, out_shape=jax.ShapeDtypeStruct(ga.shape, ga.dtype),
        grid_spec=pltpu.PrefetchScalarGridSpec(
            num_scalar_prefetch=1, grid=(N_CHIPS, nk),
            in_specs=[pl.BlockSpec((None, tr, c), lambda j, k, pr: (j, pr[1] * nk + k, 0)),
                      pl.BlockSpec((None, tr, c), lambda j, k, pr: (j, k, 0))],
            out_specs=pl.BlockSpec((None, tr, c), lambda j, k, pr: (j, k, 0))),
        compiler_params=_cparams(("parallel", "parallel")),
    )(place, g, ga)


def _chip_sum(pair, recv, buf, l, place, name):
    _, rh, c = pair.shape
    tr = _row_tile(rh)
    nk = rh // tr

    def body(p_ref, own_ref, r_ref, *rest):
        o_ref = rest[-1]
        acc = own_ref[...].astype(F32) + r_ref[0].astype(F32)
        acc = acc + r_ref[1].astype(F32)
        o_ref[...] = acc + r_ref[2].astype(F32)

    in_specs = [pl.BlockSpec((None, tr, c), lambda k, pr: (pr[0], k, 0)),
                pl.BlockSpec((3, tr, c), lambda k, pr: (0, k, 0))]
    args = [pair, recv]
    aliases = {}
    if buf is not None:
        in_specs.append(pl.BlockSpec(memory_space=pl.ANY))
        args.append(buf)
        aliases = {3: 0}
    return pl.pallas_call(
        body, name=name, out_shape=jax.ShapeDtypeStruct((DEPTH, 2 * rh, c), F32),
        grid_spec=pltpu.PrefetchScalarGridSpec(
            num_scalar_prefetch=1, grid=(nk,), in_specs=in_specs,
            out_specs=pl.BlockSpec((None, tr, c), lambda k, pr: (l, pr[1] * nk + k, 0))),
        input_output_aliases=aliases, compiler_params=_cparams(("parallel",)),
    )(place, *args)


def _adam_math(w, g, m, v):
    m = B1 * m + (1.0 - B1) * g
    v = B2 * v + (1.0 - B2) * (g * g)
    m_hat = m / (1.0 - B1 ** STEP)
    v_hat = v / (1.0 - B2 ** STEP)
    delta = -LR * (m_hat / (jnp.sqrt(v_hat) + ADAM_EPS) + WD * w)
    return delta, m, v


def _adamw(w, g, m, v, name):
    n_l, r, c = w.shape
    tr = _row_tile(r)

    def body(w_ref, g_ref, m_ref, v_ref, d_ref, nm_ref, nv_ref):
        d_ref[...], nm_ref[...], nv_ref[...] = _adam_math(w_ref[...], g_ref[...], m_ref[...], v_ref[...])

    spec = pl.BlockSpec((None, tr, c), lambda i, k: (i, k, 0))
    return pl.pallas_call(
        body, name=name, out_shape=[jax.ShapeDtypeStruct(w.shape, F32)] * 3, grid=(n_l, r // tr),
        in_specs=[spec] * 4, out_specs=[spec] * 3, compiler_params=_cparams(("parallel", "parallel")),
    )(w, g, m, v)


def _ada_forward(c_all, w_ada, b_cols):
    cols = w_ada.shape[2]
    tn = 512

    def body(c_ref, w_ref, b_ref, o_ref):
        o_ref[...] = _dotf(_silu(c_ref[...]), w_ref[...]) + b_ref[...]

    return pl.pallas_call(
        body, name="ada_forward", out_shape=jax.ShapeDtypeStruct((DEPTH, 8, cols), F32), grid=(DEPTH, cols // tn),
        in_specs=[pl.BlockSpec((8, D), lambda l, j: (0, 0)),
                  pl.BlockSpec((None, D, tn), lambda l, j: (l, 0, j)),
                  pl.BlockSpec((None, 1, tn), lambda l, j: (l, 0, j))],
        out_specs=pl.BlockSpec((None, 8, tn), lambda l, j: (l, 0, j)),
        compiler_params=_cparams(("parallel", "parallel")),
    )(c_all, w_ada, b_cols.reshape(DEPTH, 1, cols))


def _ada_backward_adamw(c_all, dmod_cols, w, m, v):
    cols = w.shape[2]
    tn = 512

    def body(c_ref, d_ref, w_ref, m_ref, v_ref, g_ref, dl_ref, nm_ref, nv_ref):
        g = _dotf_tn(_silu(c_ref[...]), d_ref[...])
        g_ref[...] = g
        dl_ref[...], nm_ref[...], nv_ref[...] = _adam_math(w_ref[...], g, m_ref[...], v_ref[...])

    wspec = pl.BlockSpec((None, D, tn), lambda l, j: (l, 0, j))
    return pl.pallas_call(
        body, name="ada_backward_adamw", out_shape=[jax.ShapeDtypeStruct(w.shape, F32)] * 4, grid=(DEPTH, cols // tn),
        in_specs=[pl.BlockSpec((8, D), lambda l, j: (0, 0)), pl.BlockSpec((None, 8, tn), lambda l, j: (l, 0, j)),
                  wspec, wspec, wspec],
        out_specs=[wspec] * 4, compiler_params=_cparams(("parallel", "parallel")),
    )(c_all, dmod_cols, w, m, v)


def _tok_tile(t):
    return min(t, 256)


def _wspec4(r, c, l):
    return pl.BlockSpec((N_CHIPS, r, c), lambda i: (0, 0, 0))


def _fwd_in(x, modv, w_in, l):
    t = x.shape[0]
    tm = _tok_tile(t)

    def body(x_ref, mod_ref, w_ref, o_ref, h_ref):
        h = _norm_mod(x_ref[...], mod_ref[6:7, :], mod_ref[1:2, :], mod_ref[0:1, :]).astype(BF16)
        h_ref[...] = h
        o_ref[...] = jnp.dot(h, w_ref[...], preferred_element_type=F32)

    return pl.pallas_call(
        body, name=f"fwd_in_{l}", grid=(t // tm,),
        out_shape=[jax.ShapeDtypeStruct((t, NW), F32), jax.ShapeDtypeStruct((t, D), BF16)],
        in_specs=[pl.BlockSpec((tm, D), lambda i: (i, 0)), pl.BlockSpec((None, 8, D), lambda i: (l, 0, 0)),
                  pl.BlockSpec((D, NW), lambda i: (0, 0))],
        out_specs=[pl.BlockSpec((tm, NW), lambda i: (i, 0)), pl.BlockSpec((tm, D), lambda i: (i, 0))],
        compiler_params=_cparams(("parallel",)),
    )(x, modv, w_in)


def _fwd_out(x, mix, modv, w_out, l):
    t = x.shape[0]
    tm = _tok_tile(t)

    def body(x_ref, mix_ref, mod_ref, w_ref, o_ref):
        w = w_ref[...].reshape(D, D)
        o_ref[...] = x_ref[...] + mod_ref[2:3, :] * jnp.dot(mix_ref[...], w, preferred_element_type=F32)

    return pl.pallas_call(
        body, name=f"fwd_out_{l}", out_shape=jax.ShapeDtypeStruct((t, D), F32), grid=(t // tm,),
        in_specs=[pl.BlockSpec((tm, D), lambda i: (i, 0)), pl.BlockSpec((tm, D), lambda i: (i, 0)),
                  pl.BlockSpec((None, 8, D), lambda i: (l, 0, 0)), _wspec4(D // N_CHIPS, D, l)],
        out_specs=pl.BlockSpec((tm, D), lambda i: (i, 0)), compiler_params=_cparams(("parallel",)),
    )(x, mix, modv, w_out)


def _fwd_ff1(x, modv, w_ff1, l):
    t = x.shape[0]
    tm = _tok_tile(t)

    def body(x_ref, mod_ref, w_ref, o_ref, h_ref):
        h = _norm_mod(x_ref[...], mod_ref[7:8, :], mod_ref[4:5, :], mod_ref[3:4, :]).astype(BF16)
        h_ref[...] = h
        for j in range(N_CHIPS):
            f = jnp.dot(h, w_ref[j], preferred_element_type=F32)
            o_ref[:, j * D:(j + 1) * D] = jnp.maximum(f, 0.0).astype(BF16)

    return pl.pallas_call(
        body, name=f"fwd_ff1_{l}", grid=(t // tm,),
        out_shape=[jax.ShapeDtypeStruct((t, DFF), BF16), jax.ShapeDtypeStruct((t, D), BF16)],
        in_specs=[pl.BlockSpec((tm, D), lambda i: (i, 0)), pl.BlockSpec((None, 8, D), lambda i: (l, 0, 0)),
                  _wspec4(D, D, l)],
        out_specs=[pl.BlockSpec((tm, DFF), lambda i: (i, 0)), pl.BlockSpec((tm, D), lambda i: (i, 0))],
        compiler_params=_cparams(("parallel",)),
    )(x, modv, w_ff1)


def _fwd_ff2(x, r, modv, w_ff2, l):
    t = x.shape[0]
    tm = _tok_tile(t)

    def body(x_ref, r_ref, mod_ref, w_ref, o_ref):
        acc = jnp.zeros((tm, D), F32)
        for j in range(N_CHIPS):
            rj = r_ref[:, j * D:(j + 1) * D].astype(F32)
            acc = acc + jnp.dot((rj * rj).astype(BF16), w_ref[j], preferred_element_type=F32)
        o_ref[...] = x_ref[...] + mod_ref[5:6, :] * acc

    return pl.pallas_call(
        body, name=f"fwd_ff2_{l}", out_shape=jax.ShapeDtypeStruct((t, D), F32), grid=(t // tm,),
        in_specs=[pl.BlockSpec((tm, D), lambda i: (i, 0)), pl.BlockSpec((tm, DFF), lambda i: (i, 0)),
                  pl.BlockSpec((None, 8, D), lambda i: (l, 0, 0)), _wspec4(D, D, l)],
        out_specs=pl.BlockSpec((tm, D), lambda i: (i, 0)), compiler_params=_cparams(("parallel",)),
    )(x, r, modv, w_ff2)


def _loss_head(x, target, final_g):
    t = x.shape[0]
    tm = _tok_tile(t)

    def body(x_ref, t_ref, g_ref, dx_ref, st_ref):
        @pl.when(pl.program_id(0) == 0)
        def _():
            st_ref[...] = jnp.zeros_like(st_ref)

        xh, rstd = _rms_stats(x_ref[...])
        g = g_ref[...]
        err = xh * g - t_ref[...]
        loss = 0.5 * jnp.sum(jnp.mean(err * err, axis=-1, keepdims=True), axis=0, keepdims=True)
        dy = err * (1.0 / D)
        st_ref[0:1, :] += jnp.sum(dy * xh, axis=0, keepdims=True)
        st_ref[1:2, :] += jnp.broadcast_to(loss, (1, D))
        dxh = dy * g
        dx_ref[...] = rstd * (dxh - xh * jnp.mean(dxh * xh, axis=-1, keepdims=True))

    return pl.pallas_call(
        body, name="loss_head", out_shape=[jax.ShapeDtypeStruct((t, D), F32), jax.ShapeDtypeStruct((8, D), F32)],
        grid=(t // tm,),
        in_specs=[pl.BlockSpec((tm, D), lambda i: (i, 0)), pl.BlockSpec((tm, D), lambda i: (i, 0)),
                  pl.BlockSpec((1, D), lambda i: (0, 0))],
        out_specs=[pl.BlockSpec((tm, D), lambda i: (i, 0)), pl.BlockSpec((8, D), lambda i: (0, 0))],
        compiler_params=_cparams(("arbitrary",)),
    )(x, target, final_g.reshape(1, D))


def _bwd_ff2(dx2, r, modv, w_ff2, l):
    t = dx2.shape[0]
    tm = _tok_tile(t)

    def body(d_ref, r_ref, mod_ref, w_ref, o_ref):
        dyg = (d_ref[...] * mod_ref[5:6, :]).astype(BF16)
        for j in range(N_CHIPS):
            da = lax.dot_general(dyg, w_ref[j], (((1,), (1,)), ((), ())), preferred_element_type=F32)
            o_ref[:, j * D:(j + 1) * D] = (da * 2.0 * r_ref[:, j * D:(j + 1) * D].astype(F32)).astype(BF16)

    return pl.pallas_call(
        body, name=f"bwd_ff2_{l}", out_shape=jax.ShapeDtypeStruct((t, DFF), BF16), grid=(t // tm,),
        in_specs=[pl.BlockSpec((tm, D), lambda i: (i, 0)), pl.BlockSpec((tm, DFF), lambda i: (i, 0)),
                  pl.BlockSpec((None, 8, D), lambda i: (l, 0, 0)), _wspec4(D, D, l)],
        out_specs=pl.BlockSpec((tm, DFF), lambda i: (i, 0)), compiler_params=_cparams(("parallel",)),
    )(dx2, r, modv, w_ff2)


def _bwd_out(dx1, modv, w_out, l):
    t = dx1.shape[0]
    tm = _tok_tile(t)

    def body(d_ref, mod_ref, w_ref, o_ref):
        dyg = (d_ref[...] * mod_ref[2:3, :]).astype(BF16)
        w = w_ref[...].reshape(D, D)
        o_ref[...] = lax.dot_general(dyg, w, (((1,), (1,)), ((), ())), preferred_element_type=F32).astype(BF16)

    return pl.pallas_call(
        body, name=f"bwd_out_{l}", out_shape=jax.ShapeDtypeStruct((t, D), BF16), grid=(t // tm,),
        in_specs=[pl.BlockSpec((tm, D), lambda i: (i, 0)), pl.BlockSpec((None, 8, D), lambda i: (l, 0, 0)),
                  _wspec4(D // N_CHIPS, D, l)],
        out_specs=pl.BlockSpec((tm, D), lambda i: (i, 0)), compiler_params=_cparams(("parallel",)),
    )(dx1, modv, w_out)


def _bwd_norm(dy, w, x, dres, modv, l, which):
    t = x.shape[0]
    tm = _tok_tile(t)
    rows = (6, 1) if which == "in" else (7, 4)
    width = dy.shape[1]

    def body(dy_ref, w_ref, x_ref, dr_ref, mod_ref, dx_ref, st_ref):
        @pl.when(pl.program_id(0) == 0)
        def _():
            st_ref[...] = jnp.zeros_like(st_ref)

        if which == "in":
            dh = lax.dot_general(dy_ref[...], w_ref[...], (((1,), (1,)), ((), ())), preferred_element_type=F32)
        else:
            dh = jnp.zeros((tm, D), F32)
            for j in range(N_CHIPS):
                dh = dh + lax.dot_general(dy_ref[:, j * D:(j + 1) * D], w_ref[j], (((1,), (1,)), ((), ())),
                                          preferred_element_type=F32)
        ng, sc = mod_ref[rows[0]:rows[0] + 1, :], mod_ref[rows[1]:rows[1] + 1, :]
        dx, dsh, dsc, dng = _norm_mod_bwd(dh, x_ref[...], ng, sc)
        dx_ref[...] = dr_ref[...] + dx
        st_ref[0:1, :] += dsh
        st_ref[1:2, :] += dsc
        st_ref[2:3, :] += dng

    wspec = pl.BlockSpec((D, NW), lambda i: (0, 0)) if which == "in" else _wspec4(D, D, l)
    return pl.pallas_call(
        body, name=f"bwd_norm_{which}_{l}",
        out_shape=[jax.ShapeDtypeStruct((t, D), F32), jax.ShapeDtypeStruct((8, D), F32)], grid=(t // tm,),
        in_specs=[pl.BlockSpec((tm, width), lambda i: (i, 0)), wspec, pl.BlockSpec((tm, D), lambda i: (i, 0)),
                  pl.BlockSpec((tm, D), lambda i: (i, 0)), pl.BlockSpec((None, 8, D), lambda i: (l, 0, 0))],
        out_specs=[pl.BlockSpec((tm, D), lambda i: (i, 0)), pl.BlockSpec((8, D), lambda i: (0, 0))],
        compiler_params=_cparams(("arbitrary",)),
    )(dy, w, x, dres, modv)


def _grad_weight(lhs, rhs, modv, l, which, w_gate=None):
    t = lhs.shape[0]
    tm = min(t, 2048)
    nt = t // tm
    gated = which in ("out", "ff2")
    if which == "in":
        nj, lw, rw, orows, ocols = 5, D, NW // 5, D, NW // 5
    elif which == "ff1":
        nj, lw, rw, orows, ocols = N_CHIPS, D, D, D, D
    elif which == "out":
        nj, lw, rw, orows, ocols = N_CHIPS, D // N_CHIPS, D, D // N_CHIPS, D
    else:
        nj, lw, rw, orows, ocols = N_CHIPS, D, D, D, D
    gate_row = 2 if which == "out" else 5

    def body(*refs):
        if gated:
            l_ref, r_ref, mod_ref, wg_ref, o_ref, dg_ref, acc = refs
        else:
            l_ref, r_ref, mod_ref, o_ref, acc = refs
        j, k = pl.program_id(0), pl.program_id(1)

        @pl.when(k == 0)
        def _():
            acc[...] = jnp.zeros_like(acc)

        if which == "ff2":
            lv = l_ref[...].astype(F32)
            lv = lv * lv
        else:
            lv = l_ref[...]
        acc[...] += _dot_tn(lv, r_ref[...])

        if gated:
            @pl.when(jnp.logical_and(j == 0, k == 0))
            def _():
                dg_ref[...] = jnp.zeros_like(dg_ref)

        @pl.when(k == nt - 1)
        def _():
            raw = acc[...]
            if gated:
                o_ref[...] = (raw * mod_ref[gate_row:gate_row + 1, :]).astype(o_ref.dtype)
                dg_ref[0:1, :] += jnp.sum(raw * wg_ref[...].astype(F32), axis=0, keepdims=True)
            else:
                o_ref[...] = raw.astype(o_ref.dtype)

    if which in ("in", "ff1"):
        lspec = pl.BlockSpec((tm, lw), lambda j, k: (k, 0))
        rspec = pl.BlockSpec((tm, rw), lambda j, k: (k, j))
    else:
        lspec = pl.BlockSpec((tm, lw), lambda j, k: (k, j))
        rspec = pl.BlockSpec((tm, rw), lambda j, k: (k, 0))
    mspec = pl.BlockSpec((None, 8, D), lambda j, k: (l, 0, 0))
    if which == "in":
        ospec = pl.BlockSpec((orows, ocols), lambda j, k: (0, j))
        out_shape = [jax.ShapeDtypeStruct((D, NW), BF16)]
    else:
        ospec = pl.BlockSpec((None, orows, ocols), lambda j, k: (j, 0, 0))
        out_shape = [jax.ShapeDtypeStruct((N_CHIPS, orows, ocols), BF16)]
    in_specs = [lspec, rspec, mspec]
    args = [lhs, rhs, modv]
    out_specs = [ospec]
    if gated:
        in_specs.append(pl.BlockSpec((None, orows, ocols), lambda j, k: (j, 0, 0)))
        args.append(w_gate)
        out_specs.append(pl.BlockSpec((8, D), lambda j, k: (0, 0)))
        out_shape.append(jax.ShapeDtypeStruct((8, D), F32))
    res = pl.pallas_call(
        body, name=f"grad_w_{which}_{l}", out_shape=out_shape, grid=(nj, nt), in_specs=in_specs, out_specs=out_specs,
        scratch_shapes=[pltpu.VMEM((orows, ocols), F32)], compiler_params=_cparams(("arbitrary", "arbitrary")),
    )(*args)
    return (res[0], res[1]) if gated else (res[0], None)


def _tri_masks():
    rows, cols = _iota2((BLK, BLK), 0), _iota2((BLK, BLK), 1)
    return rows >= cols, rows > cols


def _sgu_forward(p_ref, lnp_ref, sguw_ref, sgub_ref):
    incl, _ = _tri_masks()
    ug = _gelu(p_ref[:, 0:512])
    vg = _gelu(p_ref[:, 512:1024])
    mu = jnp.mean(vg, axis=-1, keepdims=True)
    xc = vg - mu
    rstd = lax.rsqrt(jnp.mean(xc * xc, axis=-1, keepdims=True) + LN_EPS)
    vhat = xc * rstd
    vn = vhat * lnp_ref[0:1, :] + lnp_ref[1:2, :]
    bias = sgub_ref[...]
    ys, mixed, wms = [], [], []
    for h in range(HEADS):
        wm = jnp.where(incl, sguw_ref[h], 0.0)
        mx = _dot(wm, vn[:, h * HD:(h + 1) * HD]) + _col(bias, h)
        ys.append(ug[:, h * HD:(h + 1) * HD] * mx)
        mixed.append(mx)
        wms.append(wm)
    return ys, ug, vhat, rstd, vn, mixed, wms


def _conv_forward(xbuf, cw_ref):
    conv = cw_ref[0:1, :] * xbuf[5:5 + BLK, :]
    for j in range(1, 4):
        conv = conv + cw_ref[j:j + 1, :] * xbuf[5 + j:5 + j + BLK, :]
    return conv


def _gates(gt, gv_ref):
    incl, _ = _tri_masks()
    beta = _sigmoid(gt)
    neg_a = -jnp.exp(gv_ref[0:1, :])
    gl = neg_a * _softplus(gt + gv_ref[1:2, :])
    gc = _dotf(jnp.where(incl, 1.0, 0.0).astype(F32), gl)
    return beta, gl, gc, gc.T, neg_a


def _head_chunk(act, beta, gc, gct, h):
    incl, strict = _tri_masks()
    qh = act[:, h * HD:(h + 1) * HD]
    kh = act[:, 512 + h * HD:512 + (h + 1) * HD]
    vh = act[:, 1024 + h * HD:1024 + (h + 1) * HD]
    rq = lax.rsqrt(jnp.sum(qh * qh, axis=-1, keepdims=True) + RMS_EPS)
    rk = lax.rsqrt(jnp.sum(kh * kh, axis=-1, keepdims=True) + RMS_EPS)
    qhat, khat = qh * rq, kh * rk
    qn = qhat * QK_SCALE
    b = _col(beta, h)
    gcol = _col(gc, 4 + h)
    grow = _row(gct, 4 + h)
    dmat = jnp.where(incl, jnp.exp(jnp.where(incl, gcol - grow, 0.0)), 0.0)
    gam = jnp.exp(gcol)
    glast = _row(gcol, BLK - 1)
    e = jnp.exp(glast - gcol)
    kk = _d3_nt(khat, khat)
    return dict(qhat=qhat, khat=khat, qn=qn, vh=vh, rq=rq, rk=rk, b=b, dmat=dmat, gam=gam, glast=glast, e=e, kk=kk,
                strict=strict, incl=incl)


def _mixer_forward(p, lnp, sgu_w, sgu_bt, cw, gv, l, gather=()):
    t = p.shape[0]
    nb = t // BLK
    ng = len(gather)

    def body(*refs):
        p_ref, lnp_ref, sguw_ref, sgub_ref, cw_ref, gv_ref = refs[:6]
        mix_ref, s_out, t_out, u_out, w_out, o_out, tail_out = refs[6 + ng:13 + ng]
        gbufs = refs[13 + ng:13 + 2 * ng]
        s_scr, xbuf = refs[13 + 2 * ng:15 + 2 * ng]
        gsems = refs[15 + 2 * ng:]

        @pl.when(pl.program_id(0) == 0)
        def _():
            s_scr[...] = jnp.zeros_like(s_scr)
            xbuf[0:8, :] = jnp.zeros((8, 1536), F32)
            if ng:
                _gather_start(gbufs, gsems)

        ys = _sgu_forward(p_ref, lnp_ref, sguw_ref, sgub_ref)[0]
        for h in range(HEADS):
            mix_ref[:, h * HD:(h + 1) * HD] = ys[h].astype(BF16)

        tail_out[...] = xbuf[0:8, :]
        xbuf[8:8 + BLK, :] = p_ref[:, 1024:2560]
        act = _silu(_conv_forward(xbuf, cw_ref))
        xbuf[0:8, :] = xbuf[BLK:BLK + 8, :]
        beta, _, gc, gct, _ = _gates(p_ref[:, GATE0:NW], gv_ref)
        chunks = [_head_chunk(act, beta, gc, gct, h) for h in range(HEADS)]
        for h, hc in enumerate(chunks):
            t_out[h] = jnp.where(hc["strict"], hc["b"] * hc["kk"] * hc["dmat"], 0.0)
        t_out[...] = _tri_inverse(t_out[...])
        for h, hc in enumerate(chunks):
            tm = t_out[h]
            u = _d3(tm, hc["b"] * hc["vh"])
            w = _d3(tm, (hc["b"] * hc["gam"]) * hc["khat"])
            qkm = _dot_nt(hc["qn"], hc["khat"]) * hc["dmat"]
            s = s_scr[h]
            wn = u - _dot(w, s)
            o = _dot(hc["qn"] * hc["gam"], s) + _dot(qkm, wn)
            s_out[h] = s
            s_scr[h] = jnp.exp(hc["glast"]) * s + _dot_tn(hc["khat"] * hc["e"], wn)
            sl = slice(h * HD, (h + 1) * HD)
            u_out[:, sl] = u
            w_out[:, sl] = w
            o_out[:, sl] = o
            on = o * lax.rsqrt(jnp.mean(o * o, axis=-1, keepdims=True) + RMS_EPS) * gv_ref[2:3, :]
            mix_ref[:, 512 + h * HD:512 + (h + 1) * HD] = (on * _silu(p_ref[:, 2560 + h * HD:2560 + (h + 1) * HD])).astype(BF16)

        if ng:
            @pl.when(pl.program_id(0) == nb - 1)
            def _():
                _gather_finish(gbufs, gsems)

    tok = lambda w: pl.BlockSpec((BLK, w), lambda i: (i, 0))
    st = pl.BlockSpec((None, HEADS, HD, HD), lambda i: (i, 0, 0, 0))
    return pl.pallas_call(
        body, name=f"mixer_fwd_{l}", grid=(nb,),
        out_shape=[jax.ShapeDtypeStruct((t, D), BF16), jax.ShapeDtypeStruct((nb, HEADS, HD, HD), F32),
                   jax.ShapeDtypeStruct((nb, HEADS, HD, HD), F32), jax.ShapeDtypeStruct((t, 512), F32),
                   jax.ShapeDtypeStruct((t, 512), F32), jax.ShapeDtypeStruct((t, 512), F32),
                   jax.ShapeDtypeStruct((nb, 8, 1536), F32)]
        + [jax.ShapeDtypeStruct(b.shape, b.dtype) for b in gather],
        in_specs=[tok(NW), pl.BlockSpec((None, 8, 512), lambda i: (l, 0, 0)),
                  pl.BlockSpec((None, HEADS, HD, HD), lambda i: (l, 0, 0, 0)),
                  pl.BlockSpec((None, HD, HD), lambda i: (l, 0, 0)), pl.BlockSpec((None, 8, 1536), lambda i: (l, 0, 0)),
                  pl.BlockSpec((None, 8, HD), lambda i: (l, 0, 0))] + _hbm_specs(ng),
        out_specs=[tok(D), st, st, tok(512), tok(512), tok(512), pl.BlockSpec((None, 8, 1536), lambda i: (i, 0, 0))]
        + _hbm_specs(ng),
        input_output_aliases={6 + i: 7 + i for i in range(ng)},
        scratch_shapes=[pltpu.VMEM((HEADS, HD, HD), F32), pltpu.VMEM((BLK + 8, 1536), F32)]
        + (_gather_sems(ng) if ng else []),
        compiler_params=_cparams(("arbitrary",)),
    )(p, lnp, sgu_w, sgu_bt, cw, gv, *gather)


def _mixer_backward(p, dmix, saved, lnp, sgu_w, sgu_bt, cw, gv, l, exchange=()):
    t = p.shape[0]
    nb = t // BLK
    s_sv, t_sv, u_sv, w_sv, o_sv, tail_sv = saved
    ne = len(exchange)

    def body(*refs):
        (p_ref, dmix_ref, s_ref, t_ref, u_ref, w_ref, o_ref, tail_ref, lnp_ref, sguw_ref, sgub_ref, cw_ref,
         gv_ref) = refs[:13]
        pairs = refs[13:13 + ne]
        dp_ref, dlnp_ref, dsguw_ref, dsgub_ref, dcw_ref, dgv_ref = refs[13 + ne:19 + ne]
        recvs = refs[19 + ne:19 + 2 * ne]
        ds_scr, xbuf, dcbuf = refs[19 + 2 * ne:22 + 2 * ne]
        esems = refs[22 + 2 * ne:]

        @pl.when(pl.program_id(0) == 0)
        def _():
            if ne:
                for cp in _exchange_copies(pairs, recvs, esems):
                    cp.start()
            ds_scr[...] = jnp.zeros_like(ds_scr)
            dcbuf[BLK:BLK + 8, :] = jnp.zeros((8, 1536), F32)
            dlnp_ref[...] = jnp.zeros_like(dlnp_ref)
            dsguw_ref[...] = jnp.zeros_like(dsguw_ref)
            dsgub_ref[...] = jnp.zeros_like(dsgub_ref)
            dcw_ref[...] = jnp.zeros_like(dcw_ref)
            dgv_ref[...] = jnp.zeros_like(dgv_ref)

        incl, strict = _tri_masks()
        _, ug, vhat, rstd, vn, mixed, wms = _sgu_forward(p_ref, lnp_ref, sguw_ref, sgub_ref)
        dvn_parts, dug_parts = [], []
        dbias = jnp.zeros((BLK, HD), F32)
        for h in range(HEADS):
            sl = slice(h * HD, (h + 1) * HD)
            dy = dmix_ref[:, sl].astype(F32)
            dmx = dy * ug[:, sl]
            dug_parts.append(dy * mixed[h])
            dsguw_ref[h] += jnp.where(incl, _dot_nt(dmx, vn[:, sl]), 0.0)
            dbias = dbias + _put_col(jnp.sum(dmx, axis=1, keepdims=True), h)
            dvn_parts.append(_dot_tn(wms[h], dmx))
        dsgub_ref[...] += dbias
        dvn = jnp.concatenate(dvn_parts, axis=1)
        dug = jnp.concatenate(dug_parts, axis=1)
        dlnp_ref[0:1, :] += jnp.sum(dvn * vhat, axis=0, keepdims=True)
        dlnp_ref[1:2, :] += jnp.sum(dvn, axis=0, keepdims=True)
        dvhat = dvn * lnp_ref[0:1, :]
        dvg = rstd * (dvhat - jnp.mean(dvhat, axis=-1, keepdims=True)
                      - vhat * jnp.mean(dvhat * vhat, axis=-1, keepdims=True))
        dp_ref[:, 0:512] = (dug * _gelu_grad(p_ref[:, 0:512])).astype(BF16)
        dp_ref[:, 512:1024] = (dvg * _gelu_grad(p_ref[:, 512:1024])).astype(BF16)

        xbuf[0:8, :] = tail_ref[...]
        xbuf[8:8 + BLK, :] = p_ref[:, 1024:2560]
        conv = _conv_forward(xbuf, cw_ref)
        act = _silu(conv)
        gt = p_ref[:, GATE0:NW]
        beta, gl, gc, gct, neg_a = _gates(gt, gv_ref)
        gng = gv_ref[2:3, :]
        dbeta_t = jnp.zeros((BLK, HD), F32)
        dgc_t = jnp.zeros((BLK, HD), F32)
        dgng = jnp.zeros((1, HD), F32)
        for h in range(HEADS):
            sl = slice(h * HD, (h + 1) * HD)
            hc = _head_chunk(act, beta, gc, gct, h)
            b, gam, e, dmat, kk = hc["b"], hc["gam"], hc["e"], hc["dmat"], hc["kk"]
            qn, khat, vh = hc["qn"], hc["khat"], hc["vh"]
            gamlast = jnp.exp(hc["glast"])
            s, tm, u, w, o = s_ref[h], t_ref[h], u_ref[:, sl], w_ref[:, sl], o_ref[:, sl]
            ds_next = ds_scr[h]
            z = p_ref[:, 2560 + h * HD:2560 + (h + 1) * HD]
            dy = dmix_ref[:, 512 + h * HD:512 + (h + 1) * HD].astype(F32)
            ro = lax.rsqrt(jnp.mean(o * o, axis=-1, keepdims=True) + RMS_EPS)
            ohat = o * ro
            dp_ref[:, 2560 + h * HD:2560 + (h + 1) * HD] = (dy * ohat * gng * _silu_grad(z)).astype(BF16)
            don = dy * _silu(z)
            dgng = dgng + jnp.sum(don * ohat, axis=0, keepdims=True)
            dohat = don * gng
            do = ro * (dohat - ohat * jnp.mean(dohat * ohat, axis=-1, keepdims=True))
            qk_raw = _dot_nt(qn, khat)
            qkm = qk_raw * dmat
            qd, kd = qn * gam, khat * e
            wn = u - _dot(w, s)
            dwn = _dot_tn(qkm, do) + _dot(kd, ds_next)
            dqd = _dot_nt(do, s)
            dqkm = jnp.where(incl, _dot_nt(do, wn), 0.0)
            ds_scr[h] = _dot_tn(qd, do) + gamlast * ds_next - _dot_tn(w, dwn)
            dgamlast = jnp.sum(jnp.sum(ds_next * s, axis=1, keepdims=True), axis=0, keepdims=True)
            dkd = _dot_nt(wn, ds_next)
            dw = -_dot_nt(dwn, s)
            db1 = _d3_tn(tm, dwn)
            db2 = _d3_tn(tm, dw)
            dm = jnp.where(strict, -(_dot_nt(db1, u) + _dot_nt(db2, w)), 0.0)
            dbeta = (jnp.sum(dm * kk * dmat, axis=1, keepdims=True) + jnp.sum(db1 * vh, axis=1, keepdims=True)
                     + gam * jnp.sum(db2 * khat, axis=1, keepdims=True))
            dkkm = dm * b * dmat
            ddm = dm * b * kk + dqkm * qk_raw
            dgam = b * jnp.sum(db2 * khat, axis=1, keepdims=True) + jnp.sum(dqd * qn, axis=1, keepdims=True)
            g_qk = dqkm * dmat
            dqn = _dot(g_qk, khat) + dqd * gam
            dkhat = ((b * gam) * db2 + _dot_tn(g_qk, qn) + _dot(dkkm, khat) + _dot_tn(dkkm, khat) + dkd * e)
            dvh = b * db1
            rkd = jnp.sum(dkd * kd, axis=1, keepdims=True)
            emat = ddm * dmat
            dgc = (dgam * gam - rkd + jnp.sum(emat, axis=1, keepdims=True)
                   - jnp.sum(emat.T, axis=1, keepdims=True))
            last = _iota2((BLK, 1), 0) == BLK - 1
            dgc = dgc + jnp.where(last, jnp.sum(rkd, axis=0, keepdims=True) + dgamlast * gamlast, 0.0)
            dgc_t = dgc_t + _put_col(dgc, 4 + h)
            dbeta_t = dbeta_t + _put_col(dbeta, h)
            dqhat = dqn * QK_SCALE
            dq = hc["rq"] * (dqhat - hc["qhat"] * jnp.sum(dqhat * hc["qhat"], axis=-1, keepdims=True))
            dk = hc["rk"] * (dkhat - khat * jnp.sum(dkhat * khat, axis=-1, keepdims=True))
            dcbuf[0:BLK, h * HD:(h + 1) * HD] = dq
            dcbuf[0:BLK, 512 + h * HD:512 + (h + 1) * HD] = dk
            dcbuf[0:BLK, 1024 + h * HD:1024 + (h + 1) * HD] = dvh
        dgv_ref[2:3, :] += dgng
        dgl = _dotf_tn(jnp.where(incl, 1.0, 0.0).astype(F32), dgc_t)
        sig_a = _sigmoid(gt + gv_ref[1:2, :])
        d_araw = dgl * neg_a * sig_a
        dgv_ref[0:1, :] += jnp.sum(dgl * gl, axis=0, keepdims=True)
        dgv_ref[1:2, :] += jnp.sum(d_araw, axis=0, keepdims=True)
        dp_ref[:, GATE0:NW] = (dbeta_t * beta * (1.0 - beta) + d_araw).astype(BF16)
        dcbuf[0:BLK, :] = dcbuf[0:BLK, :] * _silu_grad(conv)
        dqkv = cw_ref[0:1, :] * dcbuf[3:3 + BLK, :]
        dcw_ref[0:1, :] += jnp.sum(dcbuf[0:BLK, :] * xbuf[5:5 + BLK, :], axis=0, keepdims=True)
        for j in range(1, 4):
            dqkv = dqkv + cw_ref[j:j + 1, :] * dcbuf[3 - j:3 - j + BLK, :]
            dcw_ref[j:j + 1, :] += jnp.sum(dcbuf[0:BLK, :] * xbuf[5 + j:5 + j + BLK, :], axis=0, keepdims=True)
        dp_ref[:, 1024:2560] = dqkv.astype(BF16)
        dcbuf[BLK:BLK + 8, :] = dcbuf[0:8, :]

        if ne:
            @pl.when(pl.program_id(0) == nb - 1)
            def _():
                for cp in _exchange_copies(pairs, recvs, esems):
                    cp.wait()

    rev = lambda w: pl.BlockSpec((BLK, w), lambda i: (nb - 1 - i, 0))
    st = pl.BlockSpec((None, HEADS, HD, HD), lambda i: (nb - 1 - i, 0, 0, 0))
    fix = lambda *shape: pl.BlockSpec((None,) + shape, lambda i: (l,) + (0,) * len(shape))
    acc = lambda *shape: pl.BlockSpec(shape, lambda i: (0,) * len(shape))
    return pl.pallas_call(
        body, name=f"mixer_bwd_{l}", grid=(nb,),
        out_shape=[jax.ShapeDtypeStruct((t, NW), BF16), jax.ShapeDtypeStruct((8, 512), F32),
                   jax.ShapeDtypeStruct((HEADS, HD, HD), F32), jax.ShapeDtypeStruct((HD, HD), F32),
                   jax.ShapeDtypeStruct((8, 1536), F32), jax.ShapeDtypeStruct((8, HD), F32)]
        + _exchange_shapes(exchange),
        in_specs=[rev(NW), rev(D), st, st, rev(512), rev(512), rev(512),
                  pl.BlockSpec((None, 8, 1536), lambda i: (nb - 1 - i, 0, 0)),
                  fix(8, 512), fix(HEADS, HD, HD), fix(HD, HD), fix(8, 1536), fix(8, HD)] + _hbm_specs(ne),
        out_specs=[rev(NW), acc(8, 512), acc(HEADS, HD, HD), acc(HD, HD), acc(8, 1536), acc(8, HD)]
        + _hbm_specs(ne),
        scratch_shapes=[pltpu.VMEM((HEADS, HD, HD), F32), pltpu.VMEM((BLK + 8, 1536), F32),
                        pltpu.VMEM((BLK + 8, 1536), F32)] + (_exchange_sems(ne) if ne else []),
        compiler_params=_cparams(("arbitrary",)),
    )(p, dmix, s_sv, t_sv, u_sv, w_sv, o_sv, tail_sv, lnp, sgu_w, sgu_bt, cw, gv, *exchange)


_SMALL = (("b_ada", 24), ("norm1_g", 4), ("norm2_g", 4), ("final_g", 1), ("sgu_ln_g", 2), ("sgu_ln_b", 2),
          ("sgu_w", 256), ("sgu_b", 2), ("conv_w", 24), ("a_log", 1), ("dt_bias", 1), ("gdn_norm_g", 1))
_SMALL_ROWS = sum(n for _, n in _SMALL)
_SMALL_PAD = 328
_DMOD_ROWS = 24


def _pack_rows(parts):
    rows = []
    for (name, n), a in zip(_SMALL, parts):
        flat = a.reshape(-1).astype(F32)
        rows.append(jnp.pad(flat, (0, n * D - flat.shape[0])).reshape(n, D))
    rows.append(jnp.zeros((_SMALL_PAD - _SMALL_ROWS, D), F32))
    return jnp.concatenate(rows, axis=0)


def _unpack_rows(buf, shapes):
    out, r0 = {}, 0
    for name, n in _SMALL:
        size = math.prod(shapes[name])
        out[name] = buf[r0:r0 + n].reshape(-1)[:size].reshape(shapes[name])
        r0 += n
    return out


def _pair_combine(own, sib):
    n = own.shape[0] - _DMOD_ROWS

    def body(a_ref, b_ref, o_ref):
        first = lax.axis_index("c") == 0
        a, b = a_ref[0:_DMOD_ROWS, :], b_ref[0:_DMOD_ROWS, :]
        o_ref[0:_DMOD_ROWS, :] = jnp.where(first, a, b)
        o_ref[_DMOD_ROWS:2 * _DMOD_ROWS, :] = jnp.where(first, b, a)
        o_ref[2 * _DMOD_ROWS:, :] = a_ref[_DMOD_ROWS:, :] + b_ref[_DMOD_ROWS:, :]

    return pl.pallas_call(
        body, name="small_pair_combine", out_shape=jax.ShapeDtypeStruct((2 * _DMOD_ROWS + n, D), F32),
        compiler_params=pltpu.CompilerParams(vmem_limit_bytes=VMEM_LIMIT),
    )(own, sib)


def _small_finalize(gathered, w, m, v):
    def body(g_ref, w_ref, m_ref, v_ref, go_ref, d_ref, nm_ref, nv_ref):
        sm = g_ref[0, 2 * _DMOD_ROWS:, :] + g_ref[1, 2 * _DMOD_ROWS:, :]
        sm = sm + g_ref[2, 2 * _DMOD_ROWS:, :]
        sm = sm + g_ref[3, 2 * _DMOD_ROWS:, :]
        bsum = jnp.zeros((_DMOD_ROWS, D), F32)
        for j in range(N_CHIPS):
            bsum = bsum + g_ref[j, 0:_DMOD_ROWS, :]
            bsum = bsum + g_ref[j, _DMOD_ROWS:2 * _DMOD_ROWS, :]
        go_ref[0:_DMOD_ROWS, :] = bsum
        go_ref[_DMOD_ROWS:, :] = sm[_DMOD_ROWS:, :]
        d_ref[...], nm_ref[...], nv_ref[...] = _adam_math(w_ref[...], go_ref[...], m_ref[...], v_ref[...])

    return pl.pallas_call(
        body, name="small_finalize", out_shape=[jax.ShapeDtypeStruct(w.shape, F32)] * 4,
        compiler_params=pltpu.CompilerParams(vmem_limit_bytes=VMEM_LIMIT),
    )(gathered, w, m, v)


def kernel(x, c, w_ada, b_ada, norm1_g, w_in, sgu_ln_g, sgu_ln_b, sgu_w, sgu_b, conv_w, a_log, dt_bias, gdn_norm_g, w_out, norm2_g, w_ff1, w_ff2, final_g, loss_target, m_w_ada, m_b_ada, m_norm1_g, m_w_in, m_sgu_ln_g, m_sgu_ln_b, m_sgu_w, m_sgu_b, m_conv_w, m_a_log, m_dt_bias, m_gdn_norm_g, m_w_out, m_norm2_g, m_w_ff1, m_w_ff2, m_final_g, v_w_ada, v_b_ada, v_norm1_g, v_w_in, v_sgu_ln_g, v_sgu_ln_b, v_sgu_w, v_sgu_b, v_conv_w, v_a_log, v_dt_bias, v_gdn_norm_g, v_w_out, v_norm2_g, v_w_ff1, v_w_ff2, v_final_g):
    xi, yi, ci = lax.axis_index("x"), lax.axis_index("y"), lax.axis_index("c")
    chip = 2 * xi + yi
    dev = 2 * chip + ci
    t = x.shape[1]
    x0 = x.reshape(t, D)
    target = loss_target.reshape(t, D)

    c_sib = _pair_exchange(c, "c_pair")
    c_pair = jnp.where(ci == 0, jnp.concatenate([c, c_sib], 0), jnp.concatenate([c_sib, c], 0))
    c_all = _chip_allgather(c_pair, "c_chips").reshape(8, D)
    ada_cols = w_ada.shape[2]
    b_cols = lax.dynamic_slice_in_dim(b_ada, chip * ada_cols, ada_cols, axis=1)
    mod_part = _ada_forward(c_all, w_ada, b_cols)
    conv_cols = conv_w.shape[2]
    packed = jnp.concatenate([mod_part.reshape(DEPTH * 8, ada_cols), conv_w.reshape(DEPTH, 4 * conv_cols)], axis=0)
    packed = _chip_allgather(packed, "mod_chips")
    mod_all = packed[:, :DEPTH * 8].reshape(N_CHIPS, DEPTH, 8, ada_cols)
    mod_mine = lax.dynamic_index_in_dim(mod_all, dev, axis=2, keepdims=False)
    mod = mod_mine.transpose(1, 0, 2).reshape(DEPTH, 6, D)
    modv = jnp.concatenate([mod, norm1_g[:, None, :], norm2_g[:, None, :]], axis=1)
    conv_full = packed[:, DEPTH * 8:].reshape(N_CHIPS, DEPTH, 4, conv_cols).transpose(1, 2, 0, 3).reshape(DEPTH, 4, 1536)

    place = jnp.stack([chip, ci]).astype(jnp.int32)
    wbufs = [[_cast_into_slot(w, l, place) for w in (w_in, w_out, w_ff1, w_ff2)] for l in range(DEPTH)]
    wbufs[0][:1] = _weights_allgather(wbufs[0][:1], 0)

    def full_w_in(g):
        return jnp.pad(g.transpose(1, 0, 2).reshape(D, IN_W), ((0, 0), (0, NW - IN_W)))

    lnp = jnp.pad(jnp.stack([sgu_ln_g, sgu_ln_b], axis=1), ((0, 0), (0, 6), (0, 0)))
    sgu_bt = jnp.pad(sgu_b.transpose(0, 2, 1), ((0, 0), (0, 0), (0, HD - HEADS)))
    cw = jnp.pad(conv_full, ((0, 0), (0, 4), (0, 0)))
    lane_pad = lambda a: jnp.pad(a, ((0, 0), (4, HD - 8)))
    gv = jnp.pad(jnp.stack([lane_pad(a_log), lane_pad(dt_bias), gdn_norm_g], axis=1), ((0, 0), (0, 5), (0, 0)))

    acts = []
    xl = x0
    for l in range(DEPTH):
        win = full_w_in(wbufs[l][0])
        p, h1 = _fwd_in(xl, modv, win, l)
        nxt = wbufs[l][1:] + (wbufs[l + 1][:1] if l + 1 < DEPTH else [])
        mix, *rest = _mixer_forward(p, lnp, sgu_w, sgu_bt, cw, gv, l, gather=nxt)
        saved = rest[:6]
        wbufs[l][1:] = rest[6:9]
        if l + 1 < DEPTH:
            wbufs[l + 1][:1] = rest[9:]
        g_in, g_out, g_ff1, g_ff2 = wbufs[l]
        x1 = _fwd_out(xl, mix, modv, g_out, l)
        r, h2 = _fwd_ff1(x1, modv, g_ff1, l)
        x2 = _fwd_ff2(x1, r, modv, g_ff2, l)
        acts.append((xl, p, mix, saved, x1, r, h1, h2, win))
        xl = x2

    dx, head_stats = _loss_head(xl, target, final_g)
    loss = lax.psum(jnp.sum(head_stats[1, 0:1]), ("x", "y", "c"))
    d_final_g = head_stats[0]
    names = ("in", "out", "ff1", "ff2")
    grads_buf = [None] * len(names)

    def pair_sums(partials, kinds, lay):
        from_sib = _grads_pair_send(partials, f"{lay}_{kinds[0]}")
        return [(lay, n, _pair_sum(g, ga, place, f"pair_sum_{n}_{lay}")) for g, ga, n in zip(partials, from_sib, kinds)]

    def reduce_into_buffers(items, recv):
        for (lay, n, pair), rc in zip(items, recv):
            i = names.index(n)
            grads_buf[i] = _chip_sum(pair, rc, grads_buf[i], lay, place, f"chip_sum_{n}_{lay}")

    pending = []

    dmod, small = [None] * DEPTH, [None] * DEPTH
    for l in reversed(range(DEPTH)):
        xl, p, mix, saved, x1, r, h1, h2, win = acts[l]
        g_in, g_out, g_ff1, g_ff2 = wbufs[l]
        df = _bwd_ff2(dx, r, modv, g_ff2, l)
        gw_ff2, dg2 = _grad_weight(r, dx, modv, l, "ff2", g_ff2)
        gw_ff1, _ = _grad_weight(h2, df, modv, l, "ff1")
        dx1, st2 = _bwd_norm(df, g_ff1, x1, dx, modv, l, "ff1")
        dmix = _bwd_out(dx1, modv, g_out, l)
        gw_out, dg1 = _grad_weight(mix, dx1, modv, l, "out", g_out)
        pending = pending + pair_sums([gw_out, gw_ff1, gw_ff2], names[1:], l)
        dp, dlnp, dsguw, dsgub, dcw, dgv, *recv = _mixer_backward(p, dmix, saved, lnp, sgu_w, sgu_bt, cw, gv, l,
                                                                  exchange=[item[2] for item in pending])
        reduce_into_buffers(pending, recv)
        gw_in, _ = _grad_weight(h1, dp, modv, l, "in")
        dx, st1 = _bwd_norm(dp, win, xl, dx1, modv, l, "in")
        gw_in_c = gw_in[:, :IN_W].reshape(D, N_CHIPS, IN_W // N_CHIPS).transpose(1, 0, 2)
        pending = pair_sums([gw_in_c], names[:1], l)
        dmod[l] = jnp.stack([st1[0], st1[1], dg1[0], st2[0], st2[1], dg2[0]], axis=0)
        small[l] = dict(norm1_g=st1[2], norm2_g=st2[2], sgu_ln_g=dlnp[0], sgu_ln_b=dlnp[1], sgu_w=dsguw,
                        sgu_b=dsgub[:, :HEADS].T, conv_w=dcw[:4], a_log=dgv[0, 4:8], dt_bias=dgv[1, 4:8],
                        gdn_norm_g=dgv[2])
    grad_x = dx.reshape(1, t, D)

    stack = lambda k: jnp.stack([small[l][k] for l in range(DEPTH)], axis=0)
    small_grads = [jnp.zeros((DEPTH, 6 * D), F32), stack("norm1_g"), stack("norm2_g"), d_final_g, stack("sgu_ln_g"),
                   stack("sgu_ln_b"), stack("sgu_w"), stack("sgu_b"), stack("conv_w"), stack("a_log"),
                   stack("dt_bias"), stack("gdn_norm_g")]
    own = jnp.concatenate([jnp.stack(dmod, axis=0).reshape(_DMOD_ROWS, D), _pack_rows(small_grads)], axis=0)
    sib = _pair_exchange(own, "small_pair")
    gathered = _chip_allgather(_pair_combine(own, sib), "small_chips")
    small_shapes = dict(b_ada=b_ada.shape, norm1_g=norm1_g.shape, norm2_g=norm2_g.shape, final_g=final_g.shape,
                        sgu_ln_g=sgu_ln_g.shape, sgu_ln_b=sgu_ln_b.shape, sgu_w=sgu_w.shape, sgu_b=sgu_b.shape,
                        conv_w=(DEPTH, 4, 1536), a_log=a_log.shape, dt_bias=dt_bias.shape,
                        gdn_norm_g=gdn_norm_g.shape)

    def full_conv(a):
        return lax.dynamic_update_slice_in_dim(jnp.zeros((DEPTH, 4, 1536), F32), a, chip * conv_cols, axis=2)

    def pack_state(b_, n1, n2, fg, lg, lb, sw, sb, cv, al, db, gn):
        return _pack_rows([b_, n1, n2, fg, lg, lb, sw, sb, full_conv(cv), al, db, gn])

    w_small = pack_state(b_ada, norm1_g, norm2_g, final_g, sgu_ln_g, sgu_ln_b, sgu_w, sgu_b, conv_w, a_log, dt_bias,
                         gdn_norm_g)
    m_small = pack_state(m_b_ada, m_norm1_g, m_norm2_g, m_final_g, m_sgu_ln_g, m_sgu_ln_b, m_sgu_w, m_sgu_b, m_conv_w,
                         m_a_log, m_dt_bias, m_gdn_norm_g)
    v_small = pack_state(v_b_ada, v_norm1_g, v_norm2_g, v_final_g, v_sgu_ln_g, v_sgu_ln_b, v_sgu_w, v_sgu_b, v_conv_w,
                         v_a_log, v_dt_bias, v_gdn_norm_g)
    small_out = _small_finalize(gathered, w_small, m_small, v_small)
    sg, sd, sm, sv = [_unpack_rows(a, small_shapes) for a in small_out]
    for dct in (sg, sd, sm, sv):
        dct["conv_w"] = lax.dynamic_slice_in_dim(dct["conv_w"], chip * conv_cols, conv_cols, axis=2)

    dmod_all = gathered[:, :2 * _DMOD_ROWS].reshape(8, DEPTH, 6 * D)
    dmod_cols = lax.dynamic_slice_in_dim(dmod_all, chip * ada_cols, ada_cols, axis=2).transpose(1, 0, 2)
    g_ada, d_ada, nm_ada, nv_ada = _ada_backward_adamw(c_all, dmod_cols, w_ada, m_w_ada, v_w_ada)

    reduce_into_buffers(pending, _grads_chip_exchange([item[2] for item in pending], 0))
    grads = _grads_pair_share(grads_buf)
    big = {}
    for n, g, (w, m, v) in zip(names, grads, ((w_in, m_w_in, v_w_in), (w_out, m_w_out, v_w_out),
                                              (w_ff1, m_w_ff1, v_w_ff1), (w_ff2, m_w_ff2, v_w_ff2))):
        big[n] = (g,) + tuple(_adamw(w, g, m, v, f"adamw_{n}"))

    def outs(k):
        s = (sg, sd, sm, sv)[k]
        return [(g_ada, d_ada, nm_ada, nv_ada)[k], s["b_ada"], s["norm1_g"], big["in"][k], s["sgu_ln_g"],
                s["sgu_ln_b"], s["sgu_w"], s["sgu_b"], s["conv_w"], s["a_log"], s["dt_bias"], s["gdn_norm_g"],
                big["out"][k], s["norm2_g"], big["ff1"][k], big["ff2"][k], s["final_g"]]

    return (loss, grad_x, *outs(0), *outs(1), *outs(2), *outs(3))
```

```python
import functools
import math

import jax
import jax.numpy as jnp
from jax import lax
from jax.experimental import pallas as pl
from jax.experimental.pallas import tpu as pltpu

F32 = jnp.float32
BF16 = jnp.bfloat16

DEPTH = 4
D = 1024
HEADS = 4
HD = 128
BLK = 128
IN_W = 3080
NW = 3200
GATE0 = 3072
DFF = 4096
N_CHIPS = 4
RMS_EPS = 1e-6
LN_EPS = 1e-5
QK_SCALE = HD ** -0.5
LR, B1, B2, ADAM_EPS, WD, STEP = 0.001, 0.9, 0.999, 1e-08, 0.01, 10
VMEM_LIMIT = 56 * 1024 * 1024
MESH = pl.DeviceIdType.MESH
HOPS = ((1, 0), (0, 1), (1, 1))
HI = lax.Precision.HIGHEST


def _dot(a, b):
    return jnp.dot(a.astype(BF16), b.astype(BF16), preferred_element_type=F32)


def _dot_nt(a, b):
    return lax.dot_general(a.astype(BF16), b.astype(BF16), (((1,), (1,)), ((), ())), preferred_element_type=F32)


def _dot_tn(a, b):
    return lax.dot_general(a.astype(BF16), b.astype(BF16), (((0,), (0,)), ((), ())), preferred_element_type=F32)


def _dotf(a, b):
    return jnp.dot(a, b, precision=HI, preferred_element_type=F32)


def _split(a):
    hi = a.astype(BF16)
    return hi, (a - hi.astype(F32)).astype(BF16)


def _dg3(a, b, dims, batch=((), ())):
    ah, al = _split(a)
    bh, bl = _split(b)
    f = lambda x, y: lax.dot_general(x, y, (dims, batch), preferred_element_type=F32)
    return f(ah, bh) + (f(ah, bl) + f(al, bh))


def _bmm3(a, b):
    return _dg3(a, b, ((2,), (1,)), ((0,), (0,)))


def _d3(a, b):
    return _dg3(a, b, ((1,), (0,)))


def _d3_nt(a, b):
    return _dg3(a, b, ((1,), (1,)))


def _d3_tn(a, b):
    return _dg3(a, b, ((0,), (0,)))


def _dotf_tn(a, b):
    return lax.dot_general(a, b, (((0,), (0,)), ((), ())), precision=HI, preferred_element_type=F32)


def _sigmoid(x):
    return 1.0 / (1.0 + jnp.exp(-x))


def _softplus(x):
    return jnp.maximum(x, 0.0) + jnp.log(1.0 + jnp.exp(-jnp.abs(x)))


_G0 = math.sqrt(2.0 / math.pi)
_G1 = 0.044715


def _gelu(x):
    t = jnp.tanh(_G0 * (x + _G1 * x * x * x))
    return 0.5 * x * (1.0 + t)


def _gelu_grad(x):
    t = jnp.tanh(_G0 * (x + _G1 * x * x * x))
    return 0.5 * (1.0 + t) + 0.5 * x * (1.0 - t * t) * (_G0 * (1.0 + 3.0 * _G1 * x * x))


def _silu(x):
    return x * _sigmoid(x)


def _silu_grad(x):
    s = _sigmoid(x)
    return s * (1.0 + x * (1.0 - s))


def _rms_stats(x):
    rstd = lax.rsqrt(jnp.mean(x * x, axis=-1, keepdims=True) + RMS_EPS)
    return x * rstd, rstd


def _norm_mod(x, ng, sc, sh):
    xh, _ = _rms_stats(x)
    return xh * (ng * (1.0 + sc)) + sh


def _norm_mod_bwd(dh, x, ng, sc):
    xh, rstd = _rms_stats(x)
    dsh = jnp.sum(dh, axis=0, keepdims=True)
    dsc = jnp.sum(dh * xh, axis=0, keepdims=True) * ng
    dng = jnp.sum(dh * xh, axis=0, keepdims=True) * (1.0 + sc)
    dxh = dh * (ng * (1.0 + sc))
    dx = rstd * (dxh - xh * jnp.mean(dxh * xh, axis=-1, keepdims=True))
    return dx, dsh, dsc, dng


def _iota2(shape, axis):
    return lax.broadcasted_iota(jnp.int32, shape, axis)


def _col(tile, idx):
    return jnp.sum(jnp.where(_iota2(tile.shape, 1) == idx, tile, 0.0), axis=1, keepdims=True)


def _row(tile, idx):
    return jnp.sum(jnp.where(_iota2(tile.shape, 0) == idx, tile, 0.0), axis=0, keepdims=True)


def _put_col(col, idx, width=HD):
    shape = (col.shape[0], width)
    return jnp.where(_iota2(shape, 1) == idx, jnp.broadcast_to(col, shape), 0.0)


def _tri_inverse(m):
    rows, cols = _iota2(m.shape, m.ndim - 2), _iota2(m.shape, m.ndim - 1)
    mm = _bmm3 if m.ndim == 3 else _d3
    eye = jnp.where(rows == cols, 1.0, 0.0).astype(F32)
    n = jnp.where((rows >> 3) == (cols >> 3), -m, 0.0)
    p = eye + n
    n2 = mm(n, n)
    p = p + mm(n2, p)
    n4 = mm(n2, n2)
    p = p + mm(n4, p)
    for shift in (3, 4, 5, 6):
        same_pair = (rows >> (shift + 1)) == (cols >> (shift + 1))
        below = jnp.logical_and(((rows >> shift) & 1) == 1, ((cols >> shift) & 1) == 0)
        off = jnp.where(jnp.logical_and(same_pair, below), m, 0.0)
        p = p - mm(p, mm(off, p))
    return p


def _cparams(sem=None):
    return pltpu.CompilerParams(dimension_semantics=sem, vmem_limit_bytes=VMEM_LIMIT)


def _my_place():
    return lax.axis_index("x"), lax.axis_index("y"), lax.axis_index("c")


def _hop(xi, yi, hop):
    dx, dy = hop
    return (1 - xi if dx else xi), (1 - yi if dy else yi)


def _pair_exchange(x, name):
    def body(x_ref, o_ref, ssem, rsem):
        xi, yi, ci = _my_place()
        cp = pltpu.make_async_remote_copy(x_ref, o_ref, ssem, rsem, device_id=(xi, yi, 1 - ci), device_id_type=MESH)
        cp.start()
        cp.wait()

    return pl.pallas_call(
        body, name=name, out_shape=jax.ShapeDtypeStruct(x.shape, x.dtype),
        in_specs=[pl.BlockSpec(memory_space=pltpu.VMEM)], out_specs=pl.BlockSpec(memory_space=pltpu.VMEM),
        scratch_shapes=[pltpu.SemaphoreType.DMA, pltpu.SemaphoreType.DMA],
        compiler_params=pltpu.CompilerParams(vmem_limit_bytes=VMEM_LIMIT),
    )(x)


def _chip_allgather(x, name):
    def body(x_ref, o_ref, ssems, rsems, lsem):
        xi, yi, ci = _my_place()
        me = 2 * xi + yi
        loc = pltpu.make_async_copy(x_ref, o_ref.at[me], lsem)
        loc.start()
        sends = []
        for k, hop in enumerate(HOPS):
            tx, ty = _hop(xi, yi, hop)
            cp = pltpu.make_async_remote_copy(x_ref, o_ref.at[me], ssems.at[k], rsems.at[k],
                                              device_id=(tx, ty, ci), device_id_type=MESH)
            cp.start()
            sends.append(cp)
        for k, hop in enumerate(HOPS):
            tx, ty = _hop(xi, yi, hop)
            pltpu.make_async_remote_copy(x_ref, o_ref.at[2 * tx + ty], ssems.at[k], rsems.at[k],
                                         device_id=(tx, ty, ci), device_id_type=MESH).wait_recv()
        for cp in sends:
            cp.wait_send()
        loc.wait()

    return pl.pallas_call(
        body, name=name, out_shape=jax.ShapeDtypeStruct((N_CHIPS,) + x.shape, x.dtype),
        in_specs=[pl.BlockSpec(memory_space=pltpu.VMEM)], out_specs=pl.BlockSpec(memory_space=pltpu.VMEM),
        scratch_shapes=[pltpu.SemaphoreType.DMA((3,)), pltpu.SemaphoreType.DMA((3,)), pltpu.SemaphoreType.DMA],
        compiler_params=pltpu.CompilerParams(vmem_limit_bytes=VMEM_LIMIT),
    )(x)


def _hbm_specs(n):
    return [pl.BlockSpec(memory_space=pl.ANY)] * n


def _cast_into_slot(w, l, place):
    _, r, c = w.shape
    tr = _row_tile(r)

    def body(p_ref, w_ref, o_ref):
        o_ref[...] = w_ref[...].astype(BF16)

    return pl.pallas_call(
        body, name=f"cast_slot_{r}x{c}_{l}", out_shape=jax.ShapeDtypeStruct((N_CHIPS, r, c), BF16),
        grid_spec=pltpu.PrefetchScalarGridSpec(
            num_scalar_prefetch=1, grid=(r // tr,),
            in_specs=[pl.BlockSpec((None, tr, c), lambda k, pr: (l, k, 0))],
            out_specs=pl.BlockSpec((None, tr, c), lambda k, pr: (pr[0], k, 0))),
        compiler_params=_cparams(("parallel",)),
    )(place, w)


def _halves(ref, ci):
    half = ref.shape[-2] // 2
    return pl.ds(ci * half, half), pl.ds((1 - ci) * half, half)


def _gather_copies(bufs, sems):
    s_ici, r_ici, s_d2d, r_d2d = sems
    xi, yi, ci = _my_place()
    me = 2 * xi + yi
    ici_send, ici_recv, d2d_send, d2d_recv = [], [], [], []
    for i, buf in enumerate(bufs):
        mine, sibs = _halves(buf, ci)
        for k, hop in enumerate(HOPS):
            tx, ty = _hop(xi, yi, hop)
            src = 2 * tx + ty
            ici_send.append(pltpu.make_async_remote_copy(buf.at[me, mine], buf.at[me, mine], s_ici.at[i, k],
                                                         r_ici.at[i, k], device_id=(tx, ty, ci), device_id_type=MESH))
            ici_recv.append(pltpu.make_async_remote_copy(buf.at[src, mine], buf.at[src, mine], s_ici.at[i, k],
                                                         r_ici.at[i, k], device_id=(tx, ty, ci), device_id_type=MESH))
            d2d_send.append(pltpu.make_async_remote_copy(buf.at[src, mine], buf.at[src, mine], s_d2d.at[i, k],
                                                         r_d2d.at[i, k], device_id=(xi, yi, 1 - ci), device_id_type=MESH))
            d2d_recv.append(pltpu.make_async_remote_copy(buf.at[src, sibs], buf.at[src, sibs], s_d2d.at[i, k],
                                                         r_d2d.at[i, k], device_id=(xi, yi, 1 - ci), device_id_type=MESH))
    return ici_send, ici_recv, d2d_send, d2d_recv


def _gather_start(bufs, sems):
    for cp in _gather_copies(bufs, sems)[0]:
        cp.start()


def _gather_finish(bufs, sems):
    ici_send, ici_recv, d2d_send, d2d_recv = _gather_copies(bufs, sems)
    for arrived, forward in zip(ici_recv, d2d_send):
        arrived.wait_recv()
        forward.start()
    for cp in d2d_recv:
        cp.wait_recv()
    for cp in ici_send + d2d_send:
        cp.wait_send()


def _gather_sems(n):
    return [pltpu.SemaphoreType.DMA((n, 3))] * 4


def _weights_allgather(bufs, l):
    n = len(bufs)

    def body(*refs):
        outs, sems = refs[n:2 * n], refs[2 * n:]
        _gather_start(outs, sems)
        _gather_finish(outs, sems)

    return pl.pallas_call(
        body, name=f"weights_allgather_{l}",
        out_shape=[jax.ShapeDtypeStruct(b.shape, b.dtype) for b in bufs],
        in_specs=_hbm_specs(n), out_specs=_hbm_specs(n), input_output_aliases={i: i for i in range(n)},
        scratch_shapes=_gather_sems(n),
    )(*bufs)


def _grads_pair_send(gs, l):
    n = len(gs)

    def body(*refs):
        ins, outs, ssem, rsem = refs[:n], refs[n:2 * n], refs[2 * n], refs[2 * n + 1]
        xi, yi, ci = _my_place()
        every = pl.ds(0, N_CHIPS)
        cps = []
        for i in range(n):
            sibs = _halves(ins[i], ci)[1]
            cp = pltpu.make_async_remote_copy(ins[i].at[every, sibs], outs[i], ssem.at[i], rsem.at[i],
                                              device_id=(xi, yi, 1 - ci), device_id_type=MESH)
            cp.start()
            cps.append(cp)
        for cp in cps:
            cp.wait()

    return pl.pallas_call(
        body, name=f"grads_pair_send_{l}",
        out_shape=[jax.ShapeDtypeStruct((N_CHIPS, g.shape[1] // 2, g.shape[2]), g.dtype) for g in gs],
        in_specs=_hbm_specs(n), out_specs=_hbm_specs(n),
        scratch_shapes=[pltpu.SemaphoreType.DMA((n,)), pltpu.SemaphoreType.DMA((n,))],
    )(*gs)


def _exchange_copies(ps, recvs, sems):
    ssems, rsems = sems
    xi, yi, ci = _my_place()
    cps = []
    for i, (p, rc) in enumerate(zip(ps, recvs)):
        for k, hop in enumerate(HOPS):
            tx, ty = _hop(xi, yi, hop)
            cps.append(pltpu.make_async_remote_copy(p.at[2 * tx + ty], rc.at[k], ssems.at[i, k], rsems.at[i, k],
                                                    device_id=(tx, ty, ci), device_id_type=MESH))
    return cps


def _exchange_sems(n):
    return [pltpu.SemaphoreType.DMA((n, 3))] * 2


def _exchange_shapes(ps):
    return [jax.ShapeDtypeStruct((3,) + p.shape[1:], p.dtype) for p in ps]


def _grads_chip_exchange(ps, l):
    n = len(ps)

    def body(*refs):
        cps = _exchange_copies(refs[:n], refs[n:2 * n], refs[2 * n:])
        for cp in cps:
            cp.start()
        for cp in cps:
            cp.wait()

    return pl.pallas_call(
        body, name=f"grads_chip_exchange_{l}", out_shape=_exchange_shapes(ps),
        in_specs=_hbm_specs(n), out_specs=_hbm_specs(n), scratch_shapes=_exchange_sems(n),
    )(*ps)


def _grads_pair_share(gs):
    n = len(gs)

    def body(*refs):
        outs = refs[n:2 * n]
        ssem, rsem = refs[2 * n:]
        xi, yi, ci = _my_place()
        every = pl.ds(0, DEPTH)
        sends = []
        for i in range(n):
            mine = _halves(outs[i], ci)[0]
            cp = pltpu.make_async_remote_copy(outs[i].at[every, mine], outs[i].at[every, mine], ssem.at[i], rsem.at[i],
                                              device_id=(xi, yi, 1 - ci), device_id_type=MESH)
            cp.start()
            sends.append(cp)
        for i in range(n):
            sibs = _halves(outs[i], ci)[1]
            pltpu.make_async_remote_copy(outs[i].at[every, sibs], outs[i].at[every, sibs], ssem.at[i], rsem.at[i],
                                         device_id=(xi, yi, 1 - ci), device_id_type=MESH).wait_recv()
        for cp in sends:
            cp.wait_send()

    return pl.pallas_call(
        body, name="grads_pair_share",
        out_shape=[jax.ShapeDtypeStruct(g.shape, g.dtype) for g in gs],
        in_specs=_hbm_specs(n), out_specs=_hbm_specs(n), input_output_aliases={i: i for i in range(n)},
        scratch_shapes=[pltpu.SemaphoreType.DMA((n,)), pltpu.SemaphoreType.DMA((n,))],
    )(*gs)


def _row_tile(r):
    return min(r, 256)


def _pair_sum(g, ga, place, name):
    _, r, c = g.shape
    tr = _row_tile(r // 2)
    nk = r // 2 // tr

    def body(p_ref, g_ref, ga_ref, o_ref):
        o_ref[...] = (g_ref[...].astype(F32) + ga_ref[...].astype(F32)).astype(o_ref.dtype)

    return pl.pallas_call(
        body, name=name, out_shape=jax.ShapeDtypeStruct(ga.shape, ga.dtype),
        grid_spec=pltpu.PrefetchScalarGridSpec(
            num_scalar_prefetch=1, grid=(N_CHIPS, nk),
            in_specs=[pl.BlockSpec((None, tr, c), lambda j, k, pr: (j, pr[1] * nk + k, 0)),
                      pl.BlockSpec((None, tr, c), lambda j, k, pr: (j, k, 0))],
            out_specs=pl.BlockSpec((None, tr, c), lambda j, k, pr: (j, k, 0))),
        compiler_params=_cparams(("parallel", "parallel")),
    )(place, g, ga)


def _chip_sum(pair, recv, buf, l, place, name):
    _, rh, c = pair.shape
    tr = _row_tile(rh)
    nk = rh // tr

    def body(p_ref, own_ref, r_ref, *rest):
        o_ref = rest[-1]
        acc = own_ref[...].astype(F32) + r_ref[0].astype(F32)
        acc = acc + r_ref[1].astype(F32)
        o_ref[...] = acc + r_ref[2].astype(F32)

    in_specs = [pl.BlockSpec((None, tr, c), lambda k, pr: (pr[0], k, 0)),
                pl.BlockSpec((3, tr, c), lambda k, pr: (0, k, 0))]
    args = [pair, recv]
    aliases = {}
    if buf is not None:
        in_specs.append(pl.BlockSpec(memory_space=pl.ANY))
        args.append(buf)
        aliases = {3: 0}
    return pl.pallas_call(
        body, name=name, out_shape=jax.ShapeDtypeStruct((DEPTH, 2 * rh, c), F32),
        grid_spec=pltpu.PrefetchScalarGridSpec(
            num_scalar_prefetch=1, grid=(nk,), in_specs=in_specs,
            out_specs=pl.BlockSpec((None, tr, c), lambda k, pr: (l, pr[1] * nk + k, 0))),
        input_output_aliases=aliases, compiler_params=_cparams(("parallel",)),
    )(place, *args)


def _adam_math(w, g, m, v):
    m = B1 * m + (1.0 - B1) * g
    v = B2 * v + (1.0 - B2) * (g * g)
    m_hat = m / (1.0 - B1 ** STEP)
    v_hat = v / (1.0 - B2 ** STEP)
    delta = -LR * (m_hat / (jnp.sqrt(v_hat) + ADAM_EPS) + WD * w)
    return delta, m, v


def _adamw(w, g, m, v, name):
    n_l, r, c = w.shape
    tr = _row_tile(r)

    def body(w_ref, g_ref, m_ref, v_ref, d_ref, nm_ref, nv_ref):
        d_ref[...], nm_ref[...], nv_ref[...] = _adam_math(w_ref[...], g_ref[...], m_ref[...], v_ref[...])

    spec = pl.BlockSpec((None, tr, c), lambda i, k: (i, k, 0))
    return pl.pallas_call(
        body, name=name, out_shape=[jax.ShapeDtypeStruct(w.shape, F32)] * 3, grid=(n_l, r // tr),
        in_specs=[spec] * 4, out_specs=[spec] * 3, compiler_params=_cparams(("parallel", "parallel")),
    )(w, g, m, v)


def _ada_forward(c_all, w_ada, b_cols):
    cols = w_ada.shape[2]
    tn = 512

    def body(c_ref, w_ref, b_ref, o_ref):
        o_ref[...] = _dotf(_silu(c_ref[...]), w_ref[...]) + b_ref[...]

    return pl.pallas_call(
        body, name="ada_forward", out_shape=jax.ShapeDtypeStruct((DEPTH, 8, cols), F32), grid=(DEPTH, cols // tn),
        in_specs=[pl.BlockSpec((8, D), lambda l, j: (0, 0)),
                  pl.BlockSpec((None, D, tn), lambda l, j: (l, 0, j)),
                  pl.BlockSpec((None, 1, tn), lambda l, j: (l, 0, j))],
        out_specs=pl.BlockSpec((None, 8, tn), lambda l, j: (l, 0, j)),
        compiler_params=_cparams(("parallel", "parallel")),
    )(c_all, w_ada, b_cols.reshape(DEPTH, 1, cols))


def _ada_backward_adamw(c_all, dmod_cols, w, m, v):
    cols = w.shape[2]
    tn = 512

    def body(c_ref, d_ref, w_ref, m_ref, v_ref, g_ref, dl_ref, nm_ref, nv_ref):
        g = _dotf_tn(_silu(c_ref[...]), d_ref[...])
        g_ref[...] = g
        dl_ref[...], nm_ref[...], nv_ref[...] = _adam_math(w_ref[...], g, m_ref[...], v_ref[...])

    wspec = pl.BlockSpec((None, D, tn), lambda l, j: (l, 0, j))
    return pl.pallas_call(
        body, name="ada_backward_adamw", out_shape=[jax.ShapeDtypeStruct(w.shape, F32)] * 4, grid=(DEPTH, cols // tn),
        in_specs=[pl.BlockSpec((8, D), lambda l, j: (0, 0)), pl.BlockSpec((None, 8, tn), lambda l, j: (l, 0, j)),
                  wspec, wspec, wspec],
        out_specs=[wspec] * 4, compiler_params=_cparams(("parallel", "parallel")),
    )(c_all, dmod_cols, w, m, v)


def _tok_tile(t):
    return min(t, 512)


def _wspec4(r, c, l):
    return pl.BlockSpec((N_CHIPS, r, c), lambda i: (0, 0, 0))


def _fwd_in(x, modv, w_in, l):
    t = x.shape[0]
    tm = _tok_tile(t)

    def body(x_ref, mod_ref, w_ref, o_ref, h_ref):
        h = _norm_mod(x_ref[...], mod_ref[6:7, :], mod_ref[1:2, :], mod_ref[0:1, :]).astype(BF16)
        h_ref[...] = h
        o_ref[...] = jnp.dot(h, w_ref[...], preferred_element_type=F32)

    return pl.pallas_call(
        body, name=f"fwd_in_{l}", grid=(t // tm,),
        out_shape=[jax.ShapeDtypeStruct((t, NW), F32), jax.ShapeDtypeStruct((t, D), BF16)],
        in_specs=[pl.BlockSpec((tm, D), lambda i: (i, 0)), pl.BlockSpec((None, 8, D), lambda i: (l, 0, 0)),
                  pl.BlockSpec((D, NW), lambda i: (0, 0))],
        out_specs=[pl.BlockSpec((tm, NW), lambda i: (i, 0)), pl.BlockSpec((tm, D), lambda i: (i, 0))],
        compiler_params=_cparams(("parallel",)),
    )(x, modv, w_in)


def _fwd_out(x, mix, modv, w_out, l):
    t = x.shape[0]
    tm = _tok_tile(t)

    def body(x_ref, mix_ref, mod_ref, w_ref, o_ref):
        w = w_ref[...].reshape(D, D)
        o_ref[...] = x_ref[...] + mod_ref[2:3, :] * jnp.dot(mix_ref[...], w, preferred_element_type=F32)

    return pl.pallas_call(
        body, name=f"fwd_out_{l}", out_shape=jax.ShapeDtypeStruct((t, D), F32), grid=(t // tm,),
        in_specs=[pl.BlockSpec((tm, D), lambda i: (i, 0)), pl.BlockSpec((tm, D), lambda i: (i, 0)),
                  pl.BlockSpec((None, 8, D), lambda i: (l, 0, 0)), _wspec4(D // N_CHIPS, D, l)],
        out_specs=pl.BlockSpec((tm, D), lambda i: (i, 0)), compiler_params=_cparams(("parallel",)),
    )(x, mix, modv, w_out)


def _fwd_ff1(x, modv, w_ff1, l):
    t = x.shape[0]
    tm = _tok_tile(t)

    def body(x_ref, mod_ref, w_ref, o_ref, h_ref):
        h = _norm_mod(x_ref[...], mod_ref[7:8, :], mod_ref[4:5, :], mod_ref[3:4, :]).astype(BF16)
        h_ref[...] = h
        for j in range(N_CHIPS):
            f = jnp.dot(h, w_ref[j], preferred_element_type=F32)
            o_ref[:, j * D:(j + 1) * D] = jnp.maximum(f, 0.0).astype(BF16)

    return pl.pallas_call(
        body, name=f"fwd_ff1_{l}", grid=(t // tm,),
        out_shape=[jax.ShapeDtypeStruct((t, DFF), BF16), jax.ShapeDtypeStruct((t, D), BF16)],
        in_specs=[pl.BlockSpec((tm, D), lambda i: (i, 0)), pl.BlockSpec((None, 8, D), lambda i: (l, 0, 0)),
                  _wspec4(D, D, l)],
        out_specs=[pl.BlockSpec((tm, DFF), lambda i: (i, 0)), pl.BlockSpec((tm, D), lambda i: (i, 0))],
        compiler_params=_cparams(("parallel",)),
    )(x, modv, w_ff1)


def _fwd_ff2(x, r, modv, w_ff2, l):
    t = x.shape[0]
    tm = _tok_tile(t)

    def body(x_ref, r_ref, mod_ref, w_ref, o_ref):
        acc = jnp.zeros((tm, D), F32)
        for j in range(N_CHIPS):
            rj = r_ref[:, j * D:(j + 1) * D].astype(F32)
            acc = acc + jnp.dot((rj * rj).astype(BF16), w_ref[j], preferred_element_type=F32)
        o_ref[...] = x_ref[...] + mod_ref[5:6, :] * acc

    return pl.pallas_call(
        body, name=f"fwd_ff2_{l}", out_shape=jax.ShapeDtypeStruct((t, D), F32), grid=(t // tm,),
        in_specs=[pl.BlockSpec((tm, D), lambda i: (i, 0)), pl.BlockSpec((tm, DFF), lambda i: (i, 0)),
                  pl.BlockSpec((None, 8, D), lambda i: (l, 0, 0)), _wspec4(D, D, l)],
        out_specs=pl.BlockSpec((tm, D), lambda i: (i, 0)), compiler_params=_cparams(("parallel",)),
    )(x, r, modv, w_ff2)


def _loss_head(x, target, final_g):
    t = x.shape[0]
    tm = _tok_tile(t)

    def body(x_ref, t_ref, g_ref, dx_ref, st_ref):
        @pl.when(pl.program_id(0) == 0)
        def _():
            st_ref[...] = jnp.zeros_like(st_ref)

        xh, rstd = _rms_stats(x_ref[...])
        g = g_ref[...]
        err = xh * g - t_ref[...]
        loss = 0.5 * jnp.sum(jnp.mean(err * err, axis=-1, keepdims=True), axis=0, keepdims=True)
        dy = err * (1.0 / D)
        st_ref[0:1, :] += jnp.sum(dy * xh, axis=0, keepdims=True)
        st_ref[1:2, :] += jnp.broadcast_to(loss, (1, D))
        dxh = dy * g
        dx_ref[...] = rstd * (dxh - xh * jnp.mean(dxh * xh, axis=-1, keepdims=True))

    return pl.pallas_call(
        body, name="loss_head", out_shape=[jax.ShapeDtypeStruct((t, D), F32), jax.ShapeDtypeStruct((8, D), F32)],
        grid=(t // tm,),
        in_specs=[pl.BlockSpec((tm, D), lambda i: (i, 0)), pl.BlockSpec((tm, D), lambda i: (i, 0)),
                  pl.BlockSpec((1, D), lambda i: (0, 0))],
        out_specs=[pl.BlockSpec((tm, D), lambda i: (i, 0)), pl.BlockSpec((8, D), lambda i: (0, 0))],
        compiler_params=_cparams(("arbitrary",)),
    )(x, target, final_g.reshape(1, D))


def _bwd_ff2(dx2, r, modv, w_ff2, l):
    t = dx2.shape[0]
    tm = _tok_tile(t)

    def body(d_ref, r_ref, mod_ref, w_ref, o_ref):
        dyg = (d_ref[...] * mod_ref[5:6, :]).astype(BF16)
        for j in range(N_CHIPS):
            da = lax.dot_general(dyg, w_ref[j], (((1,), (1,)), ((), ())), preferred_element_type=F32)
            o_ref[:, j * D:(j + 1) * D] = (da * 2.0 * r_ref[:, j * D:(j + 1) * D].astype(F32)).astype(BF16)

    return pl.pallas_call(
        body, name=f"bwd_ff2_{l}", out_shape=jax.ShapeDtypeStruct((t, DFF), BF16), grid=(t // tm,),
        in_specs=[pl.BlockSpec((tm, D), lambda i: (i, 0)), pl.BlockSpec((tm, DFF), lambda i: (i, 0)),
                  pl.BlockSpec((None, 8, D), lambda i: (l, 0, 0)), _wspec4(D, D, l)],
        out_specs=pl.BlockSpec((tm, DFF), lambda i: (i, 0)), compiler_params=_cparams(("parallel",)),
    )(dx2, r, modv, w_ff2)


def _bwd_out(dx1, modv, w_out, l):
    t = dx1.shape[0]
    tm = _tok_tile(t)

    def body(d_ref, mod_ref, w_ref, o_ref):
        dyg = (d_ref[...] * mod_ref[2:3, :]).astype(BF16)
        w = w_ref[...].reshape(D, D)
        o_ref[...] = lax.dot_general(dyg, w, (((1,), (1,)), ((), ())), preferred_element_type=F32).astype(BF16)

    return pl.pallas_call(
        body, name=f"bwd_out_{l}", out_shape=jax.ShapeDtypeStruct((t, D), BF16), grid=(t // tm,),
        in_specs=[pl.BlockSpec((tm, D), lambda i: (i, 0)), pl.BlockSpec((None, 8, D), lambda i: (l, 0, 0)),
                  _wspec4(D // N_CHIPS, D, l)],
        out_specs=pl.BlockSpec((tm, D), lambda i: (i, 0)), compiler_params=_cparams(("parallel",)),
    )(dx1, modv, w_out)


def _bwd_norm(dy, w, x, dres, modv, l, which):
    t = x.shape[0]
    tm = _tok_tile(t)
    rows = (6, 1) if which == "in" else (7, 4)
    width = dy.shape[1]

    def body(dy_ref, w_ref, x_ref, dr_ref, mod_ref, dx_ref, st_ref):
        @pl.when(pl.program_id(0) == 0)
        def _():
            st_ref[...] = jnp.zeros_like(st_ref)

        if which == "in":
            dh = lax.dot_general(dy_ref[...], w_ref[...], (((1,), (1,)), ((), ())), preferred_element_type=F32)
        else:
            dh = jnp.zeros((tm, D), F32)
            for j in range(N_CHIPS):
                dh = dh + lax.dot_general(dy_ref[:, j * D:(j + 1) * D], w_ref[j], (((1,), (1,)), ((), ())),
                                          preferred_element_type=F32)
        ng, sc = mod_ref[rows[0]:rows[0] + 1, :], mod_ref[rows[1]:rows[1] + 1, :]
        dx, dsh, dsc, dng = _norm_mod_bwd(dh, x_ref[...], ng, sc)
        dx_ref[...] = dr_ref[...] + dx
        st_ref[0:1, :] += dsh
        st_ref[1:2, :] += dsc
        st_ref[2:3, :] += dng

    wspec = pl.BlockSpec((D, NW), lambda i: (0, 0)) if which == "in" else _wspec4(D, D, l)
    return pl.pallas_call(
        body, name=f"bwd_norm_{which}_{l}",
        out_shape=[jax.ShapeDtypeStruct((t, D), F32), jax.ShapeDtypeStruct((8, D), F32)], grid=(t // tm,),
        in_specs=[pl.BlockSpec((tm, width), lambda i: (i, 0)), wspec, pl.BlockSpec((tm, D), lambda i: (i, 0)),
                  pl.BlockSpec((tm, D), lambda i: (i, 0)), pl.BlockSpec((None, 8, D), lambda i: (l, 0, 0))],
        out_specs=[pl.BlockSpec((tm, D), lambda i: (i, 0)), pl.BlockSpec((8, D), lambda i: (0, 0))],
        compiler_params=_cparams(("arbitrary",)),
    )(dy, w, x, dres, modv)


def _grad_weight(lhs, rhs, modv, l, which, w_gate=None):
    t = lhs.shape[0]
    tm = min(t, 2048)
    nt = t // tm
    gated = which in ("out", "ff2")
    if which == "in":
        nj, lw, rw, orows, ocols = 5, D, NW // 5, D, NW // 5
    elif which == "ff1":
        nj, lw, rw, orows, ocols = N_CHIPS, D, D, D, D
    elif which == "out":
        nj, lw, rw, orows, ocols = N_CHIPS, D // N_CHIPS, D, D // N_CHIPS, D
    else:
        nj, lw, rw, orows, ocols = N_CHIPS, D, D, D, D
    gate_row = 2 if which == "out" else 5

    def body(*refs):
        if gated:
            l_ref, r_ref, mod_ref, wg_ref, o_ref, dg_ref, acc = refs
        else:
            l_ref, r_ref, mod_ref, o_ref, acc = refs
        j, k = pl.program_id(0), pl.program_id(1)

        @pl.when(k == 0)
        def _():
            acc[...] = jnp.zeros_like(acc)

        if which == "ff2":
            lv = l_ref[...].astype(F32)
            lv = lv * lv
        else:
            lv = l_ref[...]
        acc[...] += _dot_tn(lv, r_ref[...])

        if gated:
            @pl.when(jnp.logical_and(j == 0, k == 0))
            def _():
                dg_ref[...] = jnp.zeros_like(dg_ref)

        @pl.when(k == nt - 1)
        def _():
            raw = acc[...]
            if gated:
                o_ref[...] = (raw * mod_ref[gate_row:gate_row + 1, :]).astype(o_ref.dtype)
                dg_ref[0:1, :] += jnp.sum(raw * wg_ref[...].astype(F32), axis=0, keepdims=True)
            else:
                o_ref[...] = raw.astype(o_ref.dtype)

    if which in ("in", "ff1"):
        lspec = pl.BlockSpec((tm, lw), lambda j, k: (k, 0))
        rspec = pl.BlockSpec((tm, rw), lambda j, k: (k, j))
    else:
        lspec = pl.BlockSpec((tm, lw), lambda j, k: (k, j))
        rspec = pl.BlockSpec((tm, rw), lambda j, k: (k, 0))
    mspec = pl.BlockSpec((None, 8, D), lambda j, k: (l, 0, 0))
    if which == "in":
        ospec = pl.BlockSpec((orows, ocols), lambda j, k: (0, j))
        out_shape = [jax.ShapeDtypeStruct((D, NW), BF16)]
    else:
        ospec = pl.BlockSpec((None, orows, ocols), lambda j, k: (j, 0, 0))
        out_shape = [jax.ShapeDtypeStruct((N_CHIPS, orows, ocols), BF16)]
    in_specs = [lspec, rspec, mspec]
    args = [lhs, rhs, modv]
    out_specs = [ospec]
    if gated:
        in_specs.append(pl.BlockSpec((None, orows, ocols), lambda j, k: (j, 0, 0)))
        args.append(w_gate)
        out_specs.append(pl.BlockSpec((8, D), lambda j, k: (0, 0)))
        out_shape.append(jax.ShapeDtypeStruct((8, D), F32))
    res = pl.pallas_call(
        body, name=f"grad_w_{which}_{l}", out_shape=out_shape, grid=(nj, nt), in_specs=in_specs, out_specs=out_specs,
        scratch_shapes=[pltpu.VMEM((orows, ocols), F32)], compiler_params=_cparams(("arbitrary", "arbitrary")),
    )(*args)
    return (res[0], res[1]) if gated else (res[0], None)


def _tri_masks():
    rows, cols = _iota2((BLK, BLK), 0), _iota2((BLK, BLK), 1)
    return rows >= cols, rows > cols


def _sgu_forward(p_ref, lnp_ref, sguw_ref, sgub_ref):
    incl, _ = _tri_masks()
    ug = _gelu(p_ref[:, 0:512])
    vg = _gelu(p_ref[:, 512:1024])
    mu = jnp.mean(vg, axis=-1, keepdims=True)
    xc = vg - mu
    rstd = lax.rsqrt(jnp.mean(xc * xc, axis=-1, keepdims=True) + LN_EPS)
    vhat = xc * rstd
    vn = vhat * lnp_ref[0:1, :] + lnp_ref[1:2, :]
    bias = sgub_ref[...]
    ys, mixed, wms = [], [], []
    for h in range(HEADS):
        wm = jnp.where(incl, sguw_ref[h], 0.0)
        mx = _dot(wm, vn[:, h * HD:(h + 1) * HD]) + _col(bias, h)
        ys.append(ug[:, h * HD:(h + 1) * HD] * mx)
        mixed.append(mx)
        wms.append(wm)
    return ys, ug, vhat, rstd, vn, mixed, wms


def _conv_forward(xbuf, cw_ref):
    conv = cw_ref[0:1, :] * xbuf[5:5 + BLK, :]
    for j in range(1, 4):
        conv = conv + cw_ref[j:j + 1, :] * xbuf[5 + j:5 + j + BLK, :]
    return conv


def _gates(gt, gv_ref):
    incl, _ = _tri_masks()
    beta = _sigmoid(gt)
    neg_a = -jnp.exp(gv_ref[0:1, :])
    gl = neg_a * _softplus(gt + gv_ref[1:2, :])
    gc = _dotf(jnp.where(incl, 1.0, 0.0).astype(F32), gl)
    return beta, gl, gc, gc.T, neg_a


def _head_chunk(act, beta, gc, gct, h):
    incl, strict = _tri_masks()
    qh = act[:, h * HD:(h + 1) * HD]
    kh = act[:, 512 + h * HD:512 + (h + 1) * HD]
    vh = act[:, 1024 + h * HD:1024 + (h + 1) * HD]
    rq = lax.rsqrt(jnp.sum(qh * qh, axis=-1, keepdims=True) + RMS_EPS)
    rk = lax.rsqrt(jnp.sum(kh * kh, axis=-1, keepdims=True) + RMS_EPS)
    qhat, khat = qh * rq, kh * rk
    qn = qhat * QK_SCALE
    b = _col(beta, h)
    gcol = _col(gc, 4 + h)
    grow = _row(gct, 4 + h)
    dmat = jnp.where(incl, jnp.exp(jnp.where(incl, gcol - grow, 0.0)), 0.0)
    gam = jnp.exp(gcol)
    glast = _row(gcol, BLK - 1)
    e = jnp.exp(glast - gcol)
    kk = _d3_nt(khat, khat)
    return dict(qhat=qhat, khat=khat, qn=qn, vh=vh, rq=rq, rk=rk, b=b, dmat=dmat, gam=gam, glast=glast, e=e, kk=kk,
                strict=strict, incl=incl)


def _mixer_forward(p, lnp, sgu_w, sgu_bt, cw, gv, l, gather=()):
    t = p.shape[0]
    nb = t // BLK
    ng = len(gather)

    def body(*refs):
        p_ref, lnp_ref, sguw_ref, sgub_ref, cw_ref, gv_ref = refs[:6]
        mix_ref, s_out, t_out, u_out, w_out, o_out, tail_out = refs[6 + ng:13 + ng]
        gbufs = refs[13 + ng:13 + 2 * ng]
        s_scr, xbuf = refs[13 + 2 * ng:15 + 2 * ng]
        gsems = refs[15 + 2 * ng:]

        @pl.when(pl.program_id(0) == 0)
        def _():
            s_scr[...] = jnp.zeros_like(s_scr)
            xbuf[0:8, :] = jnp.zeros((8, 1536), F32)
            if ng:
                _gather_start(gbufs, gsems)

        ys = _sgu_forward(p_ref, lnp_ref, sguw_ref, sgub_ref)[0]
        for h in range(HEADS):
            mix_ref[:, h * HD:(h + 1) * HD] = ys[h].astype(BF16)

        tail_out[...] = xbuf[0:8, :]
        xbuf[8:8 + BLK, :] = p_ref[:, 1024:2560]
        act = _silu(_conv_forward(xbuf, cw_ref))
        xbuf[0:8, :] = xbuf[BLK:BLK + 8, :]
        beta, _, gc, gct, _ = _gates(p_ref[:, GATE0:NW], gv_ref)
        chunks = [_head_chunk(act, beta, gc, gct, h) for h in range(HEADS)]
        for h, hc in enumerate(chunks):
            t_out[h] = jnp.where(hc["strict"], hc["b"] * hc["kk"] * hc["dmat"], 0.0)
        t_out[...] = _tri_inverse(t_out[...])
        for h, hc in enumerate(chunks):
            tm = t_out[h]
            u = _d3(tm, hc["b"] * hc["vh"])
            w = _d3(tm, (hc["b"] * hc["gam"]) * hc["khat"])
            qkm = _dot_nt(hc["qn"], hc["khat"]) * hc["dmat"]
            s = s_scr[h]
            wn = u - _dot(w, s)
            o = _dot(hc["qn"] * hc["gam"], s) + _dot(qkm, wn)
            s_out[h] = s
            s_scr[h] = jnp.exp(hc["glast"]) * s + _dot_tn(hc["khat"] * hc["e"], wn)
            sl = slice(h * HD, (h + 1) * HD)
            u_out[:, sl] = u
            w_out[:, sl] = w
            o_out[:, sl] = o
            on = o * lax.rsqrt(jnp.mean(o * o, axis=-1, keepdims=True) + RMS_EPS) * gv_ref[2:3, :]
            mix_ref[:, 512 + h * HD:512 + (h + 1) * HD] = (on * _silu(p_ref[:, 2560 + h * HD:2560 + (h + 1) * HD])).astype(BF16)

        if ng:
            @pl.when(pl.program_id(0) == nb - 1)
            def _():
                _gather_finish(gbufs, gsems)

    tok = lambda w: pl.BlockSpec((BLK, w), lambda i: (i, 0))
    st = pl.BlockSpec((None, HEADS, HD, HD), lambda i: (i, 0, 0, 0))
    return pl.pallas_call(
        body, name=f"mixer_fwd_{l}", grid=(nb,),
        out_shape=[jax.ShapeDtypeStruct((t, D), BF16), jax.ShapeDtypeStruct((nb, HEADS, HD, HD), F32),
                   jax.ShapeDtypeStruct((nb, HEADS, HD, HD), F32), jax.ShapeDtypeStruct((t, 512), F32),
                   jax.ShapeDtypeStruct((t, 512), F32), jax.ShapeDtypeStruct((t, 512), F32),
                   jax.ShapeDtypeStruct((nb, 8, 1536), F32)]
        + [jax.ShapeDtypeStruct(b.shape, b.dtype) for b in gather],
        in_specs=[tok(NW), pl.BlockSpec((None, 8, 512), lambda i: (l, 0, 0)),
                  pl.BlockSpec((None, HEADS, HD, HD), lambda i: (l, 0, 0, 0)),
                  pl.BlockSpec((None, HD, HD), lambda i: (l, 0, 0)), pl.BlockSpec((None, 8, 1536), lambda i: (l, 0, 0)),
                  pl.BlockSpec((None, 8, HD), lambda i: (l, 0, 0))] + _hbm_specs(ng),
        out_specs=[tok(D), st, st, tok(512), tok(512), tok(512), pl.BlockSpec((None, 8, 1536), lambda i: (i, 0, 0))]
        + _hbm_specs(ng),
        input_output_aliases={6 + i: 7 + i for i in range(ng)},
        scratch_shapes=[pltpu.VMEM((HEADS, HD, HD), F32), pltpu.VMEM((BLK + 8, 1536), F32)]
        + (_gather_sems(ng) if ng else []),
        compiler_params=_cparams(("arbitrary",)),
    )(p, lnp, sgu_w, sgu_bt, cw, gv, *gather)


def _mixer_backward(p, dmix, saved, lnp, sgu_w, sgu_bt, cw, gv, l, exchange=()):
    t = p.shape[0]
    nb = t // BLK
    s_sv, t_sv, u_sv, w_sv, o_sv, tail_sv = saved
    ne = len(exchange)

    def body(*refs):
        (p_ref, dmix_ref, s_ref, t_ref, u_ref, w_ref, o_ref, tail_ref, lnp_ref, sguw_ref, sgub_ref, cw_ref,
         gv_ref) = refs[:13]
        pairs = refs[13:13 + ne]
        dp_ref, dlnp_ref, dsguw_ref, dsgub_ref, dcw_ref, dgv_ref = refs[13 + ne:19 + ne]
        recvs = refs[19 + ne:19 + 2 * ne]
        ds_scr, xbuf, dcbuf = refs[19 + 2 * ne:22 + 2 * ne]
        esems = refs[22 + 2 * ne:]

        @pl.when(pl.program_id(0) == 0)
        def _():
            if ne:
                for cp in _exchange_copies(pairs, recvs, esems):
                    cp.start()
            ds_scr[...] = jnp.zeros_like(ds_scr)
            dcbuf[BLK:BLK + 8, :] = jnp.zeros((8, 1536), F32)
            dlnp_ref[...] = jnp.zeros_like(dlnp_ref)
            dsguw_ref[...] = jnp.zeros_like(dsguw_ref)
            dsgub_ref[...] = jnp.zeros_like(dsgub_ref)
            dcw_ref[...] = jnp.zeros_like(dcw_ref)
            dgv_ref[...] = jnp.zeros_like(dgv_ref)

        incl, strict = _tri_masks()
        _, ug, vhat, rstd, vn, mixed, wms = _sgu_forward(p_ref, lnp_ref, sguw_ref, sgub_ref)
        dvn_parts, dug_parts = [], []
        dbias = jnp.zeros((BLK, HD), F32)
        for h in range(HEADS):
            sl = slice(h * HD, (h + 1) * HD)
            dy = dmix_ref[:, sl].astype(F32)
            dmx = dy * ug[:, sl]
            dug_parts.append(dy * mixed[h])
            dsguw_ref[h] += jnp.where(incl, _dot_nt(dmx, vn[:, sl]), 0.0)
            dbias = dbias + _put_col(jnp.sum(dmx, axis=1, keepdims=True), h)
            dvn_parts.append(_dot_tn(wms[h], dmx))
        dsgub_ref[...] += dbias
        dvn = jnp.concatenate(dvn_parts, axis=1)
        dug = jnp.concatenate(dug_parts, axis=1)
        dlnp_ref[0:1, :] += jnp.sum(dvn * vhat, axis=0, keepdims=True)
        dlnp_ref[1:2, :] += jnp.sum(dvn, axis=0, keepdims=True)
        dvhat = dvn * lnp_ref[0:1, :]
        dvg = rstd * (dvhat - jnp.mean(dvhat, axis=-1, keepdims=True)
                      - vhat * jnp.mean(dvhat * vhat, axis=-1, keepdims=True))
        dp_ref[:, 0:512] = (dug * _gelu_grad(p_ref[:, 0:512])).astype(BF16)
        dp_ref[:, 512:1024] = (dvg * _gelu_grad(p_ref[:, 512:1024])).astype(BF16)

        xbuf[0:8, :] = tail_ref[...]
        xbuf[8:8 + BLK, :] = p_ref[:, 1024:2560]
        conv = _conv_forward(xbuf, cw_ref)
        act = _silu(conv)
        gt = p_ref[:, GATE0:NW]
        beta, gl, gc, gct, neg_a = _gates(gt, gv_ref)
        gng = gv_ref[2:3, :]
        dbeta_t = jnp.zeros((BLK, HD), F32)
        dgc_t = jnp.zeros((BLK, HD), F32)
        dgng = jnp.zeros((1, HD), F32)
        for h in range(HEADS):
            sl = slice(h * HD, (h + 1) * HD)
            hc = _head_chunk(act, beta, gc, gct, h)
            b, gam, e, dmat, kk = hc["b"], hc["gam"], hc["e"], hc["dmat"], hc["kk"]
            qn, khat, vh = hc["qn"], hc["khat"], hc["vh"]
            gamlast = jnp.exp(hc["glast"])
            s, tm, u, w, o = s_ref[h], t_ref[h], u_ref[:, sl], w_ref[:, sl], o_ref[:, sl]
            ds_next = ds_scr[h]
            z = p_ref[:, 2560 + h * HD:2560 + (h + 1) * HD]
            dy = dmix_ref[:, 512 + h * HD:512 + (h + 1) * HD].astype(F32)
            ro = lax.rsqrt(jnp.mean(o * o, axis=-1, keepdims=True) + RMS_EPS)
            ohat = o * ro
            dp_ref[:, 2560 + h * HD:2560 + (h + 1) * HD] = (dy * ohat * gng * _silu_grad(z)).astype(BF16)
            don = dy * _silu(z)
            dgng = dgng + jnp.sum(don * ohat, axis=0, keepdims=True)
            dohat = don * gng
            do = ro * (dohat - ohat * jnp.mean(dohat * ohat, axis=-1, keepdims=True))
            qk_raw = _dot_nt(qn, khat)
            qkm = qk_raw * dmat
            qd, kd = qn * gam, khat * e
            wn = u - _dot(w, s)
            dwn = _dot_tn(qkm, do) + _dot(kd, ds_next)
            dqd = _dot_nt(do, s)
            dqkm = jnp.where(incl, _dot_nt(do, wn), 0.0)
            ds_scr[h] = _dot_tn(qd, do) + gamlast * ds_next - _dot_tn(w, dwn)
            dgamlast = jnp.sum(jnp.sum(ds_next * s, axis=1, keepdims=True), axis=0, keepdims=True)
            dkd = _dot_nt(wn, ds_next)
            dw = -_dot_nt(dwn, s)
            db1 = _d3_tn(tm, dwn)
            db2 = _d3_tn(tm, dw)
            dm = jnp.where(strict, -(_dot_nt(db1, u) + _dot_nt(db2, w)), 0.0)
            dbeta = (jnp.sum(dm * kk * dmat, axis=1, keepdims=True) + jnp.sum(db1 * vh, axis=1, keepdims=True)
                     + gam * jnp.sum(db2 * khat, axis=1, keepdims=True))
            dkkm = dm * b * dmat
            ddm = dm * b * kk + dqkm * qk_raw
            dgam = b * jnp.sum(db2 * khat, axis=1, keepdims=True) + jnp.sum(dqd * qn, axis=1, keepdims=True)
            g_qk = dqkm * dmat
            dqn = _dot(g_qk, khat) + dqd * gam
            dkhat = ((b * gam) * db2 + _dot_tn(g_qk, qn) + _dot(dkkm, khat) + _dot_tn(dkkm, khat) + dkd * e)
            dvh = b * db1
            rkd = jnp.sum(dkd * kd, axis=1, keepdims=True)
            emat = ddm * dmat
            dgc = (dgam * gam - rkd + jnp.sum(emat, axis=1, keepdims=True)
                   - jnp.sum(emat.T, axis=1, keepdims=True))
            last = _iota2((BLK, 1), 0) == BLK - 1
            dgc = dgc + jnp.where(last, jnp.sum(rkd, axis=0, keepdims=True) + dgamlast * gamlast, 0.0)
            dgc_t = dgc_t + _put_col(dgc, 4 + h)
            dbeta_t = dbeta_t + _put_col(dbeta, h)
            dqhat = dqn * QK_SCALE
            dq = hc["rq"] * (dqhat - hc["qhat"] * jnp.sum(dqhat * hc["qhat"], axis=-1, keepdims=True))
            dk = hc["rk"] * (dkhat - khat * jnp.sum(dkhat * khat, axis=-1, keepdims=True))
            dcbuf[0:BLK, h * HD:(h + 1) * HD] = dq
            dcbuf[0:BLK, 512 + h * HD:512 + (h + 1) * HD] = dk
            dcbuf[0:BLK, 1024 + h * HD:1024 + (h + 1) * HD] = dvh
        dgv_ref[2:3, :] += dgng
        dgl = _dotf_tn(jnp.where(incl, 1.0, 0.0).astype(F32), dgc_t)
        sig_a = _sigmoid(gt + gv_ref[1:2, :])
        d_araw = dgl * neg_a * sig_a
        dgv_ref[0:1, :] += jnp.sum(dgl * gl, axis=0, keepdims=True)
        dgv_ref[1:2, :] += jnp.sum(d_araw, axis=0, keepdims=True)
        dp_ref[:, GATE0:NW] = (dbeta_t * beta * (1.0 - beta) + d_araw).astype(BF16)
        dcbuf[0:BLK, :] = dcbuf[0:BLK, :] * _silu_grad(conv)
        xcur = xbuf[8:8 + BLK, :]
        dqkv = jnp.zeros((BLK, 1536), F32)
        for j in range(4):
            shifted = dcbuf[3 - j:3 - j + BLK, :]
            dqkv = dqkv + cw_ref[j:j + 1, :] * shifted
            dcw_ref[j:j + 1, :] += jnp.sum(shifted * xcur, axis=0, keepdims=True)
        dp_ref[:, 1024:2560] = dqkv.astype(BF16)
        dcbuf[BLK:BLK + 8, :] = dcbuf[0:8, :]

        if ne:
            @pl.when(pl.program_id(0) == nb - 1)
            def _():
                for cp in _exchange_copies(pairs, recvs, esems):
                    cp.wait()

    rev = lambda w: pl.BlockSpec((BLK, w), lambda i: (nb - 1 - i, 0))
    st = pl.BlockSpec((None, HEADS, HD, HD), lambda i: (nb - 1 - i, 0, 0, 0))
    fix = lambda *shape: pl.BlockSpec((None,) + shape, lambda i: (l,) + (0,) * len(shape))
    acc = lambda *shape: pl.BlockSpec(shape, lambda i: (0,) * len(shape))
    return pl.pallas_call(
        body, name=f"mixer_bwd_{l}", grid=(nb,),
        out_shape=[jax.ShapeDtypeStruct((t, NW), BF16), jax.ShapeDtypeStruct((8, 512), F32),
                   jax.ShapeDtypeStruct((HEADS, HD, HD), F32), jax.ShapeDtypeStruct((HD, HD), F32),
                   jax.ShapeDtypeStruct((8, 1536), F32), jax.ShapeDtypeStruct((8, HD), F32)]
        + _exchange_shapes(exchange),
        in_specs=[rev(NW), rev(D), st, st, rev(512), rev(512), rev(512),
                  pl.BlockSpec((None, 8, 1536), lambda i: (nb - 1 - i, 0, 0)),
                  fix(8, 512), fix(HEADS, HD, HD), fix(HD, HD), fix(8, 1536), fix(8, HD)] + _hbm_specs(ne),
        out_specs=[rev(NW), acc(8, 512), acc(HEADS, HD, HD), acc(HD, HD), acc(8, 1536), acc(8, HD)]
        + _hbm_specs(ne),
        scratch_shapes=[pltpu.VMEM((HEADS, HD, HD), F32), pltpu.VMEM((BLK + 8, 1536), F32),
                        pltpu.VMEM((BLK + 8, 1536), F32)] + (_exchange_sems(ne) if ne else []),
        compiler_params=_cparams(("arbitrary",)),
    )(p, dmix, s_sv, t_sv, u_sv, w_sv, o_sv, tail_sv, lnp, sgu_w, sgu_bt, cw, gv, *exchange)


_SMALL = (("b_ada", 24), ("norm1_g", 4), ("norm2_g", 4), ("final_g", 1), ("sgu_ln_g", 2), ("sgu_ln_b", 2),
          ("sgu_w", 256), ("sgu_b", 2), ("conv_w", 24), ("a_log", 1), ("dt_bias", 1), ("gdn_norm_g", 1))
_SMALL_ROWS = sum(n for _, n in _SMALL)
_SMALL_PAD = 328
_DMOD_ROWS = 24


def _pack_rows(parts):
    rows = []
    for (name, n), a in zip(_SMALL, parts):
        flat = a.reshape(-1).astype(F32)
        rows.append(jnp.pad(flat, (0, n * D - flat.shape[0])).reshape(n, D))
    rows.append(jnp.zeros((_SMALL_PAD - _SMALL_ROWS, D), F32))
    return jnp.concatenate(rows, axis=0)


def _unpack_rows(buf, shapes):
    out, r0 = {}, 0
    for name, n in _SMALL:
        size = math.prod(shapes[name])
        out[name] = buf[r0:r0 + n].reshape(-1)[:size].reshape(shapes[name])
        r0 += n
    return out


def _pair_combine(own, sib):
    n = own.shape[0] - _DMOD_ROWS

    def body(a_ref, b_ref, o_ref):
        first = lax.axis_index("c") == 0
        a, b = a_ref[0:_DMOD_ROWS, :], b_ref[0:_DMOD_ROWS, :]
        o_ref[0:_DMOD_ROWS, :] = jnp.where(first, a, b)
        o_ref[_DMOD_ROWS:2 * _DMOD_ROWS, :] = jnp.where(first, b, a)
        o_ref[2 * _DMOD_ROWS:, :] = a_ref[_DMOD_ROWS:, :] + b_ref[_DMOD_ROWS:, :]

    return pl.pallas_call(
        body, name="small_pair_combine", out_shape=jax.ShapeDtypeStruct((2 * _DMOD_ROWS + n, D), F32),
        compiler_params=pltpu.CompilerParams(vmem_limit_bytes=VMEM_LIMIT),
    )(own, sib)


def _small_finalize(gathered, w, m, v):
    def body(g_ref, w_ref, m_ref, v_ref, go_ref, d_ref, nm_ref, nv_ref):
        sm = g_ref[0, 2 * _DMOD_ROWS:, :] + g_ref[1, 2 * _DMOD_ROWS:, :]
        sm = sm + g_ref[2, 2 * _DMOD_ROWS:, :]
        sm = sm + g_ref[3, 2 * _DMOD_ROWS:, :]
        bsum = jnp.zeros((_DMOD_ROWS, D), F32)
        for j in range(N_CHIPS):
            bsum = bsum + g_ref[j, 0:_DMOD_ROWS, :]
            bsum = bsum + g_ref[j, _DMOD_ROWS:2 * _DMOD_ROWS, :]
        go_ref[0:_DMOD_ROWS, :] = bsum
        go_ref[_DMOD_ROWS:, :] = sm[_DMOD_ROWS:, :]
        d_ref[...], nm_ref[...], nv_ref[...] = _adam_math(w_ref[...], go_ref[...], m_ref[...], v_ref[...])

    return pl.pallas_call(
        body, name="small_finalize", out_shape=[jax.ShapeDtypeStruct(w.shape, F32)] * 4,
        compiler_params=pltpu.CompilerParams(vmem_limit_bytes=VMEM_LIMIT),
    )(gathered, w, m, v)


def kernel(x, c, w_ada, b_ada, norm1_g, w_in, sgu_ln_g, sgu_ln_b, sgu_w, sgu_b, conv_w, a_log, dt_bias, gdn_norm_g, w_out, norm2_g, w_ff1, w_ff2, final_g, loss_target, m_w_ada, m_b_ada, m_norm1_g, m_w_in, m_sgu_ln_g, m_sgu_ln_b, m_sgu_w, m_sgu_b, m_conv_w, m_a_log, m_dt_bias, m_gdn_norm_g, m_w_out, m_norm2_g, m_w_ff1, m_w_ff2, m_final_g, v_w_ada, v_b_ada, v_norm1_g, v_w_in, v_sgu_ln_g, v_sgu_ln_b, v_sgu_w, v_sgu_b, v_conv_w, v_a_log, v_dt_bias, v_gdn_norm_g, v_w_out, v_norm2_g, v_w_ff1, v_w_ff2, v_final_g):
    xi, yi, ci = lax.axis_index("x"), lax.axis_index("y"), lax.axis_index("c")
    chip = 2 * xi + yi
    dev = 2 * chip + ci
    t = x.shape[1]
    x0 = x.reshape(t, D)
    target = loss_target.reshape(t, D)

    c_sib = _pair_exchange(c, "c_pair")
    c_pair = jnp.where(ci == 0, jnp.concatenate([c, c_sib], 0), jnp.concatenate([c_sib, c], 0))
    c_all = _chip_allgather(c_pair, "c_chips").reshape(8, D)
    ada_cols = w_ada.shape[2]
    b_cols = lax.dynamic_slice_in_dim(b_ada, chip * ada_cols, ada_cols, axis=1)
    mod_part = _ada_forward(c_all, w_ada, b_cols)
    conv_cols = conv_w.shape[2]
    packed = jnp.concatenate([mod_part.reshape(DEPTH * 8, ada_cols), conv_w.reshape(DEPTH, 4 * conv_cols)], axis=0)
    packed = _chip_allgather(packed, "mod_chips")
    mod_all = packed[:, :DEPTH * 8].reshape(N_CHIPS, DEPTH, 8, ada_cols)
    mod_mine = lax.dynamic_index_in_dim(mod_all, dev, axis=2, keepdims=False)
    mod = mod_mine.transpose(1, 0, 2).reshape(DEPTH, 6, D)
    modv = jnp.concatenate([mod, norm1_g[:, None, :], norm2_g[:, None, :]], axis=1)
    conv_full = packed[:, DEPTH * 8:].reshape(N_CHIPS, DEPTH, 4, conv_cols).transpose(1, 2, 0, 3).reshape(DEPTH, 4, 1536)

    place = jnp.stack([chip, ci]).astype(jnp.int32)
    wbufs = [[_cast_into_slot(w, l, place) for w in (w_in, w_out, w_ff1, w_ff2)] for l in range(DEPTH)]
    wbufs[0][:1] = _weights_allgather(wbufs[0][:1], 0)

    def full_w_in(g):
        return jnp.pad(g.transpose(1, 0, 2).reshape(D, IN_W), ((0, 0), (0, NW - IN_W)))

    lnp = jnp.pad(jnp.stack([sgu_ln_g, sgu_ln_b], axis=1), ((0, 0), (0, 6), (0, 0)))
    sgu_bt = jnp.pad(sgu_b.transpose(0, 2, 1), ((0, 0), (0, 0), (0, HD - HEADS)))
    cw = jnp.pad(conv_full, ((0, 0), (0, 4), (0, 0)))
    lane_pad = lambda a: jnp.pad(a, ((0, 0), (4, HD - 8)))
    gv = jnp.pad(jnp.stack([lane_pad(a_log), lane_pad(dt_bias), gdn_norm_g], axis=1), ((0, 0), (0, 5), (0, 0)))

    acts = []
    xl = x0
    for l in range(DEPTH):
        win = full_w_in(wbufs[l][0])
        p, h1 = _fwd_in(xl, modv, win, l)
        nxt = wbufs[l][1:] + (wbufs[l + 1][:1] if l + 1 < DEPTH else [])
        mix, *rest = _mixer_forward(p, lnp, sgu_w, sgu_bt, cw, gv, l, gather=nxt)
        saved = rest[:6]
        wbufs[l][1:] = rest[6:9]
        if l + 1 < DEPTH:
            wbufs[l + 1][:1] = rest[9:]
        g_in, g_out, g_ff1, g_ff2 = wbufs[l]
        x1 = _fwd_out(xl, mix, modv, g_out, l)
        r, h2 = _fwd_ff1(x1, modv, g_ff1, l)
        x2 = _fwd_ff2(x1, r, modv, g_ff2, l)
        acts.append((xl, p, mix, saved, x1, r, h1, h2, win))
        xl = x2

    dx, head_stats = _loss_head(xl, target, final_g)
    loss = lax.psum(jnp.sum(head_stats[1, 0:1]), ("x", "y", "c"))
    d_final_g = head_stats[0]
    names = ("in", "out", "ff1", "ff2")
    grads_buf = [None] * len(names)

    def pair_sums(partials, kinds, lay):
        from_sib = _grads_pair_send(partials, f"{lay}_{kinds[0]}")
        return [(lay, n, _pair_sum(g, ga, place, f"pair_sum_{n}_{lay}")) for g, ga, n in zip(partials, from_sib, kinds)]

    def reduce_into_buffers(items, recv):
        for (lay, n, pair), rc in zip(items, recv):
            i = names.index(n)
            grads_buf[i] = _chip_sum(pair, rc, grads_buf[i], lay, place, f"chip_sum_{n}_{lay}")

    pending = []

    dmod, small = [None] * DEPTH, [None] * DEPTH
    for l in reversed(range(DEPTH)):
        xl, p, mix, saved, x1, r, h1, h2, win = acts[l]
        g_in, g_out, g_ff1, g_ff2 = wbufs[l]
        df = _bwd_ff2(dx, r, modv, g_ff2, l)
        gw_ff2, dg2 = _grad_weight(r, dx, modv, l, "ff2", g_ff2)
        gw_ff1, _ = _grad_weight(h2, df, modv, l, "ff1")
        dx1, st2 = _bwd_norm(df, g_ff1, x1, dx, modv, l, "ff1")
        dmix = _bwd_out(dx1, modv, g_out, l)
        gw_out, dg1 = _grad_weight(mix, dx1, modv, l, "out", g_out)
        pending = pending + pair_sums([gw_out, gw_ff1, gw_ff2], names[1:], l)
        dp, dlnp, dsguw, dsgub, dcw, dgv, *recv = _mixer_backward(p, dmix, saved, lnp, sgu_w, sgu_bt, cw, gv, l,
                                                                  exchange=[item[2] for item in pending])
        reduce_into_buffers(pending, recv)
        gw_in, _ = _grad_weight(h1, dp, modv, l, "in")
        dx, st1 = _bwd_norm(dp, win, xl, dx1, modv, l, "in")
        gw_in_c = gw_in[:, :IN_W].reshape(D, N_CHIPS, IN_W // N_CHIPS).transpose(1, 0, 2)
        pending = pair_sums([gw_in_c], names[:1], l)
        dmod[l] = jnp.stack([st1[0], st1[1], dg1[0], st2[0], st2[1], dg2[0]], axis=0)
        small[l] = dict(norm1_g=st1[2], norm2_g=st2[2], sgu_ln_g=dlnp[0], sgu_ln_b=dlnp[1], sgu_w=dsguw,
                        sgu_b=dsgub[:, :HEADS].T, conv_w=dcw[:4], a_log=dgv[0, 4:8], dt_bias=dgv[1, 4:8],
                        gdn_norm_g=dgv[2])
    grad_x = dx.reshape(1, t, D)

    stack = lambda k: jnp.stack([small[l][k] for l in range(DEPTH)], axis=0)
    small_grads = [jnp.zeros((DEPTH, 6 * D), F32), stack("norm1_g"), stack("norm2_g"), d_final_g, stack("sgu_ln_g"),
                   stack("sgu_ln_b"), stack("sgu_w"), stack("sgu_b"), stack("conv_w"), stack("a_log"),
                   stack("dt_bias"), stack("gdn_norm_g")]
    own = jnp.concatenate([jnp.stack(dmod, axis=0).reshape(_DMOD_ROWS, D), _pack_rows(small_grads)], axis=0)
    sib = _pair_exchange(own, "small_pair")
    gathered = _chip_allgather(_pair_combine(own, sib), "small_chips")
    small_shapes = dict(b_ada=b_ada.shape, norm1_g=norm1_g.shape, norm2_g=norm2_g.shape, final_g=final_g.shape,
                        sgu_ln_g=sgu_ln_g.shape, sgu_ln_b=sgu_ln_b.shape, sgu_w=sgu_w.shape, sgu_b=sgu_b.shape,
                        conv_w=(DEPTH, 4, 1536), a_log=a_log.shape, dt_bias=dt_bias.shape,
                        gdn_norm_g=gdn_norm_g.shape)

    def full_conv(a):
        return lax.dynamic_update_slice_in_dim(jnp.zeros((DEPTH, 4, 1536), F32), a, chip * conv_cols, axis=2)

    def pack_state(b_, n1, n2, fg, lg, lb, sw, sb, cv, al, db, gn):
        return _pack_rows([b_, n1, n2, fg, lg, lb, sw, sb, full_conv(cv), al, db, gn])

    w_small = pack_state(b_ada, norm1_g, norm2_g, final_g, sgu_ln_g, sgu_ln_b, sgu_w, sgu_b, conv_w, a_log, dt_bias,
                         gdn_norm_g)
    m_small = pack_state(m_b_ada, m_norm1_g, m_norm2_g, m_final_g, m_sgu_ln_g, m_sgu_ln_b, m_sgu_w, m_sgu_b, m_conv_w,
                         m_a_log, m_dt_bias, m_gdn_norm_g)
    v_small = pack_state(v_b_ada, v_norm1_g, v_norm2_g, v_final_g, v_sgu_ln_g, v_sgu_ln_b, v_sgu_w, v_sgu_b, v_conv_w,
                         v_a_log, v_dt_bias, v_gdn_norm_g)
    small_out = _small_finalize(gathered, w_small, m_small, v_small)
    sg, sd, sm, sv = [_unpack_rows(a, small_shapes) for a in small_out]
    for dct in (sg, sd, sm, sv):
        dct["conv_w"] = lax.dynamic_slice_in_dim(dct["conv_w"], chip * conv_cols, conv_cols, axis=2)

    dmod_all = gathered[:, :2 * _DMOD_ROWS].reshape(8, DEPTH, 6 * D)
    dmod_cols = lax.dynamic_slice_in_dim(dmod_all, chip * ada_cols, ada_cols, axis=2).transpose(1, 0, 2)
    g_ada, d_ada, nm_ada, nv_ada = _ada_backward_adamw(c_all, dmod_cols, w_ada, m_w_ada, v_w_ada)

    reduce_into_buffers(pending, _grads_chip_exchange([item[2] for item in pending], 0))
    grads = _grads_pair_share(grads_buf)
    big = {}
    for n, g, (w, m, v) in zip(names, grads, ((w_in, m_w_in, v_w_in), (w_out, m_w_out, v_w_out),
                                              (w_ff1, m_w_ff1, v_w_ff1), (w_ff2, m_w_ff2, v_w_ff2))):
        big[n] = (g,) + tuple(_adamw(w, g, m, v, f"adamw_{n}"))

    def outs(k):
        s = (sg, sd, sm, sv)[k]
        return [(g_ada, d_ada, nm_ada, nv_ada)[k], s["b_ada"], s["norm1_g"], big["in"][k], s["sgu_ln_g"],
                s["sgu_ln_b"], s["sgu_w"], s["sgu_b"], s["conv_w"], s["a_log"], s["dt_bias"], s["gdn_norm_g"],
                big["out"][k], s["norm2_g"], big["ff1"][k], big["ff2"][k], s["final_g"]]

    return (loss, grad_x, *outs(0), *outs(1), *outs(2), *outs(3))
```

```python
import functools
import math

import jax
import jax.numpy as jnp
from jax import lax
from jax.experimental import pallas as pl
from jax.experimental.pallas import tpu as pltpu

F32 = jnp.float32
BF16 = jnp.bfloat16

DEPTH = 4
D = 1024
HEADS = 4
HD = 128
BLK = 128
IN_W = 3080
NW = 3200
GATE0 = 3072
DFF = 4096
N_CHIPS = 4
RMS_EPS = 1e-6
LN_EPS = 1e-5
QK_SCALE = HD ** -0.5
LR, B1, B2, ADAM_EPS, WD, STEP = 0.001, 0.9, 0.999, 1e-08, 0.01, 10
VMEM_LIMIT = 56 * 1024 * 1024
MESH = pl.DeviceIdType.MESH
HOPS = ((1, 0), (0, 1), (1, 1))
HI = lax.Precision.HIGHEST


def _dot(a, b):
    return jnp.dot(a.astype(BF16), b.astype(BF16), preferred_element_type=F32)


def _dot_nt(a, b):
    return lax.dot_general(a.astype(BF16), b.astype(BF16), (((1,), (1,)), ((), ())), preferred_element_type=F32)


def _dot_tn(a, b):
    return lax.dot_general(a.astype(BF16), b.astype(BF16), (((0,), (0,)), ((), ())), preferred_element_type=F32)


def _dotf(a, b):
    return jnp.dot(a, b, precision=HI, preferred_element_type=F32)


def _split(a):
    hi = a.astype(BF16)
    return hi, (a - hi.astype(F32)).astype(BF16)


def _dg3(a, b, dims, batch=((), ())):
    ah, al = _split(a)
    bh, bl = _split(b)
    f = lambda x, y: lax.dot_general(x, y, (dims, batch), preferred_element_type=F32)
    return f(ah, bh) + (f(ah, bl) + f(al, bh))


def _bmm3(a, b):
    return _dg3(a, b, ((2,), (1,)), ((0,), (0,)))


def _d3(a, b):
    return _dg3(a, b, ((1,), (0,)))


def _d3_nt(a, b):
    return _dg3(a, b, ((1,), (1,)))


def _d3_tn(a, b):
    return _dg3(a, b, ((0,), (0,)))


def _dotf_tn(a, b):
    return lax.dot_general(a, b, (((0,), (0,)), ((), ())), precision=HI, preferred_element_type=F32)


def _sigmoid(x):
    return 1.0 / (1.0 + jnp.exp(-x))


def _softplus(x):
    return jnp.maximum(x, 0.0) + jnp.log(1.0 + jnp.exp(-jnp.abs(x)))


_G0 = math.sqrt(2.0 / math.pi)
_G1 = 0.044715


def _gelu(x):
    t = jnp.tanh(_G0 * (x + _G1 * x * x * x))
    return 0.5 * x * (1.0 + t)


def _gelu_grad(x):
    t = jnp.tanh(_G0 * (x + _G1 * x * x * x))
    return 0.5 * (1.0 + t) + 0.5 * x * (1.0 - t * t) * (_G0 * (1.0 + 3.0 * _G1 * x * x))


def _silu(x):
    return x * _sigmoid(x)


def _silu_grad(x):
    s = _sigmoid(x)
    return s * (1.0 + x * (1.0 - s))


def _rms_stats(x):
    rstd = lax.rsqrt(jnp.mean(x * x, axis=-1, keepdims=True) + RMS_EPS)
    return x * rstd, rstd


def _norm_mod(x, ng, sc, sh):
    xh, _ = _rms_stats(x)
    return xh * (ng * (1.0 + sc)) + sh


def _norm_mod_bwd(dh, x, ng, sc):
    xh, rstd = _rms_stats(x)
    dsh = jnp.sum(dh, axis=0, keepdims=True)
    dsc = jnp.sum(dh * xh, axis=0, keepdims=True) * ng
    dng = jnp.sum(dh * xh, axis=0, keepdims=True) * (1.0 + sc)
    dxh = dh * (ng * (1.0 + sc))
    dx = rstd * (dxh - xh * jnp.mean(dxh * xh, axis=-1, keepdims=True))
    return dx, dsh, dsc, dng


def _iota2(shape, axis):
    return lax.broadcasted_iota(jnp.int32, shape, axis)


def _col(tile, idx):
    return jnp.sum(jnp.where(_iota2(tile.shape, 1) == idx, tile, 0.0), axis=1, keepdims=True)


def _row(tile, idx):
    return jnp.sum(jnp.where(_iota2(tile.shape, 0) == idx, tile, 0.0), axis=0, keepdims=True)


def _put_col(col, idx, width=HD):
    shape = (col.shape[0], width)
    return jnp.where(_iota2(shape, 1) == idx, jnp.broadcast_to(col, shape), 0.0)


def _tri_inverse(m):
    rows, cols = _iota2(m.shape, m.ndim - 2), _iota2(m.shape, m.ndim - 1)
    mm = _bmm3 if m.ndim == 3 else _d3
    eye = jnp.where(rows == cols, 1.0, 0.0).astype(F32)
    n = jnp.where((rows >> 3) == (cols >> 3), -m, 0.0)
    p = eye + n
    n2 = mm(n, n)
    p = p + mm(n2, p)
    n4 = mm(n2, n2)
    p = p + mm(n4, p)
    for shift in (3, 4, 5, 6):
        same_pair = (rows >> (shift + 1)) == (cols >> (shift + 1))
        below = jnp.logical_and(((rows >> shift) & 1) == 1, ((cols >> shift) & 1) == 0)
        off = jnp.where(jnp.logical_and(same_pair, below), m, 0.0)
        p = p - mm(p, mm(off, p))
    return p


def _cparams(sem=None):
    return pltpu.CompilerParams(dimension_semantics=sem, vmem_limit_bytes=VMEM_LIMIT)


def _my_place():
    return lax.axis_index("x"), lax.axis_index("y"), lax.axis_index("c")


def _hop(xi, yi, hop):
    dx, dy = hop
    return (1 - xi if dx else xi), (1 - yi if dy else yi)


def _pair_exchange(x, name):
    def body(x_ref, o_ref, ssem, rsem):
        xi, yi, ci = _my_place()
        cp = pltpu.make_async_remote_copy(x_ref, o_ref, ssem, rsem, device_id=(xi, yi, 1 - ci), device_id_type=MESH)
        cp.start()
        cp.wait()

    return pl.pallas_call(
        body, name=name, out_shape=jax.ShapeDtypeStruct(x.shape, x.dtype),
        in_specs=[pl.BlockSpec(memory_space=pltpu.VMEM)], out_specs=pl.BlockSpec(memory_space=pltpu.VMEM),
        scratch_shapes=[pltpu.SemaphoreType.DMA, pltpu.SemaphoreType.DMA],
        compiler_params=pltpu.CompilerParams(vmem_limit_bytes=VMEM_LIMIT),
    )(x)


def _allgather_start(x_ref, o_ref, ssems, rsems, lsem):
    xi, yi, ci = _my_place()
    me = 2 * xi + yi
    pltpu.make_async_copy(x_ref, o_ref.at[me], lsem).start()
    for k, hop in enumerate(HOPS):
        tx, ty = _hop(xi, yi, hop)
        pltpu.make_async_remote_copy(x_ref, o_ref.at[me], ssems.at[k], rsems.at[k],
                                     device_id=(tx, ty, ci), device_id_type=MESH).start()


def _allgather_finish(x_ref, o_ref, ssems, rsems, lsem):
    xi, yi, ci = _my_place()
    me = 2 * xi + yi
    for k, hop in enumerate(HOPS):
        tx, ty = _hop(xi, yi, hop)
        cp = pltpu.make_async_remote_copy(x_ref, o_ref.at[2 * tx + ty], ssems.at[k], rsems.at[k],
                                          device_id=(tx, ty, ci), device_id_type=MESH)
        cp.wait_recv()
        cp.wait_send()
    pltpu.make_async_copy(x_ref, o_ref.at[me], lsem).wait()


_ALLGATHER_SEMS = [pltpu.SemaphoreType.DMA((3,)), pltpu.SemaphoreType.DMA((3,)), pltpu.SemaphoreType.DMA]


def _chip_allgather(x, name):
    def body(x_ref, o_ref, ssems, rsems, lsem):
        _allgather_start(x_ref, o_ref, ssems, rsems, lsem)
        _allgather_finish(x_ref, o_ref, ssems, rsems, lsem)

    return pl.pallas_call(
        body, name=name, out_shape=jax.ShapeDtypeStruct((N_CHIPS,) + x.shape, x.dtype),
        in_specs=[pl.BlockSpec(memory_space=pltpu.VMEM)], out_specs=pl.BlockSpec(memory_space=pltpu.VMEM),
        scratch_shapes=_ALLGATHER_SEMS, compiler_params=pltpu.CompilerParams(vmem_limit_bytes=VMEM_LIMIT),
    )(x)


def _hbm_specs(n):
    return [pl.BlockSpec(memory_space=pl.ANY)] * n


def _cast_into_slot(w, l, place):
    _, r, c = w.shape
    tr = _row_tile(r)

    def body(p_ref, w_ref, o_ref):
        o_ref[...] = w_ref[...].astype(BF16)

    return pl.pallas_call(
        body, name=f"cast_slot_{r}x{c}_{l}", out_shape=jax.ShapeDtypeStruct((N_CHIPS, r, c), BF16),
        grid_spec=pltpu.PrefetchScalarGridSpec(
            num_scalar_prefetch=1, grid=(r // tr,),
            in_specs=[pl.BlockSpec((None, tr, c), lambda k, pr: (l, k, 0))],
            out_specs=pl.BlockSpec((None, tr, c), lambda k, pr: (pr[0], k, 0))),
        compiler_params=_cparams(("parallel",)),
    )(place, w)


def _halves(ref, ci):
    half = ref.shape[-2] // 2
    return pl.ds(ci * half, half), pl.ds((1 - ci) * half, half)


def _gather_copies(bufs, sems):
    s_ici, r_ici, s_d2d, r_d2d = sems
    xi, yi, ci = _my_place()
    me = 2 * xi + yi
    ici_send, ici_recv, d2d_send, d2d_recv = [], [], [], []
    for i, buf in enumerate(bufs):
        mine, sibs = _halves(buf, ci)
        for k, hop in enumerate(HOPS):
            tx, ty = _hop(xi, yi, hop)
            src = 2 * tx + ty
            ici_send.append(pltpu.make_async_remote_copy(buf.at[me, mine], buf.at[me, mine], s_ici.at[i, k],
                                                         r_ici.at[i, k], device_id=(tx, ty, ci), device_id_type=MESH))
            ici_recv.append(pltpu.make_async_remote_copy(buf.at[src, mine], buf.at[src, mine], s_ici.at[i, k],
                                                         r_ici.at[i, k], device_id=(tx, ty, ci), device_id_type=MESH))
            d2d_send.append(pltpu.make_async_remote_copy(buf.at[src, mine], buf.at[src, mine], s_d2d.at[i, k],
                                                         r_d2d.at[i, k], device_id=(xi, yi, 1 - ci), device_id_type=MESH))
            d2d_recv.append(pltpu.make_async_remote_copy(buf.at[src, sibs], buf.at[src, sibs], s_d2d.at[i, k],
                                                         r_d2d.at[i, k], device_id=(xi, yi, 1 - ci), device_id_type=MESH))
    return ici_send, ici_recv, d2d_send, d2d_recv


def _gather_start(bufs, sems):
    for cp in _gather_copies(bufs, sems)[0]:
        cp.start()


def _gather_forward(bufs, sems):
    _, ici_recv, d2d_send, _ = _gather_copies(bufs, sems)
    for arrived, forward in zip(ici_recv, d2d_send):
        arrived.wait_recv()
        forward.start()


def _gather_finish(bufs, sems):
    ici_send, _, d2d_send, d2d_recv = _gather_copies(bufs, sems)
    for cp in d2d_recv:
        cp.wait_recv()
    for cp in ici_send + d2d_send:
        cp.wait_send()


def _gather_sems(n):
    return [pltpu.SemaphoreType.DMA((n, 3))] * 4


def _weights_allgather(bufs, l):
    n = len(bufs)

    def body(*refs):
        outs, sems = refs[n:2 * n], refs[2 * n:]
        _gather_start(outs, sems)
        _gather_forward(outs, sems)
        _gather_finish(outs, sems)

    return pl.pallas_call(
        body, name=f"weights_allgather_{l}",
        out_shape=[jax.ShapeDtypeStruct(b.shape, b.dtype) for b in bufs],
        in_specs=_hbm_specs(n), out_specs=_hbm_specs(n), input_output_aliases={i: i for i in range(n)},
        scratch_shapes=_gather_sems(n),
    )(*bufs)


def _pair_send_copies(gs, outs, sems):
    ssem, rsem = sems
    xi, yi, ci = _my_place()
    every = pl.ds(0, N_CHIPS)
    return [pltpu.make_async_remote_copy(g.at[every, _halves(g, ci)[1]], o, ssem.at[i], rsem.at[i],
                                         device_id=(xi, yi, 1 - ci), device_id_type=MESH)
            for i, (g, o) in enumerate(zip(gs, outs))]


def _pair_send_shapes(gs):
    return [jax.ShapeDtypeStruct((N_CHIPS, g.shape[1] // 2, g.shape[2]), g.dtype) for g in gs]


def _pair_send_sems(n):
    return [pltpu.SemaphoreType.DMA((n,)), pltpu.SemaphoreType.DMA((n,))]


def _grads_pair_send(gs, l):
    n = len(gs)

    def body(*refs):
        cps = _pair_send_copies(refs[:n], refs[n:2 * n], refs[2 * n:])
        for cp in cps:
            cp.start()
        for cp in cps:
            cp.wait()

    return pl.pallas_call(
        body, name=f"grads_pair_send_{l}", out_shape=_pair_send_shapes(gs),
        in_specs=_hbm_specs(n), out_specs=_hbm_specs(n), scratch_shapes=_pair_send_sems(n),
    )(*gs)


def _exchange_copies(ps, recvs, sems):
    ssems, rsems = sems
    xi, yi, ci = _my_place()
    cps = []
    for i, (p, rc) in enumerate(zip(ps, recvs)):
        for k, hop in enumerate(HOPS):
            tx, ty = _hop(xi, yi, hop)
            cps.append(pltpu.make_async_remote_copy(p.at[2 * tx + ty], rc.at[k], ssems.at[i, k], rsems.at[i, k],
                                                    device_id=(tx, ty, ci), device_id_type=MESH))
    return cps


def _exchange_sems(n):
    return [pltpu.SemaphoreType.DMA((n, 3))] * 2


def _exchange_shapes(ps):
    return [jax.ShapeDtypeStruct((3,) + p.shape[1:], p.dtype) for p in ps]


def _grads_chip_exchange(ps, l, small):
    n = len(ps)

    def body(*refs):
        pairs, x_ref = refs[:n], refs[n]
        recvs, o_ref = refs[n + 1:2 * n + 1], refs[2 * n + 1]
        esems, asems = refs[2 * n + 2:2 * n + 4], refs[2 * n + 4:]
        cps = _exchange_copies(pairs, recvs, esems)
        for cp in cps:
            cp.start()
        _allgather_start(x_ref, o_ref, *asems)
        _allgather_finish(x_ref, o_ref, *asems)
        for cp in cps:
            cp.wait()

    vmem = pl.BlockSpec(memory_space=pltpu.VMEM)
    res = pl.pallas_call(
        body, name=f"grads_chip_exchange_{l}",
        out_shape=_exchange_shapes(ps) + [jax.ShapeDtypeStruct((N_CHIPS,) + small.shape, small.dtype)],
        in_specs=_hbm_specs(n) + [vmem], out_specs=_hbm_specs(n) + [vmem],
        scratch_shapes=_exchange_sems(n) + _ALLGATHER_SEMS,
        compiler_params=pltpu.CompilerParams(vmem_limit_bytes=VMEM_LIMIT),
    )(*ps, small)
    return res[:n], res[n]


def _grads_pair_share(gs):
    n = len(gs)

    def body(*refs):
        outs = refs[n:2 * n]
        ssem, rsem = refs[2 * n:]
        xi, yi, ci = _my_place()
        every = pl.ds(0, DEPTH)
        sends = []
        for i in range(n):
            mine = _halves(outs[i], ci)[0]
            cp = pltpu.make_async_remote_copy(outs[i].at[every, mine], outs[i].at[every, mine], ssem.at[i], rsem.at[i],
                                              device_id=(xi, yi, 1 - ci), device_id_type=MESH)
            cp.start()
            sends.append(cp)
        for i in range(n):
            sibs = _halves(outs[i], ci)[1]
            pltpu.make_async_remote_copy(outs[i].at[every, sibs], outs[i].at[every, sibs], ssem.at[i], rsem.at[i],
                                         device_id=(xi, yi, 1 - ci), device_id_type=MESH).wait_recv()
        for cp in sends:
            cp.wait_send()

    return pl.pallas_call(
        body, name="grads_pair_share",
        out_shape=[jax.ShapeDtypeStruct(g.shape, g.dtype) for g in gs],
        in_specs=_hbm_specs(n), out_specs=_hbm_specs(n), input_output_aliases={i: i for i in range(n)},
        scratch_shapes=[pltpu.SemaphoreType.DMA((n,)), pltpu.SemaphoreType.DMA((n,))],
    )(*gs)


def _row_tile(r):
    return min(r, 256)


def _pair_sum(g, ga, place, name):
    _, r, c = g.shape
    tr = _row_tile(r // 2)
    nk = r // 2 // tr

    def body(p_ref, g_ref, ga_ref, o_ref):
        o_ref[...] = (g_ref[...].astype(F32) + ga_ref[...].astype(F32)).astype(o_ref.dtype)

    return pl.pallas_call(
        body, name=name, out_shape=jax.ShapeDtypeStruct(ga.shape, ga.dtype),
        grid_spec=pltpu.PrefetchScalarGridSpec(
            num_scalar_prefetch=1, grid=(N_CHIPS, nk),
            in_specs=[pl.BlockSpec((None, tr, c), lambda j, k, pr: (j, pr[1] * nk + k, 0)),
                      pl.BlockSpec((None, tr, c), lambda j, k, pr: (j, k, 0))],
            out_specs=pl.BlockSpec((None, tr, c), lambda j, k, pr: (j, k, 0))),
        compiler_params=_cparams(("parallel", "parallel")),
    )(place, g, ga)


def _chip_sum(pair, recv, buf, l, place, name):
    _, rh, c = pair.shape
    tr = _row_tile(rh)
    nk = rh // tr

    def body(p_ref, own_ref, r_ref, *rest):
        o_ref = rest[-1]
        acc = own_ref[...].astype(F32) + r_ref[0].astype(F32)
        acc = acc + r_ref[1].astype(F32)
        o_ref[...] = acc + r_ref[2].astype(F32)

    in_specs = [pl.BlockSpec((None, tr, c), lambda k, pr: (pr[0], k, 0)),
                pl.BlockSpec((3, tr, c), lambda k, pr: (0, k, 0))]
    args = [pair, recv]
    aliases = {}
    if buf is not None:
        in_specs.append(pl.BlockSpec(memory_space=pl.ANY))
        args.append(buf)
        aliases = {3: 0}
    return pl.pallas_call(
        body, name=name, out_shape=jax.ShapeDtypeStruct((DEPTH, 2 * rh, c), F32),
        grid_spec=pltpu.PrefetchScalarGridSpec(
            num_scalar_prefetch=1, grid=(nk,), in_specs=in_specs,
            out_specs=pl.BlockSpec((None, tr, c), lambda k, pr: (l, pr[1] * nk + k, 0))),
        input_output_aliases=aliases, compiler_params=_cparams(("parallel",)),
    )(place, *args)


def _adam_math(w, g, m, v):
    m = B1 * m + (1.0 - B1) * g
    v = B2 * v + (1.0 - B2) * (g * g)
    m_hat = m / (1.0 - B1 ** STEP)
    v_hat = v / (1.0 - B2 ** STEP)
    delta = -LR * (m_hat / (jnp.sqrt(v_hat) + ADAM_EPS) + WD * w)
    return delta, m, v


def _adamw(w, g, m, v, name):
    n_l, r, c = w.shape
    tr = _row_tile(r)

    def body(w_ref, g_ref, m_ref, v_ref, d_ref, nm_ref, nv_ref):
        d_ref[...], nm_ref[...], nv_ref[...] = _adam_math(w_ref[...], g_ref[...], m_ref[...], v_ref[...])

    spec = pl.BlockSpec((None, tr, c), lambda i, k: (i, k, 0))
    return pl.pallas_call(
        body, name=name, out_shape=[jax.ShapeDtypeStruct(w.shape, F32)] * 3, grid=(n_l, r // tr),
        in_specs=[spec] * 4, out_specs=[spec] * 3, compiler_params=_cparams(("parallel", "parallel")),
    )(w, g, m, v)


def _ada_forward(c_all, w_ada, b_cols):
    cols = w_ada.shape[2]
    tn = 512

    def body(c_ref, w_ref, b_ref, o_ref):
        o_ref[...] = _dotf(_silu(c_ref[...]), w_ref[...]) + b_ref[...]

    return pl.pallas_call(
        body, name="ada_forward", out_shape=jax.ShapeDtypeStruct((DEPTH, 8, cols), F32), grid=(DEPTH, cols // tn),
        in_specs=[pl.BlockSpec((8, D), lambda l, j: (0, 0)),
                  pl.BlockSpec((None, D, tn), lambda l, j: (l, 0, j)),
                  pl.BlockSpec((None, 1, tn), lambda l, j: (l, 0, j))],
        out_specs=pl.BlockSpec((None, 8, tn), lambda l, j: (l, 0, j)),
        compiler_params=_cparams(("parallel", "parallel")),
    )(c_all, w_ada, b_cols.reshape(DEPTH, 1, cols))


def _ada_backward_adamw(c_all, dmod_cols, w, m, v):
    cols = w.shape[2]
    tn = 512

    def body(c_ref, d_ref, w_ref, m_ref, v_ref, g_ref, dl_ref, nm_ref, nv_ref):
        g = _dotf_tn(_silu(c_ref[...]), d_ref[...])
        g_ref[...] = g
        dl_ref[...], nm_ref[...], nv_ref[...] = _adam_math(w_ref[...], g, m_ref[...], v_ref[...])

    wspec = pl.BlockSpec((None, D, tn), lambda l, j: (l, 0, j))
    return pl.pallas_call(
        body, name="ada_backward_adamw", out_shape=[jax.ShapeDtypeStruct(w.shape, F32)] * 4, grid=(DEPTH, cols // tn),
        in_specs=[pl.BlockSpec((8, D), lambda l, j: (0, 0)), pl.BlockSpec((None, 8, tn), lambda l, j: (l, 0, j)),
                  wspec, wspec, wspec],
        out_specs=[wspec] * 4, compiler_params=_cparams(("parallel", "parallel")),
    )(c_all, dmod_cols, w, m, v)


def _tok_tile(t):
    return min(t, 512)


def _wspec4(r, c, l):
    return pl.BlockSpec((N_CHIPS, r, c), lambda i: (0, 0, 0))


def _fwd_in(x, modv, w_in, l):
    t = x.shape[0]
    tm = _tok_tile(t)

    def body(x_ref, mod_ref, w_ref, o_ref, h_ref):
        h = _norm_mod(x_ref[...], mod_ref[6:7, :], mod_ref[1:2, :], mod_ref[0:1, :]).astype(BF16)
        h_ref[...] = h
        o_ref[...] = jnp.dot(h, w_ref[...], preferred_element_type=F32)

    return pl.pallas_call(
        body, name=f"fwd_in_{l}", grid=(t // tm,),
        out_shape=[jax.ShapeDtypeStruct((t, NW), F32), jax.ShapeDtypeStruct((t, D), BF16)],
        in_specs=[pl.BlockSpec((tm, D), lambda i: (i, 0)), pl.BlockSpec((None, 8, D), lambda i: (l, 0, 0)),
                  pl.BlockSpec((D, NW), lambda i: (0, 0))],
        out_specs=[pl.BlockSpec((tm, NW), lambda i: (i, 0)), pl.BlockSpec((tm, D), lambda i: (i, 0))],
        compiler_params=_cparams(("parallel",)),
    )(x, modv, w_in)


def _fwd_out(x, mix, modv, w_out, l):
    t = x.shape[0]
    tm = _tok_tile(t)

    def body(x_ref, mix_ref, mod_ref, w_ref, o_ref):
        w = w_ref[...].reshape(D, D)
        o_ref[...] = x_ref[...] + mod_ref[2:3, :] * jnp.dot(mix_ref[...], w, preferred_element_type=F32)

    return pl.pallas_call(
        body, name=f"fwd_out_{l}", out_shape=jax.ShapeDtypeStruct((t, D), F32), grid=(t // tm,),
        in_specs=[pl.BlockSpec((tm, D), lambda i: (i, 0)), pl.BlockSpec((tm, D), lambda i: (i, 0)),
                  pl.BlockSpec((None, 8, D), lambda i: (l, 0, 0)), _wspec4(D // N_CHIPS, D, l)],
        out_specs=pl.BlockSpec((tm, D), lambda i: (i, 0)), compiler_params=_cparams(("parallel",)),
    )(x, mix, modv, w_out)


def _fwd_ff1(x, modv, w_ff1, l):
    t = x.shape[0]
    tm = _tok_tile(t)

    def body(x_ref, mod_ref, w_ref, o_ref, h_ref):
        h = _norm_mod(x_ref[...], mod_ref[7:8, :], mod_ref[4:5, :], mod_ref[3:4, :]).astype(BF16)
        h_ref[...] = h
        for j in range(N_CHIPS):
            f = jnp.dot(h, w_ref[j], preferred_element_type=F32)
            o_ref[:, j * D:(j + 1) * D] = jnp.maximum(f, 0.0).astype(BF16)

    return pl.pallas_call(
        body, name=f"fwd_ff1_{l}", grid=(t // tm,),
        out_shape=[jax.ShapeDtypeStruct((t, DFF), BF16), jax.ShapeDtypeStruct((t, D), BF16)],
        in_specs=[pl.BlockSpec((tm, D), lambda i: (i, 0)), pl.BlockSpec((None, 8, D), lambda i: (l, 0, 0)),
                  _wspec4(D, D, l)],
        out_specs=[pl.BlockSpec((tm, DFF), lambda i: (i, 0)), pl.BlockSpec((tm, D), lambda i: (i, 0))],
        compiler_params=_cparams(("parallel",)),
    )(x, modv, w_ff1)


def _fwd_ff2(x, r, modv, w_ff2, l):
    t = x.shape[0]
    tm = _tok_tile(t)

    def body(x_ref, r_ref, mod_ref, w_ref, o_ref):
        acc = jnp.zeros((tm, D), F32)
        for j in range(N_CHIPS):
            rj = r_ref[:, j * D:(j + 1) * D].astype(F32)
            acc = acc + jnp.dot((rj * rj).astype(BF16), w_ref[j], preferred_element_type=F32)
        o_ref[...] = x_ref[...] + mod_ref[5:6, :] * acc

    return pl.pallas_call(
        body, name=f"fwd_ff2_{l}", out_shape=jax.ShapeDtypeStruct((t, D), F32), grid=(t // tm,),
        in_specs=[pl.BlockSpec((tm, D), lambda i: (i, 0)), pl.BlockSpec((tm, DFF), lambda i: (i, 0)),
                  pl.BlockSpec((None, 8, D), lambda i: (l, 0, 0)), _wspec4(D, D, l)],
        out_specs=pl.BlockSpec((tm, D), lambda i: (i, 0)), compiler_params=_cparams(("parallel",)),
    )(x, r, modv, w_ff2)


def _loss_head(x, target, final_g):
    t = x.shape[0]
    tm = _tok_tile(t)

    def body(x_ref, t_ref, g_ref, dx_ref, st_ref):
        @pl.when(pl.program_id(0) == 0)
        def _():
            st_ref[...] = jnp.zeros_like(st_ref)

        xh, rstd = _rms_stats(x_ref[...])
        g = g_ref[...]
        err = xh * g - t_ref[...]
        loss = 0.5 * jnp.sum(jnp.mean(err * err, axis=-1, keepdims=True), axis=0, keepdims=True)
        dy = err * (1.0 / D)
        st_ref[0:1, :] += jnp.sum(dy * xh, axis=0, keepdims=True)
        st_ref[1:2, :] += jnp.broadcast_to(loss, (1, D))
        dxh = dy * g
        dx_ref[...] = rstd * (dxh - xh * jnp.mean(dxh * xh, axis=-1, keepdims=True))

    return pl.pallas_call(
        body, name="loss_head", out_shape=[jax.ShapeDtypeStruct((t, D), F32), jax.ShapeDtypeStruct((8, D), F32)],
        grid=(t // tm,),
        in_specs=[pl.BlockSpec((tm, D), lambda i: (i, 0)), pl.BlockSpec((tm, D), lambda i: (i, 0)),
                  pl.BlockSpec((1, D), lambda i: (0, 0))],
        out_specs=[pl.BlockSpec((tm, D), lambda i: (i, 0)), pl.BlockSpec((8, D), lambda i: (0, 0))],
        compiler_params=_cparams(("arbitrary",)),
    )(x, target, final_g.reshape(1, D))


def _bwd_ff2(dx2, r, modv, w_ff2, l):
    t = dx2.shape[0]
    tm = _tok_tile(t)

    def body(d_ref, r_ref, mod_ref, w_ref, o_ref):
        dyg = (d_ref[...] * mod_ref[5:6, :]).astype(BF16)
        for j in range(N_CHIPS):
            da = lax.dot_general(dyg, w_ref[j], (((1,), (1,)), ((), ())), preferred_element_type=F32)
            o_ref[:, j * D:(j + 1) * D] = (da * 2.0 * r_ref[:, j * D:(j + 1) * D].astype(F32)).astype(BF16)

    return pl.pallas_call(
        body, name=f"bwd_ff2_{l}", out_shape=jax.ShapeDtypeStruct((t, DFF), BF16), grid=(t // tm,),
        in_specs=[pl.BlockSpec((tm, D), lambda i: (i, 0)), pl.BlockSpec((tm, DFF), lambda i: (i, 0)),
                  pl.BlockSpec((None, 8, D), lambda i: (l, 0, 0)), _wspec4(D, D, l)],
        out_specs=pl.BlockSpec((tm, DFF), lambda i: (i, 0)), compiler_params=_cparams(("parallel",)),
    )(dx2, r, modv, w_ff2)


def _bwd_out(dx1, modv, w_out, l):
    t = dx1.shape[0]
    tm = _tok_tile(t)

    def body(d_ref, mod_ref, w_ref, o_ref):
        dyg = (d_ref[...] * mod_ref[2:3, :]).astype(BF16)
        w = w_ref[...].reshape(D, D)
        o_ref[...] = lax.dot_general(dyg, w, (((1,), (1,)), ((), ())), preferred_element_type=F32).astype(BF16)

    return pl.pallas_call(
        body, name=f"bwd_out_{l}", out_shape=jax.ShapeDtypeStruct((t, D), BF16), grid=(t // tm,),
        in_specs=[pl.BlockSpec((tm, D), lambda i: (i, 0)), pl.BlockSpec((None, 8, D), lambda i: (l, 0, 0)),
                  _wspec4(D // N_CHIPS, D, l)],
        out_specs=pl.BlockSpec((tm, D), lambda i: (i, 0)), compiler_params=_cparams(("parallel",)),
    )(dx1, modv, w_out)


def _bwd_norm(dy, w, x, dres, modv, l, which, send=()):
    t = x.shape[0]
    tm = _tok_tile(t)
    nsteps = t // tm
    rows = (6, 1) if which == "in" else (7, 4)
    width = dy.shape[1]
    ns = len(send)

    def body(*refs):
        dy_ref, w_ref, x_ref, dr_ref, mod_ref = refs[:5]
        parts = refs[5:5 + ns]
        dx_ref, st_ref = refs[5 + ns:7 + ns]
        from_sib = refs[7 + ns:7 + 2 * ns]
        sems = refs[7 + 2 * ns:]

        @pl.when(pl.program_id(0) == 0)
        def _():
            st_ref[...] = jnp.zeros_like(st_ref)
            if ns:
                for cp in _pair_send_copies(parts, from_sib, sems):
                    cp.start()

        if which == "in":
            dh = lax.dot_general(dy_ref[...], w_ref[...], (((1,), (1,)), ((), ())), preferred_element_type=F32)
        else:
            dh = jnp.zeros((tm, D), F32)
            for j in range(N_CHIPS):
                dh = dh + lax.dot_general(dy_ref[:, j * D:(j + 1) * D], w_ref[j], (((1,), (1,)), ((), ())),
                                          preferred_element_type=F32)
        ng, sc = mod_ref[rows[0]:rows[0] + 1, :], mod_ref[rows[1]:rows[1] + 1, :]
        dx, dsh, dsc, dng = _norm_mod_bwd(dh, x_ref[...], ng, sc)
        dx_ref[...] = dr_ref[...] + dx
        st_ref[0:1, :] += dsh
        st_ref[1:2, :] += dsc
        st_ref[2:3, :] += dng

        if ns:
            @pl.when(pl.program_id(0) == nsteps - 1)
            def _():
                for cp in _pair_send_copies(parts, from_sib, sems):
                    cp.wait()

    wspec = pl.BlockSpec((D, NW), lambda i: (0, 0)) if which == "in" else _wspec4(D, D, l)
    return pl.pallas_call(
        body, name=f"bwd_norm_{which}_{l}",
        out_shape=[jax.ShapeDtypeStruct((t, D), F32), jax.ShapeDtypeStruct((8, D), F32)] + _pair_send_shapes(send),
        grid=(nsteps,),
        in_specs=[pl.BlockSpec((tm, width), lambda i: (i, 0)), wspec, pl.BlockSpec((tm, D), lambda i: (i, 0)),
                  pl.BlockSpec((tm, D), lambda i: (i, 0)), pl.BlockSpec((None, 8, D), lambda i: (l, 0, 0))]
        + _hbm_specs(ns),
        out_specs=[pl.BlockSpec((tm, D), lambda i: (i, 0)), pl.BlockSpec((8, D), lambda i: (0, 0))] + _hbm_specs(ns),
        scratch_shapes=_pair_send_sems(ns) if ns else [],
        compiler_params=_cparams(("arbitrary",)),
    )(dy, w, x, dres, modv, *send)


def _grad_weight(lhs, rhs, modv, l, which, w_gate=None):
    t = lhs.shape[0]
    tm = min(t, 2048)
    nt = t // tm
    gated = which in ("out", "ff2")
    if which == "in":
        nj, lw, rw, orows, ocols = 5, D, NW // 5, D, NW // 5
    elif which == "ff1":
        nj, lw, rw, orows, ocols = N_CHIPS, D, D, D, D
    elif which == "out":
        nj, lw, rw, orows, ocols = N_CHIPS, D // N_CHIPS, D, D // N_CHIPS, D
    else:
        nj, lw, rw, orows, ocols = N_CHIPS, D, D, D, D
    gate_row = 2 if which == "out" else 5

    def body(*refs):
        if gated:
            l_ref, r_ref, mod_ref, wg_ref, o_ref, dg_ref, acc = refs
        else:
            l_ref, r_ref, mod_ref, o_ref, acc = refs
        j, k = pl.program_id(0), pl.program_id(1)

        @pl.when(k == 0)
        def _():
            acc[...] = jnp.zeros_like(acc)

        if which == "ff2":
            lv = l_ref[...].astype(F32)
            lv = lv * lv
        else:
            lv = l_ref[...]
        acc[...] += _dot_tn(lv, r_ref[...])

        if gated:
            @pl.when(jnp.logical_and(j == 0, k == 0))
            def _():
                dg_ref[...] = jnp.zeros_like(dg_ref)

        @pl.when(k == nt - 1)
        def _():
            raw = acc[...]
            if gated:
                o_ref[...] = (raw * mod_ref[gate_row:gate_row + 1, :]).astype(o_ref.dtype)
                dg_ref[0:1, :] += jnp.sum(raw * wg_ref[...].astype(F32), axis=0, keepdims=True)
            else:
                o_ref[...] = raw.astype(o_ref.dtype)

    if which in ("in", "ff1"):
        lspec = pl.BlockSpec((tm, lw), lambda j, k: (k, 0))
        rspec = pl.BlockSpec((tm, rw), lambda j, k: (k, j))
    else:
        lspec = pl.BlockSpec((tm, lw), lambda j, k: (k, j))
        rspec = pl.BlockSpec((tm, rw), lambda j, k: (k, 0))
    mspec = pl.BlockSpec((None, 8, D), lambda j, k: (l, 0, 0))
    if which == "in":
        ospec = pl.BlockSpec((orows, ocols), lambda j, k: (0, j))
        out_shape = [jax.ShapeDtypeStruct((D, NW), BF16)]
    else:
        ospec = pl.BlockSpec((None, orows, ocols), lambda j, k: (j, 0, 0))
        out_shape = [jax.ShapeDtypeStruct((N_CHIPS, orows, ocols), BF16)]
    in_specs = [lspec, rspec, mspec]
    args = [lhs, rhs, modv]
    out_specs = [ospec]
    if gated:
        in_specs.append(pl.BlockSpec((None, orows, ocols), lambda j, k: (j, 0, 0)))
        args.append(w_gate)
        out_specs.append(pl.BlockSpec((8, D), lambda j, k: (0, 0)))
        out_shape.append(jax.ShapeDtypeStruct((8, D), F32))
    res = pl.pallas_call(
        body, name=f"grad_w_{which}_{l}", out_shape=out_shape, grid=(nj, nt), in_specs=in_specs, out_specs=out_specs,
        scratch_shapes=[pltpu.VMEM((orows, ocols), F32)], compiler_params=_cparams(("arbitrary", "arbitrary")),
    )(*args)
    return (res[0], res[1]) if gated else (res[0], None)


def _tri_masks():
    rows, cols = _iota2((BLK, BLK), 0), _iota2((BLK, BLK), 1)
    return rows >= cols, rows > cols


def _sgu_forward(p_ref, lnp_ref, sguw_ref, sgub_ref):
    incl, _ = _tri_masks()
    ug = _gelu(p_ref[:, 0:512])
    vg = _gelu(p_ref[:, 512:1024])
    mu = jnp.mean(vg, axis=-1, keepdims=True)
    xc = vg - mu
    rstd = lax.rsqrt(jnp.mean(xc * xc, axis=-1, keepdims=True) + LN_EPS)
    vhat = xc * rstd
    vn = vhat * lnp_ref[0:1, :] + lnp_ref[1:2, :]
    bias = sgub_ref[...]
    ys, mixed, wms = [], [], []
    for h in range(HEADS):
        wm = jnp.where(incl, sguw_ref[h], 0.0)
        mx = _dot(wm, vn[:, h * HD:(h + 1) * HD]) + _col(bias, h)
        ys.append(ug[:, h * HD:(h + 1) * HD] * mx)
        mixed.append(mx)
        wms.append(wm)
    return ys, ug, vhat, rstd, vn, mixed, wms


def _conv_forward(xbuf, cw_ref):
    conv = cw_ref[0:1, :] * xbuf[5:5 + BLK, :]
    for j in range(1, 4):
        conv = conv + cw_ref[j:j + 1, :] * xbuf[5 + j:5 + j + BLK, :]
    return conv


def _gates(gt, gv_ref):
    incl, _ = _tri_masks()
    beta = _sigmoid(gt)
    neg_a = -jnp.exp(gv_ref[0:1, :])
    gl = neg_a * _softplus(gt + gv_ref[1:2, :])
    gc = _dotf(jnp.where(incl, 1.0, 0.0).astype(F32), gl)
    return beta, gl, gc, gc.T, neg_a


def _head_chunk(act, beta, gc, gct, h):
    incl, strict = _tri_masks()
    qh = act[:, h * HD:(h + 1) * HD]
    kh = act[:, 512 + h * HD:512 + (h + 1) * HD]
    vh = act[:, 1024 + h * HD:1024 + (h + 1) * HD]
    rq = lax.rsqrt(jnp.sum(qh * qh, axis=-1, keepdims=True) + RMS_EPS)
    rk = lax.rsqrt(jnp.sum(kh * kh, axis=-1, keepdims=True) + RMS_EPS)
    qhat, khat = qh * rq, kh * rk
    qn = qhat * QK_SCALE
    b = _col(beta, h)
    gcol = _col(gc, 4 + h)
    grow = _row(gct, 4 + h)
    dmat = jnp.where(incl, jnp.exp(jnp.where(incl, gcol - grow, 0.0)), 0.0)
    gam = jnp.exp(gcol)
    glast = _row(gcol, BLK - 1)
    e = jnp.exp(glast - gcol)
    kk = _d3_nt(khat, khat)
    return dict(qhat=qhat, khat=khat, qn=qn, vh=vh, rq=rq, rk=rk, b=b, dmat=dmat, gam=gam, glast=glast, e=e, kk=kk,
                strict=strict, incl=incl)


def _mixer_forward(p, lnp, sgu_w, sgu_bt, cw, gv, l, gather=()):
    t = p.shape[0]
    nb = t // BLK
    ng = len(gather)

    def body(*refs):
        p_ref, lnp_ref, sguw_ref, sgub_ref, cw_ref, gv_ref = refs[:6]
        mix_ref, s_out, t_out, u_out, w_out, o_out, tail_out = refs[6 + ng:13 + ng]
        gbufs = refs[13 + ng:13 + 2 * ng]
        s_scr, xbuf = refs[13 + 2 * ng:15 + 2 * ng]
        gsems = refs[15 + 2 * ng:]

        @pl.when(pl.program_id(0) == 0)
        def _():
            s_scr[...] = jnp.zeros_like(s_scr)
            xbuf[0:8, :] = jnp.zeros((8, 1536), F32)
            if ng:
                _gather_start(gbufs, gsems)

        ys = _sgu_forward(p_ref, lnp_ref, sguw_ref, sgub_ref)[0]
        for h in range(HEADS):
            mix_ref[:, h * HD:(h + 1) * HD] = ys[h].astype(BF16)

        tail_out[...] = xbuf[0:8, :]
        xbuf[8:8 + BLK, :] = p_ref[:, 1024:2560]
        act = _silu(_conv_forward(xbuf, cw_ref))
        xbuf[0:8, :] = xbuf[BLK:BLK + 8, :]
        beta, _, gc, gct, _ = _gates(p_ref[:, GATE0:NW], gv_ref)
        chunks = [_head_chunk(act, beta, gc, gct, h) for h in range(HEADS)]
        for h, hc in enumerate(chunks):
            t_out[h] = jnp.where(hc["strict"], hc["b"] * hc["kk"] * hc["dmat"], 0.0)
        t_out[...] = _tri_inverse(t_out[...])
        for h, hc in enumerate(chunks):
            tm = t_out[h]
            u = _d3(tm, hc["b"] * hc["vh"])
            w = _d3(tm, (hc["b"] * hc["gam"]) * hc["khat"])
            qkm = _dot_nt(hc["qn"], hc["khat"]) * hc["dmat"]
            s = s_scr[h]
            wn = u - _dot(w, s)
            o = _dot(hc["qn"] * hc["gam"], s) + _dot(qkm, wn)
            s_out[h] = s
            s_scr[h] = jnp.exp(hc["glast"]) * s + _dot_tn(hc["khat"] * hc["e"], wn)
            sl = slice(h * HD, (h + 1) * HD)
            u_out[:, sl] = u
            w_out[:, sl] = w
            o_out[:, sl] = o
            on = o * lax.rsqrt(jnp.mean(o * o, axis=-1, keepdims=True) + RMS_EPS) * gv_ref[2:3, :]
            mix_ref[:, 512 + h * HD:512 + (h + 1) * HD] = (on * _silu(p_ref[:, 2560 + h * HD:2560 + (h + 1) * HD])).astype(BF16)

        if ng:
            @pl.when(pl.program_id(0) == nb - 1 - min(3, nb - 1))
            def _():
                _gather_forward(gbufs, gsems)

            @pl.when(pl.program_id(0) == nb - 1)
            def _():
                _gather_finish(gbufs, gsems)

    tok = lambda w: pl.BlockSpec((BLK, w), lambda i: (i, 0))
    st = pl.BlockSpec((None, HEADS, HD, HD), lambda i: (i, 0, 0, 0))
    return pl.pallas_call(
        body, name=f"mixer_fwd_{l}", grid=(nb,),
        out_shape=[jax.ShapeDtypeStruct((t, D), BF16), jax.ShapeDtypeStruct((nb, HEADS, HD, HD), F32),
                   jax.ShapeDtypeStruct((nb, HEADS, HD, HD), F32), jax.ShapeDtypeStruct((t, 512), F32),
                   jax.ShapeDtypeStruct((t, 512), F32), jax.ShapeDtypeStruct((t, 512), F32),
                   jax.ShapeDtypeStruct((nb, 8, 1536), F32)]
        + [jax.ShapeDtypeStruct(b.shape, b.dtype) for b in gather],
        in_specs=[tok(NW), pl.BlockSpec((None, 8, 512), lambda i: (l, 0, 0)),
                  pl.BlockSpec((None, HEADS, HD, HD), lambda i: (l, 0, 0, 0)),
                  pl.BlockSpec((None, HD, HD), lambda i: (l, 0, 0)), pl.BlockSpec((None, 8, 1536), lambda i: (l, 0, 0)),
                  pl.BlockSpec((None, 8, HD), lambda i: (l, 0, 0))] + _hbm_specs(ng),
        out_specs=[tok(D), st, st, tok(512), tok(512), tok(512), pl.BlockSpec((None, 8, 1536), lambda i: (i, 0, 0))]
        + _hbm_specs(ng),
        input_output_aliases={6 + i: 7 + i for i in range(ng)},
        scratch_shapes=[pltpu.VMEM((HEADS, HD, HD), F32), pltpu.VMEM((BLK + 8, 1536), F32)]
        + (_gather_sems(ng) if ng else []),
        compiler_params=_cparams(("arbitrary",)),
    )(p, lnp, sgu_w, sgu_bt, cw, gv, *gather)


def _mixer_backward(p, dmix, saved, lnp, sgu_w, sgu_bt, cw, gv, l, exchange=()):
    t = p.shape[0]
    nb = t // BLK
    s_sv, t_sv, u_sv, w_sv, o_sv, tail_sv = saved
    ne = len(exchange)

    def body(*refs):
        (p_ref, dmix_ref, s_ref, t_ref, u_ref, w_ref, o_ref, tail_ref, lnp_ref, sguw_ref, sgub_ref, cw_ref,
         gv_ref) = refs[:13]
        pairs = refs[13:13 + ne]
        dp_ref, dlnp_ref, dsguw_ref, dsgub_ref, dcw_ref, dgv_ref = refs[13 + ne:19 + ne]
        recvs = refs[19 + ne:19 + 2 * ne]
        ds_scr, xbuf, dcbuf = refs[19 + 2 * ne:22 + 2 * ne]
        esems = refs[22 + 2 * ne:]

        @pl.when(pl.program_id(0) == 0)
        def _():
            if ne:
                for cp in _exchange_copies(pairs, recvs, esems):
                    cp.start()
            ds_scr[...] = jnp.zeros_like(ds_scr)
            dcbuf[BLK:BLK + 8, :] = jnp.zeros((8, 1536), F32)
            dlnp_ref[...] = jnp.zeros_like(dlnp_ref)
            dsguw_ref[...] = jnp.zeros_like(dsguw_ref)
            dsgub_ref[...] = jnp.zeros_like(dsgub_ref)
            dcw_ref[...] = jnp.zeros_like(dcw_ref)
            dgv_ref[...] = jnp.zeros_like(dgv_ref)

        incl, strict = _tri_masks()
        _, ug, vhat, rstd, vn, mixed, wms = _sgu_forward(p_ref, lnp_ref, sguw_ref, sgub_ref)
        dvn_parts, dug_parts = [], []
        dbias = jnp.zeros((BLK, HD), F32)
        for h in range(HEADS):
            sl = slice(h * HD, (h + 1) * HD)
            dy = dmix_ref[:, sl].astype(F32)
            dmx = dy * ug[:, sl]
            dug_parts.append(dy * mixed[h])
            dsguw_ref[h] += jnp.where(incl, _dot_nt(dmx, vn[:, sl]), 0.0)
            dbias = dbias + _put_col(jnp.sum(dmx, axis=1, keepdims=True), h)
            dvn_parts.append(_dot_tn(wms[h], dmx))
        dsgub_ref[...] += dbias
        dvn = jnp.concatenate(dvn_parts, axis=1)
        dug = jnp.concatenate(dug_parts, axis=1)
        dlnp_ref[0:1, :] += jnp.sum(dvn * vhat, axis=0, keepdims=True)
        dlnp_ref[1:2, :] += jnp.sum(dvn, axis=0, keepdims=True)
        dvhat = dvn * lnp_ref[0:1, :]
        dvg = rstd * (dvhat - jnp.mean(dvhat, axis=-1, keepdims=True)
                      - vhat * jnp.mean(dvhat * vhat, axis=-1, keepdims=True))
        dp_ref[:, 0:512] = (dug * _gelu_grad(p_ref[:, 0:512])).astype(BF16)
        dp_ref[:, 512:1024] = (dvg * _gelu_grad(p_ref[:, 512:1024])).astype(BF16)

        xbuf[0:8, :] = tail_ref[...]
        xbuf[8:8 + BLK, :] = p_ref[:, 1024:2560]
        conv = _conv_forward(xbuf, cw_ref)
        act = _silu(conv)
        gt = p_ref[:, GATE0:NW]
        beta, gl, gc, gct, neg_a = _gates(gt, gv_ref)
        gng = gv_ref[2:3, :]
        dbeta_t = jnp.zeros((BLK, HD), F32)
        dgc_t = jnp.zeros((BLK, HD), F32)
        dgng = jnp.zeros((1, HD), F32)
        for h in range(HEADS):
            sl = slice(h * HD, (h + 1) * HD)
            hc = _head_chunk(act, beta, gc, gct, h)
            b, gam, e, dmat, kk = hc["b"], hc["gam"], hc["e"], hc["dmat"], hc["kk"]
            qn, khat, vh = hc["qn"], hc["khat"], hc["vh"]
            gamlast = jnp.exp(hc["glast"])
            s, tm, u, w, o = s_ref[h], t_ref[h], u_ref[:, sl], w_ref[:, sl], o_ref[:, sl]
            ds_next = ds_scr[h]
            z = p_ref[:, 2560 + h * HD:2560 + (h + 1) * HD]
            dy = dmix_ref[:, 512 + h * HD:512 + (h + 1) * HD].astype(F32)
            ro = lax.rsqrt(jnp.mean(o * o, axis=-1, keepdims=True) + RMS_EPS)
            ohat = o * ro
            dp_ref[:, 2560 + h * HD:2560 + (h + 1) * HD] = (dy * ohat * gng * _silu_grad(z)).astype(BF16)
            don = dy * _silu(z)
            dgng = dgng + jnp.sum(don * ohat, axis=0, keepdims=True)
            dohat = don * gng
            do = ro * (dohat - ohat * jnp.mean(dohat * ohat, axis=-1, keepdims=True))
            qk_raw = _dot_nt(qn, khat)
            qkm = qk_raw * dmat
            qd, kd = qn * gam, khat * e
            wn = u - _dot(w, s)
            dwn = _dot_tn(qkm, do) + _dot(kd, ds_next)
            dqd = _dot_nt(do, s)
            dqkm = jnp.where(incl, _dot_nt(do, wn), 0.0)
            ds_scr[h] = _dot_tn(qd, do) + gamlast * ds_next - _dot_tn(w, dwn)
            dgamlast = jnp.sum(jnp.sum(ds_next * s, axis=1, keepdims=True), axis=0, keepdims=True)
            dkd = _dot_nt(wn, ds_next)
            dw = -_dot_nt(dwn, s)
            db1 = _d3_tn(tm, dwn)
            db2 = _d3_tn(tm, dw)
            dm = jnp.where(strict, -(_dot_nt(db1, u) + _dot_nt(db2, w)), 0.0)
            dbeta = (jnp.sum(dm * kk * dmat, axis=1, keepdims=True) + jnp.sum(db1 * vh, axis=1, keepdims=True)
                     + gam * jnp.sum(db2 * khat, axis=1, keepdims=True))
            dkkm = dm * b * dmat
            ddm = dm * b * kk + dqkm * qk_raw
            dgam = b * jnp.sum(db2 * khat, axis=1, keepdims=True) + jnp.sum(dqd * qn, axis=1, keepdims=True)
            g_qk = dqkm * dmat
            dqn = _dot(g_qk, khat) + dqd * gam
            dkhat = ((b * gam) * db2 + _dot_tn(g_qk, qn) + _dot(dkkm, khat) + _dot_tn(dkkm, khat) + dkd * e)
            dvh = b * db1
            rkd = jnp.sum(dkd * kd, axis=1, keepdims=True)
            emat = ddm * dmat
            dgc = (dgam * gam - rkd + jnp.sum(emat, axis=1, keepdims=True)
                   - jnp.sum(emat.T, axis=1, keepdims=True))
            last = _iota2((BLK, 1), 0) == BLK - 1
            dgc = dgc + jnp.where(last, jnp.sum(rkd, axis=0, keepdims=True) + dgamlast * gamlast, 0.0)
            dgc_t = dgc_t + _put_col(dgc, 4 + h)
            dbeta_t = dbeta_t + _put_col(dbeta, h)
            dqhat = dqn * QK_SCALE
            dq = hc["rq"] * (dqhat - hc["qhat"] * jnp.sum(dqhat * hc["qhat"], axis=-1, keepdims=True))
            dk = hc["rk"] * (dkhat - khat * jnp.sum(dkhat * khat, axis=-1, keepdims=True))
            dcbuf[0:BLK, h * HD:(h + 1) * HD] = dq
            dcbuf[0:BLK, 512 + h * HD:512 + (h + 1) * HD] = dk
            dcbuf[0:BLK, 1024 + h * HD:1024 + (h + 1) * HD] = dvh
        dgv_ref[2:3, :] += dgng
        dgl = _dotf_tn(jnp.where(incl, 1.0, 0.0).astype(F32), dgc_t)
        sig_a = _sigmoid(gt + gv_ref[1:2, :])
        d_araw = dgl * neg_a * sig_a
        dgv_ref[0:1, :] += jnp.sum(dgl * gl, axis=0, keepdims=True)
        dgv_ref[1:2, :] += jnp.sum(d_araw, axis=0, keepdims=True)
        dp_ref[:, GATE0:NW] = (dbeta_t * beta * (1.0 - beta) + d_araw).astype(BF16)
        dcbuf[0:BLK, :] = dcbuf[0:BLK, :] * _silu_grad(conv)
        dqkv = cw_ref[0:1, :] * dcbuf[3:3 + BLK, :]
        dcw_ref[0:1, :] += jnp.sum(dcbuf[0:BLK, :] * xbuf[5:5 + BLK, :], axis=0, keepdims=True)
        for j in range(1, 4):
            dqkv = dqkv + cw_ref[j:j + 1, :] * dcbuf[3 - j:3 - j + BLK, :]
            dcw_ref[j:j + 1, :] += jnp.sum(dcbuf[0:BLK, :] * xbuf[5 + j:5 + j + BLK, :], axis=0, keepdims=True)
        dp_ref[:, 1024:2560] = dqkv.astype(BF16)
        dcbuf[BLK:BLK + 8, :] = dcbuf[0:8, :]

        if ne:
            @pl.when(pl.program_id(0) == nb - 1)
            def _():
                for cp in _exchange_copies(pairs, recvs, esems):
                    cp.wait()

    rev = lambda w: pl.BlockSpec((BLK, w), lambda i: (nb - 1 - i, 0))
    st = pl.BlockSpec((None, HEADS, HD, HD), lambda i: (nb - 1 - i, 0, 0, 0))
    fix = lambda *shape: pl.BlockSpec((None,) + shape, lambda i: (l,) + (0,) * len(shape))
    acc = lambda *shape: pl.BlockSpec(shape, lambda i: (0,) * len(shape))
    return pl.pallas_call(
        body, name=f"mixer_bwd_{l}", grid=(nb,),
        out_shape=[jax.ShapeDtypeStruct((t, NW), BF16), jax.ShapeDtypeStruct((8, 512), F32),
                   jax.ShapeDtypeStruct((HEADS, HD, HD), F32), jax.ShapeDtypeStruct((HD, HD), F32),
                   jax.ShapeDtypeStruct((8, 1536), F32), jax.ShapeDtypeStruct((8, HD), F32)]
        + _exchange_shapes(exchange),
        in_specs=[rev(NW), rev(D), st, st, rev(512), rev(512), rev(512),
                  pl.BlockSpec((None, 8, 1536), lambda i: (nb - 1 - i, 0, 0)),
                  fix(8, 512), fix(HEADS, HD, HD), fix(HD, HD), fix(8, 1536), fix(8, HD)] + _hbm_specs(ne),
        out_specs=[rev(NW), acc(8, 512), acc(HEADS, HD, HD), acc(HD, HD), acc(8, 1536), acc(8, HD)]
        + _hbm_specs(ne),
        scratch_shapes=[pltpu.VMEM((HEADS, HD, HD), F32), pltpu.VMEM((BLK + 8, 1536), F32),
                        pltpu.VMEM((BLK + 8, 1536), F32)] + (_exchange_sems(ne) if ne else []),
        compiler_params=_cparams(("arbitrary",)),
    )(p, dmix, s_sv, t_sv, u_sv, w_sv, o_sv, tail_sv, lnp, sgu_w, sgu_bt, cw, gv, *exchange)


_SMALL = (("b_ada", 24), ("norm1_g", 4), ("norm2_g", 4), ("final_g", 1), ("sgu_ln_g", 2), ("sgu_ln_b", 2),
          ("sgu_w", 256), ("sgu_b", 2), ("conv_w", 24), ("a_log", 1), ("dt_bias", 1), ("gdn_norm_g", 1))
_SMALL_ROWS = sum(n for _, n in _SMALL)
_SMALL_PAD = 328
_DMOD_ROWS = 24


def _pack_rows(parts):
    rows = []
    for (name, n), a in zip(_SMALL, parts):
        flat = a.reshape(-1).astype(F32)
        rows.append(jnp.pad(flat, (0, n * D - flat.shape[0])).reshape(n, D))
    rows.append(jnp.zeros((_SMALL_PAD - _SMALL_ROWS, D), F32))
    return jnp.concatenate(rows, axis=0)


def _unpack_rows(buf, shapes):
    out, r0 = {}, 0
    for name, n in _SMALL:
        size = math.prod(shapes[name])
        out[name] = buf[r0:r0 + n].reshape(-1)[:size].reshape(shapes[name])
        r0 += n
    return out


def _pair_combine(own, sib):
    n = own.shape[0] - _DMOD_ROWS

    def body(a_ref, b_ref, o_ref):
        first = lax.axis_index("c") == 0
        a, b = a_ref[0:_DMOD_ROWS, :], b_ref[0:_DMOD_ROWS, :]
        o_ref[0:_DMOD_ROWS, :] = jnp.where(first, a, b)
        o_ref[_DMOD_ROWS:2 * _DMOD_ROWS, :] = jnp.where(first, b, a)
        o_ref[2 * _DMOD_ROWS:, :] = a_ref[_DMOD_ROWS:, :] + b_ref[_DMOD_ROWS:, :]

    return pl.pallas_call(
        body, name="small_pair_combine", out_shape=jax.ShapeDtypeStruct((2 * _DMOD_ROWS + n, D), F32),
        compiler_params=pltpu.CompilerParams(vmem_limit_bytes=VMEM_LIMIT),
    )(own, sib)


def _small_finalize(gathered, w, m, v):
    def body(g_ref, w_ref, m_ref, v_ref, go_ref, d_ref, nm_ref, nv_ref):
        sm = g_ref[0, 2 * _DMOD_ROWS:, :] + g_ref[1, 2 * _DMOD_ROWS:, :]
        sm = sm + g_ref[2, 2 * _DMOD_ROWS:, :]
        sm = sm + g_ref[3, 2 * _DMOD_ROWS:, :]
        bsum = jnp.zeros((_DMOD_ROWS, D), F32)
        for j in range(N_CHIPS):
            bsum = bsum + g_ref[j, 0:_DMOD_ROWS, :]
            bsum = bsum + g_ref[j, _DMOD_ROWS:2 * _DMOD_ROWS, :]
        go_ref[0:_DMOD_ROWS, :] = bsum
        go_ref[_DMOD_ROWS:, :] = sm[_DMOD_ROWS:, :]
        d_ref[...], nm_ref[...], nv_ref[...] = _adam_math(w_ref[...], go_ref[...], m_ref[...], v_ref[...])

    return pl.pallas_call(
        body, name="small_finalize", out_shape=[jax.ShapeDtypeStruct(w.shape, F32)] * 4,
        compiler_params=pltpu.CompilerParams(vmem_limit_bytes=VMEM_LIMIT),
    )(gathered, w, m, v)


def kernel(x, c, w_ada, b_ada, norm1_g, w_in, sgu_ln_g, sgu_ln_b, sgu_w, sgu_b, conv_w, a_log, dt_bias, gdn_norm_g, w_out, norm2_g, w_ff1, w_ff2, final_g, loss_target, m_w_ada, m_b_ada, m_norm1_g, m_w_in, m_sgu_ln_g, m_sgu_ln_b, m_sgu_w, m_sgu_b, m_conv_w, m_a_log, m_dt_bias, m_gdn_norm_g, m_w_out, m_norm2_g, m_w_ff1, m_w_ff2, m_final_g, v_w_ada, v_b_ada, v_norm1_g, v_w_in, v_sgu_ln_g, v_sgu_ln_b, v_sgu_w, v_sgu_b, v_conv_w, v_a_log, v_dt_bias, v_gdn_norm_g, v_w_out, v_norm2_g, v_w_ff1, v_w_ff2, v_final_g):
    xi, yi, ci = lax.axis_index("x"), lax.axis_index("y"), lax.axis_index("c")
    chip = 2 * xi + yi
    dev = 2 * chip + ci
    t = x.shape[1]
    x0 = x.reshape(t, D)
    target = loss_target.reshape(t, D)

    c_sib = _pair_exchange(c, "c_pair")
    c_pair = jnp.where(ci == 0, jnp.concatenate([c, c_sib], 0), jnp.concatenate([c_sib, c], 0))
    c_all = _chip_allgather(c_pair, "c_chips").reshape(8, D)
    ada_cols = w_ada.shape[2]
    b_cols = lax.dynamic_slice_in_dim(b_ada, chip * ada_cols, ada_cols, axis=1)
    mod_part = _ada_forward(c_all, w_ada, b_cols)
    conv_cols = conv_w.shape[2]
    packed = jnp.concatenate([mod_part.reshape(DEPTH * 8, ada_cols), conv_w.reshape(DEPTH, 4 * conv_cols)], axis=0)
    packed = _chip_allgather(packed, "mod_chips")
    mod_all = packed[:, :DEPTH * 8].reshape(N_CHIPS, DEPTH, 8, ada_cols)
    mod_mine = lax.dynamic_index_in_dim(mod_all, dev, axis=2, keepdims=False)
    mod = mod_mine.transpose(1, 0, 2).reshape(DEPTH, 6, D)
    modv = jnp.concatenate([mod, norm1_g[:, None, :], norm2_g[:, None, :]], axis=1)
    conv_full = packed[:, DEPTH * 8:].reshape(N_CHIPS, DEPTH, 4, conv_cols).transpose(1, 2, 0, 3).reshape(DEPTH, 4, 1536)

    place = jnp.stack([chip, ci]).astype(jnp.int32)
    wbufs = [[_cast_into_slot(w, l, place) for w in (w_in, w_out, w_ff1, w_ff2)] for l in range(DEPTH)]
    wbufs[0][:1] = _weights_allgather(wbufs[0][:1], 0)

    def full_w_in(g):
        return jnp.pad(g.transpose(1, 0, 2).reshape(D, IN_W), ((0, 0), (0, NW - IN_W)))

    lnp = jnp.pad(jnp.stack([sgu_ln_g, sgu_ln_b], axis=1), ((0, 0), (0, 6), (0, 0)))
    sgu_bt = jnp.pad(sgu_b.transpose(0, 2, 1), ((0, 0), (0, 0), (0, HD - HEADS)))
    cw = jnp.pad(conv_full, ((0, 0), (0, 4), (0, 0)))
    lane_pad = lambda a: jnp.pad(a, ((0, 0), (4, HD - 8)))
    gv = jnp.pad(jnp.stack([lane_pad(a_log), lane_pad(dt_bias), gdn_norm_g], axis=1), ((0, 0), (0, 5), (0, 0)))

    acts = []
    xl = x0
    for l in range(DEPTH):
        win = full_w_in(wbufs[l][0])
        p, h1 = _fwd_in(xl, modv, win, l)
        nxt = wbufs[l][1:] + (wbufs[l + 1][:1] if l + 1 < DEPTH else [])
        mix, *rest = _mixer_forward(p, lnp, sgu_w, sgu_bt, cw, gv, l, gather=nxt)
        saved = rest[:6]
        wbufs[l][1:] = rest[6:9]
        if l + 1 < DEPTH:
            wbufs[l + 1][:1] = rest[9:]
        g_in, g_out, g_ff1, g_ff2 = wbufs[l]
        x1 = _fwd_out(xl, mix, modv, g_out, l)
        r, h2 = _fwd_ff1(x1, modv, g_ff1, l)
        x2 = _fwd_ff2(x1, r, modv, g_ff2, l)
        acts.append((xl, p, mix, saved, x1, r, h1, h2, win))
        xl = x2

    dx, head_stats = _loss_head(xl, target, final_g)
    loss = lax.psum(jnp.sum(head_stats[1, 0:1]), ("x", "y", "c"))
    d_final_g = head_stats[0]
    names = ("in", "out", "ff1", "ff2")
    grads_buf = [None] * len(names)

    def pair_sums(partials, from_sib, kinds, lay):
        return [(lay, n, _pair_sum(g, ga, place, f"pair_sum_{n}_{lay}")) for g, ga, n in zip(partials, from_sib, kinds)]

    def reduce_into_buffers(items, recv):
        for (lay, n, pair), rc in zip(items, recv):
            i = names.index(n)
            grads_buf[i] = _chip_sum(pair, rc, grads_buf[i], lay, place, f"chip_sum_{n}_{lay}")

    pending = []

    dmod, small = [None] * DEPTH, [None] * DEPTH
    for l in reversed(range(DEPTH)):
        xl, p, mix, saved, x1, r, h1, h2, win = acts[l]
        g_in, g_out, g_ff1, g_ff2 = wbufs[l]
        df = _bwd_ff2(dx, r, modv, g_ff2, l)
        gw_ff2, dg2 = _grad_weight(r, dx, modv, l, "ff2", g_ff2)
        gw_ff1, _ = _grad_weight(h2, df, modv, l, "ff1")
        dx1, st2, *sib_ff = _bwd_norm(df, g_ff1, x1, dx, modv, l, "ff1", send=[gw_ff1, gw_ff2])
        dmix = _bwd_out(dx1, modv, g_out, l)
        gw_out, dg1 = _grad_weight(mix, dx1, modv, l, "out", g_out)
        sib_out = _grads_pair_send([gw_out], f"out_{l}")
        pending = pending + pair_sums([gw_out, gw_ff1, gw_ff2], list(sib_out) + sib_ff, names[1:], l)
        dp, dlnp, dsguw, dsgub, dcw, dgv, *recv = _mixer_backward(p, dmix, saved, lnp, sgu_w, sgu_bt, cw, gv, l,
                                                                  exchange=[item[2] for item in pending])
        reduce_into_buffers(pending, recv)
        gw_in, _ = _grad_weight(h1, dp, modv, l, "in")
        gw_in_c = gw_in[:, :IN_W].reshape(D, N_CHIPS, IN_W // N_CHIPS).transpose(1, 0, 2)
        dx, st1, *sib_in = _bwd_norm(dp, win, xl, dx1, modv, l, "in", send=[gw_in_c])
        pending = pair_sums([gw_in_c], sib_in, names[:1], l)
        dmod[l] = jnp.stack([st1[0], st1[1], dg1[0], st2[0], st2[1], dg2[0]], axis=0)
        small[l] = dict(norm1_g=st1[2], norm2_g=st2[2], sgu_ln_g=dlnp[0], sgu_ln_b=dlnp[1], sgu_w=dsguw,
                        sgu_b=dsgub[:, :HEADS].T, conv_w=dcw[:4], a_log=dgv[0, 4:8], dt_bias=dgv[1, 4:8],
                        gdn_norm_g=dgv[2])
    grad_x = dx.reshape(1, t, D)

    stack = lambda k: jnp.stack([small[l][k] for l in range(DEPTH)], axis=0)
    small_grads = [jnp.zeros((DEPTH, 6 * D), F32), stack("norm1_g"), stack("norm2_g"), d_final_g, stack("sgu_ln_g"),
                   stack("sgu_ln_b"), stack("sgu_w"), stack("sgu_b"), stack("conv_w"), stack("a_log"),
                   stack("dt_bias"), stack("gdn_norm_g")]
    own = jnp.concatenate([jnp.stack(dmod, axis=0).reshape(_DMOD_ROWS, D), _pack_rows(small_grads)], axis=0)
    sib = _pair_exchange(own, "small_pair")
    recv, gathered = _grads_chip_exchange([item[2] for item in pending], 0, _pair_combine(own, sib))
    reduce_into_buffers(pending, recv)
    small_shapes = dict(b_ada=b_ada.shape, norm1_g=norm1_g.shape, norm2_g=norm2_g.shape, final_g=final_g.shape,
                        sgu_ln_g=sgu_ln_g.shape, sgu_ln_b=sgu_ln_b.shape, sgu_w=sgu_w.shape, sgu_b=sgu_b.shape,
                        conv_w=(DEPTH, 4, 1536), a_log=a_log.shape, dt_bias=dt_bias.shape,
                        gdn_norm_g=gdn_norm_g.shape)

    def full_conv(a):
        return lax.dynamic_update_slice_in_dim(jnp.zeros((DEPTH, 4, 1536), F32), a, chip * conv_cols, axis=2)

    def pack_state(b_, n1, n2, fg, lg, lb, sw, sb, cv, al, db, gn):
        return _pack_rows([b_, n1, n2, fg, lg, lb, sw, sb, full_conv(cv), al, db, gn])

    w_small = pack_state(b_ada, norm1_g, norm2_g, final_g, sgu_ln_g, sgu_ln_b, sgu_w, sgu_b, conv_w, a_log, dt_bias,
                         gdn_norm_g)
    m_small = pack_state(m_b_ada, m_norm1_g, m_norm2_g, m_final_g, m_sgu_ln_g, m_sgu_ln_b, m_sgu_w, m_sgu_b, m_conv_w,
                         m_a_log, m_dt_bias, m_gdn_norm_g)
    v_small = pack_state(v_b_ada, v_norm1_g, v_norm2_g, v_final_g, v_sgu_ln_g, v_sgu_ln_b, v_sgu_w, v_sgu_b, v_conv_w,
                         v_a_log, v_dt_bias, v_gdn_norm_g)
    small_out = _small_finalize(gathered, w_small, m_small, v_small)
    sg, sd, sm, sv = [_unpack_rows(a, small_shapes) for a in small_out]
    for dct in (sg, sd, sm, sv):
        dct["conv_w"] = lax.dynamic_slice_in_dim(dct["conv_w"], chip * conv_cols, conv_cols, axis=2)

    dmod_all = gathered[:, :2 * _DMOD_ROWS].reshape(8, DEPTH, 6 * D)
    dmod_cols = lax.dynamic_slice_in_dim(dmod_all, chip * ada_cols, ada_cols, axis=2).transpose(1, 0, 2)
    g_ada, d_ada, nm_ada, nv_ada = _ada_backward_adamw(c_all, dmod_cols, w_ada, m_w_ada, v_w_ada)

    grads = _grads_pair_share(grads_buf)
    big = {}
    for n, g, (w, m, v) in zip(names, grads, ((w_in, m_w_in, v_w_in), (w_out, m_w_out, v_w_out),
                                              (w_ff1, m_w_ff1, v_w_ff1), (w_ff2, m_w_ff2, v_w_ff2))):
        big[n] = (g,) + tuple(_adamw(w, g, m, v, f"adamw_{n}"))

    def outs(k):
        s = (sg, sd, sm, sv)[k]
        return [(g_ada, d_ada, nm_ada, nv_ada)[k], s["b_ada"], s["norm1_g"], big["in"][k], s["sgu_ln_g"],
                s["sgu_ln_b"], s["sgu_w"], s["sgu_b"], s["conv_w"], s["a_log"], s["dt_bias"], s["gdn_norm_g"],
                big["out"][k], s["norm2_g"], big["ff1"][k], big["ff2"][k], s["final_g"]]

    return (loss, grad_x, *outs(0), *outs(1), *outs(2), *outs(3))
```

```python
import functools
import math

import jax
import jax.numpy as jnp
from jax import lax
from jax.experimental import pallas as pl
from jax.experimental.pallas import tpu as pltpu

F32 = jnp.float32
BF16 = jnp.bfloat16

DEPTH = 4
D = 1024
HEADS = 4
HD = 128
BLK = 128
IN_W = 3080
NW = 3200
GATE0 = 3072
DFF = 4096
N_CHIPS = 4
RMS_EPS = 1e-6
LN_EPS = 1e-5
QK_SCALE = HD ** -0.5
LR, B1, B2, ADAM_EPS, WD, STEP = 0.001, 0.9, 0.999, 1e-08, 0.01, 10
VMEM_LIMIT = 56 * 1024 * 1024
MESH = pl.DeviceIdType.MESH
HOPS = ((1, 0), (0, 1), (1, 1))
HI = lax.Precision.HIGHEST


def _dot(a, b):
    return jnp.dot(a.astype(BF16), b.astype(BF16), preferred_element_type=F32)


def _dot_nt(a, b):
    return lax.dot_general(a.astype(BF16), b.astype(BF16), (((1,), (1,)), ((), ())), preferred_element_type=F32)


def _dot_tn(a, b):
    return lax.dot_general(a.astype(BF16), b.astype(BF16), (((0,), (0,)), ((), ())), preferred_element_type=F32)


def _dotf(a, b):
    return jnp.dot(a, b, precision=HI, preferred_element_type=F32)


def _split(a):
    hi = a.astype(BF16)
    return hi, (a - hi.astype(F32)).astype(BF16)


def _dg3(a, b, dims, batch=((), ())):
    ah, al = _split(a)
    bh, bl = _split(b)
    f = lambda x, y: lax.dot_general(x, y, (dims, batch), preferred_element_type=F32)
    return f(ah, bh) + (f(ah, bl) + f(al, bh))


def _bmm3(a, b):
    return _dg3(a, b, ((2,), (1,)), ((0,), (0,)))


def _d3(a, b):
    return _dg3(a, b, ((1,), (0,)))


def _d3_nt(a, b):
    return _dg3(a, b, ((1,), (1,)))


def _d3_tn(a, b):
    return _dg3(a, b, ((0,), (0,)))


def _dotf_tn(a, b):
    return lax.dot_general(a, b, (((0,), (0,)), ((), ())), precision=HI, preferred_element_type=F32)


def _sigmoid(x):
    return 1.0 / (1.0 + jnp.exp(-x))


def _softplus(x):
    return jnp.maximum(x, 0.0) + jnp.log(1.0 + jnp.exp(-jnp.abs(x)))


_G0 = math.sqrt(2.0 / math.pi)
_G1 = 0.044715


def _gelu(x):
    t = jnp.tanh(_G0 * (x + _G1 * x * x * x))
    return 0.5 * x * (1.0 + t)


def _gelu_grad(x):
    t = jnp.tanh(_G0 * (x + _G1 * x * x * x))
    return 0.5 * (1.0 + t) + 0.5 * x * (1.0 - t * t) * (_G0 * (1.0 + 3.0 * _G1 * x * x))


def _silu(x):
    return x * _sigmoid(x)


def _silu_grad(x):
    s = _sigmoid(x)
    return s * (1.0 + x * (1.0 - s))


def _rms_stats(x):
    rstd = lax.rsqrt(jnp.mean(x * x, axis=-1, keepdims=True) + RMS_EPS)
    return x * rstd, rstd


def _norm_mod(x, ng, sc, sh):
    xh, _ = _rms_stats(x)
    return xh * (ng * (1.0 + sc)) + sh


def _norm_mod_bwd(dh, x, ng, sc):
    xh, rstd = _rms_stats(x)
    dsh = jnp.sum(dh, axis=0, keepdims=True)
    dsc = jnp.sum(dh * xh, axis=0, keepdims=True) * ng
    dng = jnp.sum(dh * xh, axis=0, keepdims=True) * (1.0 + sc)
    dxh = dh * (ng * (1.0 + sc))
    dx = rstd * (dxh - xh * jnp.mean(dxh * xh, axis=-1, keepdims=True))
    return dx, dsh, dsc, dng


def _iota2(shape, axis):
    return lax.broadcasted_iota(jnp.int32, shape, axis)


def _col(tile, idx):
    return jnp.sum(jnp.where(_iota2(tile.shape, 1) == idx, tile, 0.0), axis=1, keepdims=True)


def _row(tile, idx):
    return jnp.sum(jnp.where(_iota2(tile.shape, 0) == idx, tile, 0.0), axis=0, keepdims=True)


def _put_col(col, idx, width=HD):
    shape = (col.shape[0], width)
    return jnp.where(_iota2(shape, 1) == idx, jnp.broadcast_to(col, shape), 0.0)


def _tri_inverse(m):
    rows, cols = _iota2(m.shape, m.ndim - 2), _iota2(m.shape, m.ndim - 1)
    mm = _bmm3 if m.ndim == 3 else _d3
    eye = jnp.where(rows == cols, 1.0, 0.0).astype(F32)
    n = jnp.where((rows >> 3) == (cols >> 3), -m, 0.0)
    p = eye + n
    n2 = mm(n, n)
    p = p + mm(n2, p)
    n4 = mm(n2, n2)
    p = p + mm(n4, p)
    for shift in (3, 4, 5, 6):
        same_pair = (rows >> (shift + 1)) == (cols >> (shift + 1))
        below = jnp.logical_and(((rows >> shift) & 1) == 1, ((cols >> shift) & 1) == 0)
        off = jnp.where(jnp.logical_and(same_pair, below), m, 0.0)
        p = p - mm(p, mm(off, p))
    return p


def _cparams(sem=None):
    return pltpu.CompilerParams(dimension_semantics=sem, vmem_limit_bytes=VMEM_LIMIT)


def _my_place():
    return lax.axis_index("x"), lax.axis_index("y"), lax.axis_index("c")


def _hop(xi, yi, hop):
    dx, dy = hop
    return (1 - xi if dx else xi), (1 - yi if dy else yi)


def _pair_exchange(x, name):
    def body(x_ref, o_ref, ssem, rsem):
        xi, yi, ci = _my_place()
        cp = pltpu.make_async_remote_copy(x_ref, o_ref, ssem, rsem, device_id=(xi, yi, 1 - ci), device_id_type=MESH)
        cp.start()
        cp.wait()

    return pl.pallas_call(
        body, name=name, out_shape=jax.ShapeDtypeStruct(x.shape, x.dtype),
        in_specs=[pl.BlockSpec(memory_space=pltpu.VMEM)], out_specs=pl.BlockSpec(memory_space=pltpu.VMEM),
        scratch_shapes=[pltpu.SemaphoreType.DMA, pltpu.SemaphoreType.DMA],
        compiler_params=pltpu.CompilerParams(vmem_limit_bytes=VMEM_LIMIT),
    )(x)


def _allgather_start(x_ref, o_ref, ssems, rsems, lsem):
    xi, yi, ci = _my_place()
    me = 2 * xi + yi
    pltpu.make_async_copy(x_ref, o_ref.at[me], lsem).start()
    for k, hop in enumerate(HOPS):
        tx, ty = _hop(xi, yi, hop)
        pltpu.make_async_remote_copy(x_ref, o_ref.at[me], ssems.at[k], rsems.at[k],
                                     device_id=(tx, ty, ci), device_id_type=MESH).start()


def _allgather_finish(x_ref, o_ref, ssems, rsems, lsem):
    xi, yi, ci = _my_place()
    me = 2 * xi + yi
    for k, hop in enumerate(HOPS):
        tx, ty = _hop(xi, yi, hop)
        cp = pltpu.make_async_remote_copy(x_ref, o_ref.at[2 * tx + ty], ssems.at[k], rsems.at[k],
                                          device_id=(tx, ty, ci), device_id_type=MESH)
        cp.wait_recv()
        cp.wait_send()
    pltpu.make_async_copy(x_ref, o_ref.at[me], lsem).wait()


_ALLGATHER_SEMS = [pltpu.SemaphoreType.DMA((3,)), pltpu.SemaphoreType.DMA((3,)), pltpu.SemaphoreType.DMA]


def _chip_allgather(x, name):
    def body(x_ref, o_ref, ssems, rsems, lsem):
        _allgather_start(x_ref, o_ref, ssems, rsems, lsem)
        _allgather_finish(x_ref, o_ref, ssems, rsems, lsem)

    return pl.pallas_call(
        body, name=name, out_shape=jax.ShapeDtypeStruct((N_CHIPS,) + x.shape, x.dtype),
        in_specs=[pl.BlockSpec(memory_space=pltpu.VMEM)], out_specs=pl.BlockSpec(memory_space=pltpu.VMEM),
        scratch_shapes=_ALLGATHER_SEMS, compiler_params=pltpu.CompilerParams(vmem_limit_bytes=VMEM_LIMIT),
    )(x)


def _hbm_specs(n):
    return [pl.BlockSpec(memory_space=pl.ANY)] * n


def _cast_into_slot(w, l, place):
    _, r, c = w.shape
    tr = _row_tile(r)

    def body(p_ref, w_ref, o_ref):
        o_ref[...] = w_ref[...].astype(BF16)

    return pl.pallas_call(
        body, name=f"cast_slot_{r}x{c}_{l}", out_shape=jax.ShapeDtypeStruct((N_CHIPS, r, c), BF16),
        grid_spec=pltpu.PrefetchScalarGridSpec(
            num_scalar_prefetch=1, grid=(r // tr,),
            in_specs=[pl.BlockSpec((None, tr, c), lambda k, pr: (l, k, 0))],
            out_specs=pl.BlockSpec((None, tr, c), lambda k, pr: (pr[0], k, 0))),
        compiler_params=_cparams(("parallel",)),
    )(place, w)


def _halves(ref, ci):
    half = ref.shape[-2] // 2
    return pl.ds(ci * half, half), pl.ds((1 - ci) * half, half)


def _gather_copies(bufs, sems):
    s_ici, r_ici, s_d2d, r_d2d = sems
    xi, yi, ci = _my_place()
    me = 2 * xi + yi
    ici_send, ici_recv, d2d_send, d2d_recv = [], [], [], []
    for i, buf in enumerate(bufs):
        mine, sibs = _halves(buf, ci)
        for k, hop in enumerate(HOPS):
            tx, ty = _hop(xi, yi, hop)
            src = 2 * tx + ty
            ici_send.append(pltpu.make_async_remote_copy(buf.at[me, mine], buf.at[me, mine], s_ici.at[i, k],
                                                         r_ici.at[i, k], device_id=(tx, ty, ci), device_id_type=MESH))
            ici_recv.append(pltpu.make_async_remote_copy(buf.at[src, mine], buf.at[src, mine], s_ici.at[i, k],
                                                         r_ici.at[i, k], device_id=(tx, ty, ci), device_id_type=MESH))
            d2d_send.append(pltpu.make_async_remote_copy(buf.at[src, mine], buf.at[src, mine], s_d2d.at[i, k],
                                                         r_d2d.at[i, k], device_id=(xi, yi, 1 - ci), device_id_type=MESH))
            d2d_recv.append(pltpu.make_async_remote_copy(buf.at[src, sibs], buf.at[src, sibs], s_d2d.at[i, k],
                                                         r_d2d.at[i, k], device_id=(xi, yi, 1 - ci), device_id_type=MESH))
    return ici_send, ici_recv, d2d_send, d2d_recv


def _gather_start(bufs, sems):
    for cp in _gather_copies(bufs, sems)[0]:
        cp.start()


def _gather_forward(bufs, sems):
    _, ici_recv, d2d_send, _ = _gather_copies(bufs, sems)
    for arrived, forward in zip(ici_recv, d2d_send):
        arrived.wait_recv()
        forward.start()


def _gather_finish(bufs, sems):
    ici_send, _, d2d_send, d2d_recv = _gather_copies(bufs, sems)
    for cp in d2d_recv:
        cp.wait_recv()
    for cp in ici_send + d2d_send:
        cp.wait_send()


def _gather_sems(n):
    return [pltpu.SemaphoreType.DMA((n, 3))] * 4


def _weights_allgather(bufs, l):
    n = len(bufs)

    def body(*refs):
        outs, sems = refs[n:2 * n], refs[2 * n:]
        _gather_start(outs, sems)
        _gather_forward(outs, sems)
        _gather_finish(outs, sems)

    return pl.pallas_call(
        body, name=f"weights_allgather_{l}",
        out_shape=[jax.ShapeDtypeStruct(b.shape, b.dtype) for b in bufs],
        in_specs=_hbm_specs(n), out_specs=_hbm_specs(n), input_output_aliases={i: i for i in range(n)},
        scratch_shapes=_gather_sems(n),
    )(*bufs)


def _pair_send_copies(gs, outs, sems):
    ssem, rsem = sems
    xi, yi, ci = _my_place()
    every = pl.ds(0, N_CHIPS)
    return [pltpu.make_async_remote_copy(g.at[every, _halves(g, ci)[1]], o, ssem.at[i], rsem.at[i],
                                         device_id=(xi, yi, 1 - ci), device_id_type=MESH)
            for i, (g, o) in enumerate(zip(gs, outs))]


def _pair_send_shapes(gs):
    return [jax.ShapeDtypeStruct((N_CHIPS, g.shape[1] // 2, g.shape[2]), g.dtype) for g in gs]


def _pair_send_sems(n):
    return [pltpu.SemaphoreType.DMA((n,)), pltpu.SemaphoreType.DMA((n,))]


def _grads_pair_send(gs, l):
    n = len(gs)

    def body(*refs):
        cps = _pair_send_copies(refs[:n], refs[n:2 * n], refs[2 * n:])
        for cp in cps:
            cp.start()
        for cp in cps:
            cp.wait()

    return pl.pallas_call(
        body, name=f"grads_pair_send_{l}", out_shape=_pair_send_shapes(gs),
        in_specs=_hbm_specs(n), out_specs=_hbm_specs(n), scratch_shapes=_pair_send_sems(n),
    )(*gs)


def _exchange_copies(ps, recvs, sems):
    ssems, rsems = sems
    xi, yi, ci = _my_place()
    cps = []
    for i, (p, rc) in enumerate(zip(ps, recvs)):
        for k, hop in enumerate(HOPS):
            tx, ty = _hop(xi, yi, hop)
            cps.append(pltpu.make_async_remote_copy(p.at[2 * tx + ty], rc.at[k], ssems.at[i, k], rsems.at[i, k],
                                                    device_id=(tx, ty, ci), device_id_type=MESH))
    return cps


def _exchange_sems(n):
    return [pltpu.SemaphoreType.DMA((n, 3))] * 2


def _exchange_shapes(ps):
    return [jax.ShapeDtypeStruct((3,) + p.shape[1:], p.dtype) for p in ps]


def _grads_chip_exchange(ps, l, small):
    n = len(ps)

    def body(*refs):
        pairs, x_ref = refs[:n], refs[n]
        recvs, o_ref = refs[n + 1:2 * n + 1], refs[2 * n + 1]
        esems, asems = refs[2 * n + 2:2 * n + 4], refs[2 * n + 4:]
        cps = _exchange_copies(pairs, recvs, esems)
        for cp in cps:
            cp.start()
        _allgather_start(x_ref, o_ref, *asems)
        _allgather_finish(x_ref, o_ref, *asems)
        for cp in cps:
            cp.wait()

    vmem = pl.BlockSpec(memory_space=pltpu.VMEM)
    res = pl.pallas_call(
        body, name=f"grads_chip_exchange_{l}",
        out_shape=_exchange_shapes(ps) + [jax.ShapeDtypeStruct((N_CHIPS,) + small.shape, small.dtype)],
        in_specs=_hbm_specs(n) + [vmem], out_specs=_hbm_specs(n) + [vmem],
        scratch_shapes=_exchange_sems(n) + _ALLGATHER_SEMS,
        compiler_params=pltpu.CompilerParams(vmem_limit_bytes=VMEM_LIMIT),
    )(*ps, small)
    return res[:n], res[n]


def _grads_pair_share(gs):
    n = len(gs)

    def body(*refs):
        outs = refs[n:2 * n]
        ssem, rsem = refs[2 * n:]
        xi, yi, ci = _my_place()
        every = pl.ds(0, DEPTH)
        sends = []
        for i in range(n):
            mine = _halves(outs[i], ci)[0]
            cp = pltpu.make_async_remote_copy(outs[i].at[every, mine], outs[i].at[every, mine], ssem.at[i], rsem.at[i],
                                              device_id=(xi, yi, 1 - ci), device_id_type=MESH)
            cp.start()
            sends.append(cp)
        for i in range(n):
            sibs = _halves(outs[i], ci)[1]
            pltpu.make_async_remote_copy(outs[i].at[every, sibs], outs[i].at[every, sibs], ssem.at[i], rsem.at[i],
                                         device_id=(xi, yi, 1 - ci), device_id_type=MESH).wait_recv()
        for cp in sends:
            cp.wait_send()

    return pl.pallas_call(
        body, name="grads_pair_share",
        out_shape=[jax.ShapeDtypeStruct(g.shape, g.dtype) for g in gs],
        in_specs=_hbm_specs(n), out_specs=_hbm_specs(n), input_output_aliases={i: i for i in range(n)},
        scratch_shapes=[pltpu.SemaphoreType.DMA((n,)), pltpu.SemaphoreType.DMA((n,))],
    )(*gs)


def _row_tile(r):
    return min(r, 256)


def _pair_sum(g, ga, place, name):
    _, r, c = g.shape
    tr = _row_tile(r // 2)
    nk = r // 2 // tr

    def body(p_ref, g_ref, ga_ref, o_ref):
        o_ref[...] = (g_ref[...].astype(F32) + ga_ref[...].astype(F32)).astype(o_ref.dtype)

    return pl.pallas_call(
        body, name=name, out_shape=jax.ShapeDtypeStruct(ga.shape, ga.dtype),
        grid_spec=pltpu.PrefetchScalarGridSpec(
            num_scalar_prefetch=1, grid=(N_CHIPS, nk),
            in_specs=[pl.BlockSpec((None, tr, c), lambda j, k, pr: (j, pr[1] * nk + k, 0)),
                      pl.BlockSpec((None, tr, c), lambda j, k, pr: (j, k, 0))],
            out_specs=pl.BlockSpec((None, tr, c), lambda j, k, pr: (j, k, 0))),
        compiler_params=_cparams(("parallel", "parallel")),
    )(place, g, ga)


def _chip_sum(pair, recv, buf, l, place, name):
    _, rh, c = pair.shape
    tr = _row_tile(rh)
    nk = rh // tr

    def body(p_ref, own_ref, r_ref, *rest):
        o_ref = rest[-1]
        acc = own_ref[...].astype(F32) + r_ref[0].astype(F32)
        acc = acc + r_ref[1].astype(F32)
        o_ref[...] = acc + r_ref[2].astype(F32)

    in_specs = [pl.BlockSpec((None, tr, c), lambda k, pr: (pr[0], k, 0)),
                pl.BlockSpec((3, tr, c), lambda k, pr: (0, k, 0))]
    args = [pair, recv]
    aliases = {}
    if buf is not None:
        in_specs.append(pl.BlockSpec(memory_space=pl.ANY))
        args.append(buf)
        aliases = {3: 0}
    return pl.pallas_call(
        body, name=name, out_shape=jax.ShapeDtypeStruct((DEPTH, 2 * rh, c), F32),
        grid_spec=pltpu.PrefetchScalarGridSpec(
            num_scalar_prefetch=1, grid=(nk,), in_specs=in_specs,
            out_specs=pl.BlockSpec((None, tr, c), lambda k, pr: (l, pr[1] * nk + k, 0))),
        input_output_aliases=aliases, compiler_params=_cparams(("parallel",)),
    )(place, *args)


def _adam_math(w, g, m, v):
    m = B1 * m + (1.0 - B1) * g
    v = B2 * v + (1.0 - B2) * (g * g)
    m_hat = m / (1.0 - B1 ** STEP)
    v_hat = v / (1.0 - B2 ** STEP)
    delta = -LR * (m_hat / (jnp.sqrt(v_hat) + ADAM_EPS) + WD * w)
    return delta, m, v


def _adamw(w, g, m, v, name):
    n_l, r, c = w.shape
    tr = _row_tile(r)

    def body(w_ref, g_ref, m_ref, v_ref, d_ref, nm_ref, nv_ref):
        d_ref[...], nm_ref[...], nv_ref[...] = _adam_math(w_ref[...], g_ref[...], m_ref[...], v_ref[...])

    spec = pl.BlockSpec((None, tr, c), lambda i, k: (i, k, 0))
    return pl.pallas_call(
        body, name=name, out_shape=[jax.ShapeDtypeStruct(w.shape, F32)] * 3, grid=(n_l, r // tr),
        in_specs=[spec] * 4, out_specs=[spec] * 3, compiler_params=_cparams(("parallel", "parallel")),
    )(w, g, m, v)


def _ada_forward(c_all, w_ada, b_cols):
    cols = w_ada.shape[2]
    tn = 512

    def body(c_ref, w_ref, b_ref, o_ref):
        o_ref[...] = _dotf(_silu(c_ref[...]), w_ref[...]) + b_ref[...]

    return pl.pallas_call(
        body, name="ada_forward", out_shape=jax.ShapeDtypeStruct((DEPTH, 8, cols), F32), grid=(DEPTH, cols // tn),
        in_specs=[pl.BlockSpec((8, D), lambda l, j: (0, 0)),
                  pl.BlockSpec((None, D, tn), lambda l, j: (l, 0, j)),
                  pl.BlockSpec((None, 1, tn), lambda l, j: (l, 0, j))],
        out_specs=pl.BlockSpec((None, 8, tn), lambda l, j: (l, 0, j)),
        compiler_params=_cparams(("parallel", "parallel")),
    )(c_all, w_ada, b_cols.reshape(DEPTH, 1, cols))


def _ada_backward_adamw(c_all, dmod_cols, w, m, v):
    cols = w.shape[2]
    tn = 512

    def body(c_ref, d_ref, w_ref, m_ref, v_ref, g_ref, dl_ref, nm_ref, nv_ref):
        g = _dotf_tn(_silu(c_ref[...]), d_ref[...])
        g_ref[...] = g
        dl_ref[...], nm_ref[...], nv_ref[...] = _adam_math(w_ref[...], g, m_ref[...], v_ref[...])

    wspec = pl.BlockSpec((None, D, tn), lambda l, j: (l, 0, j))
    return pl.pallas_call(
        body, name="ada_backward_adamw", out_shape=[jax.ShapeDtypeStruct(w.shape, F32)] * 4, grid=(DEPTH, cols // tn),
        in_specs=[pl.BlockSpec((8, D), lambda l, j: (0, 0)), pl.BlockSpec((None, 8, tn), lambda l, j: (l, 0, j)),
                  wspec, wspec, wspec],
        out_specs=[wspec] * 4, compiler_params=_cparams(("parallel", "parallel")),
    )(c_all, dmod_cols, w, m, v)


def _tok_tile(t):
    return min(t, 512)


def _wspec4(r, c, l):
    return pl.BlockSpec((N_CHIPS, r, c), lambda i: (0, 0, 0))


def _fwd_in(x, modv, w_in, l):
    t = x.shape[0]
    tm = _tok_tile(t)

    def body(x_ref, mod_ref, w_ref, o_ref, h_ref):
        h = _norm_mod(x_ref[...], mod_ref[6:7, :], mod_ref[1:2, :], mod_ref[0:1, :]).astype(BF16)
        h_ref[...] = h
        o_ref[...] = jnp.dot(h, w_ref[...], preferred_element_type=F32)

    return pl.pallas_call(
        body, name=f"fwd_in_{l}", grid=(t // tm,),
        out_shape=[jax.ShapeDtypeStruct((t, NW), F32), jax.ShapeDtypeStruct((t, D), BF16)],
        in_specs=[pl.BlockSpec((tm, D), lambda i: (i, 0)), pl.BlockSpec((None, 8, D), lambda i: (l, 0, 0)),
                  pl.BlockSpec((D, NW), lambda i: (0, 0))],
        out_specs=[pl.BlockSpec((tm, NW), lambda i: (i, 0)), pl.BlockSpec((tm, D), lambda i: (i, 0))],
        compiler_params=_cparams(("parallel",)),
    )(x, modv, w_in)


def _fwd_out_ff1(x, mix, modv, w_out, w_ff1, l):
    t = x.shape[0]
    tm = _tok_tile(t)

    def body(x_ref, mix_ref, mod_ref, wo_ref, w_ref, x1_ref, o_ref, h_ref):
        x1 = x_ref[...] + mod_ref[2:3, :] * jnp.dot(mix_ref[...], wo_ref[...].reshape(D, D), preferred_element_type=F32)
        x1_ref[...] = x1
        h = _norm_mod(x1, mod_ref[7:8, :], mod_ref[4:5, :], mod_ref[3:4, :]).astype(BF16)
        h_ref[...] = h
        for j in range(N_CHIPS):
            f = jnp.dot(h, w_ref[j], preferred_element_type=F32)
            o_ref[:, j * D:(j + 1) * D] = jnp.maximum(f, 0.0).astype(BF16)

    tok = pl.BlockSpec((tm, D), lambda i: (i, 0))
    return pl.pallas_call(
        body, name=f"fwd_out_ff1_{l}", grid=(t // tm,),
        out_shape=[jax.ShapeDtypeStruct((t, D), F32), jax.ShapeDtypeStruct((t, DFF), BF16),
                   jax.ShapeDtypeStruct((t, D), BF16)],
        in_specs=[tok, tok, pl.BlockSpec((None, 8, D), lambda i: (l, 0, 0)), _wspec4(D // N_CHIPS, D, l),
                  _wspec4(D, D, l)],
        out_specs=[tok, pl.BlockSpec((tm, DFF), lambda i: (i, 0)), tok],
        compiler_params=_cparams(("parallel",)),
    )(x, mix, modv, w_out, w_ff1)


def _fwd_ff2(x, r, modv, w_ff2, l):
    t = x.shape[0]
    tm = _tok_tile(t)

    def body(x_ref, r_ref, mod_ref, w_ref, o_ref):
        acc = jnp.zeros((tm, D), F32)
        for j in range(N_CHIPS):
            rj = r_ref[:, j * D:(j + 1) * D].astype(F32)
            acc = acc + jnp.dot((rj * rj).astype(BF16), w_ref[j], preferred_element_type=F32)
        o_ref[...] = x_ref[...] + mod_ref[5:6, :] * acc

    return pl.pallas_call(
        body, name=f"fwd_ff2_{l}", out_shape=jax.ShapeDtypeStruct((t, D), F32), grid=(t // tm,),
        in_specs=[pl.BlockSpec((tm, D), lambda i: (i, 0)), pl.BlockSpec((tm, DFF), lambda i: (i, 0)),
                  pl.BlockSpec((None, 8, D), lambda i: (l, 0, 0)), _wspec4(D, D, l)],
        out_specs=pl.BlockSpec((tm, D), lambda i: (i, 0)), compiler_params=_cparams(("parallel",)),
    )(x, r, modv, w_ff2)


def _loss_head(x, target, final_g):
    t = x.shape[0]
    tm = _tok_tile(t)

    def body(x_ref, t_ref, g_ref, dx_ref, st_ref):
        @pl.when(pl.program_id(0) == 0)
        def _():
            st_ref[...] = jnp.zeros_like(st_ref)

        xh, rstd = _rms_stats(x_ref[...])
        g = g_ref[...]
        err = xh * g - t_ref[...]
        loss = 0.5 * jnp.sum(jnp.mean(err * err, axis=-1, keepdims=True), axis=0, keepdims=True)
        dy = err * (1.0 / D)
        st_ref[0:1, :] += jnp.sum(dy * xh, axis=0, keepdims=True)
        st_ref[1:2, :] += jnp.broadcast_to(loss, (1, D))
        dxh = dy * g
        dx_ref[...] = rstd * (dxh - xh * jnp.mean(dxh * xh, axis=-1, keepdims=True))

    return pl.pallas_call(
        body, name="loss_head", out_shape=[jax.ShapeDtypeStruct((t, D), F32), jax.ShapeDtypeStruct((8, D), F32)],
        grid=(t // tm,),
        in_specs=[pl.BlockSpec((tm, D), lambda i: (i, 0)), pl.BlockSpec((tm, D), lambda i: (i, 0)),
                  pl.BlockSpec((1, D), lambda i: (0, 0))],
        out_specs=[pl.BlockSpec((tm, D), lambda i: (i, 0)), pl.BlockSpec((8, D), lambda i: (0, 0))],
        compiler_params=_cparams(("arbitrary",)),
    )(x, target, final_g.reshape(1, D))


def _bwd_ff2(dx2, r, modv, w_ff2, l):
    t = dx2.shape[0]
    tm = _tok_tile(t)

    def body(d_ref, r_ref, mod_ref, w_ref, o_ref):
        dyg = (d_ref[...] * mod_ref[5:6, :]).astype(BF16)
        for j in range(N_CHIPS):
            da = lax.dot_general(dyg, w_ref[j], (((1,), (1,)), ((), ())), preferred_element_type=F32)
            o_ref[:, j * D:(j + 1) * D] = (da * 2.0 * r_ref[:, j * D:(j + 1) * D].astype(F32)).astype(BF16)

    return pl.pallas_call(
        body, name=f"bwd_ff2_{l}", out_shape=jax.ShapeDtypeStruct((t, DFF), BF16), grid=(t // tm,),
        in_specs=[pl.BlockSpec((tm, D), lambda i: (i, 0)), pl.BlockSpec((tm, DFF), lambda i: (i, 0)),
                  pl.BlockSpec((None, 8, D), lambda i: (l, 0, 0)), _wspec4(D, D, l)],
        out_specs=pl.BlockSpec((tm, DFF), lambda i: (i, 0)), compiler_params=_cparams(("parallel",)),
    )(dx2, r, modv, w_ff2)


def _bwd_norm(dy, w, x, dres, modv, l, which, send=(), w_out=None):
    t = x.shape[0]
    tm = _tok_tile(t)
    nsteps = t // tm
    rows = (6, 1) if which == "in" else (7, 4)
    width = dy.shape[1]
    ns = len(send)
    nb_ = 0 if w_out is None else 1

    def body(*refs):
        dy_ref, w_ref, x_ref, dr_ref, mod_ref = refs[:5]
        wo_ref = refs[5] if nb_ else None
        parts = refs[5 + nb_:5 + nb_ + ns]
        dx_ref, st_ref = refs[5 + nb_ + ns:7 + nb_ + ns]
        dmix_ref = refs[7 + nb_ + ns] if nb_ else None
        from_sib = refs[7 + 2 * nb_ + ns:7 + 2 * nb_ + 2 * ns]
        sems = refs[7 + 2 * nb_ + 2 * ns:]

        @pl.when(pl.program_id(0) == 0)
        def _():
            st_ref[...] = jnp.zeros_like(st_ref)
            if ns:
                for cp in _pair_send_copies(parts, from_sib, sems):
                    cp.start()

        if which == "in":
            dh = lax.dot_general(dy_ref[...], w_ref[...], (((1,), (1,)), ((), ())), preferred_element_type=F32)
        else:
            dh = jnp.zeros((tm, D), F32)
            for j in range(N_CHIPS):
                dh = dh + lax.dot_general(dy_ref[:, j * D:(j + 1) * D], w_ref[j], (((1,), (1,)), ((), ())),
                                          preferred_element_type=F32)
        ng, sc = mod_ref[rows[0]:rows[0] + 1, :], mod_ref[rows[1]:rows[1] + 1, :]
        dx, dsh, dsc, dng = _norm_mod_bwd(dh, x_ref[...], ng, sc)
        dx_new = dr_ref[...] + dx
        dx_ref[...] = dx_new
        st_ref[0:1, :] += dsh
        st_ref[1:2, :] += dsc
        st_ref[2:3, :] += dng
        if nb_:
            dyg = (dx_new * mod_ref[2:3, :]).astype(BF16)
            dmix_ref[...] = lax.dot_general(dyg, wo_ref[...].reshape(D, D), (((1,), (1,)), ((), ())),
                                            preferred_element_type=F32).astype(BF16)

        if ns:
            @pl.when(pl.program_id(0) == nsteps - 1)
            def _():
                for cp in _pair_send_copies(parts, from_sib, sems):
                    cp.wait()

    tok = pl.BlockSpec((tm, D), lambda i: (i, 0))
    wspec = pl.BlockSpec((D, NW), lambda i: (0, 0)) if which == "in" else _wspec4(D, D, l)
    return pl.pallas_call(
        body, name=f"bwd_norm_{which}_{l}",
        out_shape=[jax.ShapeDtypeStruct((t, D), F32), jax.ShapeDtypeStruct((8, D), F32)]
        + [jax.ShapeDtypeStruct((t, D), BF16)] * nb_ + _pair_send_shapes(send),
        grid=(nsteps,),
        in_specs=[pl.BlockSpec((tm, width), lambda i: (i, 0)), wspec, tok, tok,
                  pl.BlockSpec((None, 8, D), lambda i: (l, 0, 0))]
        + [_wspec4(D // N_CHIPS, D, l)] * nb_ + _hbm_specs(ns),
        out_specs=[tok, pl.BlockSpec((8, D), lambda i: (0, 0))] + [tok] * nb_ + _hbm_specs(ns),
        scratch_shapes=_pair_send_sems(ns) if ns else [],
        compiler_params=_cparams(("arbitrary",)),
    )(dy, w, x, dres, modv, *([w_out] * nb_), *send)


def _grad_weight(lhs, rhs, modv, l, which, w_gate=None):
    t = lhs.shape[0]
    tm = min(t, 2048)
    nt = t // tm
    gated = which in ("out", "ff2")
    if which == "in":
        nj, lw, rw, orows, ocols = 5, D, NW // 5, D, NW // 5
    elif which == "ff1":
        nj, lw, rw, orows, ocols = N_CHIPS, D, D, D, D
    elif which == "out":
        nj, lw, rw, orows, ocols = N_CHIPS, D // N_CHIPS, D, D // N_CHIPS, D
    else:
        nj, lw, rw, orows, ocols = N_CHIPS, D, D, D, D
    gate_row = 2 if which == "out" else 5

    def body(*refs):
        if gated:
            l_ref, r_ref, mod_ref, wg_ref, o_ref, dg_ref, acc = refs
        else:
            l_ref, r_ref, mod_ref, o_ref, acc = refs
        j, k = pl.program_id(0), pl.program_id(1)

        @pl.when(k == 0)
        def _():
            acc[...] = jnp.zeros_like(acc)

        if which == "ff2":
            lv = l_ref[...].astype(F32)
            lv = lv * lv
        else:
            lv = l_ref[...]
        acc[...] += _dot_tn(lv, r_ref[...])

        if gated:
            @pl.when(jnp.logical_and(j == 0, k == 0))
            def _():
                dg_ref[...] = jnp.zeros_like(dg_ref)

        @pl.when(k == nt - 1)
        def _():
            raw = acc[...]
            if gated:
                o_ref[...] = (raw * mod_ref[gate_row:gate_row + 1, :]).astype(o_ref.dtype)
                dg_ref[0:1, :] += jnp.sum(raw * wg_ref[...].astype(F32), axis=0, keepdims=True)
            else:
                o_ref[...] = raw.astype(o_ref.dtype)

    if which in ("in", "ff1"):
        lspec = pl.BlockSpec((tm, lw), lambda j, k: (k, 0))
        rspec = pl.BlockSpec((tm, rw), lambda j, k: (k, j))
    else:
        lspec = pl.BlockSpec((tm, lw), lambda j, k: (k, j))
        rspec = pl.BlockSpec((tm, rw), lambda j, k: (k, 0))
    mspec = pl.BlockSpec((None, 8, D), lambda j, k: (l, 0, 0))
    if which == "in":
        ospec = pl.BlockSpec((orows, ocols), lambda j, k: (0, j))
        out_shape = [jax.ShapeDtypeStruct((D, NW), BF16)]
    else:
        ospec = pl.BlockSpec((None, orows, ocols), lambda j, k: (j, 0, 0))
        out_shape = [jax.ShapeDtypeStruct((N_CHIPS, orows, ocols), BF16)]
    in_specs = [lspec, rspec, mspec]
    args = [lhs, rhs, modv]
    out_specs = [ospec]
    if gated:
        in_specs.append(pl.BlockSpec((None, orows, ocols), lambda j, k: (j, 0, 0)))
        args.append(w_gate)
        out_specs.append(pl.BlockSpec((8, D), lambda j, k: (0, 0)))
        out_shape.append(jax.ShapeDtypeStruct((8, D), F32))
    res = pl.pallas_call(
        body, name=f"grad_w_{which}_{l}", out_shape=out_shape, grid=(nj, nt), in_specs=in_specs, out_specs=out_specs,
        scratch_shapes=[pltpu.VMEM((orows, ocols), F32)], compiler_params=_cparams(("arbitrary", "arbitrary")),
    )(*args)
    return (res[0], res[1]) if gated else (res[0], None)


def _tri_masks():
    rows, cols = _iota2((BLK, BLK), 0), _iota2((BLK, BLK), 1)
    return rows >= cols, rows > cols


def _sgu_forward(p_ref, lnp_ref, sguw_ref, sgub_ref):
    incl, _ = _tri_masks()
    ug = _gelu(p_ref[:, 0:512])
    vg = _gelu(p_ref[:, 512:1024])
    mu = jnp.mean(vg, axis=-1, keepdims=True)
    xc = vg - mu
    rstd = lax.rsqrt(jnp.mean(xc * xc, axis=-1, keepdims=True) + LN_EPS)
    vhat = xc * rstd
    vn = vhat * lnp_ref[0:1, :] + lnp_ref[1:2, :]
    bias = sgub_ref[...]
    ys, mixed, wms = [], [], []
    for h in range(HEADS):
        wm = jnp.where(incl, sguw_ref[h], 0.0)
        mx = _dot(wm, vn[:, h * HD:(h + 1) * HD]) + _col(bias, h)
        ys.append(ug[:, h * HD:(h + 1) * HD] * mx)
        mixed.append(mx)
        wms.append(wm)
    return ys, ug, vhat, rstd, vn, mixed, wms


def _conv_forward(xbuf, cw_ref):
    conv = cw_ref[0:1, :] * xbuf[5:5 + BLK, :]
    for j in range(1, 4):
        conv = conv + cw_ref[j:j + 1, :] * xbuf[5 + j:5 + j + BLK, :]
    return conv


def _gates(gt, gv_ref):
    incl, _ = _tri_masks()
    beta = _sigmoid(gt)
    neg_a = -jnp.exp(gv_ref[0:1, :])
    gl = neg_a * _softplus(gt + gv_ref[1:2, :])
    gc = _dotf(jnp.where(incl, 1.0, 0.0).astype(F32), gl)
    return beta, gl, gc, gc.T, neg_a


def _head_chunk(act, beta, gc, gct, h):
    incl, strict = _tri_masks()
    qh = act[:, h * HD:(h + 1) * HD]
    kh = act[:, 512 + h * HD:512 + (h + 1) * HD]
    vh = act[:, 1024 + h * HD:1024 + (h + 1) * HD]
    rq = lax.rsqrt(jnp.sum(qh * qh, axis=-1, keepdims=True) + RMS_EPS)
    rk = lax.rsqrt(jnp.sum(kh * kh, axis=-1, keepdims=True) + RMS_EPS)
    qhat, khat = qh * rq, kh * rk
    qn = qhat * QK_SCALE
    b = _col(beta, h)
    gcol = _col(gc, 4 + h)
    grow = _row(gct, 4 + h)
    dmat = jnp.where(incl, jnp.exp(jnp.where(incl, gcol - grow, 0.0)), 0.0)
    gam = jnp.exp(gcol)
    glast = _row(gcol, BLK - 1)
    e = jnp.exp(glast - gcol)
    kk = _d3_nt(khat, khat)
    return dict(qhat=qhat, khat=khat, qn=qn, vh=vh, rq=rq, rk=rk, b=b, dmat=dmat, gam=gam, glast=glast, e=e, kk=kk,
                strict=strict, incl=incl)


def _mixer_forward(p, lnp, sgu_w, sgu_bt, cw, gv, l, gather=()):
    t = p.shape[0]
    nb = t // BLK
    ng = len(gather)

    def body(*refs):
        p_ref, lnp_ref, sguw_ref, sgub_ref, cw_ref, gv_ref = refs[:6]
        mix_ref, s_out, t_out, u_out, w_out, o_out, conv_out = refs[6 + ng:13 + ng]
        gbufs = refs[13 + ng:13 + 2 * ng]
        s_scr, xbuf = refs[13 + 2 * ng:15 + 2 * ng]
        gsems = refs[15 + 2 * ng:]

        @pl.when(pl.program_id(0) == 0)
        def _():
            s_scr[...] = jnp.zeros_like(s_scr)
            xbuf[0:8, :] = jnp.zeros((8, 1536), F32)
            if ng:
                _gather_start(gbufs, gsems)

        ys = _sgu_forward(p_ref, lnp_ref, sguw_ref, sgub_ref)[0]
        for h in range(HEADS):
            mix_ref[:, h * HD:(h + 1) * HD] = ys[h].astype(BF16)

        xbuf[8:8 + BLK, :] = p_ref[:, 1024:2560]
        conv = _conv_forward(xbuf, cw_ref)
        conv_out[...] = conv
        act = _silu(conv)
        xbuf[0:8, :] = xbuf[BLK:BLK + 8, :]
        beta, _, gc, gct, _ = _gates(p_ref[:, GATE0:NW], gv_ref)
        chunks = [_head_chunk(act, beta, gc, gct, h) for h in range(HEADS)]
        for h, hc in enumerate(chunks):
            t_out[h] = jnp.where(hc["strict"], hc["b"] * hc["kk"] * hc["dmat"], 0.0)
        t_out[...] = _tri_inverse(t_out[...])
        for h, hc in enumerate(chunks):
            tm = t_out[h]
            u = _d3(tm, hc["b"] * hc["vh"])
            w = _d3(tm, (hc["b"] * hc["gam"]) * hc["khat"])
            qkm = _dot_nt(hc["qn"], hc["khat"]) * hc["dmat"]
            s = s_scr[h]
            wn = u - _dot(w, s)
            o = _dot(hc["qn"] * hc["gam"], s) + _dot(qkm, wn)
            s_out[h] = s
            s_scr[h] = jnp.exp(hc["glast"]) * s + _dot_tn(hc["khat"] * hc["e"], wn)
            sl = slice(h * HD, (h + 1) * HD)
            u_out[:, sl] = u
            w_out[:, sl] = w
            o_out[:, sl] = o
            on = o * lax.rsqrt(jnp.mean(o * o, axis=-1, keepdims=True) + RMS_EPS) * gv_ref[2:3, :]
            mix_ref[:, 512 + h * HD:512 + (h + 1) * HD] = (on * _silu(p_ref[:, 2560 + h * HD:2560 + (h + 1) * HD])).astype(BF16)

        if ng:
            @pl.when(pl.program_id(0) == nb - 1 - min(3, nb - 1))
            def _():
                _gather_forward(gbufs, gsems)

            @pl.when(pl.program_id(0) == nb - 1)
            def _():
                _gather_finish(gbufs, gsems)

    tok = lambda w: pl.BlockSpec((BLK, w), lambda i: (i, 0))
    st = pl.BlockSpec((None, HEADS, HD, HD), lambda i: (i, 0, 0, 0))
    return pl.pallas_call(
        body, name=f"mixer_fwd_{l}", grid=(nb,),
        out_shape=[jax.ShapeDtypeStruct((t, D), BF16), jax.ShapeDtypeStruct((nb, HEADS, HD, HD), F32),
                   jax.ShapeDtypeStruct((nb, HEADS, HD, HD), F32), jax.ShapeDtypeStruct((t, 512), F32),
                   jax.ShapeDtypeStruct((t, 512), F32), jax.ShapeDtypeStruct((t, 512), F32),
                   jax.ShapeDtypeStruct((t, 1536), F32)]
        + [jax.ShapeDtypeStruct(b.shape, b.dtype) for b in gather],
        in_specs=[tok(NW), pl.BlockSpec((None, 8, 512), lambda i: (l, 0, 0)),
                  pl.BlockSpec((None, HEADS, HD, HD), lambda i: (l, 0, 0, 0)),
                  pl.BlockSpec((None, HD, HD), lambda i: (l, 0, 0)), pl.BlockSpec((None, 8, 1536), lambda i: (l, 0, 0)),
                  pl.BlockSpec((None, 8, HD), lambda i: (l, 0, 0))] + _hbm_specs(ng),
        out_specs=[tok(D), st, st, tok(512), tok(512), tok(512), tok(1536)]
        + _hbm_specs(ng),
        input_output_aliases={6 + i: 7 + i for i in range(ng)},
        scratch_shapes=[pltpu.VMEM((HEADS, HD, HD), F32), pltpu.VMEM((BLK + 8, 1536), F32)]
        + (_gather_sems(ng) if ng else []),
        compiler_params=_cparams(("arbitrary",)),
    )(p, lnp, sgu_w, sgu_bt, cw, gv, *gather)


def _mixer_backward(p, dmix, saved, lnp, sgu_w, sgu_bt, cw, gv, l, exchange=()):
    t = p.shape[0]
    nb = t // BLK
    s_sv, t_sv, u_sv, w_sv, o_sv, conv_sv = saved
    ne = len(exchange)

    def body(*refs):
        (p_ref, dmix_ref, s_ref, t_ref, u_ref, w_ref, o_ref, conv_ref, lnp_ref, sguw_ref, sgub_ref, cw_ref,
         gv_ref) = refs[:13]
        pairs = refs[13:13 + ne]
        dp_ref, dlnp_ref, dsguw_ref, dsgub_ref, dcw_ref, dgv_ref = refs[13 + ne:19 + ne]
        recvs = refs[19 + ne:19 + 2 * ne]
        ds_scr, dcbuf = refs[19 + 2 * ne:21 + 2 * ne]
        esems = refs[21 + 2 * ne:]

        @pl.when(pl.program_id(0) == 0)
        def _():
            if ne:
                for cp in _exchange_copies(pairs, recvs, esems):
                    cp.start()
            ds_scr[...] = jnp.zeros_like(ds_scr)
            dcbuf[BLK:BLK + 8, :] = jnp.zeros((8, 1536), F32)
            dlnp_ref[...] = jnp.zeros_like(dlnp_ref)
            dsguw_ref[...] = jnp.zeros_like(dsguw_ref)
            dsgub_ref[...] = jnp.zeros_like(dsgub_ref)
            dcw_ref[...] = jnp.zeros_like(dcw_ref)
            dgv_ref[...] = jnp.zeros_like(dgv_ref)

        incl, strict = _tri_masks()
        _, ug, vhat, rstd, vn, mixed, wms = _sgu_forward(p_ref, lnp_ref, sguw_ref, sgub_ref)
        dvn_parts, dug_parts = [], []
        dbias = jnp.zeros((BLK, HD), F32)
        for h in range(HEADS):
            sl = slice(h * HD, (h + 1) * HD)
            dy = dmix_ref[:, sl].astype(F32)
            dmx = dy * ug[:, sl]
            dug_parts.append(dy * mixed[h])
            dsguw_ref[h] += jnp.where(incl, _dot_nt(dmx, vn[:, sl]), 0.0)
            dbias = dbias + _put_col(jnp.sum(dmx, axis=1, keepdims=True), h)
            dvn_parts.append(_dot_tn(wms[h], dmx))
        dsgub_ref[...] += dbias
        dvn = jnp.concatenate(dvn_parts, axis=1)
        dug = jnp.concatenate(dug_parts, axis=1)
        dlnp_ref[0:1, :] += jnp.sum(dvn * vhat, axis=0, keepdims=True)
        dlnp_ref[1:2, :] += jnp.sum(dvn, axis=0, keepdims=True)
        dvhat = dvn * lnp_ref[0:1, :]
        dvg = rstd * (dvhat - jnp.mean(dvhat, axis=-1, keepdims=True)
                      - vhat * jnp.mean(dvhat * vhat, axis=-1, keepdims=True))
        dp_ref[:, 0:512] = (dug * _gelu_grad(p_ref[:, 0:512])).astype(BF16)
        dp_ref[:, 512:1024] = (dvg * _gelu_grad(p_ref[:, 512:1024])).astype(BF16)

        conv = conv_ref[...]
        act = _silu(conv)
        gt = p_ref[:, GATE0:NW]
        beta, gl, gc, gct, neg_a = _gates(gt, gv_ref)
        gng = gv_ref[2:3, :]
        dbeta_t = jnp.zeros((BLK, HD), F32)
        dgc_t = jnp.zeros((BLK, HD), F32)
        dgng = jnp.zeros((1, HD), F32)
        for h in range(HEADS):
            sl = slice(h * HD, (h + 1) * HD)
            hc = _head_chunk(act, beta, gc, gct, h)
            b, gam, e, dmat, kk = hc["b"], hc["gam"], hc["e"], hc["dmat"], hc["kk"]
            qn, khat, vh = hc["qn"], hc["khat"], hc["vh"]
            gamlast = jnp.exp(hc["glast"])
            s, tm, u, w, o = s_ref[h], t_ref[h], u_ref[:, sl], w_ref[:, sl], o_ref[:, sl]
            ds_next = ds_scr[h]
            z = p_ref[:, 2560 + h * HD:2560 + (h + 1) * HD]
            dy = dmix_ref[:, 512 + h * HD:512 + (h + 1) * HD].astype(F32)
            ro = lax.rsqrt(jnp.mean(o * o, axis=-1, keepdims=True) + RMS_EPS)
            ohat = o * ro
            dp_ref[:, 2560 + h * HD:2560 + (h + 1) * HD] = (dy * ohat * gng * _silu_grad(z)).astype(BF16)
            don = dy * _silu(z)
            dgng = dgng + jnp.sum(don * ohat, axis=0, keepdims=True)
            dohat = don * gng
            do = ro * (dohat - ohat * jnp.mean(dohat * ohat, axis=-1, keepdims=True))
            qk_raw = _dot_nt(qn, khat)
            qkm = qk_raw * dmat
            qd, kd = qn * gam, khat * e
            wn = u - _dot(w, s)
            dwn = _dot_tn(qkm, do) + _dot(kd, ds_next)
            dqd = _dot_nt(do, s)
            dqkm = jnp.where(incl, _dot_nt(do, wn), 0.0)
            ds_scr[h] = _dot_tn(qd, do) + gamlast * ds_next - _dot_tn(w, dwn)
            dgamlast = jnp.sum(jnp.sum(ds_next * s, axis=1, keepdims=True), axis=0, keepdims=True)
            dkd = _dot_nt(wn, ds_next)
            dw = -_dot_nt(dwn, s)
            db1 = _d3_tn(tm, dwn)
            db2 = _d3_tn(tm, dw)
            dm = jnp.where(strict, -(_dot_nt(db1, u) + _dot_nt(db2, w)), 0.0)
            dbeta = (jnp.sum(dm * kk * dmat, axis=1, keepdims=True) + jnp.sum(db1 * vh, axis=1, keepdims=True)
                     + gam * jnp.sum(db2 * khat, axis=1, keepdims=True))
            dkkm = dm * b * dmat
            ddm = dm * b * kk + dqkm * qk_raw
            dgam = b * jnp.sum(db2 * khat, axis=1, keepdims=True) + jnp.sum(dqd * qn, axis=1, keepdims=True)
            g_qk = dqkm * dmat
            dqn = _dot(g_qk, khat) + dqd * gam
            dkhat = ((b * gam) * db2 + _dot_tn(g_qk, qn) + _dot(dkkm, khat) + _dot_tn(dkkm, khat) + dkd * e)
            dvh = b * db1
            rkd = jnp.sum(dkd * kd, axis=1, keepdims=True)
            emat = ddm * dmat
            dgc = (dgam * gam - rkd + jnp.sum(emat, axis=1, keepdims=True)
                   - jnp.sum(emat.T, axis=1, keepdims=True))
            last = _iota2((BLK, 1), 0) == BLK - 1
            dgc = dgc + jnp.where(last, jnp.sum(rkd, axis=0, keepdims=True) + dgamlast * gamlast, 0.0)
            dgc_t = dgc_t + _put_col(dgc, 4 + h)
            dbeta_t = dbeta_t + _put_col(dbeta, h)
            dqhat = dqn * QK_SCALE
            dq = hc["rq"] * (dqhat - hc["qhat"] * jnp.sum(dqhat * hc["qhat"], axis=-1, keepdims=True))
            dk = hc["rk"] * (dkhat - khat * jnp.sum(dkhat * khat, axis=-1, keepdims=True))
            dcbuf[0:BLK, h * HD:(h + 1) * HD] = dq
            dcbuf[0:BLK, 512 + h * HD:512 + (h + 1) * HD] = dk
            dcbuf[0:BLK, 1024 + h * HD:1024 + (h + 1) * HD] = dvh
        dgv_ref[2:3, :] += dgng
        dgl = _dotf_tn(jnp.where(incl, 1.0, 0.0).astype(F32), dgc_t)
        sig_a = _sigmoid(gt + gv_ref[1:2, :])
        d_araw = dgl * neg_a * sig_a
        dgv_ref[0:1, :] += jnp.sum(dgl * gl, axis=0, keepdims=True)
        dgv_ref[1:2, :] += jnp.sum(d_araw, axis=0, keepdims=True)
        dp_ref[:, GATE0:NW] = (dbeta_t * beta * (1.0 - beta) + d_araw).astype(BF16)
        dcbuf[0:BLK, :] = dcbuf[0:BLK, :] * _silu_grad(conv)
        xcur = p_ref[:, 1024:2560]
        dqkv = jnp.zeros((BLK, 1536), F32)
        for j in range(4):
            shifted = dcbuf[3 - j:3 - j + BLK, :]
            dqkv = dqkv + cw_ref[j:j + 1, :] * shifted
            dcw_ref[j:j + 1, :] += jnp.sum(shifted * xcur, axis=0, keepdims=True)
        dp_ref[:, 1024:2560] = dqkv.astype(BF16)
        dcbuf[BLK:BLK + 8, :] = dcbuf[0:8, :]

        if ne:
            @pl.when(pl.program_id(0) == nb - 1)
            def _():
                for cp in _exchange_copies(pairs, recvs, esems):
                    cp.wait()

    rev = lambda w: pl.BlockSpec((BLK, w), lambda i: (nb - 1 - i, 0))
    st = pl.BlockSpec((None, HEADS, HD, HD), lambda i: (nb - 1 - i, 0, 0, 0))
    fix = lambda *shape: pl.BlockSpec((None,) + shape, lambda i: (l,) + (0,) * len(shape))
    acc = lambda *shape: pl.BlockSpec(shape, lambda i: (0,) * len(shape))
    return pl.pallas_call(
        body, name=f"mixer_bwd_{l}", grid=(nb,),
        out_shape=[jax.ShapeDtypeStruct((t, NW), BF16), jax.ShapeDtypeStruct((8, 512), F32),
                   jax.ShapeDtypeStruct((HEADS, HD, HD), F32), jax.ShapeDtypeStruct((HD, HD), F32),
                   jax.ShapeDtypeStruct((8, 1536), F32), jax.ShapeDtypeStruct((8, HD), F32)]
        + _exchange_shapes(exchange),
        in_specs=[rev(NW), rev(D), st, st, rev(512), rev(512), rev(512),
                  rev(1536),
                  fix(8, 512), fix(HEADS, HD, HD), fix(HD, HD), fix(8, 1536), fix(8, HD)] + _hbm_specs(ne),
        out_specs=[rev(NW), acc(8, 512), acc(HEADS, HD, HD), acc(HD, HD), acc(8, 1536), acc(8, HD)]
        + _hbm_specs(ne),
        scratch_shapes=[pltpu.VMEM((HEADS, HD, HD), F32), pltpu.VMEM((BLK + 8, 1536), F32)]
        + (_exchange_sems(ne) if ne else []),
        compiler_params=_cparams(("arbitrary",)),
    )(p, dmix, s_sv, t_sv, u_sv, w_sv, o_sv, conv_sv, lnp, sgu_w, sgu_bt, cw, gv, *exchange)


_SMALL = (("b_ada", 24), ("norm1_g", 8), ("norm2_g", 8), ("final_g", 8), ("sgu_ln_g", 8), ("sgu_ln_b", 8),
          ("sgu_w", 256), ("sgu_b", 8), ("conv_w", 24), ("a_log", 8), ("dt_bias", 8), ("gdn_norm_g", 8))
_SMALL_PAD = sum(n for _, n in _SMALL)
_DMOD_ROWS = 24


def _pack_rows(parts):
    rows = []
    for (name, n), a in zip(_SMALL, parts):
        flat = a.reshape(-1).astype(F32)
        rows.append(jnp.pad(flat, (0, n * D - flat.shape[0])).reshape(n, D))
    return jnp.concatenate(rows, axis=0)


def _unpack_rows(buf, shapes):
    out, r0 = {}, 0
    for name, n in _SMALL:
        size = math.prod(shapes[name])
        out[name] = buf[r0:r0 + n].reshape(-1)[:size].reshape(shapes[name])
        r0 += n
    return out


def _pair_combine(own, sib):
    n = own.shape[0] - _DMOD_ROWS

    def body(a_ref, b_ref, o_ref):
        first = lax.axis_index("c") == 0
        a, b = a_ref[0:_DMOD_ROWS, :], b_ref[0:_DMOD_ROWS, :]
        o_ref[0:_DMOD_ROWS, :] = jnp.where(first, a, b)
        o_ref[_DMOD_ROWS:2 * _DMOD_ROWS, :] = jnp.where(first, b, a)
        o_ref[2 * _DMOD_ROWS:, :] = a_ref[_DMOD_ROWS:, :] + b_ref[_DMOD_ROWS:, :]

    return pl.pallas_call(
        body, name="small_pair_combine", out_shape=jax.ShapeDtypeStruct((2 * _DMOD_ROWS + n, D), F32),
        compiler_params=pltpu.CompilerParams(vmem_limit_bytes=VMEM_LIMIT),
    )(own, sib)


def _small_finalize(gathered, w, m, v):
    def body(g_ref, w_ref, m_ref, v_ref, go_ref, d_ref, nm_ref, nv_ref):
        sm = g_ref[0, 2 * _DMOD_ROWS:, :] + g_ref[1, 2 * _DMOD_ROWS:, :]
        sm = sm + g_ref[2, 2 * _DMOD_ROWS:, :]
        sm = sm + g_ref[3, 2 * _DMOD_ROWS:, :]
        bsum = jnp.zeros((_DMOD_ROWS, D), F32)
        for j in range(N_CHIPS):
            bsum = bsum + g_ref[j, 0:_DMOD_ROWS, :]
            bsum = bsum + g_ref[j, _DMOD_ROWS:2 * _DMOD_ROWS, :]
        go_ref[0:_DMOD_ROWS, :] = bsum
        go_ref[_DMOD_ROWS:, :] = sm[_DMOD_ROWS:, :]
        d_ref[...], nm_ref[...], nv_ref[...] = _adam_math(w_ref[...], go_ref[...], m_ref[...], v_ref[...])

    return pl.pallas_call(
        body, name="small_finalize", out_shape=[jax.ShapeDtypeStruct(w.shape, F32)] * 4,
        compiler_params=pltpu.CompilerParams(vmem_limit_bytes=VMEM_LIMIT),
    )(gathered, w, m, v)


def kernel(x, c, w_ada, b_ada, norm1_g, w_in, sgu_ln_g, sgu_ln_b, sgu_w, sgu_b, conv_w, a_log, dt_bias, gdn_norm_g, w_out, norm2_g, w_ff1, w_ff2, final_g, loss_target, m_w_ada, m_b_ada, m_norm1_g, m_w_in, m_sgu_ln_g, m_sgu_ln_b, m_sgu_w, m_sgu_b, m_conv_w, m_a_log, m_dt_bias, m_gdn_norm_g, m_w_out, m_norm2_g, m_w_ff1, m_w_ff2, m_final_g, v_w_ada, v_b_ada, v_norm1_g, v_w_in, v_sgu_ln_g, v_sgu_ln_b, v_sgu_w, v_sgu_b, v_conv_w, v_a_log, v_dt_bias, v_gdn_norm_g, v_w_out, v_norm2_g, v_w_ff1, v_w_ff2, v_final_g):
    xi, yi, ci = lax.axis_index("x"), lax.axis_index("y"), lax.axis_index("c")
    chip = 2 * xi + yi
    dev = 2 * chip + ci
    t = x.shape[1]
    x0 = x.reshape(t, D)
    target = loss_target.reshape(t, D)

    c_sib = _pair_exchange(c, "c_pair")
    c_pair = jnp.where(ci == 0, jnp.concatenate([c, c_sib], 0), jnp.concatenate([c_sib, c], 0))
    c_all = _chip_allgather(c_pair, "c_chips").reshape(8, D)
    ada_cols = w_ada.shape[2]
    b_cols = lax.dynamic_slice_in_dim(b_ada, chip * ada_cols, ada_cols, axis=1)
    mod_part = _ada_forward(c_all, w_ada, b_cols)
    conv_cols = conv_w.shape[2]
    packed = jnp.concatenate([mod_part.reshape(DEPTH * 8, ada_cols), conv_w.reshape(DEPTH, 4 * conv_cols)], axis=0)
    packed = _chip_allgather(packed, "mod_chips")
    mod_all = packed[:, :DEPTH * 8].reshape(N_CHIPS, DEPTH, 8, ada_cols)
    mod_mine = lax.dynamic_index_in_dim(mod_all, dev, axis=2, keepdims=False)
    mod = mod_mine.transpose(1, 0, 2).reshape(DEPTH, 6, D)
    modv = jnp.concatenate([mod, norm1_g[:, None, :], norm2_g[:, None, :]], axis=1)
    conv_full = packed[:, DEPTH * 8:].reshape(N_CHIPS, DEPTH, 4, conv_cols).transpose(1, 2, 0, 3).reshape(DEPTH, 4, 1536)

    place = jnp.stack([chip, ci]).astype(jnp.int32)
    wbufs = [[_cast_into_slot(w, l, place) for w in (w_in, w_out, w_ff1, w_ff2)] for l in range(DEPTH)]
    wbufs[0][:1] = _weights_allgather(wbufs[0][:1], 0)

    def full_w_in(g):
        return jnp.pad(g.transpose(1, 0, 2).reshape(D, IN_W), ((0, 0), (0, NW - IN_W)))

    lnp = jnp.pad(jnp.stack([sgu_ln_g, sgu_ln_b], axis=1), ((0, 0), (0, 6), (0, 0)))
    sgu_bt = jnp.pad(sgu_b.transpose(0, 2, 1), ((0, 0), (0, 0), (0, HD - HEADS)))
    cw = jnp.pad(conv_full, ((0, 0), (0, 4), (0, 0)))
    lane_pad = lambda a: jnp.pad(a, ((0, 0), (4, HD - 8)))
    gv = jnp.pad(jnp.stack([lane_pad(a_log), lane_pad(dt_bias), gdn_norm_g], axis=1), ((0, 0), (0, 5), (0, 0)))

    acts = []
    xl = x0
    for l in range(DEPTH):
        win = full_w_in(wbufs[l][0])
        p, h1 = _fwd_in(xl, modv, win, l)
        nxt = wbufs[l][1:] + (wbufs[l + 1][:1] if l + 1 < DEPTH else [])
        mix, *rest = _mixer_forward(p, lnp, sgu_w, sgu_bt, cw, gv, l, gather=nxt)
        saved = rest[:6]
        wbufs[l][1:] = rest[6:9]
        if l + 1 < DEPTH:
            wbufs[l + 1][:1] = rest[9:]
        g_in, g_out, g_ff1, g_ff2 = wbufs[l]
        x1, r, h2 = _fwd_out_ff1(xl, mix, modv, g_out, g_ff1, l)
        x2 = _fwd_ff2(x1, r, modv, g_ff2, l)
        acts.append((xl, p, mix, saved, x1, r, h1, h2, win))
        xl = x2

    dx, head_stats = _loss_head(xl, target, final_g)
    loss = lax.psum(jnp.sum(head_stats[1, 0:1]), ("x", "y", "c"))
    d_final_g = head_stats[0]
    names = ("in", "out", "ff1", "ff2")
    grads_buf = [None] * len(names)

    def pair_sums(partials, from_sib, kinds, lay):
        return [(lay, n, _pair_sum(g, ga, place, f"pair_sum_{n}_{lay}")) for g, ga, n in zip(partials, from_sib, kinds)]

    def reduce_into_buffers(items, recv):
        for (lay, n, pair), rc in zip(items, recv):
            i = names.index(n)
            grads_buf[i] = _chip_sum(pair, rc, grads_buf[i], lay, place, f"chip_sum_{n}_{lay}")

    pending = []

    dmod, small = [None] * DEPTH, [None] * DEPTH
    for l in reversed(range(DEPTH)):
        xl, p, mix, saved, x1, r, h1, h2, win = acts[l]
        g_in, g_out, g_ff1, g_ff2 = wbufs[l]
        df = _bwd_ff2(dx, r, modv, g_ff2, l)
        gw_ff2, dg2 = _grad_weight(r, dx, modv, l, "ff2", g_ff2)
        gw_ff1, _ = _grad_weight(h2, df, modv, l, "ff1")
        dx1, st2, dmix, *sib_ff = _bwd_norm(df, g_ff1, x1, dx, modv, l, "ff1", send=[gw_ff1, gw_ff2], w_out=g_out)
        gw_out, dg1 = _grad_weight(mix, dx1, modv, l, "out", g_out)
        sib_out = _grads_pair_send([gw_out], f"out_{l}")
        pending = pending + pair_sums([gw_out, gw_ff1, gw_ff2], list(sib_out) + sib_ff, names[1:], l)
        dp, dlnp, dsguw, dsgub, dcw, dgv, *recv = _mixer_backward(p, dmix, saved, lnp, sgu_w, sgu_bt, cw, gv, l,
                                                                  exchange=[item[2] for item in pending])
        reduce_into_buffers(pending, recv)
        gw_in, _ = _grad_weight(h1, dp, modv, l, "in")
        gw_in_c = gw_in[:, :IN_W].reshape(D, N_CHIPS, IN_W // N_CHIPS).transpose(1, 0, 2)
        dx, st1, *sib_in = _bwd_norm(dp, win, xl, dx1, modv, l, "in", send=[gw_in_c])
        pending = pair_sums([gw_in_c], sib_in, names[:1], l)
        dmod[l] = jnp.stack([st1[0], st1[1], dg1[0], st2[0], st2[1], dg2[0]], axis=0)
        small[l] = dict(norm1_g=st1[2], norm2_g=st2[2], sgu_ln_g=dlnp[0], sgu_ln_b=dlnp[1], sgu_w=dsguw,
                        sgu_b=dsgub[:, :HEADS].T, conv_w=dcw[:4], a_log=dgv[0, 4:8], dt_bias=dgv[1, 4:8],
                        gdn_norm_g=dgv[2])
    grad_x = dx.reshape(1, t, D)

    stack = lambda k: jnp.stack([small[l][k] for l in range(DEPTH)], axis=0)
    small_grads = [jnp.zeros((DEPTH, 6 * D), F32), stack("norm1_g"), stack("norm2_g"), d_final_g, stack("sgu_ln_g"),
                   stack("sgu_ln_b"), stack("sgu_w"), stack("sgu_b"), stack("conv_w"), stack("a_log"),
                   stack("dt_bias"), stack("gdn_norm_g")]
    own = jnp.concatenate([jnp.stack(dmod, axis=0).reshape(_DMOD_ROWS, D), _pack_rows(small_grads)], axis=0)
    sib = _pair_exchange(own, "small_pair")
    recv, gathered = _grads_chip_exchange([item[2] for item in pending], 0, _pair_combine(own, sib))
    reduce_into_buffers(pending, recv)
    small_shapes = dict(b_ada=b_ada.shape, norm1_g=norm1_g.shape, norm2_g=norm2_g.shape, final_g=final_g.shape,
                        sgu_ln_g=sgu_ln_g.shape, sgu_ln_b=sgu_ln_b.shape, sgu_w=sgu_w.shape, sgu_b=sgu_b.shape,
                        conv_w=(DEPTH, 4, 1536), a_log=a_log.shape, dt_bias=dt_bias.shape,
                        gdn_norm_g=gdn_norm_g.shape)

    def full_conv(a):
        return lax.dynamic_update_slice_in_dim(jnp.zeros((DEPTH, 4, 1536), F32), a, chip * conv_cols, axis=2)

    def pack_state(b_, n1, n2, fg, lg, lb, sw, sb, cv, al, db, gn):
        return _pack_rows([b_, n1, n2, fg, lg, lb, sw, sb, full_conv(cv), al, db, gn])

    w_small = pack_state(b_ada, norm1_g, norm2_g, final_g, sgu_ln_g, sgu_ln_b, sgu_w, sgu_b, conv_w, a_log, dt_bias,
                         gdn_norm_g)
    m_small = pack_state(m_b_ada, m_norm1_g, m_norm2_g, m_final_g, m_sgu_ln_g, m_sgu_ln_b, m_sgu_w, m_sgu_b, m_conv_w,
                         m_a_log, m_dt_bias, m_gdn_norm_g)
    v_small = pack_state(v_b_ada, v_norm1_g, v_norm2_g, v_final_g, v_sgu_ln_g, v_sgu_ln_b, v_sgu_w, v_sgu_b, v_conv_w,
                         v_a_log, v_dt_bias, v_gdn_norm_g)
    small_out = _small_finalize(gathered, w_small, m_small, v_small)
    sg, sd, sm, sv = [_unpack_rows(a, small_shapes) for a in small_out]
    for dct in (sg, sd, sm, sv):
        dct["conv_w"] = lax.dynamic_slice_in_dim(dct["conv_w"], chip * conv_cols, conv_cols, axis=2)

    dmod_all = gathered[:, :2 * _DMOD_ROWS].reshape(8, DEPTH, 6 * D)
    dmod_cols = lax.dynamic_slice_in_dim(dmod_all, chip * ada_cols, ada_cols, axis=2).transpose(1, 0, 2)
    g_ada, d_ada, nm_ada, nv_ada = _ada_backward_adamw(c_all, dmod_cols, w_ada, m_w_ada, v_w_ada)

    grads = _grads_pair_share(grads_buf)
    big = {}
    for n, g, (w, m, v) in zip(names, grads, ((w_in, m_w_in, v_w_in), (w_out, m_w_out, v_w_out),
                                              (w_ff1, m_w_ff1, v_w_ff1), (w_ff2, m_w_ff2, v_w_ff2))):
        big[n] = (g,) + tuple(_adamw(w, g, m, v, f"adamw_{n}"))

    def outs(k):
        s = (sg, sd, sm, sv)[k]
        return [(g_ada, d_ada, nm_ada, nv_ada)[k], s["b_ada"], s["norm1_g"], big["in"][k], s["sgu_ln_g"],
                s["sgu_ln_b"], s["sgu_w"], s["sgu_b"], s["conv_w"], s["a_log"], s["dt_bias"], s["gdn_norm_g"],
                big["out"][k], s["norm2_g"], big["ff1"][k], big["ff2"][k], s["final_g"]]

    return (loss, grad_x, *outs(0), *outs(1), *outs(2), *outs(3))
```

```python
import functools
import math

import jax
import jax.numpy as jnp
from jax import lax
from jax.experimental import pallas as pl
from jax.experimental.pallas import tpu as pltpu

F32 = jnp.float32
BF16 = jnp.bfloat16

DEPTH = 4
D = 1024
HEADS = 4
HD = 128
BLK = 128
IN_W = 3080
NW = 3200
GATE0 = 3072
DFF = 4096
N_CHIPS = 4
RMS_EPS = 1e-6
LN_EPS = 1e-5
QK_SCALE = HD ** -0.5
LR, B1, B2, ADAM_EPS, WD, STEP = 0.001, 0.9, 0.999, 1e-08, 0.01, 10
VMEM_LIMIT = 56 * 1024 * 1024
MESH = pl.DeviceIdType.MESH
HOPS = ((1, 0), (0, 1), (1, 1))
HI = lax.Precision.HIGHEST


def _dot(a, b):
    return jnp.dot(a.astype(BF16), b.astype(BF16), preferred_element_type=F32)


def _dot_nt(a, b):
    return lax.dot_general(a.astype(BF16), b.astype(BF16), (((1,), (1,)), ((), ())), preferred_element_type=F32)


def _dot_tn(a, b):
    return lax.dot_general(a.astype(BF16), b.astype(BF16), (((0,), (0,)), ((), ())), preferred_element_type=F32)


def _dotf(a, b):
    return jnp.dot(a, b, precision=HI, preferred_element_type=F32)


def _split(a):
    hi = a.astype(BF16)
    return hi, (a - hi.astype(F32)).astype(BF16)


def _dg3(a, b, dims, batch=((), ())):
    ah, al = _split(a)
    bh, bl = _split(b)
    f = lambda x, y: lax.dot_general(x, y, (dims, batch), preferred_element_type=F32)
    return f(ah, bh) + (f(ah, bl) + f(al, bh))


def _bmm3(a, b):
    return _dg3(a, b, ((2,), (1,)), ((0,), (0,)))


def _d3(a, b):
    return _dg3(a, b, ((1,), (0,)))


def _d3_nt(a, b):
    return _dg3(a, b, ((1,), (1,)))


def _d3_tn(a, b):
    return _dg3(a, b, ((0,), (0,)))


def _dotf_tn(a, b):
    return lax.dot_general(a, b, (((0,), (0,)), ((), ())), precision=HI, preferred_element_type=F32)


def _sigmoid(x):
    return 1.0 / (1.0 + jnp.exp(-x))


def _softplus(x):
    return jnp.maximum(x, 0.0) + jnp.log(1.0 + jnp.exp(-jnp.abs(x)))


_G0 = math.sqrt(2.0 / math.pi)
_G1 = 0.044715


def _gelu(x):
    t = jnp.tanh(_G0 * (x + _G1 * x * x * x))
    return 0.5 * x * (1.0 + t)


def _gelu_grad(x):
    t = jnp.tanh(_G0 * (x + _G1 * x * x * x))
    return 0.5 * (1.0 + t) + 0.5 * x * (1.0 - t * t) * (_G0 * (1.0 + 3.0 * _G1 * x * x))


def _silu(x):
    return x * _sigmoid(x)


def _silu_grad(x):
    s = _sigmoid(x)
    return s * (1.0 + x * (1.0 - s))


def _rms_stats(x):
    rstd = lax.rsqrt(jnp.mean(x * x, axis=-1, keepdims=True) + RMS_EPS)
    return x * rstd, rstd


def _norm_mod(x, ng, sc, sh):
    xh, _ = _rms_stats(x)
    return xh * (ng * (1.0 + sc)) + sh


def _norm_mod_bwd(dh, x, ng, sc):
    xh, rstd = _rms_stats(x)
    dsh = jnp.sum(dh, axis=0, keepdims=True)
    dsc = jnp.sum(dh * xh, axis=0, keepdims=True) * ng
    dng = jnp.sum(dh * xh, axis=0, keepdims=True) * (1.0 + sc)
    dxh = dh * (ng * (1.0 + sc))
    dx = rstd * (dxh - xh * jnp.mean(dxh * xh, axis=-1, keepdims=True))
    return dx, dsh, dsc, dng


def _iota2(shape, axis):
    return lax.broadcasted_iota(jnp.int32, shape, axis)


def _col(tile, idx):
    return jnp.sum(jnp.where(_iota2(tile.shape, 1) == idx, tile, 0.0), axis=1, keepdims=True)


def _row(tile, idx):
    return jnp.sum(jnp.where(_iota2(tile.shape, 0) == idx, tile, 0.0), axis=0, keepdims=True)


def _put_col(col, idx, width=HD):
    shape = (col.shape[0], width)
    return jnp.where(_iota2(shape, 1) == idx, jnp.broadcast_to(col, shape), 0.0)


def _tri_inverse(m):
    rows, cols = _iota2(m.shape, m.ndim - 2), _iota2(m.shape, m.ndim - 1)
    mm = _bmm3 if m.ndim == 3 else _d3
    eye = jnp.where(rows == cols, 1.0, 0.0).astype(F32)
    n = jnp.where((rows >> 3) == (cols >> 3), -m, 0.0)
    p = eye + n
    n2 = mm(n, n)
    p = p + mm(n2, p)
    n4 = mm(n2, n2)
    p = p + mm(n4, p)
    for shift in (3, 4, 5, 6):
        same_pair = (rows >> (shift + 1)) == (cols >> (shift + 1))
        below = jnp.logical_and(((rows >> shift) & 1) == 1, ((cols >> shift) & 1) == 0)
        off = jnp.where(jnp.logical_and(same_pair, below), m, 0.0)
        p = p - mm(p, mm(off, p))
    return p


def _cparams(sem=None):
    return pltpu.CompilerParams(dimension_semantics=sem, vmem_limit_bytes=VMEM_LIMIT)


def _my_place():
    return lax.axis_index("x"), lax.axis_index("y"), lax.axis_index("c")


def _hop(xi, yi, hop):
    dx, dy = hop
    return (1 - xi if dx else xi), (1 - yi if dy else yi)


def _pair_exchange(x, name):
    def body(x_ref, o_ref, ssem, rsem):
        xi, yi, ci = _my_place()
        cp = pltpu.make_async_remote_copy(x_ref, o_ref, ssem, rsem, device_id=(xi, yi, 1 - ci), device_id_type=MESH)
        cp.start()
        cp.wait()

    return pl.pallas_call(
        body, name=name, out_shape=jax.ShapeDtypeStruct(x.shape, x.dtype),
        in_specs=[pl.BlockSpec(memory_space=pltpu.VMEM)], out_specs=pl.BlockSpec(memory_space=pltpu.VMEM),
        scratch_shapes=[pltpu.SemaphoreType.DMA, pltpu.SemaphoreType.DMA],
        compiler_params=pltpu.CompilerParams(vmem_limit_bytes=VMEM_LIMIT),
    )(x)


def _allgather_start(x_ref, o_ref, ssems, rsems, lsem):
    xi, yi, ci = _my_place()
    me = 2 * xi + yi
    pltpu.make_async_copy(x_ref, o_ref.at[me], lsem).start()
    for k, hop in enumerate(HOPS):
        tx, ty = _hop(xi, yi, hop)
        pltpu.make_async_remote_copy(x_ref, o_ref.at[me], ssems.at[k], rsems.at[k],
                                     device_id=(tx, ty, ci), device_id_type=MESH).start()


def _allgather_finish(x_ref, o_ref, ssems, rsems, lsem):
    xi, yi, ci = _my_place()
    me = 2 * xi + yi
    for k, hop in enumerate(HOPS):
        tx, ty = _hop(xi, yi, hop)
        cp = pltpu.make_async_remote_copy(x_ref, o_ref.at[2 * tx + ty], ssems.at[k], rsems.at[k],
                                          device_id=(tx, ty, ci), device_id_type=MESH)
        cp.wait_recv()
        cp.wait_send()
    pltpu.make_async_copy(x_ref, o_ref.at[me], lsem).wait()


_ALLGATHER_SEMS = [pltpu.SemaphoreType.DMA((3,)), pltpu.SemaphoreType.DMA((3,)), pltpu.SemaphoreType.DMA]


def _chip_allgather(x, name):
    def body(x_ref, o_ref, ssems, rsems, lsem):
        _allgather_start(x_ref, o_ref, ssems, rsems, lsem)
        _allgather_finish(x_ref, o_ref, ssems, rsems, lsem)

    return pl.pallas_call(
        body, name=name, out_shape=jax.ShapeDtypeStruct((N_CHIPS,) + x.shape, x.dtype),
        in_specs=[pl.BlockSpec(memory_space=pltpu.VMEM)], out_specs=pl.BlockSpec(memory_space=pltpu.VMEM),
        scratch_shapes=_ALLGATHER_SEMS, compiler_params=pltpu.CompilerParams(vmem_limit_bytes=VMEM_LIMIT),
    )(x)


def _hbm_specs(n):
    return [pl.BlockSpec(memory_space=pl.ANY)] * n


def _cast_into_slot(w, l, place):
    _, r, c = w.shape
    tr = _row_tile(r)

    def body(p_ref, w_ref, o_ref):
        o_ref[...] = w_ref[...].astype(BF16)

    return pl.pallas_call(
        body, name=f"cast_slot_{r}x{c}_{l}", out_shape=jax.ShapeDtypeStruct((N_CHIPS, r, c), BF16),
        grid_spec=pltpu.PrefetchScalarGridSpec(
            num_scalar_prefetch=1, grid=(r // tr,),
            in_specs=[pl.BlockSpec((None, tr, c), lambda k, pr: (l, k, 0))],
            out_specs=pl.BlockSpec((None, tr, c), lambda k, pr: (pr[0], k, 0))),
        compiler_params=_cparams(("parallel",)),
    )(place, w)


def _halves(ref, ci):
    half = ref.shape[-2] // 2
    return pl.ds(ci * half, half), pl.ds((1 - ci) * half, half)


def _gather_copies(bufs, sems):
    s_ici, r_ici, s_d2d, r_d2d = sems
    xi, yi, ci = _my_place()
    me = 2 * xi + yi
    ici_send, ici_recv, d2d_send, d2d_recv = [], [], [], []
    for i, buf in enumerate(bufs):
        mine, sibs = _halves(buf, ci)
        for k, hop in enumerate(HOPS):
            tx, ty = _hop(xi, yi, hop)
            src = 2 * tx + ty
            ici_send.append(pltpu.make_async_remote_copy(buf.at[me, mine], buf.at[me, mine], s_ici.at[i, k],
                                                         r_ici.at[i, k], device_id=(tx, ty, ci), device_id_type=MESH))
            ici_recv.append(pltpu.make_async_remote_copy(buf.at[src, mine], buf.at[src, mine], s_ici.at[i, k],
                                                         r_ici.at[i, k], device_id=(tx, ty, ci), device_id_type=MESH))
            d2d_send.append(pltpu.make_async_remote_copy(buf.at[src, mine], buf.at[src, mine], s_d2d.at[i, k],
                                                         r_d2d.at[i, k], device_id=(xi, yi, 1 - ci), device_id_type=MESH))
            d2d_recv.append(pltpu.make_async_remote_copy(buf.at[src, sibs], buf.at[src, sibs], s_d2d.at[i, k],
                                                         r_d2d.at[i, k], device_id=(xi, yi, 1 - ci), device_id_type=MESH))
    return ici_send, ici_recv, d2d_send, d2d_recv


def _gather_start(bufs, sems):
    for cp in _gather_copies(bufs, sems)[0]:
        cp.start()


def _gather_forward(bufs, sems):
    _, ici_recv, d2d_send, _ = _gather_copies(bufs, sems)
    for arrived, forward in zip(ici_recv, d2d_send):
        arrived.wait_recv()
        forward.start()


def _gather_finish(bufs, sems):
    ici_send, _, d2d_send, d2d_recv = _gather_copies(bufs, sems)
    for cp in d2d_recv:
        cp.wait_recv()
    for cp in ici_send + d2d_send:
        cp.wait_send()


def _gather_sems(n):
    return [pltpu.SemaphoreType.DMA((n, 3))] * 4


def _weights_allgather(bufs, l):
    n = len(bufs)

    def body(*refs):
        outs, sems = refs[n:2 * n], refs[2 * n:]
        _gather_start(outs, sems)
        _gather_forward(outs, sems)
        _gather_finish(outs, sems)

    return pl.pallas_call(
        body, name=f"weights_allgather_{l}",
        out_shape=[jax.ShapeDtypeStruct(b.shape, b.dtype) for b in bufs],
        in_specs=_hbm_specs(n), out_specs=_hbm_specs(n), input_output_aliases={i: i for i in range(n)},
        scratch_shapes=_gather_sems(n),
    )(*bufs)


def _pair_send_copies(gs, outs, sems):
    ssem, rsem = sems
    xi, yi, ci = _my_place()
    every = pl.ds(0, N_CHIPS)
    return [pltpu.make_async_remote_copy(g.at[every, _halves(g, ci)[1]], o, ssem.at[i], rsem.at[i],
                                         device_id=(xi, yi, 1 - ci), device_id_type=MESH)
            for i, (g, o) in enumerate(zip(gs, outs))]


def _pair_send_shapes(gs):
    return [jax.ShapeDtypeStruct((N_CHIPS, g.shape[1] // 2, g.shape[2]), g.dtype) for g in gs]


def _pair_send_sems(n):
    return [pltpu.SemaphoreType.DMA((n,)), pltpu.SemaphoreType.DMA((n,))]


def _grads_pair_send(gs, l):
    n = len(gs)

    def body(*refs):
        cps = _pair_send_copies(refs[:n], refs[n:2 * n], refs[2 * n:])
        for cp in cps:
            cp.start()
        for cp in cps:
            cp.wait()

    return pl.pallas_call(
        body, name=f"grads_pair_send_{l}", out_shape=_pair_send_shapes(gs),
        in_specs=_hbm_specs(n), out_specs=_hbm_specs(n), scratch_shapes=_pair_send_sems(n),
    )(*gs)


def _exchange_copies(ps, recvs, sems):
    ssems, rsems = sems
    xi, yi, ci = _my_place()
    cps = []
    for i, (p, rc) in enumerate(zip(ps, recvs)):
        for k, hop in enumerate(HOPS):
            tx, ty = _hop(xi, yi, hop)
            cps.append(pltpu.make_async_remote_copy(p.at[2 * tx + ty], rc.at[k], ssems.at[i, k], rsems.at[i, k],
                                                    device_id=(tx, ty, ci), device_id_type=MESH))
    return cps


def _exchange_sems(n):
    return [pltpu.SemaphoreType.DMA((n, 3))] * 2


def _exchange_shapes(ps):
    return [jax.ShapeDtypeStruct((3,) + p.shape[1:], p.dtype) for p in ps]


def _pair_share_copies(bufs, sems):
    ssem, rsem = sems
    xi, yi, ci = _my_place()
    every = pl.ds(0, DEPTH)
    sends, arrivals = [], []
    for i, buf in enumerate(bufs):
        mine, sibs = _halves(buf, ci)
        sends.append(pltpu.make_async_remote_copy(buf.at[every, mine], buf.at[every, mine], ssem.at[i], rsem.at[i],
                                                  device_id=(xi, yi, 1 - ci), device_id_type=MESH))
        arrivals.append(pltpu.make_async_remote_copy(buf.at[every, sibs], buf.at[every, sibs], ssem.at[i], rsem.at[i],
                                                     device_id=(xi, yi, 1 - ci), device_id_type=MESH))
    return sends, arrivals


def _pair_share_start(bufs, sems):
    for cp in _pair_share_copies(bufs, sems)[0]:
        cp.start()


def _pair_share_finish(bufs, sems):
    sends, arrivals = _pair_share_copies(bufs, sems)
    for cp in arrivals:
        cp.wait_recv()
    for cp in sends:
        cp.wait_send()


def _grads_pair_share(gs, name):
    n = len(gs)

    def body(*refs):
        _pair_share_start(refs[n:2 * n], refs[2 * n:])
        _pair_share_finish(refs[n:2 * n], refs[2 * n:])

    return pl.pallas_call(
        body, name=name,
        out_shape=[jax.ShapeDtypeStruct(g.shape, g.dtype) for g in gs],
        in_specs=_hbm_specs(n), out_specs=_hbm_specs(n), input_output_aliases={i: i for i in range(n)},
        scratch_shapes=_pair_send_sems(n),
    )(*gs)


def _row_tile(r):
    return min(r, 512)


def _pair_sum(g, ga, place, name):
    _, r, c = g.shape
    tr = _row_tile(r // 2)
    nk = r // 2 // tr

    def body(p_ref, g_ref, ga_ref, o_ref):
        o_ref[...] = (g_ref[...].astype(F32) + ga_ref[...].astype(F32)).astype(o_ref.dtype)

    return pl.pallas_call(
        body, name=name, out_shape=jax.ShapeDtypeStruct(ga.shape, ga.dtype),
        grid_spec=pltpu.PrefetchScalarGridSpec(
            num_scalar_prefetch=1, grid=(N_CHIPS, nk),
            in_specs=[pl.BlockSpec((None, tr, c), lambda j, k, pr: (j, pr[1] * nk + k, 0)),
                      pl.BlockSpec((None, tr, c), lambda j, k, pr: (j, k, 0))],
            out_specs=pl.BlockSpec((None, tr, c), lambda j, k, pr: (j, k, 0))),
        compiler_params=_cparams(("parallel", "parallel")),
    )(place, g, ga)


def _chip_sum(pair, recv, buf, l, place, name):
    _, rh, c = pair.shape
    tr = _row_tile(rh)
    nk = rh // tr

    def body(p_ref, own_ref, r_ref, *rest):
        o_ref = rest[-1]
        acc = own_ref[...].astype(F32) + r_ref[0].astype(F32)
        acc = acc + r_ref[1].astype(F32)
        o_ref[...] = acc + r_ref[2].astype(F32)

    in_specs = [pl.BlockSpec((None, tr, c), lambda k, pr: (pr[0], k, 0)),
                pl.BlockSpec((3, tr, c), lambda k, pr: (0, k, 0))]
    args = [pair, recv]
    aliases = {}
    if buf is not None:
        in_specs.append(pl.BlockSpec(memory_space=pl.ANY))
        args.append(buf)
        aliases = {3: 0}
    return pl.pallas_call(
        body, name=name, out_shape=jax.ShapeDtypeStruct((DEPTH, 2 * rh, c), F32),
        grid_spec=pltpu.PrefetchScalarGridSpec(
            num_scalar_prefetch=1, grid=(nk,), in_specs=in_specs,
            out_specs=pl.BlockSpec((None, tr, c), lambda k, pr: (l, pr[1] * nk + k, 0))),
        input_output_aliases=aliases, compiler_params=_cparams(("parallel",)),
    )(place, *args)


def _adam_math(w, g, m, v):
    m = B1 * m + (1.0 - B1) * g
    v = B2 * v + (1.0 - B2) * (g * g)
    m_hat = m / (1.0 - B1 ** STEP)
    v_hat = v / (1.0 - B2 ** STEP)
    delta = -LR * (m_hat / (jnp.sqrt(v_hat) + ADAM_EPS) + WD * w)
    return delta, m, v


def _adamw(w, g, m, v, name):
    n_l, r, c = w.shape
    tr = _row_tile(r)

    def body(w_ref, g_ref, m_ref, v_ref, d_ref, nm_ref, nv_ref):
        d_ref[...], nm_ref[...], nv_ref[...] = _adam_math(w_ref[...], g_ref[...], m_ref[...], v_ref[...])

    spec = pl.BlockSpec((None, tr, c), lambda i, k: (i, k, 0))
    return pl.pallas_call(
        body, name=name, out_shape=[jax.ShapeDtypeStruct(w.shape, F32)] * 3, grid=(n_l, r // tr),
        in_specs=[spec] * 4, out_specs=[spec] * 3, compiler_params=_cparams(("parallel", "parallel")),
    )(w, g, m, v)


def _adamw_hosting(sets, pairs, small):
    nsets, ne = len(sets), len(pairs)
    steps = 4

    def body(*refs):
        ins = refs[:4 * nsets]
        ex_in, x_ref = refs[4 * nsets:4 * nsets + ne], refs[4 * nsets + ne]
        o0 = 4 * nsets + ne + 1
        outs = refs[o0:o0 + 3 * nsets]
        ex_out, o_ref = refs[o0 + 3 * nsets:o0 + 3 * nsets + ne], refs[o0 + 3 * nsets + ne]
        sems = refs[o0 + 3 * nsets + ne + 1:]
        esems, asems = sems[:2], sems[2:]
        first = jnp.logical_and(pl.program_id(0) == 0, pl.program_id(1) == 0)
        last = jnp.logical_and(pl.program_id(0) == DEPTH - 1, pl.program_id(1) == steps - 1)

        @pl.when(first)
        def _():
            for cp in _exchange_copies(ex_in, ex_out, esems):
                cp.start()
            _allgather_start(x_ref, o_ref, *asems)

        for s in range(nsets):
            w_ref, g_ref, m_ref, v_ref = ins[4 * s:4 * s + 4]
            outs[3 * s][...], outs[3 * s + 1][...], outs[3 * s + 2][...] = _adam_math(
                w_ref[...], g_ref[...], m_ref[...], v_ref[...])

        @pl.when(last)
        def _():
            _allgather_finish(x_ref, o_ref, *asems)
            for cp in _exchange_copies(ex_in, ex_out, esems):
                cp.wait()

    in_specs, out_specs, out_shape, args = [], [], [], []
    for w, g, m, v in sets:
        _, r, c = w.shape
        spec = pl.BlockSpec((None, r // steps, c), lambda i, k: (i, k, 0))
        in_specs += [spec] * 4
        out_specs += [spec] * 3
        out_shape += [jax.ShapeDtypeStruct(w.shape, F32)] * 3
        args += [w, g, m, v]
    res = pl.pallas_call(
        body, name="adamw_hosting", grid=(DEPTH, steps),
        out_shape=out_shape + _exchange_shapes(pairs) + [jax.ShapeDtypeStruct((N_CHIPS,) + small.shape, small.dtype)],
        in_specs=in_specs + _hbm_specs(ne + 1), out_specs=out_specs + _hbm_specs(ne + 1),
        scratch_shapes=_exchange_sems(ne) + _ALLGATHER_SEMS,
        compiler_params=_cparams(("arbitrary", "arbitrary")),
    )(*args, *pairs, small)
    upd = [tuple(res[3 * s:3 * s + 3]) for s in range(nsets)]
    return upd, res[3 * nsets:3 * nsets + ne], res[3 * nsets + ne]


def _ada_forward(c_all, w_ada, b_cols):
    cols = w_ada.shape[2]
    tn = 512

    def body(c_ref, w_ref, b_ref, o_ref):
        o_ref[...] = _dotf(_silu(c_ref[...]), w_ref[...]) + b_ref[...]

    return pl.pallas_call(
        body, name="ada_forward", out_shape=jax.ShapeDtypeStruct((DEPTH, 8, cols), F32), grid=(DEPTH, cols // tn),
        in_specs=[pl.BlockSpec((8, D), lambda l, j: (0, 0)),
                  pl.BlockSpec((None, D, tn), lambda l, j: (l, 0, j)),
                  pl.BlockSpec((None, 1, tn), lambda l, j: (l, 0, j))],
        out_specs=pl.BlockSpec((None, 8, tn), lambda l, j: (l, 0, j)),
        compiler_params=_cparams(("parallel", "parallel")),
    )(c_all, w_ada, b_cols.reshape(DEPTH, 1, cols))


def _ada_backward_adamw(c_all, dmod_cols, w, m, v):
    cols = w.shape[2]
    tn = 512

    def body(c_ref, d_ref, w_ref, m_ref, v_ref, g_ref, dl_ref, nm_ref, nv_ref):
        g = _dotf_tn(_silu(c_ref[...]), d_ref[...])
        g_ref[...] = g
        dl_ref[...], nm_ref[...], nv_ref[...] = _adam_math(w_ref[...], g, m_ref[...], v_ref[...])

    wspec = pl.BlockSpec((None, D, tn), lambda l, j: (l, 0, j))
    return pl.pallas_call(
        body, name="ada_backward_adamw", out_shape=[jax.ShapeDtypeStruct(w.shape, F32)] * 4, grid=(DEPTH, cols // tn),
        in_specs=[pl.BlockSpec((8, D), lambda l, j: (0, 0)), pl.BlockSpec((None, 8, tn), lambda l, j: (l, 0, j)),
                  wspec, wspec, wspec],
        out_specs=[wspec] * 4, compiler_params=_cparams(("parallel", "parallel")),
    )(c_all, dmod_cols, w, m, v)


def _tok_tile(t):
    return min(t, 512)


def _wspec4(r, c, l):
    return pl.BlockSpec((N_CHIPS, r, c), lambda i: (0, 0, 0))


def _fwd_in(x, modv, w_in, l):
    t = x.shape[0]
    tm = _tok_tile(t)

    def body(x_ref, mod_ref, w_ref, o_ref, h_ref):
        h = _norm_mod(x_ref[...], mod_ref[6:7, :], mod_ref[1:2, :], mod_ref[0:1, :]).astype(BF16)
        h_ref[...] = h
        o_ref[...] = jnp.dot(h, w_ref[...], preferred_element_type=F32)

    return pl.pallas_call(
        body, name=f"fwd_in_{l}", grid=(t // tm,),
        out_shape=[jax.ShapeDtypeStruct((t, NW), F32), jax.ShapeDtypeStruct((t, D), BF16)],
        in_specs=[pl.BlockSpec((tm, D), lambda i: (i, 0)), pl.BlockSpec((None, 8, D), lambda i: (l, 0, 0)),
                  pl.BlockSpec((D, NW), lambda i: (0, 0))],
        out_specs=[pl.BlockSpec((tm, NW), lambda i: (i, 0)), pl.BlockSpec((tm, D), lambda i: (i, 0))],
        compiler_params=_cparams(("parallel",)),
    )(x, modv, w_in)


def _fwd_out_ff1(x, mix, modv, w_out, w_ff1, l):
    t = x.shape[0]
    tm = _tok_tile(t)

    def body(x_ref, mix_ref, mod_ref, wo_ref, w_ref, x1_ref, o_ref, h_ref):
        x1 = x_ref[...] + mod_ref[2:3, :] * jnp.dot(mix_ref[...], wo_ref[...].reshape(D, D), preferred_element_type=F32)
        x1_ref[...] = x1
        h = _norm_mod(x1, mod_ref[7:8, :], mod_ref[4:5, :], mod_ref[3:4, :]).astype(BF16)
        h_ref[...] = h
        for j in range(N_CHIPS):
            f = jnp.dot(h, w_ref[j], preferred_element_type=F32)
            o_ref[:, j * D:(j + 1) * D] = jnp.maximum(f, 0.0).astype(BF16)

    tok = pl.BlockSpec((tm, D), lambda i: (i, 0))
    return pl.pallas_call(
        body, name=f"fwd_out_ff1_{l}", grid=(t // tm,),
        out_shape=[jax.ShapeDtypeStruct((t, D), F32), jax.ShapeDtypeStruct((t, DFF), BF16),
                   jax.ShapeDtypeStruct((t, D), BF16)],
        in_specs=[tok, tok, pl.BlockSpec((None, 8, D), lambda i: (l, 0, 0)), _wspec4(D // N_CHIPS, D, l),
                  _wspec4(D, D, l)],
        out_specs=[tok, pl.BlockSpec((tm, DFF), lambda i: (i, 0)), tok],
        compiler_params=_cparams(("parallel",)),
    )(x, mix, modv, w_out, w_ff1)


def _fwd_ff2(x, r, modv, w_ff2, l):
    t = x.shape[0]
    tm = _tok_tile(t)

    def body(x_ref, r_ref, mod_ref, w_ref, o_ref):
        acc = jnp.zeros((tm, D), F32)
        for j in range(N_CHIPS):
            rj = r_ref[:, j * D:(j + 1) * D].astype(F32)
            acc = acc + jnp.dot((rj * rj).astype(BF16), w_ref[j], preferred_element_type=F32)
        o_ref[...] = x_ref[...] + mod_ref[5:6, :] * acc

    return pl.pallas_call(
        body, name=f"fwd_ff2_{l}", out_shape=jax.ShapeDtypeStruct((t, D), F32), grid=(t // tm,),
        in_specs=[pl.BlockSpec((tm, D), lambda i: (i, 0)), pl.BlockSpec((tm, DFF), lambda i: (i, 0)),
                  pl.BlockSpec((None, 8, D), lambda i: (l, 0, 0)), _wspec4(D, D, l)],
        out_specs=pl.BlockSpec((tm, D), lambda i: (i, 0)), compiler_params=_cparams(("parallel",)),
    )(x, r, modv, w_ff2)


def _loss_head(x, target, final_g):
    t = x.shape[0]
    tm = _tok_tile(t)

    def body(x_ref, t_ref, g_ref, dx_ref, st_ref):
        @pl.when(pl.program_id(0) == 0)
        def _():
            st_ref[...] = jnp.zeros_like(st_ref)

        xh, rstd = _rms_stats(x_ref[...])
        g = g_ref[...]
        err = xh * g - t_ref[...]
        loss = 0.5 * jnp.sum(jnp.mean(err * err, axis=-1, keepdims=True), axis=0, keepdims=True)
        dy = err * (1.0 / D)
        st_ref[0:1, :] += jnp.sum(dy * xh, axis=0, keepdims=True)
        st_ref[1:2, :] += jnp.broadcast_to(loss, (1, D))
        dxh = dy * g
        dx_ref[...] = rstd * (dxh - xh * jnp.mean(dxh * xh, axis=-1, keepdims=True))

    return pl.pallas_call(
        body, name="loss_head", out_shape=[jax.ShapeDtypeStruct((t, D), F32), jax.ShapeDtypeStruct((8, D), F32)],
        grid=(t // tm,),
        in_specs=[pl.BlockSpec((tm, D), lambda i: (i, 0)), pl.BlockSpec((tm, D), lambda i: (i, 0)),
                  pl.BlockSpec((1, D), lambda i: (0, 0))],
        out_specs=[pl.BlockSpec((tm, D), lambda i: (i, 0)), pl.BlockSpec((8, D), lambda i: (0, 0))],
        compiler_params=_cparams(("arbitrary",)),
    )(x, target, final_g.reshape(1, D))


def _bwd_ff2(dx2, r, modv, w_ff2, l):
    t = dx2.shape[0]
    tm = _tok_tile(t)

    def body(d_ref, r_ref, mod_ref, w_ref, o_ref):
        dyg = (d_ref[...] * mod_ref[5:6, :]).astype(BF16)
        for j in range(N_CHIPS):
            da = lax.dot_general(dyg, w_ref[j], (((1,), (1,)), ((), ())), preferred_element_type=F32)
            o_ref[:, j * D:(j + 1) * D] = (da * 2.0 * r_ref[:, j * D:(j + 1) * D].astype(F32)).astype(BF16)

    return pl.pallas_call(
        body, name=f"bwd_ff2_{l}", out_shape=jax.ShapeDtypeStruct((t, DFF), BF16), grid=(t // tm,),
        in_specs=[pl.BlockSpec((tm, D), lambda i: (i, 0)), pl.BlockSpec((tm, DFF), lambda i: (i, 0)),
                  pl.BlockSpec((None, 8, D), lambda i: (l, 0, 0)), _wspec4(D, D, l)],
        out_specs=pl.BlockSpec((tm, DFF), lambda i: (i, 0)), compiler_params=_cparams(("parallel",)),
    )(dx2, r, modv, w_ff2)


def _bwd_norm(dy, w, x, dres, modv, l, which, send=(), w_out=None, share=()):
    t = x.shape[0]
    tm = _tok_tile(t)
    nsteps = t // tm
    rows = (6, 1) if which == "in" else (7, 4)
    width = dy.shape[1]
    ns, nh = len(send), len(share)
    nb_ = 0 if w_out is None else 1
    n_in = 5 + nb_ + ns + nh

    def body(*refs):
        dy_ref, w_ref, x_ref, dr_ref, mod_ref = refs[:5]
        wo_ref = refs[5] if nb_ else None
        parts = refs[5 + nb_:5 + nb_ + ns]
        dx_ref, st_ref = refs[n_in:n_in + 2]
        dmix_ref = refs[n_in + 2] if nb_ else None
        from_sib = refs[n_in + 2 + nb_:n_in + 2 + nb_ + ns]
        shared = refs[n_in + 2 + nb_ + ns:n_in + 2 + nb_ + ns + nh]
        scratch = refs[n_in + 2 + nb_ + ns + nh:]
        sems, hsems = scratch[:2 if ns else 0], scratch[2 if ns else 0:]

        @pl.when(pl.program_id(0) == 0)
        def _():
            st_ref[...] = jnp.zeros_like(st_ref)
            if ns:
                for cp in _pair_send_copies(parts, from_sib, sems):
                    cp.start()
            if nh:
                _pair_share_start(shared, hsems)

        if which == "in":
            dh = lax.dot_general(dy_ref[...], w_ref[...], (((1,), (1,)), ((), ())), preferred_element_type=F32)
        else:
            dh = jnp.zeros((tm, D), F32)
            for j in range(N_CHIPS):
                dh = dh + lax.dot_general(dy_ref[:, j * D:(j + 1) * D], w_ref[j], (((1,), (1,)), ((), ())),
                                          preferred_element_type=F32)
        ng, sc = mod_ref[rows[0]:rows[0] + 1, :], mod_ref[rows[1]:rows[1] + 1, :]
        dx, dsh, dsc, dng = _norm_mod_bwd(dh, x_ref[...], ng, sc)
        dx_new = dr_ref[...] + dx
        dx_ref[...] = dx_new
        st_ref[0:1, :] += dsh
        st_ref[1:2, :] += dsc
        st_ref[2:3, :] += dng
        if nb_:
            dyg = (dx_new * mod_ref[2:3, :]).astype(BF16)
            dmix_ref[...] = lax.dot_general(dyg, wo_ref[...].reshape(D, D), (((1,), (1,)), ((), ())),
                                            preferred_element_type=F32).astype(BF16)

        if ns:
            @pl.when(pl.program_id(0) == nsteps - 1)
            def _():
                for cp in _pair_send_copies(parts, from_sib, sems):
                    cp.wait()

        if nh:
            @pl.when(pl.program_id(0) == nsteps - 1)
            def _():
                _pair_share_finish(shared, hsems)

    tok = pl.BlockSpec((tm, D), lambda i: (i, 0))
    wspec = pl.BlockSpec((D, NW), lambda i: (0, 0)) if which == "in" else _wspec4(D, D, l)
    return pl.pallas_call(
        body, name=f"bwd_norm_{which}_{l}",
        out_shape=[jax.ShapeDtypeStruct((t, D), F32), jax.ShapeDtypeStruct((8, D), F32)]
        + [jax.ShapeDtypeStruct((t, D), BF16)] * nb_ + _pair_send_shapes(send)
        + [jax.ShapeDtypeStruct(g.shape, g.dtype) for g in share],
        grid=(nsteps,),
        in_specs=[pl.BlockSpec((tm, width), lambda i: (i, 0)), wspec, tok, tok,
                  pl.BlockSpec((None, 8, D), lambda i: (l, 0, 0))]
        + [_wspec4(D // N_CHIPS, D, l)] * nb_ + _hbm_specs(ns + nh),
        out_specs=[tok, pl.BlockSpec((8, D), lambda i: (0, 0))] + [tok] * nb_ + _hbm_specs(ns + nh),
        input_output_aliases={5 + nb_ + ns + i: 2 + nb_ + ns + i for i in range(nh)},
        scratch_shapes=(_pair_send_sems(ns) if ns else []) + (_pair_send_sems(nh) if nh else []),
        compiler_params=_cparams(("arbitrary",)),
    )(dy, w, x, dres, modv, *([w_out] * nb_), *send, *share)


def _grad_weight(lhs, rhs, modv, l, which, w_gate=None):
    t = lhs.shape[0]
    tm = min(t, 2048)
    nt = t // tm
    gated = which in ("out", "ff2")
    if which == "in":
        nj, lw, rw, orows, ocols = 5, D, NW // 5, D, NW // 5
    elif which == "ff1":
        nj, lw, rw, orows, ocols = N_CHIPS, D, D, D, D
    elif which == "out":
        nj, lw, rw, orows, ocols = N_CHIPS, D // N_CHIPS, D, D // N_CHIPS, D
    else:
        nj, lw, rw, orows, ocols = N_CHIPS, D, D, D, D
    gate_row = 2 if which == "out" else 5

    def body(*refs):
        if gated:
            l_ref, r_ref, mod_ref, wg_ref, o_ref, dg_ref, acc = refs
        else:
            l_ref, r_ref, mod_ref, o_ref, acc = refs
        j, k = pl.program_id(0), pl.program_id(1)

        @pl.when(k == 0)
        def _():
            acc[...] = jnp.zeros_like(acc)

        if which == "ff2":
            lv = l_ref[...].astype(F32)
            lv = lv * lv
        else:
            lv = l_ref[...]
        acc[...] += _dot_tn(lv, r_ref[...])

        if gated:
            @pl.when(jnp.logical_and(j == 0, k == 0))
            def _():
                dg_ref[...] = jnp.zeros_like(dg_ref)

        @pl.when(k == nt - 1)
        def _():
            raw = acc[...]
            if gated:
                o_ref[...] = (raw * mod_ref[gate_row:gate_row + 1, :]).astype(o_ref.dtype)
                dg_ref[0:1, :] += jnp.sum(raw * wg_ref[...].astype(F32), axis=0, keepdims=True)
            else:
                o_ref[...] = raw.astype(o_ref.dtype)

    if which in ("in", "ff1"):
        lspec = pl.BlockSpec((tm, lw), lambda j, k: (k, 0))
        rspec = pl.BlockSpec((tm, rw), lambda j, k: (k, j))
    else:
        lspec = pl.BlockSpec((tm, lw), lambda j, k: (k, j))
        rspec = pl.BlockSpec((tm, rw), lambda j, k: (k, 0))
    mspec = pl.BlockSpec((None, 8, D), lambda j, k: (l, 0, 0))
    if which == "in":
        ospec = pl.BlockSpec((orows, ocols), lambda j, k: (0, j))
        out_shape = [jax.ShapeDtypeStruct((D, NW), BF16)]
    else:
        ospec = pl.BlockSpec((None, orows, ocols), lambda j, k: (j, 0, 0))
        out_shape = [jax.ShapeDtypeStruct((N_CHIPS, orows, ocols), BF16)]
    in_specs = [lspec, rspec, mspec]
    args = [lhs, rhs, modv]
    out_specs = [ospec]
    if gated:
        in_specs.append(pl.BlockSpec((None, orows, ocols), lambda j, k: (j, 0, 0)))
        args.append(w_gate)
        out_specs.append(pl.BlockSpec((8, D), lambda j, k: (0, 0)))
        out_shape.append(jax.ShapeDtypeStruct((8, D), F32))
    res = pl.pallas_call(
        body, name=f"grad_w_{which}_{l}", out_shape=out_shape, grid=(nj, nt), in_specs=in_specs, out_specs=out_specs,
        scratch_shapes=[pltpu.VMEM((orows, ocols), F32)], compiler_params=_cparams(("arbitrary", "arbitrary")),
    )(*args)
    return (res[0], res[1]) if gated else (res[0], None)


def _tri_masks():
    rows, cols = _iota2((BLK, BLK), 0), _iota2((BLK, BLK), 1)
    return rows >= cols, rows > cols


def _sgu_forward(p_ref, lnp_ref, sguw_ref, sgub_ref):
    incl, _ = _tri_masks()
    ug = _gelu(p_ref[:, 0:512])
    vg = _gelu(p_ref[:, 512:1024])
    mu = jnp.mean(vg, axis=-1, keepdims=True)
    xc = vg - mu
    rstd = lax.rsqrt(jnp.mean(xc * xc, axis=-1, keepdims=True) + LN_EPS)
    vhat = xc * rstd
    vn = vhat * lnp_ref[0:1, :] + lnp_ref[1:2, :]
    bias = sgub_ref[...]
    ys, mixed, wms = [], [], []
    for h in range(HEADS):
        wm = jnp.where(incl, sguw_ref[h], 0.0)
        mx = _dot(wm, vn[:, h * HD:(h + 1) * HD]) + _col(bias, h)
        ys.append(ug[:, h * HD:(h + 1) * HD] * mx)
        mixed.append(mx)
        wms.append(wm)
    return ys, ug, vhat, rstd, vn, mixed, wms


def _conv_forward(xbuf, cw_ref):
    conv = cw_ref[0:1, :] * xbuf[5:5 + BLK, :]
    for j in range(1, 4):
        conv = conv + cw_ref[j:j + 1, :] * xbuf[5 + j:5 + j + BLK, :]
    return conv


def _gates(gt, gv_ref):
    incl, _ = _tri_masks()
    beta = _sigmoid(gt)
    neg_a = -jnp.exp(gv_ref[0:1, :])
    gl = neg_a * _softplus(gt + gv_ref[1:2, :])
    gc = _dotf(jnp.where(incl, 1.0, 0.0).astype(F32), gl)
    return beta, gl, gc, gc.T, neg_a


def _head_chunk(act, beta, gc, gct, h):
    incl, strict = _tri_masks()
    qh = act[:, h * HD:(h + 1) * HD]
    kh = act[:, 512 + h * HD:512 + (h + 1) * HD]
    vh = act[:, 1024 + h * HD:1024 + (h + 1) * HD]
    rq = lax.rsqrt(jnp.sum(qh * qh, axis=-1, keepdims=True) + RMS_EPS)
    rk = lax.rsqrt(jnp.sum(kh * kh, axis=-1, keepdims=True) + RMS_EPS)
    qhat, khat = qh * rq, kh * rk
    qn = qhat * QK_SCALE
    b = _col(beta, h)
    gcol = _col(gc, 4 + h)
    grow = _row(gct, 4 + h)
    dmat = jnp.where(incl, jnp.exp(jnp.where(incl, gcol - grow, 0.0)), 0.0)
    gam = jnp.exp(gcol)
    glast = _row(gcol, BLK - 1)
    e = jnp.exp(glast - gcol)
    kk = _d3_nt(khat, khat)
    return dict(qhat=qhat, khat=khat, qn=qn, vh=vh, rq=rq, rk=rk, b=b, dmat=dmat, gam=gam, glast=glast, e=e, kk=kk,
                strict=strict, incl=incl)


def _mixer_forward(p, lnp, sgu_w, sgu_bt, cw, gv, l, gather=()):
    t = p.shape[0]
    nb = t // BLK
    ng = len(gather)

    def body(*refs):
        p_ref, lnp_ref, sguw_ref, sgub_ref, cw_ref, gv_ref = refs[:6]
        mix_ref, s_out, t_out, u_out, w_out, o_out, conv_out = refs[6 + ng:13 + ng]
        gbufs = refs[13 + ng:13 + 2 * ng]
        s_scr, xbuf = refs[13 + 2 * ng:15 + 2 * ng]
        gsems = refs[15 + 2 * ng:]

        @pl.when(pl.program_id(0) == 0)
        def _():
            s_scr[...] = jnp.zeros_like(s_scr)
            xbuf[0:8, :] = jnp.zeros((8, 1536), F32)
            if ng:
                _gather_start(gbufs, gsems)

        ys = _sgu_forward(p_ref, lnp_ref, sguw_ref, sgub_ref)[0]
        for h in range(HEADS):
            mix_ref[:, h * HD:(h + 1) * HD] = ys[h].astype(BF16)

        xbuf[8:8 + BLK, :] = p_ref[:, 1024:2560]
        conv = _conv_forward(xbuf, cw_ref)
        conv_out[...] = conv
        act = _silu(conv)
        xbuf[0:8, :] = xbuf[BLK:BLK + 8, :]
        beta, _, gc, gct, _ = _gates(p_ref[:, GATE0:NW], gv_ref)
        chunks = [_head_chunk(act, beta, gc, gct, h) for h in range(HEADS)]
        for h, hc in enumerate(chunks):
            t_out[h] = jnp.where(hc["strict"], hc["b"] * hc["kk"] * hc["dmat"], 0.0)
        t_out[...] = _tri_inverse(t_out[...])
        for h, hc in enumerate(chunks):
            tm = t_out[h]
            u = _dot(tm, hc["b"] * hc["vh"])
            w = _dot(tm, (hc["b"] * hc["gam"]) * hc["khat"])
            qkm = _dot_nt(hc["qn"], hc["khat"]) * hc["dmat"]
            s = s_scr[h]
            wn = u - _dot(w, s)
            o = _dot(hc["qn"] * hc["gam"], s) + _dot(qkm, wn)
            s_out[h] = s
            s_scr[h] = jnp.exp(hc["glast"]) * s + _dot_tn(hc["khat"] * hc["e"], wn)
            sl = slice(h * HD, (h + 1) * HD)
            u_out[:, sl] = u
            w_out[:, sl] = w
            o_out[:, sl] = o
            on = o * lax.rsqrt(jnp.mean(o * o, axis=-1, keepdims=True) + RMS_EPS) * gv_ref[2:3, :]
            mix_ref[:, 512 + h * HD:512 + (h + 1) * HD] = (on * _silu(p_ref[:, 2560 + h * HD:2560 + (h + 1) * HD])).astype(BF16)

        if ng:
            @pl.when(pl.program_id(0) == nb - 1 - min(3, nb - 1))
            def _():
                _gather_forward(gbufs, gsems)

            @pl.when(pl.program_id(0) == nb - 1)
            def _():
                _gather_finish(gbufs, gsems)

    tok = lambda w: pl.BlockSpec((BLK, w), lambda i: (i, 0))
    st = pl.BlockSpec((None, HEADS, HD, HD), lambda i: (i, 0, 0, 0))
    return pl.pallas_call(
        body, name=f"mixer_fwd_{l}", grid=(nb,),
        out_shape=[jax.ShapeDtypeStruct((t, D), BF16), jax.ShapeDtypeStruct((nb, HEADS, HD, HD), F32),
                   jax.ShapeDtypeStruct((nb, HEADS, HD, HD), F32), jax.ShapeDtypeStruct((t, 512), F32),
                   jax.ShapeDtypeStruct((t, 512), F32), jax.ShapeDtypeStruct((t, 512), F32),
                   jax.ShapeDtypeStruct((t, 1536), F32)]
        + [jax.ShapeDtypeStruct(b.shape, b.dtype) for b in gather],
        in_specs=[tok(NW), pl.BlockSpec((None, 8, 512), lambda i: (l, 0, 0)),
                  pl.BlockSpec((None, HEADS, HD, HD), lambda i: (l, 0, 0, 0)),
                  pl.BlockSpec((None, HD, HD), lambda i: (l, 0, 0)), pl.BlockSpec((None, 8, 1536), lambda i: (l, 0, 0)),
                  pl.BlockSpec((None, 8, HD), lambda i: (l, 0, 0))] + _hbm_specs(ng),
        out_specs=[tok(D), st, st, tok(512), tok(512), tok(512), tok(1536)]
        + _hbm_specs(ng),
        input_output_aliases={6 + i: 7 + i for i in range(ng)},
        scratch_shapes=[pltpu.VMEM((HEADS, HD, HD), F32), pltpu.VMEM((BLK + 8, 1536), F32)]
        + (_gather_sems(ng) if ng else []),
        compiler_params=_cparams(("arbitrary",)),
    )(p, lnp, sgu_w, sgu_bt, cw, gv, *gather)


def _mixer_backward(p, dmix, saved, lnp, sgu_w, sgu_bt, cw, gv, l, exchange=()):
    t = p.shape[0]
    nb = t // BLK
    s_sv, t_sv, u_sv, w_sv, o_sv, conv_sv = saved
    ne = len(exchange)

    def body(*refs):
        (p_ref, dmix_ref, s_ref, t_ref, u_ref, w_ref, o_ref, conv_ref, lnp_ref, sguw_ref, sgub_ref, cw_ref,
         gv_ref) = refs[:13]
        pairs = refs[13:13 + ne]
        dp_ref, dlnp_ref, dsguw_ref, dsgub_ref, dcw_ref, dgv_ref = refs[13 + ne:19 + ne]
        recvs = refs[19 + ne:19 + 2 * ne]
        ds_scr, dcbuf = refs[19 + 2 * ne:21 + 2 * ne]
        esems = refs[21 + 2 * ne:]

        @pl.when(pl.program_id(0) == 0)
        def _():
            if ne:
                for cp in _exchange_copies(pairs, recvs, esems):
                    cp.start()
            ds_scr[...] = jnp.zeros_like(ds_scr)
            dcbuf[BLK:BLK + 8, :] = jnp.zeros((8, 1536), F32)
            dlnp_ref[...] = jnp.zeros_like(dlnp_ref)
            dsguw_ref[...] = jnp.zeros_like(dsguw_ref)
            dsgub_ref[...] = jnp.zeros_like(dsgub_ref)
            dcw_ref[...] = jnp.zeros_like(dcw_ref)
            dgv_ref[...] = jnp.zeros_like(dgv_ref)

        incl, strict = _tri_masks()
        _, ug, vhat, rstd, vn, mixed, wms = _sgu_forward(p_ref, lnp_ref, sguw_ref, sgub_ref)
        dvn_parts, dug_parts = [], []
        dbias = jnp.zeros((BLK, HD), F32)
        for h in range(HEADS):
            sl = slice(h * HD, (h + 1) * HD)
            dy = dmix_ref[:, sl].astype(F32)
            dmx = dy * ug[:, sl]
            dug_parts.append(dy * mixed[h])
            dsguw_ref[h] += jnp.where(incl, _dot_nt(dmx, vn[:, sl]), 0.0)
            dbias = dbias + _put_col(jnp.sum(dmx, axis=1, keepdims=True), h)
            dvn_parts.append(_dot_tn(wms[h], dmx))
        dsgub_ref[...] += dbias
        dvn = jnp.concatenate(dvn_parts, axis=1)
        dug = jnp.concatenate(dug_parts, axis=1)
        dlnp_ref[0:1, :] += jnp.sum(dvn * vhat, axis=0, keepdims=True)
        dlnp_ref[1:2, :] += jnp.sum(dvn, axis=0, keepdims=True)
        dvhat = dvn * lnp_ref[0:1, :]
        dvg = rstd * (dvhat - jnp.mean(dvhat, axis=-1, keepdims=True)
                      - vhat * jnp.mean(dvhat * vhat, axis=-1, keepdims=True))
        dp_ref[:, 0:512] = (dug * _gelu_grad(p_ref[:, 0:512])).astype(BF16)
        dp_ref[:, 512:1024] = (dvg * _gelu_grad(p_ref[:, 512:1024])).astype(BF16)

        conv = conv_ref[...]
        act = _silu(conv)
        gt = p_ref[:, GATE0:NW]
        beta, gl, gc, gct, neg_a = _gates(gt, gv_ref)
        gng = gv_ref[2:3, :]
        dbeta_t = jnp.zeros((BLK, HD), F32)
        dgc_t = jnp.zeros((BLK, HD), F32)
        dgng = jnp.zeros((1, HD), F32)
        for h in range(HEADS):
            sl = slice(h * HD, (h + 1) * HD)
            hc = _head_chunk(act, beta, gc, gct, h)
            b, gam, e, dmat, kk = hc["b"], hc["gam"], hc["e"], hc["dmat"], hc["kk"]
            qn, khat, vh = hc["qn"], hc["khat"], hc["vh"]
            gamlast = jnp.exp(hc["glast"])
            s, tm, u, w, o = s_ref[h], t_ref[h], u_ref[:, sl], w_ref[:, sl], o_ref[:, sl]
            ds_next = ds_scr[h]
            z = p_ref[:, 2560 + h * HD:2560 + (h + 1) * HD]
            dy = dmix_ref[:, 512 + h * HD:512 + (h + 1) * HD].astype(F32)
            ro = lax.rsqrt(jnp.mean(o * o, axis=-1, keepdims=True) + RMS_EPS)
            ohat = o * ro
            dp_ref[:, 2560 + h * HD:2560 + (h + 1) * HD] = (dy * ohat * gng * _silu_grad(z)).astype(BF16)
            don = dy * _silu(z)
            dgng = dgng + jnp.sum(don * ohat, axis=0, keepdims=True)
            dohat = don * gng
            do = ro * (dohat - ohat * jnp.mean(dohat * ohat, axis=-1, keepdims=True))
            qk_raw = _dot_nt(qn, khat)
            qkm = qk_raw * dmat
            qd, kd = qn * gam, khat * e
            wn = u - _dot(w, s)
            dwn = _dot_tn(qkm, do) + _dot(kd, ds_next)
            dqd = _dot_nt(do, s)
            dqkm = jnp.where(incl, _dot_nt(do, wn), 0.0)
            ds_scr[h] = _dot_tn(qd, do) + gamlast * ds_next - _dot_tn(w, dwn)
            dgamlast = jnp.sum(jnp.sum(ds_next * s, axis=1, keepdims=True), axis=0, keepdims=True)
            dkd = _dot_nt(wn, ds_next)
            dw = -_dot_nt(dwn, s)
            db1 = _dot_tn(tm, dwn)
            db2 = _dot_tn(tm, dw)
            dm = jnp.where(strict, -(_dot_nt(db1, u) + _dot_nt(db2, w)), 0.0)
            dbeta = (jnp.sum(dm * kk * dmat, axis=1, keepdims=True) + jnp.sum(db1 * vh, axis=1, keepdims=True)
                     + gam * jnp.sum(db2 * khat, axis=1, keepdims=True))
            dkkm = dm * b * dmat
            ddm = dm * b * kk + dqkm * qk_raw
            dgam = b * jnp.sum(db2 * khat, axis=1, keepdims=True) + jnp.sum(dqd * qn, axis=1, keepdims=True)
            g_qk = dqkm * dmat
            dqn = _dot(g_qk, khat) + dqd * gam
            dkhat = ((b * gam) * db2 + _dot_tn(g_qk, qn) + _dot(dkkm, khat) + _dot_tn(dkkm, khat) + dkd * e)
            dvh = b * db1
            rkd = jnp.sum(dkd * kd, axis=1, keepdims=True)
            emat = ddm * dmat
            dgc = (dgam * gam - rkd + jnp.sum(emat, axis=1, keepdims=True)
                   - jnp.sum(emat.T, axis=1, keepdims=True))
            last = _iota2((BLK, 1), 0) == BLK - 1
            dgc = dgc + jnp.where(last, jnp.sum(rkd, axis=0, keepdims=True) + dgamlast * gamlast, 0.0)
            dgc_t = dgc_t + _put_col(dgc, 4 + h)
            dbeta_t = dbeta_t + _put_col(dbeta, h)
            dqhat = dqn * QK_SCALE
            dq = hc["rq"] * (dqhat - hc["qhat"] * jnp.sum(dqhat * hc["qhat"], axis=-1, keepdims=True))
            dk = hc["rk"] * (dkhat - khat * jnp.sum(dkhat * khat, axis=-1, keepdims=True))
            dcbuf[0:BLK, h * HD:(h + 1) * HD] = dq
            dcbuf[0:BLK, 512 + h * HD:512 + (h + 1) * HD] = dk
            dcbuf[0:BLK, 1024 + h * HD:1024 + (h + 1) * HD] = dvh
        dgv_ref[2:3, :] += dgng
        dgl = _dotf_tn(jnp.where(incl, 1.0, 0.0).astype(F32), dgc_t)
        sig_a = _sigmoid(gt + gv_ref[1:2, :])
        d_araw = dgl * neg_a * sig_a
        dgv_ref[0:1, :] += jnp.sum(dgl * gl, axis=0, keepdims=True)
        dgv_ref[1:2, :] += jnp.sum(d_araw, axis=0, keepdims=True)
        dp_ref[:, GATE0:NW] = (dbeta_t * beta * (1.0 - beta) + d_araw).astype(BF16)
        dcbuf[0:BLK, :] = dcbuf[0:BLK, :] * _silu_grad(conv)
        xcur = p_ref[:, 1024:2560]
        dqkv = jnp.zeros((BLK, 1536), F32)
        for j in range(4):
            shifted = dcbuf[3 - j:3 - j + BLK, :]
            dqkv = dqkv + cw_ref[j:j + 1, :] * shifted
            dcw_ref[j:j + 1, :] += jnp.sum(shifted * xcur, axis=0, keepdims=True)
        dp_ref[:, 1024:2560] = dqkv.astype(BF16)
        dcbuf[BLK:BLK + 8, :] = dcbuf[0:8, :]

        if ne:
            @pl.when(pl.program_id(0) == nb - 1)
            def _():
                for cp in _exchange_copies(pairs, recvs, esems):
                    cp.wait()

    rev = lambda w: pl.BlockSpec((BLK, w), lambda i: (nb - 1 - i, 0))
    st = pl.BlockSpec((None, HEADS, HD, HD), lambda i: (nb - 1 - i, 0, 0, 0))
    fix = lambda *shape: pl.BlockSpec((None,) + shape, lambda i: (l,) + (0,) * len(shape))
    acc = lambda *shape: pl.BlockSpec(shape, lambda i: (0,) * len(shape))
    return pl.pallas_call(
        body, name=f"mixer_bwd_{l}", grid=(nb,),
        out_shape=[jax.ShapeDtypeStruct((t, NW), BF16), jax.ShapeDtypeStruct((8, 512), F32),
                   jax.ShapeDtypeStruct((HEADS, HD, HD), F32), jax.ShapeDtypeStruct((HD, HD), F32),
                   jax.ShapeDtypeStruct((8, 1536), F32), jax.ShapeDtypeStruct((8, HD), F32)]
        + _exchange_shapes(exchange),
        in_specs=[rev(NW), rev(D), st, st, rev(512), rev(512), rev(512),
                  rev(1536),
                  fix(8, 512), fix(HEADS, HD, HD), fix(HD, HD), fix(8, 1536), fix(8, HD)] + _hbm_specs(ne),
        out_specs=[rev(NW), acc(8, 512), acc(HEADS, HD, HD), acc(HD, HD), acc(8, 1536), acc(8, HD)]
        + _hbm_specs(ne),
        scratch_shapes=[pltpu.VMEM((HEADS, HD, HD), F32), pltpu.VMEM((BLK + 8, 1536), F32)]
        + (_exchange_sems(ne) if ne else []),
        compiler_params=_cparams(("arbitrary",)),
    )(p, dmix, s_sv, t_sv, u_sv, w_sv, o_sv, conv_sv, lnp, sgu_w, sgu_bt, cw, gv, *exchange)


_SMALL = (("b_ada", 24), ("norm1_g", 8), ("norm2_g", 8), ("final_g", 8), ("sgu_ln_g", 8), ("sgu_ln_b", 8),
          ("sgu_w", 256), ("sgu_b", 8), ("conv_w", 24), ("a_log", 8), ("dt_bias", 8), ("gdn_norm_g", 8))
_SMALL_PAD = sum(n for _, n in _SMALL)
_DMOD_ROWS = 24


def _pack_rows(parts):
    rows = []
    for (name, n), a in zip(_SMALL, parts):
        flat = a.reshape(-1).astype(F32)
        rows.append(jnp.pad(flat, (0, n * D - flat.shape[0])).reshape(n, D))
    return jnp.concatenate(rows, axis=0)


def _unpack_rows(buf, shapes):
    out, r0 = {}, 0
    for name, n in _SMALL:
        size = math.prod(shapes[name])
        out[name] = buf[r0:r0 + n].reshape(-1)[:size].reshape(shapes[name])
        r0 += n
    return out


def _pair_combine(own, sib):
    n = own.shape[0] - _DMOD_ROWS

    def body(a_ref, b_ref, o_ref):
        first = lax.axis_index("c") == 0
        a, b = a_ref[0:_DMOD_ROWS, :], b_ref[0:_DMOD_ROWS, :]
        o_ref[0:_DMOD_ROWS, :] = jnp.where(first, a, b)
        o_ref[_DMOD_ROWS:2 * _DMOD_ROWS, :] = jnp.where(first, b, a)
        o_ref[2 * _DMOD_ROWS:, :] = a_ref[_DMOD_ROWS:, :] + b_ref[_DMOD_ROWS:, :]

    return pl.pallas_call(
        body, name="small_pair_combine", out_shape=jax.ShapeDtypeStruct((2 * _DMOD_ROWS + n, D), F32),
        compiler_params=pltpu.CompilerParams(vmem_limit_bytes=VMEM_LIMIT),
    )(own, sib)


def _small_finalize(gathered, w, m, v):
    def body(g_ref, w_ref, m_ref, v_ref, go_ref, d_ref, nm_ref, nv_ref):
        sm = g_ref[0, 2 * _DMOD_ROWS:, :] + g_ref[1, 2 * _DMOD_ROWS:, :]
        sm = sm + g_ref[2, 2 * _DMOD_ROWS:, :]
        sm = sm + g_ref[3, 2 * _DMOD_ROWS:, :]
        bsum = jnp.zeros((_DMOD_ROWS, D), F32)
        for j in range(N_CHIPS):
            bsum = bsum + g_ref[j, 0:_DMOD_ROWS, :]
            bsum = bsum + g_ref[j, _DMOD_ROWS:2 * _DMOD_ROWS, :]
        go_ref[0:_DMOD_ROWS, :] = bsum
        go_ref[_DMOD_ROWS:, :] = sm[_DMOD_ROWS:, :]
        d_ref[...], nm_ref[...], nv_ref[...] = _adam_math(w_ref[...], go_ref[...], m_ref[...], v_ref[...])

    return pl.pallas_call(
        body, name="small_finalize", out_shape=[jax.ShapeDtypeStruct(w.shape, F32)] * 4,
        compiler_params=pltpu.CompilerParams(vmem_limit_bytes=VMEM_LIMIT),
    )(gathered, w, m, v)


def kernel(x, c, w_ada, b_ada, norm1_g, w_in, sgu_ln_g, sgu_ln_b, sgu_w, sgu_b, conv_w, a_log, dt_bias, gdn_norm_g, w_out, norm2_g, w_ff1, w_ff2, final_g, loss_target, m_w_ada, m_b_ada, m_norm1_g, m_w_in, m_sgu_ln_g, m_sgu_ln_b, m_sgu_w, m_sgu_b, m_conv_w, m_a_log, m_dt_bias, m_gdn_norm_g, m_w_out, m_norm2_g, m_w_ff1, m_w_ff2, m_final_g, v_w_ada, v_b_ada, v_norm1_g, v_w_in, v_sgu_ln_g, v_sgu_ln_b, v_sgu_w, v_sgu_b, v_conv_w, v_a_log, v_dt_bias, v_gdn_norm_g, v_w_out, v_norm2_g, v_w_ff1, v_w_ff2, v_final_g):
    xi, yi, ci = lax.axis_index("x"), lax.axis_index("y"), lax.axis_index("c")
    chip = 2 * xi + yi
    dev = 2 * chip + ci
    t = x.shape[1]
    x0 = x.reshape(t, D)
    target = loss_target.reshape(t, D)

    c_sib = _pair_exchange(c, "c_pair")
    c_pair = jnp.where(ci == 0, jnp.concatenate([c, c_sib], 0), jnp.concatenate([c_sib, c], 0))
    c_all = _chip_allgather(c_pair, "c_chips").reshape(8, D)
    ada_cols = w_ada.shape[2]
    b_cols = lax.dynamic_slice_in_dim(b_ada, chip * ada_cols, ada_cols, axis=1)
    mod_part = _ada_forward(c_all, w_ada, b_cols)
    conv_cols = conv_w.shape[2]
    packed = jnp.concatenate([mod_part.reshape(DEPTH * 8, ada_cols), conv_w.reshape(DEPTH, 4 * conv_cols)], axis=0)
    packed = _chip_allgather(packed, "mod_chips")
    mod_all = packed[:, :DEPTH * 8].reshape(N_CHIPS, DEPTH, 8, ada_cols)
    mod_mine = lax.dynamic_index_in_dim(mod_all, dev, axis=2, keepdims=False)
    mod = mod_mine.transpose(1, 0, 2).reshape(DEPTH, 6, D)
    modv = jnp.concatenate([mod, norm1_g[:, None, :], norm2_g[:, None, :]], axis=1)
    conv_full = packed[:, DEPTH * 8:].reshape(N_CHIPS, DEPTH, 4, conv_cols).transpose(1, 2, 0, 3).reshape(DEPTH, 4, 1536)

    place = jnp.stack([chip, ci]).astype(jnp.int32)
    wbufs = [[_cast_into_slot(w, l, place) for w in (w_in, w_out, w_ff1, w_ff2)] for l in range(DEPTH)]
    wbufs[0][:1] = _weights_allgather(wbufs[0][:1], 0)

    def full_w_in(g):
        return jnp.pad(g.transpose(1, 0, 2).reshape(D, IN_W), ((0, 0), (0, NW - IN_W)))

    lnp = jnp.pad(jnp.stack([sgu_ln_g, sgu_ln_b], axis=1), ((0, 0), (0, 6), (0, 0)))
    sgu_bt = jnp.pad(sgu_b.transpose(0, 2, 1), ((0, 0), (0, 0), (0, HD - HEADS)))
    cw = jnp.pad(conv_full, ((0, 0), (0, 4), (0, 0)))
    lane_pad = lambda a: jnp.pad(a, ((0, 0), (4, HD - 8)))
    gv = jnp.pad(jnp.stack([lane_pad(a_log), lane_pad(dt_bias), gdn_norm_g], axis=1), ((0, 0), (0, 5), (0, 0)))

    acts = []
    xl = x0
    for l in range(DEPTH):
        win = full_w_in(wbufs[l][0])
        p, h1 = _fwd_in(xl, modv, win, l)
        nxt = wbufs[l][1:] + (wbufs[l + 1][:1] if l + 1 < DEPTH else [])
        mix, *rest = _mixer_forward(p, lnp, sgu_w, sgu_bt, cw, gv, l, gather=nxt)
        saved = rest[:6]
        wbufs[l][1:] = rest[6:9]
        if l + 1 < DEPTH:
            wbufs[l + 1][:1] = rest[9:]
        g_in, g_out, g_ff1, g_ff2 = wbufs[l]
        x1, r, h2 = _fwd_out_ff1(xl, mix, modv, g_out, g_ff1, l)
        x2 = _fwd_ff2(x1, r, modv, g_ff2, l)
        acts.append((xl, p, mix, saved, x1, r, h1, h2, win))
        xl = x2

    dx, head_stats = _loss_head(xl, target, final_g)
    loss = lax.psum(jnp.sum(head_stats[1, 0:1]), ("x", "y", "c"))
    d_final_g = head_stats[0]
    names = ("in", "out", "ff1", "ff2")
    grads_buf = [None] * len(names)

    def pair_sums(partials, from_sib, kinds, lay):
        return [(lay, n, _pair_sum(g, ga, place, f"pair_sum_{n}_{lay}")) for g, ga, n in zip(partials, from_sib, kinds)]

    def reduce_into_buffers(items, recv):
        for (lay, n, pair), rc in zip(items, recv):
            i = names.index(n)
            grads_buf[i] = _chip_sum(pair, rc, grads_buf[i], lay, place, f"chip_sum_{n}_{lay}")

    pending = []

    dmod, small = [None] * DEPTH, [None] * DEPTH
    for l in reversed(range(DEPTH)):
        xl, p, mix, saved, x1, r, h1, h2, win = acts[l]
        g_in, g_out, g_ff1, g_ff2 = wbufs[l]
        df = _bwd_ff2(dx, r, modv, g_ff2, l)
        gw_ff2, dg2 = _grad_weight(r, dx, modv, l, "ff2", g_ff2)
        gw_ff1, _ = _grad_weight(h2, df, modv, l, "ff1")
        dx1, st2, dmix, *sib_ff = _bwd_norm(df, g_ff1, x1, dx, modv, l, "ff1", send=[gw_ff1, gw_ff2], w_out=g_out)
        gw_out, dg1 = _grad_weight(mix, dx1, modv, l, "out", g_out)
        sib_out = _grads_pair_send([gw_out], f"out_{l}")
        pending = pending + pair_sums([gw_out, gw_ff1, gw_ff2], list(sib_out) + sib_ff, names[1:], l)
        dp, dlnp, dsguw, dsgub, dcw, dgv, *recv = _mixer_backward(p, dmix, saved, lnp, sgu_w, sgu_bt, cw, gv, l,
                                                                  exchange=[item[2] for item in pending])
        reduce_into_buffers(pending, recv)
        gw_in, _ = _grad_weight(h1, dp, modv, l, "in")
        gw_in_c = gw_in[:, :IN_W].reshape(D, N_CHIPS, IN_W // N_CHIPS).transpose(1, 0, 2)
        share = grads_buf[1:] if l == 0 else []
        dx, st1, sib_in, *shared = _bwd_norm(dp, win, xl, dx1, modv, l, "in", send=[gw_in_c], share=share)
        if l == 0:
            grads_buf[1:] = shared
        pending = pair_sums([gw_in_c], [sib_in], names[:1], l)
        dmod[l] = jnp.stack([st1[0], st1[1], dg1[0], st2[0], st2[1], dg2[0]], axis=0)
        small[l] = dict(norm1_g=st1[2], norm2_g=st2[2], sgu_ln_g=dlnp[0], sgu_ln_b=dlnp[1], sgu_w=dsguw,
                        sgu_b=dsgub[:, :HEADS].T, conv_w=dcw[:4], a_log=dgv[0, 4:8], dt_bias=dgv[1, 4:8],
                        gdn_norm_g=dgv[2])
    grad_x = dx.reshape(1, t, D)

    stack = lambda k: jnp.stack([small[l][k] for l in range(DEPTH)], axis=0)
    small_grads = [jnp.zeros((DEPTH, 6 * D), F32), stack("norm1_g"), stack("norm2_g"), d_final_g, stack("sgu_ln_g"),
                   stack("sgu_ln_b"), stack("sgu_w"), stack("sgu_b"), stack("conv_w"), stack("a_log"),
                   stack("dt_bias"), stack("gdn_norm_g")]
    own = jnp.concatenate([jnp.stack(dmod, axis=0).reshape(_DMOD_ROWS, D), _pack_rows(small_grads)], axis=0)
    sib = _pair_exchange(own, "small_pair")
    big = {}
    sets = [(w_out, grads_buf[1], m_w_out, v_w_out), (w_ff1, grads_buf[2], m_w_ff1, v_w_ff1),
            (w_ff2, grads_buf[3], m_w_ff2, v_w_ff2)]
    upd, recv, gathered = _adamw_hosting(sets, [item[2] for item in pending], _pair_combine(own, sib))
    for n, g, u in zip(names[1:], grads_buf[1:], upd):
        big[n] = (g,) + u
    reduce_into_buffers(pending, recv)
    small_shapes = dict(b_ada=b_ada.shape, norm1_g=norm1_g.shape, norm2_g=norm2_g.shape, final_g=final_g.shape,
                        sgu_ln_g=sgu_ln_g.shape, sgu_ln_b=sgu_ln_b.shape, sgu_w=sgu_w.shape, sgu_b=sgu_b.shape,
                        conv_w=(DEPTH, 4, 1536), a_log=a_log.shape, dt_bias=dt_bias.shape,
                        gdn_norm_g=gdn_norm_g.shape)

    def full_conv(a):
        return lax.dynamic_update_slice_in_dim(jnp.zeros((DEPTH, 4, 1536), F32), a, chip * conv_cols, axis=2)

    def pack_state(b_, n1, n2, fg, lg, lb, sw, sb, cv, al, db, gn):
        return _pack_rows([b_, n1, n2, fg, lg, lb, sw, sb, full_conv(cv), al, db, gn])

    w_small = pack_state(b_ada, norm1_g, norm2_g, final_g, sgu_ln_g, sgu_ln_b, sgu_w, sgu_b, conv_w, a_log, dt_bias,
                         gdn_norm_g)
    m_small = pack_state(m_b_ada, m_norm1_g, m_norm2_g, m_final_g, m_sgu_ln_g, m_sgu_ln_b, m_sgu_w, m_sgu_b, m_conv_w,
                         m_a_log, m_dt_bias, m_gdn_norm_g)
    v_small = pack_state(v_b_ada, v_norm1_g, v_norm2_g, v_final_g, v_sgu_ln_g, v_sgu_ln_b, v_sgu_w, v_sgu_b, v_conv_w,
                         v_a_log, v_dt_bias, v_gdn_norm_g)
    small_out = _small_finalize(gathered, w_small, m_small, v_small)
    sg, sd, sm, sv = [_unpack_rows(a, small_shapes) for a in small_out]
    for dct in (sg, sd, sm, sv):
        dct["conv_w"] = lax.dynamic_slice_in_dim(dct["conv_w"], chip * conv_cols, conv_cols, axis=2)

    dmod_all = gathered[:, :2 * _DMOD_ROWS].reshape(8, DEPTH, 6 * D)
    dmod_cols = lax.dynamic_slice_in_dim(dmod_all, chip * ada_cols, ada_cols, axis=2).transpose(1, 0, 2)
    g_ada, d_ada, nm_ada, nv_ada = _ada_backward_adamw(c_all, dmod_cols, w_ada, m_w_ada, v_w_ada)

    g_in_full, = _grads_pair_share(grads_buf[:1], "grads_pair_share_in")
    big["in"] = (g_in_full,) + tuple(_adamw(w_in, g_in_full, m_w_in, v_w_in, "adamw_in"))

    def outs(k):
        s = (sg, sd, sm, sv)[k]
        return [(g_ada, d_ada, nm_ada, nv_ada)[k], s["b_ada"], s["norm1_g"], big["in"][k], s["sgu_ln_g"],
                s["sgu_ln_b"], s["sgu_w"], s["sgu_b"], s["conv_w"], s["a_log"], s["dt_bias"], s["gdn_norm_g"],
                big["out"][k], s["norm2_g"], big["ff1"][k], big["ff2"][k], s["final_g"]]

    return (loss, grad_x, *outs(0), *outs(1), *outs(2), *outs(3))
```

```python
import functools
import math

import jax
import jax.numpy as jnp
from jax import lax
from jax.experimental import pallas as pl
from jax.experimental.pallas import tpu as pltpu

F32 = jnp.float32
BF16 = jnp.bfloat16

DEPTH = 4
D = 1024
HEADS = 4
HD = 128
BLK = 128
IN_W = 3080
NW = 3200
GATE0 = 3072
DFF = 4096
N_CHIPS = 4
RMS_EPS = 1e-6
LN_EPS = 1e-5
QK_SCALE = HD ** -0.5
LR, B1, B2, ADAM_EPS, WD, STEP = 0.001, 0.9, 0.999, 1e-08, 0.01, 10
VMEM_LIMIT = 56 * 1024 * 1024
MESH = pl.DeviceIdType.MESH
HOPS = ((1, 0), (0, 1), (1, 1))
HI = lax.Precision.HIGHEST


def _dot(a, b):
    return jnp.dot(a.astype(BF16), b.astype(BF16), preferred_element_type=F32)


def _dot_nt(a, b):
    return lax.dot_general(a.astype(BF16), b.astype(BF16), (((1,), (1,)), ((), ())), preferred_element_type=F32)


def _dot_tn(a, b):
    return lax.dot_general(a.astype(BF16), b.astype(BF16), (((0,), (0,)), ((), ())), preferred_element_type=F32)


def _dotf(a, b):
    return jnp.dot(a, b, precision=HI, preferred_element_type=F32)


def _split(a):
    hi = a.astype(BF16)
    return hi, (a - hi.astype(F32)).astype(BF16)


def _dg3(a, b, dims, batch=((), ())):
    ah, al = _split(a)
    bh, bl = _split(b)
    f = lambda x, y: lax.dot_general(x, y, (dims, batch), preferred_element_type=F32)
    return f(ah, bh) + (f(ah, bl) + f(al, bh))


def _bmm3(a, b):
    return _dg3(a, b, ((2,), (1,)), ((0,), (0,)))


def _d3(a, b):
    return _dg3(a, b, ((1,), (0,)))


def _d3_nt(a, b):
    return _dg3(a, b, ((1,), (1,)))


def _d3_tn(a, b):
    return _dg3(a, b, ((0,), (0,)))


def _dotf_tn(a, b):
    return lax.dot_general(a, b, (((0,), (0,)), ((), ())), precision=HI, preferred_element_type=F32)


def _sigmoid(x):
    return 1.0 / (1.0 + jnp.exp(-x))


def _softplus(x):
    return jnp.maximum(x, 0.0) + jnp.log(1.0 + jnp.exp(-jnp.abs(x)))


_G0 = math.sqrt(2.0 / math.pi)
_G1 = 0.044715


def _gelu(x):
    t = jnp.tanh(_G0 * (x + _G1 * x * x * x))
    return 0.5 * x * (1.0 + t)


def _gelu_grad(x):
    t = jnp.tanh(_G0 * (x + _G1 * x * x * x))
    return 0.5 * (1.0 + t) + 0.5 * x * (1.0 - t * t) * (_G0 * (1.0 + 3.0 * _G1 * x * x))


def _silu(x):
    return x * _sigmoid(x)


def _silu_grad(x):
    s = _sigmoid(x)
    return s * (1.0 + x * (1.0 - s))


def _rms_stats(x):
    rstd = lax.rsqrt(jnp.mean(x * x, axis=-1, keepdims=True) + RMS_EPS)
    return x * rstd, rstd


def _norm_mod(x, ng, sc, sh):
    xh, _ = _rms_stats(x)
    return xh * (ng * (1.0 + sc)) + sh


def _norm_mod_bwd(dh, x, ng, sc):
    xh, rstd = _rms_stats(x)
    dsh = jnp.sum(dh, axis=0, keepdims=True)
    dsc = jnp.sum(dh * xh, axis=0, keepdims=True) * ng
    dng = jnp.sum(dh * xh, axis=0, keepdims=True) * (1.0 + sc)
    dxh = dh * (ng * (1.0 + sc))
    dx = rstd * (dxh - xh * jnp.mean(dxh * xh, axis=-1, keepdims=True))
    return dx, dsh, dsc, dng


def _iota2(shape, axis):
    return lax.broadcasted_iota(jnp.int32, shape, axis)


def _col(tile, idx):
    return jnp.sum(jnp.where(_iota2(tile.shape, 1) == idx, tile, 0.0), axis=1, keepdims=True)


def _row(tile, idx):
    return jnp.sum(jnp.where(_iota2(tile.shape, 0) == idx, tile, 0.0), axis=0, keepdims=True)


def _put_col(col, idx, width=HD):
    shape = (col.shape[0], width)
    return jnp.where(_iota2(shape, 1) == idx, jnp.broadcast_to(col, shape), 0.0)


def _tri_inverse(m):
    rows, cols = _iota2(m.shape, m.ndim - 2), _iota2(m.shape, m.ndim - 1)
    mm = _bmm3 if m.ndim == 3 else _d3
    eye = jnp.where(rows == cols, 1.0, 0.0).astype(F32)
    n = jnp.where((rows >> 3) == (cols >> 3), -m, 0.0)
    p = eye + n
    n2 = mm(n, n)
    p = p + mm(n2, p)
    n4 = mm(n2, n2)
    p = p + mm(n4, p)
    for shift in (3, 4, 5, 6):
        same_pair = (rows >> (shift + 1)) == (cols >> (shift + 1))
        below = jnp.logical_and(((rows >> shift) & 1) == 1, ((cols >> shift) & 1) == 0)
        off = jnp.where(jnp.logical_and(same_pair, below), m, 0.0)
        p = p - mm(p, mm(off, p))
    return p


def _cparams(sem=None):
    return pltpu.CompilerParams(dimension_semantics=sem, vmem_limit_bytes=VMEM_LIMIT)


def _my_place():
    return lax.axis_index("x"), lax.axis_index("y"), lax.axis_index("c")


def _hop(xi, yi, hop):
    dx, dy = hop
    return (1 - xi if dx else xi), (1 - yi if dy else yi)


def _pair_exchange(x, name):
    def body(x_ref, o_ref, ssem, rsem):
        xi, yi, ci = _my_place()
        cp = pltpu.make_async_remote_copy(x_ref, o_ref, ssem, rsem, device_id=(xi, yi, 1 - ci), device_id_type=MESH)
        cp.start()
        cp.wait()

    return pl.pallas_call(
        body, name=name, out_shape=jax.ShapeDtypeStruct(x.shape, x.dtype),
        in_specs=[pl.BlockSpec(memory_space=pltpu.VMEM)], out_specs=pl.BlockSpec(memory_space=pltpu.VMEM),
        scratch_shapes=[pltpu.SemaphoreType.DMA, pltpu.SemaphoreType.DMA],
        compiler_params=pltpu.CompilerParams(vmem_limit_bytes=VMEM_LIMIT),
    )(x)


def _allgather_start(x_ref, o_ref, ssems, rsems, lsem):
    xi, yi, ci = _my_place()
    me = 2 * xi + yi
    pltpu.make_async_copy(x_ref, o_ref.at[me], lsem).start()
    for k, hop in enumerate(HOPS):
        tx, ty = _hop(xi, yi, hop)
        pltpu.make_async_remote_copy(x_ref, o_ref.at[me], ssems.at[k], rsems.at[k],
                                     device_id=(tx, ty, ci), device_id_type=MESH).start()


def _allgather_finish(x_ref, o_ref, ssems, rsems, lsem):
    xi, yi, ci = _my_place()
    me = 2 * xi + yi
    for k, hop in enumerate(HOPS):
        tx, ty = _hop(xi, yi, hop)
        cp = pltpu.make_async_remote_copy(x_ref, o_ref.at[2 * tx + ty], ssems.at[k], rsems.at[k],
                                          device_id=(tx, ty, ci), device_id_type=MESH)
        cp.wait_recv()
        cp.wait_send()
    pltpu.make_async_copy(x_ref, o_ref.at[me], lsem).wait()


_ALLGATHER_SEMS = [pltpu.SemaphoreType.DMA((3,)), pltpu.SemaphoreType.DMA((3,)), pltpu.SemaphoreType.DMA]


def _chip_allgather(x, name):
    def body(x_ref, o_ref, ssems, rsems, lsem):
        _allgather_start(x_ref, o_ref, ssems, rsems, lsem)
        _allgather_finish(x_ref, o_ref, ssems, rsems, lsem)

    return pl.pallas_call(
        body, name=name, out_shape=jax.ShapeDtypeStruct((N_CHIPS,) + x.shape, x.dtype),
        in_specs=[pl.BlockSpec(memory_space=pltpu.VMEM)], out_specs=pl.BlockSpec(memory_space=pltpu.VMEM),
        scratch_shapes=_ALLGATHER_SEMS, compiler_params=pltpu.CompilerParams(vmem_limit_bytes=VMEM_LIMIT),
    )(x)


def _hbm_specs(n):
    return [pl.BlockSpec(memory_space=pl.ANY)] * n


def _cast_into_slot(w, l, place):
    _, r, c = w.shape
    tr = _row_tile(r)

    def body(p_ref, w_ref, o_ref):
        o_ref[...] = w_ref[...].astype(BF16)

    return pl.pallas_call(
        body, name=f"cast_slot_{r}x{c}_{l}", out_shape=jax.ShapeDtypeStruct((N_CHIPS, r, c), BF16),
        grid_spec=pltpu.PrefetchScalarGridSpec(
            num_scalar_prefetch=1, grid=(r // tr,),
            in_specs=[pl.BlockSpec((None, tr, c), lambda k, pr: (l, k, 0))],
            out_specs=pl.BlockSpec((None, tr, c), lambda k, pr: (pr[0], k, 0))),
        compiler_params=_cparams(("parallel",)),
    )(place, w)


def _halves(ref, ci):
    half = ref.shape[-2] // 2
    return pl.ds(ci * half, half), pl.ds((1 - ci) * half, half)


def _gather_copies(bufs, sems):
    s_ici, r_ici, s_d2d, r_d2d = sems
    xi, yi, ci = _my_place()
    me = 2 * xi + yi
    ici_send, ici_recv, d2d_send, d2d_recv = [], [], [], []
    for i, buf in enumerate(bufs):
        mine, sibs = _halves(buf, ci)
        for k, hop in enumerate(HOPS):
            tx, ty = _hop(xi, yi, hop)
            src = 2 * tx + ty
            ici_send.append(pltpu.make_async_remote_copy(buf.at[me, mine], buf.at[me, mine], s_ici.at[i, k],
                                                         r_ici.at[i, k], device_id=(tx, ty, ci), device_id_type=MESH))
            ici_recv.append(pltpu.make_async_remote_copy(buf.at[src, mine], buf.at[src, mine], s_ici.at[i, k],
                                                         r_ici.at[i, k], device_id=(tx, ty, ci), device_id_type=MESH))
            d2d_send.append(pltpu.make_async_remote_copy(buf.at[src, mine], buf.at[src, mine], s_d2d.at[i, k],
                                                         r_d2d.at[i, k], device_id=(xi, yi, 1 - ci), device_id_type=MESH))
            d2d_recv.append(pltpu.make_async_remote_copy(buf.at[src, sibs], buf.at[src, sibs], s_d2d.at[i, k],
                                                         r_d2d.at[i, k], device_id=(xi, yi, 1 - ci), device_id_type=MESH))
    return ici_send, ici_recv, d2d_send, d2d_recv


def _gather_start(bufs, sems):
    for cp in _gather_copies(bufs, sems)[0]:
        cp.start()


def _gather_forward(bufs, sems):
    _, ici_recv, d2d_send, _ = _gather_copies(bufs, sems)
    for arrived, forward in zip(ici_recv, d2d_send):
        arrived.wait_recv()
        forward.start()


def _gather_finish(bufs, sems):
    ici_send, _, d2d_send, d2d_recv = _gather_copies(bufs, sems)
    for cp in d2d_recv:
        cp.wait_recv()
    for cp in ici_send + d2d_send:
        cp.wait_send()


def _gather_sems(n):
    return [pltpu.SemaphoreType.DMA((n, 3))] * 4


def _weights_allgather(bufs, l):
    n = len(bufs)

    def body(*refs):
        outs, sems = refs[n:2 * n], refs[2 * n:]
        _gather_start(outs, sems)
        _gather_forward(outs, sems)
        _gather_finish(outs, sems)

    return pl.pallas_call(
        body, name=f"weights_allgather_{l}",
        out_shape=[jax.ShapeDtypeStruct(b.shape, b.dtype) for b in bufs],
        in_specs=_hbm_specs(n), out_specs=_hbm_specs(n), input_output_aliases={i: i for i in range(n)},
        scratch_shapes=_gather_sems(n),
    )(*bufs)


def _pair_send_copies(gs, outs, sems):
    ssem, rsem = sems
    xi, yi, ci = _my_place()
    every = pl.ds(0, N_CHIPS)
    return [pltpu.make_async_remote_copy(g.at[every, _halves(g, ci)[1]], o, ssem.at[i], rsem.at[i],
                                         device_id=(xi, yi, 1 - ci), device_id_type=MESH)
            for i, (g, o) in enumerate(zip(gs, outs))]


def _pair_send_shapes(gs):
    return [jax.ShapeDtypeStruct((N_CHIPS, g.shape[1] // 2, g.shape[2]), g.dtype) for g in gs]


def _pair_send_sems(n):
    return [pltpu.SemaphoreType.DMA((n,)), pltpu.SemaphoreType.DMA((n,))]


def _grads_pair_send(gs, l):
    n = len(gs)

    def body(*refs):
        cps = _pair_send_copies(refs[:n], refs[n:2 * n], refs[2 * n:])
        for cp in cps:
            cp.start()
        for cp in cps:
            cp.wait()

    return pl.pallas_call(
        body, name=f"grads_pair_send_{l}", out_shape=_pair_send_shapes(gs),
        in_specs=_hbm_specs(n), out_specs=_hbm_specs(n), scratch_shapes=_pair_send_sems(n),
    )(*gs)


def _exchange_copies(ps, recvs, sems):
    ssems, rsems = sems
    xi, yi, ci = _my_place()
    cps = []
    for i, (p, rc) in enumerate(zip(ps, recvs)):
        for k, hop in enumerate(HOPS):
            tx, ty = _hop(xi, yi, hop)
            cps.append(pltpu.make_async_remote_copy(p.at[2 * tx + ty], rc.at[k], ssems.at[i, k], rsems.at[i, k],
                                                    device_id=(tx, ty, ci), device_id_type=MESH))
    return cps


def _exchange_sems(n):
    return [pltpu.SemaphoreType.DMA((n, 3))] * 2


def _exchange_shapes(ps):
    return [jax.ShapeDtypeStruct((3,) + p.shape[1:], p.dtype) for p in ps]


def _pair_share_copies(bufs, sems):
    ssem, rsem = sems
    xi, yi, ci = _my_place()
    every = pl.ds(0, DEPTH)
    sends, arrivals = [], []
    for i, buf in enumerate(bufs):
        mine, sibs = _halves(buf, ci)
        sends.append(pltpu.make_async_remote_copy(buf.at[every, mine], buf.at[every, mine], ssem.at[i], rsem.at[i],
                                                  device_id=(xi, yi, 1 - ci), device_id_type=MESH))
        arrivals.append(pltpu.make_async_remote_copy(buf.at[every, sibs], buf.at[every, sibs], ssem.at[i], rsem.at[i],
                                                     device_id=(xi, yi, 1 - ci), device_id_type=MESH))
    return sends, arrivals


def _pair_share_start(bufs, sems):
    for cp in _pair_share_copies(bufs, sems)[0]:
        cp.start()


def _pair_share_finish(bufs, sems):
    sends, arrivals = _pair_share_copies(bufs, sems)
    for cp in arrivals:
        cp.wait_recv()
    for cp in sends:
        cp.wait_send()


def _grads_pair_share(gs, name):
    n = len(gs)

    def body(*refs):
        _pair_share_start(refs[n:2 * n], refs[2 * n:])
        _pair_share_finish(refs[n:2 * n], refs[2 * n:])

    return pl.pallas_call(
        body, name=name,
        out_shape=[jax.ShapeDtypeStruct(g.shape, g.dtype) for g in gs],
        in_specs=_hbm_specs(n), out_specs=_hbm_specs(n), input_output_aliases={i: i for i in range(n)},
        scratch_shapes=_pair_send_sems(n),
    )(*gs)


def _row_tile(r):
    return min(r, 512)


def _pair_sum(g, ga, place, name):
    _, r, c = g.shape
    tr = _row_tile(r // 2)
    nk = r // 2 // tr

    def body(p_ref, g_ref, ga_ref, o_ref):
        o_ref[...] = (g_ref[...].astype(F32) + ga_ref[...].astype(F32)).astype(o_ref.dtype)

    return pl.pallas_call(
        body, name=name, out_shape=jax.ShapeDtypeStruct(ga.shape, ga.dtype),
        grid_spec=pltpu.PrefetchScalarGridSpec(
            num_scalar_prefetch=1, grid=(N_CHIPS, nk),
            in_specs=[pl.BlockSpec((None, tr, c), lambda j, k, pr: (j, pr[1] * nk + k, 0)),
                      pl.BlockSpec((None, tr, c), lambda j, k, pr: (j, k, 0))],
            out_specs=pl.BlockSpec((None, tr, c), lambda j, k, pr: (j, k, 0))),
        compiler_params=_cparams(("parallel", "parallel")),
    )(place, g, ga)


def _chip_sum(pair, recv, buf, l, place, name):
    _, rh, c = pair.shape
    tr = _row_tile(rh)
    nk = rh // tr

    def body(p_ref, own_ref, r_ref, *rest):
        o_ref = rest[-1]
        acc = own_ref[...].astype(F32) + r_ref[0].astype(F32)
        acc = acc + r_ref[1].astype(F32)
        o_ref[...] = acc + r_ref[2].astype(F32)

    in_specs = [pl.BlockSpec((None, tr, c), lambda k, pr: (pr[0], k, 0)),
                pl.BlockSpec((3, tr, c), lambda k, pr: (0, k, 0))]
    args = [pair, recv]
    aliases = {}
    if buf is not None:
        in_specs.append(pl.BlockSpec(memory_space=pl.ANY))
        args.append(buf)
        aliases = {3: 0}
    return pl.pallas_call(
        body, name=name, out_shape=jax.ShapeDtypeStruct((DEPTH, 2 * rh, c), F32),
        grid_spec=pltpu.PrefetchScalarGridSpec(
            num_scalar_prefetch=1, grid=(nk,), in_specs=in_specs,
            out_specs=pl.BlockSpec((None, tr, c), lambda k, pr: (l, pr[1] * nk + k, 0))),
        input_output_aliases=aliases, compiler_params=_cparams(("parallel",)),
    )(place, *args)


def _adam_math(w, g, m, v):
    m = B1 * m + (1.0 - B1) * g
    v = B2 * v + (1.0 - B2) * (g * g)
    m_hat = m / (1.0 - B1 ** STEP)
    v_hat = v / (1.0 - B2 ** STEP)
    delta = -LR * (m_hat / (jnp.sqrt(v_hat) + ADAM_EPS) + WD * w)
    return delta, m, v


def _adamw(w, g, m, v, name):
    n_l, r, c = w.shape
    tr = _row_tile(r)

    def body(w_ref, g_ref, m_ref, v_ref, d_ref, nm_ref, nv_ref):
        d_ref[...], nm_ref[...], nv_ref[...] = _adam_math(w_ref[...], g_ref[...], m_ref[...], v_ref[...])

    spec = pl.BlockSpec((None, tr, c), lambda i, k: (i, k, 0))
    return pl.pallas_call(
        body, name=name, out_shape=[jax.ShapeDtypeStruct(w.shape, F32)] * 3, grid=(n_l, r // tr),
        in_specs=[spec] * 4, out_specs=[spec] * 3, compiler_params=_cparams(("parallel", "parallel")),
    )(w, g, m, v)


def _ada_forward(c_all, w_ada, b_cols):
    cols = w_ada.shape[2]
    tn = 512

    def body(c_ref, w_ref, b_ref, o_ref):
        o_ref[...] = _dotf(_silu(c_ref[...]), w_ref[...]) + b_ref[...]

    return pl.pallas_call(
        body, name="ada_forward", out_shape=jax.ShapeDtypeStruct((DEPTH, 8, cols), F32), grid=(DEPTH, cols // tn),
        in_specs=[pl.BlockSpec((8, D), lambda l, j: (0, 0)),
                  pl.BlockSpec((None, D, tn), lambda l, j: (l, 0, j)),
                  pl.BlockSpec((None, 1, tn), lambda l, j: (l, 0, j))],
        out_specs=pl.BlockSpec((None, 8, tn), lambda l, j: (l, 0, j)),
        compiler_params=_cparams(("parallel", "parallel")),
    )(c_all, w_ada, b_cols.reshape(DEPTH, 1, cols))


def _ada_backward_adamw(c_all, dmod_cols, w, m, v):
    cols = w.shape[2]
    tn = 512

    def body(c_ref, d_ref, w_ref, m_ref, v_ref, g_ref, dl_ref, nm_ref, nv_ref):
        g = _dotf_tn(_silu(c_ref[...]), d_ref[...])
        g_ref[...] = g
        dl_ref[...], nm_ref[...], nv_ref[...] = _adam_math(w_ref[...], g, m_ref[...], v_ref[...])

    wspec = pl.BlockSpec((None, D, tn), lambda l, j: (l, 0, j))
    return pl.pallas_call(
        body, name="ada_backward_adamw", out_shape=[jax.ShapeDtypeStruct(w.shape, F32)] * 4, grid=(DEPTH, cols // tn),
        in_specs=[pl.BlockSpec((8, D), lambda l, j: (0, 0)), pl.BlockSpec((None, 8, tn), lambda l, j: (l, 0, j)),
                  wspec, wspec, wspec],
        out_specs=[wspec] * 4, compiler_params=_cparams(("parallel", "parallel")),
    )(c_all, dmod_cols, w, m, v)


def _tok_tile(t):
    return min(t, 512)


def _wspec4(r, c, l):
    return pl.BlockSpec((N_CHIPS, r, c), lambda i: (0, 0, 0))


def _fwd_in(x, modv, w_in, l):
    t = x.shape[0]
    tm = _tok_tile(t)

    def body(x_ref, mod_ref, w_ref, o_ref, h_ref):
        h = _norm_mod(x_ref[...], mod_ref[6:7, :], mod_ref[1:2, :], mod_ref[0:1, :]).astype(BF16)
        h_ref[...] = h
        o_ref[...] = jnp.dot(h, w_ref[...], preferred_element_type=F32)

    return pl.pallas_call(
        body, name=f"fwd_in_{l}", grid=(t // tm,),
        out_shape=[jax.ShapeDtypeStruct((t, NW), F32), jax.ShapeDtypeStruct((t, D), BF16)],
        in_specs=[pl.BlockSpec((tm, D), lambda i: (i, 0)), pl.BlockSpec((None, 8, D), lambda i: (l, 0, 0)),
                  pl.BlockSpec((D, NW), lambda i: (0, 0))],
        out_specs=[pl.BlockSpec((tm, NW), lambda i: (i, 0)), pl.BlockSpec((tm, D), lambda i: (i, 0))],
        compiler_params=_cparams(("parallel",)),
    )(x, modv, w_in)


def _fwd_out_ff1(x, mix, modv, w_out, w_ff1, l):
    t = x.shape[0]
    tm = _tok_tile(t)

    def body(x_ref, mix_ref, mod_ref, wo_ref, w_ref, x1_ref, o_ref, h_ref):
        x1 = x_ref[...] + mod_ref[2:3, :] * jnp.dot(mix_ref[...], wo_ref[...].reshape(D, D), preferred_element_type=F32)
        x1_ref[...] = x1
        h = _norm_mod(x1, mod_ref[7:8, :], mod_ref[4:5, :], mod_ref[3:4, :]).astype(BF16)
        h_ref[...] = h
        for j in range(N_CHIPS):
            f = jnp.dot(h, w_ref[j], preferred_element_type=F32)
            o_ref[:, j * D:(j + 1) * D] = jnp.maximum(f, 0.0).astype(BF16)

    tok = pl.BlockSpec((tm, D), lambda i: (i, 0))
    return pl.pallas_call(
        body, name=f"fwd_out_ff1_{l}", grid=(t // tm,),
        out_shape=[jax.ShapeDtypeStruct((t, D), F32), jax.ShapeDtypeStruct((t, DFF), BF16),
                   jax.ShapeDtypeStruct((t, D), BF16)],
        in_specs=[tok, tok, pl.BlockSpec((None, 8, D), lambda i: (l, 0, 0)), _wspec4(D // N_CHIPS, D, l),
                  _wspec4(D, D, l)],
        out_specs=[tok, pl.BlockSpec((tm, DFF), lambda i: (i, 0)), tok],
        compiler_params=_cparams(("parallel",)),
    )(x, mix, modv, w_out, w_ff1)


def _fwd_ff2(x, r, modv, w_ff2, l):
    t = x.shape[0]
    tm = _tok_tile(t)

    def body(x_ref, r_ref, mod_ref, w_ref, o_ref):
        acc = jnp.zeros((tm, D), F32)
        for j in range(N_CHIPS):
            rj = r_ref[:, j * D:(j + 1) * D].astype(F32)
            acc = acc + jnp.dot((rj * rj).astype(BF16), w_ref[j], preferred_element_type=F32)
        o_ref[...] = x_ref[...] + mod_ref[5:6, :] * acc

    return pl.pallas_call(
        body, name=f"fwd_ff2_{l}", out_shape=jax.ShapeDtypeStruct((t, D), F32), grid=(t // tm,),
        in_specs=[pl.BlockSpec((tm, D), lambda i: (i, 0)), pl.BlockSpec((tm, DFF), lambda i: (i, 0)),
                  pl.BlockSpec((None, 8, D), lambda i: (l, 0, 0)), _wspec4(D, D, l)],
        out_specs=pl.BlockSpec((tm, D), lambda i: (i, 0)), compiler_params=_cparams(("parallel",)),
    )(x, r, modv, w_ff2)


def _loss_head(x, target, final_g):
    t = x.shape[0]
    tm = _tok_tile(t)

    def body(x_ref, t_ref, g_ref, dx_ref, st_ref):
        @pl.when(pl.program_id(0) == 0)
        def _():
            st_ref[...] = jnp.zeros_like(st_ref)

        xh, rstd = _rms_stats(x_ref[...])
        g = g_ref[...]
        err = xh * g - t_ref[...]
        loss = 0.5 * jnp.sum(jnp.mean(err * err, axis=-1, keepdims=True), axis=0, keepdims=True)
        dy = err * (1.0 / D)
        st_ref[0:1, :] += jnp.sum(dy * xh, axis=0, keepdims=True)
        st_ref[1:2, :] += jnp.broadcast_to(loss, (1, D))
        dxh = dy * g
        dx_ref[...] = rstd * (dxh - xh * jnp.mean(dxh * xh, axis=-1, keepdims=True))

    return pl.pallas_call(
        body, name="loss_head", out_shape=[jax.ShapeDtypeStruct((t, D), F32), jax.ShapeDtypeStruct((8, D), F32)],
        grid=(t // tm,),
        in_specs=[pl.BlockSpec((tm, D), lambda i: (i, 0)), pl.BlockSpec((tm, D), lambda i: (i, 0)),
                  pl.BlockSpec((1, D), lambda i: (0, 0))],
        out_specs=[pl.BlockSpec((tm, D), lambda i: (i, 0)), pl.BlockSpec((8, D), lambda i: (0, 0))],
        compiler_params=_cparams(("arbitrary",)),
    )(x, target, final_g.reshape(1, D))


def _bwd_ff2(dx2, r, modv, w_ff2, l):
    t = dx2.shape[0]
    tm = _tok_tile(t)

    def body(d_ref, r_ref, mod_ref, w_ref, o_ref):
        dyg = (d_ref[...] * mod_ref[5:6, :]).astype(BF16)
        for j in range(N_CHIPS):
            da = lax.dot_general(dyg, w_ref[j], (((1,), (1,)), ((), ())), preferred_element_type=F32)
            o_ref[:, j * D:(j + 1) * D] = (da * 2.0 * r_ref[:, j * D:(j + 1) * D].astype(F32)).astype(BF16)

    return pl.pallas_call(
        body, name=f"bwd_ff2_{l}", out_shape=jax.ShapeDtypeStruct((t, DFF), BF16), grid=(t // tm,),
        in_specs=[pl.BlockSpec((tm, D), lambda i: (i, 0)), pl.BlockSpec((tm, DFF), lambda i: (i, 0)),
                  pl.BlockSpec((None, 8, D), lambda i: (l, 0, 0)), _wspec4(D, D, l)],
        out_specs=pl.BlockSpec((tm, DFF), lambda i: (i, 0)), compiler_params=_cparams(("parallel",)),
    )(dx2, r, modv, w_ff2)


def _bwd_norm(dy, w, x, dres, modv, l, which, send=(), w_out=None, share=(), exchange=()):
    t = x.shape[0]
    tm = _tok_tile(t)
    nsteps = t // tm
    rows = (6, 1) if which == "in" else (7, 4)
    width = dy.shape[1]
    ns, nh, ne = len(send), len(share), len(exchange)
    nb_ = 0 if w_out is None else 1
    n_in = 5 + nb_ + ns + nh + ne

    def body(*refs):
        dy_ref, w_ref, x_ref, dr_ref, mod_ref = refs[:5]
        wo_ref = refs[5] if nb_ else None
        parts = refs[5 + nb_:5 + nb_ + ns]
        pairs = refs[5 + nb_ + ns + nh:n_in]
        dx_ref, st_ref = refs[n_in:n_in + 2]
        dmix_ref = refs[n_in + 2] if nb_ else None
        o0 = n_in + 2 + nb_
        from_sib, shared, recvs = refs[o0:o0 + ns], refs[o0 + ns:o0 + ns + nh], refs[o0 + ns + nh:o0 + ns + nh + ne]
        scratch = list(refs[o0 + ns + nh + ne:])
        sems = [scratch.pop(0) for _ in range(2 if ns else 0)]
        hsems = [scratch.pop(0) for _ in range(2 if nh else 0)]
        esems = scratch

        @pl.when(pl.program_id(0) == 0)
        def _():
            st_ref[...] = jnp.zeros_like(st_ref)
            if ns:
                for cp in _pair_send_copies(parts, from_sib, sems):
                    cp.start()
            if nh:
                _pair_share_start(shared, hsems)
            if ne:
                for cp in _exchange_copies(pairs, recvs, esems):
                    cp.start()

        if which == "in":
            dh = lax.dot_general(dy_ref[...], w_ref[...], (((1,), (1,)), ((), ())), preferred_element_type=F32)
        else:
            dh = jnp.zeros((tm, D), F32)
            for j in range(N_CHIPS):
                dh = dh + lax.dot_general(dy_ref[:, j * D:(j + 1) * D], w_ref[j], (((1,), (1,)), ((), ())),
                                          preferred_element_type=F32)
        ng, sc = mod_ref[rows[0]:rows[0] + 1, :], mod_ref[rows[1]:rows[1] + 1, :]
        dx, dsh, dsc, dng = _norm_mod_bwd(dh, x_ref[...], ng, sc)
        dx_new = dr_ref[...] + dx
        dx_ref[...] = dx_new
        st_ref[0:1, :] += dsh
        st_ref[1:2, :] += dsc
        st_ref[2:3, :] += dng
        if nb_:
            dyg = (dx_new * mod_ref[2:3, :]).astype(BF16)
            dmix_ref[...] = lax.dot_general(dyg, wo_ref[...].reshape(D, D), (((1,), (1,)), ((), ())),
                                            preferred_element_type=F32).astype(BF16)

        if ns:
            @pl.when(pl.program_id(0) == nsteps - 1)
            def _():
                for cp in _pair_send_copies(parts, from_sib, sems):
                    cp.wait()

        if nh:
            @pl.when(pl.program_id(0) == nsteps - 1)
            def _():
                _pair_share_finish(shared, hsems)

        if ne:
            @pl.when(pl.program_id(0) == nsteps - 1)
            def _():
                for cp in _exchange_copies(pairs, recvs, esems):
                    cp.wait()

    tok = pl.BlockSpec((tm, D), lambda i: (i, 0))
    wspec = pl.BlockSpec((D, NW), lambda i: (0, 0)) if which == "in" else _wspec4(D, D, l)
    return pl.pallas_call(
        body, name=f"bwd_norm_{which}_{l}",
        out_shape=[jax.ShapeDtypeStruct((t, D), F32), jax.ShapeDtypeStruct((8, D), F32)]
        + [jax.ShapeDtypeStruct((t, D), BF16)] * nb_ + _pair_send_shapes(send)
        + [jax.ShapeDtypeStruct(g.shape, g.dtype) for g in share] + _exchange_shapes(exchange),
        grid=(nsteps,),
        in_specs=[pl.BlockSpec((tm, width), lambda i: (i, 0)), wspec, tok, tok,
                  pl.BlockSpec((None, 8, D), lambda i: (l, 0, 0))]
        + [_wspec4(D // N_CHIPS, D, l)] * nb_ + _hbm_specs(ns + nh + ne),
        out_specs=[tok, pl.BlockSpec((8, D), lambda i: (0, 0))] + [tok] * nb_ + _hbm_specs(ns + nh + ne),
        input_output_aliases={5 + nb_ + ns + i: 2 + nb_ + ns + i for i in range(nh)},
        scratch_shapes=(_pair_send_sems(ns) if ns else []) + (_pair_send_sems(nh) if nh else [])
        + (_exchange_sems(ne) if ne else []),
        compiler_params=_cparams(("arbitrary",)),
    )(dy, w, x, dres, modv, *([w_out] * nb_), *send, *share, *exchange)


def _grad_weight(lhs, rhs, modv, l, which, w_gate=None):
    t = lhs.shape[0]
    tm = min(t, 2048)
    nt = t // tm
    gated = which in ("out", "ff2")
    if which == "in":
        nj, lw, rw, orows, ocols = 5, D, NW // 5, D, NW // 5
    elif which == "ff1":
        nj, lw, rw, orows, ocols = N_CHIPS, D, D, D, D
    elif which == "out":
        nj, lw, rw, orows, ocols = N_CHIPS, D // N_CHIPS, D, D // N_CHIPS, D
    else:
        nj, lw, rw, orows, ocols = N_CHIPS, D, D, D, D
    gate_row = 2 if which == "out" else 5

    def body(*refs):
        if gated:
            l_ref, r_ref, mod_ref, wg_ref, o_ref, dg_ref, acc = refs
        else:
            l_ref, r_ref, mod_ref, o_ref, acc = refs
        j, k = pl.program_id(0), pl.program_id(1)

        @pl.when(k == 0)
        def _():
            acc[...] = jnp.zeros_like(acc)

        if which == "ff2":
            lv = l_ref[...].astype(F32)
            lv = lv * lv
        else:
            lv = l_ref[...]
        acc[...] += _dot_tn(lv, r_ref[...])

        if gated:
            @pl.when(jnp.logical_and(j == 0, k == 0))
            def _():
                dg_ref[...] = jnp.zeros_like(dg_ref)

        @pl.when(k == nt - 1)
        def _():
            raw = acc[...]
            if gated:
                o_ref[...] = (raw * mod_ref[gate_row:gate_row + 1, :]).astype(o_ref.dtype)
                dg_ref[0:1, :] += jnp.sum(raw * wg_ref[...].astype(F32), axis=0, keepdims=True)
            else:
                o_ref[...] = raw.astype(o_ref.dtype)

    if which in ("in", "ff1"):
        lspec = pl.BlockSpec((tm, lw), lambda j, k: (k, 0))
        rspec = pl.BlockSpec((tm, rw), lambda j, k: (k, j))
    else:
        lspec = pl.BlockSpec((tm, lw), lambda j, k: (k, j))
        rspec = pl.BlockSpec((tm, rw), lambda j, k: (k, 0))
    mspec = pl.BlockSpec((None, 8, D), lambda j, k: (l, 0, 0))
    if which == "in":
        ospec = pl.BlockSpec((orows, ocols), lambda j, k: (0, j))
        out_shape = [jax.ShapeDtypeStruct((D, NW), BF16)]
    else:
        ospec = pl.BlockSpec((None, orows, ocols), lambda j, k: (j, 0, 0))
        out_shape = [jax.ShapeDtypeStruct((N_CHIPS, orows, ocols), BF16)]
    in_specs = [lspec, rspec, mspec]
    args = [lhs, rhs, modv]
    out_specs = [ospec]
    if gated:
        in_specs.append(pl.BlockSpec((None, orows, ocols), lambda j, k: (j, 0, 0)))
        args.append(w_gate)
        out_specs.append(pl.BlockSpec((8, D), lambda j, k: (0, 0)))
        out_shape.append(jax.ShapeDtypeStruct((8, D), F32))
    res = pl.pallas_call(
        body, name=f"grad_w_{which}_{l}", out_shape=out_shape, grid=(nj, nt), in_specs=in_specs, out_specs=out_specs,
        scratch_shapes=[pltpu.VMEM((orows, ocols), F32)], compiler_params=_cparams(("arbitrary", "arbitrary")),
    )(*args)
    return (res[0], res[1]) if gated else (res[0], None)


def _tri_masks():
    rows, cols = _iota2((BLK, BLK), 0), _iota2((BLK, BLK), 1)
    return rows >= cols, rows > cols


def _sgu_forward(p_ref, lnp_ref, sguw_ref, sgub_ref):
    incl, _ = _tri_masks()
    ug = _gelu(p_ref[:, 0:512])
    vg = _gelu(p_ref[:, 512:1024])
    mu = jnp.mean(vg, axis=-1, keepdims=True)
    xc = vg - mu
    rstd = lax.rsqrt(jnp.mean(xc * xc, axis=-1, keepdims=True) + LN_EPS)
    vhat = xc * rstd
    vn = vhat * lnp_ref[0:1, :] + lnp_ref[1:2, :]
    bias = sgub_ref[...]
    ys, mixed, wms = [], [], []
    for h in range(HEADS):
        wm = jnp.where(incl, sguw_ref[h], 0.0)
        mx = _dot(wm, vn[:, h * HD:(h + 1) * HD]) + _col(bias, h)
        ys.append(ug[:, h * HD:(h + 1) * HD] * mx)
        mixed.append(mx)
        wms.append(wm)
    return ys, ug, vhat, rstd, vn, mixed, wms


def _conv_forward(xbuf, cw_ref):
    conv = cw_ref[0:1, :] * xbuf[5:5 + BLK, :]
    for j in range(1, 4):
        conv = conv + cw_ref[j:j + 1, :] * xbuf[5 + j:5 + j + BLK, :]
    return conv


def _gates(gt, gv_ref):
    incl, _ = _tri_masks()
    beta = _sigmoid(gt)
    neg_a = -jnp.exp(gv_ref[0:1, :])
    gl = neg_a * _softplus(gt + gv_ref[1:2, :])
    gc = _dotf(jnp.where(incl, 1.0, 0.0).astype(F32), gl)
    return beta, gl, gc, gc.T, neg_a


def _head_chunk(act, beta, gc, gct, h):
    incl, strict = _tri_masks()
    qh = act[:, h * HD:(h + 1) * HD]
    kh = act[:, 512 + h * HD:512 + (h + 1) * HD]
    vh = act[:, 1024 + h * HD:1024 + (h + 1) * HD]
    rq = lax.rsqrt(jnp.sum(qh * qh, axis=-1, keepdims=True) + RMS_EPS)
    rk = lax.rsqrt(jnp.sum(kh * kh, axis=-1, keepdims=True) + RMS_EPS)
    qhat, khat = qh * rq, kh * rk
    qn = qhat * QK_SCALE
    b = _col(beta, h)
    gcol = _col(gc, 4 + h)
    grow = _row(gct, 4 + h)
    dmat = jnp.where(incl, jnp.exp(jnp.where(incl, gcol - grow, 0.0)), 0.0)
    gam = jnp.exp(gcol)
    glast = _row(gcol, BLK - 1)
    e = jnp.exp(glast - gcol)
    kk = _d3_nt(khat, khat)
    return dict(qhat=qhat, khat=khat, qn=qn, vh=vh, rq=rq, rk=rk, b=b, dmat=dmat, gam=gam, glast=glast, e=e, kk=kk,
                strict=strict, incl=incl)


def _mixer_forward(p, lnp, sgu_w, sgu_bt, cw, gv, l, gather=()):
    t = p.shape[0]
    nb = t // BLK
    ng = len(gather)

    def body(*refs):
        p_ref, lnp_ref, sguw_ref, sgub_ref, cw_ref, gv_ref = refs[:6]
        mix_ref, s_out, t_out, u_out, w_out, o_out, conv_out = refs[6 + ng:13 + ng]
        gbufs = refs[13 + ng:13 + 2 * ng]
        s_scr, xbuf = refs[13 + 2 * ng:15 + 2 * ng]
        gsems = refs[15 + 2 * ng:]

        @pl.when(pl.program_id(0) == 0)
        def _():
            s_scr[...] = jnp.zeros_like(s_scr)
            xbuf[0:8, :] = jnp.zeros((8, 1536), F32)
            if ng:
                _gather_start(gbufs, gsems)

        ys = _sgu_forward(p_ref, lnp_ref, sguw_ref, sgub_ref)[0]
        for h in range(HEADS):
            mix_ref[:, h * HD:(h + 1) * HD] = ys[h].astype(BF16)

        xbuf[8:8 + BLK, :] = p_ref[:, 1024:2560]
        conv = _conv_forward(xbuf, cw_ref)
        conv_out[...] = conv
        act = _silu(conv)
        xbuf[0:8, :] = xbuf[BLK:BLK + 8, :]
        beta, _, gc, gct, _ = _gates(p_ref[:, GATE0:NW], gv_ref)
        chunks = [_head_chunk(act, beta, gc, gct, h) for h in range(HEADS)]
        for h, hc in enumerate(chunks):
            t_out[h] = jnp.where(hc["strict"], hc["b"] * hc["kk"] * hc["dmat"], 0.0)
        t_out[...] = _tri_inverse(t_out[...])
        for h, hc in enumerate(chunks):
            tm = t_out[h]
            u = _dot(tm, hc["b"] * hc["vh"])
            w = _dot(tm, (hc["b"] * hc["gam"]) * hc["khat"])
            qkm = _dot_nt(hc["qn"], hc["khat"]) * hc["dmat"]
            s = s_scr[h]
            wn = u - _dot(w, s)
            o = _dot(hc["qn"] * hc["gam"], s) + _dot(qkm, wn)
            s_out[h] = s
            s_scr[h] = jnp.exp(hc["glast"]) * s + _dot_tn(hc["khat"] * hc["e"], wn)
            sl = slice(h * HD, (h + 1) * HD)
            u_out[:, sl] = u
            w_out[:, sl] = w
            o_out[:, sl] = o
            on = o * lax.rsqrt(jnp.mean(o * o, axis=-1, keepdims=True) + RMS_EPS) * gv_ref[2:3, :]
            mix_ref[:, 512 + h * HD:512 + (h + 1) * HD] = (on * _silu(p_ref[:, 2560 + h * HD:2560 + (h + 1) * HD])).astype(BF16)

        if ng:
            @pl.when(pl.program_id(0) == nb - 1 - min(3, nb - 1))
            def _():
                _gather_forward(gbufs, gsems)

            @pl.when(pl.program_id(0) == nb - 1)
            def _():
                _gather_finish(gbufs, gsems)

    tok = lambda w: pl.BlockSpec((BLK, w), lambda i: (i, 0))
    st = pl.BlockSpec((None, HEADS, HD, HD), lambda i: (i, 0, 0, 0))
    return pl.pallas_call(
        body, name=f"mixer_fwd_{l}", grid=(nb,),
        out_shape=[jax.ShapeDtypeStruct((t, D), BF16), jax.ShapeDtypeStruct((nb, HEADS, HD, HD), F32),
                   jax.ShapeDtypeStruct((nb, HEADS, HD, HD), F32), jax.ShapeDtypeStruct((t, 512), F32),
                   jax.ShapeDtypeStruct((t, 512), F32), jax.ShapeDtypeStruct((t, 512), F32),
                   jax.ShapeDtypeStruct((t, 1536), F32)]
        + [jax.ShapeDtypeStruct(b.shape, b.dtype) for b in gather],
        in_specs=[tok(NW), pl.BlockSpec((None, 8, 512), lambda i: (l, 0, 0)),
                  pl.BlockSpec((None, HEADS, HD, HD), lambda i: (l, 0, 0, 0)),
                  pl.BlockSpec((None, HD, HD), lambda i: (l, 0, 0)), pl.BlockSpec((None, 8, 1536), lambda i: (l, 0, 0)),
                  pl.BlockSpec((None, 8, HD), lambda i: (l, 0, 0))] + _hbm_specs(ng),
        out_specs=[tok(D), st, st, tok(512), tok(512), tok(512), tok(1536)]
        + _hbm_specs(ng),
        input_output_aliases={6 + i: 7 + i for i in range(ng)},
        scratch_shapes=[pltpu.VMEM((HEADS, HD, HD), F32), pltpu.VMEM((BLK + 8, 1536), F32)]
        + (_gather_sems(ng) if ng else []),
        compiler_params=_cparams(("arbitrary",)),
    )(p, lnp, sgu_w, sgu_bt, cw, gv, *gather)


def _mixer_backward(p, dmix, saved, lnp, sgu_w, sgu_bt, cw, gv, l, exchange=()):
    t = p.shape[0]
    nb = t // BLK
    s_sv, t_sv, u_sv, w_sv, o_sv, conv_sv = saved
    ne = len(exchange)

    def body(*refs):
        (p_ref, dmix_ref, s_ref, t_ref, u_ref, w_ref, o_ref, conv_ref, lnp_ref, sguw_ref, sgub_ref, cw_ref,
         gv_ref) = refs[:13]
        pairs = refs[13:13 + ne]
        dp_ref, dlnp_ref, dsguw_ref, dsgub_ref, dcw_ref, dgv_ref = refs[13 + ne:19 + ne]
        recvs = refs[19 + ne:19 + 2 * ne]
        ds_scr, dcbuf = refs[19 + 2 * ne:21 + 2 * ne]
        esems = refs[21 + 2 * ne:]

        @pl.when(pl.program_id(0) == 0)
        def _():
            if ne:
                for cp in _exchange_copies(pairs, recvs, esems):
                    cp.start()
            ds_scr[...] = jnp.zeros_like(ds_scr)
            dcbuf[BLK:BLK + 8, :] = jnp.zeros((8, 1536), F32)
            dlnp_ref[...] = jnp.zeros_like(dlnp_ref)
            dsguw_ref[...] = jnp.zeros_like(dsguw_ref)
            dsgub_ref[...] = jnp.zeros_like(dsgub_ref)
            dcw_ref[...] = jnp.zeros_like(dcw_ref)
            dgv_ref[...] = jnp.zeros_like(dgv_ref)

        incl, strict = _tri_masks()
        _, ug, vhat, rstd, vn, mixed, wms = _sgu_forward(p_ref, lnp_ref, sguw_ref, sgub_ref)
        dvn_parts, dug_parts = [], []
        dbias = jnp.zeros((BLK, HD), F32)
        for h in range(HEADS):
            sl = slice(h * HD, (h + 1) * HD)
            dy = dmix_ref[:, sl].astype(F32)
            dmx = dy * ug[:, sl]
            dug_parts.append(dy * mixed[h])
            dsguw_ref[h] += jnp.where(incl, _dot_nt(dmx, vn[:, sl]), 0.0)
            dbias = dbias + _put_col(jnp.sum(dmx, axis=1, keepdims=True), h)
            dvn_parts.append(_dot_tn(wms[h], dmx))
        dsgub_ref[...] += dbias
        dvn = jnp.concatenate(dvn_parts, axis=1)
        dug = jnp.concatenate(dug_parts, axis=1)
        dlnp_ref[0:1, :] += jnp.sum(dvn * vhat, axis=0, keepdims=True)
        dlnp_ref[1:2, :] += jnp.sum(dvn, axis=0, keepdims=True)
        dvhat = dvn * lnp_ref[0:1, :]
        dvg = rstd * (dvhat - jnp.mean(dvhat, axis=-1, keepdims=True)
                      - vhat * jnp.mean(dvhat * vhat, axis=-1, keepdims=True))
        dp_ref[:, 0:512] = (dug * _gelu_grad(p_ref[:, 0:512])).astype(BF16)
        dp_ref[:, 512:1024] = (dvg * _gelu_grad(p_ref[:, 512:1024])).astype(BF16)

        conv = conv_ref[...]
        act = _silu(conv)
        gt = p_ref[:, GATE0:NW]
        beta, gl, gc, gct, neg_a = _gates(gt, gv_ref)
        gng = gv_ref[2:3, :]
        dbeta_t = jnp.zeros((BLK, HD), F32)
        dgc_t = jnp.zeros((BLK, HD), F32)
        dgng = jnp.zeros((1, HD), F32)
        for h in range(HEADS):
            sl = slice(h * HD, (h + 1) * HD)
            hc = _head_chunk(act, beta, gc, gct, h)
            b, gam, e, dmat, kk = hc["b"], hc["gam"], hc["e"], hc["dmat"], hc["kk"]
            qn, khat, vh = hc["qn"], hc["khat"], hc["vh"]
            gamlast = jnp.exp(hc["glast"])
            s, tm, u, w, o = s_ref[h], t_ref[h], u_ref[:, sl], w_ref[:, sl], o_ref[:, sl]
            ds_next = ds_scr[h]
            z = p_ref[:, 2560 + h * HD:2560 + (h + 1) * HD]
            dy = dmix_ref[:, 512 + h * HD:512 + (h + 1) * HD].astype(F32)
            ro = lax.rsqrt(jnp.mean(o * o, axis=-1, keepdims=True) + RMS_EPS)
            ohat = o * ro
            dp_ref[:, 2560 + h * HD:2560 + (h + 1) * HD] = (dy * ohat * gng * _silu_grad(z)).astype(BF16)
            don = dy * _silu(z)
            dgng = dgng + jnp.sum(don * ohat, axis=0, keepdims=True)
            dohat = don * gng
            do = ro * (dohat - ohat * jnp.mean(dohat * ohat, axis=-1, keepdims=True))
            qk_raw = _dot_nt(qn, khat)
            qkm = qk_raw * dmat
            qd, kd = qn * gam, khat * e
            wn = u - _dot(w, s)
            dwn = _dot_tn(qkm, do) + _dot(kd, ds_next)
            dqd = _dot_nt(do, s)
            dqkm = jnp.where(incl, _dot_nt(do, wn), 0.0)
            ds_scr[h] = _dot_tn(qd, do) + gamlast * ds_next - _dot_tn(w, dwn)
            dgamlast = jnp.sum(jnp.sum(ds_next * s, axis=1, keepdims=True), axis=0, keepdims=True)
            dkd = _dot_nt(wn, ds_next)
            dw = -_dot_nt(dwn, s)
            db1 = _dot_tn(tm, dwn)
            db2 = _dot_tn(tm, dw)
            dm = jnp.where(strict, -(_dot_nt(db1, u) + _dot_nt(db2, w)), 0.0)
            dbeta = (jnp.sum(dm * kk * dmat, axis=1, keepdims=True) + jnp.sum(db1 * vh, axis=1, keepdims=True)
                     + gam * jnp.sum(db2 * khat, axis=1, keepdims=True))
            dkkm = dm * b * dmat
            ddm = dm * b * kk + dqkm * qk_raw
            dgam = b * jnp.sum(db2 * khat, axis=1, keepdims=True) + jnp.sum(dqd * qn, axis=1, keepdims=True)
            g_qk = dqkm * dmat
            dqn = _dot(g_qk, khat) + dqd * gam
            dkhat = ((b * gam) * db2 + _dot_tn(g_qk, qn) + _dot(dkkm, khat) + _dot_tn(dkkm, khat) + dkd * e)
            dvh = b * db1
            rkd = jnp.sum(dkd * kd, axis=1, keepdims=True)
            emat = ddm * dmat
            dgc = (dgam * gam - rkd + jnp.sum(emat, axis=1, keepdims=True)
                   - jnp.sum(emat.T, axis=1, keepdims=True))
            last = _iota2((BLK, 1), 0) == BLK - 1
            dgc = dgc + jnp.where(last, jnp.sum(rkd, axis=0, keepdims=True) + dgamlast * gamlast, 0.0)
            dgc_t = dgc_t + _put_col(dgc, 4 + h)
            dbeta_t = dbeta_t + _put_col(dbeta, h)
            dqhat = dqn * QK_SCALE
            dq = hc["rq"] * (dqhat - hc["qhat"] * jnp.sum(dqhat * hc["qhat"], axis=-1, keepdims=True))
            dk = hc["rk"] * (dkhat - khat * jnp.sum(dkhat * khat, axis=-1, keepdims=True))
            dcbuf[0:BLK, h * HD:(h + 1) * HD] = dq
            dcbuf[0:BLK, 512 + h * HD:512 + (h + 1) * HD] = dk
            dcbuf[0:BLK, 1024 + h * HD:1024 + (h + 1) * HD] = dvh
        dgv_ref[2:3, :] += dgng
        dgl = _dotf_tn(jnp.where(incl, 1.0, 0.0).astype(F32), dgc_t)
        sig_a = _sigmoid(gt + gv_ref[1:2, :])
        d_araw = dgl * neg_a * sig_a
        dgv_ref[0:1, :] += jnp.sum(dgl * gl, axis=0, keepdims=True)
        dgv_ref[1:2, :] += jnp.sum(d_araw, axis=0, keepdims=True)
        dp_ref[:, GATE0:NW] = (dbeta_t * beta * (1.0 - beta) + d_araw).astype(BF16)
        dcbuf[0:BLK, :] = dcbuf[0:BLK, :] * _silu_grad(conv)
        xcur = p_ref[:, 1024:2560]
        dqkv = jnp.zeros((BLK, 1536), F32)
        for j in range(4):
            shifted = dcbuf[3 - j:3 - j + BLK, :]
            dqkv = dqkv + cw_ref[j:j + 1, :] * shifted
            dcw_ref[j:j + 1, :] += jnp.sum(shifted * xcur, axis=0, keepdims=True)
        dp_ref[:, 1024:2560] = dqkv.astype(BF16)
        dcbuf[BLK:BLK + 8, :] = dcbuf[0:8, :]

        if ne:
            @pl.when(pl.program_id(0) == nb - 1)
            def _():
                for cp in _exchange_copies(pairs, recvs, esems):
                    cp.wait()

    rev = lambda w: pl.BlockSpec((BLK, w), lambda i: (nb - 1 - i, 0))
    st = pl.BlockSpec((None, HEADS, HD, HD), lambda i: (nb - 1 - i, 0, 0, 0))
    fix = lambda *shape: pl.BlockSpec((None,) + shape, lambda i: (l,) + (0,) * len(shape))
    acc = lambda *shape: pl.BlockSpec(shape, lambda i: (0,) * len(shape))
    return pl.pallas_call(
        body, name=f"mixer_bwd_{l}", grid=(nb,),
        out_shape=[jax.ShapeDtypeStruct((t, NW), BF16), jax.ShapeDtypeStruct((8, 512), F32),
                   jax.ShapeDtypeStruct((HEADS, HD, HD), F32), jax.ShapeDtypeStruct((HD, HD), F32),
                   jax.ShapeDtypeStruct((8, 1536), F32), jax.ShapeDtypeStruct((8, HD), F32)]
        + _exchange_shapes(exchange),
        in_specs=[rev(NW), rev(D), st, st, rev(512), rev(512), rev(512),
                  rev(1536),
                  fix(8, 512), fix(HEADS, HD, HD), fix(HD, HD), fix(8, 1536), fix(8, HD)] + _hbm_specs(ne),
        out_specs=[rev(NW), acc(8, 512), acc(HEADS, HD, HD), acc(HD, HD), acc(8, 1536), acc(8, HD)]
        + _hbm_specs(ne),
        scratch_shapes=[pltpu.VMEM((HEADS, HD, HD), F32), pltpu.VMEM((BLK + 8, 1536), F32)]
        + (_exchange_sems(ne) if ne else []),
        compiler_params=_cparams(("arbitrary",)),
    )(p, dmix, s_sv, t_sv, u_sv, w_sv, o_sv, conv_sv, lnp, sgu_w, sgu_bt, cw, gv, *exchange)


_SMALL = (("b_ada", 24), ("norm1_g", 8), ("norm2_g", 8), ("final_g", 8), ("sgu_ln_g", 8), ("sgu_ln_b", 8),
          ("sgu_w", 256), ("sgu_b", 8), ("conv_w", 24), ("a_log", 8), ("dt_bias", 8), ("gdn_norm_g", 8))
_SMALL_PAD = sum(n for _, n in _SMALL)
_DMOD_ROWS = 24


def _pack_rows(parts):
    rows = []
    for (name, n), a in zip(_SMALL, parts):
        flat = a.reshape(-1).astype(F32)
        rows.append(jnp.pad(flat, (0, n * D - flat.shape[0])).reshape(n, D))
    return jnp.concatenate(rows, axis=0)


def _unpack_rows(buf, shapes):
    out, r0 = {}, 0
    for name, n in _SMALL:
        size = math.prod(shapes[name])
        out[name] = buf[r0:r0 + n].reshape(-1)[:size].reshape(shapes[name])
        r0 += n
    return out


_COMBINED_ROWS = 2 * _DMOD_ROWS + _SMALL_PAD
_GATHER_ROWS = -(-_COMBINED_ROWS // 16) * 16


def _pair_combine(own, sib, place):
    rows = own.shape[0]

    def body(p_ref, a_ref, b_ref, o_ref):
        first = lax.axis_index("c") == 0
        a, b = a_ref[0:_DMOD_ROWS, :], b_ref[0:_DMOD_ROWS, :]
        o_ref[0:_DMOD_ROWS, :] = jnp.where(first, a, b)
        o_ref[_DMOD_ROWS:2 * _DMOD_ROWS, :] = jnp.where(first, b, a)
        o_ref[2 * _DMOD_ROWS:_COMBINED_ROWS, :] = a_ref[_DMOD_ROWS:, :] + b_ref[_DMOD_ROWS:, :]
        o_ref[_COMBINED_ROWS:, :] = jnp.zeros((_GATHER_ROWS - _COMBINED_ROWS, D), F32)

    return pl.pallas_call(
        body, name="small_pair_combine", out_shape=jax.ShapeDtypeStruct((N_CHIPS, _GATHER_ROWS, D), F32),
        grid_spec=pltpu.PrefetchScalarGridSpec(
            num_scalar_prefetch=1, grid=(1,),
            in_specs=[pl.BlockSpec((rows, D), lambda i, pr: (0, 0)), pl.BlockSpec((rows, D), lambda i, pr: (0, 0))],
            out_specs=pl.BlockSpec((None, _GATHER_ROWS, D), lambda i, pr: (pr[0], 0, 0))),
        compiler_params=_cparams(("arbitrary",)),
    )(place, own, sib)


def _small_finalize(gathered, w, m, v):
    def body(g_ref, w_ref, m_ref, v_ref, go_ref, d_ref, nm_ref, nv_ref):
        lo, hi = 2 * _DMOD_ROWS, _COMBINED_ROWS
        sm = g_ref[0, lo:hi, :] + g_ref[1, lo:hi, :]
        sm = sm + g_ref[2, lo:hi, :]
        sm = sm + g_ref[3, lo:hi, :]
        bsum = jnp.zeros((_DMOD_ROWS, D), F32)
        for j in range(N_CHIPS):
            bsum = bsum + g_ref[j, 0:_DMOD_ROWS, :]
            bsum = bsum + g_ref[j, _DMOD_ROWS:2 * _DMOD_ROWS, :]
        go_ref[0:_DMOD_ROWS, :] = bsum
        go_ref[_DMOD_ROWS:, :] = sm[_DMOD_ROWS:, :]
        d_ref[...], nm_ref[...], nv_ref[...] = _adam_math(w_ref[...], go_ref[...], m_ref[...], v_ref[...])

    return pl.pallas_call(
        body, name="small_finalize", out_shape=[jax.ShapeDtypeStruct(w.shape, F32)] * 4,
        compiler_params=pltpu.CompilerParams(vmem_limit_bytes=VMEM_LIMIT),
    )(gathered, w, m, v)


def kernel(x, c, w_ada, b_ada, norm1_g, w_in, sgu_ln_g, sgu_ln_b, sgu_w, sgu_b, conv_w, a_log, dt_bias, gdn_norm_g, w_out, norm2_g, w_ff1, w_ff2, final_g, loss_target, m_w_ada, m_b_ada, m_norm1_g, m_w_in, m_sgu_ln_g, m_sgu_ln_b, m_sgu_w, m_sgu_b, m_conv_w, m_a_log, m_dt_bias, m_gdn_norm_g, m_w_out, m_norm2_g, m_w_ff1, m_w_ff2, m_final_g, v_w_ada, v_b_ada, v_norm1_g, v_w_in, v_sgu_ln_g, v_sgu_ln_b, v_sgu_w, v_sgu_b, v_conv_w, v_a_log, v_dt_bias, v_gdn_norm_g, v_w_out, v_norm2_g, v_w_ff1, v_w_ff2, v_final_g):
    xi, yi, ci = lax.axis_index("x"), lax.axis_index("y"), lax.axis_index("c")
    chip = 2 * xi + yi
    dev = 2 * chip + ci
    t = x.shape[1]
    x0 = x.reshape(t, D)
    target = loss_target.reshape(t, D)

    c_sib = _pair_exchange(c, "c_pair")
    c_pair = jnp.where(ci == 0, jnp.concatenate([c, c_sib], 0), jnp.concatenate([c_sib, c], 0))
    c_all = _chip_allgather(c_pair, "c_chips").reshape(8, D)
    ada_cols = w_ada.shape[2]
    b_cols = lax.dynamic_slice_in_dim(b_ada, chip * ada_cols, ada_cols, axis=1)
    mod_part = _ada_forward(c_all, w_ada, b_cols)
    conv_cols = conv_w.shape[2]
    packed = jnp.concatenate([mod_part.reshape(DEPTH * 8, ada_cols), conv_w.reshape(DEPTH, 4 * conv_cols)], axis=0)
    packed = _chip_allgather(packed, "mod_chips")
    mod_all = packed[:, :DEPTH * 8].reshape(N_CHIPS, DEPTH, 8, ada_cols)
    mod_mine = lax.dynamic_index_in_dim(mod_all, dev, axis=2, keepdims=False)
    mod = mod_mine.transpose(1, 0, 2).reshape(DEPTH, 6, D)
    modv = jnp.concatenate([mod, norm1_g[:, None, :], norm2_g[:, None, :]], axis=1)
    conv_full = packed[:, DEPTH * 8:].reshape(N_CHIPS, DEPTH, 4, conv_cols).transpose(1, 2, 0, 3).reshape(DEPTH, 4, 1536)

    place = jnp.stack([chip, ci]).astype(jnp.int32)
    wbufs = [[_cast_into_slot(w, l, place) for w in (w_in, w_out, w_ff1, w_ff2)] for l in range(DEPTH)]
    wbufs[0][:1] = _weights_allgather(wbufs[0][:1], 0)

    def full_w_in(g):
        return jnp.pad(g.transpose(1, 0, 2).reshape(D, IN_W), ((0, 0), (0, NW - IN_W)))

    lnp = jnp.pad(jnp.stack([sgu_ln_g, sgu_ln_b], axis=1), ((0, 0), (0, 6), (0, 0)))
    sgu_bt = jnp.pad(sgu_b.transpose(0, 2, 1), ((0, 0), (0, 0), (0, HD - HEADS)))
    cw = jnp.pad(conv_full, ((0, 0), (0, 4), (0, 0)))
    lane_pad = lambda a: jnp.pad(a, ((0, 0), (4, HD - 8)))
    gv = jnp.pad(jnp.stack([lane_pad(a_log), lane_pad(dt_bias), gdn_norm_g], axis=1), ((0, 0), (0, 5), (0, 0)))

    acts = []
    xl = x0
    for l in range(DEPTH):
        win = full_w_in(wbufs[l][0])
        p, h1 = _fwd_in(xl, modv, win, l)
        nxt = wbufs[l][1:] + (wbufs[l + 1][:1] if l + 1 < DEPTH else [])
        mix, *rest = _mixer_forward(p, lnp, sgu_w, sgu_bt, cw, gv, l, gather=nxt)
        saved = rest[:6]
        wbufs[l][1:] = rest[6:9]
        if l + 1 < DEPTH:
            wbufs[l + 1][:1] = rest[9:]
        g_in, g_out, g_ff1, g_ff2 = wbufs[l]
        x1, r, h2 = _fwd_out_ff1(xl, mix, modv, g_out, g_ff1, l)
        x2 = _fwd_ff2(x1, r, modv, g_ff2, l)
        acts.append((xl, p, mix, saved, x1, r, h1, h2, win))
        xl = x2

    dx, head_stats = _loss_head(xl, target, final_g)
    loss = lax.psum(jnp.sum(head_stats[1, 0:1]), ("x", "y", "c"))
    d_final_g = head_stats[0]
    names = ("in", "out", "ff1", "ff2")
    grads_buf = [None] * len(names)

    def pair_sums(partials, from_sib, kinds, lay):
        return [(lay, n, _pair_sum(g, ga, place, f"pair_sum_{n}_{lay}")) for g, ga, n in zip(partials, from_sib, kinds)]

    def reduce_into_buffers(items, recv):
        for (lay, n, pair), rc in zip(items, recv):
            i = names.index(n)
            grads_buf[i] = _chip_sum(pair, rc, grads_buf[i], lay, place, f"chip_sum_{n}_{lay}")

    pending = []

    dmod, small = [None] * DEPTH, [None] * DEPTH
    for l in reversed(range(DEPTH)):
        xl, p, mix, saved, x1, r, h1, h2, win = acts[l]
        g_in, g_out, g_ff1, g_ff2 = wbufs[l]
        df = _bwd_ff2(dx, r, modv, g_ff2, l)
        gw_ff2, dg2 = _grad_weight(r, dx, modv, l, "ff2", g_ff2)
        gw_ff1, _ = _grad_weight(h2, df, modv, l, "ff1")
        dx1, st2, dmix, *sib_ff = _bwd_norm(df, g_ff1, x1, dx, modv, l, "ff1", send=[gw_ff1, gw_ff2], w_out=g_out)
        gw_out, dg1 = _grad_weight(mix, dx1, modv, l, "out", g_out)
        sib_out = _grads_pair_send([gw_out], f"out_{l}")
        pending = pending + pair_sums([gw_out, gw_ff1, gw_ff2], list(sib_out) + sib_ff, names[1:], l)
        dp, dlnp, dsguw, dsgub, dcw, dgv, *recv = _mixer_backward(p, dmix, saved, lnp, sgu_w, sgu_bt, cw, gv, l,
                                                                  exchange=[item[2] for item in pending])
        reduce_into_buffers(pending, recv)
        gw_in, _ = _grad_weight(h1, dp, modv, l, "in")
        gw_in_c = gw_in[:, :IN_W].reshape(D, N_CHIPS, IN_W // N_CHIPS).transpose(1, 0, 2)
        if l > 0:
            dx, st1, sib_in = _bwd_norm(dp, win, xl, dx1, modv, l, "in", send=[gw_in_c])
            pending = pair_sums([gw_in_c], [sib_in], names[:1], l)
        else:
            pending = pair_sums([gw_in_c], _grads_pair_send([gw_in_c], "in_0"), names[:1], l)
            dx, st1, *rest = _bwd_norm(dp, win, xl, dx1, modv, l, "in", share=grads_buf[1:],
                                       exchange=[item[2] for item in pending])
            grads_buf[1:] = rest[:3]
            reduce_into_buffers(pending, rest[3:])
        dmod[l] = jnp.stack([st1[0], st1[1], dg1[0], st2[0], st2[1], dg2[0]], axis=0)
        small[l] = dict(norm1_g=st1[2], norm2_g=st2[2], sgu_ln_g=dlnp[0], sgu_ln_b=dlnp[1], sgu_w=dsguw,
                        sgu_b=dsgub[:, :HEADS].T, conv_w=dcw[:4], a_log=dgv[0, 4:8], dt_bias=dgv[1, 4:8],
                        gdn_norm_g=dgv[2])
    grad_x = dx.reshape(1, t, D)

    stack = lambda k: jnp.stack([small[l][k] for l in range(DEPTH)], axis=0)
    small_grads = [jnp.zeros((DEPTH, 6 * D), F32), stack("norm1_g"), stack("norm2_g"), d_final_g, stack("sgu_ln_g"),
                   stack("sgu_ln_b"), stack("sgu_w"), stack("sgu_b"), stack("conv_w"), stack("a_log"),
                   stack("dt_bias"), stack("gdn_norm_g")]
    own = jnp.concatenate([jnp.stack(dmod, axis=0).reshape(_DMOD_ROWS, D), _pack_rows(small_grads)], axis=0)
    sib = _pair_exchange(own, "small_pair")
    gathered, = _weights_allgather([_pair_combine(own, sib, place)], "small")
    small_shapes = dict(b_ada=b_ada.shape, norm1_g=norm1_g.shape, norm2_g=norm2_g.shape, final_g=final_g.shape,
                        sgu_ln_g=sgu_ln_g.shape, sgu_ln_b=sgu_ln_b.shape, sgu_w=sgu_w.shape, sgu_b=sgu_b.shape,
                        conv_w=(DEPTH, 4, 1536), a_log=a_log.shape, dt_bias=dt_bias.shape,
                        gdn_norm_g=gdn_norm_g.shape)

    def full_conv(a):
        return lax.dynamic_update_slice_in_dim(jnp.zeros((DEPTH, 4, 1536), F32), a, chip * conv_cols, axis=2)

    def pack_state(b_, n1, n2, fg, lg, lb, sw, sb, cv, al, db, gn):
        return _pack_rows([b_, n1, n2, fg, lg, lb, sw, sb, full_conv(cv), al, db, gn])

    w_small = pack_state(b_ada, norm1_g, norm2_g, final_g, sgu_ln_g, sgu_ln_b, sgu_w, sgu_b, conv_w, a_log, dt_bias,
                         gdn_norm_g)
    m_small = pack_state(m_b_ada, m_norm1_g, m_norm2_g, m_final_g, m_sgu_ln_g, m_sgu_ln_b, m_sgu_w, m_sgu_b, m_conv_w,
                         m_a_log, m_dt_bias, m_gdn_norm_g)
    v_small = pack_state(v_b_ada, v_norm1_g, v_norm2_g, v_final_g, v_sgu_ln_g, v_sgu_ln_b, v_sgu_w, v_sgu_b, v_conv_w,
                         v_a_log, v_dt_bias, v_gdn_norm_g)
    small_out = _small_finalize(gathered, w_small, m_small, v_small)
    sg, sd, sm, sv = [_unpack_rows(a, small_shapes) for a in small_out]
    for dct in (sg, sd, sm, sv):
        dct["conv_w"] = lax.dynamic_slice_in_dim(dct["conv_w"], chip * conv_cols, conv_cols, axis=2)

    dmod_all = gathered[:, :2 * _DMOD_ROWS].reshape(8, DEPTH, 6 * D)
    dmod_cols = lax.dynamic_slice_in_dim(dmod_all, chip * ada_cols, ada_cols, axis=2).transpose(1, 0, 2)
    g_ada, d_ada, nm_ada, nv_ada = _ada_backward_adamw(c_all, dmod_cols, w_ada, m_w_ada, v_w_ada)

    grads_buf[:1] = _grads_pair_share(grads_buf[:1], "grads_pair_share_in")
    big = {}
    for n, g, (w, m, v) in zip(names, grads_buf, ((w_in, m_w_in, v_w_in), (w_out, m_w_out, v_w_out),
                                                  (w_ff1, m_w_ff1, v_w_ff1), (w_ff2, m_w_ff2, v_w_ff2))):
        big[n] = (g,) + tuple(_adamw(w, g, m, v, f"adamw_{n}"))

    def outs(k):
        s = (sg, sd, sm, sv)[k]
        return [(g_ada, d_ada, nm_ada, nv_ada)[k], s["b_ada"], s["norm1_g"], big["in"][k], s["sgu_ln_g"],
                s["sgu_ln_b"], s["sgu_w"], s["sgu_b"], s["conv_w"], s["a_log"], s["dt_bias"], s["gdn_norm_g"],
                big["out"][k], s["norm2_g"], big["ff1"][k], big["ff2"][k], s["final_g"]]

    return (loss, grad_x, *outs(0), *outs(1), *outs(2), *outs(3))
```

```python
import functools
import math

import jax
import jax.numpy as jnp
from jax import lax
from jax.experimental import pallas as pl
from jax.experimental.pallas import tpu as pltpu

F32 = jnp.float32
BF16 = jnp.bfloat16

DEPTH = 4
D = 1024
HEADS = 4
HD = 128
BLK = 128
IN_W = 3080
NW = 3200
GATE0 = 3072
DFF = 4096
N_CHIPS = 4
RMS_EPS = 1e-6
LN_EPS = 1e-5
QK_SCALE = HD ** -0.5
LR, B1, B2, ADAM_EPS, WD, STEP = 0.001, 0.9, 0.999, 1e-08, 0.01, 10
VMEM_LIMIT = 56 * 1024 * 1024
MESH = pl.DeviceIdType.MESH
HOPS = ((1, 0), (0, 1), (1, 1))
HI = lax.Precision.HIGHEST


def _dot(a, b):
    return jnp.dot(a.astype(BF16), b.astype(BF16), preferred_element_type=F32)


def _dot_nt(a, b):
    return lax.dot_general(a.astype(BF16), b.astype(BF16), (((1,), (1,)), ((), ())), preferred_element_type=F32)


def _dot_tn(a, b):
    return lax.dot_general(a.astype(BF16), b.astype(BF16), (((0,), (0,)), ((), ())), preferred_element_type=F32)


def _dotf(a, b):
    return jnp.dot(a, b, precision=HI, preferred_element_type=F32)


def _split(a):
    hi = a.astype(BF16)
    return hi, (a - hi.astype(F32)).astype(BF16)


def _dg3(a, b, dims, batch=((), ())):
    ah, al = _split(a)
    bh, bl = _split(b)
    f = lambda x, y: lax.dot_general(x, y, (dims, batch), preferred_element_type=F32)
    return f(ah, bh) + (f(ah, bl) + f(al, bh))


def _bmm3(a, b):
    return _dg3(a, b, ((2,), (1,)), ((0,), (0,)))


def _d3(a, b):
    return _dg3(a, b, ((1,), (0,)))


def _d3_nt(a, b):
    return _dg3(a, b, ((1,), (1,)))


def _d3_tn(a, b):
    return _dg3(a, b, ((0,), (0,)))


def _dotf_tn(a, b):
    return lax.dot_general(a, b, (((0,), (0,)), ((), ())), precision=HI, preferred_element_type=F32)


def _sigmoid(x):
    return 1.0 / (1.0 + jnp.exp(-x))


def _softplus(x):
    return jnp.maximum(x, 0.0) + jnp.log(1.0 + jnp.exp(-jnp.abs(x)))


_G0 = math.sqrt(2.0 / math.pi)
_G1 = 0.044715


def _gelu(x):
    t = jnp.tanh(_G0 * (x + _G1 * x * x * x))
    return 0.5 * x * (1.0 + t)


def _gelu_grad(x):
    t = jnp.tanh(_G0 * (x + _G1 * x * x * x))
    return 0.5 * (1.0 + t) + 0.5 * x * (1.0 - t * t) * (_G0 * (1.0 + 3.0 * _G1 * x * x))


def _silu(x):
    return x * _sigmoid(x)


def _silu_grad(x):
    s = _sigmoid(x)
    return s * (1.0 + x * (1.0 - s))


def _rms_stats(x):
    rstd = lax.rsqrt(jnp.mean(x * x, axis=-1, keepdims=True) + RMS_EPS)
    return x * rstd, rstd


def _norm_mod(x, ng, sc, sh):
    xh, _ = _rms_stats(x)
    return xh * (ng * (1.0 + sc)) + sh


def _norm_mod_bwd(dh, x, ng, sc):
    xh, rstd = _rms_stats(x)
    dsh = jnp.sum(dh, axis=0, keepdims=True)
    dsc = jnp.sum(dh * xh, axis=0, keepdims=True) * ng
    dng = jnp.sum(dh * xh, axis=0, keepdims=True) * (1.0 + sc)
    dxh = dh * (ng * (1.0 + sc))
    dx = rstd * (dxh - xh * jnp.mean(dxh * xh, axis=-1, keepdims=True))
    return dx, dsh, dsc, dng


def _iota2(shape, axis):
    return lax.broadcasted_iota(jnp.int32, shape, axis)


def _col(tile, idx):
    return jnp.sum(jnp.where(_iota2(tile.shape, 1) == idx, tile, 0.0), axis=1, keepdims=True)


def _row(tile, idx):
    return jnp.sum(jnp.where(_iota2(tile.shape, 0) == idx, tile, 0.0), axis=0, keepdims=True)


def _put_col(col, idx, width=HD):
    shape = (col.shape[0], width)
    return jnp.where(_iota2(shape, 1) == idx, jnp.broadcast_to(col, shape), 0.0)


def _tri_inverse(m):
    rows, cols = _iota2(m.shape, m.ndim - 2), _iota2(m.shape, m.ndim - 1)
    mm = _bmm3 if m.ndim == 3 else _d3
    eye = jnp.where(rows == cols, 1.0, 0.0).astype(F32)
    n = jnp.where((rows >> 3) == (cols >> 3), -m, 0.0)
    p = eye + n
    n2 = mm(n, n)
    p = p + mm(n2, p)
    n4 = mm(n2, n2)
    p = p + mm(n4, p)
    for shift in (3, 4, 5, 6):
        same_pair = (rows >> (shift + 1)) == (cols >> (shift + 1))
        below = jnp.logical_and(((rows >> shift) & 1) == 1, ((cols >> shift) & 1) == 0)
        off = jnp.where(jnp.logical_and(same_pair, below), m, 0.0)
        p = p - mm(p, mm(off, p))
    return p


def _cparams(sem=None):
    return pltpu.CompilerParams(dimension_semantics=sem, vmem_limit_bytes=VMEM_LIMIT)


def _my_place():
    return lax.axis_index("x"), lax.axis_index("y"), lax.axis_index("c")


def _hop(xi, yi, hop):
    dx, dy = hop
    return (1 - xi if dx else xi), (1 - yi if dy else yi)


def _pair_exchange(x, name):
    def body(x_ref, o_ref, ssem, rsem):
        xi, yi, ci = _my_place()
        cp = pltpu.make_async_remote_copy(x_ref, o_ref, ssem, rsem, device_id=(xi, yi, 1 - ci), device_id_type=MESH)
        cp.start()
        cp.wait()

    return pl.pallas_call(
        body, name=name, out_shape=jax.ShapeDtypeStruct(x.shape, x.dtype),
        in_specs=[pl.BlockSpec(memory_space=pltpu.VMEM)], out_specs=pl.BlockSpec(memory_space=pltpu.VMEM),
        scratch_shapes=[pltpu.SemaphoreType.DMA, pltpu.SemaphoreType.DMA],
        compiler_params=pltpu.CompilerParams(vmem_limit_bytes=VMEM_LIMIT),
    )(x)


def _allgather_start(x_ref, o_ref, ssems, rsems, lsem):
    xi, yi, ci = _my_place()
    me = 2 * xi + yi
    pltpu.make_async_copy(x_ref, o_ref.at[me], lsem).start()
    for k, hop in enumerate(HOPS):
        tx, ty = _hop(xi, yi, hop)
        pltpu.make_async_remote_copy(x_ref, o_ref.at[me], ssems.at[k], rsems.at[k],
                                     device_id=(tx, ty, ci), device_id_type=MESH).start()


def _allgather_finish(x_ref, o_ref, ssems, rsems, lsem):
    xi, yi, ci = _my_place()
    me = 2 * xi + yi
    for k, hop in enumerate(HOPS):
        tx, ty = _hop(xi, yi, hop)
        cp = pltpu.make_async_remote_copy(x_ref, o_ref.at[2 * tx + ty], ssems.at[k], rsems.at[k],
                                          device_id=(tx, ty, ci), device_id_type=MESH)
        cp.wait_recv()
        cp.wait_send()
    pltpu.make_async_copy(x_ref, o_ref.at[me], lsem).wait()


_ALLGATHER_SEMS = [pltpu.SemaphoreType.DMA((3,)), pltpu.SemaphoreType.DMA((3,)), pltpu.SemaphoreType.DMA]


def _chip_allgather(x, name):
    def body(x_ref, o_ref, ssems, rsems, lsem):
        _allgather_start(x_ref, o_ref, ssems, rsems, lsem)
        _allgather_finish(x_ref, o_ref, ssems, rsems, lsem)

    return pl.pallas_call(
        body, name=name, out_shape=jax.ShapeDtypeStruct((N_CHIPS,) + x.shape, x.dtype),
        in_specs=[pl.BlockSpec(memory_space=pltpu.VMEM)], out_specs=pl.BlockSpec(memory_space=pltpu.VMEM),
        scratch_shapes=_ALLGATHER_SEMS, compiler_params=pltpu.CompilerParams(vmem_limit_bytes=VMEM_LIMIT),
    )(x)


def _hbm_specs(n):
    return [pl.BlockSpec(memory_space=pl.ANY)] * n


def _cast_into_slot(w, l, place):
    _, r, c = w.shape
    tr = _row_tile(r)

    def body(p_ref, w_ref, o_ref):
        o_ref[...] = w_ref[...].astype(BF16)

    return pl.pallas_call(
        body, name=f"cast_slot_{r}x{c}_{l}", out_shape=jax.ShapeDtypeStruct((N_CHIPS, r, c), BF16),
        grid_spec=pltpu.PrefetchScalarGridSpec(
            num_scalar_prefetch=1, grid=(r // tr,),
            in_specs=[pl.BlockSpec((None, tr, c), lambda k, pr: (l, k, 0))],
            out_specs=pl.BlockSpec((None, tr, c), lambda k, pr: (pr[0], k, 0))),
        compiler_params=_cparams(("parallel",)),
    )(place, w)


def _halves(ref, ci):
    half = ref.shape[-2] // 2
    return pl.ds(ci * half, half), pl.ds((1 - ci) * half, half)


def _gather_copies(bufs, sems):
    s_ici, r_ici, s_d2d, r_d2d = sems
    xi, yi, ci = _my_place()
    me = 2 * xi + yi
    ici_send, ici_recv, d2d_send, d2d_recv = [], [], [], []
    for i, buf in enumerate(bufs):
        mine, sibs = _halves(buf, ci)
        for k, hop in enumerate(HOPS):
            tx, ty = _hop(xi, yi, hop)
            src = 2 * tx + ty
            ici_send.append(pltpu.make_async_remote_copy(buf.at[me, mine], buf.at[me, mine], s_ici.at[i, k],
                                                         r_ici.at[i, k], device_id=(tx, ty, ci), device_id_type=MESH))
            ici_recv.append(pltpu.make_async_remote_copy(buf.at[src, mine], buf.at[src, mine], s_ici.at[i, k],
                                                         r_ici.at[i, k], device_id=(tx, ty, ci), device_id_type=MESH))
            d2d_send.append(pltpu.make_async_remote_copy(buf.at[src, mine], buf.at[src, mine], s_d2d.at[i, k],
                                                         r_d2d.at[i, k], device_id=(xi, yi, 1 - ci), device_id_type=MESH))
            d2d_recv.append(pltpu.make_async_remote_copy(buf.at[src, sibs], buf.at[src, sibs], s_d2d.at[i, k],
                                                         r_d2d.at[i, k], device_id=(xi, yi, 1 - ci), device_id_type=MESH))
    return ici_send, ici_recv, d2d_send, d2d_recv


def _gather_start(bufs, sems):
    for cp in _gather_copies(bufs, sems)[0]:
        cp.start()


def _gather_forward(bufs, sems):
    _, ici_recv, d2d_send, _ = _gather_copies(bufs, sems)
    for arrived, forward in zip(ici_recv, d2d_send):
        arrived.wait_recv()
        forward.start()


def _gather_finish(bufs, sems):
    ici_send, _, d2d_send, d2d_recv = _gather_copies(bufs, sems)
    for cp in d2d_recv:
        cp.wait_recv()
    for cp in ici_send + d2d_send:
        cp.wait_send()


def _gather_sems(n):
    return [pltpu.SemaphoreType.DMA((n, 3))] * 4


def _weights_allgather(bufs, l):
    n = len(bufs)

    def body(*refs):
        outs, sems = refs[n:2 * n], refs[2 * n:]
        _gather_start(outs, sems)
        _gather_forward(outs, sems)
        _gather_finish(outs, sems)

    return pl.pallas_call(
        body, name=f"weights_allgather_{l}",
        out_shape=[jax.ShapeDtypeStruct(b.shape, b.dtype) for b in bufs],
        in_specs=_hbm_specs(n), out_specs=_hbm_specs(n), input_output_aliases={i: i for i in range(n)},
        scratch_shapes=_gather_sems(n),
    )(*bufs)


def _pair_send_copies(gs, outs, sems):
    ssem, rsem = sems
    xi, yi, ci = _my_place()
    every = pl.ds(0, N_CHIPS)
    return [pltpu.make_async_remote_copy(g.at[every, _halves(g, ci)[1]], o, ssem.at[i], rsem.at[i],
                                         device_id=(xi, yi, 1 - ci), device_id_type=MESH)
            for i, (g, o) in enumerate(zip(gs, outs))]


def _pair_send_shapes(gs):
    return [jax.ShapeDtypeStruct((N_CHIPS, g.shape[1] // 2, g.shape[2]), g.dtype) for g in gs]


def _pair_send_sems(n):
    return [pltpu.SemaphoreType.DMA((n,)), pltpu.SemaphoreType.DMA((n,))]


def _grads_pair_send(gs, l):
    n = len(gs)

    def body(*refs):
        cps = _pair_send_copies(refs[:n], refs[n:2 * n], refs[2 * n:])
        for cp in cps:
            cp.start()
        for cp in cps:
            cp.wait()

    return pl.pallas_call(
        body, name=f"grads_pair_send_{l}", out_shape=_pair_send_shapes(gs),
        in_specs=_hbm_specs(n), out_specs=_hbm_specs(n), scratch_shapes=_pair_send_sems(n),
    )(*gs)


def _exchange_copies(ps, recvs, sems):
    ssems, rsems = sems
    xi, yi, ci = _my_place()
    cps = []
    for i, (p, rc) in enumerate(zip(ps, recvs)):
        for k, hop in enumerate(HOPS):
            tx, ty = _hop(xi, yi, hop)
            cps.append(pltpu.make_async_remote_copy(p.at[2 * tx + ty], rc.at[k], ssems.at[i, k], rsems.at[i, k],
                                                    device_id=(tx, ty, ci), device_id_type=MESH))
    return cps


def _exchange_sems(n):
    return [pltpu.SemaphoreType.DMA((n, 3))] * 2


def _exchange_shapes(ps):
    return [jax.ShapeDtypeStruct((3,) + p.shape[1:], p.dtype) for p in ps]


def _pair_share_copies(bufs, sems):
    ssem, rsem = sems
    xi, yi, ci = _my_place()
    every = pl.ds(0, DEPTH)
    sends, arrivals = [], []
    for i, buf in enumerate(bufs):
        mine, sibs = _halves(buf, ci)
        sends.append(pltpu.make_async_remote_copy(buf.at[every, mine], buf.at[every, mine], ssem.at[i], rsem.at[i],
                                                  device_id=(xi, yi, 1 - ci), device_id_type=MESH))
        arrivals.append(pltpu.make_async_remote_copy(buf.at[every, sibs], buf.at[every, sibs], ssem.at[i], rsem.at[i],
                                                     device_id=(xi, yi, 1 - ci), device_id_type=MESH))
    return sends, arrivals


def _pair_share_start(bufs, sems):
    for cp in _pair_share_copies(bufs, sems)[0]:
        cp.start()


def _pair_share_finish(bufs, sems):
    sends, arrivals = _pair_share_copies(bufs, sems)
    for cp in arrivals:
        cp.wait_recv()
    for cp in sends:
        cp.wait_send()


def _grads_pair_share(gs, name):
    n = len(gs)

    def body(*refs):
        _pair_share_start(refs[n:2 * n], refs[2 * n:])
        _pair_share_finish(refs[n:2 * n], refs[2 * n:])

    return pl.pallas_call(
        body, name=name,
        out_shape=[jax.ShapeDtypeStruct(g.shape, g.dtype) for g in gs],
        in_specs=_hbm_specs(n), out_specs=_hbm_specs(n), input_output_aliases={i: i for i in range(n)},
        scratch_shapes=_pair_send_sems(n),
    )(*gs)


def _row_tile(r):
    return min(r, 512)


def _pair_sum(g, ga, place, name):
    _, r, c = g.shape
    tr = _row_tile(r // 2)
    nk = r // 2 // tr

    def body(p_ref, g_ref, ga_ref, o_ref):
        o_ref[...] = (g_ref[...].astype(F32) + ga_ref[...].astype(F32)).astype(o_ref.dtype)

    return pl.pallas_call(
        body, name=name, out_shape=jax.ShapeDtypeStruct(ga.shape, ga.dtype),
        grid_spec=pltpu.PrefetchScalarGridSpec(
            num_scalar_prefetch=1, grid=(N_CHIPS, nk),
            in_specs=[pl.BlockSpec((None, tr, c), lambda j, k, pr: (j, pr[1] * nk + k, 0)),
                      pl.BlockSpec((None, tr, c), lambda j, k, pr: (j, k, 0))],
            out_specs=pl.BlockSpec((None, tr, c), lambda j, k, pr: (j, k, 0))),
        compiler_params=_cparams(("parallel", "parallel")),
    )(place, g, ga)


def _chip_sum(pair, recv, buf, l, place, name):
    _, rh, c = pair.shape
    tr = _row_tile(rh)
    nk = rh // tr

    def body(p_ref, own_ref, r_ref, *rest):
        o_ref = rest[-1]
        acc = own_ref[...].astype(F32) + r_ref[0].astype(F32)
        acc = acc + r_ref[1].astype(F32)
        o_ref[...] = acc + r_ref[2].astype(F32)

    in_specs = [pl.BlockSpec((None, tr, c), lambda k, pr: (pr[0], k, 0)),
                pl.BlockSpec((3, tr, c), lambda k, pr: (0, k, 0))]
    args = [pair, recv]
    aliases = {}
    if buf is not None:
        in_specs.append(pl.BlockSpec(memory_space=pl.ANY))
        args.append(buf)
        aliases = {3: 0}
    return pl.pallas_call(
        body, name=name, out_shape=jax.ShapeDtypeStruct((DEPTH, 2 * rh, c), F32),
        grid_spec=pltpu.PrefetchScalarGridSpec(
            num_scalar_prefetch=1, grid=(nk,), in_specs=in_specs,
            out_specs=pl.BlockSpec((None, tr, c), lambda k, pr: (l, pr[1] * nk + k, 0))),
        input_output_aliases=aliases, compiler_params=_cparams(("parallel",)),
    )(place, *args)


def _adam_math(w, g, m, v):
    m = B1 * m + (1.0 - B1) * g
    v = B2 * v + (1.0 - B2) * (g * g)
    m_hat = m / (1.0 - B1 ** STEP)
    v_hat = v / (1.0 - B2 ** STEP)
    delta = -LR * (m_hat / (jnp.sqrt(v_hat) + ADAM_EPS) + WD * w)
    return delta, m, v


def _adamw(w, g, m, v, name):
    n_l, r, c = w.shape
    tr = _row_tile(r)

    def body(w_ref, g_ref, m_ref, v_ref, d_ref, nm_ref, nv_ref):
        d_ref[...], nm_ref[...], nv_ref[...] = _adam_math(w_ref[...], g_ref[...], m_ref[...], v_ref[...])

    spec = pl.BlockSpec((None, tr, c), lambda i, k: (i, k, 0))
    return pl.pallas_call(
        body, name=name, out_shape=[jax.ShapeDtypeStruct(w.shape, F32)] * 3, grid=(n_l, r // tr),
        in_specs=[spec] * 4, out_specs=[spec] * 3, compiler_params=_cparams(("parallel", "parallel")),
    )(w, g, m, v)


def _ada_forward(c_all, w_ada, b_cols):
    cols = w_ada.shape[2]
    tn = 512

    def body(c_ref, w_ref, b_ref, o_ref):
        o_ref[...] = _dotf(_silu(c_ref[...]), w_ref[...]) + b_ref[...]

    return pl.pallas_call(
        body, name="ada_forward", out_shape=jax.ShapeDtypeStruct((DEPTH, 8, cols), F32), grid=(DEPTH, cols // tn),
        in_specs=[pl.BlockSpec((8, D), lambda l, j: (0, 0)),
                  pl.BlockSpec((None, D, tn), lambda l, j: (l, 0, j)),
                  pl.BlockSpec((None, 1, tn), lambda l, j: (l, 0, j))],
        out_specs=pl.BlockSpec((None, 8, tn), lambda l, j: (l, 0, j)),
        compiler_params=_cparams(("parallel", "parallel")),
    )(c_all, w_ada, b_cols.reshape(DEPTH, 1, cols))


def _ada_backward_adamw(c_all, dmod_cols, w, m, v):
    cols = w.shape[2]
    tn = 512

    def body(c_ref, d_ref, w_ref, m_ref, v_ref, g_ref, dl_ref, nm_ref, nv_ref):
        g = _dotf_tn(_silu(c_ref[...]), d_ref[...])
        g_ref[...] = g
        dl_ref[...], nm_ref[...], nv_ref[...] = _adam_math(w_ref[...], g, m_ref[...], v_ref[...])

    wspec = pl.BlockSpec((None, D, tn), lambda l, j: (l, 0, j))
    return pl.pallas_call(
        body, name="ada_backward_adamw", out_shape=[jax.ShapeDtypeStruct(w.shape, F32)] * 4, grid=(DEPTH, cols // tn),
        in_specs=[pl.BlockSpec((8, D), lambda l, j: (0, 0)), pl.BlockSpec((None, 8, tn), lambda l, j: (l, 0, j)),
                  wspec, wspec, wspec],
        out_specs=[wspec] * 4, compiler_params=_cparams(("parallel", "parallel")),
    )(c_all, dmod_cols, w, m, v)


def _tok_tile(t):
    return min(t, 512)


def _wspec4(r, c, l):
    return pl.BlockSpec((N_CHIPS, r, c), lambda i: (0, 0, 0))


def _fwd_in(x, modv, w_in, l):
    t = x.shape[0]
    tm = _tok_tile(t)

    def body(x_ref, mod_ref, w_ref, o_ref, h_ref):
        h = _norm_mod(x_ref[...], mod_ref[6:7, :], mod_ref[1:2, :], mod_ref[0:1, :]).astype(BF16)
        h_ref[...] = h
        o_ref[...] = jnp.dot(h, w_ref[...], preferred_element_type=F32)

    return pl.pallas_call(
        body, name=f"fwd_in_{l}", grid=(t // tm,),
        out_shape=[jax.ShapeDtypeStruct((t, NW), F32), jax.ShapeDtypeStruct((t, D), BF16)],
        in_specs=[pl.BlockSpec((tm, D), lambda i: (i, 0)), pl.BlockSpec((None, 8, D), lambda i: (l, 0, 0)),
                  pl.BlockSpec((D, NW), lambda i: (0, 0))],
        out_specs=[pl.BlockSpec((tm, NW), lambda i: (i, 0)), pl.BlockSpec((tm, D), lambda i: (i, 0))],
        compiler_params=_cparams(("parallel",)),
    )(x, modv, w_in)


def _fwd_out_ff1(x, mix, modv, w_out, w_ff1, l):
    t = x.shape[0]
    tm = _tok_tile(t)

    def body(x_ref, mix_ref, mod_ref, wo_ref, w_ref, x1_ref, o_ref, h_ref):
        x1 = x_ref[...] + mod_ref[2:3, :] * jnp.dot(mix_ref[...], wo_ref[...].reshape(D, D), preferred_element_type=F32)
        x1_ref[...] = x1
        h = _norm_mod(x1, mod_ref[7:8, :], mod_ref[4:5, :], mod_ref[3:4, :]).astype(BF16)
        h_ref[...] = h
        for j in range(N_CHIPS):
            f = jnp.dot(h, w_ref[j], preferred_element_type=F32)
            o_ref[:, j * D:(j + 1) * D] = jnp.maximum(f, 0.0).astype(BF16)

    tok = pl.BlockSpec((tm, D), lambda i: (i, 0))
    return pl.pallas_call(
        body, name=f"fwd_out_ff1_{l}", grid=(t // tm,),
        out_shape=[jax.ShapeDtypeStruct((t, D), F32), jax.ShapeDtypeStruct((t, DFF), BF16),
                   jax.ShapeDtypeStruct((t, D), BF16)],
        in_specs=[tok, tok, pl.BlockSpec((None, 8, D), lambda i: (l, 0, 0)), _wspec4(D // N_CHIPS, D, l),
                  _wspec4(D, D, l)],
        out_specs=[tok, pl.BlockSpec((tm, DFF), lambda i: (i, 0)), tok],
        compiler_params=_cparams(("parallel",)),
    )(x, mix, modv, w_out, w_ff1)


def _fwd_ff2(x, r, modv, w_ff2, l):
    t = x.shape[0]
    tm = _tok_tile(t)

    def body(x_ref, r_ref, mod_ref, w_ref, o_ref):
        acc = jnp.zeros((tm, D), F32)
        for j in range(N_CHIPS):
            rj = r_ref[:, j * D:(j + 1) * D].astype(F32)
            acc = acc + jnp.dot((rj * rj).astype(BF16), w_ref[j], preferred_element_type=F32)
        o_ref[...] = x_ref[...] + mod_ref[5:6, :] * acc

    return pl.pallas_call(
        body, name=f"fwd_ff2_{l}", out_shape=jax.ShapeDtypeStruct((t, D), F32), grid=(t // tm,),
        in_specs=[pl.BlockSpec((tm, D), lambda i: (i, 0)), pl.BlockSpec((tm, DFF), lambda i: (i, 0)),
                  pl.BlockSpec((None, 8, D), lambda i: (l, 0, 0)), _wspec4(D, D, l)],
        out_specs=pl.BlockSpec((tm, D), lambda i: (i, 0)), compiler_params=_cparams(("parallel",)),
    )(x, r, modv, w_ff2)


def _loss_head(x, target, final_g):
    t = x.shape[0]
    tm = _tok_tile(t)

    def body(x_ref, t_ref, g_ref, dx_ref, st_ref):
        @pl.when(pl.program_id(0) == 0)
        def _():
            st_ref[...] = jnp.zeros_like(st_ref)

        xh, rstd = _rms_stats(x_ref[...])
        g = g_ref[...]
        err = xh * g - t_ref[...]
        loss = 0.5 * jnp.sum(jnp.mean(err * err, axis=-1, keepdims=True), axis=0, keepdims=True)
        dy = err * (1.0 / D)
        st_ref[0:1, :] += jnp.sum(dy * xh, axis=0, keepdims=True)
        st_ref[1:2, :] += jnp.broadcast_to(loss, (1, D))
        dxh = dy * g
        dx_ref[...] = rstd * (dxh - xh * jnp.mean(dxh * xh, axis=-1, keepdims=True))

    return pl.pallas_call(
        body, name="loss_head", out_shape=[jax.ShapeDtypeStruct((t, D), F32), jax.ShapeDtypeStruct((8, D), F32)],
        grid=(t // tm,),
        in_specs=[pl.BlockSpec((tm, D), lambda i: (i, 0)), pl.BlockSpec((tm, D), lambda i: (i, 0)),
                  pl.BlockSpec((1, D), lambda i: (0, 0))],
        out_specs=[pl.BlockSpec((tm, D), lambda i: (i, 0)), pl.BlockSpec((8, D), lambda i: (0, 0))],
        compiler_params=_cparams(("arbitrary",)),
    )(x, target, final_g.reshape(1, D))


def _bwd_ff2(dx2, r, modv, w_ff2, l):
    t = dx2.shape[0]
    tm = _tok_tile(t)

    def body(d_ref, r_ref, mod_ref, w_ref, o_ref):
        dyg = (d_ref[...] * mod_ref[5:6, :]).astype(BF16)
        for j in range(N_CHIPS):
            da = lax.dot_general(dyg, w_ref[j], (((1,), (1,)), ((), ())), preferred_element_type=F32)
            o_ref[:, j * D:(j + 1) * D] = (da * 2.0 * r_ref[:, j * D:(j + 1) * D].astype(F32)).astype(BF16)

    return pl.pallas_call(
        body, name=f"bwd_ff2_{l}", out_shape=jax.ShapeDtypeStruct((t, DFF), BF16), grid=(t // tm,),
        in_specs=[pl.BlockSpec((tm, D), lambda i: (i, 0)), pl.BlockSpec((tm, DFF), lambda i: (i, 0)),
                  pl.BlockSpec((None, 8, D), lambda i: (l, 0, 0)), _wspec4(D, D, l)],
        out_specs=pl.BlockSpec((tm, DFF), lambda i: (i, 0)), compiler_params=_cparams(("parallel",)),
    )(dx2, r, modv, w_ff2)


def _bwd_norm(dy, w, x, dres, modv, l, which, send=(), w_out=None, share=(), exchange=()):
    t = x.shape[0]
    tm = _tok_tile(t)
    nsteps = t // tm
    rows = (6, 1) if which == "in" else (7, 4)
    width = dy.shape[1]
    ns, nh, ne = len(send), len(share), len(exchange)
    nb_ = 0 if w_out is None else 1
    n_in = 5 + nb_ + ns + nh + ne

    def body(*refs):
        dy_ref, w_ref, x_ref, dr_ref, mod_ref = refs[:5]
        wo_ref = refs[5] if nb_ else None
        parts = refs[5 + nb_:5 + nb_ + ns]
        pairs = refs[5 + nb_ + ns + nh:n_in]
        dx_ref, st_ref = refs[n_in:n_in + 2]
        dmix_ref = refs[n_in + 2] if nb_ else None
        o0 = n_in + 2 + nb_
        from_sib, shared, recvs = refs[o0:o0 + ns], refs[o0 + ns:o0 + ns + nh], refs[o0 + ns + nh:o0 + ns + nh + ne]
        scratch = list(refs[o0 + ns + nh + ne:])
        sems = [scratch.pop(0) for _ in range(2 if ns else 0)]
        hsems = [scratch.pop(0) for _ in range(2 if nh else 0)]
        esems = scratch

        @pl.when(pl.program_id(0) == 0)
        def _():
            st_ref[...] = jnp.zeros_like(st_ref)
            if ns:
                for cp in _pair_send_copies(parts, from_sib, sems):
                    cp.start()
            if nh:
                _pair_share_start(shared, hsems)
            if ne:
                for cp in _exchange_copies(pairs, recvs, esems):
                    cp.start()

        if which == "in":
            dh = lax.dot_general(dy_ref[...], w_ref[...], (((1,), (1,)), ((), ())), preferred_element_type=F32)
        else:
            dh = jnp.zeros((tm, D), F32)
            for j in range(N_CHIPS):
                dh = dh + lax.dot_general(dy_ref[:, j * D:(j + 1) * D], w_ref[j], (((1,), (1,)), ((), ())),
                                          preferred_element_type=F32)
        ng, sc = mod_ref[rows[0]:rows[0] + 1, :], mod_ref[rows[1]:rows[1] + 1, :]
        dx, dsh, dsc, dng = _norm_mod_bwd(dh, x_ref[...], ng, sc)
        dx_new = dr_ref[...] + dx
        dx_ref[...] = dx_new
        st_ref[0:1, :] += dsh
        st_ref[1:2, :] += dsc
        st_ref[2:3, :] += dng
        if nb_:
            dyg = (dx_new * mod_ref[2:3, :]).astype(BF16)
            dmix_ref[...] = lax.dot_general(dyg, wo_ref[...].reshape(D, D), (((1,), (1,)), ((), ())),
                                            preferred_element_type=F32).astype(BF16)

        if ns:
            @pl.when(pl.program_id(0) == nsteps - 1)
            def _():
                for cp in _pair_send_copies(parts, from_sib, sems):
                    cp.wait()

        if nh:
            @pl.when(pl.program_id(0) == nsteps - 1)
            def _():
                _pair_share_finish(shared, hsems)

        if ne:
            @pl.when(pl.program_id(0) == nsteps - 1)
            def _():
                for cp in _exchange_copies(pairs, recvs, esems):
                    cp.wait()

    tok = pl.BlockSpec((tm, D), lambda i: (i, 0))
    wspec = pl.BlockSpec((D, NW), lambda i: (0, 0)) if which == "in" else _wspec4(D, D, l)
    return pl.pallas_call(
        body, name=f"bwd_norm_{which}_{l}",
        out_shape=[jax.ShapeDtypeStruct((t, D), F32), jax.ShapeDtypeStruct((8, D), F32)]
        + [jax.ShapeDtypeStruct((t, D), BF16)] * nb_ + _pair_send_shapes(send)
        + [jax.ShapeDtypeStruct(g.shape, g.dtype) for g in share] + _exchange_shapes(exchange),
        grid=(nsteps,),
        in_specs=[pl.BlockSpec((tm, width), lambda i: (i, 0)), wspec, tok, tok,
                  pl.BlockSpec((None, 8, D), lambda i: (l, 0, 0))]
        + [_wspec4(D // N_CHIPS, D, l)] * nb_ + _hbm_specs(ns + nh + ne),
        out_specs=[tok, pl.BlockSpec((8, D), lambda i: (0, 0))] + [tok] * nb_ + _hbm_specs(ns + nh + ne),
        input_output_aliases={5 + nb_ + ns + i: 2 + nb_ + ns + i for i in range(nh)},
        scratch_shapes=(_pair_send_sems(ns) if ns else []) + (_pair_send_sems(nh) if nh else [])
        + (_exchange_sems(ne) if ne else []),
        compiler_params=_cparams(("arbitrary",)),
    )(dy, w, x, dres, modv, *([w_out] * nb_), *send, *share, *exchange)


def _grad_weight(lhs, rhs, modv, l, which, w_gate=None):
    t = lhs.shape[0]
    tm = min(t, 2048)
    nt = t // tm
    gated = which in ("out", "ff2")
    if which == "in":
        nj, lw, rw, orows, ocols = 5, D, NW // 5, D, NW // 5
    elif which == "ff1":
        nj, lw, rw, orows, ocols = N_CHIPS, D, D, D, D
    elif which == "out":
        nj, lw, rw, orows, ocols = 1, D, D, D, D
    else:
        nj, lw, rw, orows, ocols = N_CHIPS, D, D, D, D
    gate_row = 2 if which == "out" else 5

    def body(*refs):
        if gated:
            l_ref, r_ref, mod_ref, wg_ref, o_ref, dg_ref, acc = refs
        else:
            l_ref, r_ref, mod_ref, o_ref, acc = refs
        j, k = pl.program_id(0), pl.program_id(1)

        @pl.when(k == 0)
        def _():
            acc[...] = jnp.zeros_like(acc)

        if which == "ff2":
            lv = l_ref[...].astype(F32)
            lv = lv * lv
        else:
            lv = l_ref[...]
        acc[...] += _dot_tn(lv, r_ref[...])

        if gated:
            @pl.when(jnp.logical_and(j == 0, k == 0))
            def _():
                dg_ref[...] = jnp.zeros_like(dg_ref)

        @pl.when(k == nt - 1)
        def _():
            raw = acc[...]
            if gated:
                o_ref[...] = (raw * mod_ref[gate_row:gate_row + 1, :]).astype(o_ref.dtype)
                dg_ref[0:1, :] += jnp.sum(raw * wg_ref[...].astype(F32), axis=0, keepdims=True)
            else:
                o_ref[...] = raw.astype(o_ref.dtype)

    if which in ("in", "ff1"):
        lspec = pl.BlockSpec((tm, lw), lambda j, k: (k, 0))
        rspec = pl.BlockSpec((tm, rw), lambda j, k: (k, j))
    else:
        lspec = pl.BlockSpec((tm, lw), lambda j, k: (k, j))
        rspec = pl.BlockSpec((tm, rw), lambda j, k: (k, 0))
    mspec = pl.BlockSpec((None, 8, D), lambda j, k: (l, 0, 0))
    flat = which in ("in", "out")
    if flat:
        ospec = pl.BlockSpec((orows, ocols), lambda j, k: (0, j))
        out_shape = [jax.ShapeDtypeStruct((D, nj * ocols), BF16)]
    else:
        ospec = pl.BlockSpec((None, orows, ocols), lambda j, k: (j, 0, 0))
        out_shape = [jax.ShapeDtypeStruct((N_CHIPS, orows, ocols), BF16)]
    in_specs = [lspec, rspec, mspec]
    args = [lhs, rhs, modv]
    out_specs = [ospec]
    if gated:
        if flat:
            in_specs.append(pl.BlockSpec((orows, ocols), lambda j, k: (0, 0)))
            args.append(w_gate.reshape(D, D))
        else:
            in_specs.append(pl.BlockSpec((None, orows, ocols), lambda j, k: (j, 0, 0)))
            args.append(w_gate)
        out_specs.append(pl.BlockSpec((8, D), lambda j, k: (0, 0)))
        out_shape.append(jax.ShapeDtypeStruct((8, D), F32))
    res = pl.pallas_call(
        body, name=f"grad_w_{which}_{l}", out_shape=out_shape, grid=(nj, nt), in_specs=in_specs, out_specs=out_specs,
        scratch_shapes=[pltpu.VMEM((orows, ocols), F32)], compiler_params=_cparams(("arbitrary", "arbitrary")),
    )(*args)
    return (res[0], res[1]) if gated else (res[0], None)


def _tri_masks():
    rows, cols = _iota2((BLK, BLK), 0), _iota2((BLK, BLK), 1)
    return rows >= cols, rows > cols


def _sgu_forward(p_ref, lnp_ref, sguw_ref, sgub_ref):
    incl, _ = _tri_masks()
    ug = _gelu(p_ref[:, 0:512])
    vg = _gelu(p_ref[:, 512:1024])
    mu = jnp.mean(vg, axis=-1, keepdims=True)
    xc = vg - mu
    rstd = lax.rsqrt(jnp.mean(xc * xc, axis=-1, keepdims=True) + LN_EPS)
    vhat = xc * rstd
    vn = vhat * lnp_ref[0:1, :] + lnp_ref[1:2, :]
    bias = sgub_ref[...]
    ys, mixed, wms = [], [], []
    for h in range(HEADS):
        wm = jnp.where(incl, sguw_ref[h], 0.0)
        mx = _dot(wm, vn[:, h * HD:(h + 1) * HD]) + _col(bias, h)
        ys.append(ug[:, h * HD:(h + 1) * HD] * mx)
        mixed.append(mx)
        wms.append(wm)
    return ys, ug, vhat, rstd, vn, mixed, wms


def _conv_forward(xbuf, cw_ref):
    conv = cw_ref[0:1, :] * xbuf[5:5 + BLK, :]
    for j in range(1, 4):
        conv = conv + cw_ref[j:j + 1, :] * xbuf[5 + j:5 + j + BLK, :]
    return conv


def _gates(gt, gv_ref):
    incl, _ = _tri_masks()
    beta = _sigmoid(gt)
    neg_a = -jnp.exp(gv_ref[0:1, :])
    gl = neg_a * _softplus(gt + gv_ref[1:2, :])
    gc = _dotf(jnp.where(incl, 1.0, 0.0).astype(F32), gl)
    return beta, gl, gc, gc.T, neg_a


def _head_chunk(act, beta, gc, gct, h):
    incl, strict = _tri_masks()
    qh = act[:, h * HD:(h + 1) * HD]
    kh = act[:, 512 + h * HD:512 + (h + 1) * HD]
    vh = act[:, 1024 + h * HD:1024 + (h + 1) * HD]
    rq = lax.rsqrt(jnp.sum(qh * qh, axis=-1, keepdims=True) + RMS_EPS)
    rk = lax.rsqrt(jnp.sum(kh * kh, axis=-1, keepdims=True) + RMS_EPS)
    qhat, khat = qh * rq, kh * rk
    qn = qhat * QK_SCALE
    b = _col(beta, h)
    gcol = _col(gc, 4 + h)
    grow = _row(gct, 4 + h)
    dmat = jnp.where(incl, jnp.exp(jnp.where(incl, gcol - grow, 0.0)), 0.0)
    gam = jnp.exp(gcol)
    glast = _row(gcol, BLK - 1)
    e = jnp.exp(glast - gcol)
    kk = _d3_nt(khat, khat)
    return dict(qhat=qhat, khat=khat, qn=qn, vh=vh, rq=rq, rk=rk, b=b, dmat=dmat, gam=gam, glast=glast, e=e, kk=kk,
                strict=strict, incl=incl)


def _mixer_forward(p, lnp, sgu_w, sgu_bt, cw, gv, l, gather=()):
    t = p.shape[0]
    nb = t // BLK
    ng = len(gather)

    def body(*refs):
        p_ref, lnp_ref, sguw_ref, sgub_ref, cw_ref, gv_ref = refs[:6]
        mix_ref, s_out, t_out, u_out, w_out, o_out, conv_out = refs[6 + ng:13 + ng]
        gbufs = refs[13 + ng:13 + 2 * ng]
        s_scr, xbuf = refs[13 + 2 * ng:15 + 2 * ng]
        gsems = refs[15 + 2 * ng:]

        @pl.when(pl.program_id(0) == 0)
        def _():
            s_scr[...] = jnp.zeros_like(s_scr)
            xbuf[0:8, :] = jnp.zeros((8, 1536), F32)
            if ng:
                _gather_start(gbufs, gsems)

        ys = _sgu_forward(p_ref, lnp_ref, sguw_ref, sgub_ref)[0]
        for h in range(HEADS):
            mix_ref[:, h * HD:(h + 1) * HD] = ys[h].astype(BF16)

        xbuf[8:8 + BLK, :] = p_ref[:, 1024:2560]
        conv = _conv_forward(xbuf, cw_ref)
        conv_out[...] = conv
        act = _silu(conv)
        xbuf[0:8, :] = xbuf[BLK:BLK + 8, :]
        beta, _, gc, gct, _ = _gates(p_ref[:, GATE0:NW], gv_ref)
        chunks = [_head_chunk(act, beta, gc, gct, h) for h in range(HEADS)]
        for h, hc in enumerate(chunks):
            t_out[h] = jnp.where(hc["strict"], hc["b"] * hc["kk"] * hc["dmat"], 0.0)
        t_out[...] = _tri_inverse(t_out[...])
        for h, hc in enumerate(chunks):
            tm = t_out[h]
            u = _dot(tm, hc["b"] * hc["vh"])
            w = _dot(tm, (hc["b"] * hc["gam"]) * hc["khat"])
            qkm = _dot_nt(hc["qn"], hc["khat"]) * hc["dmat"]
            s = s_scr[h]
            wn = u - _dot(w, s)
            o = _dot(hc["qn"] * hc["gam"], s) + _dot(qkm, wn)
            s_out[h] = s
            s_scr[h] = jnp.exp(hc["glast"]) * s + _dot_tn(hc["khat"] * hc["e"], wn)
            sl = slice(h * HD, (h + 1) * HD)
            u_out[:, sl] = u
            w_out[:, sl] = w
            o_out[:, sl] = o
            on = o * lax.rsqrt(jnp.mean(o * o, axis=-1, keepdims=True) + RMS_EPS) * gv_ref[2:3, :]
            mix_ref[:, 512 + h * HD:512 + (h + 1) * HD] = (on * _silu(p_ref[:, 2560 + h * HD:2560 + (h + 1) * HD])).astype(BF16)

        if ng:
            @pl.when(pl.program_id(0) == nb - 1 - min(3, nb - 1))
            def _():
                _gather_forward(gbufs, gsems)

            @pl.when(pl.program_id(0) == nb - 1)
            def _():
                _gather_finish(gbufs, gsems)

    tok = lambda w: pl.BlockSpec((BLK, w), lambda i: (i, 0))
    st = pl.BlockSpec((None, HEADS, HD, HD), lambda i: (i, 0, 0, 0))
    return pl.pallas_call(
        body, name=f"mixer_fwd_{l}", grid=(nb,),
        out_shape=[jax.ShapeDtypeStruct((t, D), BF16), jax.ShapeDtypeStruct((nb, HEADS, HD, HD), F32),
                   jax.ShapeDtypeStruct((nb, HEADS, HD, HD), F32), jax.ShapeDtypeStruct((t, 512), F32),
                   jax.ShapeDtypeStruct((t, 512), F32), jax.ShapeDtypeStruct((t, 512), F32),
                   jax.ShapeDtypeStruct((t, 1536), F32)]
        + [jax.ShapeDtypeStruct(b.shape, b.dtype) for b in gather],
        in_specs=[tok(NW), pl.BlockSpec((None, 8, 512), lambda i: (l, 0, 0)),
                  pl.BlockSpec((None, HEADS, HD, HD), lambda i: (l, 0, 0, 0)),
                  pl.BlockSpec((None, HD, HD), lambda i: (l, 0, 0)), pl.BlockSpec((None, 8, 1536), lambda i: (l, 0, 0)),
                  pl.BlockSpec((None, 8, HD), lambda i: (l, 0, 0))] + _hbm_specs(ng),
        out_specs=[tok(D), st, st, tok(512), tok(512), tok(512), tok(1536)]
        + _hbm_specs(ng),
        input_output_aliases={6 + i: 7 + i for i in range(ng)},
        scratch_shapes=[pltpu.VMEM((HEADS, HD, HD), F32), pltpu.VMEM((BLK + 8, 1536), F32)]
        + (_gather_sems(ng) if ng else []),
        compiler_params=_cparams(("arbitrary",)),
    )(p, lnp, sgu_w, sgu_bt, cw, gv, *gather)


def _mixer_backward(p, dmix, saved, lnp, sgu_w, sgu_bt, cw, gv, l, exchange=()):
    t = p.shape[0]
    nb = t // BLK
    s_sv, t_sv, u_sv, w_sv, o_sv, conv_sv = saved
    ne = len(exchange)

    def body(*refs):
        (p_ref, dmix_ref, s_ref, t_ref, u_ref, w_ref, o_ref, conv_ref, lnp_ref, sguw_ref, sgub_ref, cw_ref,
         gv_ref) = refs[:13]
        pairs = refs[13:13 + ne]
        dp_ref, dlnp_ref, dsguw_ref, dsgub_ref, dcw_ref, dgv_ref = refs[13 + ne:19 + ne]
        recvs = refs[19 + ne:19 + 2 * ne]
        ds_scr, dcbuf = refs[19 + 2 * ne:21 + 2 * ne]
        esems = refs[21 + 2 * ne:]

        @pl.when(pl.program_id(0) == 0)
        def _():
            if ne:
                for cp in _exchange_copies(pairs, recvs, esems):
                    cp.start()
            ds_scr[...] = jnp.zeros_like(ds_scr)
            dcbuf[BLK:BLK + 8, :] = jnp.zeros((8, 1536), F32)
            dlnp_ref[...] = jnp.zeros_like(dlnp_ref)
            dsguw_ref[...] = jnp.zeros_like(dsguw_ref)
            dsgub_ref[...] = jnp.zeros_like(dsgub_ref)
            dcw_ref[...] = jnp.zeros_like(dcw_ref)
            dgv_ref[...] = jnp.zeros_like(dgv_ref)

        incl, strict = _tri_masks()
        _, ug, vhat, rstd, vn, mixed, wms = _sgu_forward(p_ref, lnp_ref, sguw_ref, sgub_ref)
        dvn_parts, dug_parts = [], []
        dbias = jnp.zeros((BLK, HD), F32)
        for h in range(HEADS):
            sl = slice(h * HD, (h + 1) * HD)
            dy = dmix_ref[:, sl].astype(F32)
            dmx = dy * ug[:, sl]
            dug_parts.append(dy * mixed[h])
            dsguw_ref[h] += jnp.where(incl, _dot_nt(dmx, vn[:, sl]), 0.0)
            dbias = dbias + _put_col(jnp.sum(dmx, axis=1, keepdims=True), h)
            dvn_parts.append(_dot_tn(wms[h], dmx))
        dsgub_ref[...] += dbias
        dvn = jnp.concatenate(dvn_parts, axis=1)
        dug = jnp.concatenate(dug_parts, axis=1)
        dlnp_ref[0:1, :] += jnp.sum(dvn * vhat, axis=0, keepdims=True)
        dlnp_ref[1:2, :] += jnp.sum(dvn, axis=0, keepdims=True)
        dvhat = dvn * lnp_ref[0:1, :]
        dvg = rstd * (dvhat - jnp.mean(dvhat, axis=-1, keepdims=True)
                      - vhat * jnp.mean(dvhat * vhat, axis=-1, keepdims=True))
        dp_ref[:, 0:512] = (dug * _gelu_grad(p_ref[:, 0:512])).astype(BF16)
        dp_ref[:, 512:1024] = (dvg * _gelu_grad(p_ref[:, 512:1024])).astype(BF16)

        conv = conv_ref[...]
        act = _silu(conv)
        gt = p_ref[:, GATE0:NW]
        beta, gl, gc, gct, neg_a = _gates(gt, gv_ref)
        gng = gv_ref[2:3, :]
        dbeta_t = jnp.zeros((BLK, HD), F32)
        dgc_t = jnp.zeros((BLK, HD), F32)
        dgng = jnp.zeros((1, HD), F32)
        for h in range(HEADS):
            sl = slice(h * HD, (h + 1) * HD)
            hc = _head_chunk(act, beta, gc, gct, h)
            b, gam, e, dmat, kk = hc["b"], hc["gam"], hc["e"], hc["dmat"], hc["kk"]
            qn, khat, vh = hc["qn"], hc["khat"], hc["vh"]
            gamlast = jnp.exp(hc["glast"])
            s, tm, u, w, o = s_ref[h], t_ref[h], u_ref[:, sl], w_ref[:, sl], o_ref[:, sl]
            ds_next = ds_scr[h]
            z = p_ref[:, 2560 + h * HD:2560 + (h + 1) * HD]
            dy = dmix_ref[:, 512 + h * HD:512 + (h + 1) * HD].astype(F32)
            ro = lax.rsqrt(jnp.mean(o * o, axis=-1, keepdims=True) + RMS_EPS)
            ohat = o * ro
            dp_ref[:, 2560 + h * HD:2560 + (h + 1) * HD] = (dy * ohat * gng * _silu_grad(z)).astype(BF16)
            don = dy * _silu(z)
            dgng = dgng + jnp.sum(don * ohat, axis=0, keepdims=True)
            dohat = don * gng
            do = ro * (dohat - ohat * jnp.mean(dohat * ohat, axis=-1, keepdims=True))
            qk_raw = _dot_nt(qn, khat)
            qkm = qk_raw * dmat
            qd, kd = qn * gam, khat * e
            wn = u - _dot(w, s)
            dwn = _dot_tn(qkm, do) + _dot(kd, ds_next)
            dqd = _dot_nt(do, s)
            dqkm = jnp.where(incl, _dot_nt(do, wn), 0.0)
            ds_scr[h] = _dot_tn(qd, do) + gamlast * ds_next - _dot_tn(w, dwn)
            dgamlast = jnp.sum(jnp.sum(ds_next * s, axis=1, keepdims=True), axis=0, keepdims=True)
            dkd = _dot_nt(wn, ds_next)
            dw = -_dot_nt(dwn, s)
            db1 = _dot_tn(tm, dwn)
            db2 = _dot_tn(tm, dw)
            dm = jnp.where(strict, -(_dot_nt(db1, u) + _dot_nt(db2, w)), 0.0)
            dbeta = (jnp.sum(dm * kk * dmat, axis=1, keepdims=True) + jnp.sum(db1 * vh, axis=1, keepdims=True)
                     + gam * jnp.sum(db2 * khat, axis=1, keepdims=True))
            dkkm = dm * b * dmat
            ddm = dm * b * kk + dqkm * qk_raw
            dgam = b * jnp.sum(db2 * khat, axis=1, keepdims=True) + jnp.sum(dqd * qn, axis=1, keepdims=True)
            g_qk = dqkm * dmat
            dqn = _dot(g_qk, khat) + dqd * gam
            dkhat = ((b * gam) * db2 + _dot_tn(g_qk, qn) + _dot(dkkm, khat) + _dot_tn(dkkm, khat) + dkd * e)
            dvh = b * db1
            rkd = jnp.sum(dkd * kd, axis=1, keepdims=True)
            emat = ddm * dmat
            dgc = (dgam * gam - rkd + jnp.sum(emat, axis=1, keepdims=True)
                   - jnp.sum(emat.T, axis=1, keepdims=True))
            last = _iota2((BLK, 1), 0) == BLK - 1
            dgc = dgc + jnp.where(last, jnp.sum(rkd, axis=0, keepdims=True) + dgamlast * gamlast, 0.0)
            dgc_t = dgc_t + _put_col(dgc, 4 + h)
            dbeta_t = dbeta_t + _put_col(dbeta, h)
            dqhat = dqn * QK_SCALE
            dq = hc["rq"] * (dqhat - hc["qhat"] * jnp.sum(dqhat * hc["qhat"], axis=-1, keepdims=True))
            dk = hc["rk"] * (dkhat - khat * jnp.sum(dkhat * khat, axis=-1, keepdims=True))
            dcbuf[0:BLK, h * HD:(h + 1) * HD] = dq
            dcbuf[0:BLK, 512 + h * HD:512 + (h + 1) * HD] = dk
            dcbuf[0:BLK, 1024 + h * HD:1024 + (h + 1) * HD] = dvh
        dgv_ref[2:3, :] += dgng
        dgl = _dotf_tn(jnp.where(incl, 1.0, 0.0).astype(F32), dgc_t)
        sig_a = _sigmoid(gt + gv_ref[1:2, :])
        d_araw = dgl * neg_a * sig_a
        dgv_ref[0:1, :] += jnp.sum(dgl * gl, axis=0, keepdims=True)
        dgv_ref[1:2, :] += jnp.sum(d_araw, axis=0, keepdims=True)
        dp_ref[:, GATE0:NW] = (dbeta_t * beta * (1.0 - beta) + d_araw).astype(BF16)
        dcbuf[0:BLK, :] = dcbuf[0:BLK, :] * _silu_grad(conv)
        xcur = p_ref[:, 1024:2560]
        dqkv = jnp.zeros((BLK, 1536), F32)
        for j in range(4):
            shifted = dcbuf[3 - j:3 - j + BLK, :]
            dqkv = dqkv + cw_ref[j:j + 1, :] * shifted
            dcw_ref[j:j + 1, :] += jnp.sum(shifted * xcur, axis=0, keepdims=True)
        dp_ref[:, 1024:2560] = dqkv.astype(BF16)
        dcbuf[BLK:BLK + 8, :] = dcbuf[0:8, :]

        if ne:
            @pl.when(pl.program_id(0) == nb - 1)
            def _():
                for cp in _exchange_copies(pairs, recvs, esems):
                    cp.wait()

    rev = lambda w: pl.BlockSpec((BLK, w), lambda i: (nb - 1 - i, 0))
    st = pl.BlockSpec((None, HEADS, HD, HD), lambda i: (nb - 1 - i, 0, 0, 0))
    fix = lambda *shape: pl.BlockSpec((None,) + shape, lambda i: (l,) + (0,) * len(shape))
    acc = lambda *shape: pl.BlockSpec(shape, lambda i: (0,) * len(shape))
    return pl.pallas_call(
        body, name=f"mixer_bwd_{l}", grid=(nb,),
        out_shape=[jax.ShapeDtypeStruct((t, NW), BF16), jax.ShapeDtypeStruct((8, 512), F32),
                   jax.ShapeDtypeStruct((HEADS, HD, HD), F32), jax.ShapeDtypeStruct((HD, HD), F32),
                   jax.ShapeDtypeStruct((8, 1536), F32), jax.ShapeDtypeStruct((8, HD), F32)]
        + _exchange_shapes(exchange),
        in_specs=[rev(NW), rev(D), st, st, rev(512), rev(512), rev(512),
                  rev(1536),
                  fix(8, 512), fix(HEADS, HD, HD), fix(HD, HD), fix(8, 1536), fix(8, HD)] + _hbm_specs(ne),
        out_specs=[rev(NW), acc(8, 512), acc(HEADS, HD, HD), acc(HD, HD), acc(8, 1536), acc(8, HD)]
        + _hbm_specs(ne),
        scratch_shapes=[pltpu.VMEM((HEADS, HD, HD), F32), pltpu.VMEM((BLK + 8, 1536), F32)]
        + (_exchange_sems(ne) if ne else []),
        compiler_params=_cparams(("arbitrary",)),
    )(p, dmix, s_sv, t_sv, u_sv, w_sv, o_sv, conv_sv, lnp, sgu_w, sgu_bt, cw, gv, *exchange)


_SMALL = (("b_ada", 24), ("norm1_g", 8), ("norm2_g", 8), ("final_g", 8), ("sgu_ln_g", 8), ("sgu_ln_b", 8),
          ("sgu_w", 256), ("sgu_b", 8), ("conv_w", 24), ("a_log", 8), ("dt_bias", 8), ("gdn_norm_g", 8))
_SMALL_PAD = sum(n for _, n in _SMALL)
_DMOD_ROWS = 24


def _pack_rows(parts):
    rows = []
    for (name, n), a in zip(_SMALL, parts):
        flat = a.reshape(-1).astype(F32)
        rows.append(jnp.pad(flat, (0, n * D - flat.shape[0])).reshape(n, D))
    return jnp.concatenate(rows, axis=0)


def _unpack_rows(buf, shapes):
    out, r0 = {}, 0
    for name, n in _SMALL:
        size = math.prod(shapes[name])
        out[name] = buf[r0:r0 + n].reshape(-1)[:size].reshape(shapes[name])
        r0 += n
    return out


_COMBINED_ROWS = 2 * _DMOD_ROWS + _SMALL_PAD
_GATHER_ROWS = -(-_COMBINED_ROWS // 16) * 16


def _pair_combine(own, sib, place):
    rows = own.shape[0]

    def body(p_ref, a_ref, b_ref, o_ref):
        first = lax.axis_index("c") == 0
        a, b = a_ref[0:_DMOD_ROWS, :], b_ref[0:_DMOD_ROWS, :]
        o_ref[0:_DMOD_ROWS, :] = jnp.where(first, a, b)
        o_ref[_DMOD_ROWS:2 * _DMOD_ROWS, :] = jnp.where(first, b, a)
        o_ref[2 * _DMOD_ROWS:_COMBINED_ROWS, :] = a_ref[_DMOD_ROWS:, :] + b_ref[_DMOD_ROWS:, :]
        o_ref[_COMBINED_ROWS:, :] = jnp.zeros((_GATHER_ROWS - _COMBINED_ROWS, D), F32)

    return pl.pallas_call(
        body, name="small_pair_combine", out_shape=jax.ShapeDtypeStruct((N_CHIPS, _GATHER_ROWS, D), F32),
        grid_spec=pltpu.PrefetchScalarGridSpec(
            num_scalar_prefetch=1, grid=(1,),
            in_specs=[pl.BlockSpec((rows, D), lambda i, pr: (0, 0)), pl.BlockSpec((rows, D), lambda i, pr: (0, 0))],
            out_specs=pl.BlockSpec((None, _GATHER_ROWS, D), lambda i, pr: (pr[0], 0, 0))),
        compiler_params=_cparams(("arbitrary",)),
    )(place, own, sib)


def _small_finalize(gathered, w, m, v):
    def body(g_ref, w_ref, m_ref, v_ref, go_ref, d_ref, nm_ref, nv_ref):
        lo, hi = 2 * _DMOD_ROWS, _COMBINED_ROWS
        sm = g_ref[0, lo:hi, :] + g_ref[1, lo:hi, :]
        sm = sm + g_ref[2, lo:hi, :]
        sm = sm + g_ref[3, lo:hi, :]
        bsum = jnp.zeros((_DMOD_ROWS, D), F32)
        for j in range(N_CHIPS):
            bsum = bsum + g_ref[j, 0:_DMOD_ROWS, :]
            bsum = bsum + g_ref[j, _DMOD_ROWS:2 * _DMOD_ROWS, :]
        go_ref[0:_DMOD_ROWS, :] = bsum
        go_ref[_DMOD_ROWS:, :] = sm[_DMOD_ROWS:, :]
        d_ref[...], nm_ref[...], nv_ref[...] = _adam_math(w_ref[...], go_ref[...], m_ref[...], v_ref[...])

    return pl.pallas_call(
        body, name="small_finalize", out_shape=[jax.ShapeDtypeStruct(w.shape, F32)] * 4,
        compiler_params=pltpu.CompilerParams(vmem_limit_bytes=VMEM_LIMIT),
    )(gathered, w, m, v)


def kernel(x, c, w_ada, b_ada, norm1_g, w_in, sgu_ln_g, sgu_ln_b, sgu_w, sgu_b, conv_w, a_log, dt_bias, gdn_norm_g, w_out, norm2_g, w_ff1, w_ff2, final_g, loss_target, m_w_ada, m_b_ada, m_norm1_g, m_w_in, m_sgu_ln_g, m_sgu_ln_b, m_sgu_w, m_sgu_b, m_conv_w, m_a_log, m_dt_bias, m_gdn_norm_g, m_w_out, m_norm2_g, m_w_ff1, m_w_ff2, m_final_g, v_w_ada, v_b_ada, v_norm1_g, v_w_in, v_sgu_ln_g, v_sgu_ln_b, v_sgu_w, v_sgu_b, v_conv_w, v_a_log, v_dt_bias, v_gdn_norm_g, v_w_out, v_norm2_g, v_w_ff1, v_w_ff2, v_final_g):
    xi, yi, ci = lax.axis_index("x"), lax.axis_index("y"), lax.axis_index("c")
    chip = 2 * xi + yi
    dev = 2 * chip + ci
    t = x.shape[1]
    x0 = x.reshape(t, D)
    target = loss_target.reshape(t, D)

    c_sib = _pair_exchange(c, "c_pair")
    c_pair = jnp.where(ci == 0, jnp.concatenate([c, c_sib], 0), jnp.concatenate([c_sib, c], 0))
    c_all = _chip_allgather(c_pair, "c_chips").reshape(8, D)
    ada_cols = w_ada.shape[2]
    b_cols = lax.dynamic_slice_in_dim(b_ada, chip * ada_cols, ada_cols, axis=1)
    mod_part = _ada_forward(c_all, w_ada, b_cols)
    conv_cols = conv_w.shape[2]
    packed = jnp.concatenate([mod_part.reshape(DEPTH * 8, ada_cols), conv_w.reshape(DEPTH, 4 * conv_cols)], axis=0)
    packed = _chip_allgather(packed, "mod_chips")
    mod_all = packed[:, :DEPTH * 8].reshape(N_CHIPS, DEPTH, 8, ada_cols)
    mod_mine = lax.dynamic_index_in_dim(mod_all, dev, axis=2, keepdims=False)
    mod = mod_mine.transpose(1, 0, 2).reshape(DEPTH, 6, D)
    modv = jnp.concatenate([mod, norm1_g[:, None, :], norm2_g[:, None, :]], axis=1)
    conv_full = packed[:, DEPTH * 8:].reshape(N_CHIPS, DEPTH, 4, conv_cols).transpose(1, 2, 0, 3).reshape(DEPTH, 4, 1536)

    place = jnp.stack([chip, ci]).astype(jnp.int32)
    wbufs = [[_cast_into_slot(w, l, place) for w in (w_in, w_out, w_ff1, w_ff2)] for l in range(DEPTH)]
    wbufs[0][:1] = _weights_allgather(wbufs[0][:1], 0)

    def full_w_in(g):
        return jnp.pad(g.transpose(1, 0, 2).reshape(D, IN_W), ((0, 0), (0, NW - IN_W)))

    lnp = jnp.pad(jnp.stack([sgu_ln_g, sgu_ln_b], axis=1), ((0, 0), (0, 6), (0, 0)))
    sgu_bt = jnp.pad(sgu_b.transpose(0, 2, 1), ((0, 0), (0, 0), (0, HD - HEADS)))
    cw = jnp.pad(conv_full, ((0, 0), (0, 4), (0, 0)))
    lane_pad = lambda a: jnp.pad(a, ((0, 0), (4, HD - 8)))
    gv = jnp.pad(jnp.stack([lane_pad(a_log), lane_pad(dt_bias), gdn_norm_g], axis=1), ((0, 0), (0, 5), (0, 0)))

    acts = []
    xl = x0
    for l in range(DEPTH):
        win = full_w_in(wbufs[l][0])
        p, h1 = _fwd_in(xl, modv, win, l)
        nxt = wbufs[l][1:] + (wbufs[l + 1][:1] if l + 1 < DEPTH else [])
        mix, *rest = _mixer_forward(p, lnp, sgu_w, sgu_bt, cw, gv, l, gather=nxt)
        saved = rest[:6]
        wbufs[l][1:] = rest[6:9]
        if l + 1 < DEPTH:
            wbufs[l + 1][:1] = rest[9:]
        g_in, g_out, g_ff1, g_ff2 = wbufs[l]
        x1, r, h2 = _fwd_out_ff1(xl, mix, modv, g_out, g_ff1, l)
        x2 = _fwd_ff2(x1, r, modv, g_ff2, l)
        acts.append((xl, p, mix, saved, x1, r, h1, h2, win))
        xl = x2

    dx, head_stats = _loss_head(xl, target, final_g)
    loss = lax.psum(jnp.sum(head_stats[1, 0:1]), ("x", "y", "c"))
    d_final_g = head_stats[0]
    names = ("in", "out", "ff1", "ff2")
    grads_buf = [None] * len(names)

    def pair_sums(partials, from_sib, kinds, lay):
        return [(lay, n, _pair_sum(g, ga, place, f"pair_sum_{n}_{lay}")) for g, ga, n in zip(partials, from_sib, kinds)]

    def reduce_into_buffers(items, recv):
        for (lay, n, pair), rc in zip(items, recv):
            i = names.index(n)
            grads_buf[i] = _chip_sum(pair, rc, grads_buf[i], lay, place, f"chip_sum_{n}_{lay}")

    pending = []

    dmod, small = [None] * DEPTH, [None] * DEPTH
    for l in reversed(range(DEPTH)):
        xl, p, mix, saved, x1, r, h1, h2, win = acts[l]
        g_in, g_out, g_ff1, g_ff2 = wbufs[l]
        df = _bwd_ff2(dx, r, modv, g_ff2, l)
        gw_ff2, dg2 = _grad_weight(r, dx, modv, l, "ff2", g_ff2)
        gw_ff1, _ = _grad_weight(h2, df, modv, l, "ff1")
        dx1, st2, dmix, *sib_ff = _bwd_norm(df, g_ff1, x1, dx, modv, l, "ff1", send=[gw_ff1, gw_ff2], w_out=g_out)
        gw_out, dg1 = _grad_weight(mix, dx1, modv, l, "out", g_out)
        gw_out = gw_out.reshape(N_CHIPS, D // N_CHIPS, D)
        sib_out = _grads_pair_send([gw_out], f"out_{l}")
        pending = pending + pair_sums([gw_out, gw_ff1, gw_ff2], list(sib_out) + sib_ff, names[1:], l)
        dp, dlnp, dsguw, dsgub, dcw, dgv, *recv = _mixer_backward(p, dmix, saved, lnp, sgu_w, sgu_bt, cw, gv, l,
                                                                  exchange=[item[2] for item in pending])
        reduce_into_buffers(pending, recv)
        gw_in, _ = _grad_weight(h1, dp, modv, l, "in")
        gw_in_c = gw_in[:, :IN_W].reshape(D, N_CHIPS, IN_W // N_CHIPS).transpose(1, 0, 2)
        if l > 0:
            dx, st1, sib_in = _bwd_norm(dp, win, xl, dx1, modv, l, "in", send=[gw_in_c])
            pending = pair_sums([gw_in_c], [sib_in], names[:1], l)
        else:
            pending = pair_sums([gw_in_c], _grads_pair_send([gw_in_c], "in_0"), names[:1], l)
            dx, st1, *rest = _bwd_norm(dp, win, xl, dx1, modv, l, "in", share=grads_buf[1:],
                                       exchange=[item[2] for item in pending])
            grads_buf[1:] = rest[:3]
            reduce_into_buffers(pending, rest[3:])
        dmod[l] = jnp.stack([st1[0], st1[1], dg1[0], st2[0], st2[1], dg2[0]], axis=0)
        small[l] = dict(norm1_g=st1[2], norm2_g=st2[2], sgu_ln_g=dlnp[0], sgu_ln_b=dlnp[1], sgu_w=dsguw,
                        sgu_b=dsgub[:, :HEADS].T, conv_w=dcw[:4], a_log=dgv[0, 4:8], dt_bias=dgv[1, 4:8],
                        gdn_norm_g=dgv[2])
    grad_x = dx.reshape(1, t, D)

    stack = lambda k: jnp.stack([small[l][k] for l in range(DEPTH)], axis=0)
    small_grads = [jnp.zeros((DEPTH, 6 * D), F32), stack("norm1_g"), stack("norm2_g"), d_final_g, stack("sgu_ln_g"),
                   stack("sgu_ln_b"), stack("sgu_w"), stack("sgu_b"), stack("conv_w"), stack("a_log"),
                   stack("dt_bias"), stack("gdn_norm_g")]
    own = jnp.concatenate([jnp.stack(dmod, axis=0).reshape(_DMOD_ROWS, D), _pack_rows(small_grads)], axis=0)
    sib = _pair_exchange(own, "small_pair")
    gathered, = _weights_allgather([_pair_combine(own, sib, place)], "small")
    small_shapes = dict(b_ada=b_ada.shape, norm1_g=norm1_g.shape, norm2_g=norm2_g.shape, final_g=final_g.shape,
                        sgu_ln_g=sgu_ln_g.shape, sgu_ln_b=sgu_ln_b.shape, sgu_w=sgu_w.shape, sgu_b=sgu_b.shape,
                        conv_w=(DEPTH, 4, 1536), a_log=a_log.shape, dt_bias=dt_bias.shape,
                        gdn_norm_g=gdn_norm_g.shape)

    def full_conv(a):
        return lax.dynamic_update_slice_in_dim(jnp.zeros((DEPTH, 4, 1536), F32), a, chip * conv_cols, axis=2)

    def pack_state(b_, n1, n2, fg, lg, lb, sw, sb, cv, al, db, gn):
        return _pack_rows([b_, n1, n2, fg, lg, lb, sw, sb, full_conv(cv), al, db, gn])

    w_small = pack_state(b_ada, norm1_g, norm2_g, final_g, sgu_ln_g, sgu_ln_b, sgu_w, sgu_b, conv_w, a_log, dt_bias,
                         gdn_norm_g)
    m_small = pack_state(m_b_ada, m_norm1_g, m_norm2_g, m_final_g, m_sgu_ln_g, m_sgu_ln_b, m_sgu_w, m_sgu_b, m_conv_w,
                         m_a_log, m_dt_bias, m_gdn_norm_g)
    v_small = pack_state(v_b_ada, v_norm1_g, v_norm2_g, v_final_g, v_sgu_ln_g, v_sgu_ln_b, v_sgu_w, v_sgu_b, v_conv_w,
                         v_a_log, v_dt_bias, v_gdn_norm_g)
    small_out = _small_finalize(gathered, w_small, m_small, v_small)
    sg, sd, sm, sv = [_unpack_rows(a, small_shapes) for a in small_out]
    for dct in (sg, sd, sm, sv):
        dct["conv_w"] = lax.dynamic_slice_in_dim(dct["conv_w"], chip * conv_cols, conv_cols, axis=2)

    dmod_all = gathered[:, :2 * _DMOD_ROWS].reshape(8, DEPTH, 6 * D)
    dmod_cols = lax.dynamic_slice_in_dim(dmod_all, chip * ada_cols, ada_cols, axis=2).transpose(1, 0, 2)
    g_ada, d_ada, nm_ada, nv_ada = _ada_backward_adamw(c_all, dmod_cols, w_ada, m_w_ada, v_w_ada)

    grads_buf[:1] = _grads_pair_share(grads_buf[:1], "grads_pair_share_in")
    big = {}
    for n, g, (w, m, v) in zip(names, grads_buf, ((w_in, m_w_in, v_w_in), (w_out, m_w_out, v_w_out),
                                                  (w_ff1, m_w_ff1, v_w_ff1), (w_ff2, m_w_ff2, v_w_ff2))):
        big[n] = (g,) + tuple(_adamw(w, g, m, v, f"adamw_{n}"))

    def outs(k):
        s = (sg, sd, sm, sv)[k]
        return [(g_ada, d_ada, nm_ada, nv_ada)[k], s["b_ada"], s["norm1_g"], big["in"][k], s["sgu_ln_g"],
                s["sgu_ln_b"], s["sgu_w"], s["sgu_b"], s["conv_w"], s["a_log"], s["dt_bias"], s["gdn_norm_g"],
                big["out"][k], s["norm2_g"], big["ff1"][k], big["ff2"][k], s["final_g"]]

    return (loss, grad_x, *outs(0), *outs(1), *outs(2), *outs(3))
```

```python
import functools
import math

import jax
import jax.numpy as jnp
from jax import lax
from jax.experimental import pallas as pl
from jax.experimental.pallas import tpu as pltpu

F32 = jnp.float32
BF16 = jnp.bfloat16

DEPTH = 4
D = 1024
HEADS = 4
HD = 128
BLK = 128
IN_W = 3080
NW = 3200
GATE0 = 3072
DFF = 4096
N_CHIPS = 4
RMS_EPS = 1e-6
LN_EPS = 1e-5
QK_SCALE = HD ** -0.5
LR, B1, B2, ADAM_EPS, WD, STEP = 0.001, 0.9, 0.999, 1e-08, 0.01, 10
VMEM_LIMIT = 56 * 1024 * 1024
MESH = pl.DeviceIdType.MESH
HOPS = ((1, 0), (0, 1), (1, 1))
HI = lax.Precision.HIGHEST


def _dot(a, b):
    return jnp.dot(a.astype(BF16), b.astype(BF16), preferred_element_type=F32)


def _dot_nt(a, b):
    return lax.dot_general(a.astype(BF16), b.astype(BF16), (((1,), (1,)), ((), ())), preferred_element_type=F32)


def _dot_tn(a, b):
    return lax.dot_general(a.astype(BF16), b.astype(BF16), (((0,), (0,)), ((), ())), preferred_element_type=F32)


def _dotf(a, b):
    return jnp.dot(a, b, precision=HI, preferred_element_type=F32)


def _split(a):
    hi = a.astype(BF16)
    return hi, (a - hi.astype(F32)).astype(BF16)


def _dg3(a, b, dims, batch=((), ())):
    ah, al = _split(a)
    bh, bl = _split(b)
    f = lambda x, y: lax.dot_general(x, y, (dims, batch), preferred_element_type=F32)
    return f(ah, bh) + (f(ah, bl) + f(al, bh))


def _bmm3(a, b):
    return _dg3(a, b, ((2,), (1,)), ((0,), (0,)))


def _d3(a, b):
    return _dg3(a, b, ((1,), (0,)))


def _d3_nt(a, b):
    return _dg3(a, b, ((1,), (1,)))


def _d3_tn(a, b):
    return _dg3(a, b, ((0,), (0,)))


def _dotf_tn(a, b):
    return lax.dot_general(a, b, (((0,), (0,)), ((), ())), precision=HI, preferred_element_type=F32)


def _sigmoid(x):
    return 1.0 / (1.0 + jnp.exp(-x))


def _softplus(x):
    return jnp.maximum(x, 0.0) + jnp.log(1.0 + jnp.exp(-jnp.abs(x)))


_G0 = math.sqrt(2.0 / math.pi)
_G1 = 0.044715


def _gelu(x):
    t = jnp.tanh(_G0 * (x + _G1 * x * x * x))
    return 0.5 * x * (1.0 + t)


def _gelu_grad(x):
    t = jnp.tanh(_G0 * (x + _G1 * x * x * x))
    return 0.5 * (1.0 + t) + 0.5 * x * (1.0 - t * t) * (_G0 * (1.0 + 3.0 * _G1 * x * x))


def _silu(x):
    return x * _sigmoid(x)


def _silu_grad(x):
    s = _sigmoid(x)
    return s * (1.0 + x * (1.0 - s))


def _rms_stats(x):
    rstd = lax.rsqrt(jnp.mean(x * x, axis=-1, keepdims=True) + RMS_EPS)
    return x * rstd, rstd


def _norm_mod(x, ng, sc, sh):
    xh, _ = _rms_stats(x)
    return xh * (ng * (1.0 + sc)) + sh


def _norm_mod_bwd(dh, x, ng, sc):
    xh, rstd = _rms_stats(x)
    dsh = jnp.sum(dh, axis=0, keepdims=True)
    dsc = jnp.sum(dh * xh, axis=0, keepdims=True) * ng
    dng = jnp.sum(dh * xh, axis=0, keepdims=True) * (1.0 + sc)
    dxh = dh * (ng * (1.0 + sc))
    dx = rstd * (dxh - xh * jnp.mean(dxh * xh, axis=-1, keepdims=True))
    return dx, dsh, dsc, dng


def _iota2(shape, axis):
    return lax.broadcasted_iota(jnp.int32, shape, axis)


def _col(tile, idx):
    return jnp.sum(jnp.where(_iota2(tile.shape, 1) == idx, tile, 0.0), axis=1, keepdims=True)


def _col_slice(tile, idx):
    return tile[:, idx:idx + 1]


def _row(tile, idx):
    return jnp.sum(jnp.where(_iota2(tile.shape, 0) == idx, tile, 0.0), axis=0, keepdims=True)


def _put_col(col, idx, width=HD):
    shape = (col.shape[0], width)
    return jnp.where(_iota2(shape, 1) == idx, jnp.broadcast_to(col, shape), 0.0)


def _tri_inverse(m):
    rows, cols = _iota2(m.shape, m.ndim - 2), _iota2(m.shape, m.ndim - 1)
    mm = _bmm3 if m.ndim == 3 else _d3
    eye = jnp.where(rows == cols, 1.0, 0.0).astype(F32)
    n = jnp.where((rows >> 3) == (cols >> 3), -m, 0.0)
    p = eye + n
    n2 = mm(n, n)
    p = p + mm(n2, p)
    n4 = mm(n2, n2)
    p = p + mm(n4, p)
    for shift in (3, 4, 5, 6):
        same_pair = (rows >> (shift + 1)) == (cols >> (shift + 1))
        below = jnp.logical_and(((rows >> shift) & 1) == 1, ((cols >> shift) & 1) == 0)
        off = jnp.where(jnp.logical_and(same_pair, below), m, 0.0)
        p = p - mm(p, mm(off, p))
    return p


def _cparams(sem=None):
    return pltpu.CompilerParams(dimension_semantics=sem, vmem_limit_bytes=VMEM_LIMIT)


def _my_place():
    return lax.axis_index("x"), lax.axis_index("y"), lax.axis_index("c")


def _hop(xi, yi, hop):
    dx, dy = hop
    return (1 - xi if dx else xi), (1 - yi if dy else yi)


def _pair_exchange(x, name):
    def body(x_ref, o_ref, ssem, rsem):
        xi, yi, ci = _my_place()
        cp = pltpu.make_async_remote_copy(x_ref, o_ref, ssem, rsem, device_id=(xi, yi, 1 - ci), device_id_type=MESH)
        cp.start()
        cp.wait()

    return pl.pallas_call(
        body, name=name, out_shape=jax.ShapeDtypeStruct(x.shape, x.dtype),
        in_specs=[pl.BlockSpec(memory_space=pltpu.VMEM)], out_specs=pl.BlockSpec(memory_space=pltpu.VMEM),
        scratch_shapes=[pltpu.SemaphoreType.DMA, pltpu.SemaphoreType.DMA],
        compiler_params=pltpu.CompilerParams(vmem_limit_bytes=VMEM_LIMIT),
    )(x)


def _allgather_start(x_ref, o_ref, ssems, rsems, lsem):
    xi, yi, ci = _my_place()
    me = 2 * xi + yi
    pltpu.make_async_copy(x_ref, o_ref.at[me], lsem).start()
    for k, hop in enumerate(HOPS):
        tx, ty = _hop(xi, yi, hop)
        pltpu.make_async_remote_copy(x_ref, o_ref.at[me], ssems.at[k], rsems.at[k],
                                     device_id=(tx, ty, ci), device_id_type=MESH).start()


def _allgather_finish(x_ref, o_ref, ssems, rsems, lsem):
    xi, yi, ci = _my_place()
    me = 2 * xi + yi
    for k, hop in enumerate(HOPS):
        tx, ty = _hop(xi, yi, hop)
        cp = pltpu.make_async_remote_copy(x_ref, o_ref.at[2 * tx + ty], ssems.at[k], rsems.at[k],
                                          device_id=(tx, ty, ci), device_id_type=MESH)
        cp.wait_recv()
        cp.wait_send()
    pltpu.make_async_copy(x_ref, o_ref.at[me], lsem).wait()


_ALLGATHER_SEMS = [pltpu.SemaphoreType.DMA((3,)), pltpu.SemaphoreType.DMA((3,)), pltpu.SemaphoreType.DMA]


def _chip_allgather(x, name):
    def body(x_ref, o_ref, ssems, rsems, lsem):
        _allgather_start(x_ref, o_ref, ssems, rsems, lsem)
        _allgather_finish(x_ref, o_ref, ssems, rsems, lsem)

    return pl.pallas_call(
        body, name=name, out_shape=jax.ShapeDtypeStruct((N_CHIPS,) + x.shape, x.dtype),
        in_specs=[pl.BlockSpec(memory_space=pltpu.VMEM)], out_specs=pl.BlockSpec(memory_space=pltpu.VMEM),
        scratch_shapes=_ALLGATHER_SEMS, compiler_params=pltpu.CompilerParams(vmem_limit_bytes=VMEM_LIMIT),
    )(x)


def _hbm_specs(n):
    return [pl.BlockSpec(memory_space=pl.ANY)] * n


def _cast_into_slot(w, l, place):
    _, r, c = w.shape
    tr = _row_tile(r)

    def body(p_ref, w_ref, o_ref):
        o_ref[...] = w_ref[...].astype(BF16)

    return pl.pallas_call(
        body, name=f"cast_slot_{r}x{c}_{l}", out_shape=jax.ShapeDtypeStruct((N_CHIPS, r, c), BF16),
        grid_spec=pltpu.PrefetchScalarGridSpec(
            num_scalar_prefetch=1, grid=(r // tr,),
            in_specs=[pl.BlockSpec((None, tr, c), lambda k, pr: (l, k, 0))],
            out_specs=pl.BlockSpec((None, tr, c), lambda k, pr: (pr[0], k, 0))),
        compiler_params=_cparams(("parallel",)),
    )(place, w)


def _halves(ref, ci):
    half = ref.shape[-2] // 2
    return pl.ds(ci * half, half), pl.ds((1 - ci) * half, half)


def _gather_copies(bufs, sems):
    s_ici, r_ici, s_d2d, r_d2d = sems
    xi, yi, ci = _my_place()
    me = 2 * xi + yi
    ici_send, ici_recv, d2d_send, d2d_recv = [], [], [], []
    for i, buf in enumerate(bufs):
        mine, sibs = _halves(buf, ci)
        for k, hop in enumerate(HOPS):
            tx, ty = _hop(xi, yi, hop)
            src = 2 * tx + ty
            ici_send.append(pltpu.make_async_remote_copy(buf.at[me, mine], buf.at[me, mine], s_ici.at[i, k],
                                                         r_ici.at[i, k], device_id=(tx, ty, ci), device_id_type=MESH))
            ici_recv.append(pltpu.make_async_remote_copy(buf.at[src, mine], buf.at[src, mine], s_ici.at[i, k],
                                                         r_ici.at[i, k], device_id=(tx, ty, ci), device_id_type=MESH))
            d2d_send.append(pltpu.make_async_remote_copy(buf.at[src, mine], buf.at[src, mine], s_d2d.at[i, k],
                                                         r_d2d.at[i, k], device_id=(xi, yi, 1 - ci), device_id_type=MESH))
            d2d_recv.append(pltpu.make_async_remote_copy(buf.at[src, sibs], buf.at[src, sibs], s_d2d.at[i, k],
                                                         r_d2d.at[i, k], device_id=(xi, yi, 1 - ci), device_id_type=MESH))
    return ici_send, ici_recv, d2d_send, d2d_recv


def _gather_start(bufs, sems):
    for cp in _gather_copies(bufs, sems)[0]:
        cp.start()


def _gather_forward(bufs, sems):
    _, ici_recv, d2d_send, _ = _gather_copies(bufs, sems)
    for arrived, forward in zip(ici_recv, d2d_send):
        arrived.wait_recv()
        forward.start()


def _gather_finish(bufs, sems):
    ici_send, _, d2d_send, d2d_recv = _gather_copies(bufs, sems)
    for cp in d2d_recv:
        cp.wait_recv()
    for cp in ici_send + d2d_send:
        cp.wait_send()


def _gather_sems(n):
    return [pltpu.SemaphoreType.DMA((n, 3))] * 4


def _weights_allgather(bufs, l):
    n = len(bufs)

    def body(*refs):
        outs, sems = refs[n:2 * n], refs[2 * n:]
        _gather_start(outs, sems)
        _gather_forward(outs, sems)
        _gather_finish(outs, sems)

    return pl.pallas_call(
        body, name=f"weights_allgather_{l}",
        out_shape=[jax.ShapeDtypeStruct(b.shape, b.dtype) for b in bufs],
        in_specs=_hbm_specs(n), out_specs=_hbm_specs(n), input_output_aliases={i: i for i in range(n)},
        scratch_shapes=_gather_sems(n),
    )(*bufs)


def _pair_send_copies(gs, outs, sems):
    ssem, rsem = sems
    xi, yi, ci = _my_place()
    every = pl.ds(0, N_CHIPS)
    return [pltpu.make_async_remote_copy(g.at[every, _halves(g, ci)[1]], o, ssem.at[i], rsem.at[i],
                                         device_id=(xi, yi, 1 - ci), device_id_type=MESH)
            for i, (g, o) in enumerate(zip(gs, outs))]


def _pair_send_shapes(gs):
    return [jax.ShapeDtypeStruct((N_CHIPS, g.shape[1] // 2, g.shape[2]), g.dtype) for g in gs]


def _pair_send_sems(n):
    return [pltpu.SemaphoreType.DMA((n,)), pltpu.SemaphoreType.DMA((n,))]


def _grads_pair_send(gs, l):
    n = len(gs)

    def body(*refs):
        cps = _pair_send_copies(refs[:n], refs[n:2 * n], refs[2 * n:])
        for cp in cps:
            cp.start()
        for cp in cps:
            cp.wait()

    return pl.pallas_call(
        body, name=f"grads_pair_send_{l}", out_shape=_pair_send_shapes(gs),
        in_specs=_hbm_specs(n), out_specs=_hbm_specs(n), scratch_shapes=_pair_send_sems(n),
    )(*gs)


def _exchange_copies(ps, recvs, sems):
    ssems, rsems = sems
    xi, yi, ci = _my_place()
    cps = []
    for i, (p, rc) in enumerate(zip(ps, recvs)):
        for k, hop in enumerate(HOPS):
            tx, ty = _hop(xi, yi, hop)
            cps.append(pltpu.make_async_remote_copy(p.at[2 * tx + ty], rc.at[k], ssems.at[i, k], rsems.at[i, k],
                                                    device_id=(tx, ty, ci), device_id_type=MESH))
    return cps


def _exchange_sems(n):
    return [pltpu.SemaphoreType.DMA((n, 3))] * 2


def _exchange_shapes(ps):
    return [jax.ShapeDtypeStruct((3,) + p.shape[1:], p.dtype) for p in ps]


def _pair_share_copies(bufs, sems):
    ssem, rsem = sems
    xi, yi, ci = _my_place()
    every = pl.ds(0, DEPTH)
    sends, arrivals = [], []
    for i, buf in enumerate(bufs):
        mine, sibs = _halves(buf, ci)
        sends.append(pltpu.make_async_remote_copy(buf.at[every, mine], buf.at[every, mine], ssem.at[i], rsem.at[i],
                                                  device_id=(xi, yi, 1 - ci), device_id_type=MESH))
        arrivals.append(pltpu.make_async_remote_copy(buf.at[every, sibs], buf.at[every, sibs], ssem.at[i], rsem.at[i],
                                                     device_id=(xi, yi, 1 - ci), device_id_type=MESH))
    return sends, arrivals


def _pair_share_start(bufs, sems):
    for cp in _pair_share_copies(bufs, sems)[0]:
        cp.start()


def _pair_share_finish(bufs, sems):
    sends, arrivals = _pair_share_copies(bufs, sems)
    for cp in arrivals:
        cp.wait_recv()
    for cp in sends:
        cp.wait_send()


def _grads_pair_share(gs, name):
    n = len(gs)

    def body(*refs):
        _pair_share_start(refs[n:2 * n], refs[2 * n:])
        _pair_share_finish(refs[n:2 * n], refs[2 * n:])

    return pl.pallas_call(
        body, name=name,
        out_shape=[jax.ShapeDtypeStruct(g.shape, g.dtype) for g in gs],
        in_specs=_hbm_specs(n), out_specs=_hbm_specs(n), input_output_aliases={i: i for i in range(n)},
        scratch_shapes=_pair_send_sems(n),
    )(*gs)


def _row_tile(r):
    return min(r, 512)


def _pair_sum(g, ga, place, name):
    _, r, c = g.shape
    tr = _row_tile(r // 2)
    nk = r // 2 // tr

    def body(p_ref, g_ref, ga_ref, o_ref):
        o_ref[...] = (g_ref[...].astype(F32) + ga_ref[...].astype(F32)).astype(o_ref.dtype)

    return pl.pallas_call(
        body, name=name, out_shape=jax.ShapeDtypeStruct(ga.shape, ga.dtype),
        grid_spec=pltpu.PrefetchScalarGridSpec(
            num_scalar_prefetch=1, grid=(N_CHIPS, nk),
            in_specs=[pl.BlockSpec((None, tr, c), lambda j, k, pr: (j, pr[1] * nk + k, 0)),
                      pl.BlockSpec((None, tr, c), lambda j, k, pr: (j, k, 0))],
            out_specs=pl.BlockSpec((None, tr, c), lambda j, k, pr: (j, k, 0))),
        compiler_params=_cparams(("parallel", "parallel")),
    )(place, g, ga)


def _chip_sum(pair, recv, buf, l, place, name):
    _, rh, c = pair.shape
    tr = _row_tile(rh)
    nk = rh // tr

    def body(p_ref, own_ref, r_ref, *rest):
        o_ref = rest[-1]
        acc = own_ref[...].astype(F32) + r_ref[0].astype(F32)
        acc = acc + r_ref[1].astype(F32)
        o_ref[...] = acc + r_ref[2].astype(F32)

    in_specs = [pl.BlockSpec((None, tr, c), lambda k, pr: (pr[0], k, 0)),
                pl.BlockSpec((3, tr, c), lambda k, pr: (0, k, 0))]
    args = [pair, recv]
    aliases = {}
    if buf is not None:
        in_specs.append(pl.BlockSpec(memory_space=pl.ANY))
        args.append(buf)
        aliases = {3: 0}
    return pl.pallas_call(
        body, name=name, out_shape=jax.ShapeDtypeStruct((DEPTH, 2 * rh, c), F32),
        grid_spec=pltpu.PrefetchScalarGridSpec(
            num_scalar_prefetch=1, grid=(nk,), in_specs=in_specs,
            out_specs=pl.BlockSpec((None, tr, c), lambda k, pr: (l, pr[1] * nk + k, 0))),
        input_output_aliases=aliases, compiler_params=_cparams(("parallel",)),
    )(place, *args)


def _adam_math(w, g, m, v):
    m = B1 * m + (1.0 - B1) * g
    v = B2 * v + (1.0 - B2) * (g * g)
    m_hat = m / (1.0 - B1 ** STEP)
    v_hat = v / (1.0 - B2 ** STEP)
    delta = -LR * (m_hat / (jnp.sqrt(v_hat) + ADAM_EPS) + WD * w)
    return delta, m, v


def _adamw(w, g, m, v, name):
    n_l, r, c = w.shape
    tr = _row_tile(r)

    def body(w_ref, g_ref, m_ref, v_ref, d_ref, nm_ref, nv_ref):
        d_ref[...], nm_ref[...], nv_ref[...] = _adam_math(w_ref[...], g_ref[...], m_ref[...], v_ref[...])

    spec = pl.BlockSpec((None, tr, c), lambda i, k: (i, k, 0))
    return pl.pallas_call(
        body, name=name, out_shape=[jax.ShapeDtypeStruct(w.shape, F32)] * 3, grid=(n_l, r // tr),
        in_specs=[spec] * 4, out_specs=[spec] * 3, compiler_params=_cparams(("parallel", "parallel")),
    )(w, g, m, v)


def _ada_forward(c_all, w_ada, b_cols):
    cols = w_ada.shape[2]
    tn = 512

    def body(c_ref, w_ref, b_ref, o_ref):
        o_ref[...] = _dotf(_silu(c_ref[...]), w_ref[...]) + b_ref[...]

    return pl.pallas_call(
        body, name="ada_forward", out_shape=jax.ShapeDtypeStruct((DEPTH, 8, cols), F32), grid=(DEPTH, cols // tn),
        in_specs=[pl.BlockSpec((8, D), lambda l, j: (0, 0)),
                  pl.BlockSpec((None, D, tn), lambda l, j: (l, 0, j)),
                  pl.BlockSpec((None, 1, tn), lambda l, j: (l, 0, j))],
        out_specs=pl.BlockSpec((None, 8, tn), lambda l, j: (l, 0, j)),
        compiler_params=_cparams(("parallel", "parallel")),
    )(c_all, w_ada, b_cols.reshape(DEPTH, 1, cols))


def _ada_backward_adamw(c_all, dmod_cols, w, m, v):
    cols = w.shape[2]
    tn = 512

    def body(c_ref, d_ref, w_ref, m_ref, v_ref, g_ref, dl_ref, nm_ref, nv_ref):
        g = _dotf_tn(_silu(c_ref[...]), d_ref[...])
        g_ref[...] = g
        dl_ref[...], nm_ref[...], nv_ref[...] = _adam_math(w_ref[...], g, m_ref[...], v_ref[...])

    wspec = pl.BlockSpec((None, D, tn), lambda l, j: (l, 0, j))
    return pl.pallas_call(
        body, name="ada_backward_adamw", out_shape=[jax.ShapeDtypeStruct(w.shape, F32)] * 4, grid=(DEPTH, cols // tn),
        in_specs=[pl.BlockSpec((8, D), lambda l, j: (0, 0)), pl.BlockSpec((None, 8, tn), lambda l, j: (l, 0, j)),
                  wspec, wspec, wspec],
        out_specs=[wspec] * 4, compiler_params=_cparams(("parallel", "parallel")),
    )(c_all, dmod_cols, w, m, v)


def _tok_tile(t):
    return min(t, 512)


def _wspec4(r, c, l):
    return pl.BlockSpec((N_CHIPS, r, c), lambda i: (0, 0, 0))


def _fwd_in(x, modv, w_in, l):
    t = x.shape[0]
    tm = _tok_tile(t)

    def body(x_ref, mod_ref, w_ref, o_ref, h_ref):
        h = _norm_mod(x_ref[...], mod_ref[6:7, :], mod_ref[1:2, :], mod_ref[0:1, :]).astype(BF16)
        h_ref[...] = h
        o_ref[...] = jnp.dot(h, w_ref[...], preferred_element_type=F32)

    return pl.pallas_call(
        body, name=f"fwd_in_{l}", grid=(t // tm,),
        out_shape=[jax.ShapeDtypeStruct((t, NW), F32), jax.ShapeDtypeStruct((t, D), BF16)],
        in_specs=[pl.BlockSpec((tm, D), lambda i: (i, 0)), pl.BlockSpec((None, 8, D), lambda i: (l, 0, 0)),
                  pl.BlockSpec((D, NW), lambda i: (0, 0))],
        out_specs=[pl.BlockSpec((tm, NW), lambda i: (i, 0)), pl.BlockSpec((tm, D), lambda i: (i, 0))],
        compiler_params=_cparams(("parallel",)),
    )(x, modv, w_in)


def _fwd_out_ff1(x, mix, modv, w_out, w_ff1, l):
    t = x.shape[0]
    tm = _tok_tile(t)

    def body(x_ref, mix_ref, mod_ref, wo_ref, w_ref, x1_ref, o_ref, h_ref):
        x1 = x_ref[...] + mod_ref[2:3, :] * jnp.dot(mix_ref[...], wo_ref[...].reshape(D, D), preferred_element_type=F32)
        x1_ref[...] = x1
        h = _norm_mod(x1, mod_ref[7:8, :], mod_ref[4:5, :], mod_ref[3:4, :]).astype(BF16)
        h_ref[...] = h
        for j in range(N_CHIPS):
            f = jnp.dot(h, w_ref[j], preferred_element_type=F32)
            o_ref[:, j * D:(j + 1) * D] = jnp.maximum(f, 0.0).astype(BF16)

    tok = pl.BlockSpec((tm, D), lambda i: (i, 0))
    return pl.pallas_call(
        body, name=f"fwd_out_ff1_{l}", grid=(t // tm,),
        out_shape=[jax.ShapeDtypeStruct((t, D), F32), jax.ShapeDtypeStruct((t, DFF), BF16),
                   jax.ShapeDtypeStruct((t, D), BF16)],
        in_specs=[tok, tok, pl.BlockSpec((None, 8, D), lambda i: (l, 0, 0)), _wspec4(D // N_CHIPS, D, l),
                  _wspec4(D, D, l)],
        out_specs=[tok, pl.BlockSpec((tm, DFF), lambda i: (i, 0)), tok],
        compiler_params=_cparams(("parallel",)),
    )(x, mix, modv, w_out, w_ff1)


def _fwd_ff2(x, r, modv, w_ff2, l):
    t = x.shape[0]
    tm = _tok_tile(t)

    def body(x_ref, r_ref, mod_ref, w_ref, o_ref):
        acc = jnp.zeros((tm, D), F32)
        for j in range(N_CHIPS):
            rj = r_ref[:, j * D:(j + 1) * D].astype(F32)
            acc = acc + jnp.dot((rj * rj).astype(BF16), w_ref[j], preferred_element_type=F32)
        o_ref[...] = x_ref[...] + mod_ref[5:6, :] * acc

    return pl.pallas_call(
        body, name=f"fwd_ff2_{l}", out_shape=jax.ShapeDtypeStruct((t, D), F32), grid=(t // tm,),
        in_specs=[pl.BlockSpec((tm, D), lambda i: (i, 0)), pl.BlockSpec((tm, DFF), lambda i: (i, 0)),
                  pl.BlockSpec((None, 8, D), lambda i: (l, 0, 0)), _wspec4(D, D, l)],
        out_specs=pl.BlockSpec((tm, D), lambda i: (i, 0)), compiler_params=_cparams(("parallel",)),
    )(x, r, modv, w_ff2)


def _loss_head(x, target, final_g):
    t = x.shape[0]
    tm = _tok_tile(t)

    def body(x_ref, t_ref, g_ref, dx_ref, st_ref):
        @pl.when(pl.program_id(0) == 0)
        def _():
            st_ref[...] = jnp.zeros_like(st_ref)

        xh, rstd = _rms_stats(x_ref[...])
        g = g_ref[...]
        err = xh * g - t_ref[...]
        loss = 0.5 * jnp.sum(jnp.mean(err * err, axis=-1, keepdims=True), axis=0, keepdims=True)
        dy = err * (1.0 / D)
        st_ref[0:1, :] += jnp.sum(dy * xh, axis=0, keepdims=True)
        st_ref[1:2, :] += jnp.broadcast_to(loss, (1, D))
        dxh = dy * g
        dx_ref[...] = rstd * (dxh - xh * jnp.mean(dxh * xh, axis=-1, keepdims=True))

    return pl.pallas_call(
        body, name="loss_head", out_shape=[jax.ShapeDtypeStruct((t, D), F32), jax.ShapeDtypeStruct((8, D), F32)],
        grid=(t // tm,),
        in_specs=[pl.BlockSpec((tm, D), lambda i: (i, 0)), pl.BlockSpec((tm, D), lambda i: (i, 0)),
                  pl.BlockSpec((1, D), lambda i: (0, 0))],
        out_specs=[pl.BlockSpec((tm, D), lambda i: (i, 0)), pl.BlockSpec((8, D), lambda i: (0, 0))],
        compiler_params=_cparams(("arbitrary",)),
    )(x, target, final_g.reshape(1, D))


def _bwd_ff2(dx2, r, modv, w_ff2, l):
    t = dx2.shape[0]
    tm = _tok_tile(t)

    def body(d_ref, r_ref, mod_ref, w_ref, o_ref):
        dyg = (d_ref[...] * mod_ref[5:6, :]).astype(BF16)
        for j in range(N_CHIPS):
            da = lax.dot_general(dyg, w_ref[j], (((1,), (1,)), ((), ())), preferred_element_type=F32)
            o_ref[:, j * D:(j + 1) * D] = (da * 2.0 * r_ref[:, j * D:(j + 1) * D].astype(F32)).astype(BF16)

    return pl.pallas_call(
        body, name=f"bwd_ff2_{l}", out_shape=jax.ShapeDtypeStruct((t, DFF), BF16), grid=(t // tm,),
        in_specs=[pl.BlockSpec((tm, D), lambda i: (i, 0)), pl.BlockSpec((tm, DFF), lambda i: (i, 0)),
                  pl.BlockSpec((None, 8, D), lambda i: (l, 0, 0)), _wspec4(D, D, l)],
        out_specs=pl.BlockSpec((tm, DFF), lambda i: (i, 0)), compiler_params=_cparams(("parallel",)),
    )(dx2, r, modv, w_ff2)


def _bwd_norm(dy, w, x, dres, modv, l, which, send=(), w_out=None, share=(), exchange=()):
    t = x.shape[0]
    tm = _tok_tile(t)
    nsteps = t // tm
    rows = (6, 1) if which == "in" else (7, 4)
    width = dy.shape[1]
    ns, nh, ne = len(send), len(share), len(exchange)
    nb_ = 0 if w_out is None else 1
    n_in = 5 + nb_ + ns + nh + ne

    def body(*refs):
        dy_ref, w_ref, x_ref, dr_ref, mod_ref = refs[:5]
        wo_ref = refs[5] if nb_ else None
        parts = refs[5 + nb_:5 + nb_ + ns]
        pairs = refs[5 + nb_ + ns + nh:n_in]
        dx_ref, st_ref = refs[n_in:n_in + 2]
        dmix_ref = refs[n_in + 2] if nb_ else None
        o0 = n_in + 2 + nb_
        from_sib, shared, recvs = refs[o0:o0 + ns], refs[o0 + ns:o0 + ns + nh], refs[o0 + ns + nh:o0 + ns + nh + ne]
        scratch = list(refs[o0 + ns + nh + ne:])
        sems = [scratch.pop(0) for _ in range(2 if ns else 0)]
        hsems = [scratch.pop(0) for _ in range(2 if nh else 0)]
        esems = scratch

        @pl.when(pl.program_id(0) == 0)
        def _():
            st_ref[...] = jnp.zeros_like(st_ref)
            if ns:
                for cp in _pair_send_copies(parts, from_sib, sems):
                    cp.start()
            if nh:
                _pair_share_start(shared, hsems)
            if ne:
                for cp in _exchange_copies(pairs, recvs, esems):
                    cp.start()

        if which == "in":
            dh = lax.dot_general(dy_ref[...], w_ref[...], (((1,), (1,)), ((), ())), preferred_element_type=F32)
        else:
            dh = jnp.zeros((tm, D), F32)
            for j in range(N_CHIPS):
                dh = dh + lax.dot_general(dy_ref[:, j * D:(j + 1) * D], w_ref[j], (((1,), (1,)), ((), ())),
                                          preferred_element_type=F32)
        ng, sc = mod_ref[rows[0]:rows[0] + 1, :], mod_ref[rows[1]:rows[1] + 1, :]
        dx, dsh, dsc, dng = _norm_mod_bwd(dh, x_ref[...], ng, sc)
        dx_new = dr_ref[...] + dx
        dx_ref[...] = dx_new
        st_ref[0:1, :] += dsh
        st_ref[1:2, :] += dsc
        st_ref[2:3, :] += dng
        if nb_:
            dyg = (dx_new * mod_ref[2:3, :]).astype(BF16)
            dmix_ref[...] = lax.dot_general(dyg, wo_ref[...].reshape(D, D), (((1,), (1,)), ((), ())),
                                            preferred_element_type=F32).astype(BF16)

        if ns:
            @pl.when(pl.program_id(0) == nsteps - 1)
            def _():
                for cp in _pair_send_copies(parts, from_sib, sems):
                    cp.wait()

        if nh:
            @pl.when(pl.program_id(0) == nsteps - 1)
            def _():
                _pair_share_finish(shared, hsems)

        if ne:
            @pl.when(pl.program_id(0) == nsteps - 1)
            def _():
                for cp in _exchange_copies(pairs, recvs, esems):
                    cp.wait()

    tok = pl.BlockSpec((tm, D), lambda i: (i, 0))
    wspec = pl.BlockSpec((D, NW), lambda i: (0, 0)) if which == "in" else _wspec4(D, D, l)
    return pl.pallas_call(
        body, name=f"bwd_norm_{which}_{l}",
        out_shape=[jax.ShapeDtypeStruct((t, D), F32), jax.ShapeDtypeStruct((8, D), F32)]
        + [jax.ShapeDtypeStruct((t, D), BF16)] * nb_ + _pair_send_shapes(send)
        + [jax.ShapeDtypeStruct(g.shape, g.dtype) for g in share] + _exchange_shapes(exchange),
        grid=(nsteps,),
        in_specs=[pl.BlockSpec((tm, width), lambda i: (i, 0)), wspec, tok, tok,
                  pl.BlockSpec((None, 8, D), lambda i: (l, 0, 0))]
        + [_wspec4(D // N_CHIPS, D, l)] * nb_ + _hbm_specs(ns + nh + ne),
        out_specs=[tok, pl.BlockSpec((8, D), lambda i: (0, 0))] + [tok] * nb_ + _hbm_specs(ns + nh + ne),
        input_output_aliases={5 + nb_ + ns + i: 2 + nb_ + ns + i for i in range(nh)},
        scratch_shapes=(_pair_send_sems(ns) if ns else []) + (_pair_send_sems(nh) if nh else [])
        + (_exchange_sems(ne) if ne else []),
        compiler_params=_cparams(("arbitrary",)),
    )(dy, w, x, dres, modv, *([w_out] * nb_), *send, *share, *exchange)


def _grad_weight(lhs, rhs, modv, l, which, w_gate=None):
    t = lhs.shape[0]
    tm = min(t, 2048)
    nt = t // tm
    gated = which in ("out", "ff2")
    if which == "in":
        nj, lw, rw, orows, ocols = 5, D, NW // 5, D, NW // 5
    elif which == "ff1":
        nj, lw, rw, orows, ocols = N_CHIPS, D, D, D, D
    elif which == "out":
        nj, lw, rw, orows, ocols = 1, D, D, D, D
    else:
        nj, lw, rw, orows, ocols = N_CHIPS, D, D, D, D
    gate_row = 2 if which == "out" else 5

    def body(*refs):
        if gated:
            l_ref, r_ref, mod_ref, wg_ref, o_ref, dg_ref, acc = refs
        else:
            l_ref, r_ref, mod_ref, o_ref, acc = refs
        j, k = pl.program_id(0), pl.program_id(1)

        @pl.when(k == 0)
        def _():
            acc[...] = jnp.zeros_like(acc)

        if which == "ff2":
            lv = l_ref[...].astype(F32)
            lv = lv * lv
        else:
            lv = l_ref[...]
        acc[...] += _dot_tn(lv, r_ref[...])

        if gated:
            @pl.when(jnp.logical_and(j == 0, k == 0))
            def _():
                dg_ref[...] = jnp.zeros_like(dg_ref)

        @pl.when(k == nt - 1)
        def _():
            raw = acc[...]
            if gated:
                o_ref[...] = (raw * mod_ref[gate_row:gate_row + 1, :]).astype(o_ref.dtype)
                dg_ref[0:1, :] += jnp.sum(raw * wg_ref[...].astype(F32), axis=0, keepdims=True)
            else:
                o_ref[...] = raw.astype(o_ref.dtype)

    if which in ("in", "ff1"):
        lspec = pl.BlockSpec((tm, lw), lambda j, k: (k, 0))
        rspec = pl.BlockSpec((tm, rw), lambda j, k: (k, j))
    else:
        lspec = pl.BlockSpec((tm, lw), lambda j, k: (k, j))
        rspec = pl.BlockSpec((tm, rw), lambda j, k: (k, 0))
    mspec = pl.BlockSpec((None, 8, D), lambda j, k: (l, 0, 0))
    flat = which in ("in", "out")
    if flat:
        ospec = pl.BlockSpec((orows, ocols), lambda j, k: (0, j))
        out_shape = [jax.ShapeDtypeStruct((D, nj * ocols), BF16)]
    else:
        ospec = pl.BlockSpec((None, orows, ocols), lambda j, k: (j, 0, 0))
        out_shape = [jax.ShapeDtypeStruct((N_CHIPS, orows, ocols), BF16)]
    in_specs = [lspec, rspec, mspec]
    args = [lhs, rhs, modv]
    out_specs = [ospec]
    if gated:
        if flat:
            in_specs.append(pl.BlockSpec((orows, ocols), lambda j, k: (0, 0)))
            args.append(w_gate.reshape(D, D))
        else:
            in_specs.append(pl.BlockSpec((None, orows, ocols), lambda j, k: (j, 0, 0)))
            args.append(w_gate)
        out_specs.append(pl.BlockSpec((8, D), lambda j, k: (0, 0)))
        out_shape.append(jax.ShapeDtypeStruct((8, D), F32))
    res = pl.pallas_call(
        body, name=f"grad_w_{which}_{l}", out_shape=out_shape, grid=(nj, nt), in_specs=in_specs, out_specs=out_specs,
        scratch_shapes=[pltpu.VMEM((orows, ocols), F32)], compiler_params=_cparams(("arbitrary", "arbitrary")),
    )(*args)
    return (res[0], res[1]) if gated else (res[0], None)


def _tri_masks():
    rows, cols = _iota2((BLK, BLK), 0), _iota2((BLK, BLK), 1)
    return rows >= cols, rows > cols


def _sgu_forward(p_ref, lnp_ref, sguw_ref, sgub_ref, col=_col):
    incl, _ = _tri_masks()
    ug = _gelu(p_ref[:, 0:512])
    vg = _gelu(p_ref[:, 512:1024])
    mu = jnp.mean(vg, axis=-1, keepdims=True)
    xc = vg - mu
    rstd = lax.rsqrt(jnp.mean(xc * xc, axis=-1, keepdims=True) + LN_EPS)
    vhat = xc * rstd
    vn = vhat * lnp_ref[0:1, :] + lnp_ref[1:2, :]
    bias = sgub_ref[...]
    ys, mixed, wms = [], [], []
    for h in range(HEADS):
        wm = jnp.where(incl, sguw_ref[h], 0.0)
        mx = _dot(wm, vn[:, h * HD:(h + 1) * HD]) + col(bias, h)
        ys.append(ug[:, h * HD:(h + 1) * HD] * mx)
        mixed.append(mx)
        wms.append(wm)
    return ys, ug, vhat, rstd, vn, mixed, wms


def _conv_forward(xbuf, cw_ref):
    conv = cw_ref[0:1, :] * xbuf[5:5 + BLK, :]
    for j in range(1, 4):
        conv = conv + cw_ref[j:j + 1, :] * xbuf[5 + j:5 + j + BLK, :]
    return conv


def _gates(gt, gv_ref):
    incl, _ = _tri_masks()
    beta = _sigmoid(gt)
    neg_a = -jnp.exp(gv_ref[0:1, :])
    gl = neg_a * _softplus(gt + gv_ref[1:2, :])
    gc = _dotf(jnp.where(incl, 1.0, 0.0).astype(F32), gl)
    return beta, gl, gc, gc.T, neg_a


def _head_chunk(act, beta, gc, gct, h, col=_col):
    incl, strict = _tri_masks()
    qh = act[:, h * HD:(h + 1) * HD]
    kh = act[:, 512 + h * HD:512 + (h + 1) * HD]
    vh = act[:, 1024 + h * HD:1024 + (h + 1) * HD]
    rq = lax.rsqrt(jnp.sum(qh * qh, axis=-1, keepdims=True) + RMS_EPS)
    rk = lax.rsqrt(jnp.sum(kh * kh, axis=-1, keepdims=True) + RMS_EPS)
    qhat, khat = qh * rq, kh * rk
    qn = qhat * QK_SCALE
    b = col(beta, h)
    gcol = col(gc, 4 + h)
    grow = _row(gct, 4 + h)
    dmat = jnp.where(incl, jnp.exp(jnp.where(incl, gcol - grow, 0.0)), 0.0)
    gam = jnp.exp(gcol)
    glast = _row(gcol, BLK - 1)
    e = jnp.exp(glast - gcol)
    kk = _d3_nt(khat, khat)
    return dict(qhat=qhat, khat=khat, qn=qn, vh=vh, rq=rq, rk=rk, b=b, dmat=dmat, gam=gam, glast=glast, e=e, kk=kk,
                strict=strict, incl=incl)


def _mixer_forward(p, lnp, sgu_w, sgu_bt, cw, gv, l, gather=()):
    t = p.shape[0]
    nb = t // BLK
    ng = len(gather)

    def body(*refs):
        p_ref, lnp_ref, sguw_ref, sgub_ref, cw_ref, gv_ref = refs[:6]
        mix_ref, s_out, t_out, u_out, w_out, o_out, conv_out = refs[6 + ng:13 + ng]
        gbufs = refs[13 + ng:13 + 2 * ng]
        s_scr, xbuf = refs[13 + 2 * ng:15 + 2 * ng]
        gsems = refs[15 + 2 * ng:]

        @pl.when(pl.program_id(0) == 0)
        def _():
            s_scr[...] = jnp.zeros_like(s_scr)
            xbuf[0:8, :] = jnp.zeros((8, 1536), F32)
            if ng:
                _gather_start(gbufs, gsems)

        ys = _sgu_forward(p_ref, lnp_ref, sguw_ref, sgub_ref, col=_col_slice)[0]
        for h in range(HEADS):
            mix_ref[:, h * HD:(h + 1) * HD] = ys[h].astype(BF16)

        xbuf[8:8 + BLK, :] = p_ref[:, 1024:2560]
        conv = _conv_forward(xbuf, cw_ref)
        conv_out[...] = conv
        act = _silu(conv)
        xbuf[0:8, :] = xbuf[BLK:BLK + 8, :]
        beta, _, gc, gct, _ = _gates(p_ref[:, GATE0:NW], gv_ref)
        chunks = [_head_chunk(act, beta, gc, gct, h, col=_col_slice) for h in range(HEADS)]
        for h, hc in enumerate(chunks):
            t_out[h] = jnp.where(hc["strict"], hc["b"] * hc["kk"] * hc["dmat"], 0.0)
        t_out[...] = _tri_inverse(t_out[...])
        for h, hc in enumerate(chunks):
            tm = t_out[h]
            u = _dot(tm, hc["b"] * hc["vh"])
            w = _dot(tm, (hc["b"] * hc["gam"]) * hc["khat"])
            qkm = _dot_nt(hc["qn"], hc["khat"]) * hc["dmat"]
            s = s_scr[h]
            wn = u - _dot(w, s)
            o = _dot(hc["qn"] * hc["gam"], s) + _dot(qkm, wn)
            s_out[h] = s
            s_scr[h] = jnp.exp(hc["glast"]) * s + _dot_tn(hc["khat"] * hc["e"], wn)
            sl = slice(h * HD, (h + 1) * HD)
            u_out[:, sl] = u
            w_out[:, sl] = w
            o_out[:, sl] = o
            on = o * lax.rsqrt(jnp.mean(o * o, axis=-1, keepdims=True) + RMS_EPS) * gv_ref[2:3, :]
            mix_ref[:, 512 + h * HD:512 + (h + 1) * HD] = (on * _silu(p_ref[:, 2560 + h * HD:2560 + (h + 1) * HD])).astype(BF16)

        if ng:
            @pl.when(pl.program_id(0) == nb - 1 - min(3, nb - 1))
            def _():
                _gather_forward(gbufs, gsems)

            @pl.when(pl.program_id(0) == nb - 1)
            def _():
                _gather_finish(gbufs, gsems)

    tok = lambda w: pl.BlockSpec((BLK, w), lambda i: (i, 0))
    st = pl.BlockSpec((None, HEADS, HD, HD), lambda i: (i, 0, 0, 0))
    return pl.pallas_call(
        body, name=f"mixer_fwd_{l}", grid=(nb,),
        out_shape=[jax.ShapeDtypeStruct((t, D), BF16), jax.ShapeDtypeStruct((nb, HEADS, HD, HD), F32),
                   jax.ShapeDtypeStruct((nb, HEADS, HD, HD), F32), jax.ShapeDtypeStruct((t, 512), F32),
                   jax.ShapeDtypeStruct((t, 512), F32), jax.ShapeDtypeStruct((t, 512), F32),
                   jax.ShapeDtypeStruct((t, 1536), F32)]
        + [jax.ShapeDtypeStruct(b.shape, b.dtype) for b in gather],
        in_specs=[tok(NW), pl.BlockSpec((None, 8, 512), lambda i: (l, 0, 0)),
                  pl.BlockSpec((None, HEADS, HD, HD), lambda i: (l, 0, 0, 0)),
                  pl.BlockSpec((None, HD, HD), lambda i: (l, 0, 0)), pl.BlockSpec((None, 8, 1536), lambda i: (l, 0, 0)),
                  pl.BlockSpec((None, 8, HD), lambda i: (l, 0, 0))] + _hbm_specs(ng),
        out_specs=[tok(D), st, st, tok(512), tok(512), tok(512), tok(1536)]
        + _hbm_specs(ng),
        input_output_aliases={6 + i: 7 + i for i in range(ng)},
        scratch_shapes=[pltpu.VMEM((HEADS, HD, HD), F32), pltpu.VMEM((BLK + 8, 1536), F32)]
        + (_gather_sems(ng) if ng else []),
        compiler_params=_cparams(("arbitrary",)),
    )(p, lnp, sgu_w, sgu_bt, cw, gv, *gather)


def _mixer_backward(p, dmix, saved, lnp, sgu_w, sgu_bt, cw, gv, l, exchange=()):
    t = p.shape[0]
    nb = t // BLK
    s_sv, t_sv, u_sv, w_sv, o_sv, conv_sv = saved
    ne = len(exchange)

    def body(*refs):
        (p_ref, dmix_ref, s_ref, t_ref, u_ref, w_ref, o_ref, conv_ref, lnp_ref, sguw_ref, sgub_ref, cw_ref,
         gv_ref) = refs[:13]
        pairs = refs[13:13 + ne]
        dp_ref, dlnp_ref, dsguw_ref, dsgub_ref, dcw_ref, dgv_ref = refs[13 + ne:19 + ne]
        recvs = refs[19 + ne:19 + 2 * ne]
        ds_scr, dcbuf = refs[19 + 2 * ne:21 + 2 * ne]
        esems = refs[21 + 2 * ne:]

        @pl.when(pl.program_id(0) == 0)
        def _():
            if ne:
                for cp in _exchange_copies(pairs, recvs, esems):
                    cp.start()
            ds_scr[...] = jnp.zeros_like(ds_scr)
            dcbuf[BLK:BLK + 8, :] = jnp.zeros((8, 1536), F32)
            dlnp_ref[...] = jnp.zeros_like(dlnp_ref)
            dsguw_ref[...] = jnp.zeros_like(dsguw_ref)
            dsgub_ref[...] = jnp.zeros_like(dsgub_ref)
            dcw_ref[...] = jnp.zeros_like(dcw_ref)
            dgv_ref[...] = jnp.zeros_like(dgv_ref)

        incl, strict = _tri_masks()
        _, ug, vhat, rstd, vn, mixed, wms = _sgu_forward(p_ref, lnp_ref, sguw_ref, sgub_ref)
        dvn_parts, dug_parts = [], []
        dbias = jnp.zeros((BLK, HD), F32)
        for h in range(HEADS):
            sl = slice(h * HD, (h + 1) * HD)
            dy = dmix_ref[:, sl].astype(F32)
            dmx = dy * ug[:, sl]
            dug_parts.append(dy * mixed[h])
            dsguw_ref[h] += jnp.where(incl, _dot_nt(dmx, vn[:, sl]), 0.0)
            dbias = dbias + _put_col(jnp.sum(dmx, axis=1, keepdims=True), h)
            dvn_parts.append(_dot_tn(wms[h], dmx))
        dsgub_ref[...] += dbias
        dvn = jnp.concatenate(dvn_parts, axis=1)
        dug = jnp.concatenate(dug_parts, axis=1)
        dlnp_ref[0:1, :] += jnp.sum(dvn * vhat, axis=0, keepdims=True)
        dlnp_ref[1:2, :] += jnp.sum(dvn, axis=0, keepdims=True)
        dvhat = dvn * lnp_ref[0:1, :]
        dvg = rstd * (dvhat - jnp.mean(dvhat, axis=-1, keepdims=True)
                      - vhat * jnp.mean(dvhat * vhat, axis=-1, keepdims=True))
        dp_ref[:, 0:512] = (dug * _gelu_grad(p_ref[:, 0:512])).astype(BF16)
        dp_ref[:, 512:1024] = (dvg * _gelu_grad(p_ref[:, 512:1024])).astype(BF16)

        conv = conv_ref[...]
        act = _silu(conv)
        gt = p_ref[:, GATE0:NW]
        beta, gl, gc, gct, neg_a = _gates(gt, gv_ref)
        gng = gv_ref[2:3, :]
        dbeta_t = jnp.zeros((BLK, HD), F32)
        dgc_t = jnp.zeros((BLK, HD), F32)
        dgng = jnp.zeros((1, HD), F32)
        for h in range(HEADS):
            sl = slice(h * HD, (h + 1) * HD)
            hc = _head_chunk(act, beta, gc, gct, h)
            b, gam, e, dmat, kk = hc["b"], hc["gam"], hc["e"], hc["dmat"], hc["kk"]
            qn, khat, vh = hc["qn"], hc["khat"], hc["vh"]
            gamlast = jnp.exp(hc["glast"])
            s, tm, u, w, o = s_ref[h], t_ref[h], u_ref[:, sl], w_ref[:, sl], o_ref[:, sl]
            ds_next = ds_scr[h]
            z = p_ref[:, 2560 + h * HD:2560 + (h + 1) * HD]
            dy = dmix_ref[:, 512 + h * HD:512 + (h + 1) * HD].astype(F32)
            ro = lax.rsqrt(jnp.mean(o * o, axis=-1, keepdims=True) + RMS_EPS)
            ohat = o * ro
            dp_ref[:, 2560 + h * HD:2560 + (h + 1) * HD] = (dy * ohat * gng * _silu_grad(z)).astype(BF16)
            don = dy * _silu(z)
            dgng = dgng + jnp.sum(don * ohat, axis=0, keepdims=True)
            dohat = don * gng
            do = ro * (dohat - ohat * jnp.mean(dohat * ohat, axis=-1, keepdims=True))
            qk_raw = _dot_nt(qn, khat)
            qkm = qk_raw * dmat
            qd, kd = qn * gam, khat * e
            wn = u - _dot(w, s)
            dwn = _dot_tn(qkm, do) + _dot(kd, ds_next)
            dqd = _dot_nt(do, s)
            dqkm = jnp.where(incl, _dot_nt(do, wn), 0.0)
            ds_scr[h] = _dot_tn(qd, do) + gamlast * ds_next - _dot_tn(w, dwn)
            dgamlast = jnp.sum(jnp.sum(ds_next * s, axis=1, keepdims=True), axis=0, keepdims=True)
            dkd = _dot_nt(wn, ds_next)
            dw = -_dot_nt(dwn, s)
            db1 = _dot_tn(tm, dwn)
            db2 = _dot_tn(tm, dw)
            dm = jnp.where(strict, -(_dot_nt(db1, u) + _dot_nt(db2, w)), 0.0)
            dbeta = (jnp.sum(dm * kk * dmat, axis=1, keepdims=True) + jnp.sum(db1 * vh, axis=1, keepdims=True)
                     + gam * jnp.sum(db2 * khat, axis=1, keepdims=True))
            dkkm = dm * b * dmat
            ddm = dm * b * kk + dqkm * qk_raw
            dgam = b * jnp.sum(db2 * khat, axis=1, keepdims=True) + jnp.sum(dqd * qn, axis=1, keepdims=True)
            g_qk = dqkm * dmat
            dqn = _dot(g_qk, khat) + dqd * gam
            dkhat = ((b * gam) * db2 + _dot_tn(g_qk, qn) + _dot(dkkm, khat) + _dot_tn(dkkm, khat) + dkd * e)
            dvh = b * db1
            rkd = jnp.sum(dkd * kd, axis=1, keepdims=True)
            emat = ddm * dmat
            dgc = (dgam * gam - rkd + jnp.sum(emat, axis=1, keepdims=True)
                   - jnp.sum(emat.T, axis=1, keepdims=True))
            last = _iota2((BLK, 1), 0) == BLK - 1
            dgc = dgc + jnp.where(last, jnp.sum(rkd, axis=0, keepdims=True) + dgamlast * gamlast, 0.0)
            dgc_t = dgc_t + _put_col(dgc, 4 + h)
            dbeta_t = dbeta_t + _put_col(dbeta, h)
            dqhat = dqn * QK_SCALE
            dq = hc["rq"] * (dqhat - hc["qhat"] * jnp.sum(dqhat * hc["qhat"], axis=-1, keepdims=True))
            dk = hc["rk"] * (dkhat - khat * jnp.sum(dkhat * khat, axis=-1, keepdims=True))
            dcbuf[0:BLK, h * HD:(h + 1) * HD] = dq
            dcbuf[0:BLK, 512 + h * HD:512 + (h + 1) * HD] = dk
            dcbuf[0:BLK, 1024 + h * HD:1024 + (h + 1) * HD] = dvh
        dgv_ref[2:3, :] += dgng
        dgl = _dotf_tn(jnp.where(incl, 1.0, 0.0).astype(F32), dgc_t)
        sig_a = _sigmoid(gt + gv_ref[1:2, :])
        d_araw = dgl * neg_a * sig_a
        dgv_ref[0:1, :] += jnp.sum(dgl * gl, axis=0, keepdims=True)
        dgv_ref[1:2, :] += jnp.sum(d_araw, axis=0, keepdims=True)
        dp_ref[:, GATE0:NW] = (dbeta_t * beta * (1.0 - beta) + d_araw).astype(BF16)
        dcbuf[0:BLK, :] = dcbuf[0:BLK, :] * _silu_grad(conv)
        xcur = p_ref[:, 1024:2560]
        dqkv = jnp.zeros((BLK, 1536), F32)
        for j in range(4):
            shifted = dcbuf[3 - j:3 - j + BLK, :]
            dqkv = dqkv + cw_ref[j:j + 1, :] * shifted
            dcw_ref[j:j + 1, :] += jnp.sum(shifted * xcur, axis=0, keepdims=True)
        dp_ref[:, 1024:2560] = dqkv.astype(BF16)
        dcbuf[BLK:BLK + 8, :] = dcbuf[0:8, :]

        if ne:
            @pl.when(pl.program_id(0) == nb - 1)
            def _():
                for cp in _exchange_copies(pairs, recvs, esems):
                    cp.wait()

    rev = lambda w: pl.BlockSpec((BLK, w), lambda i: (nb - 1 - i, 0))
    st = pl.BlockSpec((None, HEADS, HD, HD), lambda i: (nb - 1 - i, 0, 0, 0))
    fix = lambda *shape: pl.BlockSpec((None,) + shape, lambda i: (l,) + (0,) * len(shape))
    acc = lambda *shape: pl.BlockSpec(shape, lambda i: (0,) * len(shape))
    return pl.pallas_call(
        body, name=f"mixer_bwd_{l}", grid=(nb,),
        out_shape=[jax.ShapeDtypeStruct((t, NW), BF16), jax.ShapeDtypeStruct((8, 512), F32),
                   jax.ShapeDtypeStruct((HEADS, HD, HD), F32), jax.ShapeDtypeStruct((HD, HD), F32),
                   jax.ShapeDtypeStruct((8, 1536), F32), jax.ShapeDtypeStruct((8, HD), F32)]
        + _exchange_shapes(exchange),
        in_specs=[rev(NW), rev(D), st, st, rev(512), rev(512), rev(512),
                  rev(1536),
                  fix(8, 512), fix(HEADS, HD, HD), fix(HD, HD), fix(8, 1536), fix(8, HD)] + _hbm_specs(ne),
        out_specs=[rev(NW), acc(8, 512), acc(HEADS, HD, HD), acc(HD, HD), acc(8, 1536), acc(8, HD)]
        + _hbm_specs(ne),
        scratch_shapes=[pltpu.VMEM((HEADS, HD, HD), F32), pltpu.VMEM((BLK + 8, 1536), F32)]
        + (_exchange_sems(ne) if ne else []),
        compiler_params=_cparams(("arbitrary",)),
    )(p, dmix, s_sv, t_sv, u_sv, w_sv, o_sv, conv_sv, lnp, sgu_w, sgu_bt, cw, gv, *exchange)


_SMALL = (("b_ada", 24), ("norm1_g", 8), ("norm2_g", 8), ("final_g", 8), ("sgu_ln_g", 8), ("sgu_ln_b", 8),
          ("sgu_w", 256), ("sgu_b", 8), ("conv_w", 24), ("a_log", 8), ("dt_bias", 8), ("gdn_norm_g", 8))
_SMALL_PAD = sum(n for _, n in _SMALL)
_DMOD_ROWS = 24


def _pack_rows(parts):
    rows = []
    for (name, n), a in zip(_SMALL, parts):
        flat = a.reshape(-1).astype(F32)
        rows.append(jnp.pad(flat, (0, n * D - flat.shape[0])).reshape(n, D))
    return jnp.concatenate(rows, axis=0)


def _unpack_rows(buf, shapes):
    out, r0 = {}, 0
    for name, n in _SMALL:
        size = math.prod(shapes[name])
        out[name] = buf[r0:r0 + n].reshape(-1)[:size].reshape(shapes[name])
        r0 += n
    return out


_COMBINED_ROWS = 2 * _DMOD_ROWS + _SMALL_PAD
_GATHER_ROWS = -(-_COMBINED_ROWS // 16) * 16


def _pair_combine(own, sib, place):
    rows = own.shape[0]

    def body(p_ref, a_ref, b_ref, o_ref):
        first = lax.axis_index("c") == 0
        a, b = a_ref[0:_DMOD_ROWS, :], b_ref[0:_DMOD_ROWS, :]
        o_ref[0:_DMOD_ROWS, :] = jnp.where(first, a, b)
        o_ref[_DMOD_ROWS:2 * _DMOD_ROWS, :] = jnp.where(first, b, a)
        o_ref[2 * _DMOD_ROWS:_COMBINED_ROWS, :] = a_ref[_DMOD_ROWS:, :] + b_ref[_DMOD_ROWS:, :]
        o_ref[_COMBINED_ROWS:, :] = jnp.zeros((_GATHER_ROWS - _COMBINED_ROWS, D), F32)

    return pl.pallas_call(
        body, name="small_pair_combine", out_shape=jax.ShapeDtypeStruct((N_CHIPS, _GATHER_ROWS, D), F32),
        grid_spec=pltpu.PrefetchScalarGridSpec(
            num_scalar_prefetch=1, grid=(1,),
            in_specs=[pl.BlockSpec((rows, D), lambda i, pr: (0, 0)), pl.BlockSpec((rows, D), lambda i, pr: (0, 0))],
            out_specs=pl.BlockSpec((None, _GATHER_ROWS, D), lambda i, pr: (pr[0], 0, 0))),
        compiler_params=_cparams(("arbitrary",)),
    )(place, own, sib)


def _small_finalize(gathered, w, m, v):
    def body(g_ref, w_ref, m_ref, v_ref, go_ref, d_ref, nm_ref, nv_ref):
        lo, hi = 2 * _DMOD_ROWS, _COMBINED_ROWS
        sm = g_ref[0, lo:hi, :] + g_ref[1, lo:hi, :]
        sm = sm + g_ref[2, lo:hi, :]
        sm = sm + g_ref[3, lo:hi, :]
        bsum = jnp.zeros((_DMOD_ROWS, D), F32)
        for j in range(N_CHIPS):
            bsum = bsum + g_ref[j, 0:_DMOD_ROWS, :]
            bsum = bsum + g_ref[j, _DMOD_ROWS:2 * _DMOD_ROWS, :]
        go_ref[0:_DMOD_ROWS, :] = bsum
        go_ref[_DMOD_ROWS:, :] = sm[_DMOD_ROWS:, :]
        d_ref[...], nm_ref[...], nv_ref[...] = _adam_math(w_ref[...], go_ref[...], m_ref[...], v_ref[...])

    return pl.pallas_call(
        body, name="small_finalize", out_shape=[jax.ShapeDtypeStruct(w.shape, F32)] * 4,
        compiler_params=pltpu.CompilerParams(vmem_limit_bytes=VMEM_LIMIT),
    )(gathered, w, m, v)


def kernel(x, c, w_ada, b_ada, norm1_g, w_in, sgu_ln_g, sgu_ln_b, sgu_w, sgu_b, conv_w, a_log, dt_bias, gdn_norm_g, w_out, norm2_g, w_ff1, w_ff2, final_g, loss_target, m_w_ada, m_b_ada, m_norm1_g, m_w_in, m_sgu_ln_g, m_sgu_ln_b, m_sgu_w, m_sgu_b, m_conv_w, m_a_log, m_dt_bias, m_gdn_norm_g, m_w_out, m_norm2_g, m_w_ff1, m_w_ff2, m_final_g, v_w_ada, v_b_ada, v_norm1_g, v_w_in, v_sgu_ln_g, v_sgu_ln_b, v_sgu_w, v_sgu_b, v_conv_w, v_a_log, v_dt_bias, v_gdn_norm_g, v_w_out, v_norm2_g, v_w_ff1, v_w_ff2, v_final_g):
    xi, yi, ci = lax.axis_index("x"), lax.axis_index("y"), lax.axis_index("c")
    chip = 2 * xi + yi
    dev = 2 * chip + ci
    t = x.shape[1]
    x0 = x.reshape(t, D)
    target = loss_target.reshape(t, D)

    c_sib = _pair_exchange(c, "c_pair")
    c_pair = jnp.where(ci == 0, jnp.concatenate([c, c_sib], 0), jnp.concatenate([c_sib, c], 0))
    c_all = _chip_allgather(c_pair, "c_chips").reshape(8, D)
    ada_cols = w_ada.shape[2]
    b_cols = lax.dynamic_slice_in_dim(b_ada, chip * ada_cols, ada_cols, axis=1)
    mod_part = _ada_forward(c_all, w_ada, b_cols)
    conv_cols = conv_w.shape[2]
    packed = jnp.concatenate([mod_part.reshape(DEPTH * 8, ada_cols), conv_w.reshape(DEPTH, 4 * conv_cols)], axis=0)
    packed = _chip_allgather(packed, "mod_chips")
    mod_all = packed[:, :DEPTH * 8].reshape(N_CHIPS, DEPTH, 8, ada_cols)
    mod_mine = lax.dynamic_index_in_dim(mod_all, dev, axis=2, keepdims=False)
    mod = mod_mine.transpose(1, 0, 2).reshape(DEPTH, 6, D)
    modv = jnp.concatenate([mod, norm1_g[:, None, :], norm2_g[:, None, :]], axis=1)
    conv_full = packed[:, DEPTH * 8:].reshape(N_CHIPS, DEPTH, 4, conv_cols).transpose(1, 2, 0, 3).reshape(DEPTH, 4, 1536)

    place = jnp.stack([chip, ci]).astype(jnp.int32)
    wbufs = [[_cast_into_slot(w, l, place) for w in (w_in, w_out, w_ff1, w_ff2)] for l in range(DEPTH)]
    wbufs[0][:1] = _weights_allgather(wbufs[0][:1], 0)

    def full_w_in(g):
        return jnp.pad(g.transpose(1, 0, 2).reshape(D, IN_W), ((0, 0), (0, NW - IN_W)))

    lnp = jnp.pad(jnp.stack([sgu_ln_g, sgu_ln_b], axis=1), ((0, 0), (0, 6), (0, 0)))
    sgu_bt = jnp.pad(sgu_b.transpose(0, 2, 1), ((0, 0), (0, 0), (0, HD - HEADS)))
    cw = jnp.pad(conv_full, ((0, 0), (0, 4), (0, 0)))
    lane_pad = lambda a: jnp.pad(a, ((0, 0), (4, HD - 8)))
    gv = jnp.pad(jnp.stack([lane_pad(a_log), lane_pad(dt_bias), gdn_norm_g], axis=1), ((0, 0), (0, 5), (0, 0)))

    acts = []
    xl = x0
    for l in range(DEPTH):
        win = full_w_in(wbufs[l][0])
        p, h1 = _fwd_in(xl, modv, win, l)
        nxt = wbufs[l][1:] + (wbufs[l + 1][:1] if l + 1 < DEPTH else [])
        mix, *rest = _mixer_forward(p, lnp, sgu_w, sgu_bt, cw, gv, l, gather=nxt)
        saved = rest[:6]
        wbufs[l][1:] = rest[6:9]
        if l + 1 < DEPTH:
            wbufs[l + 1][:1] = rest[9:]
        g_in, g_out, g_ff1, g_ff2 = wbufs[l]
        x1, r, h2 = _fwd_out_ff1(xl, mix, modv, g_out, g_ff1, l)
        x2 = _fwd_ff2(x1, r, modv, g_ff2, l)
        acts.append((xl, p, mix, saved, x1, r, h1, h2, win))
        xl = x2

    dx, head_stats = _loss_head(xl, target, final_g)
    loss = lax.psum(jnp.sum(head_stats[1, 0:1]), ("x", "y", "c"))
    d_final_g = head_stats[0]
    names = ("in", "out", "ff1", "ff2")
    grads_buf = [None] * len(names)

    def pair_sums(partials, from_sib, kinds, lay):
        return [(lay, n, _pair_sum(g, ga, place, f"pair_sum_{n}_{lay}")) for g, ga, n in zip(partials, from_sib, kinds)]

    def reduce_into_buffers(items, recv):
        for (lay, n, pair), rc in zip(items, recv):
            i = names.index(n)
            grads_buf[i] = _chip_sum(pair, rc, grads_buf[i], lay, place, f"chip_sum_{n}_{lay}")

    pending = []

    dmod, small = [None] * DEPTH, [None] * DEPTH
    for l in reversed(range(DEPTH)):
        xl, p, mix, saved, x1, r, h1, h2, win = acts[l]
        g_in, g_out, g_ff1, g_ff2 = wbufs[l]
        df = _bwd_ff2(dx, r, modv, g_ff2, l)
        gw_ff2, dg2 = _grad_weight(r, dx, modv, l, "ff2", g_ff2)
        gw_ff1, _ = _grad_weight(h2, df, modv, l, "ff1")
        dx1, st2, dmix, *sib_ff = _bwd_norm(df, g_ff1, x1, dx, modv, l, "ff1", send=[gw_ff1, gw_ff2], w_out=g_out)
        gw_out, dg1 = _grad_weight(mix, dx1, modv, l, "out", g_out)
        gw_out = gw_out.reshape(N_CHIPS, D // N_CHIPS, D)
        sib_out = _grads_pair_send([gw_out], f"out_{l}")
        pending = pending + pair_sums([gw_out, gw_ff1, gw_ff2], list(sib_out) + sib_ff, names[1:], l)
        dp, dlnp, dsguw, dsgub, dcw, dgv, *recv = _mixer_backward(p, dmix, saved, lnp, sgu_w, sgu_bt, cw, gv, l,
                                                                  exchange=[item[2] for item in pending])
        reduce_into_buffers(pending, recv)
        gw_in, _ = _grad_weight(h1, dp, modv, l, "in")
        gw_in_c = gw_in[:, :IN_W].reshape(D, N_CHIPS, IN_W // N_CHIPS).transpose(1, 0, 2)
        if l > 0:
            dx, st1, sib_in = _bwd_norm(dp, win, xl, dx1, modv, l, "in", send=[gw_in_c])
            pending = pair_sums([gw_in_c], [sib_in], names[:1], l)
        else:
            pending = pair_sums([gw_in_c], _grads_pair_send([gw_in_c], "in_0"), names[:1], l)
            dx, st1, *rest = _bwd_norm(dp, win, xl, dx1, modv, l, "in", share=grads_buf[1:],
                                       exchange=[item[2] for item in pending])
            grads_buf[1:] = rest[:3]
            reduce_into_buffers(pending, rest[3:])
        dmod[l] = jnp.stack([st1[0], st1[1], dg1[0], st2[0], st2[1], dg2[0]], axis=0)
        small[l] = dict(norm1_g=st1[2], norm2_g=st2[2], sgu_ln_g=dlnp[0], sgu_ln_b=dlnp[1], sgu_w=dsguw,
                        sgu_b=dsgub[:, :HEADS].T, conv_w=dcw[:4], a_log=dgv[0, 4:8], dt_bias=dgv[1, 4:8],
                        gdn_norm_g=dgv[2])
    grad_x = dx.reshape(1, t, D)

    stack = lambda k: jnp.stack([small[l][k] for l in range(DEPTH)], axis=0)
    small_grads = [jnp.zeros((DEPTH, 6 * D), F32), stack("norm1_g"), stack("norm2_g"), d_final_g, stack("sgu_ln_g"),
                   stack("sgu_ln_b"), stack("sgu_w"), stack("sgu_b"), stack("conv_w"), stack("a_log"),
                   stack("dt_bias"), stack("gdn_norm_g")]
    own = jnp.concatenate([jnp.stack(dmod, axis=0).reshape(_DMOD_ROWS, D), _pack_rows(small_grads)], axis=0)
    sib = _pair_exchange(own, "small_pair")
    gathered, = _weights_allgather([_pair_combine(own, sib, place)], "small")
    small_shapes = dict(b_ada=b_ada.shape, norm1_g=norm1_g.shape, norm2_g=norm2_g.shape, final_g=final_g.shape,
                        sgu_ln_g=sgu_ln_g.shape, sgu_ln_b=sgu_ln_b.shape, sgu_w=sgu_w.shape, sgu_b=sgu_b.shape,
                        conv_w=(DEPTH, 4, 1536), a_log=a_log.shape, dt_bias=dt_bias.shape,
                        gdn_norm_g=gdn_norm_g.shape)

    def full_conv(a):
        return lax.dynamic_update_slice_in_dim(jnp.zeros((DEPTH, 4, 1536), F32), a, chip * conv_cols, axis=2)

    def pack_state(b_, n1, n2, fg, lg, lb, sw, sb, cv, al, db, gn):
        return _pack_rows([b_, n1, n2, fg, lg, lb, sw, sb, full_conv(cv), al, db, gn])

    w_small = pack_state(b_ada, norm1_g, norm2_g, final_g, sgu_ln_g, sgu_ln_b, sgu_w, sgu_b, conv_w, a_log, dt_bias,
                         gdn_norm_g)
    m_small = pack_state(m_b_ada, m_norm1_g, m_norm2_g, m_final_g, m_sgu_ln_g, m_sgu_ln_b, m_sgu_w, m_sgu_b, m_conv_w,
                         m_a_log, m_dt_bias, m_gdn_norm_g)
    v_small = pack_state(v_b_ada, v_norm1_g, v_norm2_g, v_final_g, v_sgu_ln_g, v_sgu_ln_b, v_sgu_w, v_sgu_b, v_conv_w,
                         v_a_log, v_dt_bias, v_gdn_norm_g)
    small_out = _small_finalize(gathered, w_small, m_small, v_small)
    sg, sd, sm, sv = [_unpack_rows(a, small_shapes) for a in small_out]
    for dct in (sg, sd, sm, sv):
        dct["conv_w"] = lax.dynamic_slice_in_dim(dct["conv_w"], chip * conv_cols, conv_cols, axis=2)

    dmod_all = gathered[:, :2 * _DMOD_ROWS].reshape(8, DEPTH, 6 * D)
    dmod_cols = lax.dynamic_slice_in_dim(dmod_all, chip * ada_cols, ada_cols, axis=2).transpose(1, 0, 2)
    g_ada, d_ada, nm_ada, nv_ada = _ada_backward_adamw(c_all, dmod_cols, w_ada, m_w_ada, v_w_ada)

    grads_buf[:1] = _grads_pair_share(grads_buf[:1], "grads_pair_share_in")
    big = {}
    for n, g, (w, m, v) in zip(names, grads_buf, ((w_in, m_w_in, v_w_in), (w_out, m_w_out, v_w_out),
                                                  (w_ff1, m_w_ff1, v_w_ff1), (w_ff2, m_w_ff2, v_w_ff2))):
        big[n] = (g,) + tuple(_adamw(w, g, m, v, f"adamw_{n}"))

    def outs(k):
        s = (sg, sd, sm, sv)[k]
        return [(g_ada, d_ada, nm_ada, nv_ada)[k], s["b_ada"], s["norm1_g"], big["in"][k], s["sgu_ln_g"],
                s["sgu_ln_b"], s["sgu_w"], s["sgu_b"], s["conv_w"], s["a_log"], s["dt_bias"], s["gdn_norm_g"],
                big["out"][k], s["norm2_g"], big["ff1"][k], big["ff2"][k], s["final_g"]]

    return (loss, grad_x, *outs(0), *outs(1), *outs(2), *outs(3))
```

```python
import functools
import math

import jax
import jax.numpy as jnp
from jax import lax
from jax.experimental import pallas as pl
from jax.experimental.pallas import tpu as pltpu

F32 = jnp.float32
BF16 = jnp.bfloat16

DEPTH = 4
D = 1024
HEADS = 4
HD = 128
BLK = 128
IN_W = 3080
NW = 3200
GATE0 = 3072
DFF = 4096
N_CHIPS = 4
RMS_EPS = 1e-6
LN_EPS = 1e-5
QK_SCALE = HD ** -0.5
LR, B1, B2, ADAM_EPS, WD, STEP = 0.001, 0.9, 0.999, 1e-08, 0.01, 10
VMEM_LIMIT = 56 * 1024 * 1024
MESH = pl.DeviceIdType.MESH
HOPS = ((1, 0), (0, 1), (1, 1))
HI = lax.Precision.HIGHEST


def _dot(a, b):
    return jnp.dot(a.astype(BF16), b.astype(BF16), preferred_element_type=F32)


def _dot_nt(a, b):
    return lax.dot_general(a.astype(BF16), b.astype(BF16), (((1,), (1,)), ((), ())), preferred_element_type=F32)


def _dot_tn(a, b):
    return lax.dot_general(a.astype(BF16), b.astype(BF16), (((0,), (0,)), ((), ())), preferred_element_type=F32)


def _dotf(a, b):
    return jnp.dot(a, b, precision=HI, preferred_element_type=F32)


def _split(a):
    hi = a.astype(BF16)
    return hi, (a - hi.astype(F32)).astype(BF16)


def _dg3(a, b, dims, batch=((), ())):
    ah, al = _split(a)
    bh, bl = _split(b)
    f = lambda x, y: lax.dot_general(x, y, (dims, batch), preferred_element_type=F32)
    return f(ah, bh) + (f(ah, bl) + f(al, bh))


def _bmm3(a, b):
    return _dg3(a, b, ((2,), (1,)), ((0,), (0,)))


def _d3(a, b):
    return _dg3(a, b, ((1,), (0,)))


def _d3_nt(a, b):
    return _dg3(a, b, ((1,), (1,)))


def _d3_tn(a, b):
    return _dg3(a, b, ((0,), (0,)))


def _dotf_tn(a, b):
    return lax.dot_general(a, b, (((0,), (0,)), ((), ())), precision=HI, preferred_element_type=F32)


def _sigmoid(x):
    return 1.0 / (1.0 + jnp.exp(-x))


def _softplus(x):
    return jnp.maximum(x, 0.0) + jnp.log(1.0 + jnp.exp(-jnp.abs(x)))


_G0 = math.sqrt(2.0 / math.pi)
_G1 = 0.044715


def _gelu(x):
    t = jnp.tanh(_G0 * (x + _G1 * x * x * x))
    return 0.5 * x * (1.0 + t)


def _gelu_grad(x):
    t = jnp.tanh(_G0 * (x + _G1 * x * x * x))
    return 0.5 * (1.0 + t) + 0.5 * x * (1.0 - t * t) * (_G0 * (1.0 + 3.0 * _G1 * x * x))


def _silu(x):
    return x * _sigmoid(x)


def _silu_grad(x):
    s = _sigmoid(x)
    return s * (1.0 + x * (1.0 - s))


def _rms_stats(x):
    rstd = lax.rsqrt(jnp.mean(x * x, axis=-1, keepdims=True) + RMS_EPS)
    return x * rstd, rstd


def _norm_mod(x, ng, sc, sh):
    xh, _ = _rms_stats(x)
    return xh * (ng * (1.0 + sc)) + sh


def _norm_mod_bwd(dh, x, ng, sc):
    xh, rstd = _rms_stats(x)
    dsh = jnp.sum(dh, axis=0, keepdims=True)
    dsc = jnp.sum(dh * xh, axis=0, keepdims=True) * ng
    dng = jnp.sum(dh * xh, axis=0, keepdims=True) * (1.0 + sc)
    dxh = dh * (ng * (1.0 + sc))
    dx = rstd * (dxh - xh * jnp.mean(dxh * xh, axis=-1, keepdims=True))
    return dx, dsh, dsc, dng


def _iota2(shape, axis):
    return lax.broadcasted_iota(jnp.int32, shape, axis)


def _col(tile, idx):
    return jnp.sum(jnp.where(_iota2(tile.shape, 1) == idx, tile, 0.0), axis=1, keepdims=True)


def _col_slice(tile, idx):
    return tile[:, idx:idx + 1]


def _row(tile, idx):
    return jnp.sum(jnp.where(_iota2(tile.shape, 0) == idx, tile, 0.0), axis=0, keepdims=True)


def _put_col(col, idx, width=HD):
    shape = (col.shape[0], width)
    return jnp.where(_iota2(shape, 1) == idx, jnp.broadcast_to(col, shape), 0.0)


def _tri_inverse(m):
    rows, cols = _iota2(m.shape, m.ndim - 2), _iota2(m.shape, m.ndim - 1)
    mm = _bmm3 if m.ndim == 3 else _d3
    eye = jnp.where(rows == cols, 1.0, 0.0).astype(F32)
    n = jnp.where((rows >> 3) == (cols >> 3), -m, 0.0)
    p = eye + n
    n2 = mm(n, n)
    p = p + mm(n2, p)
    n4 = mm(n2, n2)
    p = p + mm(n4, p)
    for shift in (3, 4, 5, 6):
        same_pair = (rows >> (shift + 1)) == (cols >> (shift + 1))
        below = jnp.logical_and(((rows >> shift) & 1) == 1, ((cols >> shift) & 1) == 0)
        off = jnp.where(jnp.logical_and(same_pair, below), m, 0.0)
        p = p - mm(p, mm(off, p))
    return p


def _cparams(sem=None):
    return pltpu.CompilerParams(dimension_semantics=sem, vmem_limit_bytes=VMEM_LIMIT)


def _my_place():
    return lax.axis_index("x"), lax.axis_index("y"), lax.axis_index("c")


def _hop(xi, yi, hop):
    dx, dy = hop
    return (1 - xi if dx else xi), (1 - yi if dy else yi)


def _pair_exchange(x, name):
    def body(x_ref, o_ref, ssem, rsem):
        xi, yi, ci = _my_place()
        cp = pltpu.make_async_remote_copy(x_ref, o_ref, ssem, rsem, device_id=(xi, yi, 1 - ci), device_id_type=MESH)
        cp.start()
        cp.wait()

    return pl.pallas_call(
        body, name=name, out_shape=jax.ShapeDtypeStruct(x.shape, x.dtype),
        in_specs=[pl.BlockSpec(memory_space=pltpu.VMEM)], out_specs=pl.BlockSpec(memory_space=pltpu.VMEM),
        scratch_shapes=[pltpu.SemaphoreType.DMA, pltpu.SemaphoreType.DMA],
        compiler_params=pltpu.CompilerParams(vmem_limit_bytes=VMEM_LIMIT),
    )(x)


def _allgather_start(x_ref, o_ref, ssems, rsems, lsem):
    xi, yi, ci = _my_place()
    me = 2 * xi + yi
    pltpu.make_async_copy(x_ref, o_ref.at[me], lsem).start()
    for k, hop in enumerate(HOPS):
        tx, ty = _hop(xi, yi, hop)
        pltpu.make_async_remote_copy(x_ref, o_ref.at[me], ssems.at[k], rsems.at[k],
                                     device_id=(tx, ty, ci), device_id_type=MESH).start()


def _allgather_finish(x_ref, o_ref, ssems, rsems, lsem):
    xi, yi, ci = _my_place()
    me = 2 * xi + yi
    for k, hop in enumerate(HOPS):
        tx, ty = _hop(xi, yi, hop)
        cp = pltpu.make_async_remote_copy(x_ref, o_ref.at[2 * tx + ty], ssems.at[k], rsems.at[k],
                                          device_id=(tx, ty, ci), device_id_type=MESH)
        cp.wait_recv()
        cp.wait_send()
    pltpu.make_async_copy(x_ref, o_ref.at[me], lsem).wait()


_ALLGATHER_SEMS = [pltpu.SemaphoreType.DMA((3,)), pltpu.SemaphoreType.DMA((3,)), pltpu.SemaphoreType.DMA]


def _chip_allgather(x, name):
    def body(x_ref, o_ref, ssems, rsems, lsem):
        _allgather_start(x_ref, o_ref, ssems, rsems, lsem)
        _allgather_finish(x_ref, o_ref, ssems, rsems, lsem)

    return pl.pallas_call(
        body, name=name, out_shape=jax.ShapeDtypeStruct((N_CHIPS,) + x.shape, x.dtype),
        in_specs=[pl.BlockSpec(memory_space=pltpu.VMEM)], out_specs=pl.BlockSpec(memory_space=pltpu.VMEM),
        scratch_shapes=_ALLGATHER_SEMS, compiler_params=pltpu.CompilerParams(vmem_limit_bytes=VMEM_LIMIT),
    )(x)


def _hbm_specs(n):
    return [pl.BlockSpec(memory_space=pl.ANY)] * n


def _cast_into_slot(w, l, place):
    _, r, c = w.shape
    tr = _row_tile(r)

    def body(p_ref, w_ref, o_ref):
        o_ref[...] = w_ref[...].astype(BF16)

    return pl.pallas_call(
        body, name=f"cast_slot_{r}x{c}_{l}", out_shape=jax.ShapeDtypeStruct((N_CHIPS, r, c), BF16),
        grid_spec=pltpu.PrefetchScalarGridSpec(
            num_scalar_prefetch=1, grid=(r // tr,),
            in_specs=[pl.BlockSpec((None, tr, c), lambda k, pr: (l, k, 0))],
            out_specs=pl.BlockSpec((None, tr, c), lambda k, pr: (pr[0], k, 0))),
        compiler_params=_cparams(("parallel",)),
    )(place, w)


def _halves(ref, ci):
    half = ref.shape[-2] // 2
    return pl.ds(ci * half, half), pl.ds((1 - ci) * half, half)


def _gather_copies(bufs, sems):
    s_ici, r_ici, s_d2d, r_d2d = sems
    xi, yi, ci = _my_place()
    me = 2 * xi + yi
    ici_send, ici_recv, d2d_send, d2d_recv = [], [], [], []
    for i, buf in enumerate(bufs):
        mine, sibs = _halves(buf, ci)
        for k, hop in enumerate(HOPS):
            tx, ty = _hop(xi, yi, hop)
            src = 2 * tx + ty
            ici_send.append(pltpu.make_async_remote_copy(buf.at[me, mine], buf.at[me, mine], s_ici.at[i, k],
                                                         r_ici.at[i, k], device_id=(tx, ty, ci), device_id_type=MESH))
            ici_recv.append(pltpu.make_async_remote_copy(buf.at[src, mine], buf.at[src, mine], s_ici.at[i, k],
                                                         r_ici.at[i, k], device_id=(tx, ty, ci), device_id_type=MESH))
            d2d_send.append(pltpu.make_async_remote_copy(buf.at[src, mine], buf.at[src, mine], s_d2d.at[i, k],
                                                         r_d2d.at[i, k], device_id=(xi, yi, 1 - ci), device_id_type=MESH))
            d2d_recv.append(pltpu.make_async_remote_copy(buf.at[src, sibs], buf.at[src, sibs], s_d2d.at[i, k],
                                                         r_d2d.at[i, k], device_id=(xi, yi, 1 - ci), device_id_type=MESH))
    return ici_send, ici_recv, d2d_send, d2d_recv


def _gather_start(bufs, sems):
    for cp in _gather_copies(bufs, sems)[0]:
        cp.start()


def _gather_forward(bufs, sems):
    _, ici_recv, d2d_send, _ = _gather_copies(bufs, sems)
    for arrived, forward in zip(ici_recv, d2d_send):
        arrived.wait_recv()
        forward.start()


def _gather_finish(bufs, sems):
    ici_send, _, d2d_send, d2d_recv = _gather_copies(bufs, sems)
    for cp in d2d_recv:
        cp.wait_recv()
    for cp in ici_send + d2d_send:
        cp.wait_send()


def _gather_sems(n):
    return [pltpu.SemaphoreType.DMA((n, 3))] * 4


def _weights_allgather(bufs, l):
    n = len(bufs)

    def body(*refs):
        outs, sems = refs[n:2 * n], refs[2 * n:]
        _gather_start(outs, sems)
        _gather_forward(outs, sems)
        _gather_finish(outs, sems)

    return pl.pallas_call(
        body, name=f"weights_allgather_{l}",
        out_shape=[jax.ShapeDtypeStruct(b.shape, b.dtype) for b in bufs],
        in_specs=_hbm_specs(n), out_specs=_hbm_specs(n), input_output_aliases={i: i for i in range(n)},
        scratch_shapes=_gather_sems(n),
    )(*bufs)


def _pair_send_copies(gs, outs, sems):
    ssem, rsem = sems
    xi, yi, ci = _my_place()
    every = pl.ds(0, N_CHIPS)
    return [pltpu.make_async_remote_copy(g.at[every, _halves(g, ci)[1]], o, ssem.at[i], rsem.at[i],
                                         device_id=(xi, yi, 1 - ci), device_id_type=MESH)
            for i, (g, o) in enumerate(zip(gs, outs))]


def _pair_send_shapes(gs):
    return [jax.ShapeDtypeStruct((N_CHIPS, g.shape[1] // 2, g.shape[2]), g.dtype) for g in gs]


def _pair_send_sems(n):
    return [pltpu.SemaphoreType.DMA((n,)), pltpu.SemaphoreType.DMA((n,))]


def _grads_pair_send(gs, l):
    n = len(gs)

    def body(*refs):
        cps = _pair_send_copies(refs[:n], refs[n:2 * n], refs[2 * n:])
        for cp in cps:
            cp.start()
        for cp in cps:
            cp.wait()

    return pl.pallas_call(
        body, name=f"grads_pair_send_{l}", out_shape=_pair_send_shapes(gs),
        in_specs=_hbm_specs(n), out_specs=_hbm_specs(n), scratch_shapes=_pair_send_sems(n),
    )(*gs)


def _exchange_copies(ps, recvs, sems):
    ssems, rsems = sems
    xi, yi, ci = _my_place()
    cps = []
    for i, (p, rc) in enumerate(zip(ps, recvs)):
        for k, hop in enumerate(HOPS):
            tx, ty = _hop(xi, yi, hop)
            cps.append(pltpu.make_async_remote_copy(p.at[2 * tx + ty], rc.at[k], ssems.at[i, k], rsems.at[i, k],
                                                    device_id=(tx, ty, ci), device_id_type=MESH))
    return cps


def _exchange_sems(n):
    return [pltpu.SemaphoreType.DMA((n, 3))] * 2


def _exchange_shapes(ps):
    return [jax.ShapeDtypeStruct((3,) + p.shape[1:], p.dtype) for p in ps]


def _pair_share_copies(bufs, sems):
    ssem, rsem = sems
    xi, yi, ci = _my_place()
    every = pl.ds(0, DEPTH)
    sends, arrivals = [], []
    for i, buf in enumerate(bufs):
        mine, sibs = _halves(buf, ci)
        sends.append(pltpu.make_async_remote_copy(buf.at[every, mine], buf.at[every, mine], ssem.at[i], rsem.at[i],
                                                  device_id=(xi, yi, 1 - ci), device_id_type=MESH))
        arrivals.append(pltpu.make_async_remote_copy(buf.at[every, sibs], buf.at[every, sibs], ssem.at[i], rsem.at[i],
                                                     device_id=(xi, yi, 1 - ci), device_id_type=MESH))
    return sends, arrivals


def _pair_share_start(bufs, sems):
    for cp in _pair_share_copies(bufs, sems)[0]:
        cp.start()


def _pair_share_finish(bufs, sems):
    sends, arrivals = _pair_share_copies(bufs, sems)
    for cp in arrivals:
        cp.wait_recv()
    for cp in sends:
        cp.wait_send()


def _grads_pair_share(gs, name):
    n = len(gs)

    def body(*refs):
        _pair_share_start(refs[n:2 * n], refs[2 * n:])
        _pair_share_finish(refs[n:2 * n], refs[2 * n:])

    return pl.pallas_call(
        body, name=name,
        out_shape=[jax.ShapeDtypeStruct(g.shape, g.dtype) for g in gs],
        in_specs=_hbm_specs(n), out_specs=_hbm_specs(n), input_output_aliases={i: i for i in range(n)},
        scratch_shapes=_pair_send_sems(n),
    )(*gs)


def _row_tile(r):
    return min(r, 512)


def _pair_sum(g, ga, place, name):
    _, r, c = g.shape
    tr = _row_tile(r // 2)
    nk = r // 2 // tr

    def body(p_ref, g_ref, ga_ref, o_ref):
        o_ref[...] = (g_ref[...].astype(F32) + ga_ref[...].astype(F32)).astype(o_ref.dtype)

    return pl.pallas_call(
        body, name=name, out_shape=jax.ShapeDtypeStruct(ga.shape, ga.dtype),
        grid_spec=pltpu.PrefetchScalarGridSpec(
            num_scalar_prefetch=1, grid=(N_CHIPS, nk),
            in_specs=[pl.BlockSpec((None, tr, c), lambda j, k, pr: (j, pr[1] * nk + k, 0)),
                      pl.BlockSpec((None, tr, c), lambda j, k, pr: (j, k, 0))],
            out_specs=pl.BlockSpec((None, tr, c), lambda j, k, pr: (j, k, 0))),
        compiler_params=_cparams(("parallel", "parallel")),
    )(place, g, ga)


def _chip_sum(pair, recv, buf, l, place, name):
    _, rh, c = pair.shape
    tr = _row_tile(rh)
    nk = rh // tr

    def body(p_ref, own_ref, r_ref, *rest):
        o_ref = rest[-1]
        acc = own_ref[...].astype(F32) + r_ref[0].astype(F32)
        acc = acc + r_ref[1].astype(F32)
        o_ref[...] = acc + r_ref[2].astype(F32)

    in_specs = [pl.BlockSpec((None, tr, c), lambda k, pr: (pr[0], k, 0)),
                pl.BlockSpec((3, tr, c), lambda k, pr: (0, k, 0))]
    args = [pair, recv]
    aliases = {}
    if buf is not None:
        in_specs.append(pl.BlockSpec(memory_space=pl.ANY))
        args.append(buf)
        aliases = {3: 0}
    return pl.pallas_call(
        body, name=name, out_shape=jax.ShapeDtypeStruct((DEPTH, 2 * rh, c), F32),
        grid_spec=pltpu.PrefetchScalarGridSpec(
            num_scalar_prefetch=1, grid=(nk,), in_specs=in_specs,
            out_specs=pl.BlockSpec((None, tr, c), lambda k, pr: (l, pr[1] * nk + k, 0))),
        input_output_aliases=aliases, compiler_params=_cparams(("parallel",)),
    )(place, *args)


def _adam_math(w, g, m, v):
    m = B1 * m + (1.0 - B1) * g
    v = B2 * v + (1.0 - B2) * (g * g)
    m_hat = m / (1.0 - B1 ** STEP)
    v_hat = v / (1.0 - B2 ** STEP)
    delta = -LR * (m_hat / (jnp.sqrt(v_hat) + ADAM_EPS) + WD * w)
    return delta, m, v


def _adamw(w, g, m, v, name):
    n_l, r, c = w.shape
    tr = _row_tile(r)

    def body(w_ref, g_ref, m_ref, v_ref, go_ref, d_ref, nm_ref, nv_ref):
        g = g_ref[...]
        go_ref[...] = g
        d_ref[...], nm_ref[...], nv_ref[...] = _adam_math(w_ref[...], g, m_ref[...], v_ref[...])

    spec = pl.BlockSpec((None, tr, c), lambda i, k: (i, k, 0))
    return pl.pallas_call(
        body, name=name, out_shape=[jax.ShapeDtypeStruct(w.shape, F32)] * 4, grid=(n_l, r // tr),
        in_specs=[spec] * 4, out_specs=[spec] * 4, compiler_params=_cparams(("parallel", "parallel")),
    )(w, g, m, v)


def _ada_forward(c_all, w_ada, b_cols):
    cols = w_ada.shape[2]
    tn = 512

    def body(c_ref, w_ref, b_ref, o_ref):
        o_ref[...] = _dotf(_silu(c_ref[...]), w_ref[...]) + b_ref[...]

    return pl.pallas_call(
        body, name="ada_forward", out_shape=jax.ShapeDtypeStruct((DEPTH, 8, cols), F32), grid=(DEPTH, cols // tn),
        in_specs=[pl.BlockSpec((8, D), lambda l, j: (0, 0)),
                  pl.BlockSpec((None, D, tn), lambda l, j: (l, 0, j)),
                  pl.BlockSpec((None, 1, tn), lambda l, j: (l, 0, j))],
        out_specs=pl.BlockSpec((None, 8, tn), lambda l, j: (l, 0, j)),
        compiler_params=_cparams(("parallel", "parallel")),
    )(c_all, w_ada, b_cols.reshape(DEPTH, 1, cols))


def _ada_backward_adamw(c_all, dmod_cols, w, m, v):
    cols = w.shape[2]
    tn = 512

    def body(c_ref, d_ref, w_ref, m_ref, v_ref, g_ref, dl_ref, nm_ref, nv_ref):
        g = _dotf_tn(_silu(c_ref[...]), d_ref[...])
        g_ref[...] = g
        dl_ref[...], nm_ref[...], nv_ref[...] = _adam_math(w_ref[...], g, m_ref[...], v_ref[...])

    wspec = pl.BlockSpec((None, D, tn), lambda l, j: (l, 0, j))
    return pl.pallas_call(
        body, name="ada_backward_adamw", out_shape=[jax.ShapeDtypeStruct(w.shape, F32)] * 4, grid=(DEPTH, cols // tn),
        in_specs=[pl.BlockSpec((8, D), lambda l, j: (0, 0)), pl.BlockSpec((None, 8, tn), lambda l, j: (l, 0, j)),
                  wspec, wspec, wspec],
        out_specs=[wspec] * 4, compiler_params=_cparams(("parallel", "parallel")),
    )(c_all, dmod_cols, w, m, v)


def _tok_tile(t):
    return min(t, 512)


def _wspec4(r, c, l):
    return pl.BlockSpec((N_CHIPS, r, c), lambda i: (0, 0, 0))


def _fwd_in(x, modv, w_in, l):
    t = x.shape[0]
    tm = _tok_tile(t)

    def body(x_ref, mod_ref, w_ref, o_ref, h_ref):
        h = _norm_mod(x_ref[...], mod_ref[6:7, :], mod_ref[1:2, :], mod_ref[0:1, :]).astype(BF16)
        h_ref[...] = h
        o_ref[...] = jnp.dot(h, w_ref[...], preferred_element_type=F32)

    return pl.pallas_call(
        body, name=f"fwd_in_{l}", grid=(t // tm,),
        out_shape=[jax.ShapeDtypeStruct((t, NW), F32), jax.ShapeDtypeStruct((t, D), BF16)],
        in_specs=[pl.BlockSpec((tm, D), lambda i: (i, 0)), pl.BlockSpec((None, 8, D), lambda i: (l, 0, 0)),
                  pl.BlockSpec((D, NW), lambda i: (0, 0))],
        out_specs=[pl.BlockSpec((tm, NW), lambda i: (i, 0)), pl.BlockSpec((tm, D), lambda i: (i, 0))],
        compiler_params=_cparams(("parallel",)),
    )(x, modv, w_in)


def _fwd_out_ff1(x, mix, modv, w_out, w_ff1, l):
    t = x.shape[0]
    tm = _tok_tile(t)

    def body(x_ref, mix_ref, mod_ref, wo_ref, w_ref, x1_ref, o_ref, h_ref):
        x1 = x_ref[...] + mod_ref[2:3, :] * jnp.dot(mix_ref[...], wo_ref[...].reshape(D, D), preferred_element_type=F32)
        x1_ref[...] = x1
        h = _norm_mod(x1, mod_ref[7:8, :], mod_ref[4:5, :], mod_ref[3:4, :]).astype(BF16)
        h_ref[...] = h
        for j in range(N_CHIPS):
            f = jnp.dot(h, w_ref[j], preferred_element_type=F32)
            o_ref[:, j * D:(j + 1) * D] = jnp.maximum(f, 0.0).astype(BF16)

    tok = pl.BlockSpec((tm, D), lambda i: (i, 0))
    return pl.pallas_call(
        body, name=f"fwd_out_ff1_{l}", grid=(t // tm,),
        out_shape=[jax.ShapeDtypeStruct((t, D), F32), jax.ShapeDtypeStruct((t, DFF), BF16),
                   jax.ShapeDtypeStruct((t, D), BF16)],
        in_specs=[tok, tok, pl.BlockSpec((None, 8, D), lambda i: (l, 0, 0)), _wspec4(D // N_CHIPS, D, l),
                  _wspec4(D, D, l)],
        out_specs=[tok, pl.BlockSpec((tm, DFF), lambda i: (i, 0)), tok],
        compiler_params=_cparams(("parallel",)),
    )(x, mix, modv, w_out, w_ff1)


def _fwd_ff2(x, r, modv, w_ff2, l):
    t = x.shape[0]
    tm = _tok_tile(t)

    def body(x_ref, r_ref, mod_ref, w_ref, o_ref):
        acc = jnp.zeros((tm, D), F32)
        for j in range(N_CHIPS):
            rj = r_ref[:, j * D:(j + 1) * D].astype(F32)
            acc = acc + jnp.dot((rj * rj).astype(BF16), w_ref[j], preferred_element_type=F32)
        o_ref[...] = x_ref[...] + mod_ref[5:6, :] * acc

    return pl.pallas_call(
        body, name=f"fwd_ff2_{l}", out_shape=jax.ShapeDtypeStruct((t, D), F32), grid=(t // tm,),
        in_specs=[pl.BlockSpec((tm, D), lambda i: (i, 0)), pl.BlockSpec((tm, DFF), lambda i: (i, 0)),
                  pl.BlockSpec((None, 8, D), lambda i: (l, 0, 0)), _wspec4(D, D, l)],
        out_specs=pl.BlockSpec((tm, D), lambda i: (i, 0)), compiler_params=_cparams(("parallel",)),
    )(x, r, modv, w_ff2)


def _loss_head(x, target, final_g):
    t = x.shape[0]
    tm = _tok_tile(t)

    def body(x_ref, t_ref, g_ref, dx_ref, st_ref):
        @pl.when(pl.program_id(0) == 0)
        def _():
            st_ref[...] = jnp.zeros_like(st_ref)

        xh, rstd = _rms_stats(x_ref[...])
        g = g_ref[...]
        err = xh * g - t_ref[...]
        loss = 0.5 * jnp.sum(jnp.mean(err * err, axis=-1, keepdims=True), axis=0, keepdims=True)
        dy = err * (1.0 / D)
        st_ref[0:1, :] += jnp.sum(dy * xh, axis=0, keepdims=True)
        st_ref[1:2, :] += jnp.broadcast_to(loss, (1, D))
        dxh = dy * g
        dx_ref[...] = rstd * (dxh - xh * jnp.mean(dxh * xh, axis=-1, keepdims=True))

    return pl.pallas_call(
        body, name="loss_head", out_shape=[jax.ShapeDtypeStruct((t, D), F32), jax.ShapeDtypeStruct((8, D), F32)],
        grid=(t // tm,),
        in_specs=[pl.BlockSpec((tm, D), lambda i: (i, 0)), pl.BlockSpec((tm, D), lambda i: (i, 0)),
                  pl.BlockSpec((1, D), lambda i: (0, 0))],
        out_specs=[pl.BlockSpec((tm, D), lambda i: (i, 0)), pl.BlockSpec((8, D), lambda i: (0, 0))],
        compiler_params=_cparams(("arbitrary",)),
    )(x, target, final_g.reshape(1, D))


def _bwd_ff2(dx2, r, modv, w_ff2, l):
    t = dx2.shape[0]
    tm = _tok_tile(t)

    def body(d_ref, r_ref, mod_ref, w_ref, o_ref):
        dyg = (d_ref[...] * mod_ref[5:6, :]).astype(BF16)
        for j in range(N_CHIPS):
            da = lax.dot_general(dyg, w_ref[j], (((1,), (1,)), ((), ())), preferred_element_type=F32)
            o_ref[:, j * D:(j + 1) * D] = (da * 2.0 * r_ref[:, j * D:(j + 1) * D].astype(F32)).astype(BF16)

    return pl.pallas_call(
        body, name=f"bwd_ff2_{l}", out_shape=jax.ShapeDtypeStruct((t, DFF), BF16), grid=(t // tm,),
        in_specs=[pl.BlockSpec((tm, D), lambda i: (i, 0)), pl.BlockSpec((tm, DFF), lambda i: (i, 0)),
                  pl.BlockSpec((None, 8, D), lambda i: (l, 0, 0)), _wspec4(D, D, l)],
        out_specs=pl.BlockSpec((tm, DFF), lambda i: (i, 0)), compiler_params=_cparams(("parallel",)),
    )(dx2, r, modv, w_ff2)


def _bwd_norm(dy, w, x, dres, modv, l, which, send=(), w_out=None, share=(), exchange=()):
    t = x.shape[0]
    tm = _tok_tile(t)
    nsteps = t // tm
    rows = (6, 1) if which == "in" else (7, 4)
    width = dy.shape[1]
    ns, nh, ne = len(send), len(share), len(exchange)
    nb_ = 0 if w_out is None else 1
    n_in = 5 + nb_ + ns + nh + ne

    def body(*refs):
        dy_ref, w_ref, x_ref, dr_ref, mod_ref = refs[:5]
        wo_ref = refs[5] if nb_ else None
        parts = refs[5 + nb_:5 + nb_ + ns]
        pairs = refs[5 + nb_ + ns + nh:n_in]
        dx_ref, st_ref = refs[n_in:n_in + 2]
        dmix_ref = refs[n_in + 2] if nb_ else None
        o0 = n_in + 2 + nb_
        from_sib, shared, recvs = refs[o0:o0 + ns], refs[o0 + ns:o0 + ns + nh], refs[o0 + ns + nh:o0 + ns + nh + ne]
        scratch = list(refs[o0 + ns + nh + ne:])
        sems = [scratch.pop(0) for _ in range(2 if ns else 0)]
        hsems = [scratch.pop(0) for _ in range(2 if nh else 0)]
        esems = scratch

        @pl.when(pl.program_id(0) == 0)
        def _():
            st_ref[...] = jnp.zeros_like(st_ref)
            if ns:
                for cp in _pair_send_copies(parts, from_sib, sems):
                    cp.start()
            if nh:
                _pair_share_start(shared, hsems)
            if ne:
                for cp in _exchange_copies(pairs, recvs, esems):
                    cp.start()

        if which == "in":
            dh = lax.dot_general(dy_ref[...], w_ref[...], (((1,), (1,)), ((), ())), preferred_element_type=F32)
        else:
            dh = jnp.zeros((tm, D), F32)
            for j in range(N_CHIPS):
                dh = dh + lax.dot_general(dy_ref[:, j * D:(j + 1) * D], w_ref[j], (((1,), (1,)), ((), ())),
                                          preferred_element_type=F32)
        ng, sc = mod_ref[rows[0]:rows[0] + 1, :], mod_ref[rows[1]:rows[1] + 1, :]
        dx, dsh, dsc, dng = _norm_mod_bwd(dh, x_ref[...], ng, sc)
        dx_new = dr_ref[...] + dx
        dx_ref[...] = dx_new
        st_ref[0:1, :] += dsh
        st_ref[1:2, :] += dsc
        st_ref[2:3, :] += dng
        if nb_:
            dyg = (dx_new * mod_ref[2:3, :]).astype(BF16)
            dmix_ref[...] = lax.dot_general(dyg, wo_ref[...].reshape(D, D), (((1,), (1,)), ((), ())),
                                            preferred_element_type=F32).astype(BF16)

        if ns:
            @pl.when(pl.program_id(0) == nsteps - 1)
            def _():
                for cp in _pair_send_copies(parts, from_sib, sems):
                    cp.wait()

        if nh:
            @pl.when(pl.program_id(0) == nsteps - 1)
            def _():
                _pair_share_finish(shared, hsems)

        if ne:
            @pl.when(pl.program_id(0) == nsteps - 1)
            def _():
                for cp in _exchange_copies(pairs, recvs, esems):
                    cp.wait()

    tok = pl.BlockSpec((tm, D), lambda i: (i, 0))
    wspec = pl.BlockSpec((D, NW), lambda i: (0, 0)) if which == "in" else _wspec4(D, D, l)
    return pl.pallas_call(
        body, name=f"bwd_norm_{which}_{l}",
        out_shape=[jax.ShapeDtypeStruct((t, D), F32), jax.ShapeDtypeStruct((8, D), F32)]
        + [jax.ShapeDtypeStruct((t, D), BF16)] * nb_ + _pair_send_shapes(send)
        + [jax.ShapeDtypeStruct(g.shape, g.dtype) for g in share] + _exchange_shapes(exchange),
        grid=(nsteps,),
        in_specs=[pl.BlockSpec((tm, width), lambda i: (i, 0)), wspec, tok, tok,
                  pl.BlockSpec((None, 8, D), lambda i: (l, 0, 0))]
        + [_wspec4(D // N_CHIPS, D, l)] * nb_ + _hbm_specs(ns + nh + ne),
        out_specs=[tok, pl.BlockSpec((8, D), lambda i: (0, 0))] + [tok] * nb_ + _hbm_specs(ns + nh + ne),
        input_output_aliases={5 + nb_ + ns + i: 2 + nb_ + ns + i for i in range(nh)},
        scratch_shapes=(_pair_send_sems(ns) if ns else []) + (_pair_send_sems(nh) if nh else [])
        + (_exchange_sems(ne) if ne else []),
        compiler_params=_cparams(("arbitrary",)),
    )(dy, w, x, dres, modv, *([w_out] * nb_), *send, *share, *exchange)


def _grad_weight(lhs, rhs, modv, l, which, w_gate=None):
    t = lhs.shape[0]
    tm = min(t, 2048)
    nt = t // tm
    gated = which in ("out", "ff2")
    if which == "in":
        nj, lw, rw, orows, ocols = 5, D, NW // 5, D, NW // 5
    elif which == "ff1":
        nj, lw, rw, orows, ocols = N_CHIPS, D, D, D, D
    elif which == "out":
        nj, lw, rw, orows, ocols = 1, D, D, D, D
    else:
        nj, lw, rw, orows, ocols = N_CHIPS, D, D, D, D
    gate_row = 2 if which == "out" else 5

    def body(*refs):
        if gated:
            l_ref, r_ref, mod_ref, wg_ref, o_ref, dg_ref, acc = refs
        else:
            l_ref, r_ref, mod_ref, o_ref, acc = refs
        j, k = pl.program_id(0), pl.program_id(1)

        @pl.when(k == 0)
        def _():
            acc[...] = jnp.zeros_like(acc)

        if which == "ff2":
            lv = l_ref[...].astype(F32)
            lv = lv * lv
        else:
            lv = l_ref[...]
        acc[...] += _dot_tn(lv, r_ref[...])

        if gated:
            @pl.when(jnp.logical_and(j == 0, k == 0))
            def _():
                dg_ref[...] = jnp.zeros_like(dg_ref)

        @pl.when(k == nt - 1)
        def _():
            raw = acc[...]
            if gated:
                o_ref[...] = (raw * mod_ref[gate_row:gate_row + 1, :]).astype(o_ref.dtype)
                dg_ref[0:1, :] += jnp.sum(raw * wg_ref[...].astype(F32), axis=0, keepdims=True)
            else:
                o_ref[...] = raw.astype(o_ref.dtype)

    if which in ("in", "ff1"):
        lspec = pl.BlockSpec((tm, lw), lambda j, k: (k, 0))
        rspec = pl.BlockSpec((tm, rw), lambda j, k: (k, j))
    else:
        lspec = pl.BlockSpec((tm, lw), lambda j, k: (k, j))
        rspec = pl.BlockSpec((tm, rw), lambda j, k: (k, 0))
    mspec = pl.BlockSpec((None, 8, D), lambda j, k: (l, 0, 0))
    flat = which in ("in", "out")
    if flat:
        ospec = pl.BlockSpec((orows, ocols), lambda j, k: (0, j))
        out_shape = [jax.ShapeDtypeStruct((D, nj * ocols), BF16)]
    else:
        ospec = pl.BlockSpec((None, orows, ocols), lambda j, k: (j, 0, 0))
        out_shape = [jax.ShapeDtypeStruct((N_CHIPS, orows, ocols), BF16)]
    in_specs = [lspec, rspec, mspec]
    args = [lhs, rhs, modv]
    out_specs = [ospec]
    if gated:
        if flat:
            in_specs.append(pl.BlockSpec((orows, ocols), lambda j, k: (0, 0)))
            args.append(w_gate.reshape(D, D))
        else:
            in_specs.append(pl.BlockSpec((None, orows, ocols), lambda j, k: (j, 0, 0)))
            args.append(w_gate)
        out_specs.append(pl.BlockSpec((8, D), lambda j, k: (0, 0)))
        out_shape.append(jax.ShapeDtypeStruct((8, D), F32))
    res = pl.pallas_call(
        body, name=f"grad_w_{which}_{l}", out_shape=out_shape, grid=(nj, nt), in_specs=in_specs, out_specs=out_specs,
        scratch_shapes=[pltpu.VMEM((orows, ocols), F32)], compiler_params=_cparams(("arbitrary", "arbitrary")),
    )(*args)
    return (res[0], res[1]) if gated else (res[0], None)


def _tri_masks():
    rows, cols = _iota2((BLK, BLK), 0), _iota2((BLK, BLK), 1)
    return rows >= cols, rows > cols


def _sgu_forward(p_ref, lnp_ref, sguw_ref, sgub_ref, col=_col):
    incl, _ = _tri_masks()
    ug = _gelu(p_ref[:, 0:512])
    vg = _gelu(p_ref[:, 512:1024])
    mu = jnp.mean(vg, axis=-1, keepdims=True)
    xc = vg - mu
    rstd = lax.rsqrt(jnp.mean(xc * xc, axis=-1, keepdims=True) + LN_EPS)
    vhat = xc * rstd
    vn = vhat * lnp_ref[0:1, :] + lnp_ref[1:2, :]
    bias = sgub_ref[...]
    ys, mixed, wms = [], [], []
    for h in range(HEADS):
        wm = jnp.where(incl, sguw_ref[h], 0.0)
        mx = _dot(wm, vn[:, h * HD:(h + 1) * HD]) + col(bias, h)
        ys.append(ug[:, h * HD:(h + 1) * HD] * mx)
        mixed.append(mx)
        wms.append(wm)
    return ys, ug, vhat, rstd, vn, mixed, wms


def _conv_forward(xbuf, cw_ref):
    conv = cw_ref[0:1, :] * xbuf[5:5 + BLK, :]
    for j in range(1, 4):
        conv = conv + cw_ref[j:j + 1, :] * xbuf[5 + j:5 + j + BLK, :]
    return conv


def _gates(gt, gv_ref):
    incl, _ = _tri_masks()
    beta = _sigmoid(gt)
    neg_a = -jnp.exp(gv_ref[0:1, :])
    gl = neg_a * _softplus(gt + gv_ref[1:2, :])
    gc = _dotf(jnp.where(incl, 1.0, 0.0).astype(F32), gl)
    return beta, gl, gc, gc.T, neg_a


def _head_chunk(act, beta, gc, gct, h, col=_col):
    incl, strict = _tri_masks()
    qh = act[:, h * HD:(h + 1) * HD]
    kh = act[:, 512 + h * HD:512 + (h + 1) * HD]
    vh = act[:, 1024 + h * HD:1024 + (h + 1) * HD]
    rq = lax.rsqrt(jnp.sum(qh * qh, axis=-1, keepdims=True) + RMS_EPS)
    rk = lax.rsqrt(jnp.sum(kh * kh, axis=-1, keepdims=True) + RMS_EPS)
    qhat, khat = qh * rq, kh * rk
    qn = qhat * QK_SCALE
    b = col(beta, h)
    gcol = col(gc, 4 + h)
    grow = _row(gct, 4 + h)
    dmat = jnp.where(incl, jnp.exp(jnp.where(incl, gcol - grow, 0.0)), 0.0)
    gam = jnp.exp(gcol)
    glast = _row(gcol, BLK - 1)
    e = jnp.exp(glast - gcol)
    kk = _d3_nt(khat, khat)
    return dict(qhat=qhat, khat=khat, qn=qn, vh=vh, rq=rq, rk=rk, b=b, dmat=dmat, gam=gam, glast=glast, e=e, kk=kk,
                strict=strict, incl=incl)


def _mixer_forward(p, lnp, sgu_w, sgu_bt, cw, gv, l, gather=()):
    t = p.shape[0]
    nb = t // BLK
    ng = len(gather)

    def body(*refs):
        p_ref, lnp_ref, sguw_ref, sgub_ref, cw_ref, gv_ref = refs[:6]
        mix_ref, s_out, t_out, u_out, w_out, o_out, conv_out = refs[6 + ng:13 + ng]
        gbufs = refs[13 + ng:13 + 2 * ng]
        s_scr, xbuf = refs[13 + 2 * ng:15 + 2 * ng]
        gsems = refs[15 + 2 * ng:]

        @pl.when(pl.program_id(0) == 0)
        def _():
            s_scr[...] = jnp.zeros_like(s_scr)
            xbuf[0:8, :] = jnp.zeros((8, 1536), F32)
            if ng:
                _gather_start(gbufs, gsems)

        ys = _sgu_forward(p_ref, lnp_ref, sguw_ref, sgub_ref, col=_col_slice)[0]
        for h in range(HEADS):
            mix_ref[:, h * HD:(h + 1) * HD] = ys[h].astype(BF16)

        xbuf[8:8 + BLK, :] = p_ref[:, 1024:2560]
        conv = _conv_forward(xbuf, cw_ref)
        conv_out[...] = conv
        act = _silu(conv)
        xbuf[0:8, :] = xbuf[BLK:BLK + 8, :]
        beta, _, gc, gct, _ = _gates(p_ref[:, GATE0:NW], gv_ref)
        chunks = [_head_chunk(act, beta, gc, gct, h, col=_col_slice) for h in range(HEADS)]
        for h, hc in enumerate(chunks):
            t_out[h] = jnp.where(hc["strict"], hc["b"] * hc["kk"] * hc["dmat"], 0.0)
        t_out[...] = _tri_inverse(t_out[...])
        for h, hc in enumerate(chunks):
            tm = t_out[h]
            u = _dot(tm, hc["b"] * hc["vh"])
            w = _dot(tm, (hc["b"] * hc["gam"]) * hc["khat"])
            qkm = _dot_nt(hc["qn"], hc["khat"]) * hc["dmat"]
            s = s_scr[h]
            wn = u - _dot(w, s)
            o = _dot(hc["qn"] * hc["gam"], s) + _dot(qkm, wn)
            s_out[h] = s
            s_scr[h] = jnp.exp(hc["glast"]) * s + _dot_tn(hc["khat"] * hc["e"], wn)
            sl = slice(h * HD, (h + 1) * HD)
            u_out[:, sl] = u
            w_out[:, sl] = w
            o_out[:, sl] = o
            on = o * lax.rsqrt(jnp.mean(o * o, axis=-1, keepdims=True) + RMS_EPS) * gv_ref[2:3, :]
            mix_ref[:, 512 + h * HD:512 + (h + 1) * HD] = (on * _silu(p_ref[:, 2560 + h * HD:2560 + (h + 1) * HD])).astype(BF16)

        if ng:
            @pl.when(pl.program_id(0) == nb - 1 - min(3, nb - 1))
            def _():
                _gather_forward(gbufs, gsems)

            @pl.when(pl.program_id(0) == nb - 1)
            def _():
                _gather_finish(gbufs, gsems)

    tok = lambda w: pl.BlockSpec((BLK, w), lambda i: (i, 0))
    st = pl.BlockSpec((None, HEADS, HD, HD), lambda i: (i, 0, 0, 0))
    return pl.pallas_call(
        body, name=f"mixer_fwd_{l}", grid=(nb,),
        out_shape=[jax.ShapeDtypeStruct((t, D), BF16), jax.ShapeDtypeStruct((nb, HEADS, HD, HD), F32),
                   jax.ShapeDtypeStruct((nb, HEADS, HD, HD), F32), jax.ShapeDtypeStruct((t, 512), F32),
                   jax.ShapeDtypeStruct((t, 512), F32), jax.ShapeDtypeStruct((t, 512), F32),
                   jax.ShapeDtypeStruct((t, 1536), F32)]
        + [jax.ShapeDtypeStruct(b.shape, b.dtype) for b in gather],
        in_specs=[tok(NW), pl.BlockSpec((None, 8, 512), lambda i: (l, 0, 0)),
                  pl.BlockSpec((None, HEADS, HD, HD), lambda i: (l, 0, 0, 0)),
                  pl.BlockSpec((None, HD, HD), lambda i: (l, 0, 0)), pl.BlockSpec((None, 8, 1536), lambda i: (l, 0, 0)),
                  pl.BlockSpec((None, 8, HD), lambda i: (l, 0, 0))] + _hbm_specs(ng),
        out_specs=[tok(D), st, st, tok(512), tok(512), tok(512), tok(1536)]
        + _hbm_specs(ng),
        input_output_aliases={6 + i: 7 + i for i in range(ng)},
        scratch_shapes=[pltpu.VMEM((HEADS, HD, HD), F32), pltpu.VMEM((BLK + 8, 1536), F32)]
        + (_gather_sems(ng) if ng else []),
        compiler_params=_cparams(("arbitrary",)),
    )(p, lnp, sgu_w, sgu_bt, cw, gv, *gather)


def _mixer_backward(p, dmix, saved, lnp, sgu_w, sgu_bt, cw, gv, l, exchange=()):
    t = p.shape[0]
    nb = t // BLK
    s_sv, t_sv, u_sv, w_sv, o_sv, conv_sv = saved
    ne = len(exchange)

    def body(*refs):
        (p_ref, dmix_ref, s_ref, t_ref, u_ref, w_ref, o_ref, conv_ref, lnp_ref, sguw_ref, sgub_ref, cw_ref,
         gv_ref) = refs[:13]
        pairs = refs[13:13 + ne]
        dp_ref, dlnp_ref, dsguw_ref, dsgub_ref, dcw_ref, dgv_ref = refs[13 + ne:19 + ne]
        recvs = refs[19 + ne:19 + 2 * ne]
        ds_scr, dcbuf = refs[19 + 2 * ne:21 + 2 * ne]
        esems = refs[21 + 2 * ne:]

        @pl.when(pl.program_id(0) == 0)
        def _():
            if ne:
                for cp in _exchange_copies(pairs, recvs, esems):
                    cp.start()
            ds_scr[...] = jnp.zeros_like(ds_scr)
            dcbuf[BLK:BLK + 8, :] = jnp.zeros((8, 1536), F32)
            dlnp_ref[...] = jnp.zeros_like(dlnp_ref)
            dsguw_ref[...] = jnp.zeros_like(dsguw_ref)
            dsgub_ref[...] = jnp.zeros_like(dsgub_ref)
            dcw_ref[...] = jnp.zeros_like(dcw_ref)
            dgv_ref[...] = jnp.zeros_like(dgv_ref)

        incl, strict = _tri_masks()
        _, ug, vhat, rstd, vn, mixed, wms = _sgu_forward(p_ref, lnp_ref, sguw_ref, sgub_ref)
        dvn_parts, dug_parts = [], []
        dbias = jnp.zeros((BLK, HD), F32)
        for h in range(HEADS):
            sl = slice(h * HD, (h + 1) * HD)
            dy = dmix_ref[:, sl].astype(F32)
            dmx = dy * ug[:, sl]
            dug_parts.append(dy * mixed[h])
            dsguw_ref[h] += jnp.where(incl, _dot_nt(dmx, vn[:, sl]), 0.0)
            dbias = dbias + _put_col(jnp.sum(dmx, axis=1, keepdims=True), h)
            dvn_parts.append(_dot_tn(wms[h], dmx))
        dsgub_ref[...] += dbias
        dvn = jnp.concatenate(dvn_parts, axis=1)
        dug = jnp.concatenate(dug_parts, axis=1)
        dlnp_ref[0:1, :] += jnp.sum(dvn * vhat, axis=0, keepdims=True)
        dlnp_ref[1:2, :] += jnp.sum(dvn, axis=0, keepdims=True)
        dvhat = dvn * lnp_ref[0:1, :]
        dvg = rstd * (dvhat - jnp.mean(dvhat, axis=-1, keepdims=True)
                      - vhat * jnp.mean(dvhat * vhat, axis=-1, keepdims=True))
        dp_ref[:, 0:512] = (dug * _gelu_grad(p_ref[:, 0:512])).astype(BF16)
        dp_ref[:, 512:1024] = (dvg * _gelu_grad(p_ref[:, 512:1024])).astype(BF16)

        conv = conv_ref[...]
        act = _silu(conv)
        gt = p_ref[:, GATE0:NW]
        beta, gl, gc, gct, neg_a = _gates(gt, gv_ref)
        gng = gv_ref[2:3, :]
        dbeta_t = jnp.zeros((BLK, HD), F32)
        dgc_t = jnp.zeros((BLK, HD), F32)
        dgng = jnp.zeros((1, HD), F32)
        for h in range(HEADS):
            sl = slice(h * HD, (h + 1) * HD)
            hc = _head_chunk(act, beta, gc, gct, h)
            b, gam, e, dmat, kk = hc["b"], hc["gam"], hc["e"], hc["dmat"], hc["kk"]
            qn, khat, vh = hc["qn"], hc["khat"], hc["vh"]
            gamlast = jnp.exp(hc["glast"])
            s, tm, u, w, o = s_ref[h], t_ref[h], u_ref[:, sl], w_ref[:, sl], o_ref[:, sl]
            ds_next = ds_scr[h]
            z = p_ref[:, 2560 + h * HD:2560 + (h + 1) * HD]
            dy = dmix_ref[:, 512 + h * HD:512 + (h + 1) * HD].astype(F32)
            ro = lax.rsqrt(jnp.mean(o * o, axis=-1, keepdims=True) + RMS_EPS)
            ohat = o * ro
            dp_ref[:, 2560 + h * HD:2560 + (h + 1) * HD] = (dy * ohat * gng * _silu_grad(z)).astype(BF16)
            don = dy * _silu(z)
            dgng = dgng + jnp.sum(don * ohat, axis=0, keepdims=True)
            dohat = don * gng
            do = ro * (dohat - ohat * jnp.mean(dohat * ohat, axis=-1, keepdims=True))
            qk_raw = _dot_nt(qn, khat)
            qkm = qk_raw * dmat
            qd, kd = qn * gam, khat * e
            wn = u - _dot(w, s)
            dwn = _dot_tn(qkm, do) + _dot(kd, ds_next)
            dqd = _dot_nt(do, s)
            dqkm = jnp.where(incl, _dot_nt(do, wn), 0.0)
            ds_scr[h] = _dot_tn(qd, do) + gamlast * ds_next - _dot_tn(w, dwn)
            dgamlast = jnp.sum(jnp.sum(ds_next * s, axis=1, keepdims=True), axis=0, keepdims=True)
            dkd = _dot_nt(wn, ds_next)
            dw = -_dot_nt(dwn, s)
            db1 = _dot_tn(tm, dwn)
            db2 = _dot_tn(tm, dw)
            dm = jnp.where(strict, -(_dot_nt(db1, u) + _dot_nt(db2, w)), 0.0)
            dbeta = (jnp.sum(dm * kk * dmat, axis=1, keepdims=True) + jnp.sum(db1 * vh, axis=1, keepdims=True)
                     + gam * jnp.sum(db2 * khat, axis=1, keepdims=True))
            dkkm = dm * b * dmat
            ddm = dm * b * kk + dqkm * qk_raw
            dgam = b * jnp.sum(db2 * khat, axis=1, keepdims=True) + jnp.sum(dqd * qn, axis=1, keepdims=True)
            g_qk = dqkm * dmat
            dqn = _dot(g_qk, khat) + dqd * gam
            dkhat = ((b * gam) * db2 + _dot_tn(g_qk, qn) + _dot(dkkm, khat) + _dot_tn(dkkm, khat) + dkd * e)
            dvh = b * db1
            rkd = jnp.sum(dkd * kd, axis=1, keepdims=True)
            emat = ddm * dmat
            dgc = (dgam * gam - rkd + jnp.sum(emat, axis=1, keepdims=True)
                   - jnp.sum(emat.T, axis=1, keepdims=True))
            last = _iota2((BLK, 1), 0) == BLK - 1
            dgc = dgc + jnp.where(last, jnp.sum(rkd, axis=0, keepdims=True) + dgamlast * gamlast, 0.0)
            dgc_t = dgc_t + _put_col(dgc, 4 + h)
            dbeta_t = dbeta_t + _put_col(dbeta, h)
            dqhat = dqn * QK_SCALE
            dq = hc["rq"] * (dqhat - hc["qhat"] * jnp.sum(dqhat * hc["qhat"], axis=-1, keepdims=True))
            dk = hc["rk"] * (dkhat - khat * jnp.sum(dkhat * khat, axis=-1, keepdims=True))
            dcbuf[0:BLK, h * HD:(h + 1) * HD] = dq
            dcbuf[0:BLK, 512 + h * HD:512 + (h + 1) * HD] = dk
            dcbuf[0:BLK, 1024 + h * HD:1024 + (h + 1) * HD] = dvh
        dgv_ref[2:3, :] += dgng
        dgl = _dotf_tn(jnp.where(incl, 1.0, 0.0).astype(F32), dgc_t)
        sig_a = _sigmoid(gt + gv_ref[1:2, :])
        d_araw = dgl * neg_a * sig_a
        dgv_ref[0:1, :] += jnp.sum(dgl * gl, axis=0, keepdims=True)
        dgv_ref[1:2, :] += jnp.sum(d_araw, axis=0, keepdims=True)
        dp_ref[:, GATE0:NW] = (dbeta_t * beta * (1.0 - beta) + d_araw).astype(BF16)
        dcbuf[0:BLK, :] = dcbuf[0:BLK, :] * _silu_grad(conv)
        xcur = p_ref[:, 1024:2560]
        dqkv = jnp.zeros((BLK, 1536), F32)
        for j in range(4):
            shifted = dcbuf[3 - j:3 - j + BLK, :]
            dqkv = dqkv + cw_ref[j:j + 1, :] * shifted
            dcw_ref[j:j + 1, :] += jnp.sum(shifted * xcur, axis=0, keepdims=True)
        dp_ref[:, 1024:2560] = dqkv.astype(BF16)
        dcbuf[BLK:BLK + 8, :] = dcbuf[0:8, :]

        if ne:
            @pl.when(pl.program_id(0) == nb - 1)
            def _():
                for cp in _exchange_copies(pairs, recvs, esems):
                    cp.wait()

    rev = lambda w: pl.BlockSpec((BLK, w), lambda i: (nb - 1 - i, 0))
    st = pl.BlockSpec((None, HEADS, HD, HD), lambda i: (nb - 1 - i, 0, 0, 0))
    fix = lambda *shape: pl.BlockSpec((None,) + shape, lambda i: (l,) + (0,) * len(shape))
    acc = lambda *shape: pl.BlockSpec(shape, lambda i: (0,) * len(shape))
    return pl.pallas_call(
        body, name=f"mixer_bwd_{l}", grid=(nb,),
        out_shape=[jax.ShapeDtypeStruct((t, NW), BF16), jax.ShapeDtypeStruct((8, 512), F32),
                   jax.ShapeDtypeStruct((HEADS, HD, HD), F32), jax.ShapeDtypeStruct((HD, HD), F32),
                   jax.ShapeDtypeStruct((8, 1536), F32), jax.ShapeDtypeStruct((8, HD), F32)]
        + _exchange_shapes(exchange),
        in_specs=[rev(NW), rev(D), st, st, rev(512), rev(512), rev(512),
                  rev(1536),
                  fix(8, 512), fix(HEADS, HD, HD), fix(HD, HD), fix(8, 1536), fix(8, HD)] + _hbm_specs(ne),
        out_specs=[rev(NW), acc(8, 512), acc(HEADS, HD, HD), acc(HD, HD), acc(8, 1536), acc(8, HD)]
        + _hbm_specs(ne),
        scratch_shapes=[pltpu.VMEM((HEADS, HD, HD), F32), pltpu.VMEM((BLK + 8, 1536), F32)]
        + (_exchange_sems(ne) if ne else []),
        compiler_params=_cparams(("arbitrary",)),
    )(p, dmix, s_sv, t_sv, u_sv, w_sv, o_sv, conv_sv, lnp, sgu_w, sgu_bt, cw, gv, *exchange)


_SMALL = (("b_ada", 24), ("norm1_g", 8), ("norm2_g", 8), ("final_g", 8), ("sgu_ln_g", 8), ("sgu_ln_b", 8),
          ("sgu_w", 256), ("sgu_b", 8), ("conv_w", 24), ("a_log", 8), ("dt_bias", 8), ("gdn_norm_g", 8))
_SMALL_PAD = sum(n for _, n in _SMALL)
_DMOD_ROWS = 24


def _pack_rows(parts):
    rows = []
    for (name, n), a in zip(_SMALL, parts):
        flat = a.reshape(-1).astype(F32)
        rows.append(jnp.pad(flat, (0, n * D - flat.shape[0])).reshape(n, D))
    return jnp.concatenate(rows, axis=0)


def _unpack_rows(buf, shapes):
    out, r0 = {}, 0
    for name, n in _SMALL:
        size = math.prod(shapes[name])
        out[name] = buf[r0:r0 + n].reshape(-1)[:size].reshape(shapes[name])
        r0 += n
    return out


_COMBINED_ROWS = 2 * _DMOD_ROWS + _SMALL_PAD
_GATHER_ROWS = -(-_COMBINED_ROWS // 16) * 16


def _pair_combine(own, sib, place):
    rows = own.shape[0]

    def body(p_ref, a_ref, b_ref, o_ref):
        first = lax.axis_index("c") == 0
        a, b = a_ref[0:_DMOD_ROWS, :], b_ref[0:_DMOD_ROWS, :]
        o_ref[0:_DMOD_ROWS, :] = jnp.where(first, a, b)
        o_ref[_DMOD_ROWS:2 * _DMOD_ROWS, :] = jnp.where(first, b, a)
        o_ref[2 * _DMOD_ROWS:_COMBINED_ROWS, :] = a_ref[_DMOD_ROWS:, :] + b_ref[_DMOD_ROWS:, :]
        o_ref[_COMBINED_ROWS:, :] = jnp.zeros((_GATHER_ROWS - _COMBINED_ROWS, D), F32)

    return pl.pallas_call(
        body, name="small_pair_combine", out_shape=jax.ShapeDtypeStruct((N_CHIPS, _GATHER_ROWS, D), F32),
        grid_spec=pltpu.PrefetchScalarGridSpec(
            num_scalar_prefetch=1, grid=(1,),
            in_specs=[pl.BlockSpec((rows, D), lambda i, pr: (0, 0)), pl.BlockSpec((rows, D), lambda i, pr: (0, 0))],
            out_specs=pl.BlockSpec((None, _GATHER_ROWS, D), lambda i, pr: (pr[0], 0, 0))),
        compiler_params=_cparams(("arbitrary",)),
    )(place, own, sib)


def _small_finalize(gathered, w, m, v):
    def body(g_ref, w_ref, m_ref, v_ref, go_ref, d_ref, nm_ref, nv_ref):
        lo, hi = 2 * _DMOD_ROWS, _COMBINED_ROWS
        sm = g_ref[0, lo:hi, :] + g_ref[1, lo:hi, :]
        sm = sm + g_ref[2, lo:hi, :]
        sm = sm + g_ref[3, lo:hi, :]
        bsum = jnp.zeros((_DMOD_ROWS, D), F32)
        for j in range(N_CHIPS):
            bsum = bsum + g_ref[j, 0:_DMOD_ROWS, :]
            bsum = bsum + g_ref[j, _DMOD_ROWS:2 * _DMOD_ROWS, :]
        go_ref[0:_DMOD_ROWS, :] = bsum
        go_ref[_DMOD_ROWS:, :] = sm[_DMOD_ROWS:, :]
        d_ref[...], nm_ref[...], nv_ref[...] = _adam_math(w_ref[...], go_ref[...], m_ref[...], v_ref[...])

    return pl.pallas_call(
        body, name="small_finalize", out_shape=[jax.ShapeDtypeStruct(w.shape, F32)] * 4,
        compiler_params=pltpu.CompilerParams(vmem_limit_bytes=VMEM_LIMIT),
    )(gathered, w, m, v)


def kernel(x, c, w_ada, b_ada, norm1_g, w_in, sgu_ln_g, sgu_ln_b, sgu_w, sgu_b, conv_w, a_log, dt_bias, gdn_norm_g, w_out, norm2_g, w_ff1, w_ff2, final_g, loss_target, m_w_ada, m_b_ada, m_norm1_g, m_w_in, m_sgu_ln_g, m_sgu_ln_b, m_sgu_w, m_sgu_b, m_conv_w, m_a_log, m_dt_bias, m_gdn_norm_g, m_w_out, m_norm2_g, m_w_ff1, m_w_ff2, m_final_g, v_w_ada, v_b_ada, v_norm1_g, v_w_in, v_sgu_ln_g, v_sgu_ln_b, v_sgu_w, v_sgu_b, v_conv_w, v_a_log, v_dt_bias, v_gdn_norm_g, v_w_out, v_norm2_g, v_w_ff1, v_w_ff2, v_final_g):
    xi, yi, ci = lax.axis_index("x"), lax.axis_index("y"), lax.axis_index("c")
    chip = 2 * xi + yi
    dev = 2 * chip + ci
    t = x.shape[1]
    x0 = x.reshape(t, D)
    target = loss_target.reshape(t, D)

    c_sib = _pair_exchange(c, "c_pair")
    c_pair = jnp.where(ci == 0, jnp.concatenate([c, c_sib], 0), jnp.concatenate([c_sib, c], 0))
    c_all = _chip_allgather(c_pair, "c_chips").reshape(8, D)
    ada_cols = w_ada.shape[2]
    b_cols = lax.dynamic_slice_in_dim(b_ada, chip * ada_cols, ada_cols, axis=1)
    mod_part = _ada_forward(c_all, w_ada, b_cols)
    conv_cols = conv_w.shape[2]
    packed = jnp.concatenate([mod_part.reshape(DEPTH * 8, ada_cols), conv_w.reshape(DEPTH, 4 * conv_cols)], axis=0)
    packed = _chip_allgather(packed, "mod_chips")
    mod_all = packed[:, :DEPTH * 8].reshape(N_CHIPS, DEPTH, 8, ada_cols)
    mod_mine = lax.dynamic_index_in_dim(mod_all, dev, axis=2, keepdims=False)
    mod = mod_mine.transpose(1, 0, 2).reshape(DEPTH, 6, D)
    modv = jnp.concatenate([mod, norm1_g[:, None, :], norm2_g[:, None, :]], axis=1)
    conv_full = packed[:, DEPTH * 8:].reshape(N_CHIPS, DEPTH, 4, conv_cols).transpose(1, 2, 0, 3).reshape(DEPTH, 4, 1536)

    place = jnp.stack([chip, ci]).astype(jnp.int32)
    wbufs = [[_cast_into_slot(w, l, place) for w in (w_in, w_out, w_ff1, w_ff2)] for l in range(DEPTH)]
    wbufs[0][:1] = _weights_allgather(wbufs[0][:1], 0)

    def full_w_in(g):
        return jnp.pad(g.transpose(1, 0, 2).reshape(D, IN_W), ((0, 0), (0, NW - IN_W)))

    lnp = jnp.pad(jnp.stack([sgu_ln_g, sgu_ln_b], axis=1), ((0, 0), (0, 6), (0, 0)))
    sgu_bt = jnp.pad(sgu_b.transpose(0, 2, 1), ((0, 0), (0, 0), (0, HD - HEADS)))
    cw = jnp.pad(conv_full, ((0, 0), (0, 4), (0, 0)))
    lane_pad = lambda a: jnp.pad(a, ((0, 0), (4, HD - 8)))
    gv = jnp.pad(jnp.stack([lane_pad(a_log), lane_pad(dt_bias), gdn_norm_g], axis=1), ((0, 0), (0, 5), (0, 0)))

    acts = []
    xl = x0
    for l in range(DEPTH):
        win = full_w_in(wbufs[l][0])
        p, h1 = _fwd_in(xl, modv, win, l)
        nxt = wbufs[l][1:] + (wbufs[l + 1][:1] if l + 1 < DEPTH else [])
        mix, *rest = _mixer_forward(p, lnp, sgu_w, sgu_bt, cw, gv, l, gather=nxt)
        saved = rest[:6]
        wbufs[l][1:] = rest[6:9]
        if l + 1 < DEPTH:
            wbufs[l + 1][:1] = rest[9:]
        g_in, g_out, g_ff1, g_ff2 = wbufs[l]
        x1, r, h2 = _fwd_out_ff1(xl, mix, modv, g_out, g_ff1, l)
        x2 = _fwd_ff2(x1, r, modv, g_ff2, l)
        acts.append((xl, p, mix, saved, x1, r, h1, h2, win))
        xl = x2

    dx, head_stats = _loss_head(xl, target, final_g)
    loss = lax.psum(jnp.sum(head_stats[1, 0:1]), ("x", "y", "c"))
    d_final_g = head_stats[0]
    names = ("in", "out", "ff1", "ff2")
    grads_buf = [None] * len(names)

    def pair_sums(partials, from_sib, kinds, lay):
        return [(lay, n, _pair_sum(g, ga, place, f"pair_sum_{n}_{lay}")) for g, ga, n in zip(partials, from_sib, kinds)]

    def reduce_into_buffers(items, recv):
        for (lay, n, pair), rc in zip(items, recv):
            i = names.index(n)
            grads_buf[i] = _chip_sum(pair, rc, grads_buf[i], lay, place, f"chip_sum_{n}_{lay}")

    pending = []

    dmod, small = [None] * DEPTH, [None] * DEPTH
    for l in reversed(range(DEPTH)):
        xl, p, mix, saved, x1, r, h1, h2, win = acts[l]
        g_in, g_out, g_ff1, g_ff2 = wbufs[l]
        df = _bwd_ff2(dx, r, modv, g_ff2, l)
        gw_ff2, dg2 = _grad_weight(r, dx, modv, l, "ff2", g_ff2)
        gw_ff1, _ = _grad_weight(h2, df, modv, l, "ff1")
        dx1, st2, dmix, *sib_ff = _bwd_norm(df, g_ff1, x1, dx, modv, l, "ff1", send=[gw_ff1, gw_ff2], w_out=g_out)
        gw_out, dg1 = _grad_weight(mix, dx1, modv, l, "out", g_out)
        gw_out = gw_out.reshape(N_CHIPS, D // N_CHIPS, D)
        sib_out = _grads_pair_send([gw_out], f"out_{l}")
        pending = pending + pair_sums([gw_out, gw_ff1, gw_ff2], list(sib_out) + sib_ff, names[1:], l)
        dp, dlnp, dsguw, dsgub, dcw, dgv, *recv = _mixer_backward(p, dmix, saved, lnp, sgu_w, sgu_bt, cw, gv, l,
                                                                  exchange=[item[2] for item in pending])
        reduce_into_buffers(pending, recv)
        gw_in, _ = _grad_weight(h1, dp, modv, l, "in")
        gw_in_c = gw_in[:, :IN_W].reshape(D, N_CHIPS, IN_W // N_CHIPS).transpose(1, 0, 2)
        if l > 0:
            dx, st1, sib_in = _bwd_norm(dp, win, xl, dx1, modv, l, "in", send=[gw_in_c])
            pending = pair_sums([gw_in_c], [sib_in], names[:1], l)
        else:
            pending = pair_sums([gw_in_c], _grads_pair_send([gw_in_c], "in_0"), names[:1], l)
            dx, st1, *rest = _bwd_norm(dp, win, xl, dx1, modv, l, "in", share=grads_buf[1:],
                                       exchange=[item[2] for item in pending])
            grads_buf[1:] = rest[:3]
            reduce_into_buffers(pending, rest[3:])
        dmod[l] = jnp.stack([st1[0], st1[1], dg1[0], st2[0], st2[1], dg2[0]], axis=0)
        small[l] = dict(norm1_g=st1[2], norm2_g=st2[2], sgu_ln_g=dlnp[0], sgu_ln_b=dlnp[1], sgu_w=dsguw,
                        sgu_b=dsgub[:, :HEADS].T, conv_w=dcw[:4], a_log=dgv[0, 4:8], dt_bias=dgv[1, 4:8],
                        gdn_norm_g=dgv[2])
    grad_x = dx.reshape(1, t, D)

    stack = lambda k: jnp.stack([small[l][k] for l in range(DEPTH)], axis=0)
    small_grads = [jnp.zeros((DEPTH, 6 * D), F32), stack("norm1_g"), stack("norm2_g"), d_final_g, stack("sgu_ln_g"),
                   stack("sgu_ln_b"), stack("sgu_w"), stack("sgu_b"), stack("conv_w"), stack("a_log"),
                   stack("dt_bias"), stack("gdn_norm_g")]
    own = jnp.concatenate([jnp.stack(dmod, axis=0).reshape(_DMOD_ROWS, D), _pack_rows(small_grads)], axis=0)
    sib = _pair_exchange(own, "small_pair")
    gathered, = _weights_allgather([_pair_combine(own, sib, place)], "small")
    small_shapes = dict(b_ada=b_ada.shape, norm1_g=norm1_g.shape, norm2_g=norm2_g.shape, final_g=final_g.shape,
                        sgu_ln_g=sgu_ln_g.shape, sgu_ln_b=sgu_ln_b.shape, sgu_w=sgu_w.shape, sgu_b=sgu_b.shape,
                        conv_w=(DEPTH, 4, 1536), a_log=a_log.shape, dt_bias=dt_bias.shape,
                        gdn_norm_g=gdn_norm_g.shape)

    def full_conv(a):
        return lax.dynamic_update_slice_in_dim(jnp.zeros((DEPTH, 4, 1536), F32), a, chip * conv_cols, axis=2)

    def pack_state(b_, n1, n2, fg, lg, lb, sw, sb, cv, al, db, gn):
        return _pack_rows([b_, n1, n2, fg, lg, lb, sw, sb, full_conv(cv), al, db, gn])

    w_small = pack_state(b_ada, norm1_g, norm2_g, final_g, sgu_ln_g, sgu_ln_b, sgu_w, sgu_b, conv_w, a_log, dt_bias,
                         gdn_norm_g)
    m_small = pack_state(m_b_ada, m_norm1_g, m_norm2_g, m_final_g, m_sgu_ln_g, m_sgu_ln_b, m_sgu_w, m_sgu_b, m_conv_w,
                         m_a_log, m_dt_bias, m_gdn_norm_g)
    v_small = pack_state(v_b_ada, v_norm1_g, v_norm2_g, v_final_g, v_sgu_ln_g, v_sgu_ln_b, v_sgu_w, v_sgu_b, v_conv_w,
                         v_a_log, v_dt_bias, v_gdn_norm_g)
    small_out = _small_finalize(gathered, w_small, m_small, v_small)
    sg, sd, sm, sv = [_unpack_rows(a, small_shapes) for a in small_out]
    for dct in (sg, sd, sm, sv):
        dct["conv_w"] = lax.dynamic_slice_in_dim(dct["conv_w"], chip * conv_cols, conv_cols, axis=2)

    dmod_all = gathered[:, :2 * _DMOD_ROWS].reshape(8, DEPTH, 6 * D)
    dmod_cols = lax.dynamic_slice_in_dim(dmod_all, chip * ada_cols, ada_cols, axis=2).transpose(1, 0, 2)
    g_ada, d_ada, nm_ada, nv_ada = _ada_backward_adamw(c_all, dmod_cols, w_ada, m_w_ada, v_w_ada)

    grads_buf[:1] = _grads_pair_share(grads_buf[:1], "grads_pair_share_in")
    big = {}
    for n, g, (w, m, v) in zip(names, grads_buf, ((w_in, m_w_in, v_w_in), (w_out, m_w_out, v_w_out),
                                                  (w_ff1, m_w_ff1, v_w_ff1), (w_ff2, m_w_ff2, v_w_ff2))):
        big[n] = tuple(_adamw(w, g, m, v, f"adamw_{n}"))

    def outs(k):
        s = (sg, sd, sm, sv)[k]
        return [(g_ada, d_ada, nm_ada, nv_ada)[k], s["b_ada"], s["norm1_g"], big["in"][k], s["sgu_ln_g"],
                s["sgu_ln_b"], s["sgu_w"], s["sgu_b"], s["conv_w"], s["a_log"], s["dt_bias"], s["gdn_norm_g"],
                big["out"][k], s["norm2_g"], big["ff1"][k], big["ff2"][k], s["final_g"]]

    return (loss, grad_x, *outs(0), *outs(1), *outs(2), *outs(3))
```

```python
import functools
import math

import jax
import jax.numpy as jnp
from jax import lax
from jax.experimental import pallas as pl
from jax.experimental.pallas import tpu as pltpu

F32 = jnp.float32
BF16 = jnp.bfloat16

DEPTH = 4
D = 1024
HEADS = 4
HD = 128
BLK = 128
IN_W = 3080
NW = 3200
GATE0 = 3072
DFF = 4096
N_CHIPS = 4
RMS_EPS = 1e-6
LN_EPS = 1e-5
QK_SCALE = HD ** -0.5
LR, B1, B2, ADAM_EPS, WD, STEP = 0.001, 0.9, 0.999, 1e-08, 0.01, 10
VMEM_LIMIT = 56 * 1024 * 1024
MESH = pl.DeviceIdType.MESH
HOPS = ((1, 0), (0, 1), (1, 1))
HI = lax.Precision.HIGHEST


def _dot(a, b):
    return jnp.dot(a.astype(BF16), b.astype(BF16), preferred_element_type=F32)


def _dot_nt(a, b):
    return lax.dot_general(a.astype(BF16), b.astype(BF16), (((1,), (1,)), ((), ())), preferred_element_type=F32)


def _dot_tn(a, b):
    return lax.dot_general(a.astype(BF16), b.astype(BF16), (((0,), (0,)), ((), ())), preferred_element_type=F32)


def _bdot(a, b, dims=((2,), (1,))):
    return lax.dot_general(a.astype(BF16), b.astype(BF16), (dims, ((0,), (0,))), preferred_element_type=F32)


def _dotf(a, b):
    return jnp.dot(a, b, precision=HI, preferred_element_type=F32)


def _split(a):
    hi = a.astype(BF16)
    return hi, (a - hi.astype(F32)).astype(BF16)


def _dg3(a, b, dims, batch=((), ())):
    ah, al = _split(a)
    bh, bl = _split(b)
    f = lambda x, y: lax.dot_general(x, y, (dims, batch), preferred_element_type=F32)
    return f(ah, bh) + (f(ah, bl) + f(al, bh))


def _bmm3(a, b):
    return _dg3(a, b, ((2,), (1,)), ((0,), (0,)))


def _d3(a, b):
    return _dg3(a, b, ((1,), (0,)))


def _d3_nt(a, b):
    return _dg3(a, b, ((1,), (1,)))


def _d3_tn(a, b):
    return _dg3(a, b, ((0,), (0,)))


def _dotf_tn(a, b):
    return lax.dot_general(a, b, (((0,), (0,)), ((), ())), precision=HI, preferred_element_type=F32)


def _sigmoid(x):
    return 1.0 / (1.0 + jnp.exp(-x))


def _softplus(x):
    return jnp.maximum(x, 0.0) + jnp.log(1.0 + jnp.exp(-jnp.abs(x)))


_G0 = math.sqrt(2.0 / math.pi)
_G1 = 0.044715


def _gelu(x):
    t = jnp.tanh(_G0 * (x + _G1 * x * x * x))
    return 0.5 * x * (1.0 + t)


def _gelu_grad(x):
    t = jnp.tanh(_G0 * (x + _G1 * x * x * x))
    return 0.5 * (1.0 + t) + 0.5 * x * (1.0 - t * t) * (_G0 * (1.0 + 3.0 * _G1 * x * x))


def _silu(x):
    return x * _sigmoid(x)


def _silu_grad(x):
    s = _sigmoid(x)
    return s * (1.0 + x * (1.0 - s))


def _rms_stats(x):
    rstd = lax.rsqrt(jnp.mean(x * x, axis=-1, keepdims=True) + RMS_EPS)
    return x * rstd, rstd


def _norm_mod(x, ng, sc, sh):
    xh, _ = _rms_stats(x)
    return xh * (ng * (1.0 + sc)) + sh


def _norm_mod_bwd(dh, x, ng, sc):
    xh, rstd = _rms_stats(x)
    dsh = jnp.sum(dh, axis=0, keepdims=True)
    dsc = jnp.sum(dh * xh, axis=0, keepdims=True) * ng
    dng = jnp.sum(dh * xh, axis=0, keepdims=True) * (1.0 + sc)
    dxh = dh * (ng * (1.0 + sc))
    dx = rstd * (dxh - xh * jnp.mean(dxh * xh, axis=-1, keepdims=True))
    return dx, dsh, dsc, dng


def _iota2(shape, axis):
    return lax.broadcasted_iota(jnp.int32, shape, axis)


def _col(tile, idx):
    return jnp.sum(jnp.where(_iota2(tile.shape, 1) == idx, tile, 0.0), axis=1, keepdims=True)


def _col_slice(tile, idx):
    return tile[:, idx:idx + 1]


def _row(tile, idx):
    return jnp.sum(jnp.where(_iota2(tile.shape, 0) == idx, tile, 0.0), axis=0, keepdims=True)


def _put_col(col, idx, width=HD):
    shape = (col.shape[0], width)
    return jnp.where(_iota2(shape, 1) == idx, jnp.broadcast_to(col, shape), 0.0)


def _tri_inverse(m):
    rows, cols = _iota2(m.shape, m.ndim - 2), _iota2(m.shape, m.ndim - 1)
    mm = _bmm3 if m.ndim == 3 else _d3
    eye = jnp.where(rows == cols, 1.0, 0.0).astype(F32)
    n = jnp.where((rows >> 3) == (cols >> 3), -m, 0.0)
    p = eye + n
    n2 = mm(n, n)
    p = p + mm(n2, p)
    n4 = mm(n2, n2)
    p = p + mm(n4, p)
    for shift in (3, 4, 5, 6):
        same_pair = (rows >> (shift + 1)) == (cols >> (shift + 1))
        below = jnp.logical_and(((rows >> shift) & 1) == 1, ((cols >> shift) & 1) == 0)
        off = jnp.where(jnp.logical_and(same_pair, below), m, 0.0)
        p = p - mm(p, mm(off, p))
    return p


def _cparams(sem=None):
    return pltpu.CompilerParams(dimension_semantics=sem, vmem_limit_bytes=VMEM_LIMIT)


def _my_place():
    return lax.axis_index("x"), lax.axis_index("y"), lax.axis_index("c")


def _hop(xi, yi, hop):
    dx, dy = hop
    return (1 - xi if dx else xi), (1 - yi if dy else yi)


def _pair_exchange(x, name):
    def body(x_ref, o_ref, ssem, rsem):
        xi, yi, ci = _my_place()
        cp = pltpu.make_async_remote_copy(x_ref, o_ref, ssem, rsem, device_id=(xi, yi, 1 - ci), device_id_type=MESH)
        cp.start()
        cp.wait()

    return pl.pallas_call(
        body, name=name, out_shape=jax.ShapeDtypeStruct(x.shape, x.dtype),
        in_specs=[pl.BlockSpec(memory_space=pltpu.VMEM)], out_specs=pl.BlockSpec(memory_space=pltpu.VMEM),
        scratch_shapes=[pltpu.SemaphoreType.DMA, pltpu.SemaphoreType.DMA],
        compiler_params=pltpu.CompilerParams(vmem_limit_bytes=VMEM_LIMIT),
    )(x)


def _allgather_start(x_ref, o_ref, ssems, rsems, lsem):
    xi, yi, ci = _my_place()
    me = 2 * xi + yi
    pltpu.make_async_copy(x_ref, o_ref.at[me], lsem).start()
    for k, hop in enumerate(HOPS):
        tx, ty = _hop(xi, yi, hop)
        pltpu.make_async_remote_copy(x_ref, o_ref.at[me], ssems.at[k], rsems.at[k],
                                     device_id=(tx, ty, ci), device_id_type=MESH).start()


def _allgather_finish(x_ref, o_ref, ssems, rsems, lsem):
    xi, yi, ci = _my_place()
    me = 2 * xi + yi
    for k, hop in enumerate(HOPS):
        tx, ty = _hop(xi, yi, hop)
        cp = pltpu.make_async_remote_copy(x_ref, o_ref.at[2 * tx + ty], ssems.at[k], rsems.at[k],
                                          device_id=(tx, ty, ci), device_id_type=MESH)
        cp.wait_recv()
        cp.wait_send()
    pltpu.make_async_copy(x_ref, o_ref.at[me], lsem).wait()


_ALLGATHER_SEMS = [pltpu.SemaphoreType.DMA((3,)), pltpu.SemaphoreType.DMA((3,)), pltpu.SemaphoreType.DMA]


def _chip_allgather(x, name):
    def body(x_ref, o_ref, ssems, rsems, lsem):
        _allgather_start(x_ref, o_ref, ssems, rsems, lsem)
        _allgather_finish(x_ref, o_ref, ssems, rsems, lsem)

    return pl.pallas_call(
        body, name=name, out_shape=jax.ShapeDtypeStruct((N_CHIPS,) + x.shape, x.dtype),
        in_specs=[pl.BlockSpec(memory_space=pltpu.VMEM)], out_specs=pl.BlockSpec(memory_space=pltpu.VMEM),
        scratch_shapes=_ALLGATHER_SEMS, compiler_params=pltpu.CompilerParams(vmem_limit_bytes=VMEM_LIMIT),
    )(x)


def _hbm_specs(n):
    return [pl.BlockSpec(memory_space=pl.ANY)] * n


def _cast_into_slot(w, l, place):
    _, r, c = w.shape
    tr = _row_tile(r)

    def body(p_ref, w_ref, o_ref):
        o_ref[...] = w_ref[...].astype(BF16)

    return pl.pallas_call(
        body, name=f"cast_slot_{r}x{c}_{l}", out_shape=jax.ShapeDtypeStruct((N_CHIPS, r, c), BF16),
        grid_spec=pltpu.PrefetchScalarGridSpec(
            num_scalar_prefetch=1, grid=(r // tr,),
            in_specs=[pl.BlockSpec((None, tr, c), lambda k, pr: (l, k, 0))],
            out_specs=pl.BlockSpec((None, tr, c), lambda k, pr: (pr[0], k, 0))),
        compiler_params=_cparams(("parallel",)),
    )(place, w)


def _halves(ref, ci):
    half = ref.shape[-2] // 2
    return pl.ds(ci * half, half), pl.ds((1 - ci) * half, half)


def _gather_copies(bufs, sems):
    s_ici, r_ici, s_d2d, r_d2d = sems
    xi, yi, ci = _my_place()
    me = 2 * xi + yi
    ici_send, ici_recv, d2d_send, d2d_recv = [], [], [], []
    for i, buf in enumerate(bufs):
        mine, sibs = _halves(buf, ci)
        for k, hop in enumerate(HOPS):
            tx, ty = _hop(xi, yi, hop)
            src = 2 * tx + ty
            ici_send.append(pltpu.make_async_remote_copy(buf.at[me, mine], buf.at[me, mine], s_ici.at[i, k],
                                                         r_ici.at[i, k], device_id=(tx, ty, ci), device_id_type=MESH))
            ici_recv.append(pltpu.make_async_remote_copy(buf.at[src, mine], buf.at[src, mine], s_ici.at[i, k],
                                                         r_ici.at[i, k], device_id=(tx, ty, ci), device_id_type=MESH))
            d2d_send.append(pltpu.make_async_remote_copy(buf.at[src, mine], buf.at[src, mine], s_d2d.at[i, k],
                                                         r_d2d.at[i, k], device_id=(xi, yi, 1 - ci), device_id_type=MESH))
            d2d_recv.append(pltpu.make_async_remote_copy(buf.at[src, sibs], buf.at[src, sibs], s_d2d.at[i, k],
                                                         r_d2d.at[i, k], device_id=(xi, yi, 1 - ci), device_id_type=MESH))
    return ici_send, ici_recv, d2d_send, d2d_recv


def _gather_start(bufs, sems):
    for cp in _gather_copies(bufs, sems)[0]:
        cp.start()


def _gather_forward(bufs, sems):
    _, ici_recv, d2d_send, _ = _gather_copies(bufs, sems)
    for arrived, forward in zip(ici_recv, d2d_send):
        arrived.wait_recv()
        forward.start()


def _gather_finish(bufs, sems):
    ici_send, _, d2d_send, d2d_recv = _gather_copies(bufs, sems)
    for cp in d2d_recv:
        cp.wait_recv()
    for cp in ici_send + d2d_send:
        cp.wait_send()


def _gather_sems(n):
    return [pltpu.SemaphoreType.DMA((n, 3))] * 4


def _weights_allgather(bufs, l):
    n = len(bufs)

    def body(*refs):
        outs, sems = refs[n:2 * n], refs[2 * n:]
        _gather_start(outs, sems)
        _gather_forward(outs, sems)
        _gather_finish(outs, sems)

    return pl.pallas_call(
        body, name=f"weights_allgather_{l}",
        out_shape=[jax.ShapeDtypeStruct(b.shape, b.dtype) for b in bufs],
        in_specs=_hbm_specs(n), out_specs=_hbm_specs(n), input_output_aliases={i: i for i in range(n)},
        scratch_shapes=_gather_sems(n),
    )(*bufs)


def _pair_send_copies(gs, outs, sems):
    ssem, rsem = sems
    xi, yi, ci = _my_place()
    every = pl.ds(0, N_CHIPS)
    return [pltpu.make_async_remote_copy(g.at[every, _halves(g, ci)[1]], o, ssem.at[i], rsem.at[i],
                                         device_id=(xi, yi, 1 - ci), device_id_type=MESH)
            for i, (g, o) in enumerate(zip(gs, outs))]


def _pair_send_shapes(gs):
    return [jax.ShapeDtypeStruct((N_CHIPS, g.shape[1] // 2, g.shape[2]), g.dtype) for g in gs]


def _pair_send_sems(n):
    return [pltpu.SemaphoreType.DMA((n,)), pltpu.SemaphoreType.DMA((n,))]


def _grads_pair_send(gs, l):
    n = len(gs)

    def body(*refs):
        cps = _pair_send_copies(refs[:n], refs[n:2 * n], refs[2 * n:])
        for cp in cps:
            cp.start()
        for cp in cps:
            cp.wait()

    return pl.pallas_call(
        body, name=f"grads_pair_send_{l}", out_shape=_pair_send_shapes(gs),
        in_specs=_hbm_specs(n), out_specs=_hbm_specs(n), scratch_shapes=_pair_send_sems(n),
    )(*gs)


def _exchange_copies(ps, recvs, sems):
    ssems, rsems = sems
    xi, yi, ci = _my_place()
    cps = []
    for i, (p, rc) in enumerate(zip(ps, recvs)):
        for k, hop in enumerate(HOPS):
            tx, ty = _hop(xi, yi, hop)
            cps.append(pltpu.make_async_remote_copy(p.at[2 * tx + ty], rc.at[k], ssems.at[i, k], rsems.at[i, k],
                                                    device_id=(tx, ty, ci), device_id_type=MESH))
    return cps


def _exchange_sems(n):
    return [pltpu.SemaphoreType.DMA((n, 3))] * 2


def _exchange_shapes(ps):
    return [jax.ShapeDtypeStruct((3,) + p.shape[1:], p.dtype) for p in ps]


def _pair_share_copies(bufs, sems):
    ssem, rsem = sems
    xi, yi, ci = _my_place()
    every = pl.ds(0, DEPTH)
    sends, arrivals = [], []
    for i, buf in enumerate(bufs):
        mine, sibs = _halves(buf, ci)
        sends.append(pltpu.make_async_remote_copy(buf.at[every, mine], buf.at[every, mine], ssem.at[i], rsem.at[i],
                                                  device_id=(xi, yi, 1 - ci), device_id_type=MESH))
        arrivals.append(pltpu.make_async_remote_copy(buf.at[every, sibs], buf.at[every, sibs], ssem.at[i], rsem.at[i],
                                                     device_id=(xi, yi, 1 - ci), device_id_type=MESH))
    return sends, arrivals


def _pair_share_start(bufs, sems):
    for cp in _pair_share_copies(bufs, sems)[0]:
        cp.start()


def _pair_share_finish(bufs, sems):
    sends, arrivals = _pair_share_copies(bufs, sems)
    for cp in arrivals:
        cp.wait_recv()
    for cp in sends:
        cp.wait_send()


def _grads_pair_share(gs, name):
    n = len(gs)

    def body(*refs):
        _pair_share_start(refs[n:2 * n], refs[2 * n:])
        _pair_share_finish(refs[n:2 * n], refs[2 * n:])

    return pl.pallas_call(
        body, name=name,
        out_shape=[jax.ShapeDtypeStruct(g.shape, g.dtype) for g in gs],
        in_specs=_hbm_specs(n), out_specs=_hbm_specs(n), input_output_aliases={i: i for i in range(n)},
        scratch_shapes=_pair_send_sems(n),
    )(*gs)


def _row_tile(r):
    return min(r, 512)


def _pair_sum(g, ga, place, name):
    _, r, c = g.shape
    tr = _row_tile(r // 2)
    nk = r // 2 // tr

    def body(p_ref, g_ref, ga_ref, o_ref):
        o_ref[...] = (g_ref[...].astype(F32) + ga_ref[...].astype(F32)).astype(o_ref.dtype)

    return pl.pallas_call(
        body, name=name, out_shape=jax.ShapeDtypeStruct(ga.shape, ga.dtype),
        grid_spec=pltpu.PrefetchScalarGridSpec(
            num_scalar_prefetch=1, grid=(N_CHIPS, nk),
            in_specs=[pl.BlockSpec((None, tr, c), lambda j, k, pr: (j, pr[1] * nk + k, 0)),
                      pl.BlockSpec((None, tr, c), lambda j, k, pr: (j, k, 0))],
            out_specs=pl.BlockSpec((None, tr, c), lambda j, k, pr: (j, k, 0))),
        compiler_params=_cparams(("parallel", "parallel")),
    )(place, g, ga)


def _chip_sum(pair, recv, buf, l, place, name):
    _, rh, c = pair.shape
    tr = _row_tile(rh)
    nk = rh // tr

    def body(p_ref, own_ref, r_ref, *rest):
        o_ref = rest[-1]
        acc = own_ref[...].astype(F32) + r_ref[0].astype(F32)
        acc = acc + r_ref[1].astype(F32)
        o_ref[...] = acc + r_ref[2].astype(F32)

    in_specs = [pl.BlockSpec((None, tr, c), lambda k, pr: (pr[0], k, 0)),
                pl.BlockSpec((3, tr, c), lambda k, pr: (0, k, 0))]
    args = [pair, recv]
    aliases = {}
    if buf is not None:
        in_specs.append(pl.BlockSpec(memory_space=pl.ANY))
        args.append(buf)
        aliases = {3: 0}
    return pl.pallas_call(
        body, name=name, out_shape=jax.ShapeDtypeStruct((DEPTH, 2 * rh, c), F32),
        grid_spec=pltpu.PrefetchScalarGridSpec(
            num_scalar_prefetch=1, grid=(nk,), in_specs=in_specs,
            out_specs=pl.BlockSpec((None, tr, c), lambda k, pr: (l, pr[1] * nk + k, 0))),
        input_output_aliases=aliases, compiler_params=_cparams(("parallel",)),
    )(place, *args)


def _adam_math(w, g, m, v):
    m = B1 * m + (1.0 - B1) * g
    v = B2 * v + (1.0 - B2) * (g * g)
    m_hat = m / (1.0 - B1 ** STEP)
    v_hat = v / (1.0 - B2 ** STEP)
    delta = -LR * (m_hat / (jnp.sqrt(v_hat) + ADAM_EPS) + WD * w)
    return delta, m, v


def _adamw(w, g, m, v, name):
    n_l, r, c = w.shape
    tr = _row_tile(r)

    def body(w_ref, g_ref, m_ref, v_ref, d_ref, nm_ref, nv_ref):
        d_ref[...], nm_ref[...], nv_ref[...] = _adam_math(w_ref[...], g_ref[...], m_ref[...], v_ref[...])

    spec = pl.BlockSpec((None, tr, c), lambda i, k: (i, k, 0))
    return pl.pallas_call(
        body, name=name, out_shape=[jax.ShapeDtypeStruct(w.shape, F32)] * 3, grid=(n_l, r // tr),
        in_specs=[spec] * 4, out_specs=[spec] * 3, compiler_params=_cparams(("parallel", "parallel")),
    )(w, g, m, v)


def _ada_forward(c_all, w_ada, b_cols):
    cols = w_ada.shape[2]
    tn = 512

    def body(c_ref, w_ref, b_ref, o_ref):
        o_ref[...] = _dotf(_silu(c_ref[...]), w_ref[...]) + b_ref[...]

    return pl.pallas_call(
        body, name="ada_forward", out_shape=jax.ShapeDtypeStruct((DEPTH, 8, cols), F32), grid=(DEPTH, cols // tn),
        in_specs=[pl.BlockSpec((8, D), lambda l, j: (0, 0)),
                  pl.BlockSpec((None, D, tn), lambda l, j: (l, 0, j)),
                  pl.BlockSpec((None, 1, tn), lambda l, j: (l, 0, j))],
        out_specs=pl.BlockSpec((None, 8, tn), lambda l, j: (l, 0, j)),
        compiler_params=_cparams(("parallel", "parallel")),
    )(c_all, w_ada, b_cols.reshape(DEPTH, 1, cols))


def _ada_backward_adamw(c_all, dmod_cols, w, m, v):
    cols = w.shape[2]
    tn = 512

    def body(c_ref, d_ref, w_ref, m_ref, v_ref, g_ref, dl_ref, nm_ref, nv_ref):
        g = _dotf_tn(_silu(c_ref[...]), d_ref[...])
        g_ref[...] = g
        dl_ref[...], nm_ref[...], nv_ref[...] = _adam_math(w_ref[...], g, m_ref[...], v_ref[...])

    wspec = pl.BlockSpec((None, D, tn), lambda l, j: (l, 0, j))
    return pl.pallas_call(
        body, name="ada_backward_adamw", out_shape=[jax.ShapeDtypeStruct(w.shape, F32)] * 4, grid=(DEPTH, cols // tn),
        in_specs=[pl.BlockSpec((8, D), lambda l, j: (0, 0)), pl.BlockSpec((None, 8, tn), lambda l, j: (l, 0, j)),
                  wspec, wspec, wspec],
        out_specs=[wspec] * 4, compiler_params=_cparams(("parallel", "parallel")),
    )(c_all, dmod_cols, w, m, v)


def _tok_tile(t):
    return min(t, 512)


def _wspec4(r, c, l):
    return pl.BlockSpec((N_CHIPS, r, c), lambda i: (0, 0, 0))


def _fwd_in(x, modv, w_in, l):
    t = x.shape[0]
    tm = _tok_tile(t)

    def body(x_ref, mod_ref, w_ref, o_ref, h_ref):
        h = _norm_mod(x_ref[...], mod_ref[6:7, :], mod_ref[1:2, :], mod_ref[0:1, :]).astype(BF16)
        h_ref[...] = h
        o_ref[...] = jnp.dot(h, w_ref[...], preferred_element_type=F32)

    return pl.pallas_call(
        body, name=f"fwd_in_{l}", grid=(t // tm,),
        out_shape=[jax.ShapeDtypeStruct((t, NW), F32), jax.ShapeDtypeStruct((t, D), BF16)],
        in_specs=[pl.BlockSpec((tm, D), lambda i: (i, 0)), pl.BlockSpec((None, 8, D), lambda i: (l, 0, 0)),
                  pl.BlockSpec((D, NW), lambda i: (0, 0))],
        out_specs=[pl.BlockSpec((tm, NW), lambda i: (i, 0)), pl.BlockSpec((tm, D), lambda i: (i, 0))],
        compiler_params=_cparams(("parallel",)),
    )(x, modv, w_in)


def _fwd_out_ff1(x, mix, modv, w_out, w_ff1, l):
    t = x.shape[0]
    tm = _tok_tile(t)

    def body(x_ref, mix_ref, mod_ref, wo_ref, w_ref, x1_ref, o_ref, h_ref):
        x1 = x_ref[...] + mod_ref[2:3, :] * jnp.dot(mix_ref[...], wo_ref[...].reshape(D, D), preferred_element_type=F32)
        x1_ref[...] = x1
        h = _norm_mod(x1, mod_ref[7:8, :], mod_ref[4:5, :], mod_ref[3:4, :]).astype(BF16)
        h_ref[...] = h
        for j in range(N_CHIPS):
            f = jnp.dot(h, w_ref[j], preferred_element_type=F32)
            o_ref[:, j * D:(j + 1) * D] = jnp.maximum(f, 0.0).astype(BF16)

    tok = pl.BlockSpec((tm, D), lambda i: (i, 0))
    return pl.pallas_call(
        body, name=f"fwd_out_ff1_{l}", grid=(t // tm,),
        out_shape=[jax.ShapeDtypeStruct((t, D), F32), jax.ShapeDtypeStruct((t, DFF), BF16),
                   jax.ShapeDtypeStruct((t, D), BF16)],
        in_specs=[tok, tok, pl.BlockSpec((None, 8, D), lambda i: (l, 0, 0)), _wspec4(D // N_CHIPS, D, l),
                  _wspec4(D, D, l)],
        out_specs=[tok, pl.BlockSpec((tm, DFF), lambda i: (i, 0)), tok],
        compiler_params=_cparams(("parallel",)),
    )(x, mix, modv, w_out, w_ff1)


def _fwd_ff2(x, r, modv, w_ff2, l):
    t = x.shape[0]
    tm = _tok_tile(t)

    def body(x_ref, r_ref, mod_ref, w_ref, o_ref):
        acc = jnp.zeros((tm, D), F32)
        for j in range(N_CHIPS):
            rj = r_ref[:, j * D:(j + 1) * D].astype(F32)
            acc = acc + jnp.dot((rj * rj).astype(BF16), w_ref[j], preferred_element_type=F32)
        o_ref[...] = x_ref[...] + mod_ref[5:6, :] * acc

    return pl.pallas_call(
        body, name=f"fwd_ff2_{l}", out_shape=jax.ShapeDtypeStruct((t, D), F32), grid=(t // tm,),
        in_specs=[pl.BlockSpec((tm, D), lambda i: (i, 0)), pl.BlockSpec((tm, DFF), lambda i: (i, 0)),
                  pl.BlockSpec((None, 8, D), lambda i: (l, 0, 0)), _wspec4(D, D, l)],
        out_specs=pl.BlockSpec((tm, D), lambda i: (i, 0)), compiler_params=_cparams(("parallel",)),
    )(x, r, modv, w_ff2)


def _loss_head(x, target, final_g):
    t = x.shape[0]
    tm = _tok_tile(t)

    def body(x_ref, t_ref, g_ref, dx_ref, st_ref):
        @pl.when(pl.program_id(0) == 0)
        def _():
            st_ref[...] = jnp.zeros_like(st_ref)

        xh, rstd = _rms_stats(x_ref[...])
        g = g_ref[...]
        err = xh * g - t_ref[...]
        loss = 0.5 * jnp.sum(jnp.mean(err * err, axis=-1, keepdims=True), axis=0, keepdims=True)
        dy = err * (1.0 / D)
        st_ref[0:1, :] += jnp.sum(dy * xh, axis=0, keepdims=True)
        st_ref[1:2, :] += jnp.broadcast_to(loss, (1, D))
        dxh = dy * g
        dx_ref[...] = rstd * (dxh - xh * jnp.mean(dxh * xh, axis=-1, keepdims=True))

    return pl.pallas_call(
        body, name="loss_head", out_shape=[jax.ShapeDtypeStruct((t, D), F32), jax.ShapeDtypeStruct((8, D), F32)],
        grid=(t // tm,),
        in_specs=[pl.BlockSpec((tm, D), lambda i: (i, 0)), pl.BlockSpec((tm, D), lambda i: (i, 0)),
                  pl.BlockSpec((1, D), lambda i: (0, 0))],
        out_specs=[pl.BlockSpec((tm, D), lambda i: (i, 0)), pl.BlockSpec((8, D), lambda i: (0, 0))],
        compiler_params=_cparams(("arbitrary",)),
    )(x, target, final_g.reshape(1, D))


def _bwd_ff2(dx2, r, modv, w_ff2, l):
    t = dx2.shape[0]
    tm = _tok_tile(t)

    def body(d_ref, r_ref, mod_ref, w_ref, o_ref):
        dyg = (d_ref[...] * mod_ref[5:6, :]).astype(BF16)
        for j in range(N_CHIPS):
            da = lax.dot_general(dyg, w_ref[j], (((1,), (1,)), ((), ())), preferred_element_type=F32)
            o_ref[:, j * D:(j + 1) * D] = (da * 2.0 * r_ref[:, j * D:(j + 1) * D].astype(F32)).astype(BF16)

    return pl.pallas_call(
        body, name=f"bwd_ff2_{l}", out_shape=jax.ShapeDtypeStruct((t, DFF), BF16), grid=(t // tm,),
        in_specs=[pl.BlockSpec((tm, D), lambda i: (i, 0)), pl.BlockSpec((tm, DFF), lambda i: (i, 0)),
                  pl.BlockSpec((None, 8, D), lambda i: (l, 0, 0)), _wspec4(D, D, l)],
        out_specs=pl.BlockSpec((tm, DFF), lambda i: (i, 0)), compiler_params=_cparams(("parallel",)),
    )(dx2, r, modv, w_ff2)


def _bwd_norm(dy, w, x, dres, modv, l, which, send=(), w_out=None, share=(), exchange=()):
    t = x.shape[0]
    tm = _tok_tile(t)
    nsteps = t // tm
    rows = (6, 1) if which == "in" else (7, 4)
    width = dy.shape[1]
    ns, nh, ne = len(send), len(share), len(exchange)
    nb_ = 0 if w_out is None else 1
    n_in = 5 + nb_ + ns + nh + ne

    def body(*refs):
        dy_ref, w_ref, x_ref, dr_ref, mod_ref = refs[:5]
        wo_ref = refs[5] if nb_ else None
        parts = refs[5 + nb_:5 + nb_ + ns]
        pairs = refs[5 + nb_ + ns + nh:n_in]
        dx_ref, st_ref = refs[n_in:n_in + 2]
        dmix_ref = refs[n_in + 2] if nb_ else None
        o0 = n_in + 2 + nb_
        from_sib, shared, recvs = refs[o0:o0 + ns], refs[o0 + ns:o0 + ns + nh], refs[o0 + ns + nh:o0 + ns + nh + ne]
        scratch = list(refs[o0 + ns + nh + ne:])
        sems = [scratch.pop(0) for _ in range(2 if ns else 0)]
        hsems = [scratch.pop(0) for _ in range(2 if nh else 0)]
        esems = scratch

        @pl.when(pl.program_id(0) == 0)
        def _():
            st_ref[...] = jnp.zeros_like(st_ref)
            if ns:
                for cp in _pair_send_copies(parts, from_sib, sems):
                    cp.start()
            if nh:
                _pair_share_start(shared, hsems)
            if ne:
                for cp in _exchange_copies(pairs, recvs, esems):
                    cp.start()

        if which == "in":
            dh = lax.dot_general(dy_ref[...], w_ref[...], (((1,), (1,)), ((), ())), preferred_element_type=F32)
        else:
            dh = jnp.zeros((tm, D), F32)
            for j in range(N_CHIPS):
                dh = dh + lax.dot_general(dy_ref[:, j * D:(j + 1) * D], w_ref[j], (((1,), (1,)), ((), ())),
                                          preferred_element_type=F32)
        ng, sc = mod_ref[rows[0]:rows[0] + 1, :], mod_ref[rows[1]:rows[1] + 1, :]
        dx, dsh, dsc, dng = _norm_mod_bwd(dh, x_ref[...], ng, sc)
        dx_new = dr_ref[...] + dx
        dx_ref[...] = dx_new
        st_ref[0:1, :] += dsh
        st_ref[1:2, :] += dsc
        st_ref[2:3, :] += dng
        if nb_:
            dyg = (dx_new * mod_ref[2:3, :]).astype(BF16)
            dmix_ref[...] = lax.dot_general(dyg, wo_ref[...].reshape(D, D), (((1,), (1,)), ((), ())),
                                            preferred_element_type=F32).astype(BF16)

        if ns:
            @pl.when(pl.program_id(0) == nsteps - 1)
            def _():
                for cp in _pair_send_copies(parts, from_sib, sems):
                    cp.wait()

        if nh:
            @pl.when(pl.program_id(0) == nsteps - 1)
            def _():
                _pair_share_finish(shared, hsems)

        if ne:
            @pl.when(pl.program_id(0) == nsteps - 1)
            def _():
                for cp in _exchange_copies(pairs, recvs, esems):
                    cp.wait()

    tok = pl.BlockSpec((tm, D), lambda i: (i, 0))
    wspec = pl.BlockSpec((D, NW), lambda i: (0, 0)) if which == "in" else _wspec4(D, D, l)
    return pl.pallas_call(
        body, name=f"bwd_norm_{which}_{l}",
        out_shape=[jax.ShapeDtypeStruct((t, D), F32), jax.ShapeDtypeStruct((8, D), F32)]
        + [jax.ShapeDtypeStruct((t, D), BF16)] * nb_ + _pair_send_shapes(send)
        + [jax.ShapeDtypeStruct(g.shape, g.dtype) for g in share] + _exchange_shapes(exchange),
        grid=(nsteps,),
        in_specs=[pl.BlockSpec((tm, width), lambda i: (i, 0)), wspec, tok, tok,
                  pl.BlockSpec((None, 8, D), lambda i: (l, 0, 0))]
        + [_wspec4(D // N_CHIPS, D, l)] * nb_ + _hbm_specs(ns + nh + ne),
        out_specs=[tok, pl.BlockSpec((8, D), lambda i: (0, 0))] + [tok] * nb_ + _hbm_specs(ns + nh + ne),
        input_output_aliases={5 + nb_ + ns + i: 2 + nb_ + ns + i for i in range(nh)},
        scratch_shapes=(_pair_send_sems(ns) if ns else []) + (_pair_send_sems(nh) if nh else [])
        + (_exchange_sems(ne) if ne else []),
        compiler_params=_cparams(("arbitrary",)),
    )(dy, w, x, dres, modv, *([w_out] * nb_), *send, *share, *exchange)


def _grad_weight(lhs, rhs, modv, l, which, w_gate=None):
    t = lhs.shape[0]
    tm = min(t, 2048)
    nt = t // tm
    gated = which in ("out", "ff2")
    if which == "in":
        nj, lw, rw, orows, ocols = 5, D, NW // 5, D, NW // 5
    elif which == "ff1":
        nj, lw, rw, orows, ocols = N_CHIPS, D, D, D, D
    elif which == "out":
        nj, lw, rw, orows, ocols = 1, D, D, D, D
    else:
        nj, lw, rw, orows, ocols = N_CHIPS, D, D, D, D
    gate_row = 2 if which == "out" else 5

    def body(*refs):
        if gated:
            l_ref, r_ref, mod_ref, wg_ref, o_ref, dg_ref, acc = refs
        else:
            l_ref, r_ref, mod_ref, o_ref, acc = refs
        j, k = pl.program_id(0), pl.program_id(1)

        @pl.when(k == 0)
        def _():
            acc[...] = jnp.zeros_like(acc)

        if which == "ff2":
            lv = l_ref[...].astype(F32)
            lv = lv * lv
        else:
            lv = l_ref[...]
        acc[...] += _dot_tn(lv, r_ref[...])

        if gated:
            @pl.when(jnp.logical_and(j == 0, k == 0))
            def _():
                dg_ref[...] = jnp.zeros_like(dg_ref)

        @pl.when(k == nt - 1)
        def _():
            raw = acc[...]
            if gated:
                o_ref[...] = (raw * mod_ref[gate_row:gate_row + 1, :]).astype(o_ref.dtype)
                dg_ref[0:1, :] += jnp.sum(raw * wg_ref[...].astype(F32), axis=0, keepdims=True)
            else:
                o_ref[...] = raw.astype(o_ref.dtype)

    if which in ("in", "ff1"):
        lspec = pl.BlockSpec((tm, lw), lambda j, k: (k, 0))
        rspec = pl.BlockSpec((tm, rw), lambda j, k: (k, j))
    else:
        lspec = pl.BlockSpec((tm, lw), lambda j, k: (k, j))
        rspec = pl.BlockSpec((tm, rw), lambda j, k: (k, 0))
    mspec = pl.BlockSpec((None, 8, D), lambda j, k: (l, 0, 0))
    flat = which in ("in", "out")
    if flat:
        ospec = pl.BlockSpec((orows, ocols), lambda j, k: (0, j))
        out_shape = [jax.ShapeDtypeStruct((D, nj * ocols), BF16)]
    else:
        ospec = pl.BlockSpec((None, orows, ocols), lambda j, k: (j, 0, 0))
        out_shape = [jax.ShapeDtypeStruct((N_CHIPS, orows, ocols), BF16)]
    in_specs = [lspec, rspec, mspec]
    args = [lhs, rhs, modv]
    out_specs = [ospec]
    if gated:
        if flat:
            in_specs.append(pl.BlockSpec((orows, ocols), lambda j, k: (0, 0)))
            args.append(w_gate.reshape(D, D))
        else:
            in_specs.append(pl.BlockSpec((None, orows, ocols), lambda j, k: (j, 0, 0)))
            args.append(w_gate)
        out_specs.append(pl.BlockSpec((8, D), lambda j, k: (0, 0)))
        out_shape.append(jax.ShapeDtypeStruct((8, D), F32))
    res = pl.pallas_call(
        body, name=f"grad_w_{which}_{l}", out_shape=out_shape, grid=(nj, nt), in_specs=in_specs, out_specs=out_specs,
        scratch_shapes=[pltpu.VMEM((orows, ocols), F32)], compiler_params=_cparams(("arbitrary", "arbitrary")),
    )(*args)
    return (res[0], res[1]) if gated else (res[0], None)


def _tri_masks():
    rows, cols = _iota2((BLK, BLK), 0), _iota2((BLK, BLK), 1)
    return rows >= cols, rows > cols


def _sgu_forward(p_ref, lnp_ref, sguw_ref, sgub_ref, col=_col):
    incl, _ = _tri_masks()
    ug = _gelu(p_ref[:, 0:512])
    vg = _gelu(p_ref[:, 512:1024])
    mu = jnp.mean(vg, axis=-1, keepdims=True)
    xc = vg - mu
    rstd = lax.rsqrt(jnp.mean(xc * xc, axis=-1, keepdims=True) + LN_EPS)
    vhat = xc * rstd
    vn = vhat * lnp_ref[0:1, :] + lnp_ref[1:2, :]
    bias = sgub_ref[...]
    ys, mixed, wms = [], [], []
    for h in range(HEADS):
        wm = jnp.where(incl, sguw_ref[h], 0.0)
        mx = _dot(wm, vn[:, h * HD:(h + 1) * HD]) + col(bias, h)
        ys.append(ug[:, h * HD:(h + 1) * HD] * mx)
        mixed.append(mx)
        wms.append(wm)
    return ys, ug, vhat, rstd, vn, mixed, wms


def _conv_forward(xbuf, cw_ref):
    conv = cw_ref[0:1, :] * xbuf[5:5 + BLK, :]
    for j in range(1, 4):
        conv = conv + cw_ref[j:j + 1, :] * xbuf[5 + j:5 + j + BLK, :]
    return conv


def _gates(gt, gv_ref):
    incl, _ = _tri_masks()
    beta = _sigmoid(gt)
    neg_a = -jnp.exp(gv_ref[0:1, :])
    gl = neg_a * _softplus(gt + gv_ref[1:2, :])
    gc = _dotf(jnp.where(incl, 1.0, 0.0).astype(F32), gl)
    return beta, gl, gc, gc.T, neg_a


def _head_chunk(act, beta, gc, gct, h, col=_col):
    incl, strict = _tri_masks()
    qh = act[:, h * HD:(h + 1) * HD]
    kh = act[:, 512 + h * HD:512 + (h + 1) * HD]
    vh = act[:, 1024 + h * HD:1024 + (h + 1) * HD]
    rq = lax.rsqrt(jnp.sum(qh * qh, axis=-1, keepdims=True) + RMS_EPS)
    rk = lax.rsqrt(jnp.sum(kh * kh, axis=-1, keepdims=True) + RMS_EPS)
    qhat, khat = qh * rq, kh * rk
    qn = qhat * QK_SCALE
    b = col(beta, h)
    gcol = col(gc, 4 + h)
    grow = _row(gct, 4 + h)
    dmat = jnp.where(incl, jnp.exp(jnp.where(incl, gcol - grow, 0.0)), 0.0)
    gam = jnp.exp(gcol)
    glast = _row(gcol, BLK - 1)
    e = jnp.exp(glast - gcol)
    kk = _d3_nt(khat, khat)
    return dict(qhat=qhat, khat=khat, qn=qn, vh=vh, rq=rq, rk=rk, b=b, dmat=dmat, gam=gam, glast=glast, e=e, kk=kk,
                strict=strict, incl=incl)


def _mixer_forward(p, lnp, sgu_w, sgu_bt, cw, gv, l, gather=()):
    t = p.shape[0]
    nb = t // BLK
    ng = len(gather)

    def body(*refs):
        p_ref, lnp_ref, sguw_ref, sgub_ref, cw_ref, gv_ref = refs[:6]
        mix_ref, s_out, t_out, u_out, w_out, o_out, conv_out = refs[6 + ng:13 + ng]
        gbufs = refs[13 + ng:13 + 2 * ng]
        s_scr, xbuf = refs[13 + 2 * ng:15 + 2 * ng]
        gsems = refs[15 + 2 * ng:]

        @pl.when(pl.program_id(0) == 0)
        def _():
            s_scr[...] = jnp.zeros_like(s_scr)
            xbuf[0:8, :] = jnp.zeros((8, 1536), F32)
            if ng:
                _gather_start(gbufs, gsems)

        ys = _sgu_forward(p_ref, lnp_ref, sguw_ref, sgub_ref, col=_col_slice)[0]
        for h in range(HEADS):
            mix_ref[:, h * HD:(h + 1) * HD] = ys[h].astype(BF16)

        xbuf[8:8 + BLK, :] = p_ref[:, 1024:2560]
        conv = _conv_forward(xbuf, cw_ref)
        conv_out[...] = conv
        act = _silu(conv)
        xbuf[0:8, :] = xbuf[BLK:BLK + 8, :]
        beta, _, gc, gct, _ = _gates(p_ref[:, GATE0:NW], gv_ref)
        chunks = [_head_chunk(act, beta, gc, gct, h, col=_col_slice) for h in range(HEADS)]
        for h, hc in enumerate(chunks):
            t_out[h] = jnp.where(hc["strict"], hc["b"] * hc["kk"] * hc["dmat"], 0.0)
        t_out[...] = _tri_inverse(t_out[...])
        tm_all = t_out[...]
        u_all = _bdot(tm_all, jnp.stack([hc["b"] * hc["vh"] for hc in chunks]))
        w_all = _bdot(tm_all, jnp.stack([(hc["b"] * hc["gam"]) * hc["khat"] for hc in chunks]))
        s_all = s_scr[...]
        s_out[...] = s_all
        wn_all = u_all - _bdot(w_all, s_all)
        qkm_all = jnp.stack([_dot_nt(hc["qn"], hc["khat"]) * hc["dmat"] for hc in chunks])
        o_all = _bdot(jnp.stack([hc["qn"] * hc["gam"] for hc in chunks]), s_all) + _bdot(qkm_all, wn_all)
        upd_all = _bdot(jnp.stack([hc["khat"] * hc["e"] for hc in chunks]), wn_all, ((1,), (1,)))
        for h, hc in enumerate(chunks):
            s_scr[h] = jnp.exp(hc["glast"]) * s_all[h] + upd_all[h]
            sl = slice(h * HD, (h + 1) * HD)
            o = o_all[h]
            u_out[:, sl] = u_all[h]
            w_out[:, sl] = w_all[h]
            o_out[:, sl] = o
            on = o * lax.rsqrt(jnp.mean(o * o, axis=-1, keepdims=True) + RMS_EPS) * gv_ref[2:3, :]
            mix_ref[:, 512 + h * HD:512 + (h + 1) * HD] = (on * _silu(p_ref[:, 2560 + h * HD:2560 + (h + 1) * HD])).astype(BF16)

        if ng:
            @pl.when(pl.program_id(0) == nb - 1 - min(3, nb - 1))
            def _():
                _gather_forward(gbufs, gsems)

            @pl.when(pl.program_id(0) == nb - 1)
            def _():
                _gather_finish(gbufs, gsems)

    tok = lambda w: pl.BlockSpec((BLK, w), lambda i: (i, 0))
    st = pl.BlockSpec((None, HEADS, HD, HD), lambda i: (i, 0, 0, 0))
    return pl.pallas_call(
        body, name=f"mixer_fwd_{l}", grid=(nb,),
        out_shape=[jax.ShapeDtypeStruct((t, D), BF16), jax.ShapeDtypeStruct((nb, HEADS, HD, HD), F32),
                   jax.ShapeDtypeStruct((nb, HEADS, HD, HD), F32), jax.ShapeDtypeStruct((t, 512), F32),
                   jax.ShapeDtypeStruct((t, 512), F32), jax.ShapeDtypeStruct((t, 512), F32),
                   jax.ShapeDtypeStruct((t, 1536), F32)]
        + [jax.ShapeDtypeStruct(b.shape, b.dtype) for b in gather],
        in_specs=[tok(NW), pl.BlockSpec((None, 8, 512), lambda i: (l, 0, 0)),
                  pl.BlockSpec((None, HEADS, HD, HD), lambda i: (l, 0, 0, 0)),
                  pl.BlockSpec((None, HD, HD), lambda i: (l, 0, 0)), pl.BlockSpec((None, 8, 1536), lambda i: (l, 0, 0)),
                  pl.BlockSpec((None, 8, HD), lambda i: (l, 0, 0))] + _hbm_specs(ng),
        out_specs=[tok(D), st, st, tok(512), tok(512), tok(512), tok(1536)]
        + _hbm_specs(ng),
        input_output_aliases={6 + i: 7 + i for i in range(ng)},
        scratch_shapes=[pltpu.VMEM((HEADS, HD, HD), F32), pltpu.VMEM((BLK + 8, 1536), F32)]
        + (_gather_sems(ng) if ng else []),
        compiler_params=_cparams(("arbitrary",)),
    )(p, lnp, sgu_w, sgu_bt, cw, gv, *gather)


def _mixer_backward(p, dmix, saved, lnp, sgu_w, sgu_bt, cw, gv, l, exchange=()):
    t = p.shape[0]
    nb = t // BLK
    s_sv, t_sv, u_sv, w_sv, o_sv, conv_sv = saved
    ne = len(exchange)

    def body(*refs):
        (p_ref, dmix_ref, s_ref, t_ref, u_ref, w_ref, o_ref, conv_ref, lnp_ref, sguw_ref, sgub_ref, cw_ref,
         gv_ref) = refs[:13]
        pairs = refs[13:13 + ne]
        dp_ref, dlnp_ref, dsguw_ref, dsgub_ref, dcw_ref, dgv_ref = refs[13 + ne:19 + ne]
        recvs = refs[19 + ne:19 + 2 * ne]
        ds_scr, dcbuf = refs[19 + 2 * ne:21 + 2 * ne]
        esems = refs[21 + 2 * ne:]

        @pl.when(pl.program_id(0) == 0)
        def _():
            if ne:
                for cp in _exchange_copies(pairs, recvs, esems):
                    cp.start()
            ds_scr[...] = jnp.zeros_like(ds_scr)
            dcbuf[BLK:BLK + 8, :] = jnp.zeros((8, 1536), F32)
            dlnp_ref[...] = jnp.zeros_like(dlnp_ref)
            dsguw_ref[...] = jnp.zeros_like(dsguw_ref)
            dsgub_ref[...] = jnp.zeros_like(dsgub_ref)
            dcw_ref[...] = jnp.zeros_like(dcw_ref)
            dgv_ref[...] = jnp.zeros_like(dgv_ref)

        incl, strict = _tri_masks()
        _, ug, vhat, rstd, vn, mixed, wms = _sgu_forward(p_ref, lnp_ref, sguw_ref, sgub_ref)
        dvn_parts, dug_parts = [], []
        dbias = jnp.zeros((BLK, HD), F32)
        for h in range(HEADS):
            sl = slice(h * HD, (h + 1) * HD)
            dy = dmix_ref[:, sl].astype(F32)
            dmx = dy * ug[:, sl]
            dug_parts.append(dy * mixed[h])
            dsguw_ref[h] += jnp.where(incl, _dot_nt(dmx, vn[:, sl]), 0.0)
            dbias = dbias + _put_col(jnp.sum(dmx, axis=1, keepdims=True), h)
            dvn_parts.append(_dot_tn(wms[h], dmx))
        dsgub_ref[...] += dbias
        dvn = jnp.concatenate(dvn_parts, axis=1)
        dug = jnp.concatenate(dug_parts, axis=1)
        dlnp_ref[0:1, :] += jnp.sum(dvn * vhat, axis=0, keepdims=True)
        dlnp_ref[1:2, :] += jnp.sum(dvn, axis=0, keepdims=True)
        dvhat = dvn * lnp_ref[0:1, :]
        dvg = rstd * (dvhat - jnp.mean(dvhat, axis=-1, keepdims=True)
                      - vhat * jnp.mean(dvhat * vhat, axis=-1, keepdims=True))
        dp_ref[:, 0:512] = (dug * _gelu_grad(p_ref[:, 0:512])).astype(BF16)
        dp_ref[:, 512:1024] = (dvg * _gelu_grad(p_ref[:, 512:1024])).astype(BF16)

        conv = conv_ref[...]
        act = _silu(conv)
        gt = p_ref[:, GATE0:NW]
        beta, gl, gc, gct, neg_a = _gates(gt, gv_ref)
        gng = gv_ref[2:3, :]
        dbeta_t = jnp.zeros((BLK, HD), F32)
        dgc_t = jnp.zeros((BLK, HD), F32)
        dgng = jnp.zeros((1, HD), F32)
        for h in range(HEADS):
            sl = slice(h * HD, (h + 1) * HD)
            hc = _head_chunk(act, beta, gc, gct, h)
            b, gam, e, dmat, kk = hc["b"], hc["gam"], hc["e"], hc["dmat"], hc["kk"]
            qn, khat, vh = hc["qn"], hc["khat"], hc["vh"]
            gamlast = jnp.exp(hc["glast"])
            s, tm, u, w, o = s_ref[h], t_ref[h], u_ref[:, sl], w_ref[:, sl], o_ref[:, sl]
            ds_next = ds_scr[h]
            z = p_ref[:, 2560 + h * HD:2560 + (h + 1) * HD]
            dy = dmix_ref[:, 512 + h * HD:512 + (h + 1) * HD].astype(F32)
            ro = lax.rsqrt(jnp.mean(o * o, axis=-1, keepdims=True) + RMS_EPS)
            ohat = o * ro
            dp_ref[:, 2560 + h * HD:2560 + (h + 1) * HD] = (dy * ohat * gng * _silu_grad(z)).astype(BF16)
            don = dy * _silu(z)
            dgng = dgng + jnp.sum(don * ohat, axis=0, keepdims=True)
            dohat = don * gng
            do = ro * (dohat - ohat * jnp.mean(dohat * ohat, axis=-1, keepdims=True))
            qk_raw = _dot_nt(qn, khat)
            qkm = qk_raw * dmat
            qd, kd = qn * gam, khat * e
            wn = u - _dot(w, s)
            dwn = _dot_tn(qkm, do) + _dot(kd, ds_next)
            dqd = _dot_nt(do, s)
            dqkm = jnp.where(incl, _dot_nt(do, wn), 0.0)
            ds_scr[h] = _dot_tn(qd, do) + gamlast * ds_next - _dot_tn(w, dwn)
            dgamlast = jnp.sum(jnp.sum(ds_next * s, axis=1, keepdims=True), axis=0, keepdims=True)
            dkd = _dot_nt(wn, ds_next)
            dw = -_dot_nt(dwn, s)
            db1 = _dot_tn(tm, dwn)
            db2 = _dot_tn(tm, dw)
            dm = jnp.where(strict, -(_dot_nt(db1, u) + _dot_nt(db2, w)), 0.0)
            dbeta = (jnp.sum(dm * kk * dmat, axis=1, keepdims=True) + jnp.sum(db1 * vh, axis=1, keepdims=True)
                     + gam * jnp.sum(db2 * khat, axis=1, keepdims=True))
            dkkm = dm * b * dmat
            ddm = dm * b * kk + dqkm * qk_raw
            dgam = b * jnp.sum(db2 * khat, axis=1, keepdims=True) + jnp.sum(dqd * qn, axis=1, keepdims=True)
            g_qk = dqkm * dmat
            dqn = _dot(g_qk, khat) + dqd * gam
            dkhat = ((b * gam) * db2 + _dot_tn(g_qk, qn) + _dot(dkkm, khat) + _dot_tn(dkkm, khat) + dkd * e)
            dvh = b * db1
            rkd = jnp.sum(dkd * kd, axis=1, keepdims=True)
            emat = ddm * dmat
            dgc = (dgam * gam - rkd + jnp.sum(emat, axis=1, keepdims=True)
                   - jnp.sum(emat.T, axis=1, keepdims=True))
            last = _iota2((BLK, 1), 0) == BLK - 1
            dgc = dgc + jnp.where(last, jnp.sum(rkd, axis=0, keepdims=True) + dgamlast * gamlast, 0.0)
            dgc_t = dgc_t + _put_col(dgc, 4 + h)
            dbeta_t = dbeta_t + _put_col(dbeta, h)
            dqhat = dqn * QK_SCALE
            dq = hc["rq"] * (dqhat - hc["qhat"] * jnp.sum(dqhat * hc["qhat"], axis=-1, keepdims=True))
            dk = hc["rk"] * (dkhat - khat * jnp.sum(dkhat * khat, axis=-1, keepdims=True))
            dcbuf[0:BLK, h * HD:(h + 1) * HD] = dq
            dcbuf[0:BLK, 512 + h * HD:512 + (h + 1) * HD] = dk
            dcbuf[0:BLK, 1024 + h * HD:1024 + (h + 1) * HD] = dvh
        dgv_ref[2:3, :] += dgng
        dgl = _dotf_tn(jnp.where(incl, 1.0, 0.0).astype(F32), dgc_t)
        sig_a = _sigmoid(gt + gv_ref[1:2, :])
        d_araw = dgl * neg_a * sig_a
        dgv_ref[0:1, :] += jnp.sum(dgl * gl, axis=0, keepdims=True)
        dgv_ref[1:2, :] += jnp.sum(d_araw, axis=0, keepdims=True)
        dp_ref[:, GATE0:NW] = (dbeta_t * beta * (1.0 - beta) + d_araw).astype(BF16)
        dcbuf[0:BLK, :] = dcbuf[0:BLK, :] * _silu_grad(conv)
        xcur = p_ref[:, 1024:2560]
        dqkv = jnp.zeros((BLK, 1536), F32)
        for j in range(4):
            shifted = dcbuf[3 - j:3 - j + BLK, :]
            dqkv = dqkv + cw_ref[j:j + 1, :] * shifted
            dcw_ref[j:j + 1, :] += jnp.sum(shifted * xcur, axis=0, keepdims=True)
        dp_ref[:, 1024:2560] = dqkv.astype(BF16)
        dcbuf[BLK:BLK + 8, :] = dcbuf[0:8, :]

        if ne:
            @pl.when(pl.program_id(0) == nb - 1)
            def _():
                for cp in _exchange_copies(pairs, recvs, esems):
                    cp.wait()

    rev = lambda w: pl.BlockSpec((BLK, w), lambda i: (nb - 1 - i, 0))
    st = pl.BlockSpec((None, HEADS, HD, HD), lambda i: (nb - 1 - i, 0, 0, 0))
    fix = lambda *shape: pl.BlockSpec((None,) + shape, lambda i: (l,) + (0,) * len(shape))
    acc = lambda *shape: pl.BlockSpec(shape, lambda i: (0,) * len(shape))
    return pl.pallas_call(
        body, name=f"mixer_bwd_{l}", grid=(nb,),
        out_shape=[jax.ShapeDtypeStruct((t, NW), BF16), jax.ShapeDtypeStruct((8, 512), F32),
                   jax.ShapeDtypeStruct((HEADS, HD, HD), F32), jax.ShapeDtypeStruct((HD, HD), F32),
                   jax.ShapeDtypeStruct((8, 1536), F32), jax.ShapeDtypeStruct((8, HD), F32)]
        + _exchange_shapes(exchange),
        in_specs=[rev(NW), rev(D), st, st, rev(512), rev(512), rev(512),
                  rev(1536),
                  fix(8, 512), fix(HEADS, HD, HD), fix(HD, HD), fix(8, 1536), fix(8, HD)] + _hbm_specs(ne),
        out_specs=[rev(NW), acc(8, 512), acc(HEADS, HD, HD), acc(HD, HD), acc(8, 1536), acc(8, HD)]
        + _hbm_specs(ne),
        scratch_shapes=[pltpu.VMEM((HEADS, HD, HD), F32), pltpu.VMEM((BLK + 8, 1536), F32)]
        + (_exchange_sems(ne) if ne else []),
        compiler_params=_cparams(("arbitrary",)),
    )(p, dmix, s_sv, t_sv, u_sv, w_sv, o_sv, conv_sv, lnp, sgu_w, sgu_bt, cw, gv, *exchange)


_SMALL = (("b_ada", 24), ("norm1_g", 8), ("norm2_g", 8), ("final_g", 8), ("sgu_ln_g", 8), ("sgu_ln_b", 8),
          ("sgu_w", 256), ("sgu_b", 8), ("conv_w", 24), ("a_log", 8), ("dt_bias", 8), ("gdn_norm_g", 8))
_SMALL_PAD = sum(n for _, n in _SMALL)
_DMOD_ROWS = 24


def _pack_rows(parts):
    rows = []
    for (name, n), a in zip(_SMALL, parts):
        flat = a.reshape(-1).astype(F32)
        rows.append(jnp.pad(flat, (0, n * D - flat.shape[0])).reshape(n, D))
    return jnp.concatenate(rows, axis=0)


def _unpack_rows(buf, shapes):
    out, r0 = {}, 0
    for name, n in _SMALL:
        size = math.prod(shapes[name])
        out[name] = buf[r0:r0 + n].reshape(-1)[:size].reshape(shapes[name])
        r0 += n
    return out


_COMBINED_ROWS = 2 * _DMOD_ROWS + _SMALL_PAD
_GATHER_ROWS = -(-_COMBINED_ROWS // 16) * 16


def _pair_combine(own, sib, place):
    rows = own.shape[0]

    def body(p_ref, a_ref, b_ref, o_ref):
        first = lax.axis_index("c") == 0
        a, b = a_ref[0:_DMOD_ROWS, :], b_ref[0:_DMOD_ROWS, :]
        o_ref[0:_DMOD_ROWS, :] = jnp.where(first, a, b)
        o_ref[_DMOD_ROWS:2 * _DMOD_ROWS, :] = jnp.where(first, b, a)
        o_ref[2 * _DMOD_ROWS:_COMBINED_ROWS, :] = a_ref[_DMOD_ROWS:, :] + b_ref[_DMOD_ROWS:, :]
        o_ref[_COMBINED_ROWS:, :] = jnp.zeros((_GATHER_ROWS - _COMBINED_ROWS, D), F32)

    return pl.pallas_call(
        body, name="small_pair_combine", out_shape=jax.ShapeDtypeStruct((N_CHIPS, _GATHER_ROWS, D), F32),
        grid_spec=pltpu.PrefetchScalarGridSpec(
            num_scalar_prefetch=1, grid=(1,),
            in_specs=[pl.BlockSpec((rows, D), lambda i, pr: (0, 0)), pl.BlockSpec((rows, D), lambda i, pr: (0, 0))],
            out_specs=pl.BlockSpec((None, _GATHER_ROWS, D), lambda i, pr: (pr[0], 0, 0))),
        compiler_params=_cparams(("arbitrary",)),
    )(place, own, sib)


def _small_finalize(gathered, w, m, v):
    def body(g_ref, w_ref, m_ref, v_ref, go_ref, d_ref, nm_ref, nv_ref):
        lo, hi = 2 * _DMOD_ROWS, _COMBINED_ROWS
        sm = g_ref[0, lo:hi, :] + g_ref[1, lo:hi, :]
        sm = sm + g_ref[2, lo:hi, :]
        sm = sm + g_ref[3, lo:hi, :]
        bsum = jnp.zeros((_DMOD_ROWS, D), F32)
        for j in range(N_CHIPS):
            bsum = bsum + g_ref[j, 0:_DMOD_ROWS, :]
            bsum = bsum + g_ref[j, _DMOD_ROWS:2 * _DMOD_ROWS, :]
        go_ref[0:_DMOD_ROWS, :] = bsum
        go_ref[_DMOD_ROWS:, :] = sm[_DMOD_ROWS:, :]
        d_ref[...], nm_ref[...], nv_ref[...] = _adam_math(w_ref[...], go_ref[...], m_ref[...], v_ref[...])

    return pl.pallas_call(
        body, name="small_finalize", out_shape=[jax.ShapeDtypeStruct(w.shape, F32)] * 4,
        compiler_params=pltpu.CompilerParams(vmem_limit_bytes=VMEM_LIMIT),
    )(gathered, w, m, v)


def kernel(x, c, w_ada, b_ada, norm1_g, w_in, sgu_ln_g, sgu_ln_b, sgu_w, sgu_b, conv_w, a_log, dt_bias, gdn_norm_g, w_out, norm2_g, w_ff1, w_ff2, final_g, loss_target, m_w_ada, m_b_ada, m_norm1_g, m_w_in, m_sgu_ln_g, m_sgu_ln_b, m_sgu_w, m_sgu_b, m_conv_w, m_a_log, m_dt_bias, m_gdn_norm_g, m_w_out, m_norm2_g, m_w_ff1, m_w_ff2, m_final_g, v_w_ada, v_b_ada, v_norm1_g, v_w_in, v_sgu_ln_g, v_sgu_ln_b, v_sgu_w, v_sgu_b, v_conv_w, v_a_log, v_dt_bias, v_gdn_norm_g, v_w_out, v_norm2_g, v_w_ff1, v_w_ff2, v_final_g):
    xi, yi, ci = lax.axis_index("x"), lax.axis_index("y"), lax.axis_index("c")
    chip = 2 * xi + yi
    dev = 2 * chip + ci
    t = x.shape[1]
    x0 = x.reshape(t, D)
    target = loss_target.reshape(t, D)

    c_sib = _pair_exchange(c, "c_pair")
    c_pair = jnp.where(ci == 0, jnp.concatenate([c, c_sib], 0), jnp.concatenate([c_sib, c], 0))
    c_all = _chip_allgather(c_pair, "c_chips").reshape(8, D)
    ada_cols = w_ada.shape[2]
    b_cols = lax.dynamic_slice_in_dim(b_ada, chip * ada_cols, ada_cols, axis=1)
    mod_part = _ada_forward(c_all, w_ada, b_cols)
    conv_cols = conv_w.shape[2]
    packed = jnp.concatenate([mod_part.reshape(DEPTH * 8, ada_cols), conv_w.reshape(DEPTH, 4 * conv_cols)], axis=0)
    packed = _chip_allgather(packed, "mod_chips")
    mod_all = packed[:, :DEPTH * 8].reshape(N_CHIPS, DEPTH, 8, ada_cols)
    mod_mine = lax.dynamic_index_in_dim(mod_all, dev, axis=2, keepdims=False)
    mod = mod_mine.transpose(1, 0, 2).reshape(DEPTH, 6, D)
    modv = jnp.concatenate([mod, norm1_g[:, None, :], norm2_g[:, None, :]], axis=1)
    conv_full = packed[:, DEPTH * 8:].reshape(N_CHIPS, DEPTH, 4, conv_cols).transpose(1, 2, 0, 3).reshape(DEPTH, 4, 1536)

    place = jnp.stack([chip, ci]).astype(jnp.int32)
    wbufs = [[_cast_into_slot(w, l, place) for w in (w_in, w_out, w_ff1, w_ff2)] for l in range(DEPTH)]
    wbufs[0][:1] = _weights_allgather(wbufs[0][:1], 0)

    def full_w_in(g):
        return jnp.pad(g.transpose(1, 0, 2).reshape(D, IN_W), ((0, 0), (0, NW - IN_W)))

    lnp = jnp.pad(jnp.stack([sgu_ln_g, sgu_ln_b], axis=1), ((0, 0), (0, 6), (0, 0)))
    sgu_bt = jnp.pad(sgu_b.transpose(0, 2, 1), ((0, 0), (0, 0), (0, HD - HEADS)))
    cw = jnp.pad(conv_full, ((0, 0), (0, 4), (0, 0)))
    lane_pad = lambda a: jnp.pad(a, ((0, 0), (4, HD - 8)))
    gv = jnp.pad(jnp.stack([lane_pad(a_log), lane_pad(dt_bias), gdn_norm_g], axis=1), ((0, 0), (0, 5), (0, 0)))

    acts = []
    xl = x0
    for l in range(DEPTH):
        win = full_w_in(wbufs[l][0])
        p, h1 = _fwd_in(xl, modv, win, l)
        nxt = wbufs[l][1:] + (wbufs[l + 1][:1] if l + 1 < DEPTH else [])
        mix, *rest = _mixer_forward(p, lnp, sgu_w, sgu_bt, cw, gv, l, gather=nxt)
        saved = rest[:6]
        wbufs[l][1:] = rest[6:9]
        if l + 1 < DEPTH:
            wbufs[l + 1][:1] = rest[9:]
        g_in, g_out, g_ff1, g_ff2 = wbufs[l]
        x1, r, h2 = _fwd_out_ff1(xl, mix, modv, g_out, g_ff1, l)
        x2 = _fwd_ff2(x1, r, modv, g_ff2, l)
        acts.append((xl, p, mix, saved, x1, r, h1, h2, win))
        xl = x2

    dx, head_stats = _loss_head(xl, target, final_g)
    loss = lax.psum(jnp.sum(head_stats[1, 0:1]), ("x", "y", "c"))
    d_final_g = head_stats[0]
    names = ("in", "out", "ff1", "ff2")
    grads_buf = [None] * len(names)

    def pair_sums(partials, from_sib, kinds, lay):
        return [(lay, n, _pair_sum(g, ga, place, f"pair_sum_{n}_{lay}")) for g, ga, n in zip(partials, from_sib, kinds)]

    def reduce_into_buffers(items, recv):
        for (lay, n, pair), rc in zip(items, recv):
            i = names.index(n)
            grads_buf[i] = _chip_sum(pair, rc, grads_buf[i], lay, place, f"chip_sum_{n}_{lay}")

    pending = []

    dmod, small = [None] * DEPTH, [None] * DEPTH
    for l in reversed(range(DEPTH)):
        xl, p, mix, saved, x1, r, h1, h2, win = acts[l]
        g_in, g_out, g_ff1, g_ff2 = wbufs[l]
        df = _bwd_ff2(dx, r, modv, g_ff2, l)
        gw_ff2, dg2 = _grad_weight(r, dx, modv, l, "ff2", g_ff2)
        gw_ff1, _ = _grad_weight(h2, df, modv, l, "ff1")
        dx1, st2, dmix, *sib_ff = _bwd_norm(df, g_ff1, x1, dx, modv, l, "ff1", send=[gw_ff1, gw_ff2], w_out=g_out)
        gw_out, dg1 = _grad_weight(mix, dx1, modv, l, "out", g_out)
        gw_out = gw_out.reshape(N_CHIPS, D // N_CHIPS, D)
        sib_out = _grads_pair_send([gw_out], f"out_{l}")
        pending = pending + pair_sums([gw_out, gw_ff1, gw_ff2], list(sib_out) + sib_ff, names[1:], l)
        dp, dlnp, dsguw, dsgub, dcw, dgv, *recv = _mixer_backward(p, dmix, saved, lnp, sgu_w, sgu_bt, cw, gv, l,
                                                                  exchange=[item[2] for item in pending])
        reduce_into_buffers(pending, recv)
        gw_in, _ = _grad_weight(h1, dp, modv, l, "in")
        gw_in_c = gw_in[:, :IN_W].reshape(D, N_CHIPS, IN_W // N_CHIPS).transpose(1, 0, 2)
        if l > 0:
            dx, st1, sib_in = _bwd_norm(dp, win, xl, dx1, modv, l, "in", send=[gw_in_c])
            pending = pair_sums([gw_in_c], [sib_in], names[:1], l)
        else:
            pending = pair_sums([gw_in_c], _grads_pair_send([gw_in_c], "in_0"), names[:1], l)
            dx, st1, *rest = _bwd_norm(dp, win, xl, dx1, modv, l, "in", share=grads_buf[1:],
                                       exchange=[item[2] for item in pending])
            grads_buf[1:] = rest[:3]
            reduce_into_buffers(pending, rest[3:])
        dmod[l] = jnp.stack([st1[0], st1[1], dg1[0], st2[0], st2[1], dg2[0]], axis=0)
        small[l] = dict(norm1_g=st1[2], norm2_g=st2[2], sgu_ln_g=dlnp[0], sgu_ln_b=dlnp[1], sgu_w=dsguw,
                        sgu_b=dsgub[:, :HEADS].T, conv_w=dcw[:4], a_log=dgv[0, 4:8], dt_bias=dgv[1, 4:8],
                        gdn_norm_g=dgv[2])
    grad_x = dx.reshape(1, t, D)

    stack = lambda k: jnp.stack([small[l][k] for l in range(DEPTH)], axis=0)
    small_grads = [jnp.zeros((DEPTH, 6 * D), F32), stack("norm1_g"), stack("norm2_g"), d_final_g, stack("sgu_ln_g"),
                   stack("sgu_ln_b"), stack("sgu_w"), stack("sgu_b"), stack("conv_w"), stack("a_log"),
                   stack("dt_bias"), stack("gdn_norm_g")]
    own = jnp.concatenate([jnp.stack(dmod, axis=0).reshape(_DMOD_ROWS, D), _pack_rows(small_grads)], axis=0)
    sib = _pair_exchange(own, "small_pair")
    gathered, = _weights_allgather([_pair_combine(own, sib, place)], "small")
    small_shapes = dict(b_ada=b_ada.shape, norm1_g=norm1_g.shape, norm2_g=norm2_g.shape, final_g=final_g.shape,
                        sgu_ln_g=sgu_ln_g.shape, sgu_ln_b=sgu_ln_b.shape, sgu_w=sgu_w.shape, sgu_b=sgu_b.shape,
                        conv_w=(DEPTH, 4, 1536), a_log=a_log.shape, dt_bias=dt_bias.shape,
                        gdn_norm_g=gdn_norm_g.shape)

    def full_conv(a):
        return lax.dynamic_update_slice_in_dim(jnp.zeros((DEPTH, 4, 1536), F32), a, chip * conv_cols, axis=2)

    def pack_state(b_, n1, n2, fg, lg, lb, sw, sb, cv, al, db, gn):
        return _pack_rows([b_, n1, n2, fg, lg, lb, sw, sb, full_conv(cv), al, db, gn])

    w_small = pack_state(b_ada, norm1_g, norm2_g, final_g, sgu_ln_g, sgu_ln_b, sgu_w, sgu_b, conv_w, a_log, dt_bias,
                         gdn_norm_g)
    m_small = pack_state(m_b_ada, m_norm1_g, m_norm2_g, m_final_g, m_sgu_ln_g, m_sgu_ln_b, m_sgu_w, m_sgu_b, m_conv_w,
                         m_a_log, m_dt_bias, m_gdn_norm_g)
    v_small = pack_state(v_b_ada, v_norm1_g, v_norm2_g, v_final_g, v_sgu_ln_g, v_sgu_ln_b, v_sgu_w, v_sgu_b, v_conv_w,
                         v_a_log, v_dt_bias, v_gdn_norm_g)
    small_out = _small_finalize(gathered, w_small, m_small, v_small)
    sg, sd, sm, sv = [_unpack_rows(a, small_shapes) for a in small_out]
    for dct in (sg, sd, sm, sv):
        dct["conv_w"] = lax.dynamic_slice_in_dim(dct["conv_w"], chip * conv_cols, conv_cols, axis=2)

    dmod_all = gathered[:, :2 * _DMOD_ROWS].reshape(8, DEPTH, 6 * D)
    dmod_cols = lax.dynamic_slice_in_dim(dmod_all, chip * ada_cols, ada_cols, axis=2).transpose(1, 0, 2)
    g_ada, d_ada, nm_ada, nv_ada = _ada_backward_adamw(c_all, dmod_cols, w_ada, m_w_ada, v_w_ada)

    grads_buf[:1] = _grads_pair_share(grads_buf[:1], "grads_pair_share_in")
    big = {}
    for n, g, (w, m, v) in zip(names, grads_buf, ((w_in, m_w_in, v_w_in), (w_out, m_w_out, v_w_out),
                                                  (w_ff1, m_w_ff1, v_w_ff1), (w_ff2, m_w_ff2, v_w_ff2))):
        big[n] = (g,) + tuple(_adamw(w, g, m, v, f"adamw_{n}"))

    def outs(k):
        s = (sg, sd, sm, sv)[k]
        return [(g_ada, d_ada, nm_ada, nv_ada)[k], s["b_ada"], s["norm1_g"], big["in"][k], s["sgu_ln_g"],
                s["sgu_ln_b"], s["sgu_w"], s["sgu_b"], s["conv_w"], s["a_log"], s["dt_bias"], s["gdn_norm_g"],
                big["out"][k], s["norm2_g"], big["ff1"][k], big["ff2"][k], s["final_g"]]

    return (loss, grad_x, *outs(0), *outs(1), *outs(2), *outs(3))
```

```python
import functools
import math

import jax
import jax.numpy as jnp
from jax import lax
from jax.experimental import pallas as pl
from jax.experimental.pallas import tpu as pltpu

F32 = jnp.float32
BF16 = jnp.bfloat16

DEPTH = 4
D = 1024
HEADS = 4
HD = 128
BLK = 128
IN_W = 3080
NW = 3200
GATE0 = 3072
DFF = 4096
N_CHIPS = 4
RMS_EPS = 1e-6
LN_EPS = 1e-5
QK_SCALE = HD ** -0.5
LR, B1, B2, ADAM_EPS, WD, STEP = 0.001, 0.9, 0.999, 1e-08, 0.01, 10
VMEM_LIMIT = 56 * 1024 * 1024
MESH = pl.DeviceIdType.MESH
HOPS = ((1, 0), (0, 1), (1, 1))
HI = lax.Precision.HIGHEST


def _dot(a, b):
    return jnp.dot(a.astype(BF16), b.astype(BF16), preferred_element_type=F32)


def _dot_nt(a, b):
    return lax.dot_general(a.astype(BF16), b.astype(BF16), (((1,), (1,)), ((), ())), preferred_element_type=F32)


def _dot_tn(a, b):
    return lax.dot_general(a.astype(BF16), b.astype(BF16), (((0,), (0,)), ((), ())), preferred_element_type=F32)


def _bdot(a, b, dims=((2,), (1,))):
    return lax.dot_general(a.astype(BF16), b.astype(BF16), (dims, ((0,), (0,))), preferred_element_type=F32)


def _dotf(a, b):
    return jnp.dot(a, b, precision=HI, preferred_element_type=F32)


def _split(a):
    hi = a.astype(BF16)
    return hi, (a - hi.astype(F32)).astype(BF16)


def _dg3(a, b, dims, batch=((), ())):
    ah, al = _split(a)
    bh, bl = _split(b)
    f = lambda x, y: lax.dot_general(x, y, (dims, batch), preferred_element_type=F32)
    return f(ah, bh) + (f(ah, bl) + f(al, bh))


def _bmm3(a, b):
    return _dg3(a, b, ((2,), (1,)), ((0,), (0,)))


def _d3(a, b):
    return _dg3(a, b, ((1,), (0,)))


def _d3_nt(a, b):
    return _dg3(a, b, ((1,), (1,)))


def _d3_tn(a, b):
    return _dg3(a, b, ((0,), (0,)))


def _dotf_tn(a, b):
    return lax.dot_general(a, b, (((0,), (0,)), ((), ())), precision=HI, preferred_element_type=F32)


def _sigmoid(x):
    return 1.0 / (1.0 + jnp.exp(-x))


def _softplus(x):
    return jnp.maximum(x, 0.0) + jnp.log(1.0 + jnp.exp(-jnp.abs(x)))


_G0 = math.sqrt(2.0 / math.pi)
_G1 = 0.044715


def _gelu(x):
    t = jnp.tanh(_G0 * (x + _G1 * x * x * x))
    return 0.5 * x * (1.0 + t)


def _gelu_grad(x):
    t = jnp.tanh(_G0 * (x + _G1 * x * x * x))
    return 0.5 * (1.0 + t) + 0.5 * x * (1.0 - t * t) * (_G0 * (1.0 + 3.0 * _G1 * x * x))


def _silu(x):
    return x * _sigmoid(x)


def _silu_grad(x):
    s = _sigmoid(x)
    return s * (1.0 + x * (1.0 - s))


def _rms_stats(x):
    rstd = lax.rsqrt(jnp.mean(x * x, axis=-1, keepdims=True) + RMS_EPS)
    return x * rstd, rstd


def _norm_mod(x, ng, sc, sh):
    xh, _ = _rms_stats(x)
    return xh * (ng * (1.0 + sc)) + sh


def _norm_mod_bwd(dh, x, ng, sc):
    xh, rstd = _rms_stats(x)
    dsh = jnp.sum(dh, axis=0, keepdims=True)
    dsc = jnp.sum(dh * xh, axis=0, keepdims=True) * ng
    dng = jnp.sum(dh * xh, axis=0, keepdims=True) * (1.0 + sc)
    dxh = dh * (ng * (1.0 + sc))
    dx = rstd * (dxh - xh * jnp.mean(dxh * xh, axis=-1, keepdims=True))
    return dx, dsh, dsc, dng


def _iota2(shape, axis):
    return lax.broadcasted_iota(jnp.int32, shape, axis)


def _col(tile, idx):
    return jnp.sum(jnp.where(_iota2(tile.shape, 1) == idx, tile, 0.0), axis=1, keepdims=True)


def _col_slice(tile, idx):
    return tile[:, idx:idx + 1]


def _row(tile, idx):
    return jnp.sum(jnp.where(_iota2(tile.shape, 0) == idx, tile, 0.0), axis=0, keepdims=True)


def _put_col(col, idx, width=HD):
    shape = (col.shape[0], width)
    return jnp.where(_iota2(shape, 1) == idx, jnp.broadcast_to(col, shape), 0.0)


def _tri_inverse(m):
    rows, cols = _iota2(m.shape, m.ndim - 2), _iota2(m.shape, m.ndim - 1)
    mm = _bmm3 if m.ndim == 3 else _d3
    eye = jnp.where(rows == cols, 1.0, 0.0).astype(F32)
    n = jnp.where((rows >> 3) == (cols >> 3), -m, 0.0)
    p = eye + n
    n2 = mm(n, n)
    p = p + mm(n2, p)
    n4 = mm(n2, n2)
    p = p + mm(n4, p)
    for shift in (3, 4, 5, 6):
        same_pair = (rows >> (shift + 1)) == (cols >> (shift + 1))
        below = jnp.logical_and(((rows >> shift) & 1) == 1, ((cols >> shift) & 1) == 0)
        off = jnp.where(jnp.logical_and(same_pair, below), m, 0.0)
        p = p - mm(p, mm(off, p))
    return p


def _cparams(sem=None):
    return pltpu.CompilerParams(dimension_semantics=sem, vmem_limit_bytes=VMEM_LIMIT)


def _my_place():
    return lax.axis_index("x"), lax.axis_index("y"), lax.axis_index("c")


def _hop(xi, yi, hop):
    dx, dy = hop
    return (1 - xi if dx else xi), (1 - yi if dy else yi)


def _pair_exchange(x, name):
    def body(x_ref, o_ref, ssem, rsem):
        xi, yi, ci = _my_place()
        cp = pltpu.make_async_remote_copy(x_ref, o_ref, ssem, rsem, device_id=(xi, yi, 1 - ci), device_id_type=MESH)
        cp.start()
        cp.wait()

    return pl.pallas_call(
        body, name=name, out_shape=jax.ShapeDtypeStruct(x.shape, x.dtype),
        in_specs=[pl.BlockSpec(memory_space=pltpu.VMEM)], out_specs=pl.BlockSpec(memory_space=pltpu.VMEM),
        scratch_shapes=[pltpu.SemaphoreType.DMA, pltpu.SemaphoreType.DMA],
        compiler_params=pltpu.CompilerParams(vmem_limit_bytes=VMEM_LIMIT),
    )(x)


def _allgather_start(x_ref, o_ref, ssems, rsems, lsem):
    xi, yi, ci = _my_place()
    me = 2 * xi + yi
    pltpu.make_async_copy(x_ref, o_ref.at[me], lsem).start()
    for k, hop in enumerate(HOPS):
        tx, ty = _hop(xi, yi, hop)
        pltpu.make_async_remote_copy(x_ref, o_ref.at[me], ssems.at[k], rsems.at[k],
                                     device_id=(tx, ty, ci), device_id_type=MESH).start()


def _allgather_finish(x_ref, o_ref, ssems, rsems, lsem):
    xi, yi, ci = _my_place()
    me = 2 * xi + yi
    for k, hop in enumerate(HOPS):
        tx, ty = _hop(xi, yi, hop)
        cp = pltpu.make_async_remote_copy(x_ref, o_ref.at[2 * tx + ty], ssems.at[k], rsems.at[k],
                                          device_id=(tx, ty, ci), device_id_type=MESH)
        cp.wait_recv()
        cp.wait_send()
    pltpu.make_async_copy(x_ref, o_ref.at[me], lsem).wait()


_ALLGATHER_SEMS = [pltpu.SemaphoreType.DMA((3,)), pltpu.SemaphoreType.DMA((3,)), pltpu.SemaphoreType.DMA]


def _chip_allgather(x, name):
    def body(x_ref, o_ref, ssems, rsems, lsem):
        _allgather_start(x_ref, o_ref, ssems, rsems, lsem)
        _allgather_finish(x_ref, o_ref, ssems, rsems, lsem)

    return pl.pallas_call(
        body, name=name, out_shape=jax.ShapeDtypeStruct((N_CHIPS,) + x.shape, x.dtype),
        in_specs=[pl.BlockSpec(memory_space=pltpu.VMEM)], out_specs=pl.BlockSpec(memory_space=pltpu.VMEM),
        scratch_shapes=_ALLGATHER_SEMS, compiler_params=pltpu.CompilerParams(vmem_limit_bytes=VMEM_LIMIT),
    )(x)


def _hbm_specs(n):
    return [pl.BlockSpec(memory_space=pl.ANY)] * n


def _cast_into_slot(w, l, place):
    _, r, c = w.shape
    tr = _row_tile(r)

    def body(p_ref, w_ref, o_ref):
        o_ref[...] = w_ref[...].astype(BF16)

    return pl.pallas_call(
        body, name=f"cast_slot_{r}x{c}_{l}", out_shape=jax.ShapeDtypeStruct((N_CHIPS, r, c), BF16),
        grid_spec=pltpu.PrefetchScalarGridSpec(
            num_scalar_prefetch=1, grid=(r // tr,),
            in_specs=[pl.BlockSpec((None, tr, c), lambda k, pr: (l, k, 0))],
            out_specs=pl.BlockSpec((None, tr, c), lambda k, pr: (pr[0], k, 0))),
        compiler_params=_cparams(("parallel",)),
    )(place, w)


def _halves(ref, ci):
    half = ref.shape[-2] // 2
    return pl.ds(ci * half, half), pl.ds((1 - ci) * half, half)


def _gather_copies(bufs, sems):
    s_ici, r_ici, s_d2d, r_d2d = sems
    xi, yi, ci = _my_place()
    me = 2 * xi + yi
    ici_send, ici_recv, d2d_send, d2d_recv = [], [], [], []
    for i, buf in enumerate(bufs):
        mine, sibs = _halves(buf, ci)
        for k, hop in enumerate(HOPS):
            tx, ty = _hop(xi, yi, hop)
            src = 2 * tx + ty
            ici_send.append(pltpu.make_async_remote_copy(buf.at[me, mine], buf.at[me, mine], s_ici.at[i, k],
                                                         r_ici.at[i, k], device_id=(tx, ty, ci), device_id_type=MESH))
            ici_recv.append(pltpu.make_async_remote_copy(buf.at[src, mine], buf.at[src, mine], s_ici.at[i, k],
                                                         r_ici.at[i, k], device_id=(tx, ty, ci), device_id_type=MESH))
            d2d_send.append(pltpu.make_async_remote_copy(buf.at[src, mine], buf.at[src, mine], s_d2d.at[i, k],
                                                         r_d2d.at[i, k], device_id=(xi, yi, 1 - ci), device_id_type=MESH))
            d2d_recv.append(pltpu.make_async_remote_copy(buf.at[src, sibs], buf.at[src, sibs], s_d2d.at[i, k],
                                                         r_d2d.at[i, k], device_id=(xi, yi, 1 - ci), device_id_type=MESH))
    return ici_send, ici_recv, d2d_send, d2d_recv


def _gather_start(bufs, sems):
    for cp in _gather_copies(bufs, sems)[0]:
        cp.start()


def _gather_forward(bufs, sems):
    _, ici_recv, d2d_send, _ = _gather_copies(bufs, sems)
    for arrived, forward in zip(ici_recv, d2d_send):
        arrived.wait_recv()
        forward.start()


def _gather_finish(bufs, sems):
    ici_send, _, d2d_send, d2d_recv = _gather_copies(bufs, sems)
    for cp in d2d_recv:
        cp.wait_recv()
    for cp in ici_send + d2d_send:
        cp.wait_send()


def _gather_sems(n):
    return [pltpu.SemaphoreType.DMA((n, 3))] * 4


def _weights_allgather(bufs, l):
    n = len(bufs)

    def body(*refs):
        outs, sems = refs[n:2 * n], refs[2 * n:]
        _gather_start(outs, sems)
        _gather_forward(outs, sems)
        _gather_finish(outs, sems)

    return pl.pallas_call(
        body, name=f"weights_allgather_{l}",
        out_shape=[jax.ShapeDtypeStruct(b.shape, b.dtype) for b in bufs],
        in_specs=_hbm_specs(n), out_specs=_hbm_specs(n), input_output_aliases={i: i for i in range(n)},
        scratch_shapes=_gather_sems(n),
    )(*bufs)


def _pair_send_copies(gs, outs, sems):
    ssem, rsem = sems
    xi, yi, ci = _my_place()
    every = pl.ds(0, N_CHIPS)
    return [pltpu.make_async_remote_copy(g.at[every, _halves(g, ci)[1]], o, ssem.at[i], rsem.at[i],
                                         device_id=(xi, yi, 1 - ci), device_id_type=MESH)
            for i, (g, o) in enumerate(zip(gs, outs))]


def _pair_send_shapes(gs):
    return [jax.ShapeDtypeStruct((N_CHIPS, g.shape[1] // 2, g.shape[2]), g.dtype) for g in gs]


def _pair_send_sems(n):
    return [pltpu.SemaphoreType.DMA((n,)), pltpu.SemaphoreType.DMA((n,))]


def _grads_pair_send(gs, l):
    n = len(gs)

    def body(*refs):
        cps = _pair_send_copies(refs[:n], refs[n:2 * n], refs[2 * n:])
        for cp in cps:
            cp.start()
        for cp in cps:
            cp.wait()

    return pl.pallas_call(
        body, name=f"grads_pair_send_{l}", out_shape=_pair_send_shapes(gs),
        in_specs=_hbm_specs(n), out_specs=_hbm_specs(n), scratch_shapes=_pair_send_sems(n),
    )(*gs)


def _exchange_copies(ps, recvs, sems):
    ssems, rsems = sems
    xi, yi, ci = _my_place()
    cps = []
    for i, (p, rc) in enumerate(zip(ps, recvs)):
        for k, hop in enumerate(HOPS):
            tx, ty = _hop(xi, yi, hop)
            cps.append(pltpu.make_async_remote_copy(p.at[2 * tx + ty], rc.at[k], ssems.at[i, k], rsems.at[i, k],
                                                    device_id=(tx, ty, ci), device_id_type=MESH))
    return cps


def _exchange_sems(n):
    return [pltpu.SemaphoreType.DMA((n, 3))] * 2


def _exchange_shapes(ps):
    return [jax.ShapeDtypeStruct((3,) + p.shape[1:], p.dtype) for p in ps]


def _pair_share_copies(bufs, sems):
    ssem, rsem = sems
    xi, yi, ci = _my_place()
    every = pl.ds(0, DEPTH)
    sends, arrivals = [], []
    for i, buf in enumerate(bufs):
        mine, sibs = _halves(buf, ci)
        sends.append(pltpu.make_async_remote_copy(buf.at[every, mine], buf.at[every, mine], ssem.at[i], rsem.at[i],
                                                  device_id=(xi, yi, 1 - ci), device_id_type=MESH))
        arrivals.append(pltpu.make_async_remote_copy(buf.at[every, sibs], buf.at[every, sibs], ssem.at[i], rsem.at[i],
                                                     device_id=(xi, yi, 1 - ci), device_id_type=MESH))
    return sends, arrivals


def _pair_share_start(bufs, sems):
    for cp in _pair_share_copies(bufs, sems)[0]:
        cp.start()


def _pair_share_finish(bufs, sems):
    sends, arrivals = _pair_share_copies(bufs, sems)
    for cp in arrivals:
        cp.wait_recv()
    for cp in sends:
        cp.wait_send()


def _grads_pair_share(gs, name):
    n = len(gs)

    def body(*refs):
        _pair_share_start(refs[n:2 * n], refs[2 * n:])
        _pair_share_finish(refs[n:2 * n], refs[2 * n:])

    return pl.pallas_call(
        body, name=name,
        out_shape=[jax.ShapeDtypeStruct(g.shape, g.dtype) for g in gs],
        in_specs=_hbm_specs(n), out_specs=_hbm_specs(n), input_output_aliases={i: i for i in range(n)},
        scratch_shapes=_pair_send_sems(n),
    )(*gs)


def _row_tile(r):
    return min(r, 512)


def _pair_sum(g, ga, place, name):
    _, r, c = g.shape
    tr = _row_tile(r // 2)
    nk = r // 2 // tr

    def body(p_ref, g_ref, ga_ref, o_ref):
        o_ref[...] = (g_ref[...].astype(F32) + ga_ref[...].astype(F32)).astype(o_ref.dtype)

    return pl.pallas_call(
        body, name=name, out_shape=jax.ShapeDtypeStruct(ga.shape, ga.dtype),
        grid_spec=pltpu.PrefetchScalarGridSpec(
            num_scalar_prefetch=1, grid=(N_CHIPS, nk),
            in_specs=[pl.BlockSpec((None, tr, c), lambda j, k, pr: (j, pr[1] * nk + k, 0)),
                      pl.BlockSpec((None, tr, c), lambda j, k, pr: (j, k, 0))],
            out_specs=pl.BlockSpec((None, tr, c), lambda j, k, pr: (j, k, 0))),
        compiler_params=_cparams(("parallel", "parallel")),
    )(place, g, ga)


def _chip_sum(pair, recv, buf, l, place, name):
    _, rh, c = pair.shape
    tr = _row_tile(rh)
    nk = rh // tr

    def body(p_ref, own_ref, r_ref, *rest):
        o_ref = rest[-1]
        acc = own_ref[...].astype(F32) + r_ref[0].astype(F32)
        acc = acc + r_ref[1].astype(F32)
        o_ref[...] = acc + r_ref[2].astype(F32)

    in_specs = [pl.BlockSpec((None, tr, c), lambda k, pr: (pr[0], k, 0)),
                pl.BlockSpec((3, tr, c), lambda k, pr: (0, k, 0))]
    args = [pair, recv]
    aliases = {}
    if buf is not None:
        in_specs.append(pl.BlockSpec(memory_space=pl.ANY))
        args.append(buf)
        aliases = {3: 0}
    return pl.pallas_call(
        body, name=name, out_shape=jax.ShapeDtypeStruct((DEPTH, 2 * rh, c), F32),
        grid_spec=pltpu.PrefetchScalarGridSpec(
            num_scalar_prefetch=1, grid=(nk,), in_specs=in_specs,
            out_specs=pl.BlockSpec((None, tr, c), lambda k, pr: (l, pr[1] * nk + k, 0))),
        input_output_aliases=aliases, compiler_params=_cparams(("parallel",)),
    )(place, *args)


def _adam_math(w, g, m, v):
    m = B1 * m + (1.0 - B1) * g
    v = B2 * v + (1.0 - B2) * (g * g)
    m_hat = m / (1.0 - B1 ** STEP)
    v_hat = v / (1.0 - B2 ** STEP)
    delta = -LR * (m_hat / (jnp.sqrt(v_hat) + ADAM_EPS) + WD * w)
    return delta, m, v


def _adamw(w, g, m, v, name):
    n_l, r, c = w.shape
    tr = _row_tile(r)

    def body(w_ref, g_ref, m_ref, v_ref, d_ref, nm_ref, nv_ref):
        d_ref[...], nm_ref[...], nv_ref[...] = _adam_math(w_ref[...], g_ref[...], m_ref[...], v_ref[...])

    spec = pl.BlockSpec((None, tr, c), lambda i, k: (i, k, 0))
    return pl.pallas_call(
        body, name=name, out_shape=[jax.ShapeDtypeStruct(w.shape, F32)] * 3, grid=(n_l, r // tr),
        in_specs=[spec] * 4, out_specs=[spec] * 3, compiler_params=_cparams(("parallel", "parallel")),
    )(w, g, m, v)


def _ada_forward(c_all, w_ada, b_cols):
    cols = w_ada.shape[2]
    tn = 512

    def body(c_ref, w_ref, b_ref, o_ref):
        o_ref[...] = _dotf(_silu(c_ref[...]), w_ref[...]) + b_ref[...]

    return pl.pallas_call(
        body, name="ada_forward", out_shape=jax.ShapeDtypeStruct((DEPTH, 8, cols), F32), grid=(DEPTH, cols // tn),
        in_specs=[pl.BlockSpec((8, D), lambda l, j: (0, 0)),
                  pl.BlockSpec((None, D, tn), lambda l, j: (l, 0, j)),
                  pl.BlockSpec((None, 1, tn), lambda l, j: (l, 0, j))],
        out_specs=pl.BlockSpec((None, 8, tn), lambda l, j: (l, 0, j)),
        compiler_params=_cparams(("parallel", "parallel")),
    )(c_all, w_ada, b_cols.reshape(DEPTH, 1, cols))


def _ada_backward_adamw(c_all, dmod_cols, w, m, v):
    cols = w.shape[2]
    tn = 512

    def body(c_ref, d_ref, w_ref, m_ref, v_ref, g_ref, dl_ref, nm_ref, nv_ref):
        g = _dotf_tn(_silu(c_ref[...]), d_ref[...])
        g_ref[...] = g
        dl_ref[...], nm_ref[...], nv_ref[...] = _adam_math(w_ref[...], g, m_ref[...], v_ref[...])

    wspec = pl.BlockSpec((None, D, tn), lambda l, j: (l, 0, j))
    return pl.pallas_call(
        body, name="ada_backward_adamw", out_shape=[jax.ShapeDtypeStruct(w.shape, F32)] * 4, grid=(DEPTH, cols // tn),
        in_specs=[pl.BlockSpec((8, D), lambda l, j: (0, 0)), pl.BlockSpec((None, 8, tn), lambda l, j: (l, 0, j)),
                  wspec, wspec, wspec],
        out_specs=[wspec] * 4, compiler_params=_cparams(("parallel", "parallel")),
    )(c_all, dmod_cols, w, m, v)


def _tok_tile(t):
    return min(t, 512)


def _wspec4(r, c, l):
    return pl.BlockSpec((N_CHIPS, r, c), lambda i: (0, 0, 0))


def _fwd_in(x, modv, w_in, l):
    t = x.shape[0]
    tm = _tok_tile(t)

    def body(x_ref, mod_ref, w_ref, o_ref, h_ref):
        h = _norm_mod(x_ref[...], mod_ref[6:7, :], mod_ref[1:2, :], mod_ref[0:1, :]).astype(BF16)
        h_ref[...] = h
        o_ref[...] = jnp.dot(h, w_ref[...], preferred_element_type=F32)

    return pl.pallas_call(
        body, name=f"fwd_in_{l}", grid=(t // tm,),
        out_shape=[jax.ShapeDtypeStruct((t, NW), F32), jax.ShapeDtypeStruct((t, D), BF16)],
        in_specs=[pl.BlockSpec((tm, D), lambda i: (i, 0)), pl.BlockSpec((None, 8, D), lambda i: (l, 0, 0)),
                  pl.BlockSpec((D, NW), lambda i: (0, 0))],
        out_specs=[pl.BlockSpec((tm, NW), lambda i: (i, 0)), pl.BlockSpec((tm, D), lambda i: (i, 0))],
        compiler_params=_cparams(("parallel",)),
    )(x, modv, w_in)


def _fwd_out_ff1(x, mix, modv, w_out, w_ff1, l):
    t = x.shape[0]
    tm = _tok_tile(t)

    def body(x_ref, mix_ref, mod_ref, wo_ref, w_ref, x1_ref, o_ref, h_ref):
        x1 = x_ref[...] + mod_ref[2:3, :] * jnp.dot(mix_ref[...], wo_ref[...].reshape(D, D), preferred_element_type=F32)
        x1_ref[...] = x1
        h = _norm_mod(x1, mod_ref[7:8, :], mod_ref[4:5, :], mod_ref[3:4, :]).astype(BF16)
        h_ref[...] = h
        for j in range(N_CHIPS):
            f = jnp.dot(h, w_ref[j], preferred_element_type=F32)
            o_ref[:, j * D:(j + 1) * D] = jnp.maximum(f, 0.0).astype(BF16)

    tok = pl.BlockSpec((tm, D), lambda i: (i, 0))
    return pl.pallas_call(
        body, name=f"fwd_out_ff1_{l}", grid=(t // tm,),
        out_shape=[jax.ShapeDtypeStruct((t, D), F32), jax.ShapeDtypeStruct((t, DFF), BF16),
                   jax.ShapeDtypeStruct((t, D), BF16)],
        in_specs=[tok, tok, pl.BlockSpec((None, 8, D), lambda i: (l, 0, 0)), _wspec4(D // N_CHIPS, D, l),
                  _wspec4(D, D, l)],
        out_specs=[tok, pl.BlockSpec((tm, DFF), lambda i: (i, 0)), tok],
        compiler_params=_cparams(("parallel",)),
    )(x, mix, modv, w_out, w_ff1)


def _fwd_ff2(x, r, modv, w_ff2, l):
    t = x.shape[0]
    tm = _tok_tile(t)

    def body(x_ref, r_ref, mod_ref, w_ref, o_ref):
        acc = jnp.zeros((tm, D), F32)
        for j in range(N_CHIPS):
            rj = r_ref[:, j * D:(j + 1) * D].astype(F32)
            acc = acc + jnp.dot((rj * rj).astype(BF16), w_ref[j], preferred_element_type=F32)
        o_ref[...] = x_ref[...] + mod_ref[5:6, :] * acc

    return pl.pallas_call(
        body, name=f"fwd_ff2_{l}", out_shape=jax.ShapeDtypeStruct((t, D), F32), grid=(t // tm,),
        in_specs=[pl.BlockSpec((tm, D), lambda i: (i, 0)), pl.BlockSpec((tm, DFF), lambda i: (i, 0)),
                  pl.BlockSpec((None, 8, D), lambda i: (l, 0, 0)), _wspec4(D, D, l)],
        out_specs=pl.BlockSpec((tm, D), lambda i: (i, 0)), compiler_params=_cparams(("parallel",)),
    )(x, r, modv, w_ff2)


def _loss_head(x, target, final_g):
    t = x.shape[0]
    tm = _tok_tile(t)

    def body(x_ref, t_ref, g_ref, dx_ref, st_ref):
        @pl.when(pl.program_id(0) == 0)
        def _():
            st_ref[...] = jnp.zeros_like(st_ref)

        xh, rstd = _rms_stats(x_ref[...])
        g = g_ref[...]
        err = xh * g - t_ref[...]
        loss = 0.5 * jnp.sum(jnp.mean(err * err, axis=-1, keepdims=True), axis=0, keepdims=True)
        dy = err * (1.0 / D)
        st_ref[0:1, :] += jnp.sum(dy * xh, axis=0, keepdims=True)
        st_ref[1:2, :] += jnp.broadcast_to(loss, (1, D))
        dxh = dy * g
        dx_ref[...] = rstd * (dxh - xh * jnp.mean(dxh * xh, axis=-1, keepdims=True))

    return pl.pallas_call(
        body, name="loss_head", out_shape=[jax.ShapeDtypeStruct((t, D), F32), jax.ShapeDtypeStruct((8, D), F32)],
        grid=(t // tm,),
        in_specs=[pl.BlockSpec((tm, D), lambda i: (i, 0)), pl.BlockSpec((tm, D), lambda i: (i, 0)),
                  pl.BlockSpec((1, D), lambda i: (0, 0))],
        out_specs=[pl.BlockSpec((tm, D), lambda i: (i, 0)), pl.BlockSpec((8, D), lambda i: (0, 0))],
        compiler_params=_cparams(("arbitrary",)),
    )(x, target, final_g.reshape(1, D))


def _bwd_ff2(dx2, r, modv, w_ff2, l):
    t = dx2.shape[0]
    tm = _tok_tile(t)

    def body(d_ref, r_ref, mod_ref, w_ref, o_ref):
        dyg = (d_ref[...] * mod_ref[5:6, :]).astype(BF16)
        for j in range(N_CHIPS):
            da = lax.dot_general(dyg, w_ref[j], (((1,), (1,)), ((), ())), preferred_element_type=F32)
            o_ref[:, j * D:(j + 1) * D] = (da * 2.0 * r_ref[:, j * D:(j + 1) * D].astype(F32)).astype(BF16)

    return pl.pallas_call(
        body, name=f"bwd_ff2_{l}", out_shape=jax.ShapeDtypeStruct((t, DFF), BF16), grid=(t // tm,),
        in_specs=[pl.BlockSpec((tm, D), lambda i: (i, 0)), pl.BlockSpec((tm, DFF), lambda i: (i, 0)),
                  pl.BlockSpec((None, 8, D), lambda i: (l, 0, 0)), _wspec4(D, D, l)],
        out_specs=pl.BlockSpec((tm, DFF), lambda i: (i, 0)), compiler_params=_cparams(("parallel",)),
    )(dx2, r, modv, w_ff2)


def _bwd_norm(dy, w, x, dres, modv, l, which, send=(), w_out=None, share=(), exchange=()):
    t = x.shape[0]
    tm = _tok_tile(t)
    nsteps = t // tm
    rows = (6, 1) if which == "in" else (7, 4)
    width = dy.shape[1]
    ns, nh, ne = len(send), len(share), len(exchange)
    nb_ = 0 if w_out is None else 1
    n_in = 5 + nb_ + ns + nh + ne

    def body(*refs):
        dy_ref, w_ref, x_ref, dr_ref, mod_ref = refs[:5]
        wo_ref = refs[5] if nb_ else None
        parts = refs[5 + nb_:5 + nb_ + ns]
        pairs = refs[5 + nb_ + ns + nh:n_in]
        dx_ref, st_ref = refs[n_in:n_in + 2]
        dmix_ref = refs[n_in + 2] if nb_ else None
        o0 = n_in + 2 + nb_
        from_sib, shared, recvs = refs[o0:o0 + ns], refs[o0 + ns:o0 + ns + nh], refs[o0 + ns + nh:o0 + ns + nh + ne]
        scratch = list(refs[o0 + ns + nh + ne:])
        sems = [scratch.pop(0) for _ in range(2 if ns else 0)]
        hsems = [scratch.pop(0) for _ in range(2 if nh else 0)]
        esems = scratch

        @pl.when(pl.program_id(0) == 0)
        def _():
            st_ref[...] = jnp.zeros_like(st_ref)
            if ns:
                for cp in _pair_send_copies(parts, from_sib, sems):
                    cp.start()
            if nh:
                _pair_share_start(shared, hsems)
            if ne:
                for cp in _exchange_copies(pairs, recvs, esems):
                    cp.start()

        if which == "in":
            dh = lax.dot_general(dy_ref[...], w_ref[...], (((1,), (1,)), ((), ())), preferred_element_type=F32)
        else:
            dh = jnp.zeros((tm, D), F32)
            for j in range(N_CHIPS):
                dh = dh + lax.dot_general(dy_ref[:, j * D:(j + 1) * D], w_ref[j], (((1,), (1,)), ((), ())),
                                          preferred_element_type=F32)
        ng, sc = mod_ref[rows[0]:rows[0] + 1, :], mod_ref[rows[1]:rows[1] + 1, :]
        dx, dsh, dsc, dng = _norm_mod_bwd(dh, x_ref[...], ng, sc)
        dx_new = dr_ref[...] + dx
        dx_ref[...] = dx_new
        st_ref[0:1, :] += dsh
        st_ref[1:2, :] += dsc
        st_ref[2:3, :] += dng
        if nb_:
            dyg = (dx_new * mod_ref[2:3, :]).astype(BF16)
            dmix_ref[...] = lax.dot_general(dyg, wo_ref[...].reshape(D, D), (((1,), (1,)), ((), ())),
                                            preferred_element_type=F32).astype(BF16)

        if ns:
            @pl.when(pl.program_id(0) == nsteps - 1)
            def _():
                for cp in _pair_send_copies(parts, from_sib, sems):
                    cp.wait()

        if nh:
            @pl.when(pl.program_id(0) == nsteps - 1)
            def _():
                _pair_share_finish(shared, hsems)

        if ne:
            @pl.when(pl.program_id(0) == nsteps - 1)
            def _():
                for cp in _exchange_copies(pairs, recvs, esems):
                    cp.wait()

    tok = pl.BlockSpec((tm, D), lambda i: (i, 0))
    wspec = pl.BlockSpec((D, NW), lambda i: (0, 0)) if which == "in" else _wspec4(D, D, l)
    return pl.pallas_call(
        body, name=f"bwd_norm_{which}_{l}",
        out_shape=[jax.ShapeDtypeStruct((t, D), F32), jax.ShapeDtypeStruct((8, D), F32)]
        + [jax.ShapeDtypeStruct((t, D), BF16)] * nb_ + _pair_send_shapes(send)
        + [jax.ShapeDtypeStruct(g.shape, g.dtype) for g in share] + _exchange_shapes(exchange),
        grid=(nsteps,),
        in_specs=[pl.BlockSpec((tm, width), lambda i: (i, 0)), wspec, tok, tok,
                  pl.BlockSpec((None, 8, D), lambda i: (l, 0, 0))]
        + [_wspec4(D // N_CHIPS, D, l)] * nb_ + _hbm_specs(ns + nh + ne),
        out_specs=[tok, pl.BlockSpec((8, D), lambda i: (0, 0))] + [tok] * nb_ + _hbm_specs(ns + nh + ne),
        input_output_aliases={5 + nb_ + ns + i: 2 + nb_ + ns + i for i in range(nh)},
        scratch_shapes=(_pair_send_sems(ns) if ns else []) + (_pair_send_sems(nh) if nh else [])
        + (_exchange_sems(ne) if ne else []),
        compiler_params=_cparams(("arbitrary",)),
    )(dy, w, x, dres, modv, *([w_out] * nb_), *send, *share, *exchange)


def _grad_weight(lhs, rhs, modv, l, which, w_gate=None):
    t = lhs.shape[0]
    tm = min(t, 2048)
    nt = t // tm
    gated = which in ("out", "ff2")
    if which == "in":
        nj, lw, rw, orows, ocols = 5, D, NW // 5, D, NW // 5
    elif which == "ff1":
        nj, lw, rw, orows, ocols = N_CHIPS, D, D, D, D
    elif which == "out":
        nj, lw, rw, orows, ocols = 1, D, D, D, D
    else:
        nj, lw, rw, orows, ocols = N_CHIPS, D, D, D, D
    gate_row = 2 if which == "out" else 5

    def body(*refs):
        if gated:
            l_ref, r_ref, mod_ref, wg_ref, o_ref, dg_ref, acc = refs
        else:
            l_ref, r_ref, mod_ref, o_ref, acc = refs
        j, k = pl.program_id(0), pl.program_id(1)

        @pl.when(k == 0)
        def _():
            acc[...] = jnp.zeros_like(acc)

        if which == "ff2":
            lv = l_ref[...].astype(F32)
            lv = lv * lv
        else:
            lv = l_ref[...]
        acc[...] += _dot_tn(lv, r_ref[...])

        if gated:
            @pl.when(jnp.logical_and(j == 0, k == 0))
            def _():
                dg_ref[...] = jnp.zeros_like(dg_ref)

        @pl.when(k == nt - 1)
        def _():
            raw = acc[...]
            if gated:
                o_ref[...] = (raw * mod_ref[gate_row:gate_row + 1, :]).astype(o_ref.dtype)
                dg_ref[0:1, :] += jnp.sum(raw * wg_ref[...].astype(F32), axis=0, keepdims=True)
            else:
                o_ref[...] = raw.astype(o_ref.dtype)

    if which in ("in", "ff1"):
        lspec = pl.BlockSpec((tm, lw), lambda j, k: (k, 0))
        rspec = pl.BlockSpec((tm, rw), lambda j, k: (k, j))
    else:
        lspec = pl.BlockSpec((tm, lw), lambda j, k: (k, j))
        rspec = pl.BlockSpec((tm, rw), lambda j, k: (k, 0))
    mspec = pl.BlockSpec((None, 8, D), lambda j, k: (l, 0, 0))
    flat = which in ("in", "out")
    if flat:
        ospec = pl.BlockSpec((orows, ocols), lambda j, k: (0, j))
        out_shape = [jax.ShapeDtypeStruct((D, nj * ocols), BF16)]
    else:
        ospec = pl.BlockSpec((None, orows, ocols), lambda j, k: (j, 0, 0))
        out_shape = [jax.ShapeDtypeStruct((N_CHIPS, orows, ocols), BF16)]
    in_specs = [lspec, rspec, mspec]
    args = [lhs, rhs, modv]
    out_specs = [ospec]
    if gated:
        if flat:
            in_specs.append(pl.BlockSpec((orows, ocols), lambda j, k: (0, 0)))
            args.append(w_gate.reshape(D, D))
        else:
            in_specs.append(pl.BlockSpec((None, orows, ocols), lambda j, k: (j, 0, 0)))
            args.append(w_gate)
        out_specs.append(pl.BlockSpec((8, D), lambda j, k: (0, 0)))
        out_shape.append(jax.ShapeDtypeStruct((8, D), F32))
    res = pl.pallas_call(
        body, name=f"grad_w_{which}_{l}", out_shape=out_shape, grid=(nj, nt), in_specs=in_specs, out_specs=out_specs,
        scratch_shapes=[pltpu.VMEM((orows, ocols), F32)], compiler_params=_cparams(("arbitrary", "arbitrary")),
    )(*args)
    return (res[0], res[1]) if gated else (res[0], None)


def _tri_masks():
    rows, cols = _iota2((BLK, BLK), 0), _iota2((BLK, BLK), 1)
    return rows >= cols, rows > cols


def _sgu_forward(p_ref, lnp_ref, sguw_ref, sgub_ref, col=_col):
    incl, _ = _tri_masks()
    ug = _gelu(p_ref[:, 0:512])
    vg = _gelu(p_ref[:, 512:1024])
    mu = jnp.mean(vg, axis=-1, keepdims=True)
    xc = vg - mu
    rstd = lax.rsqrt(jnp.mean(xc * xc, axis=-1, keepdims=True) + LN_EPS)
    vhat = xc * rstd
    vn = vhat * lnp_ref[0:1, :] + lnp_ref[1:2, :]
    bias = sgub_ref[...]
    ys, mixed, wms = [], [], []
    for h in range(HEADS):
        wm = jnp.where(incl, sguw_ref[h], 0.0)
        mx = _dot(wm, vn[:, h * HD:(h + 1) * HD]) + col(bias, h)
        ys.append(ug[:, h * HD:(h + 1) * HD] * mx)
        mixed.append(mx)
        wms.append(wm)
    return ys, ug, vhat, rstd, vn, mixed, wms


def _conv_forward(xbuf, cw_ref):
    conv = cw_ref[0:1, :] * xbuf[5:5 + BLK, :]
    for j in range(1, 4):
        conv = conv + cw_ref[j:j + 1, :] * xbuf[5 + j:5 + j + BLK, :]
    return conv


def _gates(gt, gv_ref):
    incl, _ = _tri_masks()
    beta = _sigmoid(gt)
    neg_a = -jnp.exp(gv_ref[0:1, :])
    gl = neg_a * _softplus(gt + gv_ref[1:2, :])
    gc = _dotf(jnp.where(incl, 1.0, 0.0).astype(F32), gl)
    return beta, gl, gc, gc.T, neg_a


def _head_chunk(act, beta, gc, gct, h, col=_col):
    incl, strict = _tri_masks()
    qh = act[:, h * HD:(h + 1) * HD]
    kh = act[:, 512 + h * HD:512 + (h + 1) * HD]
    vh = act[:, 1024 + h * HD:1024 + (h + 1) * HD]
    rq = lax.rsqrt(jnp.sum(qh * qh, axis=-1, keepdims=True) + RMS_EPS)
    rk = lax.rsqrt(jnp.sum(kh * kh, axis=-1, keepdims=True) + RMS_EPS)
    qhat, khat = qh * rq, kh * rk
    qn = qhat * QK_SCALE
    b = col(beta, h)
    gcol = col(gc, 4 + h)
    grow = _row(gct, 4 + h)
    dmat = jnp.where(incl, jnp.exp(jnp.where(incl, gcol - grow, 0.0)), 0.0)
    gam = jnp.exp(gcol)
    glast = _row(gcol, BLK - 1)
    e = jnp.exp(glast - gcol)
    kk = _d3_nt(khat, khat)
    return dict(qhat=qhat, khat=khat, qn=qn, vh=vh, rq=rq, rk=rk, b=b, dmat=dmat, gam=gam, glast=glast, e=e, kk=kk,
                strict=strict, incl=incl)


def _mixer_forward(p, lnp, sgu_w, sgu_bt, cw, gv, l, gather=()):
    t = p.shape[0]
    nb = t // BLK
    ng = len(gather)

    def body(*refs):
        p_ref, lnp_ref, sguw_ref, sgub_ref, cw_ref, gv_ref = refs[:6]
        mix_ref, s_out, t_out, u_out, w_out, o_out, conv_out = refs[6 + ng:13 + ng]
        gbufs = refs[13 + ng:13 + 2 * ng]
        s_scr, xbuf = refs[13 + 2 * ng:15 + 2 * ng]
        gsems = refs[15 + 2 * ng:]

        @pl.when(pl.program_id(0) == 0)
        def _():
            s_scr[...] = jnp.zeros_like(s_scr)
            xbuf[0:8, :] = jnp.zeros((8, 1536), F32)
            if ng:
                _gather_start(gbufs, gsems)

        ys = _sgu_forward(p_ref, lnp_ref, sguw_ref, sgub_ref, col=_col_slice)[0]
        for h in range(HEADS):
            mix_ref[:, h * HD:(h + 1) * HD] = ys[h].astype(BF16)

        xbuf[8:8 + BLK, :] = p_ref[:, 1024:2560]
        conv = _conv_forward(xbuf, cw_ref)
        conv_out[...] = conv
        act = _silu(conv)
        xbuf[0:8, :] = xbuf[BLK:BLK + 8, :]
        beta, _, gc, gct, _ = _gates(p_ref[:, GATE0:NW], gv_ref)
        chunks = [_head_chunk(act, beta, gc, gct, h, col=_col_slice) for h in range(HEADS)]
        for h, hc in enumerate(chunks):
            t_out[h] = jnp.where(hc["strict"], hc["b"] * hc["kk"] * hc["dmat"], 0.0)
        t_out[...] = _tri_inverse(t_out[...])
        tm_all = t_out[...]
        u_all = _bdot(tm_all, jnp.stack([hc["b"] * hc["vh"] for hc in chunks]))
        w_all = _bdot(tm_all, jnp.stack([(hc["b"] * hc["gam"]) * hc["khat"] for hc in chunks]))
        s_all = s_scr[...]
        s_out[...] = s_all
        wn_all = u_all - _bdot(w_all, s_all)
        qkm_all = jnp.stack([_dot_nt(hc["qn"], hc["khat"]) * hc["dmat"] for hc in chunks])
        o_all = _bdot(jnp.stack([hc["qn"] * hc["gam"] for hc in chunks]), s_all) + _bdot(qkm_all, wn_all)
        upd_all = _bdot(jnp.stack([hc["khat"] * hc["e"] for hc in chunks]), wn_all, ((1,), (1,)))
        for h, hc in enumerate(chunks):
            s_scr[h] = jnp.exp(hc["glast"]) * s_all[h] + upd_all[h]
            sl = slice(h * HD, (h + 1) * HD)
            o = o_all[h]
            u_out[:, sl] = u_all[h]
            w_out[:, sl] = w_all[h]
            o_out[:, sl] = o
            on = o * lax.rsqrt(jnp.mean(o * o, axis=-1, keepdims=True) + RMS_EPS) * gv_ref[2:3, :]
            mix_ref[:, 512 + h * HD:512 + (h + 1) * HD] = (on * _silu(p_ref[:, 2560 + h * HD:2560 + (h + 1) * HD])).astype(BF16)

        if ng:
            @pl.when(pl.program_id(0) == nb - 1 - min(3, nb - 1))
            def _():
                _gather_forward(gbufs, gsems)

            @pl.when(pl.program_id(0) == nb - 1)
            def _():
                _gather_finish(gbufs, gsems)

    tok = lambda w: pl.BlockSpec((BLK, w), lambda i: (i, 0))
    st = pl.BlockSpec((None, HEADS, HD, HD), lambda i: (i, 0, 0, 0))
    return pl.pallas_call(
        body, name=f"mixer_fwd_{l}", grid=(nb,),
        out_shape=[jax.ShapeDtypeStruct((t, D), BF16), jax.ShapeDtypeStruct((nb, HEADS, HD, HD), F32),
                   jax.ShapeDtypeStruct((nb, HEADS, HD, HD), F32), jax.ShapeDtypeStruct((t, 512), F32),
                   jax.ShapeDtypeStruct((t, 512), F32), jax.ShapeDtypeStruct((t, 512), F32),
                   jax.ShapeDtypeStruct((t, 1536), F32)]
        + [jax.ShapeDtypeStruct(b.shape, b.dtype) for b in gather],
        in_specs=[tok(NW), pl.BlockSpec((None, 8, 512), lambda i: (l, 0, 0)),
                  pl.BlockSpec((None, HEADS, HD, HD), lambda i: (l, 0, 0, 0)),
                  pl.BlockSpec((None, HD, HD), lambda i: (l, 0, 0)), pl.BlockSpec((None, 8, 1536), lambda i: (l, 0, 0)),
                  pl.BlockSpec((None, 8, HD), lambda i: (l, 0, 0))] + _hbm_specs(ng),
        out_specs=[tok(D), st, st, tok(512), tok(512), tok(512), tok(1536)]
        + _hbm_specs(ng),
        input_output_aliases={6 + i: 7 + i for i in range(ng)},
        scratch_shapes=[pltpu.VMEM((HEADS, HD, HD), F32), pltpu.VMEM((BLK + 8, 1536), F32)]
        + (_gather_sems(ng) if ng else []),
        compiler_params=_cparams(("arbitrary",)),
    )(p, lnp, sgu_w, sgu_bt, cw, gv, *gather)


def _mixer_backward(p, dmix, saved, lnp, sgu_w, sgu_bt, cw, gv, l, exchange=()):
    t = p.shape[0]
    nb = t // BLK
    s_sv, t_sv, u_sv, w_sv, o_sv, conv_sv = saved
    ne = len(exchange)

    def body(*refs):
        (p_ref, dmix_ref, s_ref, t_ref, u_ref, w_ref, o_ref, conv_ref, lnp_ref, sguw_ref, sgub_ref, cw_ref,
         gv_ref) = refs[:13]
        pairs = refs[13:13 + ne]
        dp_ref, dlnp_ref, dsguw_ref, dsgub_ref, dcw_ref, dgv_ref = refs[13 + ne:19 + ne]
        recvs = refs[19 + ne:19 + 2 * ne]
        ds_scr, dcbuf = refs[19 + 2 * ne:21 + 2 * ne]
        esems = refs[21 + 2 * ne:]

        @pl.when(pl.program_id(0) == 0)
        def _():
            if ne:
                for cp in _exchange_copies(pairs, recvs, esems):
                    cp.start()
            ds_scr[...] = jnp.zeros_like(ds_scr)
            dcbuf[BLK:BLK + 8, :] = jnp.zeros((8, 1536), F32)
            dlnp_ref[...] = jnp.zeros_like(dlnp_ref)
            dsguw_ref[...] = jnp.zeros_like(dsguw_ref)
            dsgub_ref[...] = jnp.zeros_like(dsgub_ref)
            dcw_ref[...] = jnp.zeros_like(dcw_ref)
            dgv_ref[...] = jnp.zeros_like(dgv_ref)

        incl, strict = _tri_masks()
        _, ug, vhat, rstd, vn, mixed, wms = _sgu_forward(p_ref, lnp_ref, sguw_ref, sgub_ref)
        dvn_parts, dug_parts = [], []
        dbias = jnp.zeros((BLK, HD), F32)
        for h in range(HEADS):
            sl = slice(h * HD, (h + 1) * HD)
            dy = dmix_ref[:, sl].astype(F32)
            dmx = dy * ug[:, sl]
            dug_parts.append(dy * mixed[h])
            dsguw_ref[h] += jnp.where(incl, _dot_nt(dmx, vn[:, sl]), 0.0)
            dbias = dbias + _put_col(jnp.sum(dmx, axis=1, keepdims=True), h)
            dvn_parts.append(_dot_tn(wms[h], dmx))
        dsgub_ref[...] += dbias
        dvn = jnp.concatenate(dvn_parts, axis=1)
        dug = jnp.concatenate(dug_parts, axis=1)
        dlnp_ref[0:1, :] += jnp.sum(dvn * vhat, axis=0, keepdims=True)
        dlnp_ref[1:2, :] += jnp.sum(dvn, axis=0, keepdims=True)
        dvhat = dvn * lnp_ref[0:1, :]
        dvg = rstd * (dvhat - jnp.mean(dvhat, axis=-1, keepdims=True)
                      - vhat * jnp.mean(dvhat * vhat, axis=-1, keepdims=True))
        dp_ref[:, 0:512] = (dug * _gelu_grad(p_ref[:, 0:512])).astype(BF16)
        dp_ref[:, 512:1024] = (dvg * _gelu_grad(p_ref[:, 512:1024])).astype(BF16)

        conv = conv_ref[...]
        act = _silu(conv)
        gt = p_ref[:, GATE0:NW]
        beta, gl, gc, gct, neg_a = _gates(gt, gv_ref)
        gng = gv_ref[2:3, :]
        dbeta_t = jnp.zeros((BLK, HD), F32)
        dgc_t = jnp.zeros((BLK, HD), F32)
        dgng = jnp.zeros((1, HD), F32)
        chunks, dos = [], []
        for h in range(HEADS):
            sl = slice(h * HD, (h + 1) * HD)
            chunks.append(_head_chunk(act, beta, gc, gct, h))
            o = o_ref[:, sl]
            z = p_ref[:, 2560 + h * HD:2560 + (h + 1) * HD]
            dy = dmix_ref[:, 512 + h * HD:512 + (h + 1) * HD].astype(F32)
            ro = lax.rsqrt(jnp.mean(o * o, axis=-1, keepdims=True) + RMS_EPS)
            ohat = o * ro
            dp_ref[:, 2560 + h * HD:2560 + (h + 1) * HD] = (dy * ohat * gng * _silu_grad(z)).astype(BF16)
            don = dy * _silu(z)
            dgng = dgng + jnp.sum(don * ohat, axis=0, keepdims=True)
            dohat = don * gng
            dos.append(ro * (dohat - ohat * jnp.mean(dohat * ohat, axis=-1, keepdims=True)))
        stack = lambda f: jnp.stack([f(hc) for hc in chunks])
        nt_, tn_ = ((2,), (2,)), ((1,), (1,))
        qn_all, khat_all = stack(lambda hc: hc["qn"]), stack(lambda hc: hc["khat"])
        dmat_all = stack(lambda hc: hc["dmat"])
        do_all, s_all, tm_all, dsn_all = jnp.stack(dos), s_ref[...], t_ref[...], ds_scr[...]
        u_all = jnp.stack([u_ref[:, h * HD:(h + 1) * HD] for h in range(HEADS)])
        w_all = jnp.stack([w_ref[:, h * HD:(h + 1) * HD] for h in range(HEADS)])
        qk_raw_all = _bdot(qn_all, khat_all, nt_)
        qkm_all = qk_raw_all * dmat_all
        qd_all = stack(lambda hc: hc["qn"] * hc["gam"])
        kd_all = stack(lambda hc: hc["khat"] * hc["e"])
        wn_all = u_all - _bdot(w_all, s_all)
        dwn_all = _bdot(qkm_all, do_all, tn_) + _bdot(kd_all, dsn_all)
        dqd_all = _bdot(do_all, s_all, nt_)
        dqkm_all = jnp.where(incl, _bdot(do_all, wn_all, nt_), 0.0)
        ds_new_all = _bdot(qd_all, do_all, tn_) - _bdot(w_all, dwn_all, tn_)
        dkd_all = _bdot(wn_all, dsn_all, nt_)
        dw_all = -_bdot(dwn_all, s_all, nt_)
        db1_all = _bdot(tm_all, dwn_all, tn_)
        db2_all = _bdot(tm_all, dw_all, tn_)
        dm_all = jnp.where(strict, -(_bdot(db1_all, u_all, nt_) + _bdot(db2_all, w_all, nt_)), 0.0)
        b_all = stack(lambda hc: jnp.broadcast_to(hc["b"], (BLK, HD)))
        dkkm_all = dm_all * b_all * dmat_all
        g_qk_all = dqkm_all * dmat_all
        dqn_mm = _bdot(g_qk_all, khat_all)
        dkhat_mm = _bdot(g_qk_all, qn_all, tn_) + _bdot(dkkm_all, khat_all) + _bdot(dkkm_all, khat_all, tn_)
        for h in range(HEADS):
            hc = chunks[h]
            b, gam, e, dmat, kk = hc["b"], hc["gam"], hc["e"], hc["dmat"], hc["kk"]
            qn, khat, vh = hc["qn"], hc["khat"], hc["vh"]
            gamlast = jnp.exp(hc["glast"])
            s, ds_next = s_all[h], dsn_all[h]
            qk_raw, dqkm, dqd, dkd = qk_raw_all[h], dqkm_all[h], dqd_all[h], dkd_all[h]
            db1, db2, dm = db1_all[h], db2_all[h], dm_all[h]
            kd = kd_all[h]
            ds_scr[h] = ds_new_all[h] + gamlast * ds_next
            dgamlast = jnp.sum(jnp.sum(ds_next * s, axis=1, keepdims=True), axis=0, keepdims=True)
            dbeta = (jnp.sum(dm * kk * dmat, axis=1, keepdims=True) + jnp.sum(db1 * vh, axis=1, keepdims=True)
                     + gam * jnp.sum(db2 * khat, axis=1, keepdims=True))
            ddm = dm * b * kk + dqkm * qk_raw
            dgam = b * jnp.sum(db2 * khat, axis=1, keepdims=True) + jnp.sum(dqd * qn, axis=1, keepdims=True)
            dqn = dqn_mm[h] + dqd * gam
            dkhat = (b * gam) * db2 + dkhat_mm[h] + dkd * e
            dvh = b * db1
            rkd = jnp.sum(dkd * kd, axis=1, keepdims=True)
            emat = ddm * dmat
            dgc = (dgam * gam - rkd + jnp.sum(emat, axis=1, keepdims=True)
                   - jnp.sum(emat.T, axis=1, keepdims=True))
            last = _iota2((BLK, 1), 0) == BLK - 1
            dgc = dgc + jnp.where(last, jnp.sum(rkd, axis=0, keepdims=True) + dgamlast * gamlast, 0.0)
            dgc_t = dgc_t + _put_col(dgc, 4 + h)
            dbeta_t = dbeta_t + _put_col(dbeta, h)
            dqhat = dqn * QK_SCALE
            dq = hc["rq"] * (dqhat - hc["qhat"] * jnp.sum(dqhat * hc["qhat"], axis=-1, keepdims=True))
            dk = hc["rk"] * (dkhat - khat * jnp.sum(dkhat * khat, axis=-1, keepdims=True))
            dcbuf[0:BLK, h * HD:(h + 1) * HD] = dq
            dcbuf[0:BLK, 512 + h * HD:512 + (h + 1) * HD] = dk
            dcbuf[0:BLK, 1024 + h * HD:1024 + (h + 1) * HD] = dvh
        dgv_ref[2:3, :] += dgng
        dgl = _dotf_tn(jnp.where(incl, 1.0, 0.0).astype(F32), dgc_t)
        sig_a = _sigmoid(gt + gv_ref[1:2, :])
        d_araw = dgl * neg_a * sig_a
        dgv_ref[0:1, :] += jnp.sum(dgl * gl, axis=0, keepdims=True)
        dgv_ref[1:2, :] += jnp.sum(d_araw, axis=0, keepdims=True)
        dp_ref[:, GATE0:NW] = (dbeta_t * beta * (1.0 - beta) + d_araw).astype(BF16)
        dcbuf[0:BLK, :] = dcbuf[0:BLK, :] * _silu_grad(conv)
        xcur = p_ref[:, 1024:2560]
        dqkv = jnp.zeros((BLK, 1536), F32)
        for j in range(4):
            shifted = dcbuf[3 - j:3 - j + BLK, :]
            dqkv = dqkv + cw_ref[j:j + 1, :] * shifted
            dcw_ref[j:j + 1, :] += jnp.sum(shifted * xcur, axis=0, keepdims=True)
        dp_ref[:, 1024:2560] = dqkv.astype(BF16)
        dcbuf[BLK:BLK + 8, :] = dcbuf[0:8, :]

        if ne:
            @pl.when(pl.program_id(0) == nb - 1)
            def _():
                for cp in _exchange_copies(pairs, recvs, esems):
                    cp.wait()

    rev = lambda w: pl.BlockSpec((BLK, w), lambda i: (nb - 1 - i, 0))
    st = pl.BlockSpec((None, HEADS, HD, HD), lambda i: (nb - 1 - i, 0, 0, 0))
    fix = lambda *shape: pl.BlockSpec((None,) + shape, lambda i: (l,) + (0,) * len(shape))
    acc = lambda *shape: pl.BlockSpec(shape, lambda i: (0,) * len(shape))
    return pl.pallas_call(
        body, name=f"mixer_bwd_{l}", grid=(nb,),
        out_shape=[jax.ShapeDtypeStruct((t, NW), BF16), jax.ShapeDtypeStruct((8, 512), F32),
                   jax.ShapeDtypeStruct((HEADS, HD, HD), F32), jax.ShapeDtypeStruct((HD, HD), F32),
                   jax.ShapeDtypeStruct((8, 1536), F32), jax.ShapeDtypeStruct((8, HD), F32)]
        + _exchange_shapes(exchange),
        in_specs=[rev(NW), rev(D), st, st, rev(512), rev(512), rev(512),
                  rev(1536),
                  fix(8, 512), fix(HEADS, HD, HD), fix(HD, HD), fix(8, 1536), fix(8, HD)] + _hbm_specs(ne),
        out_specs=[rev(NW), acc(8, 512), acc(HEADS, HD, HD), acc(HD, HD), acc(8, 1536), acc(8, HD)]
        + _hbm_specs(ne),
        scratch_shapes=[pltpu.VMEM((HEADS, HD, HD), F32), pltpu.VMEM((BLK + 8, 1536), F32)]
        + (_exchange_sems(ne) if ne else []),
        compiler_params=_cparams(("arbitrary",)),
    )(p, dmix, s_sv, t_sv, u_sv, w_sv, o_sv, conv_sv, lnp, sgu_w, sgu_bt, cw, gv, *exchange)


_SMALL = (("b_ada", 24), ("norm1_g", 8), ("norm2_g", 8), ("final_g", 8), ("sgu_ln_g", 8), ("sgu_ln_b", 8),
          ("sgu_w", 256), ("sgu_b", 8), ("conv_w", 24), ("a_log", 8), ("dt_bias", 8), ("gdn_norm_g", 8))
_SMALL_PAD = sum(n for _, n in _SMALL)
_DMOD_ROWS = 24


def _pack_rows(parts):
    rows = []
    for (name, n), a in zip(_SMALL, parts):
        flat = a.reshape(-1).astype(F32)
        rows.append(jnp.pad(flat, (0, n * D - flat.shape[0])).reshape(n, D))
    return jnp.concatenate(rows, axis=0)


def _unpack_rows(buf, shapes):
    out, r0 = {}, 0
    for name, n in _SMALL:
        size = math.prod(shapes[name])
        out[name] = buf[r0:r0 + n].reshape(-1)[:size].reshape(shapes[name])
        r0 += n
    return out


_COMBINED_ROWS = 2 * _DMOD_ROWS + _SMALL_PAD
_GATHER_ROWS = -(-_COMBINED_ROWS // 16) * 16


def _pair_combine(own, sib, place):
    rows = own.shape[0]

    def body(p_ref, a_ref, b_ref, o_ref):
        first = lax.axis_index("c") == 0
        a, b = a_ref[0:_DMOD_ROWS, :], b_ref[0:_DMOD_ROWS, :]
        o_ref[0:_DMOD_ROWS, :] = jnp.where(first, a, b)
        o_ref[_DMOD_ROWS:2 * _DMOD_ROWS, :] = jnp.where(first, b, a)
        o_ref[2 * _DMOD_ROWS:_COMBINED_ROWS, :] = a_ref[_DMOD_ROWS:, :] + b_ref[_DMOD_ROWS:, :]
        o_ref[_COMBINED_ROWS:, :] = jnp.zeros((_GATHER_ROWS - _COMBINED_ROWS, D), F32)

    return pl.pallas_call(
        body, name="small_pair_combine", out_shape=jax.ShapeDtypeStruct((N_CHIPS, _GATHER_ROWS, D), F32),
        grid_spec=pltpu.PrefetchScalarGridSpec(
            num_scalar_prefetch=1, grid=(1,),
            in_specs=[pl.BlockSpec((rows, D), lambda i, pr: (0, 0)), pl.BlockSpec((rows, D), lambda i, pr: (0, 0))],
            out_specs=pl.BlockSpec((None, _GATHER_ROWS, D), lambda i, pr: (pr[0], 0, 0))),
        compiler_params=_cparams(("arbitrary",)),
    )(place, own, sib)


def _small_finalize(gathered, w, m, v):
    def body(g_ref, w_ref, m_ref, v_ref, go_ref, d_ref, nm_ref, nv_ref):
        lo, hi = 2 * _DMOD_ROWS, _COMBINED_ROWS
        sm = g_ref[0, lo:hi, :] + g_ref[1, lo:hi, :]
        sm = sm + g_ref[2, lo:hi, :]
        sm = sm + g_ref[3, lo:hi, :]
        bsum = jnp.zeros((_DMOD_ROWS, D), F32)
        for j in range(N_CHIPS):
            bsum = bsum + g_ref[j, 0:_DMOD_ROWS, :]
            bsum = bsum + g_ref[j, _DMOD_ROWS:2 * _DMOD_ROWS, :]
        go_ref[0:_DMOD_ROWS, :] = bsum
        go_ref[_DMOD_ROWS:, :] = sm[_DMOD_ROWS:, :]
        d_ref[...], nm_ref[...], nv_ref[...] = _adam_math(w_ref[...], go_ref[...], m_ref[...], v_ref[...])

    return pl.pallas_call(
        body, name="small_finalize", out_shape=[jax.ShapeDtypeStruct(w.shape, F32)] * 4,
        compiler_params=pltpu.CompilerParams(vmem_limit_bytes=VMEM_LIMIT),
    )(gathered, w, m, v)


def kernel(x, c, w_ada, b_ada, norm1_g, w_in, sgu_ln_g, sgu_ln_b, sgu_w, sgu_b, conv_w, a_log, dt_bias, gdn_norm_g, w_out, norm2_g, w_ff1, w_ff2, final_g, loss_target, m_w_ada, m_b_ada, m_norm1_g, m_w_in, m_sgu_ln_g, m_sgu_ln_b, m_sgu_w, m_sgu_b, m_conv_w, m_a_log, m_dt_bias, m_gdn_norm_g, m_w_out, m_norm2_g, m_w_ff1, m_w_ff2, m_final_g, v_w_ada, v_b_ada, v_norm1_g, v_w_in, v_sgu_ln_g, v_sgu_ln_b, v_sgu_w, v_sgu_b, v_conv_w, v_a_log, v_dt_bias, v_gdn_norm_g, v_w_out, v_norm2_g, v_w_ff1, v_w_ff2, v_final_g):
    xi, yi, ci = lax.axis_index("x"), lax.axis_index("y"), lax.axis_index("c")
    chip = 2 * xi + yi
    dev = 2 * chip + ci
    t = x.shape[1]
    x0 = x.reshape(t, D)
    target = loss_target.reshape(t, D)

    c_sib = _pair_exchange(c, "c_pair")
    c_pair = jnp.where(ci == 0, jnp.concatenate([c, c_sib], 0), jnp.concatenate([c_sib, c], 0))
    c_all = _chip_allgather(c_pair, "c_chips").reshape(8, D)
    ada_cols = w_ada.shape[2]
    b_cols = lax.dynamic_slice_in_dim(b_ada, chip * ada_cols, ada_cols, axis=1)
    mod_part = _ada_forward(c_all, w_ada, b_cols)
    conv_cols = conv_w.shape[2]
    packed = jnp.concatenate([mod_part.reshape(DEPTH * 8, ada_cols), conv_w.reshape(DEPTH, 4 * conv_cols)], axis=0)
    packed = _chip_allgather(packed, "mod_chips")
    mod_all = packed[:, :DEPTH * 8].reshape(N_CHIPS, DEPTH, 8, ada_cols)
    mod_mine = lax.dynamic_index_in_dim(mod_all, dev, axis=2, keepdims=False)
    mod = mod_mine.transpose(1, 0, 2).reshape(DEPTH, 6, D)
    modv = jnp.concatenate([mod, norm1_g[:, None, :], norm2_g[:, None, :]], axis=1)
    conv_full = packed[:, DEPTH * 8:].reshape(N_CHIPS, DEPTH, 4, conv_cols).transpose(1, 2, 0, 3).reshape(DEPTH, 4, 1536)

    place = jnp.stack([chip, ci]).astype(jnp.int32)
    wbufs = [[_cast_into_slot(w, l, place) for w in (w_in, w_out, w_ff1, w_ff2)] for l in range(DEPTH)]
    wbufs[0][:1] = _weights_allgather(wbufs[0][:1], 0)

    def full_w_in(g):
        return jnp.pad(g.transpose(1, 0, 2).reshape(D, IN_W), ((0, 0), (0, NW - IN_W)))

    lnp = jnp.pad(jnp.stack([sgu_ln_g, sgu_ln_b], axis=1), ((0, 0), (0, 6), (0, 0)))
    sgu_bt = jnp.pad(sgu_b.transpose(0, 2, 1), ((0, 0), (0, 0), (0, HD - HEADS)))
    cw = jnp.pad(conv_full, ((0, 0), (0, 4), (0, 0)))
    lane_pad = lambda a: jnp.pad(a, ((0, 0), (4, HD - 8)))
    gv = jnp.pad(jnp.stack([lane_pad(a_log), lane_pad(dt_bias), gdn_norm_g], axis=1), ((0, 0), (0, 5), (0, 0)))

    acts = []
    xl = x0
    for l in range(DEPTH):
        win = full_w_in(wbufs[l][0])
        p, h1 = _fwd_in(xl, modv, win, l)
        nxt = wbufs[l][1:] + (wbufs[l + 1][:1] if l + 1 < DEPTH else [])
        mix, *rest = _mixer_forward(p, lnp, sgu_w, sgu_bt, cw, gv, l, gather=nxt)
        saved = rest[:6]
        wbufs[l][1:] = rest[6:9]
        if l + 1 < DEPTH:
            wbufs[l + 1][:1] = rest[9:]
        g_in, g_out, g_ff1, g_ff2 = wbufs[l]
        x1, r, h2 = _fwd_out_ff1(xl, mix, modv, g_out, g_ff1, l)
        x2 = _fwd_ff2(x1, r, modv, g_ff2, l)
        acts.append((xl, p, mix, saved, x1, r, h1, h2, win))
        xl = x2

    dx, head_stats = _loss_head(xl, target, final_g)
    loss = lax.psum(jnp.sum(head_stats[1, 0:1]), ("x", "y", "c"))
    d_final_g = head_stats[0]
    names = ("in", "out", "ff1", "ff2")
    grads_buf = [None] * len(names)

    def pair_sums(partials, from_sib, kinds, lay):
        return [(lay, n, _pair_sum(g, ga, place, f"pair_sum_{n}_{lay}")) for g, ga, n in zip(partials, from_sib, kinds)]

    def reduce_into_buffers(items, recv):
        for (lay, n, pair), rc in zip(items, recv):
            i = names.index(n)
            grads_buf[i] = _chip_sum(pair, rc, grads_buf[i], lay, place, f"chip_sum_{n}_{lay}")

    pending = []

    dmod, small = [None] * DEPTH, [None] * DEPTH
    for l in reversed(range(DEPTH)):
        xl, p, mix, saved, x1, r, h1, h2, win = acts[l]
        g_in, g_out, g_ff1, g_ff2 = wbufs[l]
        df = _bwd_ff2(dx, r, modv, g_ff2, l)
        gw_ff2, dg2 = _grad_weight(r, dx, modv, l, "ff2", g_ff2)
        gw_ff1, _ = _grad_weight(h2, df, modv, l, "ff1")
        dx1, st2, dmix, *sib_ff = _bwd_norm(df, g_ff1, x1, dx, modv, l, "ff1", send=[gw_ff1, gw_ff2], w_out=g_out)
        gw_out, dg1 = _grad_weight(mix, dx1, modv, l, "out", g_out)
        gw_out = gw_out.reshape(N_CHIPS, D // N_CHIPS, D)
        sib_out = _grads_pair_send([gw_out], f"out_{l}")
        pending = pending + pair_sums([gw_out, gw_ff1, gw_ff2], list(sib_out) + sib_ff, names[1:], l)
        dp, dlnp, dsguw, dsgub, dcw, dgv, *recv = _mixer_backward(p, dmix, saved, lnp, sgu_w, sgu_bt, cw, gv, l,
                                                                  exchange=[item[2] for item in pending])
        reduce_into_buffers(pending, recv)
        gw_in, _ = _grad_weight(h1, dp, modv, l, "in")
        gw_in_c = gw_in[:, :IN_W].reshape(D, N_CHIPS, IN_W // N_CHIPS).transpose(1, 0, 2)
        if l > 0:
            dx, st1, sib_in = _bwd_norm(dp, win, xl, dx1, modv, l, "in", send=[gw_in_c])
            pending = pair_sums([gw_in_c], [sib_in], names[:1], l)
        else:
            pending = pair_sums([gw_in_c], _grads_pair_send([gw_in_c], "in_0"), names[:1], l)
            dx, st1, *rest = _bwd_norm(dp, win, xl, dx1, modv, l, "in", share=grads_buf[1:],
                                       exchange=[item[2] for item in pending])
            grads_buf[1:] = rest[:3]
            reduce_into_buffers(pending, rest[3:])
        dmod[l] = jnp.stack([st1[0], st1[1], dg1[0], st2[0], st2[1], dg2[0]], axis=0)
        small[l] = dict(norm1_g=st1[2], norm2_g=st2[2], sgu_ln_g=dlnp[0], sgu_ln_b=dlnp[1], sgu_w=dsguw,
                        sgu_b=dsgub[:, :HEADS].T, conv_w=dcw[:4], a_log=dgv[0, 4:8], dt_bias=dgv[1, 4:8],
                        gdn_norm_g=dgv[2])
    grad_x = dx.reshape(1, t, D)

    stack = lambda k: jnp.stack([small[l][k] for l in range(DEPTH)], axis=0)
    small_grads = [jnp.zeros((DEPTH, 6 * D), F32), stack("norm1_g"), stack("norm2_g"), d_final_g, stack("sgu_ln_g"),
                   stack("sgu_ln_b"), stack("sgu_w"), stack("sgu_b"), stack("conv_w"), stack("a_log"),
                   stack("dt_bias"), stack("gdn_norm_g")]
    own = jnp.concatenate([jnp.stack(dmod, axis=0).reshape(_DMOD_ROWS, D), _pack_rows(small_grads)], axis=0)
    sib = _pair_exchange(own, "small_pair")
    gathered, = _weights_allgather([_pair_combine(own, sib, place)], "small")
    small_shapes = dict(b_ada=b_ada.shape, norm1_g=norm1_g.shape, norm2_g=norm2_g.shape, final_g=final_g.shape,
                        sgu_ln_g=sgu_ln_g.shape, sgu_ln_b=sgu_ln_b.shape, sgu_w=sgu_w.shape, sgu_b=sgu_b.shape,
                        conv_w=(DEPTH, 4, 1536), a_log=a_log.shape, dt_bias=dt_bias.shape,
                        gdn_norm_g=gdn_norm_g.shape)

    def full_conv(a):
        return lax.dynamic_update_slice_in_dim(jnp.zeros((DEPTH, 4, 1536), F32), a, chip * conv_cols, axis=2)

    def pack_state(b_, n1, n2, fg, lg, lb, sw, sb, cv, al, db, gn):
        return _pack_rows([b_, n1, n2, fg, lg, lb, sw, sb, full_conv(cv), al, db, gn])

    w_small = pack_state(b_ada, norm1_g, norm2_g, final_g, sgu_ln_g, sgu_ln_b, sgu_w, sgu_b, conv_w, a_log, dt_bias,
                         gdn_norm_g)
    m_small = pack_state(m_b_ada, m_norm1_g, m_norm2_g, m_final_g, m_sgu_ln_g, m_sgu_ln_b, m_sgu_w, m_sgu_b, m_conv_w,
                         m_a_log, m_dt_bias, m_gdn_norm_g)
    v_small = pack_state(v_b_ada, v_norm1_g, v_norm2_g, v_final_g, v_sgu_ln_g, v_sgu_ln_b, v_sgu_w, v_sgu_b, v_conv_w,
                         v_a_log, v_dt_bias, v_gdn_norm_g)
    small_out = _small_finalize(gathered, w_small, m_small, v_small)
    sg, sd, sm, sv = [_unpack_rows(a, small_shapes) for a in small_out]
    for dct in (sg, sd, sm, sv):
        dct["conv_w"] = lax.dynamic_slice_in_dim(dct["conv_w"], chip * conv_cols, conv_cols, axis=2)

    dmod_all = gathered[:, :2 * _DMOD_ROWS].reshape(8, DEPTH, 6 * D)
    dmod_cols = lax.dynamic_slice_in_dim(dmod_all, chip * ada_cols, ada_cols, axis=2).transpose(1, 0, 2)
    g_ada, d_ada, nm_ada, nv_ada = _ada_backward_adamw(c_all, dmod_cols, w_ada, m_w_ada, v_w_ada)

    grads_buf[:1] = _grads_pair_share(grads_buf[:1], "grads_pair_share_in")
    big = {}
    for n, g, (w, m, v) in zip(names, grads_buf, ((w_in, m_w_in, v_w_in), (w_out, m_w_out, v_w_out),
                                                  (w_ff1, m_w_ff1, v_w_ff1), (w_ff2, m_w_ff2, v_w_ff2))):
        big[n] = (g,) + tuple(_adamw(w, g, m, v, f"adamw_{n}"))

    def outs(k):
        s = (sg, sd, sm, sv)[k]
        return [(g_ada, d_ada, nm_ada, nv_ada)[k], s["b_ada"], s["norm1_g"], big["in"][k], s["sgu_ln_g"],
                s["sgu_ln_b"], s["sgu_w"], s["sgu_b"], s["conv_w"], s["a_log"], s["dt_bias"], s["gdn_norm_g"],
                big["out"][k], s["norm2_g"], big["ff1"][k], big["ff2"][k], s["final_g"]]

    return (loss, grad_x, *outs(0), *outs(1), *outs(2), *outs(3))
```
